```python
import math
import jax, jax.numpy as jnp
from jax import lax
import numpy as np

D_MODEL = 2048
BATCH = 2
SEQ = 4096
DEPTH = 1

ATT_HEADS = 16
ATT_KV_HEADS = 4
HEAD_DIM = 128
REP = ATT_HEADS // ATT_KV_HEADS
IDX_HEADS = 16
IDX_DIM = 64
TOPK_MAX = 256
Q_BLOCK = 128
N_BUCKETS = 32
MAX_DISTANCE = 128
SSM_EXPAND = 2
SSM_INNER = SSM_EXPAND * D_MODEL
SSM_HEAD_DIM = 64
SSM_HEADS = SSM_INNER // SSM_HEAD_DIM
SSM_GROUPS = 8
SSM_STATE = 128
CONV_WIDTH = 4
CHUNK = 128
MLP_HIDDEN = 4 * D_MODEL
EPS = 1e-6

ATT_Q = ATT_HEADS * HEAD_DIM
ATT_KV = ATT_KV_HEADS * HEAD_DIM
IDX_Q = IDX_HEADS * IDX_DIM
SSM_BC = SSM_GROUPS * SSM_STATE
CONV_DIM = SSM_INNER + 2 * SSM_BC
SPLITS = (D_MODEL, D_MODEL, ATT_Q, ATT_KV, ATT_KV, IDX_Q, IDX_DIM, IDX_HEADS, SSM_INNER, CONV_DIM, SSM_HEADS)
IN_DIM = sum(SPLITS)
SPLIT_POINTS = tuple(sum(SPLITS[:i + 1]) for i in range(len(SPLITS) - 1))

kernel_name = "hybrid_dsa_ssd_gated_block"


def rms_norm(x, g, eps=EPS):
    xf = x.astype(jnp.float32)
    y = xf * lax.rsqrt(jnp.mean(xf * xf, axis=-1, keepdims=True) + eps)
    return (y * g.astype(jnp.float32)).astype(x.dtype)


def t5_bucket(dist):
    n = jnp.maximum(dist, 0)
    max_exact = N_BUCKETS // 2
    nf = jnp.maximum(n, 1).astype(jnp.float32)
    large = max_exact + (jnp.log(nf / max_exact) / math.log(MAX_DISTANCE / max_exact)
                         * (N_BUCKETS - max_exact)).astype(jnp.int32)
    large = jnp.minimum(large, N_BUCKETS - 1)
    return jnp.where(n < max_exact, n, large)


def dsa_attention(q, k, v, q_idx, k_idx, w_idx, rel_bias):
    b, s = q.shape[0], q.shape[1]
    topk = min(TOPK_MAX, s // 4)
    nb = s // Q_BLOCK
    key_pos = jnp.arange(s)
    gather = jax.vmap(lambda t, idx: t[idx])

    def to_blocks(t):
        return t.reshape((b, nb, Q_BLOCK) + t.shape[2:]).swapaxes(0, 1)

    def one_block(args):
        qb, qib, wb, tpos = args
        rel = jax.nn.relu(jnp.einsum('bthd,bsd->bths', qib, k_idx))
        score = jnp.einsum('bths,bth->bts', rel, wb).astype(jnp.float32)
        admissible = key_pos[None, None, :] <= tpos[None, :, None]
        score = jnp.where(admissible, score, -jnp.inf)
        _, sel = lax.top_k(score, topk)
        valid = sel <= tpos[None, :, None]
        kg = gather(k, sel)
        vg = gather(v, sel)
        qg = qb.reshape(b, Q_BLOCK, ATT_KV_HEADS, REP, HEAD_DIM)
        logits = jnp.einsum('btgrd,btkgd->btgrk', qg, kg).astype(jnp.float32)
        bias = rel_bias[t5_bucket(tpos[None, :, None] - sel)]
        bias = bias.reshape(b, Q_BLOCK, topk, ATT_KV_HEADS, REP).transpose(0, 1, 3, 4, 2)
        logits = jnp.where(valid[:, :, None, None, :], logits + bias.astype(jnp.float32), -jnp.inf)
        p = jax.nn.softmax(logits, axis=-1).astype(v.dtype)
        o = jnp.einsum('btgrk,btkgd->btgrd', p, vg)
        return o.reshape(b, Q_BLOCK, ATT_Q)

    tpos_blocks = jnp.arange(s).reshape(nb, Q_BLOCK)
    out = lax.map(one_block, (to_blocks(q), to_blocks(q_idx), to_blocks(w_idx), tpos_blocks))
    return out.swapaxes(0, 1).reshape(b, s, ATT_Q)


def causal_depthwise_conv(x, w, bias):
    y = lax.conv_general_dilated(x, w[:, None, :].astype(x.dtype), window_strides=(1,),
                                 padding=[(CONV_WIDTH - 1, 0)],
                                 dimension_numbers=('NWC', 'WIO', 'NWC'),
                                 feature_group_count=x.shape[-1])
    return y + bias.astype(x.dtype)


def ssd_chunked(x, dt, A, Bm, Cm):
    in_dtype = x.dtype
    f32 = jnp.float32
    b, s, h, p = x.shape
    g, n = Bm.shape[2], Bm.shape[3]
    e = h // g
    nc = s // CHUNK
    xd = (x.astype(f32) * dt[..., None]).reshape(b, nc, CHUNK, g, e, p)
    a = (dt * A).reshape(b, nc, CHUNK, g, e)
    Bc = Bm.astype(f32).reshape(b, nc, CHUNK, g, n)
    Cc = Cm.astype(f32).reshape(b, nc, CHUNK, g, n)
    a_cum = jnp.cumsum(a, axis=2)
    causal = jnp.tril(jnp.ones((CHUNK, CHUNK), dtype=bool))
    seg = a_cum[:, :, :, None] - a_cum[:, :, None, :]
    decay = jnp.exp(jnp.where(causal[None, None, :, :, None, None], seg, -jnp.inf))
    cb = jnp.einsum('bclgn,bcsgn->bclsg', Cc, Bc)
    y_diag = jnp.einsum('bclsg,bclsge,bcsgep->bclgep', cb, decay, xd)
    decay_to_end = jnp.exp(a_cum[:, :, -1:] - a_cum)
    chunk_states = jnp.einsum('bclgn,bclge,bclgep->bcgepn', Bc, decay_to_end, xd)
    chunk_decay = jnp.exp(a_cum[:, :, -1])

    def step(state, inp):
        st, dec = inp
        return state * dec[..., None, None] + st, state

    init = jnp.zeros((b, g, e, p, n), f32)
    _, entering = lax.scan(step, init, (chunk_states.swapaxes(0, 1), chunk_decay.swapaxes(0, 1)))
    entering = entering.swapaxes(0, 1)
    y_off = jnp.einsum('bclgn,bcgepn,bclge->bclgep', Cc, entering, jnp.exp(a_cum))
    return (y_diag + y_off).reshape(b, s, h, p).astype(in_dtype)


def setup_inputs(seed: int = 0) -> dict:
    key = jax.random.key(seed)
    ks = jax.random.split(key, 24)
    f32 = jnp.float32
    L = DEPTH

    def nrm(k, shape, scale):
        return jax.random.normal(k, shape, f32) * scale

    dt0 = jnp.exp(jax.random.uniform(ks[5], (L, SSM_HEADS), f32) * (math.log(0.1) - math.log(0.001)) + math.log(0.001))
    return {
        "x": nrm(ks[0], (BATCH, SEQ, D_MODEL), 1.0),
        "norm1_g": 1.0 + nrm(ks[1], (L, D_MODEL), 0.02),
        "w_in": nrm(ks[2], (L, D_MODEL, IN_DIM), D_MODEL ** -0.5),
        "conv_w": nrm(ks[3], (L, CONV_WIDTH, CONV_DIM), CONV_WIDTH ** -0.5),
        "conv_b": nrm(ks[4], (L, CONV_DIM), 0.02),
        "dt_bias": dt0 + jnp.log(-jnp.expm1(-dt0)),
        "a_log": jnp.log(jax.random.uniform(ks[6], (L, SSM_HEADS), f32, 1.0, 16.0)),
        "d_skip": 1.0 + nrm(ks[7], (L, SSM_HEADS), 0.02),
        "ssm_norm_g": 1.0 + nrm(ks[8], (L, SSM_INNER), 0.02),
        "q_norm_g": 1.0 + nrm(ks[9], (L, HEAD_DIM), 0.02),
        "k_norm_g": 1.0 + nrm(ks[10], (L, HEAD_DIM), 0.02),
        "rel_bias": nrm(ks[11], (N_BUCKETS, ATT_HEADS), 0.5),
        "w_att_branch": nrm(ks[12], (L, ATT_Q, D_MODEL), ATT_Q ** -0.5),
        "w_ssm_branch": nrm(ks[13], (L, SSM_INNER, D_MODEL), SSM_INNER ** -0.5),
        "w_out": nrm(ks[14], (L, D_MODEL, D_MODEL), D_MODEL ** -0.5),
        "norm2_g": 1.0 + nrm(ks[15], (L, D_MODEL), 0.02),
        "w_up": nrm(ks[16], (L, D_MODEL, MLP_HIDDEN), D_MODEL ** -0.5),
        "w_down": nrm(ks[17], (L, MLP_HIDDEN, D_MODEL), MLP_HIDDEN ** -0.5),
    }


def reference(x, norm1_g, w_in, conv_w, conv_b, dt_bias, a_log, d_skip, ssm_norm_g, q_norm_g,
              k_norm_g, rel_bias, w_att_branch, w_ssm_branch, w_out, norm2_g, w_up, w_down):
    b, s, _ = x.shape
    f32 = jnp.float32
    for l in range(DEPTH):
        h = rms_norm(x, norm1_g[l])
        proj = h @ w_in[l]
        (gate_att, gate_ssm, q, k, v, q_idx, k_idx, w_idx, z, xbc, dt) = jnp.split(proj, SPLIT_POINTS, axis=-1)

        q = rms_norm(q.reshape(b, s, ATT_HEADS, HEAD_DIM), q_norm_g[l]) * (HEAD_DIM ** -0.5)
        k = rms_norm(k.reshape(b, s, ATT_KV_HEADS, HEAD_DIM), k_norm_g[l])
        v = v.reshape(b, s, ATT_KV_HEADS, HEAD_DIM)
        q_idx = q_idx.reshape(b, s, IDX_HEADS, IDX_DIM) * (IDX_DIM ** -0.5)
        w_idx = w_idx * (IDX_HEADS ** -0.5)
        att = dsa_attention(q, k, v, q_idx, k_idx, w_idx, rel_bias)

        xbc = jax.nn.silu(causal_depthwise_conv(xbc, conv_w[l], conv_b[l]))
        xs, Bm, Cm = jnp.split(xbc, (SSM_INNER, SSM_INNER + SSM_BC), axis=-1)
        xs = xs.reshape(b, s, SSM_HEADS, SSM_HEAD_DIM)
        dt_act = jax.nn.softplus(dt.astype(f32) + dt_bias[l].astype(f32))
        A = -jnp.exp(a_log[l].astype(f32))
        y = ssd_chunked(xs, dt_act, A, Bm.reshape(b, s, SSM_GROUPS, SSM_STATE),
                        Cm.reshape(b, s, SSM_GROUPS, SSM_STATE))
        y = y + d_skip[l][:, None].astype(y.dtype) * xs
        y = y.reshape(b, s, SSM_INNER) * jax.nn.silu(z)
        y = rms_norm(y.reshape(b, s, SSM_GROUPS, SSM_INNER // SSM_GROUPS),
                     ssm_norm_g[l].reshape(SSM_GROUPS, SSM_INNER // SSM_GROUPS)).reshape(b, s, SSM_INNER)

        merged = (jax.nn.sigmoid(gate_att) * (att @ w_att_branch[l])
                  + jax.nn.sigmoid(gate_ssm) * (y @ w_ssm_branch[l]))
        x = x + merged @ w_out[l]

        h2 = rms_norm(x, norm2_g[l])
        x = x + jnp.square(jax.nn.relu(h2 @ w_up[l])) @ w_down[l]
    return x
```

```python
import functools
import math

import numpy as np
import jax
import jax.numpy as jnp
from jax import lax
from jax.experimental import pallas as pl
from jax.experimental.pallas import tpu as pltpu

F32 = jnp.float32
BF16 = jnp.bfloat16
I32 = jnp.int32

D_MODEL = 2048
ATT_HEADS = 16
ATT_KV_HEADS = 4
HEAD_DIM = 128
REP = ATT_HEADS // ATT_KV_HEADS
IDX_HEADS = 16
IDX_DIM = 64
TOPK_MAX = 256
N_BUCKETS = 32
MAX_DISTANCE = 128
SSM_INNER = 2 * D_MODEL
SSM_HEAD_DIM = 64
SSM_HEADS = SSM_INNER // SSM_HEAD_DIM
SSM_GROUPS = 8
SSM_STATE = 128
CONV_WIDTH = 4
CHUNK = 128
MLP_HIDDEN = 4 * D_MODEL
EPS = 1e-6

ATT_Q = ATT_HEADS * HEAD_DIM
ATT_KV = ATT_KV_HEADS * HEAD_DIM
IDX_Q = IDX_HEADS * IDX_DIM
SSM_BC = SSM_GROUPS * SSM_STATE
CONV_DIM = SSM_INNER + 2 * SSM_BC
SPLITS = (D_MODEL, D_MODEL, ATT_Q, ATT_KV, ATT_KV, IDX_Q, IDX_DIM, IDX_HEADS, SSM_INNER, CONV_DIM, SSM_HEADS)
_OFFS = tuple(int(v) for v in np.cumsum((0,) + SPLITS))
(O_GA, O_GS, O_Q, O_K, O_V, O_QI, O_KI, O_WI, O_Z, O_XBC, O_DT, _O_END) = _OFFS

M_GA = 0
M_GS = M_GA + D_MODEL
M_Q = M_GS + D_MODEL
M_K = M_Q + ATT_Q
M_V = M_K + ATT_KV
M_QI = M_V + ATT_KV
M_Z = M_QI + IDX_Q
M_XBC = M_Z + SSM_INNER
M_END = M_XBC + CONV_DIM
S_KI = 0
S_WI = S_KI + IDX_DIM
S_DT = S_WI + IDX_HEADS
S_END = 256

HEADS_PER_GROUP = SSM_HEADS // SSM_GROUPS
GROUP_W = HEADS_PER_GROUP * SSM_HEAD_DIM

LANES = 128
VMEM_LIMIT = 48 * 1024 * 1024

NEG_INF = float("-inf")
INT_MIN = -(2 ** 31)
KEY_NEG_INF = int(np.int32(np.uint32(0xFF800000) ^ np.uint32(0x7FFFFFFF)))


def _dot(a, b):
    return jnp.dot(a, b, preferred_element_type=F32)


def _dot_nt(a, b):
    return lax.dot_general(a, b, (((1,), (1,)), ((), ())), preferred_element_type=F32)


def _split3(x):
    hi = x.astype(BF16)
    r = x - hi.astype(F32)
    mid = r.astype(BF16)
    lo = (r - mid.astype(F32)).astype(BF16)
    return hi, mid, lo


def _sigmoid(x):
    return 1.0 / (1.0 + jnp.exp(-x))


def _params(sem):
    return pltpu.CompilerParams(dimension_semantics=sem, vmem_limit_bytes=VMEM_LIMIT)


def _norm_matmul_kernel(x_ref, g_ref, w_ref, o_ref, xn_ref):
    @pl.when(pl.program_id(1) == 0)
    def _():
        x = x_ref[...]
        ms = jnp.mean(x * x, axis=-1, keepdims=True)
        xn_ref[...] = (x * lax.rsqrt(ms + EPS) * g_ref[...]).astype(BF16)

    o_ref[...] = _dot(xn_ref[...], w_ref[...]).astype(o_ref.dtype)


def _norm_matmul(x, g, w, out_dtype, tm, tn):
    m, d = x.shape
    n = w.shape[1]
    return pl.pallas_call(
        _norm_matmul_kernel,
        grid=(m // tm, n // tn),
        in_specs=[pl.BlockSpec((tm, d), lambda i, j: (i, 0)),
                  pl.BlockSpec((1, d), lambda i, j: (0, 0)),
                  pl.BlockSpec((d, tn), lambda i, j: (0, j))],
        out_specs=pl.BlockSpec((tm, tn), lambda i, j: (i, j)),
        out_shape=jax.ShapeDtypeStruct((m, n), out_dtype),
        scratch_shapes=[pltpu.VMEM((tm, d), BF16)],
        compiler_params=_params(("parallel", "arbitrary")),
        name="norm_in_proj",
    )(x, g, w)


def _qk_prep_kernel(q_ref, k_ref, sm_ref, qg_ref, kg_ref, qn_ref, kn_ref, ki_ref):
    def norm_heads(x_ref, g, scale, o_ref, nheads):
        for h in range(nheads):
            x = x_ref[:, h * HEAD_DIM:(h + 1) * HEAD_DIM].astype(F32)
            ms = jnp.mean(x * x, axis=-1, keepdims=True)
            y = x * lax.rsqrt(ms + EPS) * g
            if scale is not None:
                y = y * scale
            o_ref[:, h * HEAD_DIM:(h + 1) * HEAD_DIM] = y.astype(BF16)

    norm_heads(q_ref, qg_ref[...], HEAD_DIM ** -0.5, qn_ref, ATT_HEADS)
    norm_heads(k_ref, kg_ref[...], None, kn_ref, ATT_KV_HEADS)
    ki_ref[...] = sm_ref[:, S_KI:S_KI + IDX_DIM].astype(BF16)


def _qk_prep(main, small, qg, kg, tq):
    t = main.shape[0]
    return pl.pallas_call(
        _qk_prep_kernel,
        grid=(t // tq,),
        in_specs=[pl.BlockSpec((tq, ATT_Q), lambda i: (i, M_Q // ATT_Q)),
                  pl.BlockSpec((tq, ATT_KV), lambda i: (i, M_K // ATT_KV)),
                  pl.BlockSpec((tq, S_END), lambda i: (i, 0)),
                  pl.BlockSpec((1, HEAD_DIM), lambda i: (0, 0)),
                  pl.BlockSpec((1, HEAD_DIM), lambda i: (0, 0))],
        out_specs=[pl.BlockSpec((tq, ATT_Q), lambda i: (i, 0)),
                   pl.BlockSpec((tq, ATT_KV), lambda i: (i, 0)),
                   pl.BlockSpec((tq, IDX_DIM), lambda i: (i, 0))],
        out_shape=[jax.ShapeDtypeStruct((t, ATT_Q), BF16),
                   jax.ShapeDtypeStruct((t, ATT_KV), BF16),
                   jax.ShapeDtypeStruct((t, IDX_DIM), BF16)],
        compiler_params=_params(("parallel",)),
        name="qk_prep",
    )(main, main, small, qg, kg)


DSA_TQ = 256
DSA_TK = 256


def _t5_bucket_np(dist):
    n = np.maximum(dist, 0)
    max_exact = N_BUCKETS // 2
    nf = np.maximum(n, 1).astype(np.float32)
    ratio = (np.log(nf / np.float32(max_exact)) / np.float32(math.log(MAX_DISTANCE / max_exact))
             * np.float32(N_BUCKETS - max_exact))
    large = max_exact + ratio.astype(np.int32)
    large = np.minimum(large, N_BUCKETS - 1)
    return np.where(n < max_exact, n, large).astype(np.int32)


def _bias_bucket_tiles(tq, tk):
    r = np.arange(tq)[:, None]
    c = np.arange(tk)[None, :]
    d0 = _t5_bucket_np(r - c)
    d1 = _t5_bucket_np(tk + r - c)
    assert np.all(_t5_bucket_np(np.arange(tk + 1, 8 * tk)) == N_BUCKETS - 1)
    return np.stack([d0, d1]).astype(np.int32)


def _dsa_kernel(relb_ref, bidx_ref, q_ref, qi_ref, sm_ref, k_ref, v_ref, ki_ref, o_ref,
                key_ref, madd_ref, bias_ref, *, tq, tk, nkc, topk):
    b = pl.program_id(0)
    i = pl.program_id(1)
    g = pl.program_id(2)
    neg_slot = nkc
    nsub = tk // LANES

    @pl.when((b == 0) & (i == 0) & (g == 0))
    def _init():
        madd_ref[neg_slot] = jnp.full((tq, tk), NEG_INF, F32)
        for t in range(2):
            bt = bidx_ref[t]

            def head_body(h, carry):
                far = relb_ref[N_BUCKETS - 1, h]

                def bucket_body(bk, acc):
                    return jnp.where(bt == bk, relb_ref[bk, h] - far, acc)

                bias_ref[t, h] = lax.fori_loop(0, N_BUCKETS, bucket_body, jnp.zeros((tq, tk), F32))
                return carry

            lax.fori_loop(0, ATT_HEADS, head_body, 0)

    @pl.when(g == 0)
    def _select():
        nj = i + 1
        w = sm_ref[:, S_WI:S_WI + IDX_HEADS] * (IDX_HEADS ** -0.5 * IDX_DIM ** -0.5)
        qpos = i * tq + lax.broadcasted_iota(I32, (tq, tk), 0)

        def score_chunk(j, carry):
            kc = ki_ref[pl.ds(pl.multiple_of(j * tk, tk), tk), :]
            acc = jnp.zeros((tq, tk), F32)
            for h in range(IDX_HEADS):
                z = _dot_nt(qi_ref[:, h * IDX_DIM:(h + 1) * IDX_DIM], kc)
                acc = acc + jnp.maximum(z, 0.0) * w[:, h:h + 1]
            kpos = j * tk + lax.broadcasted_iota(I32, (tq, tk), 1)
            acc = jnp.where(kpos <= qpos, acc, NEG_INF)
            bits = pltpu.bitcast(acc, I32)
            key_ref[j] = bits ^ ((bits >> 31) & 0x7FFFFFFF)
            return carry

        lax.fori_loop(0, nj, score_chunk, 0)

        def bit_iter(it, prefix):
            cand = prefix + lax.shift_left(jnp.int32(1), 31 - it)

            def cnt_chunk(j, cnt):
                kk = key_ref[j]
                for m in range(nsub):
                    cnt = cnt + jnp.where(kk[:, m * LANES:(m + 1) * LANES] >= cand, 1.0, 0.0)
                return cnt

            cnt = lax.fori_loop(0, nj, cnt_chunk, jnp.zeros((tq, LANES), F32))
            tot = jnp.sum(cnt, axis=-1, keepdims=True)
            return jnp.where(tot >= float(topk), cand, prefix)

        thr = lax.fori_loop(0, 32, bit_iter, jnp.full((tq, LANES), INT_MIN, I32))
        thr = jnp.maximum(thr, KEY_NEG_INF + 1)
        thr_w = jnp.concatenate([thr] * nsub, axis=1)

        def madd_chunk(j, carry):
            madd_ref[j] = jnp.where(key_ref[j] >= thr_w, 0.0, NEG_INF)
            return carry

        lax.fori_loop(0, nj, madd_chunk, 0)

    q = q_ref[...]
    q_stack = jnp.concatenate([q[:, r * HEAD_DIM:(r + 1) * HEAD_DIM] for r in range(REP)], axis=0)
    c0 = jnp.maximum(i - 1, 0)

    def attn_step(k_c, v_c, madd_c, bias_c, m, l, acc):
        s = _dot_nt(q_stack, k_c)
        parts = []
        for r in range(REP):
            add = madd_c if bias_c is None else madd_c + bias_c[r]
            parts.append(s[r * tq:(r + 1) * tq] + add)
        s = jnp.concatenate(parts, axis=0)
        m_new = jnp.maximum(m, jnp.max(s, axis=-1, keepdims=True))
        m_safe = jnp.where(m_new == NEG_INF, 0.0, m_new)
        alpha = jnp.exp(m - m_safe)
        p = jnp.exp(s - m_safe)
        l = alpha * l + jnp.sum(p, axis=-1, keepdims=True)
        acc = alpha * acc + _dot(p.astype(BF16), v_c)
        return m_new, l, acc

    m0 = jnp.full((REP * tq, 1), NEG_INF, F32)
    l0 = jnp.zeros((REP * tq, 1), F32)
    a0 = jnp.zeros((REP * tq, HEAD_DIM), F32)

    first = i == 0
    t_a = jnp.where(first, 0, 1)
    idx_b = jnp.where(first, neg_slot, c0 + 1)
    rows = pl.ds(pl.multiple_of(c0 * tk, tk), 2 * tk)
    madd_near = jnp.concatenate([madd_ref[c0], madd_ref[idx_b]], axis=1)
    bias_near = [jnp.concatenate([bias_ref[t_a, g * REP + r], bias_ref[0, g * REP + r]], axis=1)
                 for r in range(REP)]
    m1, l1, a1 = attn_step(k_ref[rows, :], v_ref[rows, :], madd_near, bias_near, m0, l0, a0)

    def far_body(jb, carry):
        m, l, acc = carry
        ca = 2 * jb
        cb = jnp.where(ca + 1 >= c0, neg_slot, ca + 1)
        rows_f = pl.ds(pl.multiple_of(ca * tk, 2 * tk), 2 * tk)
        madd_f = jnp.concatenate([madd_ref[ca], madd_ref[cb]], axis=1)
        return attn_step(k_ref[rows_f, :], v_ref[rows_f, :], madd_f, None, m, l, acc)

    m2, l2, a2 = lax.fori_loop(0, (c0 + 1) // 2, far_body, (m1, l1, a1))
    out = a2 / l2
    for r in range(REP):
        o_ref[:, r * HEAD_DIM:(r + 1) * HEAD_DIM] = out[r * tq:(r + 1) * tq].astype(o_ref.dtype)


def _dsa(qn, main, small, kn, ki, rel_bias, batch, seq):
    tq, tk = DSA_TQ, DSA_TK
    nq = seq // tq
    nkc = seq // tk
    topk = min(TOPK_MAX, seq // 4)
    bidx = jnp.asarray(_bias_bucket_tiles(tq, tk))
    kern = functools.partial(_dsa_kernel, tq=tq, tk=tk, nkc=nkc, topk=topk)
    gw = REP * HEAD_DIM
    return pl.pallas_call(
        kern,
        grid=(batch, nq, ATT_KV_HEADS),
        in_specs=[pl.BlockSpec(memory_space=pltpu.SMEM),
                  pl.BlockSpec((2, tq, tk), lambda b, i, g: (0, 0, 0)),
                  pl.BlockSpec((tq, gw), lambda b, i, g: (b * nq + i, g)),
                  pl.BlockSpec((tq, IDX_Q), lambda b, i, g: (b * nq + i, M_QI // IDX_Q)),
                  pl.BlockSpec((tq, S_END), lambda b, i, g: (b * nq + i, 0)),
                  pl.BlockSpec((seq, HEAD_DIM), lambda b, i, g: (b, g)),
                  pl.BlockSpec((seq, HEAD_DIM), lambda b, i, g: (b, M_V // HEAD_DIM + g)),
                  pl.BlockSpec((seq, IDX_DIM), lambda b, i, g: (b, 0))],
        out_specs=pl.BlockSpec((tq, gw), lambda b, i, g: (b * nq + i, g)),
        out_shape=jax.ShapeDtypeStruct((batch * seq, ATT_Q), BF16),
        scratch_shapes=[pltpu.VMEM((nkc, tq, tk), I32),
                        pltpu.VMEM((nkc + 1, tq, tk), F32),
                        pltpu.VMEM((2, ATT_HEADS, tq, tk), F32)],
        compiler_params=_params(("arbitrary", "arbitrary", "arbitrary")),
        name="dsa_attention",
    )(rel_bias, bidx, qn, main, small, kn, main, ki)


E_ROWS = 3 * CHUNK + 16


def _ssd_kernel(xbc_ref, halo_ref, z_ref, sm_ref, cw_ref, cb_ref, dtb_ref, alog_ref, dsk_ref, ng_ref,
                o_ref, xs_ref, bm_ref, cm_ref, state_ref, ypre_ref, acg_ref, actg_ref):
    c = pl.program_id(1)
    L = CHUNK

    @pl.when(c == 0)
    def _():
        state_ref[...] = jnp.zeros_like(state_ref)

    halo_on = (c > 0).astype(F32)
    cblk = GROUP_W
    for cbi in range(CONV_DIM // cblk):
        cols = slice(cbi * cblk, (cbi + 1) * cblk)
        xin = xbc_ref[:, cols].astype(F32)
        hl = halo_ref[:, cols].astype(F32) * halo_on
        ext = jnp.concatenate([hl, xin], axis=0)
        acc = cb_ref[:, cols] + cw_ref[CONV_WIDTH - 1:CONV_WIDTH, cols] * xin
        for kk in range(CONV_WIDTH - 1):
            off = 8 - (CONV_WIDTH - 1) + kk
            acc = acc + cw_ref[kk:kk + 1, cols] * ext[off:off + L]
        y = acc * _sigmoid(acc)
        if cbi < SSM_GROUPS:
            xs_ref[cbi] = y
        else:
            per = cblk // SSM_STATE
            for u in range(per):
                gi = (cbi - SSM_GROUPS) * per + u
                piece = y[:, u * SSM_STATE:(u + 1) * SSM_STATE]
                if gi < SSM_GROUPS:
                    bm_ref[gi] = piece
                else:
                    cm_ref[gi - SSM_GROUPS] = piece

    dt_in = sm_ref[:, S_DT:S_DT + SSM_HEADS] + dtb_ref[...]
    dt_act = jnp.maximum(dt_in, 0.0) + jnp.log1p(jnp.exp(-jnp.abs(dt_in)))
    a = dt_act * (-jnp.exp(alog_ref[...]))
    ri = lax.broadcasted_iota(I32, (L, L), 0)
    ci = lax.broadcasted_iota(I32, (L, L), 1)
    tril = ri >= ci
    tri_b = jnp.where(tril, 1.0, 0.0).astype(BF16)
    a_cum = sum(_dot(tri_b, p) for p in _split3(a))
    eye_b = jnp.where(lax.broadcasted_iota(I32, (SSM_HEADS, SSM_HEADS), 0)
                      == lax.broadcasted_iota(I32, (SSM_HEADS, SSM_HEADS), 1), 1.0, 0.0).astype(BF16)
    a_cum_t = sum(_dot_nt(eye_b, p) for p in _split3(a_cum))
    a_last = a_cum[L - 1:L, :]
    for gi in range(SSM_GROUPS):
        acg_ref[gi] = a_cum[:, gi * HEADS_PER_GROUP:(gi + 1) * HEADS_PER_GROUP]
        actg_ref[gi] = a_cum_t[gi * HEADS_PER_GROUP:(gi + 1) * HEADS_PER_GROUP, :]
    cd3 = _split3(jnp.exp(a_last))
    ds3 = _split3(dsk_ref[...])
    extras = jnp.concatenate([p.astype(F32) for p in cd3 + ds3]
                             + [jnp.zeros((E_ROWS - 3 * L - 6, SSM_HEADS), F32)], axis=0)
    e_mat = jnp.concatenate([dt_act, jnp.exp(a_cum), jnp.exp(a_last - a_cum), extras], axis=0).astype(BF16)

    lane = lax.broadcasted_iota(I32, (L, LANES), 1)
    lo_mask = lane < SSM_HEAD_DIM

    def group_body(gi, carry):
        xs = xs_ref[gi]
        bg = bm_ref[gi]
        cg_b = cm_ref[gi].astype(BF16)
        hsel = (lax.broadcasted_iota(I32, (SSM_HEADS, GROUP_W), 0)
                == gi * HEADS_PER_GROUP + lax.broadcasted_iota(I32, (SSM_HEADS, GROUP_W), 1) // SSM_HEAD_DIM)
        ex = _dot(e_mat, jnp.where(hsel, 1.0, 0.0).astype(BF16))
        dt_rep = ex[0:L]
        expa_rep = ex[L:2 * L]
        dte_rep = ex[2 * L:3 * L]
        cd_rep = ex[3 * L:3 * L + 1] + ex[3 * L + 1:3 * L + 2] + ex[3 * L + 2:3 * L + 3]
        dsk_rep = ex[3 * L + 3:3 * L + 4] + ex[3 * L + 4:3 * L + 5] + ex[3 * L + 5:3 * L + 6]

        xd = xs * dt_rep
        xd_b = xd.astype(BF16)
        cb = _dot_nt(cg_b, bg.astype(BF16))
        acg = acg_ref[gi]
        actg = actg_ref[gi]
        pairs = []
        for pj in range(HEADS_PER_GROUP // 2):
            gmat = []
            for e in (2 * pj, 2 * pj + 1):
                seg = acg[:, e:e + 1] - actg[e:e + 1, :]
                dec = jnp.exp(jnp.where(tril, seg, NEG_INF))
                gmat.append((cb * dec).astype(BF16))
            xp = xd_b[:, pj * LANES:(pj + 1) * LANES]
            zero = jnp.zeros_like(xp)
            pairs.append(_dot(gmat[0], jnp.where(lo_mask, xp, zero))
                         + _dot(gmat[1], jnp.where(lo_mask, zero, xp)))
        y_diag = jnp.concatenate(pairs, axis=1)

        st = state_ref[gi]
        y_off = _dot(cg_b, st.astype(BF16)) * expa_rep
        xdd = (xd * dte_rep).astype(BF16)
        state_ref[gi] = st * cd_rep + _dot(bg.T.astype(BF16), xdd)
        ypre_ref[gi] = y_diag + y_off + dsk_rep * xs
        return carry

    lax.fori_loop(0, SSM_GROUPS, group_body, 0)

    for gi in range(SSM_GROUPS):
        cols = slice(gi * GROUP_W, (gi + 1) * GROUP_W)
        zz = z_ref[:, cols].astype(F32)
        y = ypre_ref[gi] * (zz * _sigmoid(zz))
        ms = jnp.mean(y * y, axis=-1, keepdims=True)
        o_ref[:, cols] = (y * lax.rsqrt(ms + EPS) * ng_ref[:, cols]).astype(o_ref.dtype)


def _ssd(main, small, conv_w, conv_b, dt_bias, a_log, d_skip, norm_g, batch, seq):
    nc = seq // CHUNK
    hb = CHUNK // 8
    row = lambda b, c: b * nc + c
    full = lambda shape: pl.BlockSpec(shape, lambda b, c: (0,) * len(shape))
    return pl.pallas_call(
        _ssd_kernel,
        grid=(batch, nc),
        in_specs=[pl.BlockSpec((CHUNK, CONV_DIM), lambda b, c: (row(b, c), M_XBC // CONV_DIM)),
                  pl.BlockSpec((8, CONV_DIM), lambda b, c: (jnp.maximum(row(b, c) * hb - 1, 0), M_XBC // CONV_DIM)),
                  pl.BlockSpec((CHUNK, SSM_INNER), lambda b, c: (row(b, c), M_Z // SSM_INNER)),
                  pl.BlockSpec((CHUNK, S_END), lambda b, c: (row(b, c), 0)),
                  full((CONV_WIDTH, CONV_DIM)), full((1, CONV_DIM)), full((1, SSM_HEADS)),
                  full((1, SSM_HEADS)), full((1, SSM_HEADS)), full((1, SSM_INNER))],
        out_specs=pl.BlockSpec((CHUNK, SSM_INNER), lambda b, c: (row(b, c), 0)),
        out_shape=jax.ShapeDtypeStruct((batch * seq, SSM_INNER), BF16),
        scratch_shapes=[pltpu.VMEM((SSM_GROUPS, CHUNK, GROUP_W), F32),
                        pltpu.VMEM((SSM_GROUPS, CHUNK, SSM_STATE), F32),
                        pltpu.VMEM((SSM_GROUPS, CHUNK, SSM_STATE), F32),
                        pltpu.VMEM((SSM_GROUPS, SSM_STATE, GROUP_W), F32),
                        pltpu.VMEM((SSM_GROUPS, CHUNK, GROUP_W), F32),
                        pltpu.VMEM((SSM_GROUPS, CHUNK, HEADS_PER_GROUP), F32),
                        pltpu.VMEM((SSM_GROUPS, HEADS_PER_GROUP, CHUNK), F32)],
        compiler_params=_params(("arbitrary", "arbitrary")),
        name="ssd_scan",
    )(main, main, main, small, conv_w, conv_b, dt_bias, a_log, d_skip, norm_g)


def _merge_kernel(att_ref, y_ref, ga_ref, gs_ref, wa_ref, ws_ref, o_ref):
    pa = _dot(att_ref[...], wa_ref[...])
    ps = _dot(y_ref[...], ws_ref[...])
    o_ref[...] = (_sigmoid(ga_ref[...].astype(F32)) * pa + _sigmoid(gs_ref[...].astype(F32)) * ps).astype(o_ref.dtype)


def _merge(att, y, main, wa, ws, tm, tn):
    t = att.shape[0]
    return pl.pallas_call(
        _merge_kernel,
        grid=(t // tm, D_MODEL // tn),
        in_specs=[pl.BlockSpec((tm, ATT_Q), lambda i, j: (i, 0)),
                  pl.BlockSpec((tm, SSM_INNER), lambda i, j: (i, 0)),
                  pl.BlockSpec((tm, tn), lambda i, j: (i, M_GA // tn + j)),
                  pl.BlockSpec((tm, tn), lambda i, j: (i, M_GS // tn + j)),
                  pl.BlockSpec((ATT_Q, tn), lambda i, j: (0, j)),
                  pl.BlockSpec((SSM_INNER, tn), lambda i, j: (0, j))],
        out_specs=pl.BlockSpec((tm, tn), lambda i, j: (i, j)),
        out_shape=jax.ShapeDtypeStruct((t, D_MODEL), BF16),
        compiler_params=_params(("parallel", "arbitrary")),
        name="gated_merge",
    )(att, y, main, main, wa, ws)


def _out_proj_kernel(m_ref, w_ref, x_ref, g_ref, x1_ref, h2_ref):
    x1 = x_ref[...] + _dot(m_ref[...], w_ref[...])
    x1_ref[...] = x1
    ms = jnp.mean(x1 * x1, axis=-1, keepdims=True)
    h2_ref[...] = (x1 * lax.rsqrt(ms + EPS) * g_ref[...]).astype(BF16)


def _out_proj(merged, w, x, g, tm):
    t = x.shape[0]
    return pl.pallas_call(
        _out_proj_kernel,
        grid=(t // tm,),
        in_specs=[pl.BlockSpec((tm, D_MODEL), lambda i: (i, 0)),
                  pl.BlockSpec((D_MODEL, D_MODEL), lambda i: (0, 0)),
                  pl.BlockSpec((tm, D_MODEL), lambda i: (i, 0)),
                  pl.BlockSpec((1, D_MODEL), lambda i: (0, 0))],
        out_specs=[pl.BlockSpec((tm, D_MODEL), lambda i: (i, 0)),
                   pl.BlockSpec((tm, D_MODEL), lambda i: (i, 0))],
        out_shape=[jax.ShapeDtypeStruct((t, D_MODEL), F32),
                   jax.ShapeDtypeStruct((t, D_MODEL), BF16)],
        compiler_params=_params(("parallel",)),
        name="out_proj_norm",
    )(merged, w, x, g)


def _mlp_kernel(h_ref, x1_ref, wu_ref, wd_ref, o_ref):
    @pl.when(pl.program_id(1) == 0)
    def _():
        o_ref[...] = x1_ref[...]

    u = _dot(h_ref[...], wu_ref[...])
    u = jnp.square(jnp.maximum(u, 0.0)).astype(BF16)
    o_ref[...] += _dot(u, wd_ref[...])


def _mlp(h2, x1, wu, wd, tm, th):
    t = h2.shape[0]
    return pl.pallas_call(
        _mlp_kernel,
        grid=(t // tm, MLP_HIDDEN // th),
        in_specs=[pl.BlockSpec((tm, D_MODEL), lambda i, j: (i, 0)),
                  pl.BlockSpec((tm, D_MODEL), lambda i, j: (i, 0)),
                  pl.BlockSpec((D_MODEL, th), lambda i, j: (0, j)),
                  pl.BlockSpec((th, D_MODEL), lambda i, j: (j, 0))],
        out_specs=pl.BlockSpec((tm, D_MODEL), lambda i, j: (i, 0)),
        out_shape=jax.ShapeDtypeStruct((t, D_MODEL), F32),
        compiler_params=_params(("parallel", "arbitrary")),
        name="relu2_mlp",
    )(h2, x1, wu, wd)


def _pack_w_in(w):
    seg = lambda o, n: w[:, o:o + n]
    main = jnp.concatenate([seg(O_GA, D_MODEL), seg(O_GS, D_MODEL), seg(O_Q, ATT_Q), seg(O_K, ATT_KV),
                            seg(O_V, ATT_KV), seg(O_QI, IDX_Q), seg(O_Z, SSM_INNER), seg(O_XBC, CONV_DIM)],
                           axis=1).astype(BF16)
    small = jnp.concatenate([seg(O_KI, IDX_DIM), seg(O_WI, IDX_HEADS), seg(O_DT, SSM_HEADS),
                             jnp.zeros((w.shape[0], S_END - S_DT - SSM_HEADS), w.dtype)], axis=1).astype(BF16)
    return main, small


def _block(x2, batch, seq, norm1_g, w_in, conv_w, conv_b, dt_bias, a_log, d_skip, ssm_norm_g, q_norm_g,
           k_norm_g, rel_bias, w_att_branch, w_ssm_branch, w_out, norm2_g, w_up, w_down):
    row = lambda v: v.reshape(1, -1)
    w_main, w_small = _pack_w_in(w_in)
    main = _norm_matmul(x2, row(norm1_g), w_main, BF16, tm=min(1024, x2.shape[0]), tn=512)
    small = _norm_matmul(x2, row(norm1_g), w_small, F32, tm=min(1024, x2.shape[0]), tn=S_END)
    qn, kn, ki = _qk_prep(main, small, row(q_norm_g), row(k_norm_g), tq=512)
    att = _dsa(qn, main, small, kn, ki, rel_bias, batch, seq)
    y = _ssd(main, small, conv_w, row(conv_b), row(dt_bias), row(a_log), row(d_skip), row(ssm_norm_g),
             batch, seq)
    merged = _merge(att, y, main, w_att_branch.astype(BF16), w_ssm_branch.astype(BF16), tm=512, tn=512)
    x1, h2 = _out_proj(merged, w_out.astype(BF16), x2, row(norm2_g), tm=512)
    return _mlp(h2, x1, w_up.astype(BF16), w_down.astype(BF16), tm=512, th=512)


def kernel(x, norm1_g, w_in, conv_w, conv_b, dt_bias, a_log, d_skip, ssm_norm_g, q_norm_g, k_norm_g, rel_bias,
           w_att_branch, w_ssm_branch, w_out, norm2_g, w_up, w_down):
    batch, seq, d = x.shape
    x2 = x.reshape(batch * seq, d)
    for l in range(norm1_g.shape[0]):
        x2 = _block(x2, batch, seq, norm1_g[l], w_in[l], conv_w[l], conv_b[l], dt_bias[l], a_log[l], d_skip[l],
                    ssm_norm_g[l], q_norm_g[l], k_norm_g[l], rel_bias, w_att_branch[l], w_ssm_branch[l],
                    w_out[l], norm2_g[l], w_up[l], w_down[l])
    return x2.reshape(batch, seq, d)
```

```python
import functools
import math

import numpy as np
import jax
import jax.numpy as jnp
from jax import lax
from jax.experimental import pallas as pl
from jax.experimental.pallas import tpu as pltpu

F32 = jnp.float32
BF16 = jnp.bfloat16
I32 = jnp.int32

D_MODEL = 2048
ATT_HEADS = 16
ATT_KV_HEADS = 4
HEAD_DIM = 128
REP = ATT_HEADS // ATT_KV_HEADS
IDX_HEADS = 16
IDX_DIM = 64
TOPK_MAX = 256
N_BUCKETS = 32
MAX_DISTANCE = 128
SSM_INNER = 2 * D_MODEL
SSM_HEAD_DIM = 64
SSM_HEADS = SSM_INNER // SSM_HEAD_DIM
SSM_GROUPS = 8
SSM_STATE = 128
CONV_WIDTH = 4
CHUNK = 128
MLP_HIDDEN = 4 * D_MODEL
EPS = 1e-6

ATT_Q = ATT_HEADS * HEAD_DIM
ATT_KV = ATT_KV_HEADS * HEAD_DIM
IDX_Q = IDX_HEADS * IDX_DIM
SSM_BC = SSM_GROUPS * SSM_STATE
CONV_DIM = SSM_INNER + 2 * SSM_BC
SPLITS = (D_MODEL, D_MODEL, ATT_Q, ATT_KV, ATT_KV, IDX_Q, IDX_DIM, IDX_HEADS, SSM_INNER, CONV_DIM, SSM_HEADS)
_OFFS = tuple(int(v) for v in np.cumsum((0,) + SPLITS))
(O_GA, O_GS, O_Q, O_K, O_V, O_QI, O_KI, O_WI, O_Z, O_XBC, O_DT, _O_END) = _OFFS

M_GA = 0
M_GS = M_GA + D_MODEL
M_Q = M_GS + D_MODEL
M_K = M_Q + ATT_Q
M_V = M_K + ATT_KV
M_QI = M_V + ATT_KV
M_Z = M_QI + IDX_Q
M_XBC = M_Z + SSM_INNER
M_END = M_XBC + CONV_DIM
S_KI = 0
S_WI = S_KI + IDX_DIM
S_DT = S_WI + IDX_HEADS
S_END = 256

HEADS_PER_GROUP = SSM_HEADS // SSM_GROUPS
GROUP_W = HEADS_PER_GROUP * SSM_HEAD_DIM

LANES = 128
VMEM_LIMIT = 48 * 1024 * 1024

NEG_INF = float("-inf")
INT_MIN = -(2 ** 31)
KEY_NEG_INF = int(np.int32(np.uint32(0xFF800000) ^ np.uint32(0x7FFFFFFF)))


def _dot(a, b):
    return jnp.dot(a, b, preferred_element_type=F32)


def _dot_nt(a, b):
    return lax.dot_general(a, b, (((1,), (1,)), ((), ())), preferred_element_type=F32)


def _split3(x):
    hi = x.astype(BF16)
    r = x - hi.astype(F32)
    mid = r.astype(BF16)
    lo = (r - mid.astype(F32)).astype(BF16)
    return hi, mid, lo


def _sigmoid(x):
    return 1.0 / (1.0 + jnp.exp(-x))


def _params(sem):
    return pltpu.CompilerParams(dimension_semantics=sem, vmem_limit_bytes=VMEM_LIMIT)


def _norm_matmul_kernel(x_ref, g_ref, w_ref, o_ref, xn_ref):
    @pl.when(pl.program_id(1) == 0)
    def _():
        x = x_ref[...]
        ms = jnp.mean(x * x, axis=-1, keepdims=True)
        xn_ref[...] = (x * lax.rsqrt(ms + EPS) * g_ref[...]).astype(BF16)

    o_ref[...] = _dot(xn_ref[...], w_ref[...]).astype(o_ref.dtype)


def _norm_matmul(x, g, w, out_dtype, tm, tn):
    m, d = x.shape
    n = w.shape[1]
    return pl.pallas_call(
        _norm_matmul_kernel,
        grid=(m // tm, n // tn),
        in_specs=[pl.BlockSpec((tm, d), lambda i, j: (i, 0)),
                  pl.BlockSpec((1, d), lambda i, j: (0, 0)),
                  pl.BlockSpec((d, tn), lambda i, j: (0, j))],
        out_specs=pl.BlockSpec((tm, tn), lambda i, j: (i, j)),
        out_shape=jax.ShapeDtypeStruct((m, n), out_dtype),
        scratch_shapes=[pltpu.VMEM((tm, d), BF16)],
        compiler_params=_params(("parallel", "arbitrary")),
        name="norm_in_proj",
    )(x, g, w)


LOG2E = math.log2(math.e)


def _qk_prep_kernel(q_ref, k_ref, v_ref, qi_ref, sm_ref, qg_ref, kg_ref,
                    qt_ref, kn_ref, vt_ref, qit_ref, wt_ref, ki_ref):
    qg = qg_ref[...]
    for h in range(ATT_HEADS):
        x = q_ref[:, h * HEAD_DIM:(h + 1) * HEAD_DIM].astype(F32)
        ms = jnp.mean(x * x, axis=-1, keepdims=True)
        y = x * lax.rsqrt(ms + EPS) * qg * (HEAD_DIM ** -0.5 * LOG2E)
        qt_ref[0, h] = y.T.astype(BF16)
    kg = kg_ref[...]
    for h in range(ATT_KV_HEADS):
        x = k_ref[:, h * HEAD_DIM:(h + 1) * HEAD_DIM].astype(F32)
        ms = jnp.mean(x * x, axis=-1, keepdims=True)
        kn_ref[:, h * HEAD_DIM:(h + 1) * HEAD_DIM] = (x * lax.rsqrt(ms + EPS) * kg).astype(BF16)
        vt_ref[0, h, 0] = v_ref[:, h * HEAD_DIM:(h + 1) * HEAD_DIM].astype(F32).T.astype(BF16)
    for p in range(IDX_Q // LANES):
        qit_ref[0, p * LANES:(p + 1) * LANES, :] = qi_ref[:, p * LANES:(p + 1) * LANES].astype(F32).T.astype(BF16)
    sm_t = sm_ref[:, 0:LANES].T
    wt_ref[0] = sm_t[S_WI:S_WI + IDX_HEADS, :] * (IDX_HEADS ** -0.5 * IDX_DIM ** -0.5)
    ki_ref[...] = sm_ref[:, S_KI:S_KI + IDX_DIM].astype(BF16)


def _qk_prep(main, small, qg, kg, batch, seq, tq):
    t = main.shape[0]
    nq = seq // tq
    return pl.pallas_call(
        _qk_prep_kernel,
        grid=(batch, nq),
        in_specs=[pl.BlockSpec((tq, ATT_Q), lambda b, i: (b * nq + i, M_Q // ATT_Q)),
                  pl.BlockSpec((tq, ATT_KV), lambda b, i: (b * nq + i, M_K // ATT_KV)),
                  pl.BlockSpec((tq, ATT_KV), lambda b, i: (b * nq + i, M_V // ATT_KV)),
                  pl.BlockSpec((tq, IDX_Q), lambda b, i: (b * nq + i, M_QI // IDX_Q)),
                  pl.BlockSpec((tq, S_END), lambda b, i: (b * nq + i, 0)),
                  pl.BlockSpec((1, HEAD_DIM), lambda b, i: (0, 0)),
                  pl.BlockSpec((1, HEAD_DIM), lambda b, i: (0, 0))],
        out_specs=[pl.BlockSpec((1, ATT_HEADS, HEAD_DIM, tq), lambda b, i: (b * nq + i, 0, 0, 0)),
                   pl.BlockSpec((tq, ATT_KV), lambda b, i: (b * nq + i, 0)),
                   pl.BlockSpec((1, ATT_KV_HEADS, 1, HEAD_DIM, tq), lambda b, i: (b, 0, i, 0, 0)),
                   pl.BlockSpec((1, IDX_Q, tq), lambda b, i: (b * nq + i, 0, 0)),
                   pl.BlockSpec((1, IDX_HEADS, tq), lambda b, i: (b * nq + i, 0, 0)),
                   pl.BlockSpec((tq, IDX_DIM), lambda b, i: (b * nq + i, 0))],
        out_shape=[jax.ShapeDtypeStruct((t // tq, ATT_HEADS, HEAD_DIM, tq), BF16),
                   jax.ShapeDtypeStruct((t, ATT_KV), BF16),
                   jax.ShapeDtypeStruct((batch, ATT_KV_HEADS, nq, HEAD_DIM, tq), BF16),
                   jax.ShapeDtypeStruct((t // tq, IDX_Q, tq), BF16),
                   jax.ShapeDtypeStruct((t // tq, IDX_HEADS, tq), F32),
                   jax.ShapeDtypeStruct((t, IDX_DIM), BF16)],
        compiler_params=_params(("parallel", "parallel")),
        name="qk_prep",
    )(main, main, main, main, small, qg, kg)


DSA_TQ = 256
DSA_TK = 256


def _t5_bucket_np(dist):
    n = np.maximum(dist, 0)
    max_exact = N_BUCKETS // 2
    nf = np.maximum(n, 1).astype(np.float32)
    ratio = (np.log(nf / np.float32(max_exact)) / np.float32(math.log(MAX_DISTANCE / max_exact))
             * np.float32(N_BUCKETS - max_exact))
    large = max_exact + ratio.astype(np.int32)
    large = np.minimum(large, N_BUCKETS - 1)
    return np.where(n < max_exact, n, large).astype(np.int32)


def _bias_bucket_tiles(tq, tk):
    r = np.arange(tq)[None, :]
    c = np.arange(tk)[:, None]
    d0 = _t5_bucket_np(r - c)
    d1 = _t5_bucket_np(tk + r - c)
    assert np.all(_t5_bucket_np(np.arange(tk + 1, 8 * tk)) == N_BUCKETS - 1)
    return np.stack([d0, d1]).astype(np.int32)


def _dsa_kernel(relb_ref, bidx_ref, qt_ref, qit_ref, wt_ref, k_ref, vt_ref, ki_ref, o_ref,
                key_ref, madd_ref, bias_ref, *, tq, tk, nkc, topk):
    b = pl.program_id(0)
    i = pl.program_id(1)
    g = pl.program_id(2)
    neg_slot = nkc

    @pl.when((b == 0) & (i == 0) & (g == 0))
    def _init():
        madd_ref[neg_slot] = jnp.full((tk, tq), NEG_INF, F32)
        for t in range(2):
            bt = bidx_ref[t]

            def head_body(h, carry):
                far = relb_ref[N_BUCKETS - 1, h]

                def bucket_body(bk, acc):
                    return jnp.where(bt == bk, (relb_ref[bk, h] - far) * LOG2E, acc)

                bias_ref[t, h] = lax.fori_loop(0, N_BUCKETS, bucket_body, jnp.zeros((tk, tq), F32))
                return carry

            lax.fori_loop(0, ATT_HEADS, head_body, 0)

    @pl.when(g == 0)
    def _select():
        nj = i + 1
        qpos = i * tq + lax.broadcasted_iota(I32, (tk, tq), 1)

        def score_chunk(j, carry):
            kc = ki_ref[pl.ds(pl.multiple_of(j * tk, tk), tk), :]
            acc = jnp.zeros((tk, tq), F32)
            for h in range(IDX_HEADS):
                z = _dot(kc, qit_ref[0, h * IDX_DIM:(h + 1) * IDX_DIM, :])
                acc = acc + jnp.maximum(z, 0.0) * wt_ref[0, h:h + 1, :]
            kpos = j * tk + lax.broadcasted_iota(I32, (tk, tq), 0)
            acc = jnp.where(kpos <= qpos, acc, NEG_INF)
            bits = pltpu.bitcast(acc, I32)
            key_ref[j] = bits ^ ((bits >> 31) & 0x7FFFFFFF)
            return carry

        lax.fori_loop(0, nj, score_chunk, 0)

        def bit_iter(it, prefix):
            cand = prefix + lax.shift_left(jnp.int32(1), 31 - it)

            def cnt_chunk(j, cnt):
                hit = jnp.where(key_ref[j] >= cand, 1.0, 0.0)
                return cnt + jnp.sum(hit.reshape(tk // 8, 8, tq), axis=0)

            cnt = lax.fori_loop(0, nj, cnt_chunk, jnp.zeros((8, tq), F32))
            tot = jnp.sum(cnt, axis=0, keepdims=True)
            return jnp.where(tot >= float(topk), cand, prefix)

        thr = lax.fori_loop(0, 32, bit_iter, jnp.full((1, tq), INT_MIN, I32))
        thr = jnp.maximum(thr, KEY_NEG_INF + 1)

        def madd_chunk(j, carry):
            madd_ref[j] = jnp.where(key_ref[j] >= thr, 0.0, NEG_INF)
            return carry

        lax.fori_loop(0, nj, madd_chunk, 0)

    qt = jnp.concatenate([qt_ref[0, r] for r in range(REP)], axis=1)
    c0 = jnp.maximum(i - 1, 0)

    def attn_step(k_c, vt_c, madd_c, bias_c, m, l, acc):
        s = _dot(k_c, qt)
        parts = []
        for r in range(REP):
            add = madd_c if bias_c is None else madd_c + bias_c[r]
            parts.append(s[:, r * tq:(r + 1) * tq] + add)
        s = jnp.concatenate(parts, axis=1)
        m_new = jnp.maximum(m, jnp.max(s, axis=0, keepdims=True))
        m_safe = jnp.where(m_new == NEG_INF, 0.0, m_new)
        alpha = jnp.exp2(m - m_safe)
        p = jnp.exp2(s - m_safe)
        l = alpha * l + jnp.sum(p, axis=0, keepdims=True)
        acc = alpha * acc + _dot(vt_c, p.astype(BF16))
        return m_new, l, acc

    m0 = jnp.full((1, REP * tq), NEG_INF, F32)
    l0 = jnp.zeros((1, REP * tq), F32)
    a0 = jnp.zeros((HEAD_DIM, REP * tq), F32)

    first = i == 0
    t_a = jnp.where(first, 0, 1)
    c1 = jnp.minimum(c0 + 1, nkc - 1)
    idx_b = jnp.where(first, neg_slot, c0 + 1)
    rows = pl.ds(pl.multiple_of(c0 * tk, tk), 2 * tk)
    madd_near = jnp.concatenate([madd_ref[c0], madd_ref[idx_b]], axis=0)
    bias_near = [jnp.concatenate([bias_ref[t_a, g * REP + r], bias_ref[0, g * REP + r]], axis=0)
                 for r in range(REP)]
    vt_near = jnp.concatenate([vt_ref[0, 0, c0], vt_ref[0, 0, c1]], axis=1)
    m1, l1, a1 = attn_step(k_ref[rows, :], vt_near, madd_near, bias_near, m0, l0, a0)

    def far_body(jb, carry):
        m, l, acc = carry
        ca = 2 * jb
        cb = jnp.where(ca + 1 >= c0, neg_slot, ca + 1)
        rows_f = pl.ds(pl.multiple_of(ca * tk, 2 * tk), 2 * tk)
        madd_f = jnp.concatenate([madd_ref[ca], madd_ref[cb]], axis=0)
        vt_f = jnp.concatenate([vt_ref[0, 0, ca], vt_ref[0, 0, ca + 1]], axis=1)
        return attn_step(k_ref[rows_f, :], vt_f, madd_f, None, m, l, acc)

    m2, l2, a2 = lax.fori_loop(0, (c0 + 1) // 2, far_body, (m1, l1, a1))
    out = a2 / l2
    for r in range(REP):
        o_ref[:, r * HEAD_DIM:(r + 1) * HEAD_DIM] = out[:, r * tq:(r + 1) * tq].T.astype(o_ref.dtype)


def _dsa(qt, kn, vt, qit, wt, ki, rel_bias, batch, seq):
    tq, tk = DSA_TQ, DSA_TK
    nq = seq // tq
    nkc = seq // tk
    topk = min(TOPK_MAX, seq // 4)
    bidx = jnp.asarray(_bias_bucket_tiles(tq, tk))
    kern = functools.partial(_dsa_kernel, tq=tq, tk=tk, nkc=nkc, topk=topk)
    gw = REP * HEAD_DIM
    return pl.pallas_call(
        kern,
        grid=(batch, nq, ATT_KV_HEADS),
        in_specs=[pl.BlockSpec(memory_space=pltpu.SMEM),
                  pl.BlockSpec((2, tk, tq), lambda b, i, g: (0, 0, 0)),
                  pl.BlockSpec((1, REP, HEAD_DIM, tq), lambda b, i, g: (b * nq + i, g, 0, 0)),
                  pl.BlockSpec((1, IDX_Q, tq), lambda b, i, g: (b * nq + i, 0, 0)),
                  pl.BlockSpec((1, IDX_HEADS, tq), lambda b, i, g: (b * nq + i, 0, 0)),
                  pl.BlockSpec((seq, HEAD_DIM), lambda b, i, g: (b, g)),
                  pl.BlockSpec((1, 1, nkc, HEAD_DIM, tk), lambda b, i, g: (b, g, 0, 0, 0)),
                  pl.BlockSpec((seq, IDX_DIM), lambda b, i, g: (b, 0))],
        out_specs=pl.BlockSpec((tq, gw), lambda b, i, g: (b * nq + i, g)),
        out_shape=jax.ShapeDtypeStruct((batch * seq, ATT_Q), BF16),
        scratch_shapes=[pltpu.VMEM((nkc, tk, tq), I32),
                        pltpu.VMEM((nkc + 1, tk, tq), F32),
                        pltpu.VMEM((2, ATT_HEADS, tk, tq), F32)],
        compiler_params=_params(("arbitrary", "arbitrary", "arbitrary")),
        name="dsa_attention",
    )(rel_bias, bidx, qt, qit, wt, kn, vt, ki)


E_ROWS = 3 * CHUNK + 16


def _ssd_kernel(xbc_ref, halo_ref, z_ref, sm_ref, cw_ref, cb_ref, dtb_ref, alog_ref, dsk_ref, ng_ref,
                o_ref, xs_ref, bm_ref, cm_ref, state_ref, ypre_ref, acg_ref, actg_ref):
    c = pl.program_id(1)
    L = CHUNK

    @pl.when(c == 0)
    def _():
        state_ref[...] = jnp.zeros_like(state_ref)

    halo_on = (c > 0).astype(F32)
    cblk = GROUP_W
    for cbi in range(CONV_DIM // cblk):
        cols = slice(cbi * cblk, (cbi + 1) * cblk)
        xin = xbc_ref[:, cols].astype(F32)
        hl = halo_ref[:, cols].astype(F32) * halo_on
        ext = jnp.concatenate([hl, xin], axis=0)
        acc = cb_ref[:, cols] + cw_ref[CONV_WIDTH - 1:CONV_WIDTH, cols] * xin
        for kk in range(CONV_WIDTH - 1):
            off = 8 - (CONV_WIDTH - 1) + kk
            acc = acc + cw_ref[kk:kk + 1, cols] * ext[off:off + L]
        y = acc * _sigmoid(acc)
        if cbi < SSM_GROUPS:
            xs_ref[cbi] = y
        else:
            per = cblk // SSM_STATE
            for u in range(per):
                gi = (cbi - SSM_GROUPS) * per + u
                piece = y[:, u * SSM_STATE:(u + 1) * SSM_STATE]
                if gi < SSM_GROUPS:
                    bm_ref[gi] = piece
                else:
                    cm_ref[gi - SSM_GROUPS] = piece

    dt_in = sm_ref[:, S_DT:S_DT + SSM_HEADS] + dtb_ref[...]
    dt_act = jnp.maximum(dt_in, 0.0) + jnp.log1p(jnp.exp(-jnp.abs(dt_in)))
    a = dt_act * (-jnp.exp(alog_ref[...]))
    ri = lax.broadcasted_iota(I32, (L, L), 0)
    ci = lax.broadcasted_iota(I32, (L, L), 1)
    tril = ri >= ci
    tri_b = jnp.where(tril, 1.0, 0.0).astype(BF16)
    a_cum = sum(_dot(tri_b, p) for p in _split3(a))
    eye_b = jnp.where(lax.broadcasted_iota(I32, (SSM_HEADS, SSM_HEADS), 0)
                      == lax.broadcasted_iota(I32, (SSM_HEADS, SSM_HEADS), 1), 1.0, 0.0).astype(BF16)
    a_cum_t = sum(_dot_nt(eye_b, p) for p in _split3(a_cum))
    a_last = a_cum[L - 1:L, :]
    for gi in range(SSM_GROUPS):
        acg_ref[gi] = a_cum[:, gi * HEADS_PER_GROUP:(gi + 1) * HEADS_PER_GROUP]
        actg_ref[gi] = a_cum_t[gi * HEADS_PER_GROUP:(gi + 1) * HEADS_PER_GROUP, :]
    cd3 = _split3(jnp.exp(a_last))
    ds3 = _split3(dsk_ref[...])
    extras = jnp.concatenate([p.astype(F32) for p in cd3 + ds3]
                             + [jnp.zeros((E_ROWS - 3 * L - 6, SSM_HEADS), F32)], axis=0)
    e_mat = jnp.concatenate([dt_act, jnp.exp(a_cum), jnp.exp(a_last - a_cum), extras], axis=0).astype(BF16)

    lane = lax.broadcasted_iota(I32, (L, LANES), 1)
    lo_mask = lane < SSM_HEAD_DIM

    def group_body(gi, carry):
        xs = xs_ref[gi]
        bg = bm_ref[gi]
        cg_b = cm_ref[gi].astype(BF16)
        hsel = (lax.broadcasted_iota(I32, (SSM_HEADS, GROUP_W), 0)
                == gi * HEADS_PER_GROUP + lax.broadcasted_iota(I32, (SSM_HEADS, GROUP_W), 1) // SSM_HEAD_DIM)
        ex = _dot(e_mat, jnp.where(hsel, 1.0, 0.0).astype(BF16))
        dt_rep = ex[0:L]
        expa_rep = ex[L:2 * L]
        dte_rep = ex[2 * L:3 * L]
        cd_rep = ex[3 * L:3 * L + 1] + ex[3 * L + 1:3 * L + 2] + ex[3 * L + 2:3 * L + 3]
        dsk_rep = ex[3 * L + 3:3 * L + 4] + ex[3 * L + 4:3 * L + 5] + ex[3 * L + 5:3 * L + 6]

        xd = xs * dt_rep
        xd_b = xd.astype(BF16)
        cb = _dot_nt(cg_b, bg.astype(BF16))
        acg = acg_ref[gi]
        actg = actg_ref[gi]
        pairs = []
        for pj in range(HEADS_PER_GROUP // 2):
            gmat = []
            for e in (2 * pj, 2 * pj + 1):
                seg = acg[:, e:e + 1] - actg[e:e + 1, :]
                dec = jnp.exp(jnp.where(tril, seg, NEG_INF))
                gmat.append((cb * dec).astype(BF16))
            xp = xd_b[:, pj * LANES:(pj + 1) * LANES]
            zero = jnp.zeros_like(xp)
            pairs.append(_dot(gmat[0], jnp.where(lo_mask, xp, zero))
                         + _dot(gmat[1], jnp.where(lo_mask, zero, xp)))
        y_diag = jnp.concatenate(pairs, axis=1)

        st = state_ref[gi]
        y_off = _dot(cg_b, st.astype(BF16)) * expa_rep
        xdd = (xd * dte_rep).astype(BF16)
        state_ref[gi] = st * cd_rep + _dot(bg.T.astype(BF16), xdd)
        ypre_ref[gi] = y_diag + y_off + dsk_rep * xs
        return carry

    lax.fori_loop(0, SSM_GROUPS, group_body, 0)

    for gi in range(SSM_GROUPS):
        cols = slice(gi * GROUP_W, (gi + 1) * GROUP_W)
        zz = z_ref[:, cols].astype(F32)
        y = ypre_ref[gi] * (zz * _sigmoid(zz))
        ms = jnp.mean(y * y, axis=-1, keepdims=True)
        o_ref[:, cols] = (y * lax.rsqrt(ms + EPS) * ng_ref[:, cols]).astype(o_ref.dtype)


def _ssd(main, small, conv_w, conv_b, dt_bias, a_log, d_skip, norm_g, batch, seq):
    nc = seq // CHUNK
    hb = CHUNK // 8
    row = lambda b, c: b * nc + c
    full = lambda shape: pl.BlockSpec(shape, lambda b, c: (0,) * len(shape))
    return pl.pallas_call(
        _ssd_kernel,
        grid=(batch, nc),
        in_specs=[pl.BlockSpec((CHUNK, CONV_DIM), lambda b, c: (row(b, c), M_XBC // CONV_DIM)),
                  pl.BlockSpec((8, CONV_DIM), lambda b, c: (jnp.maximum(row(b, c) * hb - 1, 0), M_XBC // CONV_DIM)),
                  pl.BlockSpec((CHUNK, SSM_INNER), lambda b, c: (row(b, c), M_Z // SSM_INNER)),
                  pl.BlockSpec((CHUNK, S_END), lambda b, c: (row(b, c), 0)),
                  full((CONV_WIDTH, CONV_DIM)), full((1, CONV_DIM)), full((1, SSM_HEADS)),
                  full((1, SSM_HEADS)), full((1, SSM_HEADS)), full((1, SSM_INNER))],
        out_specs=pl.BlockSpec((CHUNK, SSM_INNER), lambda b, c: (row(b, c), 0)),
        out_shape=jax.ShapeDtypeStruct((batch * seq, SSM_INNER), BF16),
        scratch_shapes=[pltpu.VMEM((SSM_GROUPS, CHUNK, GROUP_W), F32),
                        pltpu.VMEM((SSM_GROUPS, CHUNK, SSM_STATE), F32),
                        pltpu.VMEM((SSM_GROUPS, CHUNK, SSM_STATE), F32),
                        pltpu.VMEM((SSM_GROUPS, SSM_STATE, GROUP_W), F32),
                        pltpu.VMEM((SSM_GROUPS, CHUNK, GROUP_W), F32),
                        pltpu.VMEM((SSM_GROUPS, CHUNK, HEADS_PER_GROUP), F32),
                        pltpu.VMEM((SSM_GROUPS, HEADS_PER_GROUP, CHUNK), F32)],
        compiler_params=_params(("arbitrary", "arbitrary")),
        name="ssd_scan",
    )(main, main, main, small, conv_w, conv_b, dt_bias, a_log, d_skip, norm_g)


def _merge_kernel(att_ref, y_ref, ga_ref, gs_ref, wa_ref, ws_ref, o_ref):
    pa = _dot(att_ref[...], wa_ref[...])
    ps = _dot(y_ref[...], ws_ref[...])
    o_ref[...] = (_sigmoid(ga_ref[...].astype(F32)) * pa + _sigmoid(gs_ref[...].astype(F32)) * ps).astype(o_ref.dtype)


def _merge(att, y, main, wa, ws, tm, tn):
    t = att.shape[0]
    return pl.pallas_call(
        _merge_kernel,
        grid=(t // tm, D_MODEL // tn),
        in_specs=[pl.BlockSpec((tm, ATT_Q), lambda i, j: (i, 0)),
                  pl.BlockSpec((tm, SSM_INNER), lambda i, j: (i, 0)),
                  pl.BlockSpec((tm, tn), lambda i, j: (i, M_GA // tn + j)),
                  pl.BlockSpec((tm, tn), lambda i, j: (i, M_GS // tn + j)),
                  pl.BlockSpec((ATT_Q, tn), lambda i, j: (0, j)),
                  pl.BlockSpec((SSM_INNER, tn), lambda i, j: (0, j))],
        out_specs=pl.BlockSpec((tm, tn), lambda i, j: (i, j)),
        out_shape=jax.ShapeDtypeStruct((t, D_MODEL), BF16),
        compiler_params=_params(("parallel", "arbitrary")),
        name="gated_merge",
    )(att, y, main, main, wa, ws)


def _out_proj_kernel(m_ref, w_ref, x_ref, g_ref, x1_ref, h2_ref):
    x1 = x_ref[...] + _dot(m_ref[...], w_ref[...])
    x1_ref[...] = x1
    ms = jnp.mean(x1 * x1, axis=-1, keepdims=True)
    h2_ref[...] = (x1 * lax.rsqrt(ms + EPS) * g_ref[...]).astype(BF16)


def _out_proj(merged, w, x, g, tm):
    t = x.shape[0]
    return pl.pallas_call(
        _out_proj_kernel,
        grid=(t // tm,),
        in_specs=[pl.BlockSpec((tm, D_MODEL), lambda i: (i, 0)),
                  pl.BlockSpec((D_MODEL, D_MODEL), lambda i: (0, 0)),
                  pl.BlockSpec((tm, D_MODEL), lambda i: (i, 0)),
                  pl.BlockSpec((1, D_MODEL), lambda i: (0, 0))],
        out_specs=[pl.BlockSpec((tm, D_MODEL), lambda i: (i, 0)),
                   pl.BlockSpec((tm, D_MODEL), lambda i: (i, 0))],
        out_shape=[jax.ShapeDtypeStruct((t, D_MODEL), F32),
                   jax.ShapeDtypeStruct((t, D_MODEL), BF16)],
        compiler_params=_params(("parallel",)),
        name="out_proj_norm",
    )(merged, w, x, g)


def _mlp_kernel(h_ref, x1_ref, wu_ref, wd_ref, o_ref):
    @pl.when(pl.program_id(1) == 0)
    def _():
        o_ref[...] = x1_ref[...]

    u = _dot(h_ref[...], wu_ref[...])
    u = jnp.square(jnp.maximum(u, 0.0)).astype(BF16)
    o_ref[...] += _dot(u, wd_ref[...])


def _mlp(h2, x1, wu, wd, tm, th):
    t = h2.shape[0]
    return pl.pallas_call(
        _mlp_kernel,
        grid=(t // tm, MLP_HIDDEN // th),
        in_specs=[pl.BlockSpec((tm, D_MODEL), lambda i, j: (i, 0)),
                  pl.BlockSpec((tm, D_MODEL), lambda i, j: (i, 0)),
                  pl.BlockSpec((D_MODEL, th), lambda i, j: (0, j)),
                  pl.BlockSpec((th, D_MODEL), lambda i, j: (j, 0))],
        out_specs=pl.BlockSpec((tm, D_MODEL), lambda i, j: (i, 0)),
        out_shape=jax.ShapeDtypeStruct((t, D_MODEL), F32),
        compiler_params=_params(("parallel", "arbitrary")),
        name="relu2_mlp",
    )(h2, x1, wu, wd)


def _pack_w_in(w):
    seg = lambda o, n: w[:, o:o + n]
    main = jnp.concatenate([seg(O_GA, D_MODEL), seg(O_GS, D_MODEL), seg(O_Q, ATT_Q), seg(O_K, ATT_KV),
                            seg(O_V, ATT_KV), seg(O_QI, IDX_Q), seg(O_Z, SSM_INNER), seg(O_XBC, CONV_DIM)],
                           axis=1).astype(BF16)
    small = jnp.concatenate([seg(O_KI, IDX_DIM), seg(O_WI, IDX_HEADS), seg(O_DT, SSM_HEADS),
                             jnp.zeros((w.shape[0], S_END - S_DT - SSM_HEADS), w.dtype)], axis=1).astype(BF16)
    return main, small


def _block(x2, batch, seq, norm1_g, w_in, conv_w, conv_b, dt_bias, a_log, d_skip, ssm_norm_g, q_norm_g,
           k_norm_g, rel_bias, w_att_branch, w_ssm_branch, w_out, norm2_g, w_up, w_down):
    row = lambda v: v.reshape(1, -1)
    w_main, w_small = _pack_w_in(w_in)
    main = _norm_matmul(x2, row(norm1_g), w_main, BF16, tm=min(1024, x2.shape[0]), tn=512)
    small = _norm_matmul(x2, row(norm1_g), w_small, F32, tm=min(1024, x2.shape[0]), tn=S_END)
    qt, kn, vt, qit, wt, ki = _qk_prep(main, small, row(q_norm_g), row(k_norm_g), batch, seq, tq=DSA_TQ)
    att = _dsa(qt, kn, vt, qit, wt, ki, rel_bias, batch, seq)
    y = _ssd(main, small, conv_w, row(conv_b), row(dt_bias), row(a_log), row(d_skip), row(ssm_norm_g),
             batch, seq)
    merged = _merge(att, y, main, w_att_branch.astype(BF16), w_ssm_branch.astype(BF16), tm=512, tn=512)
    x1, h2 = _out_proj(merged, w_out.astype(BF16), x2, row(norm2_g), tm=512)
    return _mlp(h2, x1, w_up.astype(BF16), w_down.astype(BF16), tm=512, th=512)


def kernel(x, norm1_g, w_in, conv_w, conv_b, dt_bias, a_log, d_skip, ssm_norm_g, q_norm_g, k_norm_g, rel_bias,
           w_att_branch, w_ssm_branch, w_out, norm2_g, w_up, w_down):
    batch, seq, d = x.shape
    x2 = x.reshape(batch * seq, d)
    for l in range(norm1_g.shape[0]):
        x2 = _block(x2, batch, seq, norm1_g[l], w_in[l], conv_w[l], conv_b[l], dt_bias[l], a_log[l], d_skip[l],
                    ssm_norm_g[l], q_norm_g[l], k_norm_g[l], rel_bias, w_att_branch[l], w_ssm_branch[l],
                    w_out[l], norm2_g[l], w_up[l], w_down[l])
    return x2.reshape(batch, seq, d)
```

```python
import functools
import math

import numpy as np
import jax
import jax.numpy as jnp
from jax import lax
from jax.experimental import pallas as pl
from jax.experimental.pallas import tpu as pltpu

F32 = jnp.float32
BF16 = jnp.bfloat16
I32 = jnp.int32

D_MODEL = 2048
ATT_HEADS = 16
ATT_KV_HEADS = 4
HEAD_DIM = 128
REP = ATT_HEADS // ATT_KV_HEADS
IDX_HEADS = 16
IDX_DIM = 64
TOPK_MAX = 256
N_BUCKETS = 32
MAX_DISTANCE = 128
SSM_INNER = 2 * D_MODEL
SSM_HEAD_DIM = 64
SSM_HEADS = SSM_INNER // SSM_HEAD_DIM
SSM_GROUPS = 8
SSM_STATE = 128
CONV_WIDTH = 4
CHUNK = 128
MLP_HIDDEN = 4 * D_MODEL
EPS = 1e-6

ATT_Q = ATT_HEADS * HEAD_DIM
ATT_KV = ATT_KV_HEADS * HEAD_DIM
IDX_Q = IDX_HEADS * IDX_DIM
SSM_BC = SSM_GROUPS * SSM_STATE
CONV_DIM = SSM_INNER + 2 * SSM_BC
SPLITS = (D_MODEL, D_MODEL, ATT_Q, ATT_KV, ATT_KV, IDX_Q, IDX_DIM, IDX_HEADS, SSM_INNER, CONV_DIM, SSM_HEADS)
_OFFS = tuple(int(v) for v in np.cumsum((0,) + SPLITS))
(O_GA, O_GS, O_Q, O_K, O_V, O_QI, O_KI, O_WI, O_Z, O_XBC, O_DT, _O_END) = _OFFS

M_GA = 0
M_GS = M_GA + D_MODEL
M_Q = M_GS + D_MODEL
M_K = M_Q + ATT_Q
M_V = M_K + ATT_KV
M_QI = M_V + ATT_KV
M_Z = M_QI + IDX_Q
M_XBC = M_Z + SSM_INNER
M_END = M_XBC + CONV_DIM
S_KI = 0
S_WI = S_KI + IDX_DIM
S_DT = S_WI + IDX_HEADS
S_END = 256

HEADS_PER_GROUP = SSM_HEADS // SSM_GROUPS
GROUP_W = HEADS_PER_GROUP * SSM_HEAD_DIM

LANES = 128
VMEM_LIMIT = 48 * 1024 * 1024

NEG_INF = float("-inf")
INT_MIN = -(2 ** 31)
KEY_NEG_INF = int(np.int32(np.uint32(0xFF800000) ^ np.uint32(0x7FFFFFFF)))


def _dot(a, b):
    return jnp.dot(a, b, preferred_element_type=F32)


def _dot_nt(a, b):
    return lax.dot_general(a, b, (((1,), (1,)), ((), ())), preferred_element_type=F32)


def _split3(x):
    hi = x.astype(BF16)
    r = x - hi.astype(F32)
    mid = r.astype(BF16)
    lo = (r - mid.astype(F32)).astype(BF16)
    return hi, mid, lo


def _sigmoid(x):
    return 1.0 / (1.0 + jnp.exp(-x))


def _params(sem):
    return pltpu.CompilerParams(dimension_semantics=sem, vmem_limit_bytes=VMEM_LIMIT)


def _norm_matmul_kernel(x_ref, g_ref, w_ref, o_ref, xn_ref):
    @pl.when(pl.program_id(1) == 0)
    def _():
        x = x_ref[...]
        ms = jnp.mean(x * x, axis=-1, keepdims=True)
        xn_ref[...] = (x * lax.rsqrt(ms + EPS) * g_ref[...]).astype(BF16)

    o_ref[...] = _dot(xn_ref[...], w_ref[...]).astype(o_ref.dtype)


def _norm_matmul(x, g, w, out_dtype, tm, tn):
    m, d = x.shape
    n = w.shape[1]
    return pl.pallas_call(
        _norm_matmul_kernel,
        grid=(m // tm, n // tn),
        in_specs=[pl.BlockSpec((tm, d), lambda i, j: (i, 0)),
                  pl.BlockSpec((1, d), lambda i, j: (0, 0)),
                  pl.BlockSpec((d, tn), lambda i, j: (0, j))],
        out_specs=pl.BlockSpec((tm, tn), lambda i, j: (i, j)),
        out_shape=jax.ShapeDtypeStruct((m, n), out_dtype),
        scratch_shapes=[pltpu.VMEM((tm, d), BF16)],
        compiler_params=_params(("parallel", "arbitrary")),
        name="norm_in_proj",
    )(x, g, w)


LOG2E = math.log2(math.e)


def _qk_prep_kernel(q_ref, k_ref, v_ref, qi_ref, sm_ref, qg_ref, kg_ref,
                    qt_ref, kn_ref, vt_ref, qit_ref, wt_ref, ki_ref):
    qg = qg_ref[...]
    for h in range(ATT_HEADS):
        x = q_ref[:, h * HEAD_DIM:(h + 1) * HEAD_DIM].astype(F32)
        ms = jnp.mean(x * x, axis=-1, keepdims=True)
        y = x * lax.rsqrt(ms + EPS) * qg * (HEAD_DIM ** -0.5 * LOG2E)
        qt_ref[0, h] = y.T.astype(BF16)
    kg = kg_ref[...]
    for h in range(ATT_KV_HEADS):
        x = k_ref[:, h * HEAD_DIM:(h + 1) * HEAD_DIM].astype(F32)
        ms = jnp.mean(x * x, axis=-1, keepdims=True)
        kn_ref[:, h * HEAD_DIM:(h + 1) * HEAD_DIM] = (x * lax.rsqrt(ms + EPS) * kg).astype(BF16)
        vt_ref[0, h, 0] = v_ref[:, h * HEAD_DIM:(h + 1) * HEAD_DIM].astype(F32).T.astype(BF16)
    for p in range(IDX_Q // LANES):
        qit_ref[0, p * LANES:(p + 1) * LANES, :] = qi_ref[:, p * LANES:(p + 1) * LANES].astype(F32).T.astype(BF16)
    sm_t = sm_ref[:, 0:LANES].T
    wt_ref[0] = sm_t[S_WI:S_WI + IDX_HEADS, :] * (IDX_HEADS ** -0.5 * IDX_DIM ** -0.5)
    ki_ref[...] = sm_ref[:, S_KI:S_KI + IDX_DIM].astype(BF16)


def _qk_prep(main, small, qg, kg, batch, seq, tq):
    t = main.shape[0]
    nq = seq // tq
    return pl.pallas_call(
        _qk_prep_kernel,
        grid=(batch, nq),
        in_specs=[pl.BlockSpec((tq, ATT_Q), lambda b, i: (b * nq + i, M_Q // ATT_Q)),
                  pl.BlockSpec((tq, ATT_KV), lambda b, i: (b * nq + i, M_K // ATT_KV)),
                  pl.BlockSpec((tq, ATT_KV), lambda b, i: (b * nq + i, M_V // ATT_KV)),
                  pl.BlockSpec((tq, IDX_Q), lambda b, i: (b * nq + i, M_QI // IDX_Q)),
                  pl.BlockSpec((tq, S_END), lambda b, i: (b * nq + i, 0)),
                  pl.BlockSpec((1, HEAD_DIM), lambda b, i: (0, 0)),
                  pl.BlockSpec((1, HEAD_DIM), lambda b, i: (0, 0))],
        out_specs=[pl.BlockSpec((1, ATT_HEADS, HEAD_DIM, tq), lambda b, i: (b * nq + i, 0, 0, 0)),
                   pl.BlockSpec((tq, ATT_KV), lambda b, i: (b * nq + i, 0)),
                   pl.BlockSpec((1, ATT_KV_HEADS, 1, HEAD_DIM, tq), lambda b, i: (b, 0, i, 0, 0)),
                   pl.BlockSpec((1, IDX_Q, tq), lambda b, i: (b * nq + i, 0, 0)),
                   pl.BlockSpec((1, IDX_HEADS, tq), lambda b, i: (b * nq + i, 0, 0)),
                   pl.BlockSpec((tq, IDX_DIM), lambda b, i: (b * nq + i, 0))],
        out_shape=[jax.ShapeDtypeStruct((t // tq, ATT_HEADS, HEAD_DIM, tq), BF16),
                   jax.ShapeDtypeStruct((t, ATT_KV), BF16),
                   jax.ShapeDtypeStruct((batch, ATT_KV_HEADS, nq, HEAD_DIM, tq), BF16),
                   jax.ShapeDtypeStruct((t // tq, IDX_Q, tq), BF16),
                   jax.ShapeDtypeStruct((t // tq, IDX_HEADS, tq), F32),
                   jax.ShapeDtypeStruct((t, IDX_DIM), BF16)],
        compiler_params=_params(("parallel", "parallel")),
        name="qk_prep",
    )(main, main, main, main, small, qg, kg)


DSA_TQ = 256
DSA_TK = 256


def _t5_bucket_np(dist):
    n = np.maximum(dist, 0)
    max_exact = N_BUCKETS // 2
    nf = np.maximum(n, 1).astype(np.float32)
    ratio = (np.log(nf / np.float32(max_exact)) / np.float32(math.log(MAX_DISTANCE / max_exact))
             * np.float32(N_BUCKETS - max_exact))
    large = max_exact + ratio.astype(np.int32)
    large = np.minimum(large, N_BUCKETS - 1)
    return np.where(n < max_exact, n, large).astype(np.int32)


def _bias_bucket_tiles(tq, tk):
    r = np.arange(tq)[None, :]
    c = np.arange(tk)[:, None]
    d0 = _t5_bucket_np(r - c)
    d1 = _t5_bucket_np(tk + r - c)
    assert np.all(_t5_bucket_np(np.arange(tk + 1, 8 * tk)) == N_BUCKETS - 1)
    return np.stack([d0, d1]).astype(np.int32)


def _dsa_kernel(relb_ref, bidx_ref, qt_ref, qit_ref, wt_ref, k_ref, vt_ref, ki_ref, o_ref,
                key_ref, madd_ref, bias_ref, *, tq, tk, nkc, topk):
    b = pl.program_id(0)
    i = pl.program_id(1)
    g = pl.program_id(2)
    neg_slot = nkc

    @pl.when((b == 0) & (i == 0) & (g == 0))
    def _init():
        madd_ref[neg_slot] = jnp.full((tk, tq), NEG_INF, F32)
        for t in range(2):
            bt = bidx_ref[t]

            def head_body(h, carry):
                far = relb_ref[N_BUCKETS - 1, h]

                def bucket_body(bk, acc):
                    return jnp.where(bt == bk, (relb_ref[bk, h] - far) * LOG2E, acc)

                bias_ref[t, h] = lax.fori_loop(0, N_BUCKETS, bucket_body, jnp.zeros((tk, tq), F32))
                return carry

            lax.fori_loop(0, ATT_HEADS, head_body, 0)

    @pl.when(g == 0)
    def _select():
        nj = i + 1
        qpos = i * tq + lax.broadcasted_iota(I32, (tk, tq), 1)

        def score_chunk(j, carry):
            kc = ki_ref[pl.ds(pl.multiple_of(j * tk, tk), tk), :]
            acc = jnp.zeros((tk, tq), F32)
            for h in range(IDX_HEADS):
                z = _dot(kc, qit_ref[0, h * IDX_DIM:(h + 1) * IDX_DIM, :])
                acc = acc + jnp.maximum(z, 0.0) * wt_ref[0, h:h + 1, :]
            kpos = j * tk + lax.broadcasted_iota(I32, (tk, tq), 0)
            acc = jnp.where(kpos <= qpos, acc, NEG_INF)
            bits = pltpu.bitcast(acc, I32)
            key_ref[j] = bits ^ ((bits >> 31) & 0x7FFFFFFF)
            return carry

        lax.fori_loop(0, nj, score_chunk, 0)

        def bit_iter(it, prefix):
            cand = prefix + lax.shift_left(jnp.int32(1), 31 - it)

            def cnt_chunk(j, cnt):
                hit = jnp.where(key_ref[j] >= cand, 1.0, 0.0)
                return cnt + jnp.sum(hit.reshape(tk // 8, 8, tq), axis=0)

            cnt = lax.fori_loop(0, nj, cnt_chunk, jnp.zeros((8, tq), F32))
            tot = jnp.sum(cnt, axis=0, keepdims=True)
            return jnp.where(tot >= float(topk), cand, prefix)

        thr = lax.fori_loop(0, 32, bit_iter, jnp.full((1, tq), INT_MIN, I32))
        thr = jnp.maximum(thr, KEY_NEG_INF + 1)

        def madd_chunk(j, carry):
            madd_ref[j] = jnp.where(key_ref[j] >= thr, 0.0, NEG_INF)
            return carry

        lax.fori_loop(0, nj, madd_chunk, 0)

    qt = jnp.concatenate([qt_ref[0, r] for r in range(REP)], axis=1)
    c0 = jnp.maximum(i - 1, 0)

    def attn_step(k_c, vt_c, madd_c, bias_c, m, l, acc):
        s = _dot(k_c, qt)
        parts = []
        for r in range(REP):
            add = madd_c if bias_c is None else madd_c + bias_c[r]
            parts.append(s[:, r * tq:(r + 1) * tq] + add)
        s = jnp.concatenate(parts, axis=1)
        m_new = jnp.maximum(m, jnp.max(s, axis=0, keepdims=True))
        m_safe = jnp.where(m_new == NEG_INF, 0.0, m_new)
        alpha = jnp.exp2(m - m_safe)
        p = jnp.exp2(s - m_safe)
        l = alpha * l + jnp.sum(p, axis=0, keepdims=True)
        acc = alpha * acc + _dot(vt_c, p.astype(BF16))
        return m_new, l, acc

    m0 = jnp.full((1, REP * tq), NEG_INF, F32)
    l0 = jnp.zeros((1, REP * tq), F32)
    a0 = jnp.zeros((HEAD_DIM, REP * tq), F32)

    first = i == 0
    t_a = jnp.where(first, 0, 1)
    c1 = jnp.minimum(c0 + 1, nkc - 1)
    idx_b = jnp.where(first, neg_slot, c0 + 1)
    rows = pl.ds(pl.multiple_of(c0 * tk, tk), 2 * tk)
    madd_near = jnp.concatenate([madd_ref[c0], madd_ref[idx_b]], axis=0)
    bias_near = [jnp.concatenate([bias_ref[t_a, g * REP + r], bias_ref[0, g * REP + r]], axis=0)
                 for r in range(REP)]
    vt_near = jnp.concatenate([vt_ref[0, 0, c0], vt_ref[0, 0, c1]], axis=1)
    m1, l1, a1 = attn_step(k_ref[rows, :], vt_near, madd_near, bias_near, m0, l0, a0)

    def far_body(jb, carry):
        m, l, acc = carry
        ca = 2 * jb
        cb = jnp.where(ca + 1 >= c0, neg_slot, ca + 1)
        rows_f = pl.ds(pl.multiple_of(ca * tk, 2 * tk), 2 * tk)
        madd_f = jnp.concatenate([madd_ref[ca], madd_ref[cb]], axis=0)
        vt_f = jnp.concatenate([vt_ref[0, 0, ca], vt_ref[0, 0, ca + 1]], axis=1)
        return attn_step(k_ref[rows_f, :], vt_f, madd_f, None, m, l, acc)

    m2, l2, a2 = lax.fori_loop(0, (c0 + 1) // 2, far_body, (m1, l1, a1))
    out = a2 / l2
    for r in range(REP):
        o_ref[:, r * HEAD_DIM:(r + 1) * HEAD_DIM] = out[:, r * tq:(r + 1) * tq].T.astype(o_ref.dtype)


def _dsa(qt, kn, vt, qit, wt, ki, rel_bias, batch, seq):
    tq, tk = DSA_TQ, DSA_TK
    nq = seq // tq
    nkc = seq // tk
    topk = min(TOPK_MAX, seq // 4)
    bidx = jnp.asarray(_bias_bucket_tiles(tq, tk))
    kern = functools.partial(_dsa_kernel, tq=tq, tk=tk, nkc=nkc, topk=topk)
    gw = REP * HEAD_DIM
    return pl.pallas_call(
        kern,
        grid=(batch, nq, ATT_KV_HEADS),
        in_specs=[pl.BlockSpec(memory_space=pltpu.SMEM),
                  pl.BlockSpec((2, tk, tq), lambda b, i, g: (0, 0, 0)),
                  pl.BlockSpec((1, REP, HEAD_DIM, tq), lambda b, i, g: (b * nq + i, g, 0, 0)),
                  pl.BlockSpec((1, IDX_Q, tq), lambda b, i, g: (b * nq + i, 0, 0)),
                  pl.BlockSpec((1, IDX_HEADS, tq), lambda b, i, g: (b * nq + i, 0, 0)),
                  pl.BlockSpec((seq, HEAD_DIM), lambda b, i, g: (b, g)),
                  pl.BlockSpec((1, 1, nkc, HEAD_DIM, tk), lambda b, i, g: (b, g, 0, 0, 0)),
                  pl.BlockSpec((seq, IDX_DIM), lambda b, i, g: (b, 0))],
        out_specs=pl.BlockSpec((tq, gw), lambda b, i, g: (b * nq + i, g)),
        out_shape=jax.ShapeDtypeStruct((batch * seq, ATT_Q), BF16),
        scratch_shapes=[pltpu.VMEM((nkc, tk, tq), I32),
                        pltpu.VMEM((nkc + 1, tk, tq), F32),
                        pltpu.VMEM((2, ATT_HEADS, tk, tq), F32)],
        compiler_params=_params(("arbitrary", "arbitrary", "arbitrary")),
        name="dsa_attention",
    )(rel_bias, bidx, qt, qit, wt, kn, vt, ki)


E_ROWS = 3 * CHUNK + 16
CONV_HALO = 16


def _conv_shift_matrix():
    s = np.zeros((CONV_WIDTH * CHUNK, CONV_HALO + CHUNK), np.float32)
    for k in range(CONV_WIDTH):
        t = np.arange(CHUNK)
        s[k * CHUNK + t, CONV_HALO + t - (CONV_WIDTH - 1) + k] = 1.0
    return s


def _ssd_kernel(xbc_ref, halo_ref, z_ref, sm_ref, shift_ref, cw_ref, cb_ref, dtb_ref, alog_ref, dsk_ref, ng_ref,
                o_ref, xs_ref, bm_ref, cm_ref, state_ref, ypre_ref, acg_ref, actg_ref):
    c = pl.program_id(1)
    L = CHUNK

    @pl.when(c == 0)
    def _():
        state_ref[...] = jnp.zeros_like(state_ref)

    halo_on = (c > 0).astype(BF16)
    shift = shift_ref[...]
    cblk = GROUP_W
    for cbi in range(CONV_DIM // cblk):
        cols = slice(cbi * cblk, (cbi + 1) * cblk)
        ext = jnp.concatenate([halo_ref[:, cols] * halo_on, xbc_ref[:, cols]], axis=0)
        taps = _dot(shift, ext)
        acc = cb_ref[:, cols]
        for kk in range(CONV_WIDTH):
            acc = acc + cw_ref[kk:kk + 1, cols] * taps[kk * L:(kk + 1) * L]
        y = acc * _sigmoid(acc)
        if cbi < SSM_GROUPS:
            xs_ref[cbi] = y
        else:
            per = cblk // SSM_STATE
            for u in range(per):
                gi = (cbi - SSM_GROUPS) * per + u
                piece = y[:, u * SSM_STATE:(u + 1) * SSM_STATE]
                if gi < SSM_GROUPS:
                    bm_ref[gi] = piece
                else:
                    cm_ref[gi - SSM_GROUPS] = piece

    dt_in = sm_ref[:, S_DT:S_DT + SSM_HEADS] + dtb_ref[...]
    dt_act = jnp.maximum(dt_in, 0.0) + jnp.log1p(jnp.exp(-jnp.abs(dt_in)))
    a = dt_act * (-jnp.exp(alog_ref[...]))
    ri = lax.broadcasted_iota(I32, (L, L), 0)
    ci = lax.broadcasted_iota(I32, (L, L), 1)
    tril = ri >= ci
    tri_b = jnp.where(tril, 1.0, 0.0).astype(BF16)
    a_cum = sum(_dot(tri_b, p) for p in _split3(a))
    eye_b = jnp.where(lax.broadcasted_iota(I32, (SSM_HEADS, SSM_HEADS), 0)
                      == lax.broadcasted_iota(I32, (SSM_HEADS, SSM_HEADS), 1), 1.0, 0.0).astype(BF16)
    a_cum_t = sum(_dot_nt(eye_b, p) for p in _split3(a_cum))
    a_last = a_cum[L - 1:L, :]
    for gi in range(SSM_GROUPS):
        acg_ref[gi] = a_cum[:, gi * HEADS_PER_GROUP:(gi + 1) * HEADS_PER_GROUP]
        actg_ref[gi] = a_cum_t[gi * HEADS_PER_GROUP:(gi + 1) * HEADS_PER_GROUP, :]
    cd3 = _split3(jnp.exp(a_last))
    ds3 = _split3(dsk_ref[...])
    extras = jnp.concatenate([p.astype(F32) for p in cd3 + ds3]
                             + [jnp.zeros((E_ROWS - 3 * L - 6, SSM_HEADS), F32)], axis=0)
    e_mat = jnp.concatenate([dt_act, jnp.exp(a_cum), jnp.exp(a_last - a_cum), extras], axis=0).astype(BF16)

    lane = lax.broadcasted_iota(I32, (L, LANES), 1)
    lo_mask = lane < SSM_HEAD_DIM

    def group_body(gi, carry):
        xs = xs_ref[gi]
        bg = bm_ref[gi]
        cg_b = cm_ref[gi].astype(BF16)
        hsel = (lax.broadcasted_iota(I32, (SSM_HEADS, GROUP_W), 0)
                == gi * HEADS_PER_GROUP + lax.broadcasted_iota(I32, (SSM_HEADS, GROUP_W), 1) // SSM_HEAD_DIM)
        ex = _dot(e_mat, jnp.where(hsel, 1.0, 0.0).astype(BF16))
        dt_rep = ex[0:L]
        expa_rep = ex[L:2 * L]
        dte_rep = ex[2 * L:3 * L]
        cd_rep = ex[3 * L:3 * L + 1] + ex[3 * L + 1:3 * L + 2] + ex[3 * L + 2:3 * L + 3]
        dsk_rep = ex[3 * L + 3:3 * L + 4] + ex[3 * L + 4:3 * L + 5] + ex[3 * L + 5:3 * L + 6]

        xd = xs * dt_rep
        xd_b = xd.astype(BF16)
        cb = _dot_nt(cg_b, bg.astype(BF16))
        acg = acg_ref[gi]
        actg = actg_ref[gi]
        pairs = []
        for pj in range(HEADS_PER_GROUP // 2):
            gmat = []
            for e in (2 * pj, 2 * pj + 1):
                seg = acg[:, e:e + 1] - actg[e:e + 1, :]
                dec = jnp.exp(jnp.where(tril, seg, NEG_INF))
                gmat.append((cb * dec).astype(BF16))
            xp = xd_b[:, pj * LANES:(pj + 1) * LANES]
            zero = jnp.zeros_like(xp)
            pairs.append(_dot(gmat[0], jnp.where(lo_mask, xp, zero))
                         + _dot(gmat[1], jnp.where(lo_mask, zero, xp)))
        y_diag = jnp.concatenate(pairs, axis=1)

        st = state_ref[gi]
        y_off = _dot(cg_b, st.astype(BF16)) * expa_rep
        xdd = (xd * dte_rep).astype(BF16)
        state_ref[gi] = st * cd_rep + _dot(bg.T.astype(BF16), xdd)
        ypre_ref[gi] = y_diag + y_off + dsk_rep * xs
        return carry

    lax.fori_loop(0, SSM_GROUPS, group_body, 0, unroll=2)

    for gi in range(SSM_GROUPS):
        cols = slice(gi * GROUP_W, (gi + 1) * GROUP_W)
        zz = z_ref[:, cols].astype(F32)
        y = ypre_ref[gi] * (zz * _sigmoid(zz))
        ms = jnp.mean(y * y, axis=-1, keepdims=True)
        o_ref[:, cols] = (y * lax.rsqrt(ms + EPS) * ng_ref[:, cols]).astype(o_ref.dtype)


def _ssd(main, small, conv_w, conv_b, dt_bias, a_log, d_skip, norm_g, batch, seq):
    nc = seq // CHUNK
    hb = CHUNK // CONV_HALO
    row = lambda b, c: b * nc + c
    full = lambda shape: pl.BlockSpec(shape, lambda b, c: (0,) * len(shape))
    shift = jnp.asarray(_conv_shift_matrix(), BF16)
    return pl.pallas_call(
        _ssd_kernel,
        grid=(batch, nc),
        in_specs=[pl.BlockSpec((CHUNK, CONV_DIM), lambda b, c: (row(b, c), M_XBC // CONV_DIM)),
                  pl.BlockSpec((CONV_HALO, CONV_DIM),
                               lambda b, c: (jnp.maximum(row(b, c) * hb - 1, 0), M_XBC // CONV_DIM)),
                  pl.BlockSpec((CHUNK, SSM_INNER), lambda b, c: (row(b, c), M_Z // SSM_INNER)),
                  pl.BlockSpec((CHUNK, S_END), lambda b, c: (row(b, c), 0)),
                  full((CONV_WIDTH * CHUNK, CONV_HALO + CHUNK)),
                  full((CONV_WIDTH, CONV_DIM)), full((1, CONV_DIM)), full((1, SSM_HEADS)),
                  full((1, SSM_HEADS)), full((1, SSM_HEADS)), full((1, SSM_INNER))],
        out_specs=pl.BlockSpec((CHUNK, SSM_INNER), lambda b, c: (row(b, c), 0)),
        out_shape=jax.ShapeDtypeStruct((batch * seq, SSM_INNER), BF16),
        scratch_shapes=[pltpu.VMEM((SSM_GROUPS, CHUNK, GROUP_W), F32),
                        pltpu.VMEM((SSM_GROUPS, CHUNK, SSM_STATE), F32),
                        pltpu.VMEM((SSM_GROUPS, CHUNK, SSM_STATE), F32),
                        pltpu.VMEM((SSM_GROUPS, SSM_STATE, GROUP_W), F32),
                        pltpu.VMEM((SSM_GROUPS, CHUNK, GROUP_W), F32),
                        pltpu.VMEM((SSM_GROUPS, CHUNK, HEADS_PER_GROUP), F32),
                        pltpu.VMEM((SSM_GROUPS, HEADS_PER_GROUP, CHUNK), F32)],
        compiler_params=_params(("arbitrary", "arbitrary")),
        name="ssd_scan",
    )(main, main, main, small, shift, conv_w, conv_b, dt_bias, a_log, d_skip, norm_g)


def _merge_kernel(att_ref, y_ref, ga_ref, gs_ref, wa_ref, ws_ref, o_ref):
    pa = _dot(att_ref[...], wa_ref[...])
    ps = _dot(y_ref[...], ws_ref[...])
    o_ref[...] = (_sigmoid(ga_ref[...].astype(F32)) * pa + _sigmoid(gs_ref[...].astype(F32)) * ps).astype(o_ref.dtype)


def _merge(att, y, main, wa, ws, tm, tn):
    t = att.shape[0]
    return pl.pallas_call(
        _merge_kernel,
        grid=(t // tm, D_MODEL // tn),
        in_specs=[pl.BlockSpec((tm, ATT_Q), lambda i, j: (i, 0)),
                  pl.BlockSpec((tm, SSM_INNER), lambda i, j: (i, 0)),
                  pl.BlockSpec((tm, tn), lambda i, j: (i, M_GA // tn + j)),
                  pl.BlockSpec((tm, tn), lambda i, j: (i, M_GS // tn + j)),
                  pl.BlockSpec((ATT_Q, tn), lambda i, j: (0, j)),
                  pl.BlockSpec((SSM_INNER, tn), lambda i, j: (0, j))],
        out_specs=pl.BlockSpec((tm, tn), lambda i, j: (i, j)),
        out_shape=jax.ShapeDtypeStruct((t, D_MODEL), BF16),
        compiler_params=_params(("parallel", "arbitrary")),
        name="gated_merge",
    )(att, y, main, main, wa, ws)


def _out_proj_kernel(m_ref, w_ref, x_ref, g_ref, x1_ref, h2_ref):
    x1 = x_ref[...] + _dot(m_ref[...], w_ref[...])
    x1_ref[...] = x1
    ms = jnp.mean(x1 * x1, axis=-1, keepdims=True)
    h2_ref[...] = (x1 * lax.rsqrt(ms + EPS) * g_ref[...]).astype(BF16)


def _out_proj(merged, w, x, g, tm):
    t = x.shape[0]
    return pl.pallas_call(
        _out_proj_kernel,
        grid=(t // tm,),
        in_specs=[pl.BlockSpec((tm, D_MODEL), lambda i: (i, 0)),
                  pl.BlockSpec((D_MODEL, D_MODEL), lambda i: (0, 0)),
                  pl.BlockSpec((tm, D_MODEL), lambda i: (i, 0)),
                  pl.BlockSpec((1, D_MODEL), lambda i: (0, 0))],
        out_specs=[pl.BlockSpec((tm, D_MODEL), lambda i: (i, 0)),
                   pl.BlockSpec((tm, D_MODEL), lambda i: (i, 0))],
        out_shape=[jax.ShapeDtypeStruct((t, D_MODEL), F32),
                   jax.ShapeDtypeStruct((t, D_MODEL), BF16)],
        compiler_params=_params(("parallel",)),
        name="out_proj_norm",
    )(merged, w, x, g)


def _mlp_kernel(h_ref, x1_ref, wu_ref, wd_ref, o_ref):
    @pl.when(pl.program_id(1) == 0)
    def _():
        o_ref[...] = x1_ref[...]

    u = _dot(h_ref[...], wu_ref[...])
    u = jnp.square(jnp.maximum(u, 0.0)).astype(BF16)
    o_ref[...] += _dot(u, wd_ref[...])


def _mlp(h2, x1, wu, wd, tm, th):
    t = h2.shape[0]
    return pl.pallas_call(
        _mlp_kernel,
        grid=(t // tm, MLP_HIDDEN // th),
        in_specs=[pl.BlockSpec((tm, D_MODEL), lambda i, j: (i, 0)),
                  pl.BlockSpec((tm, D_MODEL), lambda i, j: (i, 0)),
                  pl.BlockSpec((D_MODEL, th), lambda i, j: (0, j)),
                  pl.BlockSpec((th, D_MODEL), lambda i, j: (j, 0))],
        out_specs=pl.BlockSpec((tm, D_MODEL), lambda i, j: (i, 0)),
        out_shape=jax.ShapeDtypeStruct((t, D_MODEL), F32),
        compiler_params=_params(("parallel", "arbitrary")),
        name="relu2_mlp",
    )(h2, x1, wu, wd)


def _pack_w_in(w):
    seg = lambda o, n: w[:, o:o + n]
    main = jnp.concatenate([seg(O_GA, D_MODEL), seg(O_GS, D_MODEL), seg(O_Q, ATT_Q), seg(O_K, ATT_KV),
                            seg(O_V, ATT_KV), seg(O_QI, IDX_Q), seg(O_Z, SSM_INNER), seg(O_XBC, CONV_DIM)],
                           axis=1).astype(BF16)
    small = jnp.concatenate([seg(O_KI, IDX_DIM), seg(O_WI, IDX_HEADS), seg(O_DT, SSM_HEADS),
                             jnp.zeros((w.shape[0], S_END - S_DT - SSM_HEADS), w.dtype)], axis=1).astype(BF16)
    return main, small


def _block(x2, batch, seq, norm1_g, w_in, conv_w, conv_b, dt_bias, a_log, d_skip, ssm_norm_g, q_norm_g,
           k_norm_g, rel_bias, w_att_branch, w_ssm_branch, w_out, norm2_g, w_up, w_down):
    row = lambda v: v.reshape(1, -1)
    w_main, w_small = _pack_w_in(w_in)
    main = _norm_matmul(x2, row(norm1_g), w_main, BF16, tm=min(1024, x2.shape[0]), tn=512)
    small = _norm_matmul(x2, row(norm1_g), w_small, F32, tm=min(1024, x2.shape[0]), tn=S_END)
    qt, kn, vt, qit, wt, ki = _qk_prep(main, small, row(q_norm_g), row(k_norm_g), batch, seq, tq=DSA_TQ)
    att = _dsa(qt, kn, vt, qit, wt, ki, rel_bias, batch, seq)
    y = _ssd(main, small, conv_w, row(conv_b), row(dt_bias), row(a_log), row(d_skip), row(ssm_norm_g),
             batch, seq)
    merged = _merge(att, y, main, w_att_branch.astype(BF16), w_ssm_branch.astype(BF16), tm=512, tn=512)
    x1, h2 = _out_proj(merged, w_out.astype(BF16), x2, row(norm2_g), tm=512)
    return _mlp(h2, x1, w_up.astype(BF16), w_down.astype(BF16), tm=512, th=512)


def kernel(x, norm1_g, w_in, conv_w, conv_b, dt_bias, a_log, d_skip, ssm_norm_g, q_norm_g, k_norm_g, rel_bias,
           w_att_branch, w_ssm_branch, w_out, norm2_g, w_up, w_down):
    batch, seq, d = x.shape
    x2 = x.reshape(batch * seq, d)
    for l in range(norm1_g.shape[0]):
        x2 = _block(x2, batch, seq, norm1_g[l], w_in[l], conv_w[l], conv_b[l], dt_bias[l], a_log[l], d_skip[l],
                    ssm_norm_g[l], q_norm_g[l], k_norm_g[l], rel_bias, w_att_branch[l], w_ssm_branch[l],
                    w_out[l], norm2_g[l], w_up[l], w_down[l])
    return x2.reshape(batch, seq, d)
```

```python
import functools
import math

import numpy as np
import jax
import jax.numpy as jnp
from jax import lax
from jax.experimental import pallas as pl
from jax.experimental.pallas import tpu as pltpu

F32 = jnp.float32
BF16 = jnp.bfloat16
I32 = jnp.int32

D_MODEL = 2048
ATT_HEADS = 16
ATT_KV_HEADS = 4
HEAD_DIM = 128
REP = ATT_HEADS // ATT_KV_HEADS
IDX_HEADS = 16
IDX_DIM = 64
TOPK_MAX = 256
N_BUCKETS = 32
MAX_DISTANCE = 128
SSM_INNER = 2 * D_MODEL
SSM_HEAD_DIM = 64
SSM_HEADS = SSM_INNER // SSM_HEAD_DIM
SSM_GROUPS = 8
SSM_STATE = 128
CONV_WIDTH = 4
CHUNK = 128
MLP_HIDDEN = 4 * D_MODEL
EPS = 1e-6

ATT_Q = ATT_HEADS * HEAD_DIM
ATT_KV = ATT_KV_HEADS * HEAD_DIM
IDX_Q = IDX_HEADS * IDX_DIM
SSM_BC = SSM_GROUPS * SSM_STATE
CONV_DIM = SSM_INNER + 2 * SSM_BC
SPLITS = (D_MODEL, D_MODEL, ATT_Q, ATT_KV, ATT_KV, IDX_Q, IDX_DIM, IDX_HEADS, SSM_INNER, CONV_DIM, SSM_HEADS)
_OFFS = tuple(int(v) for v in np.cumsum((0,) + SPLITS))
(O_GA, O_GS, O_Q, O_K, O_V, O_QI, O_KI, O_WI, O_Z, O_XBC, O_DT, _O_END) = _OFFS

M_GA = 0
M_GS = M_GA + D_MODEL
M_Q = M_GS + D_MODEL
M_K = M_Q + ATT_Q
M_V = M_K + ATT_KV
M_QI = M_V + ATT_KV
M_Z = M_QI + IDX_Q
M_XBC = M_Z + SSM_INNER
M_END = M_XBC + CONV_DIM
S_KI = 0
S_WI = S_KI + IDX_DIM
S_DT = S_WI + IDX_HEADS
S_END = 256

HEADS_PER_GROUP = SSM_HEADS // SSM_GROUPS
GROUP_W = HEADS_PER_GROUP * SSM_HEAD_DIM

LANES = 128
VMEM_LIMIT = 56 * 1024 * 1024

NEG_INF = float("-inf")
INT_MIN = -(2 ** 31)
KEY_NEG_INF = int(np.int32(np.uint32(0xFF800000) ^ np.uint32(0x7FFFFFFF)))


def _dot(a, b):
    return jnp.dot(a, b, preferred_element_type=F32)


def _dot_nt(a, b):
    return lax.dot_general(a, b, (((1,), (1,)), ((), ())), preferred_element_type=F32)


def _split3(x):
    hi = x.astype(BF16)
    r = x - hi.astype(F32)
    mid = r.astype(BF16)
    lo = (r - mid.astype(F32)).astype(BF16)
    return hi, mid, lo


def _sigmoid(x):
    return 1.0 / (1.0 + jnp.exp(-x))


def _params(sem):
    return pltpu.CompilerParams(dimension_semantics=sem, vmem_limit_bytes=VMEM_LIMIT)


def _norm_matmul_kernel(x_ref, g_ref, w_ref, o_ref, xn_ref):
    @pl.when(pl.program_id(1) == 0)
    def _():
        x = x_ref[...]
        ms = jnp.mean(x * x, axis=-1, keepdims=True)
        xn_ref[...] = (x * lax.rsqrt(ms + EPS) * g_ref[...]).astype(BF16)

    o_ref[...] = _dot(xn_ref[...], w_ref[...]).astype(o_ref.dtype)


def _norm_matmul(x, g, w, out_dtype, tm, tn):
    m, d = x.shape
    n = w.shape[1]
    return pl.pallas_call(
        _norm_matmul_kernel,
        grid=(m // tm, n // tn),
        in_specs=[pl.BlockSpec((tm, d), lambda i, j: (i, 0)),
                  pl.BlockSpec((1, d), lambda i, j: (0, 0)),
                  pl.BlockSpec((d, tn), lambda i, j: (0, j))],
        out_specs=pl.BlockSpec((tm, tn), lambda i, j: (i, j)),
        out_shape=jax.ShapeDtypeStruct((m, n), out_dtype),
        scratch_shapes=[pltpu.VMEM((tm, d), BF16)],
        compiler_params=_params(("parallel", "arbitrary")),
        name="norm_in_proj",
    )(x, g, w)


LOG2E = math.log2(math.e)


def _qk_prep_kernel(q_ref, k_ref, v_ref, qi_ref, sm_ref, qg_ref, kg_ref,
                    qt_ref, kn_ref, vt_ref, qit_ref, wt_ref, ki_ref):
    qg = qg_ref[...]
    for h in range(ATT_HEADS):
        x = q_ref[:, h * HEAD_DIM:(h + 1) * HEAD_DIM].astype(F32)
        ms = jnp.mean(x * x, axis=-1, keepdims=True)
        y = x * lax.rsqrt(ms + EPS) * qg * (HEAD_DIM ** -0.5 * LOG2E)
        qt_ref[0, h] = y.T.astype(BF16)
    kg = kg_ref[...]
    for h in range(ATT_KV_HEADS):
        x = k_ref[:, h * HEAD_DIM:(h + 1) * HEAD_DIM].astype(F32)
        ms = jnp.mean(x * x, axis=-1, keepdims=True)
        kn_ref[:, h * HEAD_DIM:(h + 1) * HEAD_DIM] = (x * lax.rsqrt(ms + EPS) * kg).astype(BF16)
        vt_ref[0, h, 0] = v_ref[:, h * HEAD_DIM:(h + 1) * HEAD_DIM].astype(F32).T.astype(BF16)
    for p in range(IDX_Q // LANES):
        qit_ref[0, p * LANES:(p + 1) * LANES, :] = qi_ref[:, p * LANES:(p + 1) * LANES].astype(F32).T.astype(BF16)
    sm_t = sm_ref[:, 0:LANES].T
    wt_ref[0] = sm_t[S_WI:S_WI + IDX_HEADS, :] * (IDX_HEADS ** -0.5 * IDX_DIM ** -0.5)
    ki_ref[...] = sm_ref[:, S_KI:S_KI + IDX_DIM].astype(BF16)


def _qk_prep(main, small, qg, kg, batch, seq, tq):
    t = main.shape[0]
    nq = seq // tq
    return pl.pallas_call(
        _qk_prep_kernel,
        grid=(batch, nq),
        in_specs=[pl.BlockSpec((tq, ATT_Q), lambda b, i: (b * nq + i, M_Q // ATT_Q)),
                  pl.BlockSpec((tq, ATT_KV), lambda b, i: (b * nq + i, M_K // ATT_KV)),
                  pl.BlockSpec((tq, ATT_KV), lambda b, i: (b * nq + i, M_V // ATT_KV)),
                  pl.BlockSpec((tq, IDX_Q), lambda b, i: (b * nq + i, M_QI // IDX_Q)),
                  pl.BlockSpec((tq, S_END), lambda b, i: (b * nq + i, 0)),
                  pl.BlockSpec((1, HEAD_DIM), lambda b, i: (0, 0)),
                  pl.BlockSpec((1, HEAD_DIM), lambda b, i: (0, 0))],
        out_specs=[pl.BlockSpec((1, ATT_HEADS, HEAD_DIM, tq), lambda b, i: (b * nq + i, 0, 0, 0)),
                   pl.BlockSpec((tq, ATT_KV), lambda b, i: (b * nq + i, 0)),
                   pl.BlockSpec((1, ATT_KV_HEADS, 1, HEAD_DIM, tq), lambda b, i: (b, 0, i, 0, 0)),
                   pl.BlockSpec((1, IDX_Q, tq), lambda b, i: (b * nq + i, 0, 0)),
                   pl.BlockSpec((1, IDX_HEADS, tq), lambda b, i: (b * nq + i, 0, 0)),
                   pl.BlockSpec((tq, IDX_DIM), lambda b, i: (b * nq + i, 0))],
        out_shape=[jax.ShapeDtypeStruct((t // tq, ATT_HEADS, HEAD_DIM, tq), BF16),
                   jax.ShapeDtypeStruct((t, ATT_KV), BF16),
                   jax.ShapeDtypeStruct((batch, ATT_KV_HEADS, nq, HEAD_DIM, tq), BF16),
                   jax.ShapeDtypeStruct((t // tq, IDX_Q, tq), BF16),
                   jax.ShapeDtypeStruct((t // tq, IDX_HEADS, tq), F32),
                   jax.ShapeDtypeStruct((t, IDX_DIM), BF16)],
        compiler_params=_params(("parallel", "parallel")),
        name="qk_prep",
    )(main, main, main, main, small, qg, kg)


DSA_TQ = 256
DSA_TK = 256
CNT_WAYS = 4


def _t5_bucket_np(dist):
    n = np.maximum(dist, 0)
    max_exact = N_BUCKETS // 2
    nf = np.maximum(n, 1).astype(np.float32)
    ratio = (np.log(nf / np.float32(max_exact)) / np.float32(math.log(MAX_DISTANCE / max_exact))
             * np.float32(N_BUCKETS - max_exact))
    large = max_exact + ratio.astype(np.int32)
    large = np.minimum(large, N_BUCKETS - 1)
    return np.where(n < max_exact, n, large).astype(np.int32)


def _bias_bucket_tiles(tq, tk):
    r = np.arange(tq)[None, :]
    c = np.arange(tk)[:, None]
    d0 = _t5_bucket_np(r - c)
    d1 = _t5_bucket_np(tk + r - c)
    assert np.all(_t5_bucket_np(np.arange(tk + 1, 8 * tk)) == N_BUCKETS - 1)
    return np.stack([d0, d1]).astype(np.int32)


def _dsa_kernel(relb_ref, bidx_ref, qt_ref, qit_ref, wt_ref, k_ref, vt_ref, ki_ref, o_ref,
                key_ref, madd_ref, bias_ref, s_buf, p_buf, st_ref, acc_ref, *, tq, tk, nkc, topk):
    b = pl.program_id(0)
    i = pl.program_id(1)
    g = pl.program_id(2)
    neg_slot = nkc

    @pl.when((b == 0) & (i == 0) & (g == 0))
    def _init():
        madd_ref[neg_slot] = jnp.full((tk, tq), NEG_INF, F32)
        for t in range(2):
            bt = bidx_ref[t]

            def head_body(h, carry):
                far = relb_ref[N_BUCKETS - 1, h]

                def bucket_body(bk, acc):
                    return jnp.where(bt == bk, (relb_ref[bk, h] - far) * LOG2E, acc)

                bias_ref[t, h] = lax.fori_loop(0, N_BUCKETS, bucket_body, jnp.zeros((tk, tq), F32))
                return carry

            lax.fori_loop(0, ATT_HEADS, head_body, 0)

    @pl.when(g == 0)
    def _select():
        nj = i + 1
        qpos = i * tq + lax.broadcasted_iota(I32, (tk, tq), 1)

        def score_chunk(j, carry):
            kc = ki_ref[pl.ds(pl.multiple_of(j * tk, tk), tk), :]
            acc = jnp.zeros((tk, tq), F32)
            for h in range(IDX_HEADS):
                z = _dot(kc, qit_ref[0, h * IDX_DIM:(h + 1) * IDX_DIM, :])
                acc = acc + jnp.maximum(z, 0.0) * wt_ref[0, h:h + 1, :]
            kpos = j * tk + lax.broadcasted_iota(I32, (tk, tq), 0)
            acc = jnp.where(kpos <= qpos, acc, NEG_INF)
            bits = pltpu.bitcast(acc, I32)
            key_ref[j] = bits ^ ((bits >> 31) & 0x7FFFFFFF)
            return carry

        lax.fori_loop(0, nj, score_chunk, 0)

        def bit_iter(it, prefix):
            cand = prefix + lax.shift_left(jnp.int32(1), 31 - it)

            def cnt_chunk(j, cnt):
                hit = jnp.where(key_ref[j] >= cand, 1.0, 0.0)
                return cnt + jnp.sum(hit.reshape(tk // (8 * CNT_WAYS), CNT_WAYS * 8, tq), axis=0)

            cnt = lax.fori_loop(0, nj, cnt_chunk, jnp.zeros((CNT_WAYS * 8, tq), F32))
            tot = jnp.sum(cnt, axis=0, keepdims=True)
            return jnp.where(tot >= float(topk), cand, prefix)

        thr = lax.fori_loop(0, 32, bit_iter, jnp.full((1, tq), INT_MIN, I32))
        thr = jnp.maximum(thr, KEY_NEG_INF + 1)

        def madd_chunk(j, carry):
            madd_ref[j] = jnp.where(key_ref[j] >= thr, 0.0, NEG_INF)
            return carry

        lax.fori_loop(0, nj, madd_chunk, 0)

    qt = jnp.concatenate([qt_ref[0, r] for r in range(REP)], axis=1)
    c0 = jnp.maximum(i - 1, 0)
    c1 = jnp.minimum(c0 + 1, nkc - 1)
    n_far = (c0 + 1) // 2

    def scores_to(slot, ca, madd_c, bias_c):
        rows = pl.ds(pl.multiple_of(ca * tk, tk), 2 * tk)
        add = jnp.concatenate([madd_c if bias_c is None else madd_c + bias_c[r] for r in range(REP)], axis=1)
        s_buf[slot] = _dot(k_ref[rows, :], qt) + add

    def softmax_to(slot):
        m = st_ref[0:1, :]
        m_new = jnp.maximum(m, jnp.max(s_buf[slot], axis=0, keepdims=True))
        m_safe = jnp.where(m_new == NEG_INF, 0.0, m_new)
        alpha = jnp.exp2(m - m_safe)
        p = jnp.exp2(s_buf[slot] - m_safe)
        st_ref[0:1, :] = m_new
        st_ref[1:2, :] = alpha * st_ref[1:2, :] + jnp.sum(p, axis=0, keepdims=True)
        st_ref[2 + slot:3 + slot, :] = alpha
        p_buf[slot] = p.astype(BF16)

    def values_from(slot, ca, cb):
        vt_c = jnp.concatenate([vt_ref[0, 0, ca], vt_ref[0, 0, cb]], axis=1)
        acc_ref[...] = st_ref[2 + slot:3 + slot, :] * acc_ref[...] + _dot(vt_c, p_buf[slot])

    def value_chunks(k):
        return jnp.where(k == 0, c0, 2 * (k - 1)), jnp.where(k == 0, c1, 2 * (k - 1) + 1)

    def far_scores_to(slot, k):
        ca = jnp.minimum(2 * (k - 1), nkc - 2)
        cb = jnp.where(ca + 1 >= c0, neg_slot, ca + 1)
        scores_to(slot, ca, jnp.concatenate([madd_ref[ca], madd_ref[cb]], axis=0), None)

    def far_stage(k, cur, nxt):
        far_scores_to(nxt, k + 1)
        values_from(nxt, *value_chunks(k - 1))
        softmax_to(cur)

    st_ref[0:1, :] = jnp.full((1, REP * tq), NEG_INF, F32)
    st_ref[1:2, :] = jnp.zeros((1, REP * tq), F32)
    acc_ref[...] = jnp.zeros_like(acc_ref)

    first = i == 0
    t_a = jnp.where(first, 0, 1)
    idx_b = jnp.where(first, neg_slot, c0 + 1)
    madd_near = jnp.concatenate([madd_ref[c0], madd_ref[idx_b]], axis=0)
    bias_near = [jnp.concatenate([bias_ref[t_a, g * REP + r], bias_ref[0, g * REP + r]], axis=0)
                 for r in range(REP)]
    scores_to(0, c0, madd_near, bias_near)
    far_scores_to(1, 1)
    softmax_to(0)

    def far_pair(u, carry):
        far_stage(2 * u + 1, 1, 0)
        far_stage(2 * u + 2, 0, 1)
        return carry

    lax.fori_loop(0, n_far // 2, far_pair, 0)
    odd = n_far % 2 == 1

    @pl.when(odd)
    def _():
        far_stage(n_far, 1, 0)
        values_from(1, *value_chunks(n_far))

    @pl.when(jnp.logical_not(odd))
    def _():
        values_from(0, *value_chunks(n_far))

    out = acc_ref[...] / st_ref[1:2, :]
    for r in range(REP):
        o_ref[:, r * HEAD_DIM:(r + 1) * HEAD_DIM] = out[:, r * tq:(r + 1) * tq].T.astype(o_ref.dtype)


def _dsa(qt, kn, vt, qit, wt, ki, rel_bias, batch, seq):
    tq, tk = DSA_TQ, DSA_TK
    nq = seq // tq
    nkc = seq // tk
    topk = min(TOPK_MAX, seq // 4)
    bidx = jnp.asarray(_bias_bucket_tiles(tq, tk))
    kern = functools.partial(_dsa_kernel, tq=tq, tk=tk, nkc=nkc, topk=topk)
    gw = REP * HEAD_DIM
    return pl.pallas_call(
        kern,
        grid=(batch, nq, ATT_KV_HEADS),
        in_specs=[pl.BlockSpec(memory_space=pltpu.SMEM),
                  pl.BlockSpec((2, tk, tq), lambda b, i, g: (0, 0, 0)),
                  pl.BlockSpec((1, REP, HEAD_DIM, tq), lambda b, i, g: (b * nq + i, g, 0, 0)),
                  pl.BlockSpec((1, IDX_Q, tq), lambda b, i, g: (b * nq + i, 0, 0)),
                  pl.BlockSpec((1, IDX_HEADS, tq), lambda b, i, g: (b * nq + i, 0, 0)),
                  pl.BlockSpec((seq, HEAD_DIM), lambda b, i, g: (b, g)),
                  pl.BlockSpec((1, 1, nkc, HEAD_DIM, tk), lambda b, i, g: (b, g, 0, 0, 0)),
                  pl.BlockSpec((seq, IDX_DIM), lambda b, i, g: (b, 0))],
        out_specs=pl.BlockSpec((tq, gw), lambda b, i, g: (b * nq + i, g)),
        out_shape=jax.ShapeDtypeStruct((batch * seq, ATT_Q), BF16),
        scratch_shapes=[pltpu.VMEM((nkc, tk, tq), I32),
                        pltpu.VMEM((nkc + 1, tk, tq), F32),
                        pltpu.VMEM((2, ATT_HEADS, tk, tq), F32),
                        pltpu.VMEM((2, 2 * tk, REP * tq), F32),
                        pltpu.VMEM((2, 2 * tk, REP * tq), BF16),
                        pltpu.VMEM((8, REP * tq), F32),
                        pltpu.VMEM((HEAD_DIM, REP * tq), F32)],
        compiler_params=_params(("arbitrary", "arbitrary", "arbitrary")),
        name="dsa_attention",
    )(rel_bias, bidx, qt, qit, wt, kn, vt, ki)


E_ROWS = 3 * CHUNK + 16
CONV_HALO = 16


def _conv_shift_matrix():
    s = np.zeros((CONV_WIDTH * CHUNK, CONV_HALO + CHUNK), np.float32)
    for k in range(CONV_WIDTH):
        t = np.arange(CHUNK)
        s[k * CHUNK + t, CONV_HALO + t - (CONV_WIDTH - 1) + k] = 1.0
    return s


def _ssd_kernel(xbc_ref, halo_ref, z_ref, sm_ref, shift_ref, cw_ref, cb_ref, dtb_ref, alog_ref, dsk_ref, ng_ref,
                o_ref, xs_ref, bm_ref, cm_ref, state_ref, ypre_ref, acg_ref, actg_ref):
    c = pl.program_id(1)
    L = CHUNK

    @pl.when(c == 0)
    def _():
        state_ref[...] = jnp.zeros_like(state_ref)

    halo_on = (c > 0).astype(BF16)
    shift = shift_ref[...]
    cblk = GROUP_W
    for cbi in range(CONV_DIM // cblk):
        cols = slice(cbi * cblk, (cbi + 1) * cblk)
        ext = jnp.concatenate([halo_ref[:, cols] * halo_on, xbc_ref[:, cols]], axis=0)
        taps = _dot(shift, ext)
        acc = cb_ref[:, cols]
        for kk in range(CONV_WIDTH):
            acc = acc + cw_ref[kk:kk + 1, cols] * taps[kk * L:(kk + 1) * L]
        y = acc * _sigmoid(acc)
        if cbi < SSM_GROUPS:
            xs_ref[cbi] = y
        else:
            per = cblk // SSM_STATE
            for u in range(per):
                gi = (cbi - SSM_GROUPS) * per + u
                piece = y[:, u * SSM_STATE:(u + 1) * SSM_STATE]
                if gi < SSM_GROUPS:
                    bm_ref[gi] = piece
                else:
                    cm_ref[gi - SSM_GROUPS] = piece

    dt_in = sm_ref[:, S_DT:S_DT + SSM_HEADS] + dtb_ref[...]
    dt_act = jnp.maximum(dt_in, 0.0) + jnp.log1p(jnp.exp(-jnp.abs(dt_in)))
    a = dt_act * (-jnp.exp(alog_ref[...]))
    ri = lax.broadcasted_iota(I32, (L, L), 0)
    ci = lax.broadcasted_iota(I32, (L, L), 1)
    tril = ri >= ci
    tri_b = jnp.where(tril, 1.0, 0.0).astype(BF16)
    a_cum = sum(_dot(tri_b, p) for p in _split3(a))
    eye_b = jnp.where(lax.broadcasted_iota(I32, (SSM_HEADS, SSM_HEADS), 0)
                      == lax.broadcasted_iota(I32, (SSM_HEADS, SSM_HEADS), 1), 1.0, 0.0).astype(BF16)
    a_cum_t = sum(_dot_nt(eye_b, p) for p in _split3(a_cum))
    a_last = a_cum[L - 1:L, :]
    for gi in range(SSM_GROUPS):
        acg_ref[gi] = a_cum[:, gi * HEADS_PER_GROUP:(gi + 1) * HEADS_PER_GROUP]
        actg_ref[gi] = a_cum_t[gi * HEADS_PER_GROUP:(gi + 1) * HEADS_PER_GROUP, :]
    cd3 = _split3(jnp.exp(a_last))
    ds3 = _split3(dsk_ref[...])
    extras = jnp.concatenate([p.astype(F32) for p in cd3 + ds3]
                             + [jnp.zeros((E_ROWS - 3 * L - 6, SSM_HEADS), F32)], axis=0)
    e_mat = jnp.concatenate([dt_act, jnp.exp(a_cum), jnp.exp(a_last - a_cum), extras], axis=0).astype(BF16)

    lane = lax.broadcasted_iota(I32, (L, LANES), 1)
    lo_mask = lane < SSM_HEAD_DIM

    def group_body(gi, carry):
        xs = xs_ref[gi]
        bg = bm_ref[gi]
        cg_b = cm_ref[gi].astype(BF16)
        hsel = (lax.broadcasted_iota(I32, (SSM_HEADS, GROUP_W), 0)
                == gi * HEADS_PER_GROUP + lax.broadcasted_iota(I32, (SSM_HEADS, GROUP_W), 1) // SSM_HEAD_DIM)
        ex = _dot(e_mat, jnp.where(hsel, 1.0, 0.0).astype(BF16))
        dt_rep = ex[0:L]
        expa_rep = ex[L:2 * L]
        dte_rep = ex[2 * L:3 * L]
        cd_rep = ex[3 * L:3 * L + 1] + ex[3 * L + 1:3 * L + 2] + ex[3 * L + 2:3 * L + 3]
        dsk_rep = ex[3 * L + 3:3 * L + 4] + ex[3 * L + 4:3 * L + 5] + ex[3 * L + 5:3 * L + 6]

        xd = xs * dt_rep
        xd_b = xd.astype(BF16)
        cb = _dot_nt(cg_b, bg.astype(BF16))
        acg = acg_ref[gi]
        actg = actg_ref[gi]
        pairs = []
        for pj in range(HEADS_PER_GROUP // 2):
            gmat = []
            for e in (2 * pj, 2 * pj + 1):
                seg = acg[:, e:e + 1] - actg[e:e + 1, :]
                dec = jnp.exp(jnp.where(tril, seg, NEG_INF))
                gmat.append((cb * dec).astype(BF16))
            xp = xd_b[:, pj * LANES:(pj + 1) * LANES]
            zero = jnp.zeros_like(xp)
            pairs.append(_dot(gmat[0], jnp.where(lo_mask, xp, zero))
                         + _dot(gmat[1], jnp.where(lo_mask, zero, xp)))
        y_diag = jnp.concatenate(pairs, axis=1)

        st = state_ref[gi]
        y_off = _dot(cg_b, st.astype(BF16)) * expa_rep
        xdd = (xd * dte_rep).astype(BF16)
        state_ref[gi] = st * cd_rep + _dot(bg.T.astype(BF16), xdd)
        ypre_ref[gi] = y_diag + y_off + dsk_rep * xs
        return carry

    lax.fori_loop(0, SSM_GROUPS, group_body, 0, unroll=2)

    for gi in range(SSM_GROUPS):
        cols = slice(gi * GROUP_W, (gi + 1) * GROUP_W)
        zz = z_ref[:, cols].astype(F32)
        y = ypre_ref[gi] * (zz * _sigmoid(zz))
        ms = jnp.mean(y * y, axis=-1, keepdims=True)
        o_ref[:, cols] = (y * lax.rsqrt(ms + EPS) * ng_ref[:, cols]).astype(o_ref.dtype)


def _ssd(main, small, conv_w, conv_b, dt_bias, a_log, d_skip, norm_g, batch, seq):
    nc = seq // CHUNK
    hb = CHUNK // CONV_HALO
    row = lambda b, c: b * nc + c
    full = lambda shape: pl.BlockSpec(shape, lambda b, c: (0,) * len(shape))
    shift = jnp.asarray(_conv_shift_matrix(), BF16)
    return pl.pallas_call(
        _ssd_kernel,
        grid=(batch, nc),
        in_specs=[pl.BlockSpec((CHUNK, CONV_DIM), lambda b, c: (row(b, c), M_XBC // CONV_DIM)),
                  pl.BlockSpec((CONV_HALO, CONV_DIM),
                               lambda b, c: (jnp.maximum(row(b, c) * hb - 1, 0), M_XBC // CONV_DIM)),
                  pl.BlockSpec((CHUNK, SSM_INNER), lambda b, c: (row(b, c), M_Z // SSM_INNER)),
                  pl.BlockSpec((CHUNK, S_END), lambda b, c: (row(b, c), 0)),
                  full((CONV_WIDTH * CHUNK, CONV_HALO + CHUNK)),
                  full((CONV_WIDTH, CONV_DIM)), full((1, CONV_DIM)), full((1, SSM_HEADS)),
                  full((1, SSM_HEADS)), full((1, SSM_HEADS)), full((1, SSM_INNER))],
        out_specs=pl.BlockSpec((CHUNK, SSM_INNER), lambda b, c: (row(b, c), 0)),
        out_shape=jax.ShapeDtypeStruct((batch * seq, SSM_INNER), BF16),
        scratch_shapes=[pltpu.VMEM((SSM_GROUPS, CHUNK, GROUP_W), F32),
                        pltpu.VMEM((SSM_GROUPS, CHUNK, SSM_STATE), F32),
                        pltpu.VMEM((SSM_GROUPS, CHUNK, SSM_STATE), F32),
                        pltpu.VMEM((SSM_GROUPS, SSM_STATE, GROUP_W), F32),
                        pltpu.VMEM((SSM_GROUPS, CHUNK, GROUP_W), F32),
                        pltpu.VMEM((SSM_GROUPS, CHUNK, HEADS_PER_GROUP), F32),
                        pltpu.VMEM((SSM_GROUPS, HEADS_PER_GROUP, CHUNK), F32)],
        compiler_params=_params(("arbitrary", "arbitrary")),
        name="ssd_scan",
    )(main, main, main, small, shift, conv_w, conv_b, dt_bias, a_log, d_skip, norm_g)


def _merge_kernel(att_ref, y_ref, ga_ref, gs_ref, wa_ref, ws_ref, o_ref):
    pa = _dot(att_ref[...], wa_ref[...])
    ps = _dot(y_ref[...], ws_ref[...])
    o_ref[...] = (_sigmoid(ga_ref[...].astype(F32)) * pa + _sigmoid(gs_ref[...].astype(F32)) * ps).astype(o_ref.dtype)


def _merge(att, y, main, wa, ws, tm, tn):
    t = att.shape[0]
    return pl.pallas_call(
        _merge_kernel,
        grid=(t // tm, D_MODEL // tn),
        in_specs=[pl.BlockSpec((tm, ATT_Q), lambda i, j: (i, 0)),
                  pl.BlockSpec((tm, SSM_INNER), lambda i, j: (i, 0)),
                  pl.BlockSpec((tm, tn), lambda i, j: (i, M_GA // tn + j)),
                  pl.BlockSpec((tm, tn), lambda i, j: (i, M_GS // tn + j)),
                  pl.BlockSpec((ATT_Q, tn), lambda i, j: (0, j)),
                  pl.BlockSpec((SSM_INNER, tn), lambda i, j: (0, j))],
        out_specs=pl.BlockSpec((tm, tn), lambda i, j: (i, j)),
        out_shape=jax.ShapeDtypeStruct((t, D_MODEL), BF16),
        compiler_params=_params(("parallel", "arbitrary")),
        name="gated_merge",
    )(att, y, main, main, wa, ws)


def _out_proj_kernel(m_ref, w_ref, x_ref, g_ref, x1_ref, h2_ref):
    x1 = x_ref[...] + _dot(m_ref[...], w_ref[...])
    x1_ref[...] = x1
    ms = jnp.mean(x1 * x1, axis=-1, keepdims=True)
    h2_ref[...] = (x1 * lax.rsqrt(ms + EPS) * g_ref[...]).astype(BF16)


def _out_proj(merged, w, x, g, tm):
    t = x.shape[0]
    return pl.pallas_call(
        _out_proj_kernel,
        grid=(t // tm,),
        in_specs=[pl.BlockSpec((tm, D_MODEL), lambda i: (i, 0)),
                  pl.BlockSpec((D_MODEL, D_MODEL), lambda i: (0, 0)),
                  pl.BlockSpec((tm, D_MODEL), lambda i: (i, 0)),
                  pl.BlockSpec((1, D_MODEL), lambda i: (0, 0))],
        out_specs=[pl.BlockSpec((tm, D_MODEL), lambda i: (i, 0)),
                   pl.BlockSpec((tm, D_MODEL), lambda i: (i, 0))],
        out_shape=[jax.ShapeDtypeStruct((t, D_MODEL), F32),
                   jax.ShapeDtypeStruct((t, D_MODEL), BF16)],
        compiler_params=_params(("parallel",)),
        name="out_proj_norm",
    )(merged, w, x, g)


def _mlp_kernel(h_ref, x1_ref, wu_ref, wd_ref, o_ref):
    @pl.when(pl.program_id(1) == 0)
    def _():
        o_ref[...] = x1_ref[...]

    u = _dot(h_ref[...], wu_ref[...])
    u = jnp.square(jnp.maximum(u, 0.0)).astype(BF16)
    o_ref[...] += _dot(u, wd_ref[...])


def _mlp(h2, x1, wu, wd, tm, th):
    t = h2.shape[0]
    return pl.pallas_call(
        _mlp_kernel,
        grid=(t // tm, MLP_HIDDEN // th),
        in_specs=[pl.BlockSpec((tm, D_MODEL), lambda i, j: (i, 0)),
                  pl.BlockSpec((tm, D_MODEL), lambda i, j: (i, 0)),
                  pl.BlockSpec((D_MODEL, th), lambda i, j: (0, j)),
                  pl.BlockSpec((th, D_MODEL), lambda i, j: (j, 0))],
        out_specs=pl.BlockSpec((tm, D_MODEL), lambda i, j: (i, 0)),
        out_shape=jax.ShapeDtypeStruct((t, D_MODEL), F32),
        compiler_params=_params(("parallel", "arbitrary")),
        name="relu2_mlp",
    )(h2, x1, wu, wd)


def _pack_w_in(w):
    assert (O_GA, O_KI, O_Z, O_DT) == (M_GA, M_Z, O_WI + IDX_HEADS, O_Z + M_END - M_Z)
    wb = w.astype(BF16)
    main = jnp.concatenate([wb[:, O_GA:O_KI], wb[:, O_Z:O_DT]], axis=1)
    small = jnp.concatenate([wb[:, O_KI:O_Z], wb[:, O_DT:O_DT + SSM_HEADS],
                             jnp.zeros((w.shape[0], S_END - S_DT - SSM_HEADS), BF16)], axis=1)
    return main, small


def _block(x2, batch, seq, norm1_g, w_in, conv_w, conv_b, dt_bias, a_log, d_skip, ssm_norm_g, q_norm_g,
           k_norm_g, rel_bias, w_att_branch, w_ssm_branch, w_out, norm2_g, w_up, w_down):
    row = lambda v: v.reshape(1, -1)
    w_main, w_small = _pack_w_in(w_in)
    main = _norm_matmul(x2, row(norm1_g), w_main, BF16, tm=min(1024, x2.shape[0]), tn=1024)
    small = _norm_matmul(x2, row(norm1_g), w_small, F32, tm=min(1024, x2.shape[0]), tn=S_END)
    qt, kn, vt, qit, wt, ki = _qk_prep(main, small, row(q_norm_g), row(k_norm_g), batch, seq, tq=DSA_TQ)
    att = _dsa(qt, kn, vt, qit, wt, ki, rel_bias, batch, seq)
    y = _ssd(main, small, conv_w, row(conv_b), row(dt_bias), row(a_log), row(d_skip), row(ssm_norm_g),
             batch, seq)
    merged = _merge(att, y, main, w_att_branch.astype(BF16), w_ssm_branch.astype(BF16), tm=512, tn=1024)
    x1, h2 = _out_proj(merged, w_out.astype(BF16), x2, row(norm2_g), tm=512)
    return _mlp(h2, x1, w_up.astype(BF16), w_down.astype(BF16), tm=512, th=1024)


def kernel(x, norm1_g, w_in, conv_w, conv_b, dt_bias, a_log, d_skip, ssm_norm_g, q_norm_g, k_norm_g, rel_bias,
           w_att_branch, w_ssm_branch, w_out, norm2_g, w_up, w_down):
    batch, seq, d = x.shape
    x2 = x.reshape(batch * seq, d)
    for l in range(norm1_g.shape[0]):
        x2 = _block(x2, batch, seq, norm1_g[l], w_in[l], conv_w[l], conv_b[l], dt_bias[l], a_log[l], d_skip[l],
                    ssm_norm_g[l], q_norm_g[l], k_norm_g[l], rel_bias, w_att_branch[l], w_ssm_branch[l],
                    w_out[l], norm2_g[l], w_up[l], w_down[l])
    return x2.reshape(batch, seq, d)
```

```python
import functools
import math

import numpy as np
import jax
import jax.numpy as jnp
from jax import lax
from jax.experimental import pallas as pl
from jax.experimental.pallas import tpu as pltpu

F32 = jnp.float32
BF16 = jnp.bfloat16
I32 = jnp.int32

D_MODEL = 2048
ATT_HEADS = 16
ATT_KV_HEADS = 4
HEAD_DIM = 128
REP = ATT_HEADS // ATT_KV_HEADS
IDX_HEADS = 16
IDX_DIM = 64
TOPK_MAX = 256
N_BUCKETS = 32
MAX_DISTANCE = 128
SSM_INNER = 2 * D_MODEL
SSM_HEAD_DIM = 64
SSM_HEADS = SSM_INNER // SSM_HEAD_DIM
SSM_GROUPS = 8
SSM_STATE = 128
CONV_WIDTH = 4
CHUNK = 128
MLP_HIDDEN = 4 * D_MODEL
EPS = 1e-6

ATT_Q = ATT_HEADS * HEAD_DIM
ATT_KV = ATT_KV_HEADS * HEAD_DIM
IDX_Q = IDX_HEADS * IDX_DIM
SSM_BC = SSM_GROUPS * SSM_STATE
CONV_DIM = SSM_INNER + 2 * SSM_BC
SPLITS = (D_MODEL, D_MODEL, ATT_Q, ATT_KV, ATT_KV, IDX_Q, IDX_DIM, IDX_HEADS, SSM_INNER, CONV_DIM, SSM_HEADS)
_OFFS = tuple(int(v) for v in np.cumsum((0,) + SPLITS))
(O_GA, O_GS, O_Q, O_K, O_V, O_QI, O_KI, O_WI, O_Z, O_XBC, O_DT, _O_END) = _OFFS

M_GA = 0
M_GS = M_GA + D_MODEL
M_Q = M_GS + D_MODEL
M_K = M_Q + ATT_Q
M_V = M_K + ATT_KV
M_QI = M_V + ATT_KV
M_Z = M_QI + IDX_Q
M_XBC = M_Z + SSM_INNER
M_END = M_XBC + CONV_DIM
S_KI = 0
S_WI = S_KI + IDX_DIM
S_DT = S_WI + IDX_HEADS
S_END = 256

HEADS_PER_GROUP = SSM_HEADS // SSM_GROUPS
GROUP_W = HEADS_PER_GROUP * SSM_HEAD_DIM

LANES = 128
VMEM_LIMIT = 56 * 1024 * 1024

NEG_INF = float("-inf")
INT_MIN = -(2 ** 31)
KEY_NEG_INF = int(np.int32(np.uint32(0xFF800000) ^ np.uint32(0x7FFFFFFF)))


def _dot(a, b):
    return jnp.dot(a, b, preferred_element_type=F32)


def _dot_nt(a, b):
    return lax.dot_general(a, b, (((1,), (1,)), ((), ())), preferred_element_type=F32)


def _split3(x):
    hi = x.astype(BF16)
    r = x - hi.astype(F32)
    mid = r.astype(BF16)
    lo = (r - mid.astype(F32)).astype(BF16)
    return hi, mid, lo


def _sigmoid(x):
    return 1.0 / (1.0 + jnp.exp(-x))


def _params(sem):
    return pltpu.CompilerParams(dimension_semantics=sem, vmem_limit_bytes=VMEM_LIMIT)


def _norm_small_kernel(x_ref, g_ref, w_ref, xn_ref, sm_ref):
    x = x_ref[...]
    ms = jnp.mean(x * x, axis=-1, keepdims=True)
    xn = (x * lax.rsqrt(ms + EPS) * g_ref[...]).astype(BF16)
    xn_ref[...] = xn
    sm_ref[...] = _dot(xn, w_ref[...])


def _norm_small(x, g, w_small, tm):
    m, d = x.shape
    n = w_small.shape[1]
    return pl.pallas_call(
        _norm_small_kernel,
        grid=(m // tm,),
        in_specs=[pl.BlockSpec((tm, d), lambda i: (i, 0)),
                  pl.BlockSpec((1, d), lambda i: (0, 0)),
                  pl.BlockSpec((d, n), lambda i: (0, 0))],
        out_specs=[pl.BlockSpec((tm, d), lambda i: (i, 0)),
                   pl.BlockSpec((tm, n), lambda i: (i, 0))],
        out_shape=[jax.ShapeDtypeStruct((m, d), BF16), jax.ShapeDtypeStruct((m, n), F32)],
        compiler_params=_params(("parallel",)),
        name="norm_small_proj",
    )(x, g, w_small)


IN_TN = 1024
IN_ALIGNED_TILES = O_KI // IN_TN
IN_SHIFT = O_Z - M_Z
IN_ROWS = 256


def _in_proj_kernel(xn_ref, wa_ref, wn_ref, o_ref, w_scr):
    j = pl.program_id(0)
    i = pl.program_id(1)
    d = wa_ref.shape[0]

    @pl.when((i == 0) & (j < IN_ALIGNED_TILES))
    def _():
        def body(r, carry):
            rows = pl.ds(pl.multiple_of(r * IN_ROWS, IN_ROWS), IN_ROWS)
            w_scr[rows, :] = wa_ref[rows, :].astype(BF16)
            return carry

        lax.fori_loop(0, d // IN_ROWS, body, 0)

    @pl.when((i == 0) & (j >= IN_ALIGNED_TILES))
    def _():
        def body(r, carry):
            rows = pl.ds(pl.multiple_of(r * IN_ROWS, IN_ROWS), IN_ROWS)
            wide = jnp.concatenate([wa_ref[rows, :], wn_ref[rows, :]], axis=1)
            w_scr[rows, :] = wide[:, IN_SHIFT:IN_SHIFT + IN_TN].astype(BF16)
            return carry

        lax.fori_loop(0, d // IN_ROWS, body, 0)

    o_ref[...] = _dot(xn_ref[...], w_scr[...]).astype(o_ref.dtype)


def _in_proj(xn, w_in, tm):
    m, d = xn.shape
    assert O_KI % IN_TN == 0 and M_END % IN_TN == 0 and 0 < IN_SHIFT < LANES
    return pl.pallas_call(
        _in_proj_kernel,
        grid=(M_END // IN_TN, m // tm),
        in_specs=[pl.BlockSpec((tm, d), lambda j, i: (i, 0)),
                  pl.BlockSpec((d, IN_TN), lambda j, i: (0, j)),
                  pl.BlockSpec((d, LANES), lambda j, i: (0, (j + 1) * (IN_TN // LANES)))],
        out_specs=pl.BlockSpec((tm, IN_TN), lambda j, i: (i, j)),
        out_shape=jax.ShapeDtypeStruct((m, M_END), BF16),
        scratch_shapes=[pltpu.VMEM((d, IN_TN), BF16)],
        compiler_params=_params(("arbitrary", "arbitrary")),
        name="in_proj",
    )(xn, w_in, w_in)


LOG2E = math.log2(math.e)


def _qk_prep_kernel(q_ref, k_ref, v_ref, qi_ref, sm_ref, qg_ref, kg_ref,
                    qt_ref, kn_ref, vt_ref, qit_ref, wt_ref, ki_ref):
    qg = qg_ref[...]
    for h in range(ATT_HEADS):
        x = q_ref[:, h * HEAD_DIM:(h + 1) * HEAD_DIM].astype(F32)
        ms = jnp.mean(x * x, axis=-1, keepdims=True)
        y = x * lax.rsqrt(ms + EPS) * qg * (HEAD_DIM ** -0.5 * LOG2E)
        qt_ref[0, h] = y.T.astype(BF16)
    kg = kg_ref[...]
    for h in range(ATT_KV_HEADS):
        x = k_ref[:, h * HEAD_DIM:(h + 1) * HEAD_DIM].astype(F32)
        ms = jnp.mean(x * x, axis=-1, keepdims=True)
        kn_ref[:, h * HEAD_DIM:(h + 1) * HEAD_DIM] = (x * lax.rsqrt(ms + EPS) * kg).astype(BF16)
        vt_ref[0, h, 0] = v_ref[:, h * HEAD_DIM:(h + 1) * HEAD_DIM].astype(F32).T.astype(BF16)
    for p in range(IDX_Q // LANES):
        qit_ref[0, p * LANES:(p + 1) * LANES, :] = qi_ref[:, p * LANES:(p + 1) * LANES].astype(F32).T.astype(BF16)
    sm_t = sm_ref[:, 0:LANES].T
    wt_ref[0] = sm_t[S_WI:S_WI + IDX_HEADS, :] * (IDX_HEADS ** -0.5 * IDX_DIM ** -0.5)
    ki_ref[...] = sm_ref[:, S_KI:S_KI + IDX_DIM].astype(BF16)


def _qk_prep(main, small, qg, kg, batch, seq, tq):
    t = main.shape[0]
    nq = seq // tq
    return pl.pallas_call(
        _qk_prep_kernel,
        grid=(batch, nq),
        in_specs=[pl.BlockSpec((tq, ATT_Q), lambda b, i: (b * nq + i, M_Q // ATT_Q)),
                  pl.BlockSpec((tq, ATT_KV), lambda b, i: (b * nq + i, M_K // ATT_KV)),
                  pl.BlockSpec((tq, ATT_KV), lambda b, i: (b * nq + i, M_V // ATT_KV)),
                  pl.BlockSpec((tq, IDX_Q), lambda b, i: (b * nq + i, M_QI // IDX_Q)),
                  pl.BlockSpec((tq, S_END), lambda b, i: (b * nq + i, 0)),
                  pl.BlockSpec((1, HEAD_DIM), lambda b, i: (0, 0)),
                  pl.BlockSpec((1, HEAD_DIM), lambda b, i: (0, 0))],
        out_specs=[pl.BlockSpec((1, ATT_HEADS, HEAD_DIM, tq), lambda b, i: (b * nq + i, 0, 0, 0)),
                   pl.BlockSpec((tq, ATT_KV), lambda b, i: (b * nq + i, 0)),
                   pl.BlockSpec((1, ATT_KV_HEADS, 1, HEAD_DIM, tq), lambda b, i: (b, 0, i, 0, 0)),
                   pl.BlockSpec((1, IDX_Q, tq), lambda b, i: (b * nq + i, 0, 0)),
                   pl.BlockSpec((1, IDX_HEADS, tq), lambda b, i: (b * nq + i, 0, 0)),
                   pl.BlockSpec((tq, IDX_DIM), lambda b, i: (b * nq + i, 0))],
        out_shape=[jax.ShapeDtypeStruct((t // tq, ATT_HEADS, HEAD_DIM, tq), BF16),
                   jax.ShapeDtypeStruct((t, ATT_KV), BF16),
                   jax.ShapeDtypeStruct((batch, ATT_KV_HEADS, nq, HEAD_DIM, tq), BF16),
                   jax.ShapeDtypeStruct((t // tq, IDX_Q, tq), BF16),
                   jax.ShapeDtypeStruct((t // tq, IDX_HEADS, tq), F32),
                   jax.ShapeDtypeStruct((t, IDX_DIM), BF16)],
        compiler_params=_params(("parallel", "parallel")),
        name="qk_prep",
    )(main, main, main, main, small, qg, kg)


DSA_TQ = 256
DSA_TK = 256
CNT_WAYS = 4


def _t5_bucket_np(dist):
    n = np.maximum(dist, 0)
    max_exact = N_BUCKETS // 2
    nf = np.maximum(n, 1).astype(np.float32)
    ratio = (np.log(nf / np.float32(max_exact)) / np.float32(math.log(MAX_DISTANCE / max_exact))
             * np.float32(N_BUCKETS - max_exact))
    large = max_exact + ratio.astype(np.int32)
    large = np.minimum(large, N_BUCKETS - 1)
    return np.where(n < max_exact, n, large).astype(np.int32)


def _bias_bucket_tiles(tq, tk):
    r = np.arange(tq)[None, :]
    c = np.arange(tk)[:, None]
    d0 = _t5_bucket_np(r - c)
    d1 = _t5_bucket_np(tk + r - c)
    assert np.all(_t5_bucket_np(np.arange(tk + 1, 8 * tk)) == N_BUCKETS - 1)
    return np.stack([d0, d1]).astype(np.int32)


def _dsa_kernel(relb_ref, bidx_ref, qt_ref, qit_ref, wt_ref, k_ref, vt_ref, ki_ref, o_ref,
                key_ref, madd_ref, bias_ref, s_buf, p_buf, st_ref, acc_ref, *, tq, tk, nkc, topk):
    b = pl.program_id(0)
    i = pl.program_id(1)
    g = pl.program_id(2)
    neg_slot = nkc

    @pl.when((b == 0) & (i == 0) & (g == 0))
    def _init():
        madd_ref[neg_slot] = jnp.full((tk, tq), NEG_INF, F32)
        for t in range(2):
            bt = bidx_ref[t]

            def head_body(h, carry):
                far = relb_ref[N_BUCKETS - 1, h]

                def bucket_body(bk, acc):
                    return jnp.where(bt == bk, (relb_ref[bk, h] - far) * LOG2E, acc)

                bias_ref[t, h] = lax.fori_loop(0, N_BUCKETS, bucket_body, jnp.zeros((tk, tq), F32))
                return carry

            lax.fori_loop(0, ATT_HEADS, head_body, 0)

    @pl.when(g == 0)
    def _select():
        nj = i + 1
        qpos = i * tq + lax.broadcasted_iota(I32, (tk, tq), 1)

        def score_chunk(j, carry):
            kc = ki_ref[pl.ds(pl.multiple_of(j * tk, tk), tk), :]
            acc = jnp.zeros((tk, tq), F32)
            for h in range(IDX_HEADS):
                z = _dot(kc, qit_ref[0, h * IDX_DIM:(h + 1) * IDX_DIM, :])
                acc = acc + jnp.maximum(z, 0.0) * wt_ref[0, h:h + 1, :]
            kpos = j * tk + lax.broadcasted_iota(I32, (tk, tq), 0)
            acc = jnp.where(kpos <= qpos, acc, NEG_INF)
            bits = pltpu.bitcast(acc, I32)
            key_ref[j] = bits ^ ((bits >> 31) & 0x7FFFFFFF)
            return carry

        lax.fori_loop(0, nj, score_chunk, 0)

        def bit_iter(it, prefix):
            cand = prefix + lax.shift_left(jnp.int32(1), 31 - it)

            def cnt_chunk(j, cnt):
                hit = jnp.where(key_ref[j] >= cand, 1.0, 0.0)
                return cnt + jnp.sum(hit.reshape(tk // (8 * CNT_WAYS), CNT_WAYS * 8, tq), axis=0)

            cnt = lax.fori_loop(0, nj, cnt_chunk, jnp.zeros((CNT_WAYS * 8, tq), F32))
            tot = jnp.sum(cnt, axis=0, keepdims=True)
            return jnp.where(tot >= float(topk), cand, prefix)

        thr = lax.fori_loop(0, 32, bit_iter, jnp.full((1, tq), INT_MIN, I32))
        thr = jnp.maximum(thr, KEY_NEG_INF + 1)

        def madd_chunk(j, carry):
            madd_ref[j] = jnp.where(key_ref[j] >= thr, 0.0, NEG_INF)
            return carry

        lax.fori_loop(0, nj, madd_chunk, 0)

    qt = jnp.concatenate([qt_ref[0, r] for r in range(REP)], axis=1)
    c0 = jnp.maximum(i - 1, 0)
    c1 = jnp.minimum(c0 + 1, nkc - 1)
    n_far = (c0 + 1) // 2

    def scores_to(slot, ca, madd_c, bias_c):
        rows = pl.ds(pl.multiple_of(ca * tk, tk), 2 * tk)
        add = jnp.concatenate([madd_c if bias_c is None else madd_c + bias_c[r] for r in range(REP)], axis=1)
        s_buf[slot] = _dot(k_ref[rows, :], qt) + add

    def softmax_to(slot):
        m = st_ref[0:1, :]
        m_new = jnp.maximum(m, jnp.max(s_buf[slot], axis=0, keepdims=True))
        m_safe = jnp.where(m_new == NEG_INF, 0.0, m_new)
        alpha = jnp.exp2(m - m_safe)
        p = jnp.exp2(s_buf[slot] - m_safe)
        st_ref[0:1, :] = m_new
        st_ref[1:2, :] = alpha * st_ref[1:2, :] + jnp.sum(p, axis=0, keepdims=True)
        st_ref[2 + slot:3 + slot, :] = alpha
        p_buf[slot] = p.astype(BF16)

    def values_from(slot, ca, cb):
        vt_c = jnp.concatenate([vt_ref[0, 0, ca], vt_ref[0, 0, cb]], axis=1)
        acc_ref[...] = st_ref[2 + slot:3 + slot, :] * acc_ref[...] + _dot(vt_c, p_buf[slot])

    def value_chunks(k):
        return jnp.where(k == 0, c0, 2 * (k - 1)), jnp.where(k == 0, c1, 2 * (k - 1) + 1)

    def far_scores_to(slot, k):
        ca = jnp.minimum(2 * (k - 1), nkc - 2)
        cb = jnp.where(ca + 1 >= c0, neg_slot, ca + 1)
        scores_to(slot, ca, jnp.concatenate([madd_ref[ca], madd_ref[cb]], axis=0), None)

    def far_stage(k, cur, nxt):
        far_scores_to(nxt, k + 1)
        values_from(nxt, *value_chunks(k - 1))
        softmax_to(cur)

    st_ref[0:1, :] = jnp.full((1, REP * tq), NEG_INF, F32)
    st_ref[1:2, :] = jnp.zeros((1, REP * tq), F32)
    acc_ref[...] = jnp.zeros_like(acc_ref)

    first = i == 0
    t_a = jnp.where(first, 0, 1)
    idx_b = jnp.where(first, neg_slot, c0 + 1)
    madd_near = jnp.concatenate([madd_ref[c0], madd_ref[idx_b]], axis=0)
    bias_near = [jnp.concatenate([bias_ref[t_a, g * REP + r], bias_ref[0, g * REP + r]], axis=0)
                 for r in range(REP)]
    scores_to(0, c0, madd_near, bias_near)
    far_scores_to(1, 1)
    softmax_to(0)

    def far_pair(u, carry):
        far_stage(2 * u + 1, 1, 0)
        far_stage(2 * u + 2, 0, 1)
        return carry

    lax.fori_loop(0, n_far // 2, far_pair, 0)
    odd = n_far % 2 == 1

    @pl.when(odd)
    def _():
        far_stage(n_far, 1, 0)
        values_from(1, *value_chunks(n_far))

    @pl.when(jnp.logical_not(odd))
    def _():
        values_from(0, *value_chunks(n_far))

    out = acc_ref[...] / st_ref[1:2, :]
    for r in range(REP):
        o_ref[:, r * HEAD_DIM:(r + 1) * HEAD_DIM] = out[:, r * tq:(r + 1) * tq].T.astype(o_ref.dtype)


def _dsa(qt, kn, vt, qit, wt, ki, rel_bias, batch, seq):
    tq, tk = DSA_TQ, DSA_TK
    nq = seq // tq
    nkc = seq // tk
    topk = min(TOPK_MAX, seq // 4)
    bidx = jnp.asarray(_bias_bucket_tiles(tq, tk))
    kern = functools.partial(_dsa_kernel, tq=tq, tk=tk, nkc=nkc, topk=topk)
    gw = REP * HEAD_DIM
    return pl.pallas_call(
        kern,
        grid=(batch, nq, ATT_KV_HEADS),
        in_specs=[pl.BlockSpec(memory_space=pltpu.SMEM),
                  pl.BlockSpec((2, tk, tq), lambda b, i, g: (0, 0, 0)),
                  pl.BlockSpec((1, REP, HEAD_DIM, tq), lambda b, i, g: (b * nq + i, g, 0, 0)),
                  pl.BlockSpec((1, IDX_Q, tq), lambda b, i, g: (b * nq + i, 0, 0)),
                  pl.BlockSpec((1, IDX_HEADS, tq), lambda b, i, g: (b * nq + i, 0, 0)),
                  pl.BlockSpec((seq, HEAD_DIM), lambda b, i, g: (b, g)),
                  pl.BlockSpec((1, 1, nkc, HEAD_DIM, tk), lambda b, i, g: (b, g, 0, 0, 0)),
                  pl.BlockSpec((seq, IDX_DIM), lambda b, i, g: (b, 0))],
        out_specs=pl.BlockSpec((tq, gw), lambda b, i, g: (b * nq + i, g)),
        out_shape=jax.ShapeDtypeStruct((batch * seq, ATT_Q), BF16),
        scratch_shapes=[pltpu.VMEM((nkc, tk, tq), I32),
                        pltpu.VMEM((nkc + 1, tk, tq), F32),
                        pltpu.VMEM((2, ATT_HEADS, tk, tq), F32),
                        pltpu.VMEM((2, 2 * tk, REP * tq), F32),
                        pltpu.VMEM((2, 2 * tk, REP * tq), BF16),
                        pltpu.VMEM((8, REP * tq), F32),
                        pltpu.VMEM((HEAD_DIM, REP * tq), F32)],
        compiler_params=_params(("arbitrary", "arbitrary", "arbitrary")),
        name="dsa_attention",
    )(rel_bias, bidx, qt, qit, wt, kn, vt, ki)


E_ROWS = 3 * CHUNK + 16
CONV_HALO = 16


def _conv_shift_matrix():
    s = np.zeros((CONV_WIDTH * CHUNK, CONV_HALO + CHUNK), np.float32)
    for k in range(CONV_WIDTH):
        t = np.arange(CHUNK)
        s[k * CHUNK + t, CONV_HALO + t - (CONV_WIDTH - 1) + k] = 1.0
    return s


def _ssd_kernel(xbc_ref, halo_ref, z_ref, sm_ref, shift_ref, cw_ref, cb_ref, dtb_ref, alog_ref, dsk_ref, ng_ref,
                o_ref, xs_ref, bm_ref, cm_ref, state_ref, ypre_ref, acg_ref, actg_ref):
    c = pl.program_id(1)
    L = CHUNK

    @pl.when(c == 0)
    def _():
        state_ref[...] = jnp.zeros_like(state_ref)

    halo_on = (c > 0).astype(BF16)
    shift = shift_ref[...]
    cblk = GROUP_W
    for cbi in range(CONV_DIM // cblk):
        cols = slice(cbi * cblk, (cbi + 1) * cblk)
        ext = jnp.concatenate([halo_ref[:, cols] * halo_on, xbc_ref[:, cols]], axis=0)
        taps = _dot(shift, ext)
        acc = cb_ref[:, cols]
        for kk in range(CONV_WIDTH):
            acc = acc + cw_ref[kk:kk + 1, cols] * taps[kk * L:(kk + 1) * L]
        y = acc * _sigmoid(acc)
        if cbi < SSM_GROUPS:
            xs_ref[cbi] = y
        else:
            per = cblk // SSM_STATE
            for u in range(per):
                gi = (cbi - SSM_GROUPS) * per + u
                piece = y[:, u * SSM_STATE:(u + 1) * SSM_STATE]
                if gi < SSM_GROUPS:
                    bm_ref[gi] = piece
                else:
                    cm_ref[gi - SSM_GROUPS] = piece

    dt_in = sm_ref[:, S_DT:S_DT + SSM_HEADS] + dtb_ref[...]
    dt_act = jnp.maximum(dt_in, 0.0) + jnp.log1p(jnp.exp(-jnp.abs(dt_in)))
    a = dt_act * (-jnp.exp(alog_ref[...]))
    ri = lax.broadcasted_iota(I32, (L, L), 0)
    ci = lax.broadcasted_iota(I32, (L, L), 1)
    tril = ri >= ci
    tri_b = jnp.where(tril, 1.0, 0.0).astype(BF16)
    a_cum = sum(_dot(tri_b, p) for p in _split3(a))
    eye_b = jnp.where(lax.broadcasted_iota(I32, (SSM_HEADS, SSM_HEADS), 0)
                      == lax.broadcasted_iota(I32, (SSM_HEADS, SSM_HEADS), 1), 1.0, 0.0).astype(BF16)
    a_cum_t = sum(_dot_nt(eye_b, p) for p in _split3(a_cum))
    a_last = a_cum[L - 1:L, :]
    for gi in range(SSM_GROUPS):
        acg_ref[gi] = a_cum[:, gi * HEADS_PER_GROUP:(gi + 1) * HEADS_PER_GROUP]
        actg_ref[gi] = a_cum_t[gi * HEADS_PER_GROUP:(gi + 1) * HEADS_PER_GROUP, :]
    cd3 = _split3(jnp.exp(a_last))
    ds3 = _split3(dsk_ref[...])
    extras = jnp.concatenate([p.astype(F32) for p in cd3 + ds3]
                             + [jnp.zeros((E_ROWS - 3 * L - 6, SSM_HEADS), F32)], axis=0)
    e_mat = jnp.concatenate([dt_act, jnp.exp(a_cum), jnp.exp(a_last - a_cum), extras], axis=0).astype(BF16)

    lane = lax.broadcasted_iota(I32, (L, LANES), 1)
    lo_mask = lane < SSM_HEAD_DIM

    def group_body(gi, carry):
        xs = xs_ref[gi]
        bg = bm_ref[gi]
        cg_b = cm_ref[gi].astype(BF16)
        hsel = (lax.broadcasted_iota(I32, (SSM_HEADS, GROUP_W), 0)
                == gi * HEADS_PER_GROUP + lax.broadcasted_iota(I32, (SSM_HEADS, GROUP_W), 1) // SSM_HEAD_DIM)
        ex = _dot(e_mat, jnp.where(hsel, 1.0, 0.0).astype(BF16))
        dt_rep = ex[0:L]
        expa_rep = ex[L:2 * L]
        dte_rep = ex[2 * L:3 * L]
        cd_rep = ex[3 * L:3 * L + 1] + ex[3 * L + 1:3 * L + 2] + ex[3 * L + 2:3 * L + 3]
        dsk_rep = ex[3 * L + 3:3 * L + 4] + ex[3 * L + 4:3 * L + 5] + ex[3 * L + 5:3 * L + 6]

        xd = xs * dt_rep
        xd_b = xd.astype(BF16)
        cb = _dot_nt(cg_b, bg.astype(BF16))
        acg = acg_ref[gi]
        actg = actg_ref[gi]
        pairs = []
        for pj in range(HEADS_PER_GROUP // 2):
            gmat = []
            for e in (2 * pj, 2 * pj + 1):
                seg = acg[:, e:e + 1] - actg[e:e + 1, :]
                dec = jnp.exp(jnp.where(tril, seg, NEG_INF))
                gmat.append((cb * dec).astype(BF16))
            xp = xd_b[:, pj * LANES:(pj + 1) * LANES]
            zero = jnp.zeros_like(xp)
            pairs.append(_dot(gmat[0], jnp.where(lo_mask, xp, zero))
                         + _dot(gmat[1], jnp.where(lo_mask, zero, xp)))
        y_diag = jnp.concatenate(pairs, axis=1)

        st = state_ref[gi]
        y_off = _dot(cg_b, st.astype(BF16)) * expa_rep
        xdd = (xd * dte_rep).astype(BF16)
        state_ref[gi] = st * cd_rep + _dot(bg.T.astype(BF16), xdd)
        ypre_ref[gi] = y_diag + y_off + dsk_rep * xs
        return carry

    lax.fori_loop(0, SSM_GROUPS, group_body, 0, unroll=2)

    for gi in range(SSM_GROUPS):
        cols = slice(gi * GROUP_W, (gi + 1) * GROUP_W)
        zz = z_ref[:, cols].astype(F32)
        y = ypre_ref[gi] * (zz * _sigmoid(zz))
        ms = jnp.mean(y * y, axis=-1, keepdims=True)
        o_ref[:, cols] = (y * lax.rsqrt(ms + EPS) * ng_ref[:, cols]).astype(o_ref.dtype)


def _ssd(main, small, conv_w, conv_b, dt_bias, a_log, d_skip, norm_g, batch, seq):
    nc = seq // CHUNK
    hb = CHUNK // CONV_HALO
    row = lambda b, c: b * nc + c
    full = lambda shape: pl.BlockSpec(shape, lambda b, c: (0,) * len(shape))
    shift = jnp.asarray(_conv_shift_matrix(), BF16)
    return pl.pallas_call(
        _ssd_kernel,
        grid=(batch, nc),
        in_specs=[pl.BlockSpec((CHUNK, CONV_DIM), lambda b, c: (row(b, c), M_XBC // CONV_DIM)),
                  pl.BlockSpec((CONV_HALO, CONV_DIM),
                               lambda b, c: (jnp.maximum(row(b, c) * hb - 1, 0), M_XBC // CONV_DIM)),
                  pl.BlockSpec((CHUNK, SSM_INNER), lambda b, c: (row(b, c), M_Z // SSM_INNER)),
                  pl.BlockSpec((CHUNK, S_END), lambda b, c: (row(b, c), 0)),
                  full((CONV_WIDTH * CHUNK, CONV_HALO + CHUNK)),
                  full((CONV_WIDTH, CONV_DIM)), full((1, CONV_DIM)), full((1, SSM_HEADS)),
                  full((1, SSM_HEADS)), full((1, SSM_HEADS)), full((1, SSM_INNER))],
        out_specs=pl.BlockSpec((CHUNK, SSM_INNER), lambda b, c: (row(b, c), 0)),
        out_shape=jax.ShapeDtypeStruct((batch * seq, SSM_INNER), BF16),
        scratch_shapes=[pltpu.VMEM((SSM_GROUPS, CHUNK, GROUP_W), F32),
                        pltpu.VMEM((SSM_GROUPS, CHUNK, SSM_STATE), F32),
                        pltpu.VMEM((SSM_GROUPS, CHUNK, SSM_STATE), F32),
                        pltpu.VMEM((SSM_GROUPS, SSM_STATE, GROUP_W), F32),
                        pltpu.VMEM((SSM_GROUPS, CHUNK, GROUP_W), F32),
                        pltpu.VMEM((SSM_GROUPS, CHUNK, HEADS_PER_GROUP), F32),
                        pltpu.VMEM((SSM_GROUPS, HEADS_PER_GROUP, CHUNK), F32)],
        compiler_params=_params(("arbitrary", "arbitrary")),
        name="ssd_scan",
    )(main, main, main, small, shift, conv_w, conv_b, dt_bias, a_log, d_skip, norm_g)


def _merge_kernel(att_ref, y_ref, ga_ref, gs_ref, wa_ref, ws_ref, o_ref):
    pa = _dot(att_ref[...], wa_ref[...])
    ps = _dot(y_ref[...], ws_ref[...])
    o_ref[...] = (_sigmoid(ga_ref[...].astype(F32)) * pa + _sigmoid(gs_ref[...].astype(F32)) * ps).astype(o_ref.dtype)


def _merge(att, y, main, wa, ws, tm, tn):
    t = att.shape[0]
    return pl.pallas_call(
        _merge_kernel,
        grid=(t // tm, D_MODEL // tn),
        in_specs=[pl.BlockSpec((tm, ATT_Q), lambda i, j: (i, 0)),
                  pl.BlockSpec((tm, SSM_INNER), lambda i, j: (i, 0)),
                  pl.BlockSpec((tm, tn), lambda i, j: (i, M_GA // tn + j)),
                  pl.BlockSpec((tm, tn), lambda i, j: (i, M_GS // tn + j)),
                  pl.BlockSpec((ATT_Q, tn), lambda i, j: (0, j)),
                  pl.BlockSpec((SSM_INNER, tn), lambda i, j: (0, j))],
        out_specs=pl.BlockSpec((tm, tn), lambda i, j: (i, j)),
        out_shape=jax.ShapeDtypeStruct((t, D_MODEL), BF16),
        compiler_params=_params(("parallel", "arbitrary")),
        name="gated_merge",
    )(att, y, main, main, wa, ws)


def _out_proj_kernel(m_ref, w_ref, x_ref, g_ref, x1_ref, h2_ref):
    x1 = x_ref[...] + _dot(m_ref[...], w_ref[...])
    x1_ref[...] = x1
    ms = jnp.mean(x1 * x1, axis=-1, keepdims=True)
    h2_ref[...] = (x1 * lax.rsqrt(ms + EPS) * g_ref[...]).astype(BF16)


def _out_proj(merged, w, x, g, tm):
    t = x.shape[0]
    return pl.pallas_call(
        _out_proj_kernel,
        grid=(t // tm,),
        in_specs=[pl.BlockSpec((tm, D_MODEL), lambda i: (i, 0)),
                  pl.BlockSpec((D_MODEL, D_MODEL), lambda i: (0, 0)),
                  pl.BlockSpec((tm, D_MODEL), lambda i: (i, 0)),
                  pl.BlockSpec((1, D_MODEL), lambda i: (0, 0))],
        out_specs=[pl.BlockSpec((tm, D_MODEL), lambda i: (i, 0)),
                   pl.BlockSpec((tm, D_MODEL), lambda i: (i, 0))],
        out_shape=[jax.ShapeDtypeStruct((t, D_MODEL), F32),
                   jax.ShapeDtypeStruct((t, D_MODEL), BF16)],
        compiler_params=_params(("parallel",)),
        name="out_proj_norm",
    )(merged, w, x, g)


def _mlp_kernel(h_ref, x1_ref, wu_ref, wd_ref, o_ref):
    @pl.when(pl.program_id(1) == 0)
    def _():
        o_ref[...] = x1_ref[...]

    u = _dot(h_ref[...], wu_ref[...])
    u = jnp.square(jnp.maximum(u, 0.0)).astype(BF16)
    o_ref[...] += _dot(u, wd_ref[...])


def _mlp(h2, x1, wu, wd, tm, th):
    t = h2.shape[0]
    return pl.pallas_call(
        _mlp_kernel,
        grid=(t // tm, MLP_HIDDEN // th),
        in_specs=[pl.BlockSpec((tm, D_MODEL), lambda i, j: (i, 0)),
                  pl.BlockSpec((tm, D_MODEL), lambda i, j: (i, 0)),
                  pl.BlockSpec((D_MODEL, th), lambda i, j: (0, j)),
                  pl.BlockSpec((th, D_MODEL), lambda i, j: (j, 0))],
        out_specs=pl.BlockSpec((tm, D_MODEL), lambda i, j: (i, 0)),
        out_shape=jax.ShapeDtypeStruct((t, D_MODEL), F32),
        compiler_params=_params(("parallel", "arbitrary")),
        name="relu2_mlp",
    )(h2, x1, wu, wd)


def _pack_w_in(w):
    assert (O_GA, O_KI, O_Z, O_DT) == (M_GA, M_Z, O_WI + IDX_HEADS, O_Z + M_END - M_Z)
    return jnp.concatenate([w[:, O_KI:O_Z], w[:, O_DT:O_DT + SSM_HEADS],
                            jnp.zeros((w.shape[0], S_END - S_DT - SSM_HEADS), w.dtype)], axis=1).astype(BF16)


def _block(x2, batch, seq, norm1_g, w_in, conv_w, conv_b, dt_bias, a_log, d_skip, ssm_norm_g, q_norm_g,
           k_norm_g, rel_bias, w_att_branch, w_ssm_branch, w_out, norm2_g, w_up, w_down):
    row = lambda v: v.reshape(1, -1)
    xn, small = _norm_small(x2, row(norm1_g), _pack_w_in(w_in), tm=min(1024, x2.shape[0]))
    main = _in_proj(xn, w_in, tm=min(1024, x2.shape[0]))
    qt, kn, vt, qit, wt, ki = _qk_prep(main, small, row(q_norm_g), row(k_norm_g), batch, seq, tq=DSA_TQ)
    att = _dsa(qt, kn, vt, qit, wt, ki, rel_bias, batch, seq)
    y = _ssd(main, small, conv_w, row(conv_b), row(dt_bias), row(a_log), row(d_skip), row(ssm_norm_g),
             batch, seq)
    merged = _merge(att, y, main, w_att_branch.astype(BF16), w_ssm_branch.astype(BF16), tm=512, tn=1024)
    x1, h2 = _out_proj(merged, w_out.astype(BF16), x2, row(norm2_g), tm=512)
    return _mlp(h2, x1, w_up.astype(BF16), w_down.astype(BF16), tm=512, th=1024)


def kernel(x, norm1_g, w_in, conv_w, conv_b, dt_bias, a_log, d_skip, ssm_norm_g, q_norm_g, k_norm_g, rel_bias,
           w_att_branch, w_ssm_branch, w_out, norm2_g, w_up, w_down):
    batch, seq, d = x.shape
    x2 = x.reshape(batch * seq, d)
    for l in range(norm1_g.shape[0]):
        x2 = _block(x2, batch, seq, norm1_g[l], w_in[l], conv_w[l], conv_b[l], dt_bias[l], a_log[l], d_skip[l],
                    ssm_norm_g[l], q_norm_g[l], k_norm_g[l], rel_bias, w_att_branch[l], w_ssm_branch[l],
                    w_out[l], norm2_g[l], w_up[l], w_down[l])
    return x2.reshape(batch, seq, d)
```

```python
import functools
import math

import numpy as np
import jax
import jax.numpy as jnp
from jax import lax
from jax.experimental import pallas as pl
from jax.experimental.pallas import tpu as pltpu

F32 = jnp.float32
BF16 = jnp.bfloat16
I32 = jnp.int32

D_MODEL = 2048
ATT_HEADS = 16
ATT_KV_HEADS = 4
HEAD_DIM = 128
REP = ATT_HEADS // ATT_KV_HEADS
IDX_HEADS = 16
IDX_DIM = 64
TOPK_MAX = 256
N_BUCKETS = 32
MAX_DISTANCE = 128
SSM_INNER = 2 * D_MODEL
SSM_HEAD_DIM = 64
SSM_HEADS = SSM_INNER // SSM_HEAD_DIM
SSM_GROUPS = 8
SSM_STATE = 128
CONV_WIDTH = 4
CHUNK = 128
MLP_HIDDEN = 4 * D_MODEL
EPS = 1e-6

ATT_Q = ATT_HEADS * HEAD_DIM
ATT_KV = ATT_KV_HEADS * HEAD_DIM
IDX_Q = IDX_HEADS * IDX_DIM
SSM_BC = SSM_GROUPS * SSM_STATE
CONV_DIM = SSM_INNER + 2 * SSM_BC
SPLITS = (D_MODEL, D_MODEL, ATT_Q, ATT_KV, ATT_KV, IDX_Q, IDX_DIM, IDX_HEADS, SSM_INNER, CONV_DIM, SSM_HEADS)
_OFFS = tuple(int(v) for v in np.cumsum((0,) + SPLITS))
(O_GA, O_GS, O_Q, O_K, O_V, O_QI, O_KI, O_WI, O_Z, O_XBC, O_DT, _O_END) = _OFFS

M_GA = 0
M_GS = M_GA + D_MODEL
M_Q = M_GS + D_MODEL
M_K = M_Q + ATT_Q
M_V = M_K + ATT_KV
M_QI = M_V + ATT_KV
M_Z = M_QI + IDX_Q
M_XBC = M_Z + SSM_INNER
M_END = M_XBC + CONV_DIM
S_KI = 0
S_WI = S_KI + IDX_DIM
S_DT = S_WI + IDX_HEADS
S_END = 256

HEADS_PER_GROUP = SSM_HEADS // SSM_GROUPS
GROUP_W = HEADS_PER_GROUP * SSM_HEAD_DIM

LANES = 128
VMEM_LIMIT = 56 * 1024 * 1024

NEG_INF = float("-inf")
INT_MIN = -(2 ** 31)
KEY_NEG_INF = int(np.int32(np.uint32(0xFF800000) ^ np.uint32(0x7FFFFFFF)))


def _dot(a, b):
    return jnp.dot(a, b, preferred_element_type=F32)


def _dot_nt(a, b):
    return lax.dot_general(a, b, (((1,), (1,)), ((), ())), preferred_element_type=F32)


def _split3(x):
    hi = x.astype(BF16)
    r = x - hi.astype(F32)
    mid = r.astype(BF16)
    lo = (r - mid.astype(F32)).astype(BF16)
    return hi, mid, lo


def _sigmoid(x):
    return 1.0 / (1.0 + jnp.exp(-x))


def _params(sem):
    return pltpu.CompilerParams(dimension_semantics=sem, vmem_limit_bytes=VMEM_LIMIT)


def _norm_small_kernel(x_ref, g_ref, w_ref, xn_ref, sm_ref):
    x = x_ref[...]
    ms = jnp.mean(x * x, axis=-1, keepdims=True)
    xn = (x * lax.rsqrt(ms + EPS) * g_ref[...]).astype(BF16)
    xn_ref[...] = xn
    sm_ref[...] = _dot_nt(xn, w_ref[...].astype(BF16))


def _norm_small(x, g, w_small_t, tm):
    m, d = x.shape
    n = w_small_t.shape[0]
    return pl.pallas_call(
        _norm_small_kernel,
        grid=(m // tm,),
        in_specs=[pl.BlockSpec((tm, d), lambda i: (i, 0)),
                  pl.BlockSpec((1, d), lambda i: (0, 0)),
                  pl.BlockSpec((n, d), lambda i: (0, 0))],
        out_specs=[pl.BlockSpec((tm, d), lambda i: (i, 0)),
                   pl.BlockSpec((tm, n), lambda i: (i, 0))],
        out_shape=[jax.ShapeDtypeStruct((m, d), BF16), jax.ShapeDtypeStruct((m, n), F32)],
        compiler_params=_params(("parallel",)),
        name="norm_small_proj",
    )(x, g, w_small_t)


IN_TN = 1024
IN_ALIGNED_TILES = O_KI // IN_TN
IN_SHIFT = O_Z - M_Z
IN_NEXT = 128
IN_ROWS = 256


def _in_proj_kernel(xn_ref, wa_ref, wn_ref, o_ref, w_scr):
    j = pl.program_id(0)
    i = pl.program_id(1)

    @pl.when((i == 0) & (j < IN_ALIGNED_TILES))
    def _():
        for r0 in range(0, IN_TN, IN_ROWS):
            w_scr[r0:r0 + IN_ROWS, :] = wa_ref[r0:r0 + IN_ROWS, :].astype(BF16)

    @pl.when((i == 0) & (j >= IN_ALIGNED_TILES))
    def _():
        for r0 in range(0, IN_TN - IN_SHIFT, IN_ROWS):
            r1 = min(r0 + IN_ROWS, IN_TN - IN_SHIFT)
            w_scr[r0:r1, :] = wa_ref[r0 + IN_SHIFT:r1 + IN_SHIFT, :].astype(BF16)
        w_scr[IN_TN - IN_SHIFT:IN_TN, :] = wn_ref[0:IN_SHIFT, :].astype(BF16)

    o_ref[...] = _dot_nt(xn_ref[...], w_scr[...]).astype(o_ref.dtype)


def _in_proj(xn, w_t, tm):
    m, d = xn.shape
    assert O_KI % IN_TN == 0 and M_END % IN_TN == 0 and IN_TN % IN_NEXT == 0
    assert 0 < IN_SHIFT <= IN_NEXT and IN_SHIFT % 16 == 0
    return pl.pallas_call(
        _in_proj_kernel,
        grid=(M_END // IN_TN, m // tm),
        in_specs=[pl.BlockSpec((tm, d), lambda j, i: (i, 0)),
                  pl.BlockSpec((IN_TN, d), lambda j, i: (j, 0)),
                  pl.BlockSpec((IN_NEXT, d), lambda j, i: ((j + 1) * (IN_TN // IN_NEXT), 0))],
        out_specs=pl.BlockSpec((tm, IN_TN), lambda j, i: (i, j)),
        out_shape=jax.ShapeDtypeStruct((m, M_END), BF16),
        scratch_shapes=[pltpu.VMEM((IN_TN, d), BF16)],
        compiler_params=_params(("arbitrary", "arbitrary")),
        name="in_proj",
    )(xn, w_t, w_t)


LOG2E = math.log2(math.e)


def _qk_prep_kernel(q_ref, k_ref, v_ref, qi_ref, sm_ref, qg_ref, kg_ref,
                    qt_ref, kn_ref, vt_ref, qit_ref, wt_ref, ki_ref):
    qg = qg_ref[...]
    for h in range(ATT_HEADS):
        x = q_ref[:, h * HEAD_DIM:(h + 1) * HEAD_DIM].astype(F32)
        ms = jnp.mean(x * x, axis=-1, keepdims=True)
        y = x * lax.rsqrt(ms + EPS) * qg * (HEAD_DIM ** -0.5 * LOG2E)
        qt_ref[0, h] = y.T.astype(BF16)
    kg = kg_ref[...]
    for h in range(ATT_KV_HEADS):
        x = k_ref[:, h * HEAD_DIM:(h + 1) * HEAD_DIM].astype(F32)
        ms = jnp.mean(x * x, axis=-1, keepdims=True)
        kn_ref[:, h * HEAD_DIM:(h + 1) * HEAD_DIM] = (x * lax.rsqrt(ms + EPS) * kg).astype(BF16)
        vt_ref[0, h, 0] = v_ref[:, h * HEAD_DIM:(h + 1) * HEAD_DIM].astype(F32).T.astype(BF16)
    for p in range(IDX_Q // LANES):
        qit_ref[0, p * LANES:(p + 1) * LANES, :] = qi_ref[:, p * LANES:(p + 1) * LANES].astype(F32).T.astype(BF16)
    sm_t = sm_ref[:, 0:LANES].T
    wt_ref[0] = sm_t[S_WI:S_WI + IDX_HEADS, :] * (IDX_HEADS ** -0.5 * IDX_DIM ** -0.5)
    ki_ref[...] = sm_ref[:, S_KI:S_KI + IDX_DIM].astype(BF16)


def _qk_prep(main, small, qg, kg, batch, seq, tq):
    t = main.shape[0]
    nq = seq // tq
    return pl.pallas_call(
        _qk_prep_kernel,
        grid=(batch, nq),
        in_specs=[pl.BlockSpec((tq, ATT_Q), lambda b, i: (b * nq + i, M_Q // ATT_Q)),
                  pl.BlockSpec((tq, ATT_KV), lambda b, i: (b * nq + i, M_K // ATT_KV)),
                  pl.BlockSpec((tq, ATT_KV), lambda b, i: (b * nq + i, M_V // ATT_KV)),
                  pl.BlockSpec((tq, IDX_Q), lambda b, i: (b * nq + i, M_QI // IDX_Q)),
                  pl.BlockSpec((tq, S_END), lambda b, i: (b * nq + i, 0)),
                  pl.BlockSpec((1, HEAD_DIM), lambda b, i: (0, 0)),
                  pl.BlockSpec((1, HEAD_DIM), lambda b, i: (0, 0))],
        out_specs=[pl.BlockSpec((1, ATT_HEADS, HEAD_DIM, tq), lambda b, i: (b * nq + i, 0, 0, 0)),
                   pl.BlockSpec((tq, ATT_KV), lambda b, i: (b * nq + i, 0)),
                   pl.BlockSpec((1, ATT_KV_HEADS, 1, HEAD_DIM, tq), lambda b, i: (b, 0, i, 0, 0)),
                   pl.BlockSpec((1, IDX_Q, tq), lambda b, i: (b * nq + i, 0, 0)),
                   pl.BlockSpec((1, IDX_HEADS, tq), lambda b, i: (b * nq + i, 0, 0)),
                   pl.BlockSpec((tq, IDX_DIM), lambda b, i: (b * nq + i, 0))],
        out_shape=[jax.ShapeDtypeStruct((t // tq, ATT_HEADS, HEAD_DIM, tq), BF16),
                   jax.ShapeDtypeStruct((t, ATT_KV), BF16),
                   jax.ShapeDtypeStruct((batch, ATT_KV_HEADS, nq, HEAD_DIM, tq), BF16),
                   jax.ShapeDtypeStruct((t // tq, IDX_Q, tq), BF16),
                   jax.ShapeDtypeStruct((t // tq, IDX_HEADS, tq), F32),
                   jax.ShapeDtypeStruct((t, IDX_DIM), BF16)],
        compiler_params=_params(("parallel", "parallel")),
        name="qk_prep",
    )(main, main, main, main, small, qg, kg)


DSA_TQ = 256
DSA_TK = 256
CNT_WAYS = 4


def _t5_bucket_np(dist):
    n = np.maximum(dist, 0)
    max_exact = N_BUCKETS // 2
    nf = np.maximum(n, 1).astype(np.float32)
    ratio = (np.log(nf / np.float32(max_exact)) / np.float32(math.log(MAX_DISTANCE / max_exact))
             * np.float32(N_BUCKETS - max_exact))
    large = max_exact + ratio.astype(np.int32)
    large = np.minimum(large, N_BUCKETS - 1)
    return np.where(n < max_exact, n, large).astype(np.int32)


def _bias_bucket_tiles(tq, tk):
    r = np.arange(tq)[None, :]
    c = np.arange(tk)[:, None]
    d0 = _t5_bucket_np(r - c)
    d1 = _t5_bucket_np(tk + r - c)
    assert np.all(_t5_bucket_np(np.arange(tk + 1, 8 * tk)) == N_BUCKETS - 1)
    return np.stack([d0, d1]).astype(np.int32)


def _dsa_kernel(relb_ref, bidx_ref, qt_ref, qit_ref, wt_ref, k_ref, vt_ref, ki_ref, o_ref,
                key_ref, madd_ref, bias_ref, s_buf, p_buf, st_ref, acc_ref, *, tq, tk, nkc, topk):
    b = pl.program_id(0)
    i = pl.program_id(1)
    g = pl.program_id(2)
    neg_slot = nkc

    @pl.when((b == 0) & (i == 0) & (g == 0))
    def _init():
        madd_ref[neg_slot] = jnp.full((tk, tq), NEG_INF, F32)
        for t in range(2):
            bt = bidx_ref[t]

            def head_body(h, carry):
                far = relb_ref[N_BUCKETS - 1, h]

                def bucket_body(bk, acc):
                    return jnp.where(bt == bk, (relb_ref[bk, h] - far) * LOG2E, acc)

                bias_ref[t, h] = lax.fori_loop(0, N_BUCKETS, bucket_body, jnp.zeros((tk, tq), F32))
                return carry

            lax.fori_loop(0, ATT_HEADS, head_body, 0)

    @pl.when(g == 0)
    def _select():
        nj = i + 1
        qpos = i * tq + lax.broadcasted_iota(I32, (tk, tq), 1)

        def score_chunk(j, carry):
            kc = ki_ref[pl.ds(pl.multiple_of(j * tk, tk), tk), :]
            acc = jnp.zeros((tk, tq), F32)
            for h in range(IDX_HEADS):
                z = _dot(kc, qit_ref[0, h * IDX_DIM:(h + 1) * IDX_DIM, :])
                acc = acc + jnp.maximum(z, 0.0) * wt_ref[0, h:h + 1, :]
            kpos = j * tk + lax.broadcasted_iota(I32, (tk, tq), 0)
            acc = jnp.where(kpos <= qpos, acc, NEG_INF)
            bits = pltpu.bitcast(acc, I32)
            key_ref[j] = bits ^ ((bits >> 31) & 0x7FFFFFFF)
            return carry

        lax.fori_loop(0, nj, score_chunk, 0)

        def bit_iter(it, prefix):
            cand = prefix + lax.shift_left(jnp.int32(1), 31 - it)

            def cnt_chunk(j, cnt):
                hit = jnp.where(key_ref[j] >= cand, 1.0, 0.0)
                return cnt + jnp.sum(hit.reshape(tk // (8 * CNT_WAYS), CNT_WAYS * 8, tq), axis=0)

            cnt = lax.fori_loop(0, nj, cnt_chunk, jnp.zeros((CNT_WAYS * 8, tq), F32))
            tot = jnp.sum(cnt, axis=0, keepdims=True)
            return jnp.where(tot >= float(topk), cand, prefix)

        thr = lax.fori_loop(0, 32, bit_iter, jnp.full((1, tq), INT_MIN, I32))
        thr = jnp.maximum(thr, KEY_NEG_INF + 1)

        def madd_chunk(j, carry):
            madd_ref[j] = jnp.where(key_ref[j] >= thr, 0.0, NEG_INF)
            return carry

        lax.fori_loop(0, nj, madd_chunk, 0)

    qt = jnp.concatenate([qt_ref[0, r] for r in range(REP)], axis=1)
    c0 = jnp.maximum(i - 1, 0)
    c1 = jnp.minimum(c0 + 1, nkc - 1)
    n_far = (c0 + 1) // 2

    def scores_to(slot, ca, madd_c, bias_c):
        rows = pl.ds(pl.multiple_of(ca * tk, tk), 2 * tk)
        add = jnp.concatenate([madd_c if bias_c is None else madd_c + bias_c[r] for r in range(REP)], axis=1)
        s_buf[slot] = _dot(k_ref[rows, :], qt) + add

    def softmax_to(slot):
        m = st_ref[0:1, :]
        m_new = jnp.maximum(m, jnp.max(s_buf[slot], axis=0, keepdims=True))
        m_safe = jnp.where(m_new == NEG_INF, 0.0, m_new)
        alpha = jnp.exp2(m - m_safe)
        p = jnp.exp2(s_buf[slot] - m_safe)
        st_ref[0:1, :] = m_new
        st_ref[1:2, :] = alpha * st_ref[1:2, :] + jnp.sum(p, axis=0, keepdims=True)
        st_ref[2 + slot:3 + slot, :] = alpha
        p_buf[slot] = p.astype(BF16)

    def values_from(slot, ca, cb):
        vt_c = jnp.concatenate([vt_ref[0, 0, ca], vt_ref[0, 0, cb]], axis=1)
        acc_ref[...] = st_ref[2 + slot:3 + slot, :] * acc_ref[...] + _dot(vt_c, p_buf[slot])

    def value_chunks(k):
        return jnp.where(k == 0, c0, 2 * (k - 1)), jnp.where(k == 0, c1, 2 * (k - 1) + 1)

    def far_scores_to(slot, k):
        ca = jnp.minimum(2 * (k - 1), nkc - 2)
        cb = jnp.where(ca + 1 >= c0, neg_slot, ca + 1)
        scores_to(slot, ca, jnp.concatenate([madd_ref[ca], madd_ref[cb]], axis=0), None)

    def far_stage(k, cur, nxt):
        far_scores_to(nxt, k + 1)
        values_from(nxt, *value_chunks(k - 1))
        softmax_to(cur)

    st_ref[0:1, :] = jnp.full((1, REP * tq), NEG_INF, F32)
    st_ref[1:2, :] = jnp.zeros((1, REP * tq), F32)
    acc_ref[...] = jnp.zeros_like(acc_ref)

    first = i == 0
    t_a = jnp.where(first, 0, 1)
    idx_b = jnp.where(first, neg_slot, c0 + 1)
    madd_near = jnp.concatenate([madd_ref[c0], madd_ref[idx_b]], axis=0)
    bias_near = [jnp.concatenate([bias_ref[t_a, g * REP + r], bias_ref[0, g * REP + r]], axis=0)
                 for r in range(REP)]
    scores_to(0, c0, madd_near, bias_near)
    far_scores_to(1, 1)
    softmax_to(0)

    def far_pair(u, carry):
        far_stage(2 * u + 1, 1, 0)
        far_stage(2 * u + 2, 0, 1)
        return carry

    lax.fori_loop(0, n_far // 2, far_pair, 0)
    odd = n_far % 2 == 1

    @pl.when(odd)
    def _():
        far_stage(n_far, 1, 0)
        values_from(1, *value_chunks(n_far))

    @pl.when(jnp.logical_not(odd))
    def _():
        values_from(0, *value_chunks(n_far))

    out = acc_ref[...] / st_ref[1:2, :]
    for r in range(REP):
        o_ref[:, r * HEAD_DIM:(r + 1) * HEAD_DIM] = out[:, r * tq:(r + 1) * tq].T.astype(o_ref.dtype)


def _dsa(qt, kn, vt, qit, wt, ki, rel_bias, batch, seq):
    tq, tk = DSA_TQ, DSA_TK
    nq = seq // tq
    nkc = seq // tk
    topk = min(TOPK_MAX, seq // 4)
    bidx = jnp.asarray(_bias_bucket_tiles(tq, tk))
    kern = functools.partial(_dsa_kernel, tq=tq, tk=tk, nkc=nkc, topk=topk)
    gw = REP * HEAD_DIM
    return pl.pallas_call(
        kern,
        grid=(batch, nq, ATT_KV_HEADS),
        in_specs=[pl.BlockSpec(memory_space=pltpu.SMEM),
                  pl.BlockSpec((2, tk, tq), lambda b, i, g: (0, 0, 0)),
                  pl.BlockSpec((1, REP, HEAD_DIM, tq), lambda b, i, g: (b * nq + i, g, 0, 0)),
                  pl.BlockSpec((1, IDX_Q, tq), lambda b, i, g: (b * nq + i, 0, 0)),
                  pl.BlockSpec((1, IDX_HEADS, tq), lambda b, i, g: (b * nq + i, 0, 0)),
                  pl.BlockSpec((seq, HEAD_DIM), lambda b, i, g: (b, g)),
                  pl.BlockSpec((1, 1, nkc, HEAD_DIM, tk), lambda b, i, g: (b, g, 0, 0, 0)),
                  pl.BlockSpec((seq, IDX_DIM), lambda b, i, g: (b, 0))],
        out_specs=pl.BlockSpec((tq, gw), lambda b, i, g: (b * nq + i, g)),
        out_shape=jax.ShapeDtypeStruct((batch * seq, ATT_Q), BF16),
        scratch_shapes=[pltpu.VMEM((nkc, tk, tq), I32),
                        pltpu.VMEM((nkc + 1, tk, tq), F32),
                        pltpu.VMEM((2, ATT_HEADS, tk, tq), F32),
                        pltpu.VMEM((2, 2 * tk, REP * tq), F32),
                        pltpu.VMEM((2, 2 * tk, REP * tq), BF16),
                        pltpu.VMEM((8, REP * tq), F32),
                        pltpu.VMEM((HEAD_DIM, REP * tq), F32)],
        compiler_params=_params(("arbitrary", "arbitrary", "arbitrary")),
        name="dsa_attention",
    )(rel_bias, bidx, qt, qit, wt, kn, vt, ki)


E_ROWS = 3 * CHUNK + 16
CONV_HALO = 16


def _conv_shift_matrix():
    s = np.zeros((CONV_WIDTH * CHUNK, CONV_HALO + CHUNK), np.float32)
    for k in range(CONV_WIDTH):
        t = np.arange(CHUNK)
        s[k * CHUNK + t, CONV_HALO + t - (CONV_WIDTH - 1) + k] = 1.0
    return s


def _ssd_kernel(xbc_ref, halo_ref, z_ref, sm_ref, shift_ref, cw_ref, cb_ref, dtb_ref, alog_ref, dsk_ref, ng_ref,
                o_ref, xs_ref, bm_ref, cm_ref, state_ref, ypre_ref, acg_ref, actg_ref):
    c = pl.program_id(1)
    L = CHUNK

    @pl.when(c == 0)
    def _():
        state_ref[...] = jnp.zeros_like(state_ref)

    halo_on = (c > 0).astype(BF16)
    shift = shift_ref[...]
    cblk = GROUP_W
    for cbi in range(CONV_DIM // cblk):
        cols = slice(cbi * cblk, (cbi + 1) * cblk)
        ext = jnp.concatenate([halo_ref[:, cols] * halo_on, xbc_ref[:, cols]], axis=0)
        taps = _dot(shift, ext)
        acc = cb_ref[:, cols]
        for kk in range(CONV_WIDTH):
            acc = acc + cw_ref[kk:kk + 1, cols] * taps[kk * L:(kk + 1) * L]
        y = acc * _sigmoid(acc)
        if cbi < SSM_GROUPS:
            xs_ref[cbi] = y
        else:
            per = cblk // SSM_STATE
            for u in range(per):
                gi = (cbi - SSM_GROUPS) * per + u
                piece = y[:, u * SSM_STATE:(u + 1) * SSM_STATE]
                if gi < SSM_GROUPS:
                    bm_ref[gi] = piece
                else:
                    cm_ref[gi - SSM_GROUPS] = piece

    dt_in = sm_ref[:, S_DT:S_DT + SSM_HEADS] + dtb_ref[...]
    dt_act = jnp.maximum(dt_in, 0.0) + jnp.log1p(jnp.exp(-jnp.abs(dt_in)))
    a = dt_act * (-jnp.exp(alog_ref[...]))
    ri = lax.broadcasted_iota(I32, (L, L), 0)
    ci = lax.broadcasted_iota(I32, (L, L), 1)
    tril = ri >= ci
    tri_b = jnp.where(tril, 1.0, 0.0).astype(BF16)
    a_cum = sum(_dot(tri_b, p) for p in _split3(a))
    eye_b = jnp.where(lax.broadcasted_iota(I32, (SSM_HEADS, SSM_HEADS), 0)
                      == lax.broadcasted_iota(I32, (SSM_HEADS, SSM_HEADS), 1), 1.0, 0.0).astype(BF16)
    a_cum_t = sum(_dot_nt(eye_b, p) for p in _split3(a_cum))
    a_last = a_cum[L - 1:L, :]
    for gi in range(SSM_GROUPS):
        acg_ref[gi] = a_cum[:, gi * HEADS_PER_GROUP:(gi + 1) * HEADS_PER_GROUP]
        actg_ref[gi] = a_cum_t[gi * HEADS_PER_GROUP:(gi + 1) * HEADS_PER_GROUP, :]
    cd3 = _split3(jnp.exp(a_last))
    ds3 = _split3(dsk_ref[...])
    extras = jnp.concatenate([p.astype(F32) for p in cd3 + ds3]
                             + [jnp.zeros((E_ROWS - 3 * L - 6, SSM_HEADS), F32)], axis=0)
    e_mat = jnp.concatenate([dt_act, jnp.exp(a_cum), jnp.exp(a_last - a_cum), extras], axis=0).astype(BF16)

    lane = lax.broadcasted_iota(I32, (L, LANES), 1)
    lo_mask = lane < SSM_HEAD_DIM

    def group_body(gi, carry):
        xs = xs_ref[gi]
        bg = bm_ref[gi]
        cg_b = cm_ref[gi].astype(BF16)
        hsel = (lax.broadcasted_iota(I32, (SSM_HEADS, GROUP_W), 0)
                == gi * HEADS_PER_GROUP + lax.broadcasted_iota(I32, (SSM_HEADS, GROUP_W), 1) // SSM_HEAD_DIM)
        ex = _dot(e_mat, jnp.where(hsel, 1.0, 0.0).astype(BF16))
        dt_rep = ex[0:L]
        expa_rep = ex[L:2 * L]
        dte_rep = ex[2 * L:3 * L]
        cd_rep = ex[3 * L:3 * L + 1] + ex[3 * L + 1:3 * L + 2] + ex[3 * L + 2:3 * L + 3]
        dsk_rep = ex[3 * L + 3:3 * L + 4] + ex[3 * L + 4:3 * L + 5] + ex[3 * L + 5:3 * L + 6]

        xd = xs * dt_rep
        xd_b = xd.astype(BF16)
        cb = _dot_nt(cg_b, bg.astype(BF16))
        acg = acg_ref[gi]
        actg = actg_ref[gi]
        pairs = []
        for pj in range(HEADS_PER_GROUP // 2):
            gmat = []
            for e in (2 * pj, 2 * pj + 1):
                seg = acg[:, e:e + 1] - actg[e:e + 1, :]
                dec = jnp.exp(jnp.where(tril, seg, NEG_INF))
                gmat.append((cb * dec).astype(BF16))
            xp = xd_b[:, pj * LANES:(pj + 1) * LANES]
            zero = jnp.zeros_like(xp)
            pairs.append(_dot(gmat[0], jnp.where(lo_mask, xp, zero))
                         + _dot(gmat[1], jnp.where(lo_mask, zero, xp)))
        y_diag = jnp.concatenate(pairs, axis=1)

        st = state_ref[gi]
        y_off = _dot(cg_b, st.astype(BF16)) * expa_rep
        xdd = (xd * dte_rep).astype(BF16)
        state_ref[gi] = st * cd_rep + _dot(bg.T.astype(BF16), xdd)
        ypre_ref[gi] = y_diag + y_off + dsk_rep * xs
        return carry

    lax.fori_loop(0, SSM_GROUPS, group_body, 0, unroll=2)

    for gi in range(SSM_GROUPS):
        cols = slice(gi * GROUP_W, (gi + 1) * GROUP_W)
        zz = z_ref[:, cols].astype(F32)
        y = ypre_ref[gi] * (zz * _sigmoid(zz))
        ms = jnp.mean(y * y, axis=-1, keepdims=True)
        o_ref[:, cols] = (y * lax.rsqrt(ms + EPS) * ng_ref[:, cols]).astype(o_ref.dtype)


def _ssd(main, small, conv_w, conv_b, dt_bias, a_log, d_skip, norm_g, batch, seq):
    nc = seq // CHUNK
    hb = CHUNK // CONV_HALO
    row = lambda b, c: b * nc + c
    full = lambda shape: pl.BlockSpec(shape, lambda b, c: (0,) * len(shape))
    shift = jnp.asarray(_conv_shift_matrix(), BF16)
    return pl.pallas_call(
        _ssd_kernel,
        grid=(batch, nc),
        in_specs=[pl.BlockSpec((CHUNK, CONV_DIM), lambda b, c: (row(b, c), M_XBC // CONV_DIM)),
                  pl.BlockSpec((CONV_HALO, CONV_DIM),
                               lambda b, c: (jnp.maximum(row(b, c) * hb - 1, 0), M_XBC // CONV_DIM)),
                  pl.BlockSpec((CHUNK, SSM_INNER), lambda b, c: (row(b, c), M_Z // SSM_INNER)),
                  pl.BlockSpec((CHUNK, S_END), lambda b, c: (row(b, c), 0)),
                  full((CONV_WIDTH * CHUNK, CONV_HALO + CHUNK)),
                  full((CONV_WIDTH, CONV_DIM)), full((1, CONV_DIM)), full((1, SSM_HEADS)),
                  full((1, SSM_HEADS)), full((1, SSM_HEADS)), full((1, SSM_INNER))],
        out_specs=pl.BlockSpec((CHUNK, SSM_INNER), lambda b, c: (row(b, c), 0)),
        out_shape=jax.ShapeDtypeStruct((batch * seq, SSM_INNER), BF16),
        scratch_shapes=[pltpu.VMEM((SSM_GROUPS, CHUNK, GROUP_W), F32),
                        pltpu.VMEM((SSM_GROUPS, CHUNK, SSM_STATE), F32),
                        pltpu.VMEM((SSM_GROUPS, CHUNK, SSM_STATE), F32),
                        pltpu.VMEM((SSM_GROUPS, SSM_STATE, GROUP_W), F32),
                        pltpu.VMEM((SSM_GROUPS, CHUNK, GROUP_W), F32),
                        pltpu.VMEM((SSM_GROUPS, CHUNK, HEADS_PER_GROUP), F32),
                        pltpu.VMEM((SSM_GROUPS, HEADS_PER_GROUP, CHUNK), F32)],
        compiler_params=_params(("arbitrary", "arbitrary")),
        name="ssd_scan",
    )(main, main, main, small, shift, conv_w, conv_b, dt_bias, a_log, d_skip, norm_g)


def _merge_kernel(att_ref, y_ref, ga_ref, gs_ref, wa_ref, ws_ref, o_ref):
    pa = _dot(att_ref[...], wa_ref[...])
    ps = _dot(y_ref[...], ws_ref[...])
    o_ref[...] = (_sigmoid(ga_ref[...].astype(F32)) * pa + _sigmoid(gs_ref[...].astype(F32)) * ps).astype(o_ref.dtype)


def _merge(att, y, main, wa, ws, tm, tn):
    t = att.shape[0]
    return pl.pallas_call(
        _merge_kernel,
        grid=(t // tm, D_MODEL // tn),
        in_specs=[pl.BlockSpec((tm, ATT_Q), lambda i, j: (i, 0)),
                  pl.BlockSpec((tm, SSM_INNER), lambda i, j: (i, 0)),
                  pl.BlockSpec((tm, tn), lambda i, j: (i, M_GA // tn + j)),
                  pl.BlockSpec((tm, tn), lambda i, j: (i, M_GS // tn + j)),
                  pl.BlockSpec((ATT_Q, tn), lambda i, j: (0, j)),
                  pl.BlockSpec((SSM_INNER, tn), lambda i, j: (0, j))],
        out_specs=pl.BlockSpec((tm, tn), lambda i, j: (i, j)),
        out_shape=jax.ShapeDtypeStruct((t, D_MODEL), BF16),
        compiler_params=_params(("parallel", "arbitrary")),
        name="gated_merge",
    )(att, y, main, main, wa, ws)


def _out_proj_kernel(m_ref, w_ref, x_ref, g_ref, x1_ref, h2_ref):
    x1 = x_ref[...] + _dot(m_ref[...], w_ref[...])
    x1_ref[...] = x1
    ms = jnp.mean(x1 * x1, axis=-1, keepdims=True)
    h2_ref[...] = (x1 * lax.rsqrt(ms + EPS) * g_ref[...]).astype(BF16)


def _out_proj(merged, w, x, g, tm):
    t = x.shape[0]
    return pl.pallas_call(
        _out_proj_kernel,
        grid=(t // tm,),
        in_specs=[pl.BlockSpec((tm, D_MODEL), lambda i: (i, 0)),
                  pl.BlockSpec((D_MODEL, D_MODEL), lambda i: (0, 0)),
                  pl.BlockSpec((tm, D_MODEL), lambda i: (i, 0)),
                  pl.BlockSpec((1, D_MODEL), lambda i: (0, 0))],
        out_specs=[pl.BlockSpec((tm, D_MODEL), lambda i: (i, 0)),
                   pl.BlockSpec((tm, D_MODEL), lambda i: (i, 0))],
        out_shape=[jax.ShapeDtypeStruct((t, D_MODEL), F32),
                   jax.ShapeDtypeStruct((t, D_MODEL), BF16)],
        compiler_params=_params(("parallel",)),
        name="out_proj_norm",
    )(merged, w, x, g)


def _mlp_kernel(h_ref, x1_ref, wu_ref, wd_ref, o_ref):
    @pl.when(pl.program_id(1) == 0)
    def _():
        o_ref[...] = x1_ref[...]

    u = _dot(h_ref[...], wu_ref[...])
    u = jnp.square(jnp.maximum(u, 0.0)).astype(BF16)
    o_ref[...] += _dot(u, wd_ref[...])


def _mlp(h2, x1, wu, wd, tm, th):
    t = h2.shape[0]
    return pl.pallas_call(
        _mlp_kernel,
        grid=(t // tm, MLP_HIDDEN // th),
        in_specs=[pl.BlockSpec((tm, D_MODEL), lambda i, j: (i, 0)),
                  pl.BlockSpec((tm, D_MODEL), lambda i, j: (i, 0)),
                  pl.BlockSpec((D_MODEL, th), lambda i, j: (0, j)),
                  pl.BlockSpec((th, D_MODEL), lambda i, j: (j, 0))],
        out_specs=pl.BlockSpec((tm, D_MODEL), lambda i, j: (i, 0)),
        out_shape=jax.ShapeDtypeStruct((t, D_MODEL), F32),
        compiler_params=_params(("parallel", "arbitrary")),
        name="relu2_mlp",
    )(h2, x1, wu, wd)


def _pack_w_small(w_t):
    assert (O_GA, O_KI, O_Z, O_DT) == (M_GA, M_Z, O_WI + IDX_HEADS, O_Z + M_END - M_Z)
    return jnp.concatenate([w_t[O_KI:O_Z], w_t[O_DT:O_DT + SSM_HEADS],
                            jnp.zeros((S_END - S_DT - SSM_HEADS, w_t.shape[1]), w_t.dtype)], axis=0)


def _block(x2, batch, seq, norm1_g, w_in, conv_w, conv_b, dt_bias, a_log, d_skip, ssm_norm_g, q_norm_g,
           k_norm_g, rel_bias, w_att_branch, w_ssm_branch, w_out, norm2_g, w_up, w_down):
    row = lambda v: v.reshape(1, -1)
    w_t = w_in.T
    xn, small = _norm_small(x2, row(norm1_g), _pack_w_small(w_t), tm=min(1024, x2.shape[0]))
    main = _in_proj(xn, w_t, tm=min(1024, x2.shape[0]))
    qt, kn, vt, qit, wt, ki = _qk_prep(main, small, row(q_norm_g), row(k_norm_g), batch, seq, tq=DSA_TQ)
    att = _dsa(qt, kn, vt, qit, wt, ki, rel_bias, batch, seq)
    y = _ssd(main, small, conv_w, row(conv_b), row(dt_bias), row(a_log), row(d_skip), row(ssm_norm_g),
             batch, seq)
    merged = _merge(att, y, main, w_att_branch.astype(BF16), w_ssm_branch.astype(BF16), tm=512, tn=1024)
    x1, h2 = _out_proj(merged, w_out.astype(BF16), x2, row(norm2_g), tm=512)
    return _mlp(h2, x1, w_up.astype(BF16), w_down.astype(BF16), tm=512, th=1024)


def kernel(x, norm1_g, w_in, conv_w, conv_b, dt_bias, a_log, d_skip, ssm_norm_g, q_norm_g, k_norm_g, rel_bias,
           w_att_branch, w_ssm_branch, w_out, norm2_g, w_up, w_down):
    batch, seq, d = x.shape
    x2 = x.reshape(batch * seq, d)
    for l in range(norm1_g.shape[0]):
        x2 = _block(x2, batch, seq, norm1_g[l], w_in[l], conv_w[l], conv_b[l], dt_bias[l], a_log[l], d_skip[l],
                    ssm_norm_g[l], q_norm_g[l], k_norm_g[l], rel_bias, w_att_branch[l], w_ssm_branch[l],
                    w_out[l], norm2_g[l], w_up[l], w_down[l])
    return x2.reshape(batch, seq, d)
```

```python
import functools
import math

import numpy as np
import jax
import jax.numpy as jnp
from jax import lax
from jax.experimental import pallas as pl
from jax.experimental.pallas import tpu as pltpu

F32 = jnp.float32
BF16 = jnp.bfloat16
I32 = jnp.int32

D_MODEL = 2048
ATT_HEADS = 16
ATT_KV_HEADS = 4
HEAD_DIM = 128
REP = ATT_HEADS // ATT_KV_HEADS
IDX_HEADS = 16
IDX_DIM = 64
TOPK_MAX = 256
N_BUCKETS = 32
MAX_DISTANCE = 128
SSM_INNER = 2 * D_MODEL
SSM_HEAD_DIM = 64
SSM_HEADS = SSM_INNER // SSM_HEAD_DIM
SSM_GROUPS = 8
SSM_STATE = 128
CONV_WIDTH = 4
CHUNK = 128
MLP_HIDDEN = 4 * D_MODEL
EPS = 1e-6

ATT_Q = ATT_HEADS * HEAD_DIM
ATT_KV = ATT_KV_HEADS * HEAD_DIM
IDX_Q = IDX_HEADS * IDX_DIM
SSM_BC = SSM_GROUPS * SSM_STATE
CONV_DIM = SSM_INNER + 2 * SSM_BC
SPLITS = (D_MODEL, D_MODEL, ATT_Q, ATT_KV, ATT_KV, IDX_Q, IDX_DIM, IDX_HEADS, SSM_INNER, CONV_DIM, SSM_HEADS)
_OFFS = tuple(int(v) for v in np.cumsum((0,) + SPLITS))
(O_GA, O_GS, O_Q, O_K, O_V, O_QI, O_KI, O_WI, O_Z, O_XBC, O_DT, _O_END) = _OFFS

M_GA = 0
M_GS = M_GA + D_MODEL
M_Q = M_GS + D_MODEL
M_K = M_Q + ATT_Q
M_V = M_K + ATT_KV
M_QI = M_V + ATT_KV
M_Z = M_QI + IDX_Q
M_XBC = M_Z + SSM_INNER
M_END = M_XBC + CONV_DIM
S_KI = 0
S_WI = S_KI + IDX_DIM
S_DT = S_WI + IDX_HEADS
S_END = 256

HEADS_PER_GROUP = SSM_HEADS // SSM_GROUPS
GROUP_W = HEADS_PER_GROUP * SSM_HEAD_DIM

LANES = 128
VMEM_LIMIT = 56 * 1024 * 1024

NEG_INF = float("-inf")
INT_MIN = -(2 ** 31)
KEY_NEG_INF = int(np.int32(np.uint32(0xFF800000) ^ np.uint32(0x7FFFFFFF)))


def _dot(a, b):
    return jnp.dot(a, b, preferred_element_type=F32)


def _dot_nt(a, b):
    return lax.dot_general(a, b, (((1,), (1,)), ((), ())), preferred_element_type=F32)


def _split3(x):
    hi = x.astype(BF16)
    r = x - hi.astype(F32)
    mid = r.astype(BF16)
    lo = (r - mid.astype(F32)).astype(BF16)
    return hi, mid, lo


def _sigmoid(x):
    return 1.0 / (1.0 + jnp.exp(-x))


def _params(sem):
    return pltpu.CompilerParams(dimension_semantics=sem, vmem_limit_bytes=VMEM_LIMIT)


def _norm_small_kernel(x_ref, g_ref, w_ref, xn_ref, sm_ref):
    x = x_ref[...]
    ms = jnp.mean(x * x, axis=-1, keepdims=True)
    xn = (x * lax.rsqrt(ms + EPS) * g_ref[...]).astype(BF16)
    xn_ref[...] = xn
    sm_ref[...] = _dot_nt(xn, w_ref[...].astype(BF16))


def _norm_small(x, g, w_small_t, tm):
    m, d = x.shape
    n = w_small_t.shape[0]
    return pl.pallas_call(
        _norm_small_kernel,
        grid=(m // tm,),
        in_specs=[pl.BlockSpec((tm, d), lambda i: (i, 0)),
                  pl.BlockSpec((1, d), lambda i: (0, 0)),
                  pl.BlockSpec((n, d), lambda i: (0, 0))],
        out_specs=[pl.BlockSpec((tm, d), lambda i: (i, 0)),
                   pl.BlockSpec((tm, n), lambda i: (i, 0))],
        out_shape=[jax.ShapeDtypeStruct((m, d), BF16), jax.ShapeDtypeStruct((m, n), F32)],
        compiler_params=_params(("parallel",)),
        name="norm_small_proj",
    )(x, g, w_small_t)


IN_TN = 1024
IN_ALIGNED_TILES = O_KI // IN_TN
IN_SHIFT = O_Z - M_Z
IN_NEXT = 128
IN_ROWS = 256


def _in_proj_kernel(xn_ref, wa_ref, wn_ref, o_ref, w_scr):
    j = pl.program_id(0)
    i = pl.program_id(1)

    @pl.when((i == 0) & (j < IN_ALIGNED_TILES))
    def _():
        for r0 in range(0, IN_TN, IN_ROWS):
            w_scr[r0:r0 + IN_ROWS, :] = wa_ref[r0:r0 + IN_ROWS, :].astype(BF16)

    @pl.when((i == 0) & (j >= IN_ALIGNED_TILES))
    def _():
        for r0 in range(0, IN_TN - IN_SHIFT, IN_ROWS):
            r1 = min(r0 + IN_ROWS, IN_TN - IN_SHIFT)
            w_scr[r0:r1, :] = wa_ref[r0 + IN_SHIFT:r1 + IN_SHIFT, :].astype(BF16)
        w_scr[IN_TN - IN_SHIFT:IN_TN, :] = wn_ref[0:IN_SHIFT, :].astype(BF16)

    o_ref[...] = _dot_nt(xn_ref[...], w_scr[...]).astype(o_ref.dtype)


def _in_proj(xn, w_t, tm):
    m, d = xn.shape
    assert O_KI % IN_TN == 0 and M_END % IN_TN == 0 and IN_TN % IN_NEXT == 0
    assert 0 < IN_SHIFT <= IN_NEXT and IN_SHIFT % 16 == 0
    return pl.pallas_call(
        _in_proj_kernel,
        grid=(M_END // IN_TN, m // tm),
        in_specs=[pl.BlockSpec((tm, d), lambda j, i: (i, 0)),
                  pl.BlockSpec((IN_TN, d), lambda j, i: (j, 0)),
                  pl.BlockSpec((IN_NEXT, d), lambda j, i: ((j + 1) * (IN_TN // IN_NEXT), 0))],
        out_specs=pl.BlockSpec((tm, IN_TN), lambda j, i: (i, j)),
        out_shape=jax.ShapeDtypeStruct((m, M_END), BF16),
        scratch_shapes=[pltpu.VMEM((IN_TN, d), BF16)],
        compiler_params=_params(("arbitrary", "arbitrary")),
        name="in_proj",
    )(xn, w_t, w_t)


LOG2E = math.log2(math.e)
VT_ROWS = HEAD_DIM + 16


def _qk_prep_kernel(q_ref, k_ref, v_ref, qi_ref, sm_ref, qg_ref, kg_ref,
                    qt_ref, kn_ref, vt_ref, qit_ref, wt_ref, ki_ref):
    qg = qg_ref[...]
    for h in range(ATT_HEADS):
        x = q_ref[:, h * HEAD_DIM:(h + 1) * HEAD_DIM].astype(F32)
        ms = jnp.mean(x * x, axis=-1, keepdims=True)
        y = x * lax.rsqrt(ms + EPS) * qg * (HEAD_DIM ** -0.5 * LOG2E)
        qt_ref[0, h] = y.T.astype(BF16)
    kg = kg_ref[...]
    for h in range(ATT_KV_HEADS):
        x = k_ref[:, h * HEAD_DIM:(h + 1) * HEAD_DIM].astype(F32)
        ms = jnp.mean(x * x, axis=-1, keepdims=True)
        kn_ref[:, h * HEAD_DIM:(h + 1) * HEAD_DIM] = (x * lax.rsqrt(ms + EPS) * kg).astype(BF16)
        vt_ref[0, h, 0, 0:HEAD_DIM, :] = v_ref[:, h * HEAD_DIM:(h + 1) * HEAD_DIM].astype(F32).T.astype(BF16)
        vt_ref[0, h, 0, HEAD_DIM:VT_ROWS, :] = jnp.ones((VT_ROWS - HEAD_DIM, v_ref.shape[0]), BF16)
    for p in range(IDX_Q // LANES):
        qit_ref[0, p * LANES:(p + 1) * LANES, :] = qi_ref[:, p * LANES:(p + 1) * LANES].astype(F32).T.astype(BF16)
    sm_t = sm_ref[:, 0:LANES].T
    wt_ref[0] = sm_t[S_WI:S_WI + IDX_HEADS, :] * (IDX_HEADS ** -0.5 * IDX_DIM ** -0.5)
    ki_ref[...] = sm_ref[:, S_KI:S_KI + IDX_DIM].astype(BF16)


def _qk_prep(main, small, qg, kg, batch, seq, tq):
    t = main.shape[0]
    nq = seq // tq
    return pl.pallas_call(
        _qk_prep_kernel,
        grid=(batch, nq),
        in_specs=[pl.BlockSpec((tq, ATT_Q), lambda b, i: (b * nq + i, M_Q // ATT_Q)),
                  pl.BlockSpec((tq, ATT_KV), lambda b, i: (b * nq + i, M_K // ATT_KV)),
                  pl.BlockSpec((tq, ATT_KV), lambda b, i: (b * nq + i, M_V // ATT_KV)),
                  pl.BlockSpec((tq, IDX_Q), lambda b, i: (b * nq + i, M_QI // IDX_Q)),
                  pl.BlockSpec((tq, S_END), lambda b, i: (b * nq + i, 0)),
                  pl.BlockSpec((1, HEAD_DIM), lambda b, i: (0, 0)),
                  pl.BlockSpec((1, HEAD_DIM), lambda b, i: (0, 0))],
        out_specs=[pl.BlockSpec((1, ATT_HEADS, HEAD_DIM, tq), lambda b, i: (b * nq + i, 0, 0, 0)),
                   pl.BlockSpec((tq, ATT_KV), lambda b, i: (b * nq + i, 0)),
                   pl.BlockSpec((1, ATT_KV_HEADS, 1, VT_ROWS, tq), lambda b, i: (b, 0, i, 0, 0)),
                   pl.BlockSpec((1, IDX_Q, tq), lambda b, i: (b * nq + i, 0, 0)),
                   pl.BlockSpec((1, IDX_HEADS, tq), lambda b, i: (b * nq + i, 0, 0)),
                   pl.BlockSpec((tq, IDX_DIM), lambda b, i: (b * nq + i, 0))],
        out_shape=[jax.ShapeDtypeStruct((t // tq, ATT_HEADS, HEAD_DIM, tq), BF16),
                   jax.ShapeDtypeStruct((t, ATT_KV), BF16),
                   jax.ShapeDtypeStruct((batch, ATT_KV_HEADS, nq, VT_ROWS, tq), BF16),
                   jax.ShapeDtypeStruct((t // tq, IDX_Q, tq), BF16),
                   jax.ShapeDtypeStruct((t // tq, IDX_HEADS, tq), F32),
                   jax.ShapeDtypeStruct((t, IDX_DIM), BF16)],
        compiler_params=_params(("parallel", "parallel")),
        name="qk_prep",
    )(main, main, main, main, small, qg, kg)


DSA_TQ = 256
DSA_TK = 256
CNT_WAYS = 4


def _t5_bucket_np(dist):
    n = np.maximum(dist, 0)
    max_exact = N_BUCKETS // 2
    nf = np.maximum(n, 1).astype(np.float32)
    ratio = (np.log(nf / np.float32(max_exact)) / np.float32(math.log(MAX_DISTANCE / max_exact))
             * np.float32(N_BUCKETS - max_exact))
    large = max_exact + ratio.astype(np.int32)
    large = np.minimum(large, N_BUCKETS - 1)
    return np.where(n < max_exact, n, large).astype(np.int32)


def _bias_bucket_tiles(tq, tk):
    r = np.arange(tq)[None, :]
    c = np.arange(tk)[:, None]
    d0 = _t5_bucket_np(r - c)
    d1 = _t5_bucket_np(tk + r - c)
    assert np.all(_t5_bucket_np(np.arange(tk + 1, 8 * tk)) == N_BUCKETS - 1)
    return np.stack([d0, d1]).astype(np.int32)


def _dsa_kernel(relb_ref, bidx_ref, qt_ref, qit_ref, wt_ref, k_ref, vt_ref, ki_ref, o_ref,
                key_ref, plane_ref, madd_ref, bias_ref, s_buf, p_buf, st_ref, acc_ref, *, tq, tk, nkc, topk):
    b = pl.program_id(0)
    i = pl.program_id(1)
    g = pl.program_id(2)
    neg_slot = nkc

    @pl.when((b == 0) & (i == 0) & (g == 0))
    def _init():
        madd_ref[neg_slot] = jnp.full((tk, tq), NEG_INF, F32)
        for t in range(2):
            bt = bidx_ref[t]

            def head_body(h, carry):
                far = relb_ref[N_BUCKETS - 1, h]

                def bucket_body(bk, acc):
                    return jnp.where(bt == bk, (relb_ref[bk, h] - far) * LOG2E, acc)

                bias_ref[t, h] = lax.fori_loop(0, N_BUCKETS, bucket_body, jnp.zeros((tk, tq), F32))
                return carry

            lax.fori_loop(0, ATT_HEADS, head_body, 0)

    @pl.when(g == 0)
    def _select():
        nj = i + 1
        qpos = i * tq + lax.broadcasted_iota(I32, (tk, tq), 1)

        def score_chunk(j, carry):
            kc = ki_ref[pl.ds(pl.multiple_of(j * tk, tk), tk), :]
            acc = jnp.zeros((tk, tq), F32)
            for h in range(IDX_HEADS):
                z = _dot(kc, qit_ref[0, h * IDX_DIM:(h + 1) * IDX_DIM, :])
                acc = acc + jnp.maximum(z, 0.0) * wt_ref[0, h:h + 1, :]
            kpos = j * tk + lax.broadcasted_iota(I32, (tk, tq), 0)
            acc = jnp.where(kpos <= qpos, acc, NEG_INF)
            bits = pltpu.bitcast(acc, I32)
            bits = jnp.where(bits == INT_MIN, 0, bits)
            key = bits ^ ((bits >> 31) & 0x7FFFFFFF)
            key_ref[j] = key
            plane_ref[0, j] = pltpu.bitcast(bits & jnp.int32(-65536), F32).astype(BF16)
            plane_ref[1, j] = ((key >> 8) & 0xFF).astype(F32).astype(BF16)
            plane_ref[2, j] = (key & 0xFF).astype(F32).astype(BF16)
            return carry

        lax.fori_loop(0, nj, score_chunk, 0)

        kf = float(topk)
        one_b = jnp.ones((tk, tq), BF16)
        zero_b = jnp.zeros((tk, tq), BF16)
        grp = 16 * CNT_WAYS

        def count(plane, cand_b, strict):
            def body(j, cnt):
                pv = plane_ref[plane, j]
                hit = jnp.where(pv > cand_b if strict else pv >= cand_b, one_b, zero_b)
                for r0 in range(0, tk, grp):
                    cnt = cnt + hit[r0:r0 + grp]
                return cnt

            cnt = lax.fori_loop(0, nj, body, jnp.zeros((grp, tq), BF16))
            return jnp.sum(cnt.astype(F32), axis=0, keepdims=True)

        def keep_ties(src, dst, tie_b):
            def body(j, carry):
                plane_ref[dst, j] = jnp.where(plane_ref[src, j] == tie_b, plane_ref[dst, j], -one_b)
                return carry

            lax.fori_loop(0, nj, body, 0)

        def top_digit_float(p16):
            fb = (p16 & 0xFFFF) ^ jnp.where(p16 < 0, 0x7FFF, 0)
            return pltpu.bitcast(lax.shift_left(fb, 16), F32).astype(BF16)

        def top_iter(it, prefix):
            cand = prefix + lax.shift_left(jnp.int32(1), 15 - it)
            return jnp.where(count(0, top_digit_float(cand), False) >= kf, cand, prefix)

        d_top = lax.fori_loop(0, 16, top_iter, jnp.full((1, tq), -32768, I32))
        t_top = top_digit_float(d_top)
        above = count(0, t_top, True)

        def byte_digit(plane, above_n):
            def it_body(it, prefix):
                cand = prefix + lax.shift_left(jnp.int32(1), 7 - it)
                tot = above_n + count(plane, cand.astype(F32).astype(BF16), False)
                return jnp.where(tot >= kf, cand, prefix)

            return lax.fori_loop(0, 8, it_body, jnp.zeros((1, tq), I32))

        keep_ties(0, 1, t_top)
        d_mid = byte_digit(1, above)
        t_mid = d_mid.astype(F32).astype(BF16)
        above = above + count(1, t_mid, True)
        keep_ties(1, 2, t_mid)
        d_low = byte_digit(2, above)
        thr = lax.shift_left(d_top, 16) | lax.shift_left(d_mid, 8) | d_low
        thr = jnp.maximum(thr, KEY_NEG_INF + 1)

        def madd_chunk(j, carry):
            madd_ref[j] = jnp.where(key_ref[j] >= thr, 0.0, NEG_INF)
            return carry

        lax.fori_loop(0, nj, madd_chunk, 0)

    qt = jnp.concatenate([qt_ref[0, r] for r in range(REP)], axis=1)
    c0 = jnp.maximum(i - 1, 0)
    c1 = jnp.minimum(c0 + 1, nkc - 1)
    n_far = (c0 + 1) // 2

    def scores_to(slot, ca, madd_c, bias_c):
        rows = pl.ds(pl.multiple_of(ca * tk, tk), 2 * tk)
        add = jnp.concatenate([madd_c if bias_c is None else madd_c + bias_c[r] for r in range(REP)], axis=1)
        s_buf[slot] = _dot(k_ref[rows, :], qt) + add

    def softmax_to(slot):
        m = st_ref[0:1, :]
        m_new = jnp.maximum(m, jnp.max(s_buf[slot], axis=0, keepdims=True))
        m_safe = jnp.where(m_new == NEG_INF, 0.0, m_new)
        alpha = jnp.exp2(m - m_safe)
        p = jnp.exp2(s_buf[slot] - m_safe)
        st_ref[0:1, :] = m_new
        st_ref[2 + slot:3 + slot, :] = alpha
        p_buf[slot] = p.astype(BF16)

    def values_from(slot, ca, cb):
        vt_c = jnp.concatenate([vt_ref[0, 0, ca], vt_ref[0, 0, cb]], axis=1)
        acc_ref[...] = st_ref[2 + slot:3 + slot, :] * acc_ref[...] + _dot(vt_c, p_buf[slot])

    def value_chunks(k):
        return jnp.where(k == 0, c0, 2 * (k - 1)), jnp.where(k == 0, c1, 2 * (k - 1) + 1)

    def far_scores_to(slot, k):
        ca = jnp.minimum(2 * (k - 1), nkc - 2)
        cb = jnp.where(ca + 1 >= c0, neg_slot, ca + 1)
        scores_to(slot, ca, jnp.concatenate([madd_ref[ca], madd_ref[cb]], axis=0), None)

    def far_stage(k, cur, nxt):
        far_scores_to(nxt, k + 1)
        values_from(nxt, *value_chunks(k - 1))
        softmax_to(cur)

    st_ref[0:1, :] = jnp.full((1, REP * tq), NEG_INF, F32)
    acc_ref[...] = jnp.zeros_like(acc_ref)

    first = i == 0
    t_a = jnp.where(first, 0, 1)
    idx_b = jnp.where(first, neg_slot, c0 + 1)
    madd_near = jnp.concatenate([madd_ref[c0], madd_ref[idx_b]], axis=0)
    bias_near = [jnp.concatenate([bias_ref[t_a, g * REP + r], bias_ref[0, g * REP + r]], axis=0)
                 for r in range(REP)]
    scores_to(0, c0, madd_near, bias_near)
    far_scores_to(1, 1)
    softmax_to(0)

    def far_pair(u, carry):
        far_stage(2 * u + 1, 1, 0)
        far_stage(2 * u + 2, 0, 1)
        return carry

    lax.fori_loop(0, n_far // 2, far_pair, 0)
    odd = n_far % 2 == 1

    @pl.when(odd)
    def _():
        far_stage(n_far, 1, 0)
        values_from(1, *value_chunks(n_far))

    @pl.when(jnp.logical_not(odd))
    def _():
        values_from(0, *value_chunks(n_far))

    out = acc_ref[0:HEAD_DIM, :] / acc_ref[HEAD_DIM:HEAD_DIM + 1, :]
    for r in range(REP):
        o_ref[:, r * HEAD_DIM:(r + 1) * HEAD_DIM] = out[:, r * tq:(r + 1) * tq].T.astype(o_ref.dtype)


def _dsa(qt, kn, vt, qit, wt, ki, rel_bias, batch, seq):
    tq, tk = DSA_TQ, DSA_TK
    nq = seq // tq
    nkc = seq // tk
    topk = min(TOPK_MAX, seq // 4)
    bidx = jnp.asarray(_bias_bucket_tiles(tq, tk))
    kern = functools.partial(_dsa_kernel, tq=tq, tk=tk, nkc=nkc, topk=topk)
    gw = REP * HEAD_DIM
    return pl.pallas_call(
        kern,
        grid=(batch, nq, ATT_KV_HEADS),
        in_specs=[pl.BlockSpec(memory_space=pltpu.SMEM),
                  pl.BlockSpec((2, tk, tq), lambda b, i, g: (0, 0, 0)),
                  pl.BlockSpec((1, REP, HEAD_DIM, tq), lambda b, i, g: (b * nq + i, g, 0, 0)),
                  pl.BlockSpec((1, IDX_Q, tq), lambda b, i, g: (b * nq + i, 0, 0)),
                  pl.BlockSpec((1, IDX_HEADS, tq), lambda b, i, g: (b * nq + i, 0, 0)),
                  pl.BlockSpec((seq, HEAD_DIM), lambda b, i, g: (b, g)),
                  pl.BlockSpec((1, 1, nkc, VT_ROWS, tk), lambda b, i, g: (b, g, 0, 0, 0)),
                  pl.BlockSpec((seq, IDX_DIM), lambda b, i, g: (b, 0))],
        out_specs=pl.BlockSpec((tq, gw), lambda b, i, g: (b * nq + i, g)),
        out_shape=jax.ShapeDtypeStruct((batch * seq, ATT_Q), BF16),
        scratch_shapes=[pltpu.VMEM((nkc, tk, tq), I32),
                        pltpu.VMEM((3, nkc, tk, tq), BF16),
                        pltpu.VMEM((nkc + 1, tk, tq), F32),
                        pltpu.VMEM((2, ATT_HEADS, tk, tq), F32),
                        pltpu.VMEM((2, 2 * tk, REP * tq), F32),
                        pltpu.VMEM((2, 2 * tk, REP * tq), BF16),
                        pltpu.VMEM((8, REP * tq), F32),
                        pltpu.VMEM((VT_ROWS, REP * tq), F32)],
        compiler_params=_params(("arbitrary", "arbitrary", "arbitrary")),
        name="dsa_attention",
    )(rel_bias, bidx, qt, qit, wt, kn, vt, ki)


E_ROWS = 3 * CHUNK + 16
CONV_HALO = 16


def _conv_shift_matrix():
    s = np.zeros((CONV_WIDTH * CHUNK, CONV_HALO + CHUNK), np.float32)
    for k in range(CONV_WIDTH):
        t = np.arange(CHUNK)
        s[k * CHUNK + t, CONV_HALO + t - (CONV_WIDTH - 1) + k] = 1.0
    return s


def _ssd_kernel(xbc_ref, halo_ref, z_ref, sm_ref, shift_ref, cw_ref, cb_ref, dtb_ref, alog_ref, dsk_ref, ng_ref,
                o_ref, xs_ref, bm_ref, cm_ref, state_ref, ypre_ref, acg_ref, actg_ref):
    c = pl.program_id(1)
    L = CHUNK

    @pl.when(c == 0)
    def _():
        state_ref[...] = jnp.zeros_like(state_ref)

    halo_on = (c > 0).astype(BF16)
    shift = shift_ref[...]
    cblk = GROUP_W
    for cbi in range(CONV_DIM // cblk):
        cols = slice(cbi * cblk, (cbi + 1) * cblk)
        ext = jnp.concatenate([halo_ref[:, cols] * halo_on, xbc_ref[:, cols]], axis=0)
        taps = _dot(shift, ext)
        acc = cb_ref[:, cols]
        for kk in range(CONV_WIDTH):
            acc = acc + cw_ref[kk:kk + 1, cols] * taps[kk * L:(kk + 1) * L]
        y = acc * _sigmoid(acc)
        if cbi < SSM_GROUPS:
            xs_ref[cbi] = y
        else:
            per = cblk // SSM_STATE
            for u in range(per):
                gi = (cbi - SSM_GROUPS) * per + u
                piece = y[:, u * SSM_STATE:(u + 1) * SSM_STATE]
                if gi < SSM_GROUPS:
                    bm_ref[gi] = piece
                else:
                    cm_ref[gi - SSM_GROUPS] = piece

    dt_in = sm_ref[:, S_DT:S_DT + SSM_HEADS] + dtb_ref[...]
    dt_act = jnp.maximum(dt_in, 0.0) + jnp.log1p(jnp.exp(-jnp.abs(dt_in)))
    a = dt_act * (-jnp.exp(alog_ref[...]))
    ri = lax.broadcasted_iota(I32, (L, L), 0)
    ci = lax.broadcasted_iota(I32, (L, L), 1)
    tril = ri >= ci
    tri_b = jnp.where(tril, 1.0, 0.0).astype(BF16)
    a_cum = sum(_dot(tri_b, p) for p in _split3(a))
    eye_b = jnp.where(lax.broadcasted_iota(I32, (SSM_HEADS, SSM_HEADS), 0)
                      == lax.broadcasted_iota(I32, (SSM_HEADS, SSM_HEADS), 1), 1.0, 0.0).astype(BF16)
    a_cum_t = sum(_dot_nt(eye_b, p) for p in _split3(a_cum))
    a_last = a_cum[L - 1:L, :]
    for gi in range(SSM_GROUPS):
        acg_ref[gi] = a_cum[:, gi * HEADS_PER_GROUP:(gi + 1) * HEADS_PER_GROUP]
        actg_ref[gi] = a_cum_t[gi * HEADS_PER_GROUP:(gi + 1) * HEADS_PER_GROUP, :]
    cd3 = _split3(jnp.exp(a_last))
    ds3 = _split3(dsk_ref[...])
    extras = jnp.concatenate([p.astype(F32) for p in cd3 + ds3]
                             + [jnp.zeros((E_ROWS - 3 * L - 6, SSM_HEADS), F32)], axis=0)
    e_mat = jnp.concatenate([dt_act, jnp.exp(a_cum), jnp.exp(a_last - a_cum), extras], axis=0).astype(BF16)

    lane = lax.broadcasted_iota(I32, (L, LANES), 1)
    lo_mask = lane < SSM_HEAD_DIM

    def group_body(gi, carry):
        xs = xs_ref[gi]
        bg = bm_ref[gi]
        cg_b = cm_ref[gi].astype(BF16)
        hsel = (lax.broadcasted_iota(I32, (SSM_HEADS, GROUP_W), 0)
                == gi * HEADS_PER_GROUP + lax.broadcasted_iota(I32, (SSM_HEADS, GROUP_W), 1) // SSM_HEAD_DIM)
        ex = _dot(e_mat, jnp.where(hsel, 1.0, 0.0).astype(BF16))
        dt_rep = ex[0:L]
        expa_rep = ex[L:2 * L]
        dte_rep = ex[2 * L:3 * L]
        cd_rep = ex[3 * L:3 * L + 1] + ex[3 * L + 1:3 * L + 2] + ex[3 * L + 2:3 * L + 3]
        dsk_rep = ex[3 * L + 3:3 * L + 4] + ex[3 * L + 4:3 * L + 5] + ex[3 * L + 5:3 * L + 6]

        xd = xs * dt_rep
        xd_b = xd.astype(BF16)
        cb = _dot_nt(cg_b, bg.astype(BF16))
        acg = acg_ref[gi]
        actg = actg_ref[gi]
        pairs = []
        for pj in range(HEADS_PER_GROUP // 2):
            gmat = []
            for e in (2 * pj, 2 * pj + 1):
                seg = acg[:, e:e + 1] - actg[e:e + 1, :]
                dec = jnp.exp(jnp.where(tril, seg, NEG_INF))
                gmat.append((cb * dec).astype(BF16))
            xp = xd_b[:, pj * LANES:(pj + 1) * LANES]
            zero = jnp.zeros_like(xp)
            pairs.append(_dot(gmat[0], jnp.where(lo_mask, xp, zero))
                         + _dot(gmat[1], jnp.where(lo_mask, zero, xp)))
        y_diag = jnp.concatenate(pairs, axis=1)

        st = state_ref[gi]
        y_off = _dot(cg_b, st.astype(BF16)) * expa_rep
        xdd = (xd * dte_rep).astype(BF16)
        state_ref[gi] = st * cd_rep + _dot(bg.T.astype(BF16), xdd)
        ypre_ref[gi] = y_diag + y_off + dsk_rep * xs
        return carry

    lax.fori_loop(0, SSM_GROUPS, group_body, 0, unroll=8)

    for gi in range(SSM_GROUPS):
        cols = slice(gi * GROUP_W, (gi + 1) * GROUP_W)
        zz = z_ref[:, cols].astype(F32)
        y = ypre_ref[gi] * (zz * _sigmoid(zz))
        ms = jnp.mean(y * y, axis=-1, keepdims=True)
        o_ref[:, cols] = (y * lax.rsqrt(ms + EPS) * ng_ref[:, cols]).astype(o_ref.dtype)


def _ssd(main, small, conv_w, conv_b, dt_bias, a_log, d_skip, norm_g, batch, seq):
    nc = seq // CHUNK
    hb = CHUNK // CONV_HALO
    row = lambda b, c: b * nc + c
    full = lambda shape: pl.BlockSpec(shape, lambda b, c: (0,) * len(shape))
    shift = jnp.asarray(_conv_shift_matrix(), BF16)
    return pl.pallas_call(
        _ssd_kernel,
        grid=(batch, nc),
        in_specs=[pl.BlockSpec((CHUNK, CONV_DIM), lambda b, c: (row(b, c), M_XBC // CONV_DIM)),
                  pl.BlockSpec((CONV_HALO, CONV_DIM),
                               lambda b, c: (jnp.maximum(row(b, c) * hb - 1, 0), M_XBC // CONV_DIM)),
                  pl.BlockSpec((CHUNK, SSM_INNER), lambda b, c: (row(b, c), M_Z // SSM_INNER)),
                  pl.BlockSpec((CHUNK, S_END), lambda b, c: (row(b, c), 0)),
                  full((CONV_WIDTH * CHUNK, CONV_HALO + CHUNK)),
                  full((CONV_WIDTH, CONV_DIM)), full((1, CONV_DIM)), full((1, SSM_HEADS)),
                  full((1, SSM_HEADS)), full((1, SSM_HEADS)), full((1, SSM_INNER))],
        out_specs=pl.BlockSpec((CHUNK, SSM_INNER), lambda b, c: (row(b, c), 0)),
        out_shape=jax.ShapeDtypeStruct((batch * seq, SSM_INNER), BF16),
        scratch_shapes=[pltpu.VMEM((SSM_GROUPS, CHUNK, GROUP_W), F32),
                        pltpu.VMEM((SSM_GROUPS, CHUNK, SSM_STATE), F32),
                        pltpu.VMEM((SSM_GROUPS, CHUNK, SSM_STATE), F32),
                        pltpu.VMEM((SSM_GROUPS, SSM_STATE, GROUP_W), F32),
                        pltpu.VMEM((SSM_GROUPS, CHUNK, GROUP_W), F32),
                        pltpu.VMEM((SSM_GROUPS, CHUNK, HEADS_PER_GROUP), F32),
                        pltpu.VMEM((SSM_GROUPS, HEADS_PER_GROUP, CHUNK), F32)],
        compiler_params=_params(("arbitrary", "arbitrary")),
        name="ssd_scan",
    )(main, main, main, small, shift, conv_w, conv_b, dt_bias, a_log, d_skip, norm_g)


def _merge_kernel(att_ref, y_ref, ga_ref, gs_ref, wa_ref, ws_ref, o_ref):
    pa = _dot(att_ref[...], wa_ref[...])
    ps = _dot(y_ref[...], ws_ref[...])
    o_ref[...] = (_sigmoid(ga_ref[...].astype(F32)) * pa + _sigmoid(gs_ref[...].astype(F32)) * ps).astype(o_ref.dtype)


def _merge(att, y, main, wa, ws, tm, tn):
    t = att.shape[0]
    return pl.pallas_call(
        _merge_kernel,
        grid=(t // tm, D_MODEL // tn),
        in_specs=[pl.BlockSpec((tm, ATT_Q), lambda i, j: (i, 0)),
                  pl.BlockSpec((tm, SSM_INNER), lambda i, j: (i, 0)),
                  pl.BlockSpec((tm, tn), lambda i, j: (i, M_GA // tn + j)),
                  pl.BlockSpec((tm, tn), lambda i, j: (i, M_GS // tn + j)),
                  pl.BlockSpec((ATT_Q, tn), lambda i, j: (0, j)),
                  pl.BlockSpec((SSM_INNER, tn), lambda i, j: (0, j))],
        out_specs=pl.BlockSpec((tm, tn), lambda i, j: (i, j)),
        out_shape=jax.ShapeDtypeStruct((t, D_MODEL), BF16),
        compiler_params=_params(("parallel", "arbitrary")),
        name="gated_merge",
    )(att, y, main, main, wa, ws)


def _out_proj_kernel(m_ref, w_ref, x_ref, g_ref, x1_ref, h2_ref):
    x1 = x_ref[...] + _dot(m_ref[...], w_ref[...])
    x1_ref[...] = x1
    ms = jnp.mean(x1 * x1, axis=-1, keepdims=True)
    h2_ref[...] = (x1 * lax.rsqrt(ms + EPS) * g_ref[...]).astype(BF16)


def _out_proj(merged, w, x, g, tm):
    t = x.shape[0]
    return pl.pallas_call(
        _out_proj_kernel,
        grid=(t // tm,),
        in_specs=[pl.BlockSpec((tm, D_MODEL), lambda i: (i, 0)),
                  pl.BlockSpec((D_MODEL, D_MODEL), lambda i: (0, 0)),
                  pl.BlockSpec((tm, D_MODEL), lambda i: (i, 0)),
                  pl.BlockSpec((1, D_MODEL), lambda i: (0, 0))],
        out_specs=[pl.BlockSpec((tm, D_MODEL), lambda i: (i, 0)),
                   pl.BlockSpec((tm, D_MODEL), lambda i: (i, 0))],
        out_shape=[jax.ShapeDtypeStruct((t, D_MODEL), F32),
                   jax.ShapeDtypeStruct((t, D_MODEL), BF16)],
        compiler_params=_params(("parallel",)),
        name="out_proj_norm",
    )(merged, w, x, g)


def _mlp_kernel(h_ref, x1_ref, wu_ref, wd_ref, o_ref):
    @pl.when(pl.program_id(1) == 0)
    def _():
        o_ref[...] = x1_ref[...]

    u = _dot(h_ref[...], wu_ref[...])
    u = jnp.square(jnp.maximum(u, 0.0)).astype(BF16)
    o_ref[...] += _dot(u, wd_ref[...])


def _mlp(h2, x1, wu, wd, tm, th):
    t = h2.shape[0]
    return pl.pallas_call(
        _mlp_kernel,
        grid=(t // tm, MLP_HIDDEN // th),
        in_specs=[pl.BlockSpec((tm, D_MODEL), lambda i, j: (i, 0)),
                  pl.BlockSpec((tm, D_MODEL), lambda i, j: (i, 0)),
                  pl.BlockSpec((D_MODEL, th), lambda i, j: (0, j)),
                  pl.BlockSpec((th, D_MODEL), lambda i, j: (j, 0))],
        out_specs=pl.BlockSpec((tm, D_MODEL), lambda i, j: (i, 0)),
        out_shape=jax.ShapeDtypeStruct((t, D_MODEL), F32),
        compiler_params=_params(("parallel", "arbitrary")),
        name="relu2_mlp",
    )(h2, x1, wu, wd)


def _pack_w_small(w_t):
    assert (O_GA, O_KI, O_Z, O_DT) == (M_GA, M_Z, O_WI + IDX_HEADS, O_Z + M_END - M_Z)
    return jnp.concatenate([w_t[O_KI:O_Z], w_t[O_DT:O_DT + SSM_HEADS],
                            jnp.zeros((S_END - S_DT - SSM_HEADS, w_t.shape[1]), w_t.dtype)], axis=0)


def _block(x2, batch, seq, norm1_g, w_in, conv_w, conv_b, dt_bias, a_log, d_skip, ssm_norm_g, q_norm_g,
           k_norm_g, rel_bias, w_att_branch, w_ssm_branch, w_out, norm2_g, w_up, w_down):
    row = lambda v: v.reshape(1, -1)
    w_t = w_in.T
    xn, small = _norm_small(x2, row(norm1_g), _pack_w_small(w_t), tm=min(1024, x2.shape[0]))
    main = _in_proj(xn, w_t, tm=min(1024, x2.shape[0]))
    qt, kn, vt, qit, wt, ki = _qk_prep(main, small, row(q_norm_g), row(k_norm_g), batch, seq, tq=DSA_TQ)
    att = _dsa(qt, kn, vt, qit, wt, ki, rel_bias, batch, seq)
    y = _ssd(main, small, conv_w, row(conv_b), row(dt_bias), row(a_log), row(d_skip), row(ssm_norm_g),
             batch, seq)
    merged = _merge(att, y, main, w_att_branch.astype(BF16), w_ssm_branch.astype(BF16), tm=512, tn=1024)
    x1, h2 = _out_proj(merged, w_out.astype(BF16), x2, row(norm2_g), tm=512)
    return _mlp(h2, x1, w_up.astype(BF16), w_down.astype(BF16), tm=512, th=1024)


def kernel(x, norm1_g, w_in, conv_w, conv_b, dt_bias, a_log, d_skip, ssm_norm_g, q_norm_g, k_norm_g, rel_bias,
           w_att_branch, w_ssm_branch, w_out, norm2_g, w_up, w_down):
    batch, seq, d = x.shape
    x2 = x.reshape(batch * seq, d)
    for l in range(norm1_g.shape[0]):
        x2 = _block(x2, batch, seq, norm1_g[l], w_in[l], conv_w[l], conv_b[l], dt_bias[l], a_log[l], d_skip[l],
                    ssm_norm_g[l], q_norm_g[l], k_norm_g[l], rel_bias, w_att_branch[l], w_ssm_branch[l],
                    w_out[l], norm2_g[l], w_up[l], w_down[l])
    return x2.reshape(batch, seq, d)
```

```python
import functools
import math

import numpy as np
import jax
import jax.numpy as jnp
from jax import lax
from jax.experimental import pallas as pl
from jax.experimental.pallas import tpu as pltpu

F32 = jnp.float32
BF16 = jnp.bfloat16
I32 = jnp.int32

D_MODEL = 2048
ATT_HEADS = 16
ATT_KV_HEADS = 4
HEAD_DIM = 128
REP = ATT_HEADS // ATT_KV_HEADS
IDX_HEADS = 16
IDX_DIM = 64
TOPK_MAX = 256
N_BUCKETS = 32
MAX_DISTANCE = 128
SSM_INNER = 2 * D_MODEL
SSM_HEAD_DIM = 64
SSM_HEADS = SSM_INNER // SSM_HEAD_DIM
SSM_GROUPS = 8
SSM_STATE = 128
CONV_WIDTH = 4
CHUNK = 128
MLP_HIDDEN = 4 * D_MODEL
EPS = 1e-6

ATT_Q = ATT_HEADS * HEAD_DIM
ATT_KV = ATT_KV_HEADS * HEAD_DIM
IDX_Q = IDX_HEADS * IDX_DIM
SSM_BC = SSM_GROUPS * SSM_STATE
CONV_DIM = SSM_INNER + 2 * SSM_BC
SPLITS = (D_MODEL, D_MODEL, ATT_Q, ATT_KV, ATT_KV, IDX_Q, IDX_DIM, IDX_HEADS, SSM_INNER, CONV_DIM, SSM_HEADS)
_OFFS = tuple(int(v) for v in np.cumsum((0,) + SPLITS))
(O_GA, O_GS, O_Q, O_K, O_V, O_QI, O_KI, O_WI, O_Z, O_XBC, O_DT, _O_END) = _OFFS

M_GA = 0
M_GS = M_GA + D_MODEL
M_Q = M_GS + D_MODEL
M_K = M_Q + ATT_Q
M_V = M_K + ATT_KV
M_QI = M_V + ATT_KV
M_Z = M_QI + IDX_Q
M_XBC = M_Z + SSM_INNER
M_END = M_XBC + CONV_DIM
S_KI = 0
S_WI = S_KI + IDX_DIM
S_DT = S_WI + IDX_HEADS
S_END = 256

HEADS_PER_GROUP = SSM_HEADS // SSM_GROUPS
GROUP_W = HEADS_PER_GROUP * SSM_HEAD_DIM

LANES = 128
VMEM_LIMIT = 56 * 1024 * 1024

NEG_INF = float("-inf")
INT_MIN = -(2 ** 31)
KEY_NEG_INF = int(np.int32(np.uint32(0xFF800000) ^ np.uint32(0x7FFFFFFF)))


def _dot(a, b):
    return jnp.dot(a, b, preferred_element_type=F32)


def _dot_nt(a, b):
    return lax.dot_general(a, b, (((1,), (1,)), ((), ())), preferred_element_type=F32)


def _split3(x):
    hi = x.astype(BF16)
    r = x - hi.astype(F32)
    mid = r.astype(BF16)
    lo = (r - mid.astype(F32)).astype(BF16)
    return hi, mid, lo


def _silu(x):
    h = 0.5 * x
    return h + h * jnp.tanh(h)


def _sigmoid(x):
    return 1.0 / (1.0 + jnp.exp(-x))


def _params(sem):
    return pltpu.CompilerParams(dimension_semantics=sem, vmem_limit_bytes=VMEM_LIMIT)


def _norm_small_kernel(x_ref, g_ref, w_ref, xn_ref, sm_ref):
    x = x_ref[...]
    ms = jnp.mean(x * x, axis=-1, keepdims=True)
    xn = (x * lax.rsqrt(ms + EPS) * g_ref[...]).astype(BF16)
    xn_ref[...] = xn
    sm_ref[...] = _dot_nt(xn, w_ref[...].astype(BF16))


def _norm_small(x, g, w_small_t, tm):
    m, d = x.shape
    n = w_small_t.shape[0]
    return pl.pallas_call(
        _norm_small_kernel,
        grid=(m // tm,),
        in_specs=[pl.BlockSpec((tm, d), lambda i: (i, 0)),
                  pl.BlockSpec((1, d), lambda i: (0, 0)),
                  pl.BlockSpec((n, d), lambda i: (0, 0))],
        out_specs=[pl.BlockSpec((tm, d), lambda i: (i, 0)),
                   pl.BlockSpec((tm, n), lambda i: (i, 0))],
        out_shape=[jax.ShapeDtypeStruct((m, d), BF16), jax.ShapeDtypeStruct((m, n), F32)],
        compiler_params=_params(("parallel",)),
        name="norm_small_proj",
    )(x, g, w_small_t)


IN_TN = 1024
IN_ALIGNED_TILES = O_KI // IN_TN
IN_SHIFT = O_Z - M_Z
IN_NEXT = 128
IN_ROWS = 256


def _in_proj_kernel(xn_ref, wa_ref, wn_ref, o_ref, w_scr):
    j = pl.program_id(0)
    i = pl.program_id(1)

    @pl.when((i == 0) & (j < IN_ALIGNED_TILES))
    def _():
        for r0 in range(0, IN_TN, IN_ROWS):
            w_scr[r0:r0 + IN_ROWS, :] = wa_ref[r0:r0 + IN_ROWS, :].astype(BF16)

    @pl.when((i == 0) & (j >= IN_ALIGNED_TILES))
    def _():
        for r0 in range(0, IN_TN - IN_SHIFT, IN_ROWS):
            r1 = min(r0 + IN_ROWS, IN_TN - IN_SHIFT)
            w_scr[r0:r1, :] = wa_ref[r0 + IN_SHIFT:r1 + IN_SHIFT, :].astype(BF16)
        w_scr[IN_TN - IN_SHIFT:IN_TN, :] = wn_ref[0:IN_SHIFT, :].astype(BF16)

    o_ref[...] = _dot_nt(xn_ref[...], w_scr[...]).astype(o_ref.dtype)


def _in_proj(xn, w_t, tm):
    m, d = xn.shape
    assert O_KI % IN_TN == 0 and M_END % IN_TN == 0 and IN_TN % IN_NEXT == 0
    assert 0 < IN_SHIFT <= IN_NEXT and IN_SHIFT % 16 == 0
    return pl.pallas_call(
        _in_proj_kernel,
        grid=(M_END // IN_TN, m // tm),
        in_specs=[pl.BlockSpec((tm, d), lambda j, i: (i, 0)),
                  pl.BlockSpec((IN_TN, d), lambda j, i: (j, 0)),
                  pl.BlockSpec((IN_NEXT, d), lambda j, i: ((j + 1) * (IN_TN // IN_NEXT), 0))],
        out_specs=pl.BlockSpec((tm, IN_TN), lambda j, i: (i, j)),
        out_shape=jax.ShapeDtypeStruct((m, M_END), BF16),
        scratch_shapes=[pltpu.VMEM((IN_TN, d), BF16)],
        compiler_params=_params(("arbitrary", "arbitrary")),
        name="in_proj",
    )(xn, w_t, w_t)


LOG2E = math.log2(math.e)
VT_ROWS = HEAD_DIM + 16


def _qk_prep_kernel(q_ref, k_ref, v_ref, qi_ref, sm_ref, qg_ref, kg_ref,
                    qt_ref, kn_ref, vt_ref, qit_ref, wt_ref, ki_ref):
    qg = qg_ref[...]
    for h in range(ATT_HEADS):
        x = q_ref[:, h * HEAD_DIM:(h + 1) * HEAD_DIM].astype(F32)
        ms = jnp.mean(x * x, axis=-1, keepdims=True)
        y = x * lax.rsqrt(ms + EPS) * qg * (HEAD_DIM ** -0.5 * LOG2E)
        qt_ref[0, h] = y.T.astype(BF16)
    kg = kg_ref[...]
    for h in range(ATT_KV_HEADS):
        x = k_ref[:, h * HEAD_DIM:(h + 1) * HEAD_DIM].astype(F32)
        ms = jnp.mean(x * x, axis=-1, keepdims=True)
        kn_ref[:, h * HEAD_DIM:(h + 1) * HEAD_DIM] = (x * lax.rsqrt(ms + EPS) * kg).astype(BF16)
        vt_ref[0, h, 0, 0:HEAD_DIM, :] = v_ref[:, h * HEAD_DIM:(h + 1) * HEAD_DIM].astype(F32).T.astype(BF16)
        vt_ref[0, h, 0, HEAD_DIM:VT_ROWS, :] = jnp.ones((VT_ROWS - HEAD_DIM, v_ref.shape[0]), BF16)
    for p in range(IDX_Q // LANES):
        qit_ref[0, p * LANES:(p + 1) * LANES, :] = qi_ref[:, p * LANES:(p + 1) * LANES].astype(F32).T.astype(BF16)
    sm_t = sm_ref[:, 0:LANES].T
    wt_ref[0] = sm_t[S_WI:S_WI + IDX_HEADS, :] * (IDX_HEADS ** -0.5 * IDX_DIM ** -0.5)
    ki_ref[...] = sm_ref[:, S_KI:S_KI + IDX_DIM].astype(BF16)


def _qk_prep(main, small, qg, kg, batch, seq, tq):
    t = main.shape[0]
    nq = seq // tq
    return pl.pallas_call(
        _qk_prep_kernel,
        grid=(batch, nq),
        in_specs=[pl.BlockSpec((tq, ATT_Q), lambda b, i: (b * nq + i, M_Q // ATT_Q)),
                  pl.BlockSpec((tq, ATT_KV), lambda b, i: (b * nq + i, M_K // ATT_KV)),
                  pl.BlockSpec((tq, ATT_KV), lambda b, i: (b * nq + i, M_V // ATT_KV)),
                  pl.BlockSpec((tq, IDX_Q), lambda b, i: (b * nq + i, M_QI // IDX_Q)),
                  pl.BlockSpec((tq, S_END), lambda b, i: (b * nq + i, 0)),
                  pl.BlockSpec((1, HEAD_DIM), lambda b, i: (0, 0)),
                  pl.BlockSpec((1, HEAD_DIM), lambda b, i: (0, 0))],
        out_specs=[pl.BlockSpec((1, ATT_HEADS, HEAD_DIM, tq), lambda b, i: (b * nq + i, 0, 0, 0)),
                   pl.BlockSpec((tq, ATT_KV), lambda b, i: (b * nq + i, 0)),
                   pl.BlockSpec((1, ATT_KV_HEADS, 1, VT_ROWS, tq), lambda b, i: (b, 0, i, 0, 0)),
                   pl.BlockSpec((1, IDX_Q, tq), lambda b, i: (b * nq + i, 0, 0)),
                   pl.BlockSpec((1, IDX_HEADS, tq), lambda b, i: (b * nq + i, 0, 0)),
                   pl.BlockSpec((tq, IDX_DIM), lambda b, i: (b * nq + i, 0))],
        out_shape=[jax.ShapeDtypeStruct((t // tq, ATT_HEADS, HEAD_DIM, tq), BF16),
                   jax.ShapeDtypeStruct((t, ATT_KV), BF16),
                   jax.ShapeDtypeStruct((batch, ATT_KV_HEADS, nq, VT_ROWS, tq), BF16),
                   jax.ShapeDtypeStruct((t // tq, IDX_Q, tq), BF16),
                   jax.ShapeDtypeStruct((t // tq, IDX_HEADS, tq), F32),
                   jax.ShapeDtypeStruct((t, IDX_DIM), BF16)],
        compiler_params=_params(("parallel", "parallel")),
        name="qk_prep",
    )(main, main, main, main, small, qg, kg)


DSA_TQ = 256
DSA_TK = 256
CNT_WAYS = 4


def _t5_bucket_np(dist):
    n = np.maximum(dist, 0)
    max_exact = N_BUCKETS // 2
    nf = np.maximum(n, 1).astype(np.float32)
    ratio = (np.log(nf / np.float32(max_exact)) / np.float32(math.log(MAX_DISTANCE / max_exact))
             * np.float32(N_BUCKETS - max_exact))
    large = max_exact + ratio.astype(np.int32)
    large = np.minimum(large, N_BUCKETS - 1)
    return np.where(n < max_exact, n, large).astype(np.int32)


def _bias_bucket_tiles(tq, tk):
    r = np.arange(tq)[None, :]
    c = np.arange(tk)[:, None]
    d0 = _t5_bucket_np(r - c)
    d1 = _t5_bucket_np(tk + r - c)
    assert np.all(_t5_bucket_np(np.arange(tk + 1, 8 * tk)) == N_BUCKETS - 1)
    return np.stack([d0, d1]).astype(np.int32)


def _dsa_kernel(relb_ref, bidx_ref, qt_ref, qit_ref, wt_ref, k_ref, vt_ref, ki_ref, o_ref,
                key_ref, plane_ref, madd_ref, bias_ref, s_buf, p_buf, st_ref, acc_ref, *, tq, tk, nkc, topk):
    b = pl.program_id(0)
    i = pl.program_id(1)
    g = pl.program_id(2)
    neg_slot = nkc

    @pl.when((b == 0) & (i == 0) & (g == 0))
    def _init():
        madd_ref[neg_slot] = jnp.full((tk, tq), NEG_INF, F32)
        for t in range(2):
            bt = bidx_ref[t]

            def head_body(h, carry):
                far = relb_ref[N_BUCKETS - 1, h]

                def bucket_body(bk, acc):
                    return jnp.where(bt == bk, (relb_ref[bk, h] - far) * LOG2E, acc)

                bias_ref[t, h] = lax.fori_loop(0, N_BUCKETS, bucket_body, jnp.zeros((tk, tq), F32))
                return carry

            lax.fori_loop(0, ATT_HEADS, head_body, 0)

    @pl.when(g == 0)
    def _select():
        nj = i + 1
        qpos = i * tq + lax.broadcasted_iota(I32, (tk, tq), 1)

        def score_chunk(j, carry):
            kc = ki_ref[pl.ds(pl.multiple_of(j * tk, tk), tk), :]
            acc = jnp.zeros((tk, tq), F32)
            for h in range(IDX_HEADS):
                z = _dot(kc, qit_ref[0, h * IDX_DIM:(h + 1) * IDX_DIM, :])
                acc = acc + jnp.maximum(z, 0.0) * wt_ref[0, h:h + 1, :]
            kpos = j * tk + lax.broadcasted_iota(I32, (tk, tq), 0)
            acc = jnp.where(kpos <= qpos, acc, NEG_INF)
            bits = pltpu.bitcast(acc, I32)
            bits = jnp.where(bits == INT_MIN, 0, bits)
            key = bits ^ ((bits >> 31) & 0x7FFFFFFF)
            key_ref[j] = key
            plane_ref[0, j] = pltpu.bitcast(bits & jnp.int32(-65536), F32).astype(BF16)
            plane_ref[1, j] = ((key >> 8) & 0xFF).astype(F32).astype(BF16)
            plane_ref[2, j] = (key & 0xFF).astype(F32).astype(BF16)
            return carry

        lax.fori_loop(0, nj, score_chunk, 0)

        kf = float(topk)
        one_b = jnp.ones((tk, tq), BF16)
        zero_b = jnp.zeros((tk, tq), BF16)
        grp = 16 * CNT_WAYS

        def count(plane, cand_b, strict):
            def body(j, cnt):
                pv = plane_ref[plane, j]
                hit = jnp.where(pv > cand_b if strict else pv >= cand_b, one_b, zero_b)
                for r0 in range(0, tk, grp):
                    cnt = cnt + hit[r0:r0 + grp]
                return cnt

            cnt = lax.fori_loop(0, nj, body, jnp.zeros((grp, tq), BF16))
            return jnp.sum(cnt.astype(F32), axis=0, keepdims=True)

        def keep_ties(src, dst, tie_b):
            def body(j, carry):
                plane_ref[dst, j] = jnp.where(plane_ref[src, j] == tie_b, plane_ref[dst, j], -one_b)
                return carry

            lax.fori_loop(0, nj, body, 0)

        def top_digit_float(p16):
            fb = (p16 & 0xFFFF) ^ jnp.where(p16 < 0, 0x7FFF, 0)
            return pltpu.bitcast(lax.shift_left(fb, 16), F32).astype(BF16)

        def top_iter(it, prefix):
            cand = prefix + lax.shift_left(jnp.int32(1), 15 - it)
            return jnp.where(count(0, top_digit_float(cand), False) >= kf, cand, prefix)

        d_top = lax.fori_loop(0, 16, top_iter, jnp.full((1, tq), -32768, I32))
        t_top = top_digit_float(d_top)
        above = count(0, t_top, True)

        def byte_digit(plane, above_n):
            def it_body(it, prefix):
                cand = prefix + lax.shift_left(jnp.int32(1), 7 - it)
                tot = above_n + count(plane, cand.astype(F32).astype(BF16), False)
                return jnp.where(tot >= kf, cand, prefix)

            return lax.fori_loop(0, 8, it_body, jnp.zeros((1, tq), I32))

        keep_ties(0, 1, t_top)
        d_mid = byte_digit(1, above)
        t_mid = d_mid.astype(F32).astype(BF16)
        above = above + count(1, t_mid, True)
        keep_ties(1, 2, t_mid)
        d_low = byte_digit(2, above)
        thr = lax.shift_left(d_top, 16) | lax.shift_left(d_mid, 8) | d_low
        thr = jnp.maximum(thr, KEY_NEG_INF + 1)

        def madd_chunk(j, carry):
            madd_ref[j] = jnp.where(key_ref[j] >= thr, 0.0, NEG_INF)
            return carry

        lax.fori_loop(0, nj, madd_chunk, 0)

    qt = jnp.concatenate([qt_ref[0, r] for r in range(REP)], axis=1)
    c0 = jnp.maximum(i - 1, 0)
    c1 = jnp.minimum(c0 + 1, nkc - 1)
    n_far = (c0 + 1) // 2

    def scores_to(slot, ca, madd_c, bias_c):
        rows = pl.ds(pl.multiple_of(ca * tk, tk), 2 * tk)
        add = jnp.concatenate([madd_c if bias_c is None else madd_c + bias_c[r] for r in range(REP)], axis=1)
        s = _dot(k_ref[rows, :], qt) + add
        s_buf[slot] = s
        st_ref[4 + slot:5 + slot, :] = jnp.max(s, axis=0, keepdims=True)

    def softmax_to(slot):
        m = st_ref[0:1, :]
        m_new = jnp.maximum(m, st_ref[4 + slot:5 + slot, :])
        m_safe = jnp.where(m_new == NEG_INF, 0.0, m_new)
        alpha = jnp.exp2(m - m_safe)
        p_buf[slot] = jnp.exp2((s_buf[slot] - m_safe).astype(BF16))
        st_ref[0:1, :] = m_new
        st_ref[2 + slot:3 + slot, :] = alpha

    def values_from(slot, ca, cb):
        vt_c = jnp.concatenate([vt_ref[0, 0, ca], vt_ref[0, 0, cb]], axis=1)
        acc_ref[...] = st_ref[2 + slot:3 + slot, :] * acc_ref[...] + _dot(vt_c, p_buf[slot])

    def far_scores_to(slot, k):
        ca = jnp.minimum(2 * (k - 1), nkc - 2)
        cb = jnp.where(ca + 1 >= c0, neg_slot, ca + 1)
        scores_to(slot, ca, jnp.concatenate([madd_ref[ca], madd_ref[cb]], axis=0), None)

    def far_stage(k, cur, nxt):
        far_scores_to(nxt, k + 1)
        softmax_to(cur)
        values_from(cur, 2 * (k - 1), 2 * (k - 1) + 1)

    st_ref[0:1, :] = jnp.full((1, REP * tq), NEG_INF, F32)
    acc_ref[...] = jnp.zeros_like(acc_ref)

    first = i == 0
    t_a = jnp.where(first, 0, 1)
    idx_b = jnp.where(first, neg_slot, c0 + 1)
    madd_near = jnp.concatenate([madd_ref[c0], madd_ref[idx_b]], axis=0)
    bias_near = [jnp.concatenate([bias_ref[t_a, g * REP + r], bias_ref[0, g * REP + r]], axis=0)
                 for r in range(REP)]
    scores_to(0, c0, madd_near, bias_near)
    far_scores_to(1, 1)
    softmax_to(0)
    values_from(0, c0, c1)

    def far_pair(u, carry):
        far_stage(2 * u + 1, 1, 0)
        far_stage(2 * u + 2, 0, 1)
        return carry

    lax.fori_loop(0, n_far // 2, far_pair, 0)

    @pl.when(n_far % 2 == 1)
    def _():
        far_stage(n_far, 1, 0)

    out = acc_ref[0:HEAD_DIM, :] / acc_ref[HEAD_DIM:HEAD_DIM + 1, :]
    for r in range(REP):
        o_ref[:, r * HEAD_DIM:(r + 1) * HEAD_DIM] = out[:, r * tq:(r + 1) * tq].T.astype(o_ref.dtype)


def _dsa(qt, kn, vt, qit, wt, ki, rel_bias, batch, seq):
    tq, tk = DSA_TQ, DSA_TK
    nq = seq // tq
    nkc = seq // tk
    topk = min(TOPK_MAX, seq // 4)
    bidx = jnp.asarray(_bias_bucket_tiles(tq, tk))
    kern = functools.partial(_dsa_kernel, tq=tq, tk=tk, nkc=nkc, topk=topk)
    gw = REP * HEAD_DIM
    return pl.pallas_call(
        kern,
        grid=(batch, nq, ATT_KV_HEADS),
        in_specs=[pl.BlockSpec(memory_space=pltpu.SMEM),
                  pl.BlockSpec((2, tk, tq), lambda b, i, g: (0, 0, 0)),
                  pl.BlockSpec((1, REP, HEAD_DIM, tq), lambda b, i, g: (b * nq + i, g, 0, 0)),
                  pl.BlockSpec((1, IDX_Q, tq), lambda b, i, g: (b * nq + i, 0, 0)),
                  pl.BlockSpec((1, IDX_HEADS, tq), lambda b, i, g: (b * nq + i, 0, 0)),
                  pl.BlockSpec((seq, HEAD_DIM), lambda b, i, g: (b, g)),
                  pl.BlockSpec((1, 1, nkc, VT_ROWS, tk), lambda b, i, g: (b, g, 0, 0, 0)),
                  pl.BlockSpec((seq, IDX_DIM), lambda b, i, g: (b, 0))],
        out_specs=pl.BlockSpec((tq, gw), lambda b, i, g: (b * nq + i, g)),
        out_shape=jax.ShapeDtypeStruct((batch * seq, ATT_Q), BF16),
        scratch_shapes=[pltpu.VMEM((nkc, tk, tq), I32),
                        pltpu.VMEM((3, nkc, tk, tq), BF16),
                        pltpu.VMEM((nkc + 1, tk, tq), F32),
                        pltpu.VMEM((2, ATT_HEADS, tk, tq), F32),
                        pltpu.VMEM((2, 2 * tk, REP * tq), F32),
                        pltpu.VMEM((2, 2 * tk, REP * tq), BF16),
                        pltpu.VMEM((8, REP * tq), F32),
                        pltpu.VMEM((VT_ROWS, REP * tq), F32)],
        compiler_params=_params(("arbitrary", "arbitrary", "arbitrary")),
        name="dsa_attention",
    )(rel_bias, bidx, qt, qit, wt, kn, vt, ki)


E_ROWS = 3 * CHUNK + 16
CONV_HALO = 16


def _conv_shift_matrix():
    s = np.zeros((CONV_WIDTH * CHUNK, CONV_HALO + CHUNK), np.float32)
    for k in range(CONV_WIDTH):
        t = np.arange(CHUNK)
        s[k * CHUNK + t, CONV_HALO + t - (CONV_WIDTH - 1) + k] = 1.0
    return s


def _ssd_kernel(xbc_ref, halo_ref, z_ref, sm_ref, shift_ref, cw_ref, cb_ref, dtb_ref, alog_ref, dsk_ref, ng_ref,
                o_ref, xs_ref, bm_ref, cm_ref, state_ref, ypre_ref, acg_ref, actg_ref):
    c = pl.program_id(1)
    L = CHUNK

    @pl.when(c == 0)
    def _():
        state_ref[...] = jnp.zeros_like(state_ref)

    halo_on = (c > 0).astype(BF16)
    shift = shift_ref[...]
    cblk = GROUP_W
    for cbi in range(CONV_DIM // cblk):
        cols = slice(cbi * cblk, (cbi + 1) * cblk)
        ext = jnp.concatenate([halo_ref[:, cols] * halo_on, xbc_ref[:, cols]], axis=0)
        taps = _dot(shift, ext)
        acc = cb_ref[:, cols]
        for kk in range(CONV_WIDTH):
            acc = acc + cw_ref[kk:kk + 1, cols] * taps[kk * L:(kk + 1) * L]
        y = _silu(acc)
        if cbi < SSM_GROUPS:
            xs_ref[cbi] = y
        else:
            per = cblk // SSM_STATE
            for u in range(per):
                gi = (cbi - SSM_GROUPS) * per + u
                piece = y[:, u * SSM_STATE:(u + 1) * SSM_STATE]
                if gi < SSM_GROUPS:
                    bm_ref[gi] = piece
                else:
                    cm_ref[gi - SSM_GROUPS] = piece

    dt_in = sm_ref[:, S_DT:S_DT + SSM_HEADS] + dtb_ref[...]
    dt_act = jnp.maximum(dt_in, 0.0) + jnp.log1p(jnp.exp(-jnp.abs(dt_in)))
    a = dt_act * (-jnp.exp(alog_ref[...]))
    ri = lax.broadcasted_iota(I32, (L, L), 0)
    ci = lax.broadcasted_iota(I32, (L, L), 1)
    tril = ri >= ci
    tri_b = jnp.where(tril, 1.0, 0.0).astype(BF16)
    a_cum = sum(_dot(tri_b, p) for p in _split3(a))
    eye_b = jnp.where(lax.broadcasted_iota(I32, (SSM_HEADS, SSM_HEADS), 0)
                      == lax.broadcasted_iota(I32, (SSM_HEADS, SSM_HEADS), 1), 1.0, 0.0).astype(BF16)
    a_cum_t = sum(_dot_nt(eye_b, p) for p in _split3(a_cum))
    a_last = a_cum[L - 1:L, :]
    for gi in range(SSM_GROUPS):
        acg_ref[gi] = a_cum[:, gi * HEADS_PER_GROUP:(gi + 1) * HEADS_PER_GROUP]
        actg_ref[gi] = a_cum_t[gi * HEADS_PER_GROUP:(gi + 1) * HEADS_PER_GROUP, :]
    cd3 = _split3(jnp.exp(a_last))
    ds3 = _split3(dsk_ref[...])
    extras = jnp.concatenate([p.astype(F32) for p in cd3 + ds3]
                             + [jnp.zeros((E_ROWS - 3 * L - 6, SSM_HEADS), F32)], axis=0)
    e_mat = jnp.concatenate([dt_act, jnp.exp(a_cum), jnp.exp(a_last - a_cum), extras], axis=0).astype(BF16)

    lane = lax.broadcasted_iota(I32, (L, LANES), 1)
    lo_mask = lane < SSM_HEAD_DIM

    def group_body(gi, carry):
        xs = xs_ref[gi]
        bg = bm_ref[gi]
        cg_b = cm_ref[gi].astype(BF16)
        hsel = (lax.broadcasted_iota(I32, (SSM_HEADS, GROUP_W), 0)
                == gi * HEADS_PER_GROUP + lax.broadcasted_iota(I32, (SSM_HEADS, GROUP_W), 1) // SSM_HEAD_DIM)
        ex = _dot(e_mat, jnp.where(hsel, 1.0, 0.0).astype(BF16))
        dt_rep = ex[0:L]
        expa_rep = ex[L:2 * L]
        dte_rep = ex[2 * L:3 * L]
        cd_rep = ex[3 * L:3 * L + 1] + ex[3 * L + 1:3 * L + 2] + ex[3 * L + 2:3 * L + 3]
        dsk_rep = ex[3 * L + 3:3 * L + 4] + ex[3 * L + 4:3 * L + 5] + ex[3 * L + 5:3 * L + 6]

        xd = xs * dt_rep
        xd_b = xd.astype(BF16)
        cb = _dot_nt(cg_b, bg.astype(BF16))
        acg = acg_ref[gi]
        actg = actg_ref[gi]
        pairs = []
        for pj in range(HEADS_PER_GROUP // 2):
            gmat = []
            for e in (2 * pj, 2 * pj + 1):
                seg = acg[:, e:e + 1] - actg[e:e + 1, :]
                dec = jnp.exp(jnp.where(tril, seg, NEG_INF))
                gmat.append((cb * dec).astype(BF16))
            xp = xd_b[:, pj * LANES:(pj + 1) * LANES]
            zero = jnp.zeros_like(xp)
            pairs.append(_dot(gmat[0], jnp.where(lo_mask, xp, zero))
                         + _dot(gmat[1], jnp.where(lo_mask, zero, xp)))
        y_diag = jnp.concatenate(pairs, axis=1)

        st = state_ref[gi]
        y_off = _dot(cg_b, st.astype(BF16)) * expa_rep
        xdd = (xd * dte_rep).astype(BF16)
        state_ref[gi] = st * cd_rep + _dot(bg.T.astype(BF16), xdd)
        ypre_ref[gi] = y_diag + y_off + dsk_rep * xs
        return carry

    lax.fori_loop(0, SSM_GROUPS, group_body, 0, unroll=8)

    for gi in range(SSM_GROUPS):
        cols = slice(gi * GROUP_W, (gi + 1) * GROUP_W)
        zz = z_ref[:, cols].astype(F32)
        y = ypre_ref[gi] * _silu(zz)
        ms = jnp.mean(y * y, axis=-1, keepdims=True)
        o_ref[:, cols] = (y * lax.rsqrt(ms + EPS) * ng_ref[:, cols]).astype(o_ref.dtype)


def _ssd(main, small, conv_w, conv_b, dt_bias, a_log, d_skip, norm_g, batch, seq):
    nc = seq // CHUNK
    hb = CHUNK // CONV_HALO
    row = lambda b, c: b * nc + c
    full = lambda shape: pl.BlockSpec(shape, lambda b, c: (0,) * len(shape))
    shift = jnp.asarray(_conv_shift_matrix(), BF16)
    return pl.pallas_call(
        _ssd_kernel,
        grid=(batch, nc),
        in_specs=[pl.BlockSpec((CHUNK, CONV_DIM), lambda b, c: (row(b, c), M_XBC // CONV_DIM)),
                  pl.BlockSpec((CONV_HALO, CONV_DIM),
                               lambda b, c: (jnp.maximum(row(b, c) * hb - 1, 0), M_XBC // CONV_DIM)),
                  pl.BlockSpec((CHUNK, SSM_INNER), lambda b, c: (row(b, c), M_Z // SSM_INNER)),
                  pl.BlockSpec((CHUNK, S_END), lambda b, c: (row(b, c), 0)),
                  full((CONV_WIDTH * CHUNK, CONV_HALO + CHUNK)),
                  full((CONV_WIDTH, CONV_DIM)), full((1, CONV_DIM)), full((1, SSM_HEADS)),
                  full((1, SSM_HEADS)), full((1, SSM_HEADS)), full((1, SSM_INNER))],
        out_specs=pl.BlockSpec((CHUNK, SSM_INNER), lambda b, c: (row(b, c), 0)),
        out_shape=jax.ShapeDtypeStruct((batch * seq, SSM_INNER), BF16),
        scratch_shapes=[pltpu.VMEM((SSM_GROUPS, CHUNK, GROUP_W), F32),
                        pltpu.VMEM((SSM_GROUPS, CHUNK, SSM_STATE), F32),
                        pltpu.VMEM((SSM_GROUPS, CHUNK, SSM_STATE), F32),
                        pltpu.VMEM((SSM_GROUPS, SSM_STATE, GROUP_W), F32),
                        pltpu.VMEM((SSM_GROUPS, CHUNK, GROUP_W), F32),
                        pltpu.VMEM((SSM_GROUPS, CHUNK, HEADS_PER_GROUP), F32),
                        pltpu.VMEM((SSM_GROUPS, HEADS_PER_GROUP, CHUNK), F32)],
        compiler_params=_params(("arbitrary", "arbitrary")),
        name="ssd_scan",
    )(main, main, main, small, shift, conv_w, conv_b, dt_bias, a_log, d_skip, norm_g)


def _merge_kernel(att_ref, y_ref, ga_ref, gs_ref, wa_ref, ws_ref, o_ref):
    pa = _dot(att_ref[...], wa_ref[...])
    ps = _dot(y_ref[...], ws_ref[...])
    o_ref[...] = (_sigmoid(ga_ref[...].astype(F32)) * pa + _sigmoid(gs_ref[...].astype(F32)) * ps).astype(o_ref.dtype)


def _merge(att, y, main, wa, ws, tm, tn):
    t = att.shape[0]
    return pl.pallas_call(
        _merge_kernel,
        grid=(t // tm, D_MODEL // tn),
        in_specs=[pl.BlockSpec((tm, ATT_Q), lambda i, j: (i, 0)),
                  pl.BlockSpec((tm, SSM_INNER), lambda i, j: (i, 0)),
                  pl.BlockSpec((tm, tn), lambda i, j: (i, M_GA // tn + j)),
                  pl.BlockSpec((tm, tn), lambda i, j: (i, M_GS // tn + j)),
                  pl.BlockSpec((ATT_Q, tn), lambda i, j: (0, j)),
                  pl.BlockSpec((SSM_INNER, tn), lambda i, j: (0, j))],
        out_specs=pl.BlockSpec((tm, tn), lambda i, j: (i, j)),
        out_shape=jax.ShapeDtypeStruct((t, D_MODEL), BF16),
        compiler_params=_params(("parallel", "arbitrary")),
        name="gated_merge",
    )(att, y, main, main, wa, ws)


def _out_proj_kernel(m_ref, w_ref, x_ref, g_ref, x1_ref, h2_ref):
    x1 = x_ref[...] + _dot(m_ref[...], w_ref[...])
    x1_ref[...] = x1
    ms = jnp.mean(x1 * x1, axis=-1, keepdims=True)
    h2_ref[...] = (x1 * lax.rsqrt(ms + EPS) * g_ref[...]).astype(BF16)


def _out_proj(merged, w, x, g, tm):
    t = x.shape[0]
    return pl.pallas_call(
        _out_proj_kernel,
        grid=(t // tm,),
        in_specs=[pl.BlockSpec((tm, D_MODEL), lambda i: (i, 0)),
                  pl.BlockSpec((D_MODEL, D_MODEL), lambda i: (0, 0)),
                  pl.BlockSpec((tm, D_MODEL), lambda i: (i, 0)),
                  pl.BlockSpec((1, D_MODEL), lambda i: (0, 0))],
        out_specs=[pl.BlockSpec((tm, D_MODEL), lambda i: (i, 0)),
                   pl.BlockSpec((tm, D_MODEL), lambda i: (i, 0))],
        out_shape=[jax.ShapeDtypeStruct((t, D_MODEL), F32),
                   jax.ShapeDtypeStruct((t, D_MODEL), BF16)],
        compiler_params=_params(("parallel",)),
        name="out_proj_norm",
    )(merged, w, x, g)


def _mlp_kernel(h_ref, x1_ref, wu_ref, wd_ref, o_ref):
    @pl.when(pl.program_id(1) == 0)
    def _():
        o_ref[...] = x1_ref[...]

    u = _dot(h_ref[...], wu_ref[...])
    u = jnp.square(jnp.maximum(u, 0.0)).astype(BF16)
    o_ref[...] += _dot(u, wd_ref[...])


def _mlp(h2, x1, wu, wd, tm, th):
    t = h2.shape[0]
    return pl.pallas_call(
        _mlp_kernel,
        grid=(t // tm, MLP_HIDDEN // th),
        in_specs=[pl.BlockSpec((tm, D_MODEL), lambda i, j: (i, 0)),
                  pl.BlockSpec((tm, D_MODEL), lambda i, j: (i, 0)),
                  pl.BlockSpec((D_MODEL, th), lambda i, j: (0, j)),
                  pl.BlockSpec((th, D_MODEL), lambda i, j: (j, 0))],
        out_specs=pl.BlockSpec((tm, D_MODEL), lambda i, j: (i, 0)),
        out_shape=jax.ShapeDtypeStruct((t, D_MODEL), F32),
        compiler_params=_params(("parallel", "arbitrary")),
        name="relu2_mlp",
    )(h2, x1, wu, wd)


def _pack_w_small(w_t):
    assert (O_GA, O_KI, O_Z, O_DT) == (M_GA, M_Z, O_WI + IDX_HEADS, O_Z + M_END - M_Z)
    return jnp.concatenate([w_t[O_KI:O_Z], w_t[O_DT:O_DT + SSM_HEADS],
                            jnp.zeros((S_END - S_DT - SSM_HEADS, w_t.shape[1]), w_t.dtype)], axis=0)


def _block(x2, batch, seq, norm1_g, w_in, conv_w, conv_b, dt_bias, a_log, d_skip, ssm_norm_g, q_norm_g,
           k_norm_g, rel_bias, w_att_branch, w_ssm_branch, w_out, norm2_g, w_up, w_down):
    row = lambda v: v.reshape(1, -1)
    w_t = w_in.T
    xn, small = _norm_small(x2, row(norm1_g), _pack_w_small(w_t), tm=min(1024, x2.shape[0]))
    main = _in_proj(xn, w_t, tm=min(1024, x2.shape[0]))
    qt, kn, vt, qit, wt, ki = _qk_prep(main, small, row(q_norm_g), row(k_norm_g), batch, seq, tq=DSA_TQ)
    att = _dsa(qt, kn, vt, qit, wt, ki, rel_bias, batch, seq)
    y = _ssd(main, small, conv_w, row(conv_b), row(dt_bias), row(a_log), row(d_skip), row(ssm_norm_g),
             batch, seq)
    merged = _merge(att, y, main, w_att_branch.astype(BF16), w_ssm_branch.astype(BF16), tm=512, tn=1024)
    x1, h2 = _out_proj(merged, w_out.astype(BF16), x2, row(norm2_g), tm=512)
    return _mlp(h2, x1, w_up.astype(BF16), w_down.astype(BF16), tm=512, th=1024)


def kernel(x, norm1_g, w_in, conv_w, conv_b, dt_bias, a_log, d_skip, ssm_norm_g, q_norm_g, k_norm_g, rel_bias,
           w_att_branch, w_ssm_branch, w_out, norm2_g, w_up, w_down):
    batch, seq, d = x.shape
    x2 = x.reshape(batch * seq, d)
    for l in range(norm1_g.shape[0]):
        x2 = _block(x2, batch, seq, norm1_g[l], w_in[l], conv_w[l], conv_b[l], dt_bias[l], a_log[l], d_skip[l],
                    ssm_norm_g[l], q_norm_g[l], k_norm_g[l], rel_bias, w_att_branch[l], w_ssm_branch[l],
                    w_out[l], norm2_g[l], w_up[l], w_down[l])
    return x2.reshape(batch, seq, d)
```

```python
import functools
import math

import numpy as np
import jax
import jax.numpy as jnp
from jax import lax
from jax.experimental import pallas as pl
from jax.experimental.pallas import tpu as pltpu

F32 = jnp.float32
BF16 = jnp.bfloat16
I32 = jnp.int32

D_MODEL = 2048
ATT_HEADS = 16
ATT_KV_HEADS = 4
HEAD_DIM = 128
REP = ATT_HEADS // ATT_KV_HEADS
IDX_HEADS = 16
IDX_DIM = 64
TOPK_MAX = 256
N_BUCKETS = 32
MAX_DISTANCE = 128
SSM_INNER = 2 * D_MODEL
SSM_HEAD_DIM = 64
SSM_HEADS = SSM_INNER // SSM_HEAD_DIM
SSM_GROUPS = 8
SSM_STATE = 128
CONV_WIDTH = 4
CHUNK = 128
MLP_HIDDEN = 4 * D_MODEL
EPS = 1e-6

ATT_Q = ATT_HEADS * HEAD_DIM
ATT_KV = ATT_KV_HEADS * HEAD_DIM
IDX_Q = IDX_HEADS * IDX_DIM
SSM_BC = SSM_GROUPS * SSM_STATE
CONV_DIM = SSM_INNER + 2 * SSM_BC
SPLITS = (D_MODEL, D_MODEL, ATT_Q, ATT_KV, ATT_KV, IDX_Q, IDX_DIM, IDX_HEADS, SSM_INNER, CONV_DIM, SSM_HEADS)
_OFFS = tuple(int(v) for v in np.cumsum((0,) + SPLITS))
(O_GA, O_GS, O_Q, O_K, O_V, O_QI, O_KI, O_WI, O_Z, O_XBC, O_DT, _O_END) = _OFFS

M_GA = 0
M_GS = M_GA + D_MODEL
M_Q = M_GS + D_MODEL
M_K = M_Q + ATT_Q
M_V = M_K + ATT_KV
M_QI = M_V + ATT_KV
M_Z = M_QI + IDX_Q
M_XBC = M_Z + SSM_INNER
M_END = M_XBC + CONV_DIM
S_KI = 0
S_WI = S_KI + IDX_DIM
S_DT = S_WI + IDX_HEADS
S_END = 256

HEADS_PER_GROUP = SSM_HEADS // SSM_GROUPS
GROUP_W = HEADS_PER_GROUP * SSM_HEAD_DIM

LANES = 128
VMEM_LIMIT = 56 * 1024 * 1024

NEG_INF = float("-inf")
INT_MIN = -(2 ** 31)
KEY_NEG_INF = int(np.int32(np.uint32(0xFF800000) ^ np.uint32(0x7FFFFFFF)))


def _dot(a, b):
    return jnp.dot(a, b, preferred_element_type=F32)


def _dot_nt(a, b):
    return lax.dot_general(a, b, (((1,), (1,)), ((), ())), preferred_element_type=F32)


def _split3(x):
    hi = x.astype(BF16)
    r = x - hi.astype(F32)
    mid = r.astype(BF16)
    lo = (r - mid.astype(F32)).astype(BF16)
    return hi, mid, lo


def _silu(x):
    h = 0.5 * x
    return h + h * jnp.tanh(h)


def _sigmoid(x):
    return 1.0 / (1.0 + jnp.exp(-x))


def _params(sem):
    return pltpu.CompilerParams(dimension_semantics=sem, vmem_limit_bytes=VMEM_LIMIT)


def _norm_small_kernel(x_ref, g_ref, w_ref, xn_ref, sm_ref):
    x = x_ref[...]
    ms = jnp.mean(x * x, axis=-1, keepdims=True)
    xn = (x * lax.rsqrt(ms + EPS) * g_ref[...]).astype(BF16)
    xn_ref[...] = xn
    sm_ref[...] = _dot_nt(xn, w_ref[...].astype(BF16))


def _norm_small(x, g, w_small_t, tm):
    m, d = x.shape
    n = w_small_t.shape[0]
    return pl.pallas_call(
        _norm_small_kernel,
        grid=(m // tm,),
        in_specs=[pl.BlockSpec((tm, d), lambda i: (i, 0)),
                  pl.BlockSpec((1, d), lambda i: (0, 0)),
                  pl.BlockSpec((n, d), lambda i: (0, 0))],
        out_specs=[pl.BlockSpec((tm, d), lambda i: (i, 0)),
                   pl.BlockSpec((tm, n), lambda i: (i, 0))],
        out_shape=[jax.ShapeDtypeStruct((m, d), BF16), jax.ShapeDtypeStruct((m, n), F32)],
        compiler_params=_params(("parallel",)),
        name="norm_small_proj",
    )(x, g, w_small_t)


IN_TN = 1024
IN_ALIGNED_TILES = O_KI // IN_TN
IN_SHIFT = O_Z - M_Z
IN_NEXT = 128
IN_ROWS = 256


def _in_proj_kernel(xn_ref, wa_ref, wn_ref, o_ref, w_scr):
    j = pl.program_id(0)
    i = pl.program_id(1)

    @pl.when((i == 0) & (j < IN_ALIGNED_TILES))
    def _():
        for r0 in range(0, IN_TN, IN_ROWS):
            w_scr[r0:r0 + IN_ROWS, :] = wa_ref[r0:r0 + IN_ROWS, :].astype(BF16)

    @pl.when((i == 0) & (j >= IN_ALIGNED_TILES))
    def _():
        for r0 in range(0, IN_TN - IN_SHIFT, IN_ROWS):
            r1 = min(r0 + IN_ROWS, IN_TN - IN_SHIFT)
            w_scr[r0:r1, :] = wa_ref[r0 + IN_SHIFT:r1 + IN_SHIFT, :].astype(BF16)
        w_scr[IN_TN - IN_SHIFT:IN_TN, :] = wn_ref[0:IN_SHIFT, :].astype(BF16)

    o_ref[...] = _dot_nt(xn_ref[...], w_scr[...]).astype(o_ref.dtype)


def _in_proj(xn, w_t, tm):
    m, d = xn.shape
    assert O_KI % IN_TN == 0 and M_END % IN_TN == 0 and IN_TN % IN_NEXT == 0
    assert 0 < IN_SHIFT <= IN_NEXT and IN_SHIFT % 16 == 0
    return pl.pallas_call(
        _in_proj_kernel,
        grid=(M_END // IN_TN, m // tm),
        in_specs=[pl.BlockSpec((tm, d), lambda j, i: (i, 0)),
                  pl.BlockSpec((IN_TN, d), lambda j, i: (j, 0)),
                  pl.BlockSpec((IN_NEXT, d), lambda j, i: ((j + 1) * (IN_TN // IN_NEXT), 0))],
        out_specs=pl.BlockSpec((tm, IN_TN), lambda j, i: (i, j)),
        out_shape=jax.ShapeDtypeStruct((m, M_END), BF16),
        scratch_shapes=[pltpu.VMEM((IN_TN, d), BF16)],
        compiler_params=_params(("arbitrary", "arbitrary")),
        name="in_proj",
    )(xn, w_t, w_t)


LOG2E = math.log2(math.e)
VT_ROWS = HEAD_DIM + 16


def _qk_prep_kernel(q_ref, k_ref, v_ref, qi_ref, sm_ref, qg_ref, kg_ref,
                    qt_ref, kn_ref, vt_ref, qit_ref, wt_ref, ki_ref):
    qg = qg_ref[...]
    for h in range(ATT_HEADS):
        x = q_ref[:, h * HEAD_DIM:(h + 1) * HEAD_DIM].astype(F32)
        ms = jnp.mean(x * x, axis=-1, keepdims=True)
        y = x * lax.rsqrt(ms + EPS) * qg * (HEAD_DIM ** -0.5 * LOG2E)
        qt_ref[0, h] = y.T.astype(BF16)
    kg = kg_ref[...]
    for h in range(ATT_KV_HEADS):
        x = k_ref[:, h * HEAD_DIM:(h + 1) * HEAD_DIM].astype(F32)
        ms = jnp.mean(x * x, axis=-1, keepdims=True)
        kn_ref[:, h * HEAD_DIM:(h + 1) * HEAD_DIM] = (x * lax.rsqrt(ms + EPS) * kg).astype(BF16)
        vt_ref[0, h, 0, 0:HEAD_DIM, :] = v_ref[:, h * HEAD_DIM:(h + 1) * HEAD_DIM].astype(F32).T.astype(BF16)
        vt_ref[0, h, 0, HEAD_DIM:VT_ROWS, :] = jnp.ones((VT_ROWS - HEAD_DIM, v_ref.shape[0]), BF16)
    for p in range(IDX_Q // LANES):
        qit_ref[0, p * LANES:(p + 1) * LANES, :] = qi_ref[:, p * LANES:(p + 1) * LANES].astype(F32).T.astype(BF16)
    sm_t = sm_ref[:, 0:LANES].T
    wt_ref[0] = sm_t[S_WI:S_WI + IDX_HEADS, :] * (IDX_HEADS ** -0.5 * IDX_DIM ** -0.5)
    ki_ref[...] = sm_ref[:, S_KI:S_KI + IDX_DIM].astype(BF16)


def _qk_prep(main, small, qg, kg, batch, seq, tq):
    t = main.shape[0]
    nq = seq // tq
    return pl.pallas_call(
        _qk_prep_kernel,
        grid=(batch, nq),
        in_specs=[pl.BlockSpec((tq, ATT_Q), lambda b, i: (b * nq + i, M_Q // ATT_Q)),
                  pl.BlockSpec((tq, ATT_KV), lambda b, i: (b * nq + i, M_K // ATT_KV)),
                  pl.BlockSpec((tq, ATT_KV), lambda b, i: (b * nq + i, M_V // ATT_KV)),
                  pl.BlockSpec((tq, IDX_Q), lambda b, i: (b * nq + i, M_QI // IDX_Q)),
                  pl.BlockSpec((tq, S_END), lambda b, i: (b * nq + i, 0)),
                  pl.BlockSpec((1, HEAD_DIM), lambda b, i: (0, 0)),
                  pl.BlockSpec((1, HEAD_DIM), lambda b, i: (0, 0))],
        out_specs=[pl.BlockSpec((1, ATT_HEADS, HEAD_DIM, tq), lambda b, i: (b * nq + i, 0, 0, 0)),
                   pl.BlockSpec((tq, ATT_KV), lambda b, i: (b * nq + i, 0)),
                   pl.BlockSpec((1, ATT_KV_HEADS, 1, VT_ROWS, tq), lambda b, i: (b, 0, i, 0, 0)),
                   pl.BlockSpec((1, IDX_Q, tq), lambda b, i: (b * nq + i, 0, 0)),
                   pl.BlockSpec((1, IDX_HEADS, tq), lambda b, i: (b * nq + i, 0, 0)),
                   pl.BlockSpec((tq, IDX_DIM), lambda b, i: (b * nq + i, 0))],
        out_shape=[jax.ShapeDtypeStruct((t // tq, ATT_HEADS, HEAD_DIM, tq), BF16),
                   jax.ShapeDtypeStruct((t, ATT_KV), BF16),
                   jax.ShapeDtypeStruct((batch, ATT_KV_HEADS, nq, VT_ROWS, tq), BF16),
                   jax.ShapeDtypeStruct((t // tq, IDX_Q, tq), BF16),
                   jax.ShapeDtypeStruct((t // tq, IDX_HEADS, tq), F32),
                   jax.ShapeDtypeStruct((t, IDX_DIM), BF16)],
        compiler_params=_params(("parallel", "parallel")),
        name="qk_prep",
    )(main, main, main, main, small, qg, kg)


DSA_TQ = 256
DSA_TK = 256
CNT_WAYS = 4


def _t5_bucket_np(dist):
    n = np.maximum(dist, 0)
    max_exact = N_BUCKETS // 2
    nf = np.maximum(n, 1).astype(np.float32)
    ratio = (np.log(nf / np.float32(max_exact)) / np.float32(math.log(MAX_DISTANCE / max_exact))
             * np.float32(N_BUCKETS - max_exact))
    large = max_exact + ratio.astype(np.int32)
    large = np.minimum(large, N_BUCKETS - 1)
    return np.where(n < max_exact, n, large).astype(np.int32)


def _bias_bucket_tiles(tq, tk):
    r = np.arange(tq)[None, :]
    c = np.arange(tk)[:, None]
    d0 = _t5_bucket_np(r - c)
    d1 = _t5_bucket_np(tk + r - c)
    assert np.all(_t5_bucket_np(np.arange(tk + 1, 8 * tk)) == N_BUCKETS - 1)
    return np.stack([d0, d1]).astype(np.int32)


def _dsa_kernel(relb_ref, bidx_ref, qt_ref, qit_ref, wt_ref, k_ref, vt_ref, ki_ref, o_ref,
                key_ref, plane_ref, madd_ref, bias_ref, s_buf, p_buf, st_ref, acc_ref, *, tq, tk, nkc, topk):
    b = pl.program_id(0)
    i = pl.program_id(1)
    g = pl.program_id(2)
    neg_slot = nkc

    @pl.when((b == 0) & (i == 0) & (g == 0))
    def _init():
        madd_ref[neg_slot] = jnp.full((tk, tq), NEG_INF, F32)
        for t in range(2):
            bt = bidx_ref[t]

            def head_body(h, carry):
                far = relb_ref[N_BUCKETS - 1, h]

                def bucket_body(bk, acc):
                    return jnp.where(bt == bk, (relb_ref[bk, h] - far) * LOG2E, acc)

                bias_ref[t, h] = lax.fori_loop(0, N_BUCKETS, bucket_body, jnp.zeros((tk, tq), F32))
                return carry

            lax.fori_loop(0, ATT_HEADS, head_body, 0)

    @pl.when(g == 0)
    def _select():
        nj = i + 1
        qpos = i * tq + lax.broadcasted_iota(I32, (tk, tq), 1)

        def score_chunk(j, carry):
            kc = ki_ref[pl.ds(pl.multiple_of(j * tk, tk), tk), :]
            acc = jnp.zeros((tk, tq), F32)
            for h in range(IDX_HEADS):
                z = _dot(kc, qit_ref[0, h * IDX_DIM:(h + 1) * IDX_DIM, :])
                acc = acc + jnp.maximum(z, 0.0) * wt_ref[0, h:h + 1, :]
            kpos = j * tk + lax.broadcasted_iota(I32, (tk, tq), 0)
            acc = jnp.where(kpos <= qpos, acc, NEG_INF)
            bits = pltpu.bitcast(acc, I32)
            bits = jnp.where(bits == INT_MIN, 0, bits)
            key = bits ^ ((bits >> 31) & 0x7FFFFFFF)
            key_ref[j] = key
            plane_ref[0, j] = pltpu.bitcast(bits & jnp.int32(-65536), F32).astype(BF16)
            plane_ref[1, j] = ((key >> 8) & 0xFF).astype(F32).astype(BF16)
            plane_ref[2, j] = (key & 0xFF).astype(F32).astype(BF16)
            return carry

        lax.fori_loop(0, nj, score_chunk, 0)

        kf = float(topk)
        one_b = jnp.ones((tk, tq), BF16)
        zero_b = jnp.zeros((tk, tq), BF16)
        grp = 16 * CNT_WAYS

        def count(plane, cand_b, strict):
            def body(j, cnt):
                pv = plane_ref[plane, j]
                hit = jnp.where(pv > cand_b if strict else pv >= cand_b, one_b, zero_b)
                for r0 in range(0, tk, grp):
                    cnt = cnt + hit[r0:r0 + grp]
                return cnt

            cnt = lax.fori_loop(0, nj, body, jnp.zeros((grp, tq), BF16))
            return jnp.sum(cnt.astype(F32), axis=0, keepdims=True)

        def keep_ties(src, dst, tie_b):
            def body(j, carry):
                plane_ref[dst, j] = jnp.where(plane_ref[src, j] == tie_b, plane_ref[dst, j], -one_b)
                return carry

            lax.fori_loop(0, nj, body, 0)

        def top_digit_float(p16):
            fb = (p16 & 0xFFFF) ^ jnp.where(p16 < 0, 0x7FFF, 0)
            return pltpu.bitcast(lax.shift_left(fb, 16), F32).astype(BF16)

        def top_iter(it, prefix):
            cand = prefix + lax.shift_left(jnp.int32(1), 15 - it)
            return jnp.where(count(0, top_digit_float(cand), False) >= kf, cand, prefix)

        d_top = lax.fori_loop(0, 16, top_iter, jnp.full((1, tq), -32768, I32))
        t_top = top_digit_float(d_top)
        above = count(0, t_top, True)

        def byte_digit(plane, above_n):
            def it_body(it, prefix):
                cand = prefix + lax.shift_left(jnp.int32(1), 7 - it)
                tot = above_n + count(plane, cand.astype(F32).astype(BF16), False)
                return jnp.where(tot >= kf, cand, prefix)

            return lax.fori_loop(0, 8, it_body, jnp.zeros((1, tq), I32))

        keep_ties(0, 1, t_top)
        d_mid = byte_digit(1, above)
        t_mid = d_mid.astype(F32).astype(BF16)
        above = above + count(1, t_mid, True)
        keep_ties(1, 2, t_mid)
        d_low = byte_digit(2, above)
        thr = lax.shift_left(d_top, 16) | lax.shift_left(d_mid, 8) | d_low
        thr = jnp.maximum(thr, KEY_NEG_INF + 1)

        def madd_chunk(j, carry):
            madd_ref[j] = jnp.where(key_ref[j] >= thr, 0.0, NEG_INF)
            return carry

        lax.fori_loop(0, nj, madd_chunk, 0)

    qt = jnp.concatenate([qt_ref[0, r] for r in range(REP)], axis=1)
    c0 = jnp.maximum(i - 1, 0)
    c1 = jnp.minimum(c0 + 1, nkc - 1)
    n_far = (c0 + 1) // 2

    def scores_to(slot, ca, madd_c, bias_c):
        rows = pl.ds(pl.multiple_of(ca * tk, tk), 2 * tk)
        add = jnp.concatenate([madd_c if bias_c is None else madd_c + bias_c[r] for r in range(REP)], axis=1)
        s = _dot(k_ref[rows, :], qt) + add
        s_buf[slot] = s
        st_ref[4 + slot:5 + slot, :] = jnp.max(s, axis=0, keepdims=True)

    def softmax_to(slot):
        m = st_ref[0:1, :]
        m_new = jnp.maximum(m, st_ref[4 + slot:5 + slot, :])
        m_safe = jnp.where(m_new == NEG_INF, 0.0, m_new)
        alpha = jnp.exp2(m - m_safe)
        p_buf[slot] = jnp.exp2((s_buf[slot] - m_safe).astype(BF16))
        st_ref[0:1, :] = m_new
        st_ref[2 + slot:3 + slot, :] = alpha

    def values_from(slot, ca, cb):
        vt_c = jnp.concatenate([vt_ref[0, 0, ca], vt_ref[0, 0, cb]], axis=1)
        acc_ref[...] = st_ref[2 + slot:3 + slot, :] * acc_ref[...] + _dot(vt_c, p_buf[slot])

    def far_scores_to(slot, k):
        ca = jnp.minimum(2 * (k - 1), nkc - 2)
        cb = jnp.where(ca + 1 >= c0, neg_slot, ca + 1)
        scores_to(slot, ca, jnp.concatenate([madd_ref[ca], madd_ref[cb]], axis=0), None)

    def far_stage(k, cur, nxt):
        far_scores_to(nxt, k + 1)
        softmax_to(cur)
        values_from(cur, 2 * (k - 1), 2 * (k - 1) + 1)

    st_ref[0:1, :] = jnp.full((1, REP * tq), NEG_INF, F32)
    acc_ref[...] = jnp.zeros_like(acc_ref)

    first = i == 0
    t_a = jnp.where(first, 0, 1)
    idx_b = jnp.where(first, neg_slot, c0 + 1)
    madd_near = jnp.concatenate([madd_ref[c0], madd_ref[idx_b]], axis=0)
    bias_near = [jnp.concatenate([bias_ref[t_a, g * REP + r], bias_ref[0, g * REP + r]], axis=0)
                 for r in range(REP)]
    scores_to(0, c0, madd_near, bias_near)
    far_scores_to(1, 1)
    softmax_to(0)
    values_from(0, c0, c1)

    def far_pair(u, carry):
        far_stage(2 * u + 1, 1, 0)
        far_stage(2 * u + 2, 0, 1)
        return carry

    lax.fori_loop(0, n_far // 2, far_pair, 0)

    @pl.when(n_far % 2 == 1)
    def _():
        far_stage(n_far, 1, 0)

    out = acc_ref[0:HEAD_DIM, :] / acc_ref[HEAD_DIM:HEAD_DIM + 1, :]
    for r in range(REP):
        o_ref[:, r * HEAD_DIM:(r + 1) * HEAD_DIM] = out[:, r * tq:(r + 1) * tq].T.astype(o_ref.dtype)


def _dsa(qt, kn, vt, qit, wt, ki, rel_bias, batch, seq):
    tq, tk = DSA_TQ, DSA_TK
    nq = seq // tq
    nkc = seq // tk
    topk = min(TOPK_MAX, seq // 4)
    bidx = jnp.asarray(_bias_bucket_tiles(tq, tk))
    kern = functools.partial(_dsa_kernel, tq=tq, tk=tk, nkc=nkc, topk=topk)
    gw = REP * HEAD_DIM
    return pl.pallas_call(
        kern,
        grid=(batch, nq, ATT_KV_HEADS),
        in_specs=[pl.BlockSpec(memory_space=pltpu.SMEM),
                  pl.BlockSpec((2, tk, tq), lambda b, i, g: (0, 0, 0)),
                  pl.BlockSpec((1, REP, HEAD_DIM, tq), lambda b, i, g: (b * nq + i, g, 0, 0)),
                  pl.BlockSpec((1, IDX_Q, tq), lambda b, i, g: (b * nq + i, 0, 0)),
                  pl.BlockSpec((1, IDX_HEADS, tq), lambda b, i, g: (b * nq + i, 0, 0)),
                  pl.BlockSpec((seq, HEAD_DIM), lambda b, i, g: (b, g)),
                  pl.BlockSpec((1, 1, nkc, VT_ROWS, tk), lambda b, i, g: (b, g, 0, 0, 0)),
                  pl.BlockSpec((seq, IDX_DIM), lambda b, i, g: (b, 0))],
        out_specs=pl.BlockSpec((tq, gw), lambda b, i, g: (b * nq + i, g)),
        out_shape=jax.ShapeDtypeStruct((batch * seq, ATT_Q), BF16),
        scratch_shapes=[pltpu.VMEM((nkc, tk, tq), I32),
                        pltpu.VMEM((3, nkc, tk, tq), BF16),
                        pltpu.VMEM((nkc + 1, tk, tq), F32),
                        pltpu.VMEM((2, ATT_HEADS, tk, tq), F32),
                        pltpu.VMEM((2, 2 * tk, REP * tq), F32),
                        pltpu.VMEM((2, 2 * tk, REP * tq), BF16),
                        pltpu.VMEM((8, REP * tq), F32),
                        pltpu.VMEM((VT_ROWS, REP * tq), F32)],
        compiler_params=_params(("arbitrary", "arbitrary", "arbitrary")),
        name="dsa_attention",
    )(rel_bias, bidx, qt, qit, wt, kn, vt, ki)


E_ROWS = 3 * CHUNK + 16
CONV_HALO = 16


def _conv_shift_matrix():
    ext = CONV_HALO + CHUNK
    s = np.zeros((CHUNK, CONV_WIDTH * ext), np.float32)
    for k in range(CONV_WIDTH):
        t = np.arange(CHUNK)
        s[t, k * ext + CONV_HALO + t - (CONV_WIDTH - 1) + k] = 1.0
    return s


def _ssd_kernel(xbc_ref, halo_ref, z_ref, sm_ref, shift_ref, cw_ref, cb_ref, dtb_ref, alog_ref, dsk_ref, ng_ref,
                o_ref, xs_ref, bm_ref, cm_ref, state_ref, ypre_ref, acg_ref, actg_ref):
    c = pl.program_id(1)
    L = CHUNK

    @pl.when(c == 0)
    def _():
        state_ref[...] = jnp.zeros_like(state_ref)

    halo_on = (c > 0).astype(BF16)
    shift = shift_ref[...]
    cblk = GROUP_W
    for cbi in range(CONV_DIM // cblk):
        cols = slice(cbi * cblk, (cbi + 1) * cblk)
        ext = jnp.concatenate([halo_ref[:, cols] * halo_on, xbc_ref[:, cols]], axis=0)
        wtap = cw_ref[:, cols].astype(BF16)
        prods = jnp.concatenate([ext * wtap[kk:kk + 1, :] for kk in range(CONV_WIDTH)], axis=0)
        y = _silu(cb_ref[:, cols] + _dot(shift, prods))
        if cbi < SSM_GROUPS:
            xs_ref[cbi] = y
        else:
            per = cblk // SSM_STATE
            for u in range(per):
                gi = (cbi - SSM_GROUPS) * per + u
                piece = y[:, u * SSM_STATE:(u + 1) * SSM_STATE]
                if gi < SSM_GROUPS:
                    bm_ref[gi] = piece
                else:
                    cm_ref[gi - SSM_GROUPS] = piece

    dt_in = sm_ref[:, S_DT:S_DT + SSM_HEADS] + dtb_ref[...]
    dt_act = jnp.maximum(dt_in, 0.0) + jnp.log1p(jnp.exp(-jnp.abs(dt_in)))
    a = dt_act * (-jnp.exp(alog_ref[...]))
    ri = lax.broadcasted_iota(I32, (L, L), 0)
    ci = lax.broadcasted_iota(I32, (L, L), 1)
    tril = ri >= ci
    tri_b = jnp.where(tril, 1.0, 0.0).astype(BF16)
    a_cum = sum(_dot(tri_b, p) for p in _split3(a))
    eye_b = jnp.where(lax.broadcasted_iota(I32, (SSM_HEADS, SSM_HEADS), 0)
                      == lax.broadcasted_iota(I32, (SSM_HEADS, SSM_HEADS), 1), 1.0, 0.0).astype(BF16)
    a_cum_t = sum(_dot_nt(eye_b, p) for p in _split3(a_cum))
    a_last = a_cum[L - 1:L, :]
    for gi in range(SSM_GROUPS):
        acg_ref[gi] = a_cum[:, gi * HEADS_PER_GROUP:(gi + 1) * HEADS_PER_GROUP]
        actg_ref[gi] = a_cum_t[gi * HEADS_PER_GROUP:(gi + 1) * HEADS_PER_GROUP, :]
    cd3 = _split3(jnp.exp(a_last))
    ds3 = _split3(dsk_ref[...])
    extras = jnp.concatenate([p.astype(F32) for p in cd3 + ds3]
                             + [jnp.zeros((E_ROWS - 3 * L - 6, SSM_HEADS), F32)], axis=0)
    e_mat = jnp.concatenate([dt_act, jnp.exp(a_cum), jnp.exp(a_last - a_cum), extras], axis=0).astype(BF16)

    lane = lax.broadcasted_iota(I32, (L, LANES), 1)
    lo_mask = lane < SSM_HEAD_DIM

    def group_body(gi, carry):
        xs = xs_ref[gi]
        bg = bm_ref[gi]
        cg_b = cm_ref[gi].astype(BF16)
        hsel = (lax.broadcasted_iota(I32, (SSM_HEADS, GROUP_W), 0)
                == gi * HEADS_PER_GROUP + lax.broadcasted_iota(I32, (SSM_HEADS, GROUP_W), 1) // SSM_HEAD_DIM)
        ex = _dot(e_mat, jnp.where(hsel, 1.0, 0.0).astype(BF16))
        dt_rep = ex[0:L]
        expa_rep = ex[L:2 * L]
        dte_rep = ex[2 * L:3 * L]
        cd_rep = ex[3 * L:3 * L + 1] + ex[3 * L + 1:3 * L + 2] + ex[3 * L + 2:3 * L + 3]
        dsk_rep = ex[3 * L + 3:3 * L + 4] + ex[3 * L + 4:3 * L + 5] + ex[3 * L + 5:3 * L + 6]

        xd = xs * dt_rep
        xd_b = xd.astype(BF16)
        cb = _dot_nt(cg_b, bg.astype(BF16))
        acg = acg_ref[gi]
        actg = actg_ref[gi]
        pairs = []
        for pj in range(HEADS_PER_GROUP // 2):
            gmat = []
            for e in (2 * pj, 2 * pj + 1):
                seg = acg[:, e:e + 1] - actg[e:e + 1, :]
                dec = jnp.exp(jnp.where(tril, seg, NEG_INF))
                gmat.append((cb * dec).astype(BF16))
            xp = xd_b[:, pj * LANES:(pj + 1) * LANES]
            zero = jnp.zeros_like(xp)
            pairs.append(_dot(gmat[0], jnp.where(lo_mask, xp, zero))
                         + _dot(gmat[1], jnp.where(lo_mask, zero, xp)))
        y_diag = jnp.concatenate(pairs, axis=1)

        st = state_ref[gi]
        y_off = _dot(cg_b, st.astype(BF16)) * expa_rep
        xdd = (xd * dte_rep).astype(BF16)
        state_ref[gi] = st * cd_rep + _dot(bg.T.astype(BF16), xdd)
        ypre_ref[gi] = y_diag + y_off + dsk_rep * xs
        return carry

    lax.fori_loop(0, SSM_GROUPS, group_body, 0, unroll=8)

    for gi in range(SSM_GROUPS):
        cols = slice(gi * GROUP_W, (gi + 1) * GROUP_W)
        zz = z_ref[:, cols].astype(F32)
        y = ypre_ref[gi] * _silu(zz)
        ms = jnp.mean(y * y, axis=-1, keepdims=True)
        o_ref[:, cols] = (y * lax.rsqrt(ms + EPS) * ng_ref[:, cols]).astype(o_ref.dtype)


def _ssd(main, small, conv_w, conv_b, dt_bias, a_log, d_skip, norm_g, batch, seq):
    nc = seq // CHUNK
    hb = CHUNK // CONV_HALO
    row = lambda b, c: b * nc + c
    full = lambda shape: pl.BlockSpec(shape, lambda b, c: (0,) * len(shape))
    shift = jnp.asarray(_conv_shift_matrix(), BF16)
    return pl.pallas_call(
        _ssd_kernel,
        grid=(batch, nc),
        in_specs=[pl.BlockSpec((CHUNK, CONV_DIM), lambda b, c: (row(b, c), M_XBC // CONV_DIM)),
                  pl.BlockSpec((CONV_HALO, CONV_DIM),
                               lambda b, c: (jnp.maximum(row(b, c) * hb - 1, 0), M_XBC // CONV_DIM)),
                  pl.BlockSpec((CHUNK, SSM_INNER), lambda b, c: (row(b, c), M_Z // SSM_INNER)),
                  pl.BlockSpec((CHUNK, S_END), lambda b, c: (row(b, c), 0)),
                  full((CHUNK, CONV_WIDTH * (CONV_HALO + CHUNK))),
                  full((CONV_WIDTH, CONV_DIM)), full((1, CONV_DIM)), full((1, SSM_HEADS)),
                  full((1, SSM_HEADS)), full((1, SSM_HEADS)), full((1, SSM_INNER))],
        out_specs=pl.BlockSpec((CHUNK, SSM_INNER), lambda b, c: (row(b, c), 0)),
        out_shape=jax.ShapeDtypeStruct((batch * seq, SSM_INNER), BF16),
        scratch_shapes=[pltpu.VMEM((SSM_GROUPS, CHUNK, GROUP_W), F32),
                        pltpu.VMEM((SSM_GROUPS, CHUNK, SSM_STATE), F32),
                        pltpu.VMEM((SSM_GROUPS, CHUNK, SSM_STATE), F32),
                        pltpu.VMEM((SSM_GROUPS, SSM_STATE, GROUP_W), F32),
                        pltpu.VMEM((SSM_GROUPS, CHUNK, GROUP_W), F32),
                        pltpu.VMEM((SSM_GROUPS, CHUNK, HEADS_PER_GROUP), F32),
                        pltpu.VMEM((SSM_GROUPS, HEADS_PER_GROUP, CHUNK), F32)],
        compiler_params=_params(("arbitrary", "arbitrary")),
        name="ssd_scan",
    )(main, main, main, small, shift, conv_w, conv_b, dt_bias, a_log, d_skip, norm_g)


def _merge_kernel(att_ref, y_ref, ga_ref, gs_ref, wa_ref, ws_ref, o_ref):
    pa = _dot(att_ref[...], wa_ref[...])
    ps = _dot(y_ref[...], ws_ref[...])
    o_ref[...] = (_sigmoid(ga_ref[...].astype(F32)) * pa + _sigmoid(gs_ref[...].astype(F32)) * ps).astype(o_ref.dtype)


def _merge(att, y, main, wa, ws, tm, tn):
    t = att.shape[0]
    return pl.pallas_call(
        _merge_kernel,
        grid=(t // tm, D_MODEL // tn),
        in_specs=[pl.BlockSpec((tm, ATT_Q), lambda i, j: (i, 0)),
                  pl.BlockSpec((tm, SSM_INNER), lambda i, j: (i, 0)),
                  pl.BlockSpec((tm, tn), lambda i, j: (i, M_GA // tn + j)),
                  pl.BlockSpec((tm, tn), lambda i, j: (i, M_GS // tn + j)),
                  pl.BlockSpec((ATT_Q, tn), lambda i, j: (0, j)),
                  pl.BlockSpec((SSM_INNER, tn), lambda i, j: (0, j))],
        out_specs=pl.BlockSpec((tm, tn), lambda i, j: (i, j)),
        out_shape=jax.ShapeDtypeStruct((t, D_MODEL), BF16),
        compiler_params=_params(("parallel", "arbitrary")),
        name="gated_merge",
    )(att, y, main, main, wa, ws)


def _out_proj_kernel(m_ref, w_ref, x_ref, g_ref, x1_ref, h2_ref):
    x1 = x_ref[...] + _dot(m_ref[...], w_ref[...])
    x1_ref[...] = x1
    ms = jnp.mean(x1 * x1, axis=-1, keepdims=True)
    h2_ref[...] = (x1 * lax.rsqrt(ms + EPS) * g_ref[...]).astype(BF16)


def _out_proj(merged, w, x, g, tm):
    t = x.shape[0]
    return pl.pallas_call(
        _out_proj_kernel,
        grid=(t // tm,),
        in_specs=[pl.BlockSpec((tm, D_MODEL), lambda i: (i, 0)),
                  pl.BlockSpec((D_MODEL, D_MODEL), lambda i: (0, 0)),
                  pl.BlockSpec((tm, D_MODEL), lambda i: (i, 0)),
                  pl.BlockSpec((1, D_MODEL), lambda i: (0, 0))],
        out_specs=[pl.BlockSpec((tm, D_MODEL), lambda i: (i, 0)),
                   pl.BlockSpec((tm, D_MODEL), lambda i: (i, 0))],
        out_shape=[jax.ShapeDtypeStruct((t, D_MODEL), F32),
                   jax.ShapeDtypeStruct((t, D_MODEL), BF16)],
        compiler_params=_params(("parallel",)),
        name="out_proj_norm",
    )(merged, w, x, g)


def _mlp_kernel(h_ref, x1_ref, wu_ref, wd_ref, o_ref):
    @pl.when(pl.program_id(1) == 0)
    def _():
        o_ref[...] = x1_ref[...]

    u = _dot(h_ref[...], wu_ref[...])
    u = jnp.square(jnp.maximum(u, 0.0)).astype(BF16)
    o_ref[...] += _dot(u, wd_ref[...])


def _mlp(h2, x1, wu, wd, tm, th):
    t = h2.shape[0]
    return pl.pallas_call(
        _mlp_kernel,
        grid=(t // tm, MLP_HIDDEN // th),
        in_specs=[pl.BlockSpec((tm, D_MODEL), lambda i, j: (i, 0)),
                  pl.BlockSpec((tm, D_MODEL), lambda i, j: (i, 0)),
                  pl.BlockSpec((D_MODEL, th), lambda i, j: (0, j)),
                  pl.BlockSpec((th, D_MODEL), lambda i, j: (j, 0))],
        out_specs=pl.BlockSpec((tm, D_MODEL), lambda i, j: (i, 0)),
        out_shape=jax.ShapeDtypeStruct((t, D_MODEL), F32),
        compiler_params=_params(("parallel", "arbitrary")),
        name="relu2_mlp",
    )(h2, x1, wu, wd)


def _pack_w_small(w_t):
    assert (O_GA, O_KI, O_Z, O_DT) == (M_GA, M_Z, O_WI + IDX_HEADS, O_Z + M_END - M_Z)
    return jnp.concatenate([w_t[O_KI:O_Z], w_t[O_DT:O_DT + SSM_HEADS],
                            jnp.zeros((S_END - S_DT - SSM_HEADS, w_t.shape[1]), w_t.dtype)], axis=0)


def _block(x2, batch, seq, norm1_g, w_in, conv_w, conv_b, dt_bias, a_log, d_skip, ssm_norm_g, q_norm_g,
           k_norm_g, rel_bias, w_att_branch, w_ssm_branch, w_out, norm2_g, w_up, w_down):
    row = lambda v: v.reshape(1, -1)
    w_t = w_in.T
    xn, small = _norm_small(x2, row(norm1_g), _pack_w_small(w_t), tm=min(1024, x2.shape[0]))
    main = _in_proj(xn, w_t, tm=min(1024, x2.shape[0]))
    qt, kn, vt, qit, wt, ki = _qk_prep(main, small, row(q_norm_g), row(k_norm_g), batch, seq, tq=DSA_TQ)
    att = _dsa(qt, kn, vt, qit, wt, ki, rel_bias, batch, seq)
    y = _ssd(main, small, conv_w, row(conv_b), row(dt_bias), row(a_log), row(d_skip), row(ssm_norm_g),
             batch, seq)
    merged = _merge(att, y, main, w_att_branch.astype(BF16), w_ssm_branch.astype(BF16), tm=512, tn=1024)
    x1, h2 = _out_proj(merged, w_out.astype(BF16), x2, row(norm2_g), tm=512)
    return _mlp(h2, x1, w_up.astype(BF16), w_down.astype(BF16), tm=512, th=1024)


def kernel(x, norm1_g, w_in, conv_w, conv_b, dt_bias, a_log, d_skip, ssm_norm_g, q_norm_g, k_norm_g, rel_bias,
           w_att_branch, w_ssm_branch, w_out, norm2_g, w_up, w_down):
    batch, seq, d = x.shape
    x2 = x.reshape(batch * seq, d)
    for l in range(norm1_g.shape[0]):
        x2 = _block(x2, batch, seq, norm1_g[l], w_in[l], conv_w[l], conv_b[l], dt_bias[l], a_log[l], d_skip[l],
                    ssm_norm_g[l], q_norm_g[l], k_norm_g[l], rel_bias, w_att_branch[l], w_ssm_branch[l],
                    w_out[l], norm2_g[l], w_up[l], w_down[l])
    return x2.reshape(batch, seq, d)
```

```python
import functools
import math

import numpy as np
import jax
import jax.numpy as jnp
from jax import lax
from jax.experimental import pallas as pl
from jax.experimental.pallas import tpu as pltpu

F32 = jnp.float32
BF16 = jnp.bfloat16
I32 = jnp.int32

D_MODEL = 2048
ATT_HEADS = 16
ATT_KV_HEADS = 4
HEAD_DIM = 128
REP = ATT_HEADS // ATT_KV_HEADS
IDX_HEADS = 16
IDX_DIM = 64
TOPK_MAX = 256
N_BUCKETS = 32
MAX_DISTANCE = 128
SSM_INNER = 2 * D_MODEL
SSM_HEAD_DIM = 64
SSM_HEADS = SSM_INNER // SSM_HEAD_DIM
SSM_GROUPS = 8
SSM_STATE = 128
CONV_WIDTH = 4
CHUNK = 128
MLP_HIDDEN = 4 * D_MODEL
EPS = 1e-6

ATT_Q = ATT_HEADS * HEAD_DIM
ATT_KV = ATT_KV_HEADS * HEAD_DIM
IDX_Q = IDX_HEADS * IDX_DIM
SSM_BC = SSM_GROUPS * SSM_STATE
CONV_DIM = SSM_INNER + 2 * SSM_BC
SPLITS = (D_MODEL, D_MODEL, ATT_Q, ATT_KV, ATT_KV, IDX_Q, IDX_DIM, IDX_HEADS, SSM_INNER, CONV_DIM, SSM_HEADS)
_OFFS = tuple(int(v) for v in np.cumsum((0,) + SPLITS))
(O_GA, O_GS, O_Q, O_K, O_V, O_QI, O_KI, O_WI, O_Z, O_XBC, O_DT, _O_END) = _OFFS

M_GA = 0
M_GS = M_GA + D_MODEL
M_Q = M_GS + D_MODEL
M_K = M_Q + ATT_Q
M_V = M_K + ATT_KV
M_QI = M_V + ATT_KV
M_Z = M_QI + IDX_Q
M_XBC = M_Z + SSM_INNER
M_END = M_XBC + CONV_DIM
S_KI = 0
S_WI = S_KI + IDX_DIM
S_DT = S_WI + IDX_HEADS
S_END = 256

HEADS_PER_GROUP = SSM_HEADS // SSM_GROUPS
GROUP_W = HEADS_PER_GROUP * SSM_HEAD_DIM

LANES = 128
VMEM_LIMIT = 56 * 1024 * 1024

NEG_INF = float("-inf")
INT_MIN = -(2 ** 31)
KEY_NEG_INF = int(np.int32(np.uint32(0xFF800000) ^ np.uint32(0x7FFFFFFF)))


def _dot(a, b):
    return jnp.dot(a, b, preferred_element_type=F32)


def _dot_nt(a, b):
    return lax.dot_general(a, b, (((1,), (1,)), ((), ())), preferred_element_type=F32)


def _split3(x):
    hi = x.astype(BF16)
    r = x - hi.astype(F32)
    mid = r.astype(BF16)
    lo = (r - mid.astype(F32)).astype(BF16)
    return hi, mid, lo


def _silu(x):
    h = 0.5 * x
    return h + h * jnp.tanh(h)


def _sigmoid(x):
    return 1.0 / (1.0 + jnp.exp(-x))


def _params(sem):
    return pltpu.CompilerParams(dimension_semantics=sem, vmem_limit_bytes=VMEM_LIMIT)


def _norm_small_kernel(x_ref, g_ref, w_ref, xn_ref, sm_ref):
    x = x_ref[...]
    ms = jnp.mean(x * x, axis=-1, keepdims=True)
    xn = (x * lax.rsqrt(ms + EPS) * g_ref[...]).astype(BF16)
    xn_ref[...] = xn
    sm_ref[...] = _dot_nt(xn, w_ref[...].astype(BF16))


def _norm_small(x, g, w_small_t, tm):
    m, d = x.shape
    n = w_small_t.shape[0]
    return pl.pallas_call(
        _norm_small_kernel,
        grid=(m // tm,),
        in_specs=[pl.BlockSpec((tm, d), lambda i: (i, 0)),
                  pl.BlockSpec((1, d), lambda i: (0, 0)),
                  pl.BlockSpec((n, d), lambda i: (0, 0))],
        out_specs=[pl.BlockSpec((tm, d), lambda i: (i, 0)),
                   pl.BlockSpec((tm, n), lambda i: (i, 0))],
        out_shape=[jax.ShapeDtypeStruct((m, d), BF16), jax.ShapeDtypeStruct((m, n), F32)],
        compiler_params=_params(("parallel",)),
        name="norm_small_proj",
    )(x, g, w_small_t)


IN_TN = 1024
IN_ALIGNED_TILES = O_KI // IN_TN
IN_SHIFT = O_Z - M_Z
IN_NEXT = 128
IN_ROWS = 256


def _in_proj_kernel(xn_ref, wa_ref, wn_ref, o_ref, w_scr):
    j = pl.program_id(0)
    i = pl.program_id(1)

    @pl.when((i == 0) & (j < IN_ALIGNED_TILES))
    def _():
        for r0 in range(0, IN_TN, IN_ROWS):
            w_scr[r0:r0 + IN_ROWS, :] = wa_ref[r0:r0 + IN_ROWS, :].astype(BF16)

    @pl.when((i == 0) & (j >= IN_ALIGNED_TILES))
    def _():
        for r0 in range(0, IN_TN - IN_SHIFT, IN_ROWS):
            r1 = min(r0 + IN_ROWS, IN_TN - IN_SHIFT)
            w_scr[r0:r1, :] = wa_ref[r0 + IN_SHIFT:r1 + IN_SHIFT, :].astype(BF16)
        w_scr[IN_TN - IN_SHIFT:IN_TN, :] = wn_ref[0:IN_SHIFT, :].astype(BF16)

    o_ref[...] = _dot_nt(xn_ref[...], w_scr[...]).astype(o_ref.dtype)


def _in_proj(xn, w_t, tm):
    m, d = xn.shape
    assert O_KI % IN_TN == 0 and M_END % IN_TN == 0 and IN_TN % IN_NEXT == 0
    assert 0 < IN_SHIFT <= IN_NEXT and IN_SHIFT % 16 == 0
    return pl.pallas_call(
        _in_proj_kernel,
        grid=(M_END // IN_TN, m // tm),
        in_specs=[pl.BlockSpec((tm, d), lambda j, i: (i, 0)),
                  pl.BlockSpec((IN_TN, d), lambda j, i: (j, 0)),
                  pl.BlockSpec((IN_NEXT, d), lambda j, i: ((j + 1) * (IN_TN // IN_NEXT), 0))],
        out_specs=pl.BlockSpec((tm, IN_TN), lambda j, i: (i, j)),
        out_shape=jax.ShapeDtypeStruct((m, M_END), BF16),
        scratch_shapes=[pltpu.VMEM((IN_TN, d), BF16)],
        compiler_params=_params(("arbitrary", "arbitrary")),
        name="in_proj",
    )(xn, w_t, w_t)


LOG2E = math.log2(math.e)
VT_ROWS = HEAD_DIM + 16


def _qk_prep_kernel(q_ref, k_ref, v_ref, qi_ref, sm_ref, qg_ref, kg_ref,
                    qt_ref, kn_ref, vt_ref, qit_ref, wt_ref, ki_ref):
    qg = qg_ref[...]
    for h in range(ATT_HEADS):
        x = q_ref[:, h * HEAD_DIM:(h + 1) * HEAD_DIM].astype(F32)
        ms = jnp.mean(x * x, axis=-1, keepdims=True)
        y = x * lax.rsqrt(ms + EPS) * qg * (HEAD_DIM ** -0.5 * LOG2E)
        qt_ref[0, h] = y.T.astype(BF16)
    kg = kg_ref[...]
    for h in range(ATT_KV_HEADS):
        x = k_ref[:, h * HEAD_DIM:(h + 1) * HEAD_DIM].astype(F32)
        ms = jnp.mean(x * x, axis=-1, keepdims=True)
        kn_ref[:, h * HEAD_DIM:(h + 1) * HEAD_DIM] = (x * lax.rsqrt(ms + EPS) * kg).astype(BF16)
        vt_ref[0, h, 0, 0:HEAD_DIM, :] = v_ref[:, h * HEAD_DIM:(h + 1) * HEAD_DIM].astype(F32).T.astype(BF16)
        vt_ref[0, h, 0, HEAD_DIM:VT_ROWS, :] = jnp.ones((VT_ROWS - HEAD_DIM, v_ref.shape[0]), BF16)
    for p in range(IDX_Q // LANES):
        qit_ref[0, p * LANES:(p + 1) * LANES, :] = qi_ref[:, p * LANES:(p + 1) * LANES].astype(F32).T.astype(BF16)
    sm_t = sm_ref[:, 0:LANES].T
    wt_ref[0] = sm_t[S_WI:S_WI + IDX_HEADS, :] * (IDX_HEADS ** -0.5 * IDX_DIM ** -0.5)
    ki_ref[...] = sm_ref[:, S_KI:S_KI + IDX_DIM].astype(BF16)


def _qk_prep(main, small, qg, kg, batch, seq, tq):
    t = main.shape[0]
    nq = seq // tq
    return pl.pallas_call(
        _qk_prep_kernel,
        grid=(batch, nq),
        in_specs=[pl.BlockSpec((tq, ATT_Q), lambda b, i: (b * nq + i, M_Q // ATT_Q)),
                  pl.BlockSpec((tq, ATT_KV), lambda b, i: (b * nq + i, M_K // ATT_KV)),
                  pl.BlockSpec((tq, ATT_KV), lambda b, i: (b * nq + i, M_V // ATT_KV)),
                  pl.BlockSpec((tq, IDX_Q), lambda b, i: (b * nq + i, M_QI // IDX_Q)),
                  pl.BlockSpec((tq, S_END), lambda b, i: (b * nq + i, 0)),
                  pl.BlockSpec((1, HEAD_DIM), lambda b, i: (0, 0)),
                  pl.BlockSpec((1, HEAD_DIM), lambda b, i: (0, 0))],
        out_specs=[pl.BlockSpec((1, ATT_HEADS, HEAD_DIM, tq), lambda b, i: (b * nq + i, 0, 0, 0)),
                   pl.BlockSpec((tq, ATT_KV), lambda b, i: (b * nq + i, 0)),
                   pl.BlockSpec((1, ATT_KV_HEADS, 1, VT_ROWS, tq), lambda b, i: (b, 0, i, 0, 0)),
                   pl.BlockSpec((1, IDX_Q, tq), lambda b, i: (b * nq + i, 0, 0)),
                   pl.BlockSpec((1, IDX_HEADS, tq), lambda b, i: (b * nq + i, 0, 0)),
                   pl.BlockSpec((tq, IDX_DIM), lambda b, i: (b * nq + i, 0))],
        out_shape=[jax.ShapeDtypeStruct((t // tq, ATT_HEADS, HEAD_DIM, tq), BF16),
                   jax.ShapeDtypeStruct((t, ATT_KV), BF16),
                   jax.ShapeDtypeStruct((batch, ATT_KV_HEADS, nq, VT_ROWS, tq), BF16),
                   jax.ShapeDtypeStruct((t // tq, IDX_Q, tq), BF16),
                   jax.ShapeDtypeStruct((t // tq, IDX_HEADS, tq), F32),
                   jax.ShapeDtypeStruct((t, IDX_DIM), BF16)],
        compiler_params=_params(("parallel", "parallel")),
        name="qk_prep",
    )(main, main, main, main, small, qg, kg)


DSA_TQ = 256
DSA_TK = 256
CNT_WAYS = 4


def _t5_bucket_np(dist):
    n = np.maximum(dist, 0)
    max_exact = N_BUCKETS // 2
    nf = np.maximum(n, 1).astype(np.float32)
    ratio = (np.log(nf / np.float32(max_exact)) / np.float32(math.log(MAX_DISTANCE / max_exact))
             * np.float32(N_BUCKETS - max_exact))
    large = max_exact + ratio.astype(np.int32)
    large = np.minimum(large, N_BUCKETS - 1)
    return np.where(n < max_exact, n, large).astype(np.int32)


def _bias_bucket_tiles(tq, tk):
    r = np.arange(tq)[None, :]
    c = np.arange(tk)[:, None]
    d0 = _t5_bucket_np(r - c)
    d1 = _t5_bucket_np(tk + r - c)
    assert np.all(_t5_bucket_np(np.arange(tk + 1, 8 * tk)) == N_BUCKETS - 1)
    return np.stack([d0, d1]).astype(np.int32)


def _dsa_kernel(relb_ref, bidx_ref, qt_ref, qit_ref, wt_ref, k_ref, vt_ref, ki_ref, o_ref,
                key_ref, plane_ref, madd_ref, bias_ref, s_buf, p_buf, st_ref, acc_ref, *, tq, tk, nkc, topk):
    b = pl.program_id(0)
    i = pl.program_id(1)
    neg_slot = nkc

    @pl.when((b == 0) & (i == 0))
    def _init():
        madd_ref[neg_slot] = jnp.full((tk, tq), NEG_INF, F32)
        for t in range(2):
            bt = bidx_ref[t]

            def head_body(h, carry):
                far = relb_ref[N_BUCKETS - 1, h]

                def bucket_body(bk, acc):
                    return jnp.where(bt == bk, (relb_ref[bk, h] - far) * LOG2E, acc)

                bias_ref[t, h] = lax.fori_loop(0, N_BUCKETS, bucket_body, jnp.zeros((tk, tq), F32))
                return carry

            lax.fori_loop(0, ATT_HEADS, head_body, 0)

    def _select():
        nj = i + 1
        qpos = i * tq + lax.broadcasted_iota(I32, (tk, tq), 1)

        def score_chunk(j, carry):
            kc = ki_ref[pl.ds(pl.multiple_of(j * tk, tk), tk), :]
            acc = jnp.zeros((tk, tq), F32)
            for h in range(IDX_HEADS):
                z = _dot(kc, qit_ref[0, h * IDX_DIM:(h + 1) * IDX_DIM, :])
                acc = acc + jnp.maximum(z, 0.0) * wt_ref[0, h:h + 1, :]
            kpos = j * tk + lax.broadcasted_iota(I32, (tk, tq), 0)
            acc = jnp.where(kpos <= qpos, acc, NEG_INF)
            bits = pltpu.bitcast(acc, I32)
            bits = jnp.where(bits == INT_MIN, 0, bits)
            key = bits ^ ((bits >> 31) & 0x7FFFFFFF)
            key_ref[j] = key
            plane_ref[0, j] = pltpu.bitcast(bits & jnp.int32(-65536), F32).astype(BF16)
            plane_ref[1, j] = ((key >> 8) & 0xFF).astype(F32).astype(BF16)
            plane_ref[2, j] = (key & 0xFF).astype(F32).astype(BF16)
            return carry

        lax.fori_loop(0, nj, score_chunk, 0)

        kf = float(topk)
        one_b = jnp.ones((tk, tq), BF16)
        zero_b = jnp.zeros((tk, tq), BF16)
        grp = 16 * CNT_WAYS

        def count(plane, cand_b, strict):
            def body(j, cnt):
                pv = plane_ref[plane, j]
                hit = jnp.where(pv > cand_b if strict else pv >= cand_b, one_b, zero_b)
                for r0 in range(0, tk, grp):
                    cnt = cnt + hit[r0:r0 + grp]
                return cnt

            cnt = lax.fori_loop(0, nj, body, jnp.zeros((grp, tq), BF16))
            return jnp.sum(cnt.astype(F32), axis=0, keepdims=True)

        def keep_ties(src, dst, tie_b):
            def body(j, carry):
                plane_ref[dst, j] = jnp.where(plane_ref[src, j] == tie_b, plane_ref[dst, j], -one_b)
                return carry

            lax.fori_loop(0, nj, body, 0)

        def top_digit_float(p16):
            fb = (p16 & 0xFFFF) ^ jnp.where(p16 < 0, 0x7FFF, 0)
            return pltpu.bitcast(lax.shift_left(fb, 16), F32).astype(BF16)

        def top_iter(it, prefix):
            cand = prefix + lax.shift_left(jnp.int32(1), 15 - it)
            return jnp.where(count(0, top_digit_float(cand), False) >= kf, cand, prefix)

        d_top = lax.fori_loop(0, 16, top_iter, jnp.full((1, tq), -32768, I32))
        t_top = top_digit_float(d_top)
        above = count(0, t_top, True)

        def byte_digit(plane, above_n):
            def it_body(it, prefix):
                cand = prefix + lax.shift_left(jnp.int32(1), 7 - it)
                tot = above_n + count(plane, cand.astype(F32).astype(BF16), False)
                return jnp.where(tot >= kf, cand, prefix)

            return lax.fori_loop(0, 8, it_body, jnp.zeros((1, tq), I32))

        keep_ties(0, 1, t_top)
        d_mid = byte_digit(1, above)
        t_mid = d_mid.astype(F32).astype(BF16)
        above = above + count(1, t_mid, True)
        keep_ties(1, 2, t_mid)
        d_low = byte_digit(2, above)
        thr = lax.shift_left(d_top, 16) | lax.shift_left(d_mid, 8) | d_low
        thr = jnp.maximum(thr, KEY_NEG_INF + 1)

        def madd_chunk(j, carry):
            madd_ref[j] = jnp.where(key_ref[j] >= thr, 0.0, NEG_INF)
            return carry

        lax.fori_loop(0, nj, madd_chunk, 0)

    _select()

    ng = ATT_KV_HEADS
    c0 = jnp.maximum(i - 1, 0)
    c1 = jnp.minimum(c0 + 1, nkc - 1)
    n_far = (c0 + 1) // 2

    def scores_to(slot, g, ca, madd_c, bias_c):
        rows = pl.ds(pl.multiple_of(ca * tk, tk), 2 * tk)
        qt = jnp.concatenate([qt_ref[0, g * REP + r] for r in range(REP)], axis=1)
        add = jnp.concatenate([madd_c if bias_c is None else madd_c + bias_c[r] for r in range(REP)], axis=1)
        s = _dot(k_ref[rows, g * HEAD_DIM:(g + 1) * HEAD_DIM], qt) + add
        s_buf[slot] = s
        st_ref[ng + 2 + slot:ng + 3 + slot, :] = jnp.max(s, axis=0, keepdims=True)

    def softmax_to(slot, g):
        m = st_ref[g:g + 1, :]
        m_new = jnp.maximum(m, st_ref[ng + 2 + slot:ng + 3 + slot, :])
        m_safe = jnp.where(m_new == NEG_INF, 0.0, m_new)
        alpha = jnp.exp2(m - m_safe)
        p_buf[slot] = jnp.exp2((s_buf[slot] - m_safe).astype(BF16))
        st_ref[g:g + 1, :] = m_new
        st_ref[ng + slot:ng + 1 + slot, :] = alpha

    def values_from(slot, g, ca, cb):
        vt_c = jnp.concatenate([vt_ref[0, g, ca], vt_ref[0, g, cb]], axis=1)
        acc_ref[g] = st_ref[ng + slot:ng + 1 + slot, :] * acc_ref[g] + _dot(vt_c, p_buf[slot])

    def near_scores_to(slot, g):
        first = i == 0
        t_a = jnp.where(first, 0, 1)
        idx_b = jnp.where(first, neg_slot, c0 + 1)
        madd_near = jnp.concatenate([madd_ref[c0], madd_ref[idx_b]], axis=0)
        bias_near = [jnp.concatenate([bias_ref[t_a, g * REP + r], bias_ref[0, g * REP + r]], axis=0)
                     for r in range(REP)]
        scores_to(slot, g, c0, madd_near, bias_near)

    def far_scores_to(slot, g, k):
        ca = jnp.minimum(2 * (k - 1), nkc - 2)
        cb = jnp.where(ca + 1 >= c0, neg_slot, ca + 1)
        scores_to(slot, g, ca, jnp.concatenate([madd_ref[ca], madd_ref[cb]], axis=0), None)

    st_ref[0:ng, :] = jnp.full((ng, REP * tq), NEG_INF, F32)
    acc_ref[...] = jnp.zeros_like(acc_ref)

    near_scores_to(0, 0)
    for g in range(ng):
        if g + 1 < ng:
            near_scores_to((g + 1) % 2, g + 1)
        else:
            far_scores_to(0, 0, 1)
        softmax_to(g % 2, g)
        values_from(g % 2, g, c0, c1)

    def far_step(k, carry):
        for g in range(ng):
            if g + 1 < ng:
                far_scores_to((g + 1) % 2, g + 1, k)
            else:
                far_scores_to(0, 0, k + 1)
            softmax_to(g % 2, g)
            values_from(g % 2, g, 2 * (k - 1), 2 * (k - 1) + 1)
        return carry

    lax.fori_loop(1, n_far + 1, far_step, 0)

    for g in range(ng):
        out = acc_ref[g, 0:HEAD_DIM, :] / acc_ref[g, HEAD_DIM:HEAD_DIM + 1, :]
        for r in range(REP):
            h = g * REP + r
            o_ref[:, h * HEAD_DIM:(h + 1) * HEAD_DIM] = out[:, r * tq:(r + 1) * tq].T.astype(o_ref.dtype)


def _dsa(qt, kn, vt, qit, wt, ki, rel_bias, batch, seq):
    tq, tk = DSA_TQ, DSA_TK
    nq = seq // tq
    nkc = seq // tk
    topk = min(TOPK_MAX, seq // 4)
    bidx = jnp.asarray(_bias_bucket_tiles(tq, tk))
    kern = functools.partial(_dsa_kernel, tq=tq, tk=tk, nkc=nkc, topk=topk)
    once = pl.Buffered(1)
    return pl.pallas_call(
        kern,
        grid=(batch, nq),
        in_specs=[pl.BlockSpec(memory_space=pltpu.SMEM),
                  pl.BlockSpec((2, tk, tq), lambda b, i: (0, 0, 0), pipeline_mode=once),
                  pl.BlockSpec((1, ATT_HEADS, HEAD_DIM, tq), lambda b, i: (b * nq + i, 0, 0, 0)),
                  pl.BlockSpec((1, IDX_Q, tq), lambda b, i: (b * nq + i, 0, 0)),
                  pl.BlockSpec((1, IDX_HEADS, tq), lambda b, i: (b * nq + i, 0, 0)),
                  pl.BlockSpec((seq, ATT_KV), lambda b, i: (b, 0), pipeline_mode=once),
                  pl.BlockSpec((1, ATT_KV_HEADS, nkc, VT_ROWS, tk), lambda b, i: (b, 0, 0, 0, 0),
                               pipeline_mode=once),
                  pl.BlockSpec((seq, IDX_DIM), lambda b, i: (b, 0), pipeline_mode=once)],
        out_specs=pl.BlockSpec((tq, ATT_Q), lambda b, i: (b * nq + i, 0)),
        out_shape=jax.ShapeDtypeStruct((batch * seq, ATT_Q), BF16),
        scratch_shapes=[pltpu.VMEM((nkc, tk, tq), I32),
                        pltpu.VMEM((3, nkc, tk, tq), BF16),
                        pltpu.VMEM((nkc + 1, tk, tq), F32),
                        pltpu.VMEM((2, ATT_HEADS, tk, tq), F32),
                        pltpu.VMEM((2, 2 * tk, REP * tq), F32),
                        pltpu.VMEM((2, 2 * tk, REP * tq), BF16),
                        pltpu.VMEM((ATT_KV_HEADS + 4, REP * tq), F32),
                        pltpu.VMEM((ATT_KV_HEADS, VT_ROWS, REP * tq), F32)],
        compiler_params=_params(("arbitrary", "arbitrary")),
        name="dsa_attention",
    )(rel_bias, bidx, qt, qit, wt, kn, vt, ki)


E_ROWS = 3 * CHUNK + 16
CONV_HALO = 16


def _conv_shift_matrix():
    ext = CONV_HALO + CHUNK
    s = np.zeros((CHUNK, CONV_WIDTH * ext), np.float32)
    for k in range(CONV_WIDTH):
        t = np.arange(CHUNK)
        s[t, k * ext + CONV_HALO + t - (CONV_WIDTH - 1) + k] = 1.0
    return s


def _ssd_kernel(xbc_ref, halo_ref, z_ref, sm_ref, shift_ref, cw_ref, cb_ref, dtb_ref, alog_ref, dsk_ref, ng_ref,
                o_ref, xs_ref, bm_ref, cm_ref, state_ref, ypre_ref, acg_ref, actg_ref):
    c = pl.program_id(1)
    L = CHUNK

    @pl.when(c == 0)
    def _():
        state_ref[...] = jnp.zeros_like(state_ref)

    halo_on = (c > 0).astype(BF16)
    shift = shift_ref[...]
    cblk = GROUP_W
    for cbi in range(CONV_DIM // cblk):
        cols = slice(cbi * cblk, (cbi + 1) * cblk)
        ext = jnp.concatenate([halo_ref[:, cols] * halo_on, xbc_ref[:, cols]], axis=0)
        wtap = cw_ref[:, cols].astype(BF16)
        prods = jnp.concatenate([ext * wtap[kk:kk + 1, :] for kk in range(CONV_WIDTH)], axis=0)
        y = _silu(cb_ref[:, cols] + _dot(shift, prods))
        if cbi < SSM_GROUPS:
            xs_ref[cbi] = y
        else:
            per = cblk // SSM_STATE
            for u in range(per):
                gi = (cbi - SSM_GROUPS) * per + u
                piece = y[:, u * SSM_STATE:(u + 1) * SSM_STATE]
                if gi < SSM_GROUPS:
                    bm_ref[gi] = piece
                else:
                    cm_ref[gi - SSM_GROUPS] = piece

    dt_in = sm_ref[:, S_DT:S_DT + SSM_HEADS] + dtb_ref[...]
    dt_act = jnp.maximum(dt_in, 0.0) + jnp.log1p(jnp.exp(-jnp.abs(dt_in)))
    a = dt_act * (-jnp.exp(alog_ref[...]))
    ri = lax.broadcasted_iota(I32, (L, L), 0)
    ci = lax.broadcasted_iota(I32, (L, L), 1)
    tril = ri >= ci
    tri_b = jnp.where(tril, 1.0, 0.0).astype(BF16)
    a_cum = sum(_dot(tri_b, p) for p in _split3(a))
    eye_b = jnp.where(lax.broadcasted_iota(I32, (SSM_HEADS, SSM_HEADS), 0)
                      == lax.broadcasted_iota(I32, (SSM_HEADS, SSM_HEADS), 1), 1.0, 0.0).astype(BF16)
    a_cum_t = sum(_dot_nt(eye_b, p) for p in _split3(a_cum))
    a_last = a_cum[L - 1:L, :]
    for gi in range(SSM_GROUPS):
        acg_ref[gi] = a_cum[:, gi * HEADS_PER_GROUP:(gi + 1) * HEADS_PER_GROUP]
        actg_ref[gi] = a_cum_t[gi * HEADS_PER_GROUP:(gi + 1) * HEADS_PER_GROUP, :]
    cd3 = _split3(jnp.exp(a_last))
    ds3 = _split3(dsk_ref[...])
    extras = jnp.concatenate([p.astype(F32) for p in cd3 + ds3]
                             + [jnp.zeros((E_ROWS - 3 * L - 6, SSM_HEADS), F32)], axis=0)
    e_mat = jnp.concatenate([dt_act, jnp.exp(a_cum), jnp.exp(a_last - a_cum), extras], axis=0).astype(BF16)

    lane = lax.broadcasted_iota(I32, (L, LANES), 1)
    lo_mask = lane < SSM_HEAD_DIM

    def group_body(gi, carry):
        xs = xs_ref[gi]
        bg = bm_ref[gi]
        cg_b = cm_ref[gi].astype(BF16)
        hsel = (lax.broadcasted_iota(I32, (SSM_HEADS, GROUP_W), 0)
                == gi * HEADS_PER_GROUP + lax.broadcasted_iota(I32, (SSM_HEADS, GROUP_W), 1) // SSM_HEAD_DIM)
        ex = _dot(e_mat, jnp.where(hsel, 1.0, 0.0).astype(BF16))
        dt_rep = ex[0:L]
        expa_rep = ex[L:2 * L]
        dte_rep = ex[2 * L:3 * L]
        cd_rep = ex[3 * L:3 * L + 1] + ex[3 * L + 1:3 * L + 2] + ex[3 * L + 2:3 * L + 3]
        dsk_rep = ex[3 * L + 3:3 * L + 4] + ex[3 * L + 4:3 * L + 5] + ex[3 * L + 5:3 * L + 6]

        xd = xs * dt_rep
        xd_b = xd.astype(BF16)
        cb = _dot_nt(cg_b, bg.astype(BF16))
        acg = acg_ref[gi]
        actg = actg_ref[gi]
        pairs = []
        for pj in range(HEADS_PER_GROUP // 2):
            gmat = []
            for e in (2 * pj, 2 * pj + 1):
                seg = acg[:, e:e + 1] - actg[e:e + 1, :]
                dec = jnp.exp(jnp.where(tril, seg, NEG_INF))
                gmat.append((cb * dec).astype(BF16))
            xp = xd_b[:, pj * LANES:(pj + 1) * LANES]
            zero = jnp.zeros_like(xp)
            pairs.append(_dot(gmat[0], jnp.where(lo_mask, xp, zero))
                         + _dot(gmat[1], jnp.where(lo_mask, zero, xp)))
        y_diag = jnp.concatenate(pairs, axis=1)

        st = state_ref[gi]
        y_off = _dot(cg_b, st.astype(BF16)) * expa_rep
        xdd = (xd * dte_rep).astype(BF16)
        state_ref[gi] = st * cd_rep + _dot(bg.T.astype(BF16), xdd)
        ypre_ref[gi] = y_diag + y_off + dsk_rep * xs
        return carry

    lax.fori_loop(0, SSM_GROUPS, group_body, 0, unroll=8)

    for gi in range(SSM_GROUPS):
        cols = slice(gi * GROUP_W, (gi + 1) * GROUP_W)
        zz = z_ref[:, cols].astype(F32)
        y = ypre_ref[gi] * _silu(zz)
        ms = jnp.mean(y * y, axis=-1, keepdims=True)
        o_ref[:, cols] = (y * lax.rsqrt(ms + EPS) * ng_ref[:, cols]).astype(o_ref.dtype)


def _ssd(main, small, conv_w, conv_b, dt_bias, a_log, d_skip, norm_g, batch, seq):
    nc = seq // CHUNK
    hb = CHUNK // CONV_HALO
    row = lambda b, c: b * nc + c
    full = lambda shape: pl.BlockSpec(shape, lambda b, c: (0,) * len(shape))
    shift = jnp.asarray(_conv_shift_matrix(), BF16)
    return pl.pallas_call(
        _ssd_kernel,
        grid=(batch, nc),
        in_specs=[pl.BlockSpec((CHUNK, CONV_DIM), lambda b, c: (row(b, c), M_XBC // CONV_DIM)),
                  pl.BlockSpec((CONV_HALO, CONV_DIM),
                               lambda b, c: (jnp.maximum(row(b, c) * hb - 1, 0), M_XBC // CONV_DIM)),
                  pl.BlockSpec((CHUNK, SSM_INNER), lambda b, c: (row(b, c), M_Z // SSM_INNER)),
                  pl.BlockSpec((CHUNK, S_END), lambda b, c: (row(b, c), 0)),
                  full((CHUNK, CONV_WIDTH * (CONV_HALO + CHUNK))),
                  full((CONV_WIDTH, CONV_DIM)), full((1, CONV_DIM)), full((1, SSM_HEADS)),
                  full((1, SSM_HEADS)), full((1, SSM_HEADS)), full((1, SSM_INNER))],
        out_specs=pl.BlockSpec((CHUNK, SSM_INNER), lambda b, c: (row(b, c), 0)),
        out_shape=jax.ShapeDtypeStruct((batch * seq, SSM_INNER), BF16),
        scratch_shapes=[pltpu.VMEM((SSM_GROUPS, CHUNK, GROUP_W), F32),
                        pltpu.VMEM((SSM_GROUPS, CHUNK, SSM_STATE), F32),
                        pltpu.VMEM((SSM_GROUPS, CHUNK, SSM_STATE), F32),
                        pltpu.VMEM((SSM_GROUPS, SSM_STATE, GROUP_W), F32),
                        pltpu.VMEM((SSM_GROUPS, CHUNK, GROUP_W), F32),
                        pltpu.VMEM((SSM_GROUPS, CHUNK, HEADS_PER_GROUP), F32),
                        pltpu.VMEM((SSM_GROUPS, HEADS_PER_GROUP, CHUNK), F32)],
        compiler_params=_params(("arbitrary", "arbitrary")),
        name="ssd_scan",
    )(main, main, main, small, shift, conv_w, conv_b, dt_bias, a_log, d_skip, norm_g)


def _merge_kernel(att_ref, y_ref, ga_ref, gs_ref, wa_ref, ws_ref, o_ref):
    pa = _dot(att_ref[...], wa_ref[...])
    ps = _dot(y_ref[...], ws_ref[...])
    o_ref[...] = (_sigmoid(ga_ref[...].astype(F32)) * pa + _sigmoid(gs_ref[...].astype(F32)) * ps).astype(o_ref.dtype)


def _merge(att, y, main, wa, ws, tm, tn):
    t = att.shape[0]
    return pl.pallas_call(
        _merge_kernel,
        grid=(t // tm, D_MODEL // tn),
        in_specs=[pl.BlockSpec((tm, ATT_Q), lambda i, j: (i, 0)),
                  pl.BlockSpec((tm, SSM_INNER), lambda i, j: (i, 0)),
                  pl.BlockSpec((tm, tn), lambda i, j: (i, M_GA // tn + j)),
                  pl.BlockSpec((tm, tn), lambda i, j: (i, M_GS // tn + j)),
                  pl.BlockSpec((ATT_Q, tn), lambda i, j: (0, j)),
                  pl.BlockSpec((SSM_INNER, tn), lambda i, j: (0, j))],
        out_specs=pl.BlockSpec((tm, tn), lambda i, j: (i, j)),
        out_shape=jax.ShapeDtypeStruct((t, D_MODEL), BF16),
        compiler_params=_params(("parallel", "arbitrary")),
        name="gated_merge",
    )(att, y, main, main, wa, ws)


def _out_proj_kernel(m_ref, w_ref, x_ref, g_ref, x1_ref, h2_ref):
    x1 = x_ref[...] + _dot(m_ref[...], w_ref[...])
    x1_ref[...] = x1
    ms = jnp.mean(x1 * x1, axis=-1, keepdims=True)
    h2_ref[...] = (x1 * lax.rsqrt(ms + EPS) * g_ref[...]).astype(BF16)


def _out_proj(merged, w, x, g, tm):
    t = x.shape[0]
    return pl.pallas_call(
        _out_proj_kernel,
        grid=(t // tm,),
        in_specs=[pl.BlockSpec((tm, D_MODEL), lambda i: (i, 0)),
                  pl.BlockSpec((D_MODEL, D_MODEL), lambda i: (0, 0)),
                  pl.BlockSpec((tm, D_MODEL), lambda i: (i, 0)),
                  pl.BlockSpec((1, D_MODEL), lambda i: (0, 0))],
        out_specs=[pl.BlockSpec((tm, D_MODEL), lambda i: (i, 0)),
                   pl.BlockSpec((tm, D_MODEL), lambda i: (i, 0))],
        out_shape=[jax.ShapeDtypeStruct((t, D_MODEL), F32),
                   jax.ShapeDtypeStruct((t, D_MODEL), BF16)],
        compiler_params=_params(("parallel",)),
        name="out_proj_norm",
    )(merged, w, x, g)


def _mlp_kernel(h_ref, x1_ref, wu_ref, wd_ref, o_ref):
    @pl.when(pl.program_id(1) == 0)
    def _():
        o_ref[...] = x1_ref[...]

    u = _dot(h_ref[...], wu_ref[...])
    u = jnp.square(jnp.maximum(u, 0.0)).astype(BF16)
    o_ref[...] += _dot(u, wd_ref[...])


def _mlp(h2, x1, wu, wd, tm, th):
    t = h2.shape[0]
    return pl.pallas_call(
        _mlp_kernel,
        grid=(t // tm, MLP_HIDDEN // th),
        in_specs=[pl.BlockSpec((tm, D_MODEL), lambda i, j: (i, 0)),
                  pl.BlockSpec((tm, D_MODEL), lambda i, j: (i, 0)),
                  pl.BlockSpec((D_MODEL, th), lambda i, j: (0, j)),
                  pl.BlockSpec((th, D_MODEL), lambda i, j: (j, 0))],
        out_specs=pl.BlockSpec((tm, D_MODEL), lambda i, j: (i, 0)),
        out_shape=jax.ShapeDtypeStruct((t, D_MODEL), F32),
        compiler_params=_params(("parallel", "arbitrary")),
        name="relu2_mlp",
    )(h2, x1, wu, wd)


def _pack_w_small(w_t):
    assert (O_GA, O_KI, O_Z, O_DT) == (M_GA, M_Z, O_WI + IDX_HEADS, O_Z + M_END - M_Z)
    return jnp.concatenate([w_t[O_KI:O_Z], w_t[O_DT:O_DT + SSM_HEADS],
                            jnp.zeros((S_END - S_DT - SSM_HEADS, w_t.shape[1]), w_t.dtype)], axis=0)


def _block(x2, batch, seq, norm1_g, w_in, conv_w, conv_b, dt_bias, a_log, d_skip, ssm_norm_g, q_norm_g,
           k_norm_g, rel_bias, w_att_branch, w_ssm_branch, w_out, norm2_g, w_up, w_down):
    row = lambda v: v.reshape(1, -1)
    w_t = w_in.T
    xn, small = _norm_small(x2, row(norm1_g), _pack_w_small(w_t), tm=min(1024, x2.shape[0]))
    main = _in_proj(xn, w_t, tm=min(1024, x2.shape[0]))
    qt, kn, vt, qit, wt, ki = _qk_prep(main, small, row(q_norm_g), row(k_norm_g), batch, seq, tq=DSA_TQ)
    att = _dsa(qt, kn, vt, qit, wt, ki, rel_bias, batch, seq)
    y = _ssd(main, small, conv_w, row(conv_b), row(dt_bias), row(a_log), row(d_skip), row(ssm_norm_g),
             batch, seq)
    merged = _merge(att, y, main, w_att_branch.astype(BF16), w_ssm_branch.astype(BF16), tm=512, tn=1024)
    x1, h2 = _out_proj(merged, w_out.astype(BF16), x2, row(norm2_g), tm=512)
    return _mlp(h2, x1, w_up.astype(BF16), w_down.astype(BF16), tm=512, th=1024)


def kernel(x, norm1_g, w_in, conv_w, conv_b, dt_bias, a_log, d_skip, ssm_norm_g, q_norm_g, k_norm_g, rel_bias,
           w_att_branch, w_ssm_branch, w_out, norm2_g, w_up, w_down):
    batch, seq, d = x.shape
    x2 = x.reshape(batch * seq, d)
    for l in range(norm1_g.shape[0]):
        x2 = _block(x2, batch, seq, norm1_g[l], w_in[l], conv_w[l], conv_b[l], dt_bias[l], a_log[l], d_skip[l],
                    ssm_norm_g[l], q_norm_g[l], k_norm_g[l], rel_bias, w_att_branch[l], w_ssm_branch[l],
                    w_out[l], norm2_g[l], w_up[l], w_down[l])
    return x2.reshape(batch, seq, d)
```

```python
import functools
import math

import numpy as np
import jax
import jax.numpy as jnp
from jax import lax
from jax.experimental import pallas as pl
from jax.experimental.pallas import tpu as pltpu

F32 = jnp.float32
BF16 = jnp.bfloat16
I32 = jnp.int32

D_MODEL = 2048
ATT_HEADS = 16
ATT_KV_HEADS = 4
HEAD_DIM = 128
REP = ATT_HEADS // ATT_KV_HEADS
IDX_HEADS = 16
IDX_DIM = 64
TOPK_MAX = 256
N_BUCKETS = 32
MAX_DISTANCE = 128
SSM_INNER = 2 * D_MODEL
SSM_HEAD_DIM = 64
SSM_HEADS = SSM_INNER // SSM_HEAD_DIM
SSM_GROUPS = 8
SSM_STATE = 128
CONV_WIDTH = 4
CHUNK = 128
MLP_HIDDEN = 4 * D_MODEL
EPS = 1e-6

ATT_Q = ATT_HEADS * HEAD_DIM
ATT_KV = ATT_KV_HEADS * HEAD_DIM
IDX_Q = IDX_HEADS * IDX_DIM
SSM_BC = SSM_GROUPS * SSM_STATE
CONV_DIM = SSM_INNER + 2 * SSM_BC
SPLITS = (D_MODEL, D_MODEL, ATT_Q, ATT_KV, ATT_KV, IDX_Q, IDX_DIM, IDX_HEADS, SSM_INNER, CONV_DIM, SSM_HEADS)
_OFFS = tuple(int(v) for v in np.cumsum((0,) + SPLITS))
(O_GA, O_GS, O_Q, O_K, O_V, O_QI, O_KI, O_WI, O_Z, O_XBC, O_DT, _O_END) = _OFFS

M_GA = 0
M_GS = M_GA + D_MODEL
M_Q = M_GS + D_MODEL
M_K = M_Q + ATT_Q
M_V = M_K + ATT_KV
M_QI = M_V + ATT_KV
M_Z = M_QI + IDX_Q
M_XBC = M_Z + SSM_INNER
M_END = M_XBC + CONV_DIM
S_KI = 0
S_WI = S_KI + IDX_DIM
S_DT = S_WI + IDX_HEADS
S_END = 256

HEADS_PER_GROUP = SSM_HEADS // SSM_GROUPS
GROUP_W = HEADS_PER_GROUP * SSM_HEAD_DIM

LANES = 128
VMEM_LIMIT = 56 * 1024 * 1024

NEG_INF = float("-inf")
INT_MIN = -(2 ** 31)
KEY_NEG_INF = int(np.int32(np.uint32(0xFF800000) ^ np.uint32(0x7FFFFFFF)))


def _dot(a, b):
    return jnp.dot(a, b, preferred_element_type=F32)


def _dot_nt(a, b):
    return lax.dot_general(a, b, (((1,), (1,)), ((), ())), preferred_element_type=F32)


def _split3(x):
    hi = x.astype(BF16)
    r = x - hi.astype(F32)
    mid = r.astype(BF16)
    lo = (r - mid.astype(F32)).astype(BF16)
    return hi, mid, lo


def _silu(x):
    h = 0.5 * x
    return h + h * jnp.tanh(h)


def _sigmoid(x):
    return 1.0 / (1.0 + jnp.exp(-x))


def _params(sem):
    return pltpu.CompilerParams(dimension_semantics=sem, vmem_limit_bytes=VMEM_LIMIT)


def _norm_small_kernel(x_ref, g_ref, w_ref, xn_ref, sm_ref):
    x = x_ref[...]
    ms = jnp.mean(x * x, axis=-1, keepdims=True)
    xn = (x * lax.rsqrt(ms + EPS) * g_ref[...]).astype(BF16)
    xn_ref[...] = xn
    sm_ref[...] = _dot_nt(xn, w_ref[...].astype(BF16))


def _norm_small(x, g, w_small_t, tm):
    m, d = x.shape
    n = w_small_t.shape[0]
    return pl.pallas_call(
        _norm_small_kernel,
        grid=(m // tm,),
        in_specs=[pl.BlockSpec((tm, d), lambda i: (i, 0)),
                  pl.BlockSpec((1, d), lambda i: (0, 0)),
                  pl.BlockSpec((n, d), lambda i: (0, 0))],
        out_specs=[pl.BlockSpec((tm, d), lambda i: (i, 0)),
                   pl.BlockSpec((tm, n), lambda i: (i, 0))],
        out_shape=[jax.ShapeDtypeStruct((m, d), BF16), jax.ShapeDtypeStruct((m, n), F32)],
        compiler_params=_params(("parallel",)),
        name="norm_small_proj",
    )(x, g, w_small_t)


IN_TN = 1024
IN_ALIGNED_TILES = O_KI // IN_TN
IN_SHIFT = O_Z - M_Z
IN_NEXT = 128
IN_ROWS = 256


def _in_proj_kernel(xn_ref, wa_ref, wn_ref, *rest, n_cast, n_side):
    side_in, o_ref, side_out, w_scr = rest[:n_side], rest[n_side], rest[n_side + 1:2 * n_side + 1], rest[-1]
    j = pl.program_id(0)
    i = pl.program_id(1)

    @pl.when(j * pl.num_programs(1) + i < n_cast)
    def _():
        for src, dst in zip(side_in, side_out):
            dst[...] = src[...].astype(BF16)

    @pl.when((i == 0) & (j < IN_ALIGNED_TILES))
    def _():
        for r0 in range(0, IN_TN, IN_ROWS):
            w_scr[r0:r0 + IN_ROWS, :] = wa_ref[r0:r0 + IN_ROWS, :].astype(BF16)

    @pl.when((i == 0) & (j >= IN_ALIGNED_TILES))
    def _():
        for r0 in range(0, IN_TN - IN_SHIFT, IN_ROWS):
            r1 = min(r0 + IN_ROWS, IN_TN - IN_SHIFT)
            w_scr[r0:r1, :] = wa_ref[r0 + IN_SHIFT:r1 + IN_SHIFT, :].astype(BF16)
        w_scr[IN_TN - IN_SHIFT:IN_TN, :] = wn_ref[0:IN_SHIFT, :].astype(BF16)

    o_ref[...] = _dot_nt(xn_ref[...], w_scr[...]).astype(o_ref.dtype)


def _in_proj(xn, w_t, side_weights, tm):
    m, d = xn.shape
    assert O_KI % IN_TN == 0 and M_END % IN_TN == 0 and IN_TN % IN_NEXT == 0
    assert 0 < IN_SHIFT <= IN_NEXT and IN_SHIFT % 16 == 0
    nj, ni = M_END // IN_TN, m // tm
    n_cast = 1 << ((nj * ni).bit_length() - 1)
    side_specs = []
    for w in side_weights:
        rows = w.shape[0] // n_cast
        assert w.shape[0] % n_cast == 0 and rows % 16 == 0
        side_specs.append(pl.BlockSpec((rows, w.shape[1]), lambda j, i: (jnp.minimum(j * ni + i, n_cast - 1), 0)))
    kern = functools.partial(_in_proj_kernel, n_cast=n_cast, n_side=len(side_weights))
    outs = pl.pallas_call(
        kern,
        grid=(nj, ni),
        in_specs=[pl.BlockSpec((tm, d), lambda j, i: (i, 0)),
                  pl.BlockSpec((IN_TN, d), lambda j, i: (j, 0)),
                  pl.BlockSpec((IN_NEXT, d), lambda j, i: ((j + 1) * (IN_TN // IN_NEXT), 0))] + side_specs,
        out_specs=[pl.BlockSpec((tm, IN_TN), lambda j, i: (i, j))] + side_specs,
        out_shape=[jax.ShapeDtypeStruct((m, M_END), BF16)]
                  + [jax.ShapeDtypeStruct(w.shape, BF16) for w in side_weights],
        scratch_shapes=[pltpu.VMEM((IN_TN, d), BF16)],
        compiler_params=_params(("arbitrary", "arbitrary")),
        name="in_proj",
    )(xn, w_t, w_t, *side_weights)
    return outs[0], outs[1:]


LOG2E = math.log2(math.e)
VT_ROWS = HEAD_DIM + 16


def _qk_prep_kernel(q_ref, k_ref, v_ref, qi_ref, sm_ref, qg_ref, kg_ref,
                    qt_ref, kn_ref, vt_ref, qit_ref, wt_ref, ki_ref):
    qg = qg_ref[...]
    for h in range(ATT_HEADS):
        x = q_ref[:, h * HEAD_DIM:(h + 1) * HEAD_DIM].astype(F32)
        ms = jnp.mean(x * x, axis=-1, keepdims=True)
        y = x * lax.rsqrt(ms + EPS) * qg * (HEAD_DIM ** -0.5 * LOG2E)
        qt_ref[0, h] = y.T.astype(BF16)
    kg = kg_ref[...]
    for h in range(ATT_KV_HEADS):
        x = k_ref[:, h * HEAD_DIM:(h + 1) * HEAD_DIM].astype(F32)
        ms = jnp.mean(x * x, axis=-1, keepdims=True)
        kn_ref[:, h * HEAD_DIM:(h + 1) * HEAD_DIM] = (x * lax.rsqrt(ms + EPS) * kg).astype(BF16)
        vt_ref[0, h, 0, 0:HEAD_DIM, :] = v_ref[:, h * HEAD_DIM:(h + 1) * HEAD_DIM].astype(F32).T.astype(BF16)
        vt_ref[0, h, 0, HEAD_DIM:VT_ROWS, :] = jnp.ones((VT_ROWS - HEAD_DIM, v_ref.shape[0]), BF16)
    for p in range(IDX_Q // LANES):
        qit_ref[0, p * LANES:(p + 1) * LANES, :] = qi_ref[:, p * LANES:(p + 1) * LANES].astype(F32).T.astype(BF16)
    sm_t = sm_ref[:, 0:LANES].T
    wt_ref[0] = sm_t[S_WI:S_WI + IDX_HEADS, :] * (IDX_HEADS ** -0.5 * IDX_DIM ** -0.5)
    ki_ref[...] = sm_ref[:, S_KI:S_KI + IDX_DIM].astype(BF16)


def _qk_prep(main, small, qg, kg, batch, seq, tq):
    t = main.shape[0]
    nq = seq // tq
    return pl.pallas_call(
        _qk_prep_kernel,
        grid=(batch, nq),
        in_specs=[pl.BlockSpec((tq, ATT_Q), lambda b, i: (b * nq + i, M_Q // ATT_Q)),
                  pl.BlockSpec((tq, ATT_KV), lambda b, i: (b * nq + i, M_K // ATT_KV)),
                  pl.BlockSpec((tq, ATT_KV), lambda b, i: (b * nq + i, M_V // ATT_KV)),
                  pl.BlockSpec((tq, IDX_Q), lambda b, i: (b * nq + i, M_QI // IDX_Q)),
                  pl.BlockSpec((tq, S_END), lambda b, i: (b * nq + i, 0)),
                  pl.BlockSpec((1, HEAD_DIM), lambda b, i: (0, 0)),
                  pl.BlockSpec((1, HEAD_DIM), lambda b, i: (0, 0))],
        out_specs=[pl.BlockSpec((1, ATT_HEADS, HEAD_DIM, tq), lambda b, i: (b * nq + i, 0, 0, 0)),
                   pl.BlockSpec((tq, ATT_KV), lambda b, i: (b * nq + i, 0)),
                   pl.BlockSpec((1, ATT_KV_HEADS, 1, VT_ROWS, tq), lambda b, i: (b, 0, i, 0, 0)),
                   pl.BlockSpec((1, IDX_Q, tq), lambda b, i: (b * nq + i, 0, 0)),
                   pl.BlockSpec((1, IDX_HEADS, tq), lambda b, i: (b * nq + i, 0, 0)),
                   pl.BlockSpec((tq, IDX_DIM), lambda b, i: (b * nq + i, 0))],
        out_shape=[jax.ShapeDtypeStruct((t // tq, ATT_HEADS, HEAD_DIM, tq), BF16),
                   jax.ShapeDtypeStruct((t, ATT_KV), BF16),
                   jax.ShapeDtypeStruct((batch, ATT_KV_HEADS, nq, VT_ROWS, tq), BF16),
                   jax.ShapeDtypeStruct((t // tq, IDX_Q, tq), BF16),
                   jax.ShapeDtypeStruct((t // tq, IDX_HEADS, tq), F32),
                   jax.ShapeDtypeStruct((t, IDX_DIM), BF16)],
        compiler_params=_params(("parallel", "parallel")),
        name="qk_prep",
    )(main, main, main, main, small, qg, kg)


DSA_TQ = 256
DSA_TK = 256
CNT_WAYS = 4


def _t5_bucket_np(dist):
    n = np.maximum(dist, 0)
    max_exact = N_BUCKETS // 2
    nf = np.maximum(n, 1).astype(np.float32)
    ratio = (np.log(nf / np.float32(max_exact)) / np.float32(math.log(MAX_DISTANCE / max_exact))
             * np.float32(N_BUCKETS - max_exact))
    large = max_exact + ratio.astype(np.int32)
    large = np.minimum(large, N_BUCKETS - 1)
    return np.where(n < max_exact, n, large).astype(np.int32)


def _bias_bucket_tiles(tq, tk):
    r = np.arange(tq)[None, :]
    c = np.arange(tk)[:, None]
    d0 = _t5_bucket_np(r - c)
    d1 = _t5_bucket_np(tk + r - c)
    assert np.all(_t5_bucket_np(np.arange(tk + 1, 8 * tk)) == N_BUCKETS - 1)
    return np.stack([d0, d1]).astype(np.int32)


def _dsa_kernel(relb_ref, bidx_ref, qt_ref, qit_ref, wt_ref, k_ref, vt_ref, ki_ref, o_ref,
                key_ref, plane_ref, madd_ref, bias_ref, s_buf, p_buf, st_ref, acc_ref, *, tq, tk, nkc, topk):
    b = pl.program_id(0)
    i = pl.program_id(1)
    neg_slot = nkc

    @pl.when((b == 0) & (i == 0))
    def _init():
        madd_ref[neg_slot] = jnp.full((tk, tq), NEG_INF, F32)
        for t in range(2):
            bt = bidx_ref[t]

            def head_body(h, carry):
                far = relb_ref[N_BUCKETS - 1, h]

                def bucket_body(bk, acc):
                    return jnp.where(bt == bk, (relb_ref[bk, h] - far) * LOG2E, acc)

                bias_ref[t, h] = lax.fori_loop(0, N_BUCKETS, bucket_body, jnp.zeros((tk, tq), F32))
                return carry

            lax.fori_loop(0, ATT_HEADS, head_body, 0)

    def _select():
        nj = i + 1
        qpos = i * tq + lax.broadcasted_iota(I32, (tk, tq), 1)

        def score_chunk(j):
            kc = ki_ref[pl.ds(pl.multiple_of(j * tk, tk), tk), :]
            acc = jnp.zeros((tk, tq), F32)
            for h in range(IDX_HEADS):
                z = _dot(kc, qit_ref[0, h * IDX_DIM:(h + 1) * IDX_DIM, :])
                acc = acc + jnp.maximum(z, 0.0) * wt_ref[0, h:h + 1, :]
            kpos = j * tk + lax.broadcasted_iota(I32, (tk, tq), 0)
            acc = jnp.where(kpos <= qpos, acc, NEG_INF)
            bits = pltpu.bitcast(acc, I32)
            bits = jnp.where(bits == INT_MIN, 0, bits)
            key = bits ^ ((bits >> 31) & 0x7FFFFFFF)
            key_ref[j] = key
            plane_ref[0, j] = pltpu.bitcast(bits & jnp.int32(-65536), F32).astype(BF16)
            plane_ref[1, j] = ((key >> 8) & 0xFF).astype(F32).astype(BF16)
            plane_ref[2, j] = (key & 0xFF).astype(F32).astype(BF16)

        def score_quad(u, carry):
            for v in range(4):
                score_chunk(4 * u + v)
            return carry

        lax.fori_loop(0, nj // 4, score_quad, 0)
        done = (nj // 4) * 4

        @pl.when(nj % 4 >= 2)
        def _():
            score_chunk(done)
            score_chunk(done + 1)

        @pl.when(nj % 2 == 1)
        def _():
            score_chunk(nj - 1)

        kf = float(topk)
        one_b = jnp.ones((tk, tq), BF16)
        zero_b = jnp.zeros((tk, tq), BF16)
        grp = 16 * CNT_WAYS

        def count(plane, cand_b, strict):
            def body(j, cnt):
                pv = plane_ref[plane, j]
                hit = jnp.where(pv > cand_b if strict else pv >= cand_b, one_b, zero_b)
                for r0 in range(0, tk, grp):
                    cnt = cnt + hit[r0:r0 + grp]
                return cnt

            cnt = lax.fori_loop(0, nj, body, jnp.zeros((grp, tq), BF16))
            return jnp.sum(cnt.astype(F32), axis=0, keepdims=True)

        def keep_ties(src, dst, tie_b):
            def body(j, carry):
                plane_ref[dst, j] = jnp.where(plane_ref[src, j] == tie_b, plane_ref[dst, j], -one_b)
                return carry

            lax.fori_loop(0, nj, body, 0)

        def top_digit_float(p16):
            fb = (p16 & 0xFFFF) ^ jnp.where(p16 < 0, 0x7FFF, 0)
            return pltpu.bitcast(lax.shift_left(fb, 16), F32).astype(BF16)

        def top_iter(it, prefix):
            cand = prefix + lax.shift_left(jnp.int32(1), 15 - it)
            return jnp.where(count(0, top_digit_float(cand), False) >= kf, cand, prefix)

        d_top = lax.fori_loop(0, 16, top_iter, jnp.full((1, tq), -32768, I32))
        t_top = top_digit_float(d_top)
        above = count(0, t_top, True)

        def byte_digit(plane, above_n):
            def it_body(it, prefix):
                cand = prefix + lax.shift_left(jnp.int32(1), 7 - it)
                tot = above_n + count(plane, cand.astype(F32).astype(BF16), False)
                return jnp.where(tot >= kf, cand, prefix)

            return lax.fori_loop(0, 8, it_body, jnp.zeros((1, tq), I32))

        keep_ties(0, 1, t_top)
        d_mid = byte_digit(1, above)
        t_mid = d_mid.astype(F32).astype(BF16)
        above = above + count(1, t_mid, True)
        keep_ties(1, 2, t_mid)
        d_low = byte_digit(2, above)
        thr = lax.shift_left(d_top, 16) | lax.shift_left(d_mid, 8) | d_low
        thr = jnp.maximum(thr, KEY_NEG_INF + 1)

        def madd_chunk(j, carry):
            madd_ref[j] = jnp.where(key_ref[j] >= thr, 0.0, NEG_INF)
            return carry

        lax.fori_loop(0, nj, madd_chunk, 0)

    _select()

    ng = ATT_KV_HEADS
    c0 = jnp.maximum(i - 1, 0)
    c1 = jnp.minimum(c0 + 1, nkc - 1)
    n_far = (c0 + 1) // 2

    def scores_to(slot, g, ca, madd_c, bias_c):
        rows = pl.ds(pl.multiple_of(ca * tk, tk), 2 * tk)
        qt = jnp.concatenate([qt_ref[0, g * REP + r] for r in range(REP)], axis=1)
        add = jnp.concatenate([madd_c if bias_c is None else madd_c + bias_c[r] for r in range(REP)], axis=1)
        s = _dot(k_ref[rows, g * HEAD_DIM:(g + 1) * HEAD_DIM], qt) + add
        s_buf[slot] = s
        st_ref[ng + 2 + slot:ng + 3 + slot, :] = jnp.max(s, axis=0, keepdims=True)

    def softmax_to(slot, g):
        m = st_ref[g:g + 1, :]
        m_new = jnp.maximum(m, st_ref[ng + 2 + slot:ng + 3 + slot, :])
        m_safe = jnp.where(m_new == NEG_INF, 0.0, m_new)
        alpha = jnp.exp2(m - m_safe)
        p_buf[slot] = jnp.exp2((s_buf[slot] - m_safe).astype(BF16))
        st_ref[g:g + 1, :] = m_new
        st_ref[ng + slot:ng + 1 + slot, :] = alpha

    def values_from(slot, g, ca, cb):
        vt_c = jnp.concatenate([vt_ref[0, g, ca], vt_ref[0, g, cb]], axis=1)
        acc_ref[g] = st_ref[ng + slot:ng + 1 + slot, :] * acc_ref[g] + _dot(vt_c, p_buf[slot])

    def near_scores_to(slot, g):
        first = i == 0
        t_a = jnp.where(first, 0, 1)
        idx_b = jnp.where(first, neg_slot, c0 + 1)
        madd_near = jnp.concatenate([madd_ref[c0], madd_ref[idx_b]], axis=0)
        bias_near = [jnp.concatenate([bias_ref[t_a, g * REP + r], bias_ref[0, g * REP + r]], axis=0)
                     for r in range(REP)]
        scores_to(slot, g, c0, madd_near, bias_near)

    def far_scores_to(slot, g, k):
        ca = jnp.minimum(2 * (k - 1), nkc - 2)
        cb = jnp.where(ca + 1 >= c0, neg_slot, ca + 1)
        scores_to(slot, g, ca, jnp.concatenate([madd_ref[ca], madd_ref[cb]], axis=0), None)

    st_ref[0:ng, :] = jnp.full((ng, REP * tq), NEG_INF, F32)
    acc_ref[...] = jnp.zeros_like(acc_ref)

    near_scores_to(0, 0)
    for g in range(ng):
        if g + 1 < ng:
            near_scores_to((g + 1) % 2, g + 1)
        else:
            far_scores_to(0, 0, 1)
        softmax_to(g % 2, g)
        values_from(g % 2, g, c0, c1)

    def far_step(k, carry):
        for g in range(ng):
            if g + 1 < ng:
                far_scores_to((g + 1) % 2, g + 1, k)
            else:
                far_scores_to(0, 0, k + 1)
            softmax_to(g % 2, g)
            values_from(g % 2, g, 2 * (k - 1), 2 * (k - 1) + 1)
        return carry

    lax.fori_loop(1, n_far + 1, far_step, 0)

    for g in range(ng):
        out = acc_ref[g, 0:HEAD_DIM, :] / acc_ref[g, HEAD_DIM:HEAD_DIM + 1, :]
        for r in range(REP):
            h = g * REP + r
            o_ref[:, h * HEAD_DIM:(h + 1) * HEAD_DIM] = out[:, r * tq:(r + 1) * tq].T.astype(o_ref.dtype)


def _dsa(qt, kn, vt, qit, wt, ki, rel_bias, batch, seq):
    tq, tk = DSA_TQ, DSA_TK
    nq = seq // tq
    nkc = seq // tk
    topk = min(TOPK_MAX, seq // 4)
    bidx = jnp.asarray(_bias_bucket_tiles(tq, tk))
    kern = functools.partial(_dsa_kernel, tq=tq, tk=tk, nkc=nkc, topk=topk)
    once = pl.Buffered(1)
    return pl.pallas_call(
        kern,
        grid=(batch, nq),
        in_specs=[pl.BlockSpec(memory_space=pltpu.SMEM),
                  pl.BlockSpec((2, tk, tq), lambda b, i: (0, 0, 0), pipeline_mode=once),
                  pl.BlockSpec((1, ATT_HEADS, HEAD_DIM, tq), lambda b, i: (b * nq + i, 0, 0, 0)),
                  pl.BlockSpec((1, IDX_Q, tq), lambda b, i: (b * nq + i, 0, 0)),
                  pl.BlockSpec((1, IDX_HEADS, tq), lambda b, i: (b * nq + i, 0, 0)),
                  pl.BlockSpec((seq, ATT_KV), lambda b, i: (b, 0), pipeline_mode=once),
                  pl.BlockSpec((1, ATT_KV_HEADS, nkc, VT_ROWS, tk), lambda b, i: (b, 0, 0, 0, 0),
                               pipeline_mode=once),
                  pl.BlockSpec((seq, IDX_DIM), lambda b, i: (b, 0), pipeline_mode=once)],
        out_specs=pl.BlockSpec((tq, ATT_Q), lambda b, i: (b * nq + i, 0)),
        out_shape=jax.ShapeDtypeStruct((batch * seq, ATT_Q), BF16),
        scratch_shapes=[pltpu.VMEM((nkc, tk, tq), I32),
                        pltpu.VMEM((3, nkc, tk, tq), BF16),
                        pltpu.VMEM((nkc + 1, tk, tq), F32),
                        pltpu.VMEM((2, ATT_HEADS, tk, tq), F32),
                        pltpu.VMEM((2, 2 * tk, REP * tq), F32),
                        pltpu.VMEM((2, 2 * tk, REP * tq), BF16),
                        pltpu.VMEM((ATT_KV_HEADS + 4, REP * tq), F32),
                        pltpu.VMEM((ATT_KV_HEADS, VT_ROWS, REP * tq), F32)],
        compiler_params=_params(("arbitrary", "arbitrary")),
        name="dsa_attention",
    )(rel_bias, bidx, qt, qit, wt, kn, vt, ki)


E_ROWS = 3 * CHUNK + 16
CONV_HALO = 16


def _conv_shift_matrix():
    ext = CONV_HALO + CHUNK
    s = np.zeros((CHUNK, CONV_WIDTH * ext), np.float32)
    for k in range(CONV_WIDTH):
        t = np.arange(CHUNK)
        s[t, k * ext + CONV_HALO + t - (CONV_WIDTH - 1) + k] = 1.0
    return s


def _ssd_kernel(xbc_ref, halo_ref, z_ref, sm_ref, shift_ref, cw_ref, cb_ref, dtb_ref, alog_ref, dsk_ref, ng_ref,
                o_ref, xs_ref, bm_ref, cm_ref, state_ref, ypre_ref, acg_ref, actg_ref):
    c = pl.program_id(1)
    L = CHUNK

    @pl.when(c == 0)
    def _():
        state_ref[...] = jnp.zeros_like(state_ref)

    halo_on = (c > 0).astype(BF16)
    shift = shift_ref[...]
    cblk = GROUP_W
    for cbi in range(CONV_DIM // cblk):
        cols = slice(cbi * cblk, (cbi + 1) * cblk)
        ext = jnp.concatenate([halo_ref[:, cols] * halo_on, xbc_ref[:, cols]], axis=0)
        wtap = cw_ref[:, cols].astype(BF16)
        prods = jnp.concatenate([ext * wtap[kk:kk + 1, :] for kk in range(CONV_WIDTH)], axis=0)
        y = _silu(cb_ref[:, cols] + _dot(shift, prods))
        if cbi < SSM_GROUPS:
            xs_ref[cbi] = y
        else:
            per = cblk // SSM_STATE
            for u in range(per):
                gi = (cbi - SSM_GROUPS) * per + u
                piece = y[:, u * SSM_STATE:(u + 1) * SSM_STATE]
                if gi < SSM_GROUPS:
                    bm_ref[gi] = piece
                else:
                    cm_ref[gi - SSM_GROUPS] = piece

    dt_in = sm_ref[:, S_DT:S_DT + SSM_HEADS] + dtb_ref[...]
    dt_act = jnp.maximum(dt_in, 0.0) + jnp.log1p(jnp.exp(-jnp.abs(dt_in)))
    a = dt_act * (-jnp.exp(alog_ref[...]))
    ri = lax.broadcasted_iota(I32, (L, L), 0)
    ci = lax.broadcasted_iota(I32, (L, L), 1)
    tril = ri >= ci
    tri_b = jnp.where(tril, 1.0, 0.0).astype(BF16)
    a_cum = sum(_dot(tri_b, p) for p in _split3(a))
    eye_b = jnp.where(lax.broadcasted_iota(I32, (SSM_HEADS, SSM_HEADS), 0)
                      == lax.broadcasted_iota(I32, (SSM_HEADS, SSM_HEADS), 1), 1.0, 0.0).astype(BF16)
    a_cum_t = sum(_dot_nt(eye_b, p) for p in _split3(a_cum))
    a_last = a_cum[L - 1:L, :]
    for gi in range(SSM_GROUPS):
        acg_ref[gi] = a_cum[:, gi * HEADS_PER_GROUP:(gi + 1) * HEADS_PER_GROUP]
        actg_ref[gi] = a_cum_t[gi * HEADS_PER_GROUP:(gi + 1) * HEADS_PER_GROUP, :]
    cd3 = _split3(jnp.exp(a_last))
    ds3 = _split3(dsk_ref[...])
    extras = jnp.concatenate([p.astype(F32) for p in cd3 + ds3]
                             + [jnp.zeros((E_ROWS - 3 * L - 6, SSM_HEADS), F32)], axis=0)
    e_mat = jnp.concatenate([dt_act, jnp.exp(a_cum), jnp.exp(a_last - a_cum), extras], axis=0).astype(BF16)

    lane = lax.broadcasted_iota(I32, (L, LANES), 1)
    lo_mask = lane < SSM_HEAD_DIM

    def group_body(gi, carry):
        xs = xs_ref[gi]
        bg = bm_ref[gi]
        cg_b = cm_ref[gi].astype(BF16)
        hsel = (lax.broadcasted_iota(I32, (SSM_HEADS, GROUP_W), 0)
                == gi * HEADS_PER_GROUP + lax.broadcasted_iota(I32, (SSM_HEADS, GROUP_W), 1) // SSM_HEAD_DIM)
        ex = _dot(e_mat, jnp.where(hsel, 1.0, 0.0).astype(BF16))
        dt_rep = ex[0:L]
        expa_rep = ex[L:2 * L]
        dte_rep = ex[2 * L:3 * L]
        cd_rep = ex[3 * L:3 * L + 1] + ex[3 * L + 1:3 * L + 2] + ex[3 * L + 2:3 * L + 3]
        dsk_rep = ex[3 * L + 3:3 * L + 4] + ex[3 * L + 4:3 * L + 5] + ex[3 * L + 5:3 * L + 6]

        xd = xs * dt_rep
        xd_b = xd.astype(BF16)
        cb = _dot_nt(cg_b, bg.astype(BF16))
        acg = acg_ref[gi]
        actg = actg_ref[gi]
        pairs = []
        for pj in range(HEADS_PER_GROUP // 2):
            gmat = []
            for e in (2 * pj, 2 * pj + 1):
                seg = acg[:, e:e + 1] - actg[e:e + 1, :]
                dec = jnp.exp(jnp.where(tril, seg, NEG_INF))
                gmat.append((cb * dec).astype(BF16))
            xp = xd_b[:, pj * LANES:(pj + 1) * LANES]
            zero = jnp.zeros_like(xp)
            pairs.append(_dot(gmat[0], jnp.where(lo_mask, xp, zero))
                         + _dot(gmat[1], jnp.where(lo_mask, zero, xp)))
        y_diag = jnp.concatenate(pairs, axis=1)

        st = state_ref[gi]
        y_off = _dot(cg_b, st.astype(BF16)) * expa_rep
        xdd = (xd * dte_rep).astype(BF16)
        state_ref[gi] = st * cd_rep + _dot(bg.T.astype(BF16), xdd)
        ypre_ref[gi] = y_diag + y_off + dsk_rep * xs
        return carry

    lax.fori_loop(0, SSM_GROUPS, group_body, 0, unroll=8)

    for gi in range(SSM_GROUPS):
        cols = slice(gi * GROUP_W, (gi + 1) * GROUP_W)
        zz = z_ref[:, cols].astype(F32)
        y = ypre_ref[gi] * _silu(zz)
        ms = jnp.mean(y * y, axis=-1, keepdims=True)
        o_ref[:, cols] = (y * lax.rsqrt(ms + EPS) * ng_ref[:, cols]).astype(o_ref.dtype)


def _ssd(main, small, conv_w, conv_b, dt_bias, a_log, d_skip, norm_g, batch, seq):
    nc = seq // CHUNK
    hb = CHUNK // CONV_HALO
    row = lambda b, c: b * nc + c
    full = lambda shape: pl.BlockSpec(shape, lambda b, c: (0,) * len(shape))
    shift = jnp.asarray(_conv_shift_matrix(), BF16)
    return pl.pallas_call(
        _ssd_kernel,
        grid=(batch, nc),
        in_specs=[pl.BlockSpec((CHUNK, CONV_DIM), lambda b, c: (row(b, c), M_XBC // CONV_DIM)),
                  pl.BlockSpec((CONV_HALO, CONV_DIM),
                               lambda b, c: (jnp.maximum(row(b, c) * hb - 1, 0), M_XBC // CONV_DIM)),
                  pl.BlockSpec((CHUNK, SSM_INNER), lambda b, c: (row(b, c), M_Z // SSM_INNER)),
                  pl.BlockSpec((CHUNK, S_END), lambda b, c: (row(b, c), 0)),
                  full((CHUNK, CONV_WIDTH * (CONV_HALO + CHUNK))),
                  full((CONV_WIDTH, CONV_DIM)), full((1, CONV_DIM)), full((1, SSM_HEADS)),
                  full((1, SSM_HEADS)), full((1, SSM_HEADS)), full((1, SSM_INNER))],
        out_specs=pl.BlockSpec((CHUNK, SSM_INNER), lambda b, c: (row(b, c), 0)),
        out_shape=jax.ShapeDtypeStruct((batch * seq, SSM_INNER), BF16),
        scratch_shapes=[pltpu.VMEM((SSM_GROUPS, CHUNK, GROUP_W), F32),
                        pltpu.VMEM((SSM_GROUPS, CHUNK, SSM_STATE), F32),
                        pltpu.VMEM((SSM_GROUPS, CHUNK, SSM_STATE), F32),
                        pltpu.VMEM((SSM_GROUPS, SSM_STATE, GROUP_W), F32),
                        pltpu.VMEM((SSM_GROUPS, CHUNK, GROUP_W), F32),
                        pltpu.VMEM((SSM_GROUPS, CHUNK, HEADS_PER_GROUP), F32),
                        pltpu.VMEM((SSM_GROUPS, HEADS_PER_GROUP, CHUNK), F32)],
        compiler_params=_params(("arbitrary", "arbitrary")),
        name="ssd_scan",
    )(main, main, main, small, shift, conv_w, conv_b, dt_bias, a_log, d_skip, norm_g)


def _merge_kernel(att_ref, y_ref, ga_ref, gs_ref, wa_ref, ws_ref, o_ref):
    pa = _dot(att_ref[...], wa_ref[...])
    ps = _dot(y_ref[...], ws_ref[...])
    o_ref[...] = (_sigmoid(ga_ref[...].astype(F32)) * pa + _sigmoid(gs_ref[...].astype(F32)) * ps).astype(o_ref.dtype)


def _merge(att, y, main, wa, ws, tm, tn):
    t = att.shape[0]
    return pl.pallas_call(
        _merge_kernel,
        grid=(t // tm, D_MODEL // tn),
        in_specs=[pl.BlockSpec((tm, ATT_Q), lambda i, j: (i, 0)),
                  pl.BlockSpec((tm, SSM_INNER), lambda i, j: (i, 0)),
                  pl.BlockSpec((tm, tn), lambda i, j: (i, M_GA // tn + j)),
                  pl.BlockSpec((tm, tn), lambda i, j: (i, M_GS // tn + j)),
                  pl.BlockSpec((ATT_Q, tn), lambda i, j: (0, j)),
                  pl.BlockSpec((SSM_INNER, tn), lambda i, j: (0, j))],
        out_specs=pl.BlockSpec((tm, tn), lambda i, j: (i, j)),
        out_shape=jax.ShapeDtypeStruct((t, D_MODEL), BF16),
        compiler_params=_params(("parallel", "arbitrary")),
        name="gated_merge",
    )(att, y, main, main, wa, ws)


def _out_proj_kernel(m_ref, w_ref, x_ref, g_ref, x1_ref, h2_ref):
    x1 = x_ref[...] + _dot(m_ref[...], w_ref[...])
    x1_ref[...] = x1
    ms = jnp.mean(x1 * x1, axis=-1, keepdims=True)
    h2_ref[...] = (x1 * lax.rsqrt(ms + EPS) * g_ref[...]).astype(BF16)


def _out_proj(merged, w, x, g, tm):
    t = x.shape[0]
    return pl.pallas_call(
        _out_proj_kernel,
        grid=(t // tm,),
        in_specs=[pl.BlockSpec((tm, D_MODEL), lambda i: (i, 0)),
                  pl.BlockSpec((D_MODEL, D_MODEL), lambda i: (0, 0)),
                  pl.BlockSpec((tm, D_MODEL), lambda i: (i, 0)),
                  pl.BlockSpec((1, D_MODEL), lambda i: (0, 0))],
        out_specs=[pl.BlockSpec((tm, D_MODEL), lambda i: (i, 0)),
                   pl.BlockSpec((tm, D_MODEL), lambda i: (i, 0))],
        out_shape=[jax.ShapeDtypeStruct((t, D_MODEL), F32),
                   jax.ShapeDtypeStruct((t, D_MODEL), BF16)],
        compiler_params=_params(("parallel",)),
        name="out_proj_norm",
    )(merged, w, x, g)


def _mlp_kernel(h_ref, x1_ref, wu_ref, wd_ref, o_ref):
    @pl.when(pl.program_id(1) == 0)
    def _():
        o_ref[...] = x1_ref[...]

    u = _dot(h_ref[...], wu_ref[...])
    u = jnp.square(jnp.maximum(u, 0.0)).astype(BF16)
    o_ref[...] += _dot(u, wd_ref[...])


def _mlp(h2, x1, wu, wd, tm, th):
    t = h2.shape[0]
    return pl.pallas_call(
        _mlp_kernel,
        grid=(t // tm, MLP_HIDDEN // th),
        in_specs=[pl.BlockSpec((tm, D_MODEL), lambda i, j: (i, 0)),
                  pl.BlockSpec((tm, D_MODEL), lambda i, j: (i, 0)),
                  pl.BlockSpec((D_MODEL, th), lambda i, j: (0, j)),
                  pl.BlockSpec((th, D_MODEL), lambda i, j: (j, 0))],
        out_specs=pl.BlockSpec((tm, D_MODEL), lambda i, j: (i, 0)),
        out_shape=jax.ShapeDtypeStruct((t, D_MODEL), F32),
        compiler_params=_params(("parallel", "arbitrary")),
        name="relu2_mlp",
    )(h2, x1, wu, wd)


def _pack_w_small(w_t):
    assert (O_GA, O_KI, O_Z, O_DT) == (M_GA, M_Z, O_WI + IDX_HEADS, O_Z + M_END - M_Z)
    return jnp.concatenate([w_t[O_KI:O_Z], w_t[O_DT:O_DT + SSM_HEADS],
                            jnp.zeros((S_END - S_DT - SSM_HEADS, w_t.shape[1]), w_t.dtype)], axis=0)


def _block(x2, batch, seq, norm1_g, w_in, conv_w, conv_b, dt_bias, a_log, d_skip, ssm_norm_g, q_norm_g,
           k_norm_g, rel_bias, w_att_branch, w_ssm_branch, w_out, norm2_g, w_up, w_down):
    row = lambda v: v.reshape(1, -1)
    w_t = w_in.T
    xn, small = _norm_small(x2, row(norm1_g), _pack_w_small(w_t), tm=min(1024, x2.shape[0]))
    main, (w_att_b, w_ssm_b, w_out_b, w_up_b, w_down_b) = _in_proj(
        xn, w_t, (w_att_branch, w_ssm_branch, w_out, w_up, w_down), tm=min(1024, x2.shape[0]))
    qt, kn, vt, qit, wt, ki = _qk_prep(main, small, row(q_norm_g), row(k_norm_g), batch, seq, tq=DSA_TQ)
    att = _dsa(qt, kn, vt, qit, wt, ki, rel_bias, batch, seq)
    y = _ssd(main, small, conv_w, row(conv_b), row(dt_bias), row(a_log), row(d_skip), row(ssm_norm_g),
             batch, seq)
    merged = _merge(att, y, main, w_att_b, w_ssm_b, tm=512, tn=1024)
    x1, h2 = _out_proj(merged, w_out_b, x2, row(norm2_g), tm=512)
    return _mlp(h2, x1, w_up_b, w_down_b, tm=512, th=1024)


def kernel(x, norm1_g, w_in, conv_w, conv_b, dt_bias, a_log, d_skip, ssm_norm_g, q_norm_g, k_norm_g, rel_bias,
           w_att_branch, w_ssm_branch, w_out, norm2_g, w_up, w_down):
    batch, seq, d = x.shape
    x2 = x.reshape(batch * seq, d)
    for l in range(norm1_g.shape[0]):
        x2 = _block(x2, batch, seq, norm1_g[l], w_in[l], conv_w[l], conv_b[l], dt_bias[l], a_log[l], d_skip[l],
                    ssm_norm_g[l], q_norm_g[l], k_norm_g[l], rel_bias, w_att_branch[l], w_ssm_branch[l],
                    w_out[l], norm2_g[l], w_up[l], w_down[l])
    return x2.reshape(batch, seq, d)
```

```python
import functools
import math

import numpy as np
import jax
import jax.numpy as jnp
from jax import lax
from jax.experimental import pallas as pl
from jax.experimental.pallas import tpu as pltpu

F32 = jnp.float32
BF16 = jnp.bfloat16
I32 = jnp.int32

D_MODEL = 2048
ATT_HEADS = 16
ATT_KV_HEADS = 4
HEAD_DIM = 128
REP = ATT_HEADS // ATT_KV_HEADS
IDX_HEADS = 16
IDX_DIM = 64
TOPK_MAX = 256
N_BUCKETS = 32
MAX_DISTANCE = 128
SSM_INNER = 2 * D_MODEL
SSM_HEAD_DIM = 64
SSM_HEADS = SSM_INNER // SSM_HEAD_DIM
SSM_GROUPS = 8
SSM_STATE = 128
CONV_WIDTH = 4
CHUNK = 128
MLP_HIDDEN = 4 * D_MODEL
EPS = 1e-6

ATT_Q = ATT_HEADS * HEAD_DIM
ATT_KV = ATT_KV_HEADS * HEAD_DIM
IDX_Q = IDX_HEADS * IDX_DIM
SSM_BC = SSM_GROUPS * SSM_STATE
CONV_DIM = SSM_INNER + 2 * SSM_BC
SPLITS = (D_MODEL, D_MODEL, ATT_Q, ATT_KV, ATT_KV, IDX_Q, IDX_DIM, IDX_HEADS, SSM_INNER, CONV_DIM, SSM_HEADS)
_OFFS = tuple(int(v) for v in np.cumsum((0,) + SPLITS))
(O_GA, O_GS, O_Q, O_K, O_V, O_QI, O_KI, O_WI, O_Z, O_XBC, O_DT, _O_END) = _OFFS

M_GA = 0
M_GS = M_GA + D_MODEL
M_Q = M_GS + D_MODEL
M_K = M_Q + ATT_Q
M_V = M_K + ATT_KV
M_QI = M_V + ATT_KV
M_Z = M_QI + IDX_Q
M_XBC = M_Z + SSM_INNER
M_END = M_XBC + CONV_DIM
S_KI = 0
S_WI = S_KI + IDX_DIM
S_DT = S_WI + IDX_HEADS
S_END = 256

HEADS_PER_GROUP = SSM_HEADS // SSM_GROUPS
GROUP_W = HEADS_PER_GROUP * SSM_HEAD_DIM

LANES = 128
VMEM_LIMIT = 56 * 1024 * 1024

NEG_INF = float("-inf")
INT_MIN = -(2 ** 31)
KEY_NEG_INF = int(np.int32(np.uint32(0xFF800000) ^ np.uint32(0x7FFFFFFF)))


def _dot(a, b):
    return jnp.dot(a, b, preferred_element_type=F32)


def _dot_nt(a, b):
    return lax.dot_general(a, b, (((1,), (1,)), ((), ())), preferred_element_type=F32)


def _split3(x):
    hi = x.astype(BF16)
    r = x - hi.astype(F32)
    mid = r.astype(BF16)
    lo = (r - mid.astype(F32)).astype(BF16)
    return hi, mid, lo


def _silu(x):
    h = 0.5 * x
    return h + h * jnp.tanh(h)


def _sigmoid(x):
    return 1.0 / (1.0 + jnp.exp(-x))


def _params(sem):
    return pltpu.CompilerParams(dimension_semantics=sem, vmem_limit_bytes=VMEM_LIMIT)


def _norm_small_kernel(x_ref, g_ref, w_ref, xn_ref, sm_ref):
    x = x_ref[...]
    ms = jnp.mean(x * x, axis=-1, keepdims=True)
    xn = (x * lax.rsqrt(ms + EPS) * g_ref[...]).astype(BF16)
    xn_ref[...] = xn
    sm_ref[...] = _dot_nt(xn, w_ref[...].astype(BF16))


def _norm_small(x, g, w_small_t, tm):
    m, d = x.shape
    n = w_small_t.shape[0]
    return pl.pallas_call(
        _norm_small_kernel,
        grid=(m // tm,),
        in_specs=[pl.BlockSpec((tm, d), lambda i: (i, 0)),
                  pl.BlockSpec((1, d), lambda i: (0, 0)),
                  pl.BlockSpec((n, d), lambda i: (0, 0))],
        out_specs=[pl.BlockSpec((tm, d), lambda i: (i, 0)),
                   pl.BlockSpec((tm, n), lambda i: (i, 0))],
        out_shape=[jax.ShapeDtypeStruct((m, d), BF16), jax.ShapeDtypeStruct((m, n), F32)],
        compiler_params=_params(("parallel",)),
        name="norm_small_proj",
    )(x, g, w_small_t)


IN_TN = 1024
IN_ALIGNED_TILES = O_KI // IN_TN
IN_SHIFT = O_Z - M_Z
IN_NEXT = 128
IN_ROWS = 256


def _in_proj_kernel(xn_ref, wa_ref, wn_ref, *rest, n_cast, n_side):
    side_in, o_ref, side_out, w_scr = rest[:n_side], rest[n_side], rest[n_side + 1:2 * n_side + 1], rest[-1]
    j = pl.program_id(0)
    i = pl.program_id(1)

    @pl.when(j * pl.num_programs(1) + i < n_cast)
    def _():
        for src, dst in zip(side_in, side_out):
            dst[...] = src[...].astype(BF16)

    @pl.when((i == 0) & (j < IN_ALIGNED_TILES))
    def _():
        for r0 in range(0, IN_TN, IN_ROWS):
            w_scr[r0:r0 + IN_ROWS, :] = wa_ref[r0:r0 + IN_ROWS, :].astype(BF16)

    @pl.when((i == 0) & (j >= IN_ALIGNED_TILES))
    def _():
        for r0 in range(0, IN_TN - IN_SHIFT, IN_ROWS):
            r1 = min(r0 + IN_ROWS, IN_TN - IN_SHIFT)
            w_scr[r0:r1, :] = wa_ref[r0 + IN_SHIFT:r1 + IN_SHIFT, :].astype(BF16)
        w_scr[IN_TN - IN_SHIFT:IN_TN, :] = wn_ref[0:IN_SHIFT, :].astype(BF16)

    o_ref[...] = _dot_nt(xn_ref[...], w_scr[...]).astype(o_ref.dtype)


def _in_proj(xn, w_t, side_weights, tm):
    m, d = xn.shape
    assert O_KI % IN_TN == 0 and M_END % IN_TN == 0 and IN_TN % IN_NEXT == 0
    assert 0 < IN_SHIFT <= IN_NEXT and IN_SHIFT % 16 == 0
    nj, ni = M_END // IN_TN, m // tm
    n_cast = 1 << ((nj * ni).bit_length() - 1)
    side_specs = []
    for w in side_weights:
        rows = w.shape[0] // n_cast
        assert w.shape[0] % n_cast == 0 and rows % 16 == 0
        side_specs.append(pl.BlockSpec((rows, w.shape[1]), lambda j, i: (jnp.minimum(j * ni + i, n_cast - 1), 0)))
    kern = functools.partial(_in_proj_kernel, n_cast=n_cast, n_side=len(side_weights))
    outs = pl.pallas_call(
        kern,
        grid=(nj, ni),
        in_specs=[pl.BlockSpec((tm, d), lambda j, i: (i, 0)),
                  pl.BlockSpec((IN_TN, d), lambda j, i: (j, 0)),
                  pl.BlockSpec((IN_NEXT, d), lambda j, i: ((j + 1) * (IN_TN // IN_NEXT), 0))] + side_specs,
        out_specs=[pl.BlockSpec((tm, IN_TN), lambda j, i: (i, j))] + side_specs,
        out_shape=[jax.ShapeDtypeStruct((m, M_END), BF16)]
                  + [jax.ShapeDtypeStruct(w.shape, BF16) for w in side_weights],
        scratch_shapes=[pltpu.VMEM((IN_TN, d), BF16)],
        compiler_params=_params(("arbitrary", "arbitrary")),
        name="in_proj",
    )(xn, w_t, w_t, *side_weights)
    return outs[0], outs[1:]


LOG2E = math.log2(math.e)
VT_ROWS = HEAD_DIM + 16


def _qk_prep_kernel(q_ref, k_ref, v_ref, qi_ref, sm_ref, qg_ref, kg_ref,
                    qt_ref, kn_ref, vt_ref, qit_ref, wt_ref, ki_ref):
    qg = qg_ref[...]
    for h in range(ATT_HEADS):
        x = q_ref[:, h * HEAD_DIM:(h + 1) * HEAD_DIM].astype(F32)
        ms = jnp.mean(x * x, axis=-1, keepdims=True)
        y = x * lax.rsqrt(ms + EPS) * qg * (HEAD_DIM ** -0.5 * LOG2E)
        qt_ref[0, h] = y.T.astype(BF16)
    kg = kg_ref[...]
    for h in range(ATT_KV_HEADS):
        x = k_ref[:, h * HEAD_DIM:(h + 1) * HEAD_DIM].astype(F32)
        ms = jnp.mean(x * x, axis=-1, keepdims=True)
        kn_ref[:, h * HEAD_DIM:(h + 1) * HEAD_DIM] = (x * lax.rsqrt(ms + EPS) * kg).astype(BF16)
        vt_ref[0, h, 0, 0:HEAD_DIM, :] = v_ref[:, h * HEAD_DIM:(h + 1) * HEAD_DIM].astype(F32).T.astype(BF16)
        vt_ref[0, h, 0, HEAD_DIM:VT_ROWS, :] = jnp.ones((VT_ROWS - HEAD_DIM, v_ref.shape[0]), BF16)
    for p in range(IDX_Q // LANES):
        qit_ref[0, p * LANES:(p + 1) * LANES, :] = qi_ref[:, p * LANES:(p + 1) * LANES].astype(F32).T.astype(BF16)
    sm_t = sm_ref[:, 0:LANES].T
    wt_ref[0] = sm_t[S_WI:S_WI + IDX_HEADS, :] * (IDX_HEADS ** -0.5 * IDX_DIM ** -0.5)
    ki_ref[...] = sm_ref[:, S_KI:S_KI + IDX_DIM].astype(BF16)


def _qk_prep(main, small, qg, kg, batch, seq, tq):
    t = main.shape[0]
    nq = seq // tq
    return pl.pallas_call(
        _qk_prep_kernel,
        grid=(batch, nq),
        in_specs=[pl.BlockSpec((tq, ATT_Q), lambda b, i: (b * nq + i, M_Q // ATT_Q)),
                  pl.BlockSpec((tq, ATT_KV), lambda b, i: (b * nq + i, M_K // ATT_KV)),
                  pl.BlockSpec((tq, ATT_KV), lambda b, i: (b * nq + i, M_V // ATT_KV)),
                  pl.BlockSpec((tq, IDX_Q), lambda b, i: (b * nq + i, M_QI // IDX_Q)),
                  pl.BlockSpec((tq, S_END), lambda b, i: (b * nq + i, 0)),
                  pl.BlockSpec((1, HEAD_DIM), lambda b, i: (0, 0)),
                  pl.BlockSpec((1, HEAD_DIM), lambda b, i: (0, 0))],
        out_specs=[pl.BlockSpec((1, ATT_HEADS, HEAD_DIM, tq), lambda b, i: (b * nq + i, 0, 0, 0)),
                   pl.BlockSpec((tq, ATT_KV), lambda b, i: (b * nq + i, 0)),
                   pl.BlockSpec((1, ATT_KV_HEADS, 1, VT_ROWS, tq), lambda b, i: (b, 0, i, 0, 0)),
                   pl.BlockSpec((1, IDX_Q, tq), lambda b, i: (b * nq + i, 0, 0)),
                   pl.BlockSpec((1, IDX_HEADS, tq), lambda b, i: (b * nq + i, 0, 0)),
                   pl.BlockSpec((tq, IDX_DIM), lambda b, i: (b * nq + i, 0))],
        out_shape=[jax.ShapeDtypeStruct((t // tq, ATT_HEADS, HEAD_DIM, tq), BF16),
                   jax.ShapeDtypeStruct((t, ATT_KV), BF16),
                   jax.ShapeDtypeStruct((batch, ATT_KV_HEADS, nq, VT_ROWS, tq), BF16),
                   jax.ShapeDtypeStruct((t // tq, IDX_Q, tq), BF16),
                   jax.ShapeDtypeStruct((t // tq, IDX_HEADS, tq), F32),
                   jax.ShapeDtypeStruct((t, IDX_DIM), BF16)],
        compiler_params=_params(("parallel", "parallel")),
        name="qk_prep",
    )(main, main, main, main, small, qg, kg)


DSA_TQ = 256
DSA_TK = 256
CNT_WAYS = 4


def _t5_bucket_np(dist):
    n = np.maximum(dist, 0)
    max_exact = N_BUCKETS // 2
    nf = np.maximum(n, 1).astype(np.float32)
    ratio = (np.log(nf / np.float32(max_exact)) / np.float32(math.log(MAX_DISTANCE / max_exact))
             * np.float32(N_BUCKETS - max_exact))
    large = max_exact + ratio.astype(np.int32)
    large = np.minimum(large, N_BUCKETS - 1)
    return np.where(n < max_exact, n, large).astype(np.int32)


def _bias_bucket_tiles(tq, tk):
    r = np.arange(tq)[None, :]
    c = np.arange(tk)[:, None]
    d0 = _t5_bucket_np(r - c)
    d1 = _t5_bucket_np(tk + r - c)
    assert np.all(_t5_bucket_np(np.arange(tk + 1, 8 * tk)) == N_BUCKETS - 1)
    return np.stack([d0, d1]).astype(np.int32)


def _dsa_kernel(relb_ref, bidx_ref, qt_ref, qit_ref, wt_ref, k_ref, vt_ref, ki_ref, o_ref,
                key_ref, plane_ref, gmax_ref, madd_ref, bias_ref, s_buf, p_buf, st_ref, acc_ref,
                *, tq, tk, nkc, topk):
    b = pl.program_id(0)
    i = pl.program_id(1)
    neg_slot = nkc

    @pl.when((b == 0) & (i == 0))
    def _init():
        madd_ref[neg_slot] = jnp.full((tk, tq), NEG_INF, F32)
        for t in range(2):
            bt = bidx_ref[t]

            def head_body(h, carry):
                far = relb_ref[N_BUCKETS - 1, h]

                def bucket_body(bk, acc):
                    return jnp.where(bt == bk, (relb_ref[bk, h] - far) * LOG2E, acc)

                bias_ref[t, h] = lax.fori_loop(0, N_BUCKETS, bucket_body, jnp.zeros((tk, tq), F32))
                return carry

            lax.fori_loop(0, ATT_HEADS, head_body, 0)

    def _select():
        nj = i + 1
        qpos = i * tq + lax.broadcasted_iota(I32, (tk, tq), 1)

        def score_chunk(j):
            kc = ki_ref[pl.ds(pl.multiple_of(j * tk, tk), tk), :]
            acc = jnp.zeros((tk, tq), F32)
            for h in range(IDX_HEADS):
                z = _dot(kc, qit_ref[0, h * IDX_DIM:(h + 1) * IDX_DIM, :])
                acc = acc + jnp.maximum(z, 0.0) * wt_ref[0, h:h + 1, :]
            kpos = j * tk + lax.broadcasted_iota(I32, (tk, tq), 0)
            acc = jnp.where(kpos <= qpos, acc, NEG_INF)
            gmax_ref[...] = jnp.maximum(gmax_ref[...], acc)
            bits = pltpu.bitcast(acc, I32)
            bits = jnp.where(bits == INT_MIN, 0, bits)
            key = bits ^ ((bits >> 31) & 0x7FFFFFFF)
            key_ref[j] = key
            plane_ref[0, j] = pltpu.bitcast(bits & jnp.int32(-65536), F32).astype(BF16)
            plane_ref[1, j] = ((key >> 8) & 0xFF).astype(F32).astype(BF16)
            plane_ref[2, j] = (key & 0xFF).astype(F32).astype(BF16)

        def score_quad(u, carry):
            for v in range(4):
                score_chunk(4 * u + v)
            return carry

        gmax_ref[...] = jnp.full((tk, tq), NEG_INF, F32)
        lax.fori_loop(0, nj // 4, score_quad, 0)
        done = (nj // 4) * 4

        @pl.when(nj % 4 >= 2)
        def _():
            score_chunk(done)
            score_chunk(done + 1)

        @pl.when(nj % 2 == 1)
        def _():
            score_chunk(nj - 1)

        kf = float(topk)
        one_b = jnp.ones((tk, tq), BF16)
        zero_b = jnp.zeros((tk, tq), BF16)
        grp = 16 * CNT_WAYS

        def count(plane, cand_b, strict):
            def body(j, cnt):
                pv = plane_ref[plane, j]
                hit = jnp.where(pv > cand_b if strict else pv >= cand_b, one_b, zero_b)
                for r0 in range(0, tk, grp):
                    cnt = cnt + hit[r0:r0 + grp]
                return cnt

            cnt = lax.fori_loop(0, nj, body, jnp.zeros((grp, tq), BF16))
            return jnp.sum(cnt.astype(F32), axis=0, keepdims=True)

        def keep_ties(src, dst, tie_b):
            def body(j, carry):
                plane_ref[dst, j] = jnp.where(plane_ref[src, j] == tie_b, plane_ref[dst, j], -one_b)
                return carry

            lax.fori_loop(0, nj, body, 0)

        def top_digit_float(p16):
            fb = (p16 & 0xFFFF) ^ jnp.where(p16 < 0, 0x7FFF, 0)
            return pltpu.bitcast(lax.shift_left(fb, 16), F32).astype(BF16)

        def top_digit(x):
            xb = pltpu.bitcast(x, I32)
            xb = jnp.where(xb == INT_MIN, 0, xb)
            return ((xb ^ ((xb >> 31) & 0x7FFFFFFF)) >> 16) + 32768

        gm = gmax_ref[...]
        u_hi = top_digit(jnp.max(gm, axis=0, keepdims=True))
        u_lo = top_digit(jnp.min(gm, axis=0, keepdims=True))
        nbits = jnp.max(32 - lax.clz(u_hi ^ u_lo))
        prefix0 = lax.shift_left(lax.shift_right_logical(u_hi, nbits), nbits) - 32768

        def top_iter(it, prefix):
            cand = prefix + lax.shift_left(jnp.int32(1), nbits - 1 - it)
            return jnp.where(count(0, top_digit_float(cand), False) >= kf, cand, prefix)

        d_top = lax.fori_loop(0, nbits, top_iter, prefix0)
        t_top = top_digit_float(d_top)
        above = count(0, t_top, True)

        def byte_digit(plane, above_n):
            def it_body(it, prefix):
                cand = prefix + lax.shift_left(jnp.int32(1), 7 - it)
                tot = above_n + count(plane, cand.astype(F32).astype(BF16), False)
                return jnp.where(tot >= kf, cand, prefix)

            return lax.fori_loop(0, 8, it_body, jnp.zeros((1, tq), I32))

        keep_ties(0, 1, t_top)
        d_mid = byte_digit(1, above)
        t_mid = d_mid.astype(F32).astype(BF16)
        above = above + count(1, t_mid, True)
        keep_ties(1, 2, t_mid)
        d_low = byte_digit(2, above)
        thr = lax.shift_left(d_top, 16) | lax.shift_left(d_mid, 8) | d_low
        thr = jnp.maximum(thr, KEY_NEG_INF + 1)

        def madd_chunk(j, carry):
            madd_ref[j] = jnp.where(key_ref[j] >= thr, 0.0, NEG_INF)
            return carry

        lax.fori_loop(0, nj, madd_chunk, 0)

    _select()

    ng = ATT_KV_HEADS
    c0 = jnp.maximum(i - 1, 0)
    c1 = jnp.minimum(c0 + 1, nkc - 1)
    n_far = (c0 + 1) // 2

    def scores_to(slot, g, ca, madd_c, bias_c):
        rows = pl.ds(pl.multiple_of(ca * tk, tk), 2 * tk)
        qt = jnp.concatenate([qt_ref[0, g * REP + r] for r in range(REP)], axis=1)
        add = jnp.concatenate([madd_c if bias_c is None else madd_c + bias_c[r] for r in range(REP)], axis=1)
        s = _dot(k_ref[rows, g * HEAD_DIM:(g + 1) * HEAD_DIM], qt) + add
        s_buf[slot] = s
        st_ref[ng + 2 + slot:ng + 3 + slot, :] = jnp.max(s, axis=0, keepdims=True)

    def softmax_to(slot, g):
        m = st_ref[g:g + 1, :]
        m_new = jnp.maximum(m, st_ref[ng + 2 + slot:ng + 3 + slot, :])
        m_safe = jnp.where(m_new == NEG_INF, 0.0, m_new)
        alpha = jnp.exp2(m - m_safe)
        p_buf[slot] = jnp.exp2((s_buf[slot] - m_safe).astype(BF16))
        st_ref[g:g + 1, :] = m_new
        st_ref[ng + slot:ng + 1 + slot, :] = alpha

    def values_from(slot, g, ca, cb):
        vt_c = jnp.concatenate([vt_ref[0, g, ca], vt_ref[0, g, cb]], axis=1)
        acc_ref[g] = st_ref[ng + slot:ng + 1 + slot, :] * acc_ref[g] + _dot(vt_c, p_buf[slot])

    def near_scores_to(slot, g):
        first = i == 0
        t_a = jnp.where(first, 0, 1)
        idx_b = jnp.where(first, neg_slot, c0 + 1)
        madd_near = jnp.concatenate([madd_ref[c0], madd_ref[idx_b]], axis=0)
        bias_near = [jnp.concatenate([bias_ref[t_a, g * REP + r], bias_ref[0, g * REP + r]], axis=0)
                     for r in range(REP)]
        scores_to(slot, g, c0, madd_near, bias_near)

    def far_scores_to(slot, g, k):
        ca = jnp.minimum(2 * (k - 1), nkc - 2)
        cb = jnp.where(ca + 1 >= c0, neg_slot, ca + 1)
        scores_to(slot, g, ca, jnp.concatenate([madd_ref[ca], madd_ref[cb]], axis=0), None)

    st_ref[0:ng, :] = jnp.full((ng, REP * tq), NEG_INF, F32)
    acc_ref[...] = jnp.zeros_like(acc_ref)

    near_scores_to(0, 0)
    for g in range(ng):
        if g + 1 < ng:
            near_scores_to((g + 1) % 2, g + 1)
        else:
            far_scores_to(0, 0, 1)
        softmax_to(g % 2, g)
        values_from(g % 2, g, c0, c1)

    def far_step(k, carry):
        for g in range(ng):
            if g + 1 < ng:
                far_scores_to((g + 1) % 2, g + 1, k)
            else:
                far_scores_to(0, 0, k + 1)
            softmax_to(g % 2, g)
            values_from(g % 2, g, 2 * (k - 1), 2 * (k - 1) + 1)
        return carry

    lax.fori_loop(1, n_far + 1, far_step, 0)

    for g in range(ng):
        out = acc_ref[g, 0:HEAD_DIM, :] / acc_ref[g, HEAD_DIM:HEAD_DIM + 1, :]
        for r in range(REP):
            h = g * REP + r
            o_ref[:, h * HEAD_DIM:(h + 1) * HEAD_DIM] = out[:, r * tq:(r + 1) * tq].T.astype(o_ref.dtype)


def _dsa(qt, kn, vt, qit, wt, ki, rel_bias, batch, seq):
    tq, tk = DSA_TQ, DSA_TK
    nq = seq // tq
    nkc = seq // tk
    topk = min(TOPK_MAX, seq // 4)
    assert topk <= tk
    bidx = jnp.asarray(_bias_bucket_tiles(tq, tk))
    kern = functools.partial(_dsa_kernel, tq=tq, tk=tk, nkc=nkc, topk=topk)
    once = pl.Buffered(1)
    return pl.pallas_call(
        kern,
        grid=(batch, nq),
        in_specs=[pl.BlockSpec(memory_space=pltpu.SMEM),
                  pl.BlockSpec((2, tk, tq), lambda b, i: (0, 0, 0), pipeline_mode=once),
                  pl.BlockSpec((1, ATT_HEADS, HEAD_DIM, tq), lambda b, i: (b * nq + i, 0, 0, 0)),
                  pl.BlockSpec((1, IDX_Q, tq), lambda b, i: (b * nq + i, 0, 0)),
                  pl.BlockSpec((1, IDX_HEADS, tq), lambda b, i: (b * nq + i, 0, 0)),
                  pl.BlockSpec((seq, ATT_KV), lambda b, i: (b, 0), pipeline_mode=once),
                  pl.BlockSpec((1, ATT_KV_HEADS, nkc, VT_ROWS, tk), lambda b, i: (b, 0, 0, 0, 0),
                               pipeline_mode=once),
                  pl.BlockSpec((seq, IDX_DIM), lambda b, i: (b, 0), pipeline_mode=once)],
        out_specs=pl.BlockSpec((tq, ATT_Q), lambda b, i: (b * nq + i, 0)),
        out_shape=jax.ShapeDtypeStruct((batch * seq, ATT_Q), BF16),
        scratch_shapes=[pltpu.VMEM((nkc, tk, tq), I32),
                        pltpu.VMEM((3, nkc, tk, tq), BF16),
                        pltpu.VMEM((tk, tq), F32),
                        pltpu.VMEM((nkc + 1, tk, tq), F32),
                        pltpu.VMEM((2, ATT_HEADS, tk, tq), F32),
                        pltpu.VMEM((2, 2 * tk, REP * tq), F32),
                        pltpu.VMEM((2, 2 * tk, REP * tq), BF16),
                        pltpu.VMEM((ATT_KV_HEADS + 4, REP * tq), F32),
                        pltpu.VMEM((ATT_KV_HEADS, VT_ROWS, REP * tq), F32)],
        compiler_params=_params(("arbitrary", "arbitrary")),
        name="dsa_attention",
    )(rel_bias, bidx, qt, qit, wt, kn, vt, ki)


E_ROWS = 3 * CHUNK + 16
CONV_HALO = 16


def _conv_shift_matrix():
    ext = CONV_HALO + CHUNK
    s = np.zeros((CHUNK, CONV_WIDTH * ext), np.float32)
    for k in range(CONV_WIDTH):
        t = np.arange(CHUNK)
        s[t, k * ext + CONV_HALO + t - (CONV_WIDTH - 1) + k] = 1.0
    return s


def _ssd_kernel(xbc_ref, halo_ref, z_ref, sm_ref, shift_ref, cw_ref, cb_ref, dtb_ref, alog_ref, dsk_ref, ng_ref,
                o_ref, xs_ref, bm_ref, cm_ref, state_ref, ypre_ref, acg_ref, actg_ref):
    c = pl.program_id(1)
    L = CHUNK

    @pl.when(c == 0)
    def _():
        state_ref[...] = jnp.zeros_like(state_ref)

    halo_on = (c > 0).astype(BF16)
    shift = shift_ref[...]
    cblk = GROUP_W
    for cbi in range(CONV_DIM // cblk):
        cols = slice(cbi * cblk, (cbi + 1) * cblk)
        ext = jnp.concatenate([halo_ref[:, cols] * halo_on, xbc_ref[:, cols]], axis=0)
        wtap = cw_ref[:, cols].astype(BF16)
        prods = jnp.concatenate([ext * wtap[kk:kk + 1, :] for kk in range(CONV_WIDTH)], axis=0)
        y = _silu(cb_ref[:, cols] + _dot(shift, prods))
        if cbi < SSM_GROUPS:
            xs_ref[cbi] = y
        else:
            per = cblk // SSM_STATE
            for u in range(per):
                gi = (cbi - SSM_GROUPS) * per + u
                piece = y[:, u * SSM_STATE:(u + 1) * SSM_STATE]
                if gi < SSM_GROUPS:
                    bm_ref[gi] = piece
                else:
                    cm_ref[gi - SSM_GROUPS] = piece

    dt_in = sm_ref[:, S_DT:S_DT + SSM_HEADS] + dtb_ref[...]
    dt_act = jnp.maximum(dt_in, 0.0) + jnp.log1p(jnp.exp(-jnp.abs(dt_in)))
    a = dt_act * (-jnp.exp(alog_ref[...]))
    ri = lax.broadcasted_iota(I32, (L, L), 0)
    ci = lax.broadcasted_iota(I32, (L, L), 1)
    tril = ri >= ci
    tri_b = jnp.where(tril, 1.0, 0.0).astype(BF16)
    a_cum = sum(_dot(tri_b, p) for p in _split3(a))
    eye_b = jnp.where(lax.broadcasted_iota(I32, (SSM_HEADS, SSM_HEADS), 0)
                      == lax.broadcasted_iota(I32, (SSM_HEADS, SSM_HEADS), 1), 1.0, 0.0).astype(BF16)
    a_cum_t = sum(_dot_nt(eye_b, p) for p in _split3(a_cum))
    a_last = a_cum[L - 1:L, :]
    for gi in range(SSM_GROUPS):
        acg_ref[gi] = a_cum[:, gi * HEADS_PER_GROUP:(gi + 1) * HEADS_PER_GROUP]
        actg_ref[gi] = a_cum_t[gi * HEADS_PER_GROUP:(gi + 1) * HEADS_PER_GROUP, :]
    cd3 = _split3(jnp.exp(a_last))
    ds3 = _split3(dsk_ref[...])
    extras = jnp.concatenate([p.astype(F32) for p in cd3 + ds3]
                             + [jnp.zeros((E_ROWS - 3 * L - 6, SSM_HEADS), F32)], axis=0)
    e_mat = jnp.concatenate([dt_act, jnp.exp(a_cum), jnp.exp(a_last - a_cum), extras], axis=0).astype(BF16)

    lane = lax.broadcasted_iota(I32, (L, LANES), 1)
    lo_mask = lane < SSM_HEAD_DIM

    def group_body(gi, carry):
        xs = xs_ref[gi]
        bg = bm_ref[gi]
        cg_b = cm_ref[gi].astype(BF16)
        hsel = (lax.broadcasted_iota(I32, (SSM_HEADS, GROUP_W), 0)
                == gi * HEADS_PER_GROUP + lax.broadcasted_iota(I32, (SSM_HEADS, GROUP_W), 1) // SSM_HEAD_DIM)
        ex = _dot(e_mat, jnp.where(hsel, 1.0, 0.0).astype(BF16))
        dt_rep = ex[0:L]
        expa_rep = ex[L:2 * L]
        dte_rep = ex[2 * L:3 * L]
        cd_rep = ex[3 * L:3 * L + 1] + ex[3 * L + 1:3 * L + 2] + ex[3 * L + 2:3 * L + 3]
        dsk_rep = ex[3 * L + 3:3 * L + 4] + ex[3 * L + 4:3 * L + 5] + ex[3 * L + 5:3 * L + 6]

        xd = xs * dt_rep
        xd_b = xd.astype(BF16)
        cb = _dot_nt(cg_b, bg.astype(BF16))
        acg = acg_ref[gi]
        actg = actg_ref[gi]
        pairs = []
        for pj in range(HEADS_PER_GROUP // 2):
            gmat = []
            for e in (2 * pj, 2 * pj + 1):
                seg = acg[:, e:e + 1] - actg[e:e + 1, :]
                dec = jnp.exp(jnp.where(tril, seg, NEG_INF))
                gmat.append((cb * dec).astype(BF16))
            xp = xd_b[:, pj * LANES:(pj + 1) * LANES]
            zero = jnp.zeros_like(xp)
            pairs.append(_dot(gmat[0], jnp.where(lo_mask, xp, zero))
                         + _dot(gmat[1], jnp.where(lo_mask, zero, xp)))
        y_diag = jnp.concatenate(pairs, axis=1)

        st = state_ref[gi]
        y_off = _dot(cg_b, st.astype(BF16)) * expa_rep
        xdd = (xd * dte_rep).astype(BF16)
        state_ref[gi] = st * cd_rep + _dot(bg.T.astype(BF16), xdd)
        ypre_ref[gi] = y_diag + y_off + dsk_rep * xs
        return carry

    lax.fori_loop(0, SSM_GROUPS, group_body, 0, unroll=8)

    for gi in range(SSM_GROUPS):
        cols = slice(gi * GROUP_W, (gi + 1) * GROUP_W)
        zz = z_ref[:, cols].astype(F32)
        y = ypre_ref[gi] * _silu(zz)
        ms = jnp.mean(y * y, axis=-1, keepdims=True)
        o_ref[:, cols] = (y * lax.rsqrt(ms + EPS) * ng_ref[:, cols]).astype(o_ref.dtype)


def _ssd(main, small, conv_w, conv_b, dt_bias, a_log, d_skip, norm_g, batch, seq):
    nc = seq // CHUNK
    hb = CHUNK // CONV_HALO
    row = lambda b, c: b * nc + c
    full = lambda shape: pl.BlockSpec(shape, lambda b, c: (0,) * len(shape))
    shift = jnp.asarray(_conv_shift_matrix(), BF16)
    return pl.pallas_call(
        _ssd_kernel,
        grid=(batch, nc),
        in_specs=[pl.BlockSpec((CHUNK, CONV_DIM), lambda b, c: (row(b, c), M_XBC // CONV_DIM)),
                  pl.BlockSpec((CONV_HALO, CONV_DIM),
                               lambda b, c: (jnp.maximum(row(b, c) * hb - 1, 0), M_XBC // CONV_DIM)),
                  pl.BlockSpec((CHUNK, SSM_INNER), lambda b, c: (row(b, c), M_Z // SSM_INNER)),
                  pl.BlockSpec((CHUNK, S_END), lambda b, c: (row(b, c), 0)),
                  full((CHUNK, CONV_WIDTH * (CONV_HALO + CHUNK))),
                  full((CONV_WIDTH, CONV_DIM)), full((1, CONV_DIM)), full((1, SSM_HEADS)),
                  full((1, SSM_HEADS)), full((1, SSM_HEADS)), full((1, SSM_INNER))],
        out_specs=pl.BlockSpec((CHUNK, SSM_INNER), lambda b, c: (row(b, c), 0)),
        out_shape=jax.ShapeDtypeStruct((batch * seq, SSM_INNER), BF16),
        scratch_shapes=[pltpu.VMEM((SSM_GROUPS, CHUNK, GROUP_W), F32),
                        pltpu.VMEM((SSM_GROUPS, CHUNK, SSM_STATE), F32),
                        pltpu.VMEM((SSM_GROUPS, CHUNK, SSM_STATE), F32),
                        pltpu.VMEM((SSM_GROUPS, SSM_STATE, GROUP_W), F32),
                        pltpu.VMEM((SSM_GROUPS, CHUNK, GROUP_W), F32),
                        pltpu.VMEM((SSM_GROUPS, CHUNK, HEADS_PER_GROUP), F32),
                        pltpu.VMEM((SSM_GROUPS, HEADS_PER_GROUP, CHUNK), F32)],
        compiler_params=_params(("arbitrary", "arbitrary")),
        name="ssd_scan",
    )(main, main, main, small, shift, conv_w, conv_b, dt_bias, a_log, d_skip, norm_g)


def _merge_kernel(att_ref, y_ref, ga_ref, gs_ref, wa_ref, ws_ref, o_ref):
    pa = _dot(att_ref[...], wa_ref[...])
    ps = _dot(y_ref[...], ws_ref[...])
    o_ref[...] = (_sigmoid(ga_ref[...].astype(F32)) * pa + _sigmoid(gs_ref[...].astype(F32)) * ps).astype(o_ref.dtype)


def _merge(att, y, main, wa, ws, tm, tn):
    t = att.shape[0]
    return pl.pallas_call(
        _merge_kernel,
        grid=(t // tm, D_MODEL // tn),
        in_specs=[pl.BlockSpec((tm, ATT_Q), lambda i, j: (i, 0)),
                  pl.BlockSpec((tm, SSM_INNER), lambda i, j: (i, 0)),
                  pl.BlockSpec((tm, tn), lambda i, j: (i, M_GA // tn + j)),
                  pl.BlockSpec((tm, tn), lambda i, j: (i, M_GS // tn + j)),
                  pl.BlockSpec((ATT_Q, tn), lambda i, j: (0, j)),
                  pl.BlockSpec((SSM_INNER, tn), lambda i, j: (0, j))],
        out_specs=pl.BlockSpec((tm, tn), lambda i, j: (i, j)),
        out_shape=jax.ShapeDtypeStruct((t, D_MODEL), BF16),
        compiler_params=_params(("parallel", "arbitrary")),
        name="gated_merge",
    )(att, y, main, main, wa, ws)


def _out_proj_kernel(m_ref, w_ref, x_ref, g_ref, x1_ref, h2_ref):
    x1 = x_ref[...] + _dot(m_ref[...], w_ref[...])
    x1_ref[...] = x1
    ms = jnp.mean(x1 * x1, axis=-1, keepdims=True)
    h2_ref[...] = (x1 * lax.rsqrt(ms + EPS) * g_ref[...]).astype(BF16)


def _out_proj(merged, w, x, g, tm):
    t = x.shape[0]
    return pl.pallas_call(
        _out_proj_kernel,
        grid=(t // tm,),
        in_specs=[pl.BlockSpec((tm, D_MODEL), lambda i: (i, 0)),
                  pl.BlockSpec((D_MODEL, D_MODEL), lambda i: (0, 0)),
                  pl.BlockSpec((tm, D_MODEL), lambda i: (i, 0)),
                  pl.BlockSpec((1, D_MODEL), lambda i: (0, 0))],
        out_specs=[pl.BlockSpec((tm, D_MODEL), lambda i: (i, 0)),
                   pl.BlockSpec((tm, D_MODEL), lambda i: (i, 0))],
        out_shape=[jax.ShapeDtypeStruct((t, D_MODEL), F32),
                   jax.ShapeDtypeStruct((t, D_MODEL), BF16)],
        compiler_params=_params(("parallel",)),
        name="out_proj_norm",
    )(merged, w, x, g)


def _mlp_kernel(h_ref, x1_ref, wu_ref, wd_ref, o_ref):
    @pl.when(pl.program_id(1) == 0)
    def _():
        o_ref[...] = x1_ref[...]

    u = _dot(h_ref[...], wu_ref[...])
    u = jnp.square(jnp.maximum(u, 0.0)).astype(BF16)
    o_ref[...] += _dot(u, wd_ref[...])


def _mlp(h2, x1, wu, wd, tm, th):
    t = h2.shape[0]
    return pl.pallas_call(
        _mlp_kernel,
        grid=(t // tm, MLP_HIDDEN // th),
        in_specs=[pl.BlockSpec((tm, D_MODEL), lambda i, j: (i, 0)),
                  pl.BlockSpec((tm, D_MODEL), lambda i, j: (i, 0)),
                  pl.BlockSpec((D_MODEL, th), lambda i, j: (0, j)),
                  pl.BlockSpec((th, D_MODEL), lambda i, j: (j, 0))],
        out_specs=pl.BlockSpec((tm, D_MODEL), lambda i, j: (i, 0)),
        out_shape=jax.ShapeDtypeStruct((t, D_MODEL), F32),
        compiler_params=_params(("parallel", "arbitrary")),
        name="relu2_mlp",
    )(h2, x1, wu, wd)


def _pack_w_small(w_t):
    assert (O_GA, O_KI, O_Z, O_DT) == (M_GA, M_Z, O_WI + IDX_HEADS, O_Z + M_END - M_Z)
    return jnp.concatenate([w_t[O_KI:O_Z], w_t[O_DT:O_DT + SSM_HEADS],
                            jnp.zeros((S_END - S_DT - SSM_HEADS, w_t.shape[1]), w_t.dtype)], axis=0)


def _block(x2, batch, seq, norm1_g, w_in, conv_w, conv_b, dt_bias, a_log, d_skip, ssm_norm_g, q_norm_g,
           k_norm_g, rel_bias, w_att_branch, w_ssm_branch, w_out, norm2_g, w_up, w_down):
    row = lambda v: v.reshape(1, -1)
    w_t = w_in.T
    xn, small = _norm_small(x2, row(norm1_g), _pack_w_small(w_t), tm=min(1024, x2.shape[0]))
    main, (w_att_b, w_ssm_b, w_out_b, w_up_b, w_down_b) = _in_proj(
        xn, w_t, (w_att_branch, w_ssm_branch, w_out, w_up, w_down), tm=min(1024, x2.shape[0]))
    qt, kn, vt, qit, wt, ki = _qk_prep(main, small, row(q_norm_g), row(k_norm_g), batch, seq, tq=DSA_TQ)
    att = _dsa(qt, kn, vt, qit, wt, ki, rel_bias, batch, seq)
    y = _ssd(main, small, conv_w, row(conv_b), row(dt_bias), row(a_log), row(d_skip), row(ssm_norm_g),
             batch, seq)
    merged = _merge(att, y, main, w_att_b, w_ssm_b, tm=512, tn=1024)
    x1, h2 = _out_proj(merged, w_out_b, x2, row(norm2_g), tm=512)
    return _mlp(h2, x1, w_up_b, w_down_b, tm=512, th=1024)


def kernel(x, norm1_g, w_in, conv_w, conv_b, dt_bias, a_log, d_skip, ssm_norm_g, q_norm_g, k_norm_g, rel_bias,
           w_att_branch, w_ssm_branch, w_out, norm2_g, w_up, w_down):
    batch, seq, d = x.shape
    x2 = x.reshape(batch * seq, d)
    for l in range(norm1_g.shape[0]):
        x2 = _block(x2, batch, seq, norm1_g[l], w_in[l], conv_w[l], conv_b[l], dt_bias[l], a_log[l], d_skip[l],
                    ssm_norm_g[l], q_norm_g[l], k_norm_g[l], rel_bias, w_att_branch[l], w_ssm_branch[l],
                    w_out[l], norm2_g[l], w_up[l], w_down[l])
    return x2.reshape(batch, seq, d)
```

```python
import functools
import math

import numpy as np
import jax
import jax.numpy as jnp
from jax import lax
from jax.experimental import pallas as pl
from jax.experimental.pallas import tpu as pltpu

F32 = jnp.float32
BF16 = jnp.bfloat16
I32 = jnp.int32

D_MODEL = 2048
ATT_HEADS = 16
ATT_KV_HEADS = 4
HEAD_DIM = 128
REP = ATT_HEADS // ATT_KV_HEADS
IDX_HEADS = 16
IDX_DIM = 64
TOPK_MAX = 256
N_BUCKETS = 32
MAX_DISTANCE = 128
SSM_INNER = 2 * D_MODEL
SSM_HEAD_DIM = 64
SSM_HEADS = SSM_INNER // SSM_HEAD_DIM
SSM_GROUPS = 8
SSM_STATE = 128
CONV_WIDTH = 4
CHUNK = 128
MLP_HIDDEN = 4 * D_MODEL
EPS = 1e-6

ATT_Q = ATT_HEADS * HEAD_DIM
ATT_KV = ATT_KV_HEADS * HEAD_DIM
IDX_Q = IDX_HEADS * IDX_DIM
SSM_BC = SSM_GROUPS * SSM_STATE
CONV_DIM = SSM_INNER + 2 * SSM_BC
SPLITS = (D_MODEL, D_MODEL, ATT_Q, ATT_KV, ATT_KV, IDX_Q, IDX_DIM, IDX_HEADS, SSM_INNER, CONV_DIM, SSM_HEADS)
_OFFS = tuple(int(v) for v in np.cumsum((0,) + SPLITS))
(O_GA, O_GS, O_Q, O_K, O_V, O_QI, O_KI, O_WI, O_Z, O_XBC, O_DT, _O_END) = _OFFS

M_GA = 0
M_GS = M_GA + D_MODEL
M_Q = M_GS + D_MODEL
M_K = M_Q + ATT_Q
M_V = M_K + ATT_KV
M_QI = M_V + ATT_KV
M_Z = M_QI + IDX_Q
M_XBC = M_Z + SSM_INNER
M_END = M_XBC + CONV_DIM
S_KI = 0
S_WI = S_KI + IDX_DIM
S_DT = S_WI + IDX_HEADS
S_END = 256

HEADS_PER_GROUP = SSM_HEADS // SSM_GROUPS
GROUP_W = HEADS_PER_GROUP * SSM_HEAD_DIM

LANES = 128
VMEM_LIMIT = 56 * 1024 * 1024

NEG_INF = float("-inf")
INT_MIN = -(2 ** 31)
KEY_NEG_INF = int(np.int32(np.uint32(0xFF800000) ^ np.uint32(0x7FFFFFFF)))


def _dot(a, b):
    return jnp.dot(a, b, preferred_element_type=F32)


def _dot_nt(a, b):
    return lax.dot_general(a, b, (((1,), (1,)), ((), ())), preferred_element_type=F32)


def _split3(x):
    hi = x.astype(BF16)
    r = x - hi.astype(F32)
    mid = r.astype(BF16)
    lo = (r - mid.astype(F32)).astype(BF16)
    return hi, mid, lo


def _silu(x):
    h = 0.5 * x
    return h + h * jnp.tanh(h)


def _sigmoid(x):
    return 1.0 / (1.0 + jnp.exp(-x))


def _params(sem):
    return pltpu.CompilerParams(dimension_semantics=sem, vmem_limit_bytes=VMEM_LIMIT)


def _norm_small_kernel(x_ref, g_ref, w_ref, xn_ref, sm_ref):
    x = x_ref[...]
    ms = jnp.mean(x * x, axis=-1, keepdims=True)
    xn = (x * lax.rsqrt(ms + EPS) * g_ref[...]).astype(BF16)
    xn_ref[...] = xn
    sm_ref[...] = _dot_nt(xn, w_ref[...].astype(BF16))


def _norm_small(x, g, w_small_t, tm):
    m, d = x.shape
    n = w_small_t.shape[0]
    return pl.pallas_call(
        _norm_small_kernel,
        grid=(m // tm,),
        in_specs=[pl.BlockSpec((tm, d), lambda i: (i, 0)),
                  pl.BlockSpec((1, d), lambda i: (0, 0)),
                  pl.BlockSpec((n, d), lambda i: (0, 0))],
        out_specs=[pl.BlockSpec((tm, d), lambda i: (i, 0)),
                   pl.BlockSpec((tm, n), lambda i: (i, 0))],
        out_shape=[jax.ShapeDtypeStruct((m, d), BF16), jax.ShapeDtypeStruct((m, n), F32)],
        compiler_params=_params(("parallel",)),
        name="norm_small_proj",
    )(x, g, w_small_t)


IN_TN = 1024
IN_ALIGNED_TILES = O_KI // IN_TN
IN_SHIFT = O_Z - M_Z
IN_NEXT = 128
IN_ROWS = 256


def _in_proj_kernel(xn_ref, wa_ref, wn_ref, *rest, n_cast, n_side):
    side_in, o_ref, side_out, w_scr = rest[:n_side], rest[n_side], rest[n_side + 1:2 * n_side + 1], rest[-1]
    j = pl.program_id(0)
    i = pl.program_id(1)

    @pl.when(j * pl.num_programs(1) + i < n_cast)
    def _():
        for src, dst in zip(side_in, side_out):
            dst[...] = src[...].astype(BF16)

    @pl.when((i == 0) & (j < IN_ALIGNED_TILES))
    def _():
        for r0 in range(0, IN_TN, IN_ROWS):
            w_scr[r0:r0 + IN_ROWS, :] = wa_ref[r0:r0 + IN_ROWS, :].astype(BF16)

    @pl.when((i == 0) & (j >= IN_ALIGNED_TILES))
    def _():
        for r0 in range(0, IN_TN - IN_SHIFT, IN_ROWS):
            r1 = min(r0 + IN_ROWS, IN_TN - IN_SHIFT)
            w_scr[r0:r1, :] = wa_ref[r0 + IN_SHIFT:r1 + IN_SHIFT, :].astype(BF16)
        w_scr[IN_TN - IN_SHIFT:IN_TN, :] = wn_ref[0:IN_SHIFT, :].astype(BF16)

    o_ref[...] = _dot_nt(xn_ref[...], w_scr[...]).astype(o_ref.dtype)


def _in_proj(xn, w_t, side_weights, tm):
    m, d = xn.shape
    assert O_KI % IN_TN == 0 and M_END % IN_TN == 0 and IN_TN % IN_NEXT == 0
    assert 0 < IN_SHIFT <= IN_NEXT and IN_SHIFT % 16 == 0
    nj, ni = M_END // IN_TN, m // tm
    n_cast = 1 << ((nj * ni).bit_length() - 1)
    side_specs = []
    for w in side_weights:
        rows = w.shape[0] // n_cast
        assert w.shape[0] % n_cast == 0 and rows % 16 == 0
        side_specs.append(pl.BlockSpec((rows, w.shape[1]), lambda j, i: (jnp.minimum(j * ni + i, n_cast - 1), 0)))
    kern = functools.partial(_in_proj_kernel, n_cast=n_cast, n_side=len(side_weights))
    outs = pl.pallas_call(
        kern,
        grid=(nj, ni),
        in_specs=[pl.BlockSpec((tm, d), lambda j, i: (i, 0)),
                  pl.BlockSpec((IN_TN, d), lambda j, i: (j, 0)),
                  pl.BlockSpec((IN_NEXT, d), lambda j, i: ((j + 1) * (IN_TN // IN_NEXT), 0))] + side_specs,
        out_specs=[pl.BlockSpec((tm, IN_TN), lambda j, i: (i, j))] + side_specs,
        out_shape=[jax.ShapeDtypeStruct((m, M_END), BF16)]
                  + [jax.ShapeDtypeStruct(w.shape, BF16) for w in side_weights],
        scratch_shapes=[pltpu.VMEM((IN_TN, d), BF16)],
        compiler_params=_params(("arbitrary", "arbitrary")),
        name="in_proj",
    )(xn, w_t, w_t, *side_weights)
    return outs[0], outs[1:]


LOG2E = math.log2(math.e)
VT_ROWS = HEAD_DIM + 16


def _qk_prep_kernel(q_ref, k_ref, v_ref, qi_ref, sm_ref, qg_ref, kg_ref,
                    qt_ref, kn_ref, vt_ref, qit_ref, wt_ref, ki_ref):
    qg = qg_ref[...]
    for h in range(ATT_HEADS):
        x = q_ref[:, h * HEAD_DIM:(h + 1) * HEAD_DIM].astype(F32)
        ms = jnp.mean(x * x, axis=-1, keepdims=True)
        y = x * lax.rsqrt(ms + EPS) * qg * (HEAD_DIM ** -0.5 * LOG2E)
        qt_ref[0, h] = y.T.astype(BF16)
    kg = kg_ref[...]
    for h in range(ATT_KV_HEADS):
        x = k_ref[:, h * HEAD_DIM:(h + 1) * HEAD_DIM].astype(F32)
        ms = jnp.mean(x * x, axis=-1, keepdims=True)
        kn_ref[:, h * HEAD_DIM:(h + 1) * HEAD_DIM] = (x * lax.rsqrt(ms + EPS) * kg).astype(BF16)
        vt_ref[0, h, 0, 0:HEAD_DIM, :] = v_ref[:, h * HEAD_DIM:(h + 1) * HEAD_DIM].astype(F32).T.astype(BF16)
        vt_ref[0, h, 0, HEAD_DIM:VT_ROWS, :] = jnp.ones((VT_ROWS - HEAD_DIM, v_ref.shape[0]), BF16)
    for p in range(IDX_Q // LANES):
        qit_ref[0, p * LANES:(p + 1) * LANES, :] = qi_ref[:, p * LANES:(p + 1) * LANES].astype(F32).T.astype(BF16)
    sm_t = sm_ref[:, 0:LANES].T
    wt_ref[0] = sm_t[S_WI:S_WI + IDX_HEADS, :] * (IDX_HEADS ** -0.5 * IDX_DIM ** -0.5)
    ki_ref[...] = sm_ref[:, S_KI:S_KI + IDX_DIM].astype(BF16)


def _qk_prep(main, small, qg, kg, batch, seq, tq):
    t = main.shape[0]
    nq = seq // tq
    return pl.pallas_call(
        _qk_prep_kernel,
        grid=(batch, nq),
        in_specs=[pl.BlockSpec((tq, ATT_Q), lambda b, i: (b * nq + i, M_Q // ATT_Q)),
                  pl.BlockSpec((tq, ATT_KV), lambda b, i: (b * nq + i, M_K // ATT_KV)),
                  pl.BlockSpec((tq, ATT_KV), lambda b, i: (b * nq + i, M_V // ATT_KV)),
                  pl.BlockSpec((tq, IDX_Q), lambda b, i: (b * nq + i, M_QI // IDX_Q)),
                  pl.BlockSpec((tq, S_END), lambda b, i: (b * nq + i, 0)),
                  pl.BlockSpec((1, HEAD_DIM), lambda b, i: (0, 0)),
                  pl.BlockSpec((1, HEAD_DIM), lambda b, i: (0, 0))],
        out_specs=[pl.BlockSpec((1, ATT_HEADS, HEAD_DIM, tq), lambda b, i: (b * nq + i, 0, 0, 0)),
                   pl.BlockSpec((tq, ATT_KV), lambda b, i: (b * nq + i, 0)),
                   pl.BlockSpec((1, ATT_KV_HEADS, 1, VT_ROWS, tq), lambda b, i: (b, 0, i, 0, 0)),
                   pl.BlockSpec((1, IDX_Q, tq), lambda b, i: (b * nq + i, 0, 0)),
                   pl.BlockSpec((1, IDX_HEADS, tq), lambda b, i: (b * nq + i, 0, 0)),
                   pl.BlockSpec((tq, IDX_DIM), lambda b, i: (b * nq + i, 0))],
        out_shape=[jax.ShapeDtypeStruct((t // tq, ATT_HEADS, HEAD_DIM, tq), BF16),
                   jax.ShapeDtypeStruct((t, ATT_KV), BF16),
                   jax.ShapeDtypeStruct((batch, ATT_KV_HEADS, nq, VT_ROWS, tq), BF16),
                   jax.ShapeDtypeStruct((t // tq, IDX_Q, tq), BF16),
                   jax.ShapeDtypeStruct((t // tq, IDX_HEADS, tq), F32),
                   jax.ShapeDtypeStruct((t, IDX_DIM), BF16)],
        compiler_params=_params(("parallel", "parallel")),
        name="qk_prep",
    )(main, main, main, main, small, qg, kg)


DSA_TQ = 256
DSA_TK = 256
CNT_WAYS = 4


def _t5_bucket_np(dist):
    n = np.maximum(dist, 0)
    max_exact = N_BUCKETS // 2
    nf = np.maximum(n, 1).astype(np.float32)
    ratio = (np.log(nf / np.float32(max_exact)) / np.float32(math.log(MAX_DISTANCE / max_exact))
             * np.float32(N_BUCKETS - max_exact))
    large = max_exact + ratio.astype(np.int32)
    large = np.minimum(large, N_BUCKETS - 1)
    return np.where(n < max_exact, n, large).astype(np.int32)


def _bias_bucket_tiles(tq, tk):
    r = np.arange(tq)[None, :]
    c = np.arange(tk)[:, None]
    d0 = _t5_bucket_np(r - c)
    d1 = _t5_bucket_np(tk + r - c)
    assert np.all(_t5_bucket_np(np.arange(tk + 1, 8 * tk)) == N_BUCKETS - 1)
    return np.stack([d0, d1]).astype(np.int32)


def _dsa_kernel(relb_ref, bidx_ref, qt_ref, qit_ref, wt_ref, k_ref, vt_ref, ki_ref, o_ref,
                key_ref, plane_ref, madd_ref, bias_ref, s_buf, p_buf, st_ref, acc_ref, *, tq, tk, nkc, topk):
    b = pl.program_id(0)
    i = pl.program_id(1)
    neg_slot = nkc

    @pl.when((b == 0) & (i == 0))
    def _init():
        madd_ref[neg_slot] = jnp.full((tk, tq), NEG_INF, F32)
        for t in range(2):
            bt = bidx_ref[t]

            def head_body(h, carry):
                far = relb_ref[N_BUCKETS - 1, h]

                def bucket_body(bk, acc):
                    return jnp.where(bt == bk, (relb_ref[bk, h] - far) * LOG2E, acc)

                bias_ref[t, h] = lax.fori_loop(0, N_BUCKETS, bucket_body, jnp.zeros((tk, tq), F32))
                return carry

            lax.fori_loop(0, ATT_HEADS, head_body, 0)

    def _select():
        nj = i + 1
        qpos = i * tq + lax.broadcasted_iota(I32, (tk, tq), 1)

        def score_chunk(j):
            kc = ki_ref[pl.ds(pl.multiple_of(j * tk, tk), tk), :]
            acc = jnp.zeros((tk, tq), F32)
            for h in range(IDX_HEADS):
                z = _dot(kc, qit_ref[0, h * IDX_DIM:(h + 1) * IDX_DIM, :])
                acc = acc + jnp.maximum(z, 0.0) * wt_ref[0, h:h + 1, :]
            kpos = j * tk + lax.broadcasted_iota(I32, (tk, tq), 0)
            acc = jnp.where(kpos <= qpos, acc, NEG_INF)
            bits = pltpu.bitcast(acc, I32)
            bits = jnp.where(bits == INT_MIN, 0, bits)
            key = bits ^ ((bits >> 31) & 0x7FFFFFFF)
            key_ref[j] = key
            plane_ref[0, j] = pltpu.bitcast(bits & jnp.int32(-65536), F32).astype(BF16)
            plane_ref[1, j] = ((key >> 8) & 0xFF).astype(F32).astype(BF16)
            plane_ref[2, j] = (key & 0xFF).astype(F32).astype(BF16)

        def score_quad(u, carry):
            for v in range(4):
                score_chunk(4 * u + v)
            return carry

        lax.fori_loop(0, nj // 4, score_quad, 0)
        done = (nj // 4) * 4

        @pl.when(nj % 4 >= 2)
        def _():
            score_chunk(done)
            score_chunk(done + 1)

        @pl.when(nj % 2 == 1)
        def _():
            score_chunk(nj - 1)

        kf = float(topk)
        one_b = jnp.ones((tk, tq), BF16)
        zero_b = jnp.zeros((tk, tq), BF16)
        grp = 16 * CNT_WAYS

        def count(plane, cand_b, strict):
            def body(j, cnt):
                pv = plane_ref[plane, j]
                hit = jnp.where(pv > cand_b if strict else pv >= cand_b, one_b, zero_b)
                for r0 in range(0, tk, grp):
                    cnt = cnt + hit[r0:r0 + grp]
                return cnt

            cnt = lax.fori_loop(0, nj, body, jnp.zeros((grp, tq), BF16))
            return jnp.sum(cnt.astype(F32), axis=0, keepdims=True)

        def keep_ties(src, dst, tie_b):
            def body(j, carry):
                plane_ref[dst, j] = jnp.where(plane_ref[src, j] == tie_b, plane_ref[dst, j], -one_b)
                return carry

            lax.fori_loop(0, nj, body, 0)

        def top_digit_float(p16):
            fb = (p16 & 0xFFFF) ^ jnp.where(p16 < 0, 0x7FFF, 0)
            return pltpu.bitcast(lax.shift_left(fb, 16), F32).astype(BF16)

        def top_iter(it, prefix):
            cand = prefix + lax.shift_left(jnp.int32(1), 15 - it)
            return jnp.where(count(0, top_digit_float(cand), False) >= kf, cand, prefix)

        d_top = lax.fori_loop(0, 16, top_iter, jnp.full((1, tq), -32768, I32))
        t_top = top_digit_float(d_top)
        above = count(0, t_top, True)

        def byte_digit(plane, above_n):
            def it_body(it, prefix):
                cand = prefix + lax.shift_left(jnp.int32(1), 7 - it)
                tot = above_n + count(plane, cand.astype(F32).astype(BF16), False)
                return jnp.where(tot >= kf, cand, prefix)

            return lax.fori_loop(0, 8, it_body, jnp.zeros((1, tq), I32))

        keep_ties(0, 1, t_top)
        d_mid = byte_digit(1, above)
        t_mid = d_mid.astype(F32).astype(BF16)
        above = above + count(1, t_mid, True)
        keep_ties(1, 2, t_mid)
        d_low = byte_digit(2, above)
        t_low = d_low.astype(F32).astype(BF16)
        thr = lax.shift_left(d_top, 16) | lax.shift_left(d_mid, 8) | d_low
        n_ge = above + count(2, t_low, False)
        tied = jnp.max(n_ge) > kf

        @pl.when(jnp.logical_not(tied))
        def _():
            thr_c = jnp.maximum(thr, KEY_NEG_INF + 1)

            def madd_chunk(j, carry):
                madd_ref[j] = jnp.where(key_ref[j] >= thr_c, 0.0, NEG_INF)
                return carry

            lax.fori_loop(0, nj, madd_chunk, 0)

        @pl.when(tied)
        def _():
            need = kf - (above + count(2, t_low, True))
            row = lax.broadcasted_iota(I32, (tk, tq), 0)

            def pos_iter(it, q_pos):
                cand = q_pos + lax.shift_left(jnp.int32(1), (nkc * tk).bit_length() - 2 - it)

                def body(j, cnt):
                    hit = jnp.where(key_ref[j] == thr, jnp.where(row + j * tk < cand, 1.0, 0.0), 0.0)
                    return cnt + jnp.sum(hit.reshape(tk // 8, 8, tq), axis=0)

                cnt = lax.fori_loop(0, nj, body, jnp.zeros((8, tq), F32))
                return jnp.where(jnp.sum(cnt, axis=0, keepdims=True) < need, cand, q_pos)

            last = lax.fori_loop(0, (nkc * tk).bit_length() - 1, pos_iter, jnp.zeros((1, tq), I32))
            thr_c = jnp.maximum(thr, KEY_NEG_INF)

            def madd_chunk(j, carry):
                kk = key_ref[j]
                take_tie = jnp.where(row + j * tk <= last, jnp.where(kk > KEY_NEG_INF, 0.0, NEG_INF), NEG_INF)
                madd_ref[j] = jnp.where(kk > thr_c, 0.0, jnp.where(kk == thr, take_tie, NEG_INF))
                return carry

            lax.fori_loop(0, nj, madd_chunk, 0)

    _select()

    ng = ATT_KV_HEADS
    c0 = jnp.maximum(i - 1, 0)
    c1 = jnp.minimum(c0 + 1, nkc - 1)
    n_far = (c0 + 1) // 2

    def scores_to(slot, g, ca, madd_c, bias_c):
        rows = pl.ds(pl.multiple_of(ca * tk, tk), 2 * tk)
        qt = jnp.concatenate([qt_ref[0, g * REP + r] for r in range(REP)], axis=1)
        add = jnp.concatenate([madd_c if bias_c is None else madd_c + bias_c[r] for r in range(REP)], axis=1)
        s = _dot(k_ref[rows, g * HEAD_DIM:(g + 1) * HEAD_DIM], qt) + add
        s_buf[slot] = s
        st_ref[ng + 2 + slot:ng + 3 + slot, :] = jnp.max(s, axis=0, keepdims=True)

    def softmax_to(slot, g):
        m = st_ref[g:g + 1, :]
        m_new = jnp.maximum(m, st_ref[ng + 2 + slot:ng + 3 + slot, :])
        m_safe = jnp.where(m_new == NEG_INF, 0.0, m_new)
        alpha = jnp.exp2(m - m_safe)
        p_buf[slot] = jnp.exp2((s_buf[slot] - m_safe).astype(BF16))
        st_ref[g:g + 1, :] = m_new
        st_ref[ng + slot:ng + 1 + slot, :] = alpha

    def values_from(slot, g, ca, cb):
        vt_c = jnp.concatenate([vt_ref[0, g, ca], vt_ref[0, g, cb]], axis=1)
        acc_ref[g] = st_ref[ng + slot:ng + 1 + slot, :] * acc_ref[g] + _dot(vt_c, p_buf[slot])

    def near_scores_to(slot, g):
        first = i == 0
        t_a = jnp.where(first, 0, 1)
        idx_b = jnp.where(first, neg_slot, c0 + 1)
        madd_near = jnp.concatenate([madd_ref[c0], madd_ref[idx_b]], axis=0)
        bias_near = [jnp.concatenate([bias_ref[t_a, g * REP + r], bias_ref[0, g * REP + r]], axis=0)
                     for r in range(REP)]
        scores_to(slot, g, c0, madd_near, bias_near)

    def far_scores_to(slot, g, k):
        ca = jnp.minimum(2 * (k - 1), nkc - 2)
        cb = jnp.where(ca + 1 >= c0, neg_slot, ca + 1)
        scores_to(slot, g, ca, jnp.concatenate([madd_ref[ca], madd_ref[cb]], axis=0), None)

    st_ref[0:ng, :] = jnp.full((ng, REP * tq), NEG_INF, F32)
    acc_ref[...] = jnp.zeros_like(acc_ref)

    near_scores_to(0, 0)
    for g in range(ng):
        if g + 1 < ng:
            near_scores_to((g + 1) % 2, g + 1)
        else:
            far_scores_to(0, 0, 1)
        softmax_to(g % 2, g)
        values_from(g % 2, g, c0, c1)

    def far_step(k, carry):
        for g in range(ng):
            if g + 1 < ng:
                far_scores_to((g + 1) % 2, g + 1, k)
            else:
                far_scores_to(0, 0, k + 1)
            softmax_to(g % 2, g)
            values_from(g % 2, g, 2 * (k - 1), 2 * (k - 1) + 1)
        return carry

    lax.fori_loop(1, n_far + 1, far_step, 0)

    for g in range(ng):
        out = acc_ref[g, 0:HEAD_DIM, :] / acc_ref[g, HEAD_DIM:HEAD_DIM + 1, :]
        for r in range(REP):
            h = g * REP + r
            o_ref[:, h * HEAD_DIM:(h + 1) * HEAD_DIM] = out[:, r * tq:(r + 1) * tq].T.astype(o_ref.dtype)


def _dsa(qt, kn, vt, qit, wt, ki, rel_bias, batch, seq):
    tq, tk = DSA_TQ, DSA_TK
    nq = seq // tq
    nkc = seq // tk
    topk = min(TOPK_MAX, seq // 4)
    bidx = jnp.asarray(_bias_bucket_tiles(tq, tk))
    kern = functools.partial(_dsa_kernel, tq=tq, tk=tk, nkc=nkc, topk=topk)
    once = pl.Buffered(1)
    return pl.pallas_call(
        kern,
        grid=(batch, nq),
        in_specs=[pl.BlockSpec(memory_space=pltpu.SMEM),
                  pl.BlockSpec((2, tk, tq), lambda b, i: (0, 0, 0), pipeline_mode=once),
                  pl.BlockSpec((1, ATT_HEADS, HEAD_DIM, tq), lambda b, i: (b * nq + i, 0, 0, 0)),
                  pl.BlockSpec((1, IDX_Q, tq), lambda b, i: (b * nq + i, 0, 0)),
                  pl.BlockSpec((1, IDX_HEADS, tq), lambda b, i: (b * nq + i, 0, 0)),
                  pl.BlockSpec((seq, ATT_KV), lambda b, i: (b, 0), pipeline_mode=once),
                  pl.BlockSpec((1, ATT_KV_HEADS, nkc, VT_ROWS, tk), lambda b, i: (b, 0, 0, 0, 0),
                               pipeline_mode=once),
                  pl.BlockSpec((seq, IDX_DIM), lambda b, i: (b, 0), pipeline_mode=once)],
        out_specs=pl.BlockSpec((tq, ATT_Q), lambda b, i: (b * nq + i, 0)),
        out_shape=jax.ShapeDtypeStruct((batch * seq, ATT_Q), BF16),
        scratch_shapes=[pltpu.VMEM((nkc, tk, tq), I32),
                        pltpu.VMEM((3, nkc, tk, tq), BF16),
                        pltpu.VMEM((nkc + 1, tk, tq), F32),
                        pltpu.VMEM((2, ATT_HEADS, tk, tq), F32),
                        pltpu.VMEM((2, 2 * tk, REP * tq), F32),
                        pltpu.VMEM((2, 2 * tk, REP * tq), BF16),
                        pltpu.VMEM((ATT_KV_HEADS + 4, REP * tq), F32),
                        pltpu.VMEM((ATT_KV_HEADS, VT_ROWS, REP * tq), F32)],
        compiler_params=_params(("arbitrary", "arbitrary")),
        name="dsa_attention",
    )(rel_bias, bidx, qt, qit, wt, kn, vt, ki)


E_ROWS = 3 * CHUNK + 16
CONV_HALO = 16


def _conv_shift_matrix():
    ext = CONV_HALO + CHUNK
    s = np.zeros((CHUNK, CONV_WIDTH * ext), np.float32)
    for k in range(CONV_WIDTH):
        t = np.arange(CHUNK)
        s[t, k * ext + CONV_HALO + t - (CONV_WIDTH - 1) + k] = 1.0
    return s


def _ssd_kernel(xbc_ref, halo_ref, z_ref, sm_ref, shift_ref, cw_ref, cb_ref, dtb_ref, alog_ref, dsk_ref, ng_ref,
                o_ref, xs_ref, bm_ref, cm_ref, state_ref, ypre_ref, acg_ref, actg_ref):
    c = pl.program_id(1)
    L = CHUNK

    @pl.when(c == 0)
    def _():
        state_ref[...] = jnp.zeros_like(state_ref)

    halo_on = (c > 0).astype(BF16)
    shift = shift_ref[...]
    cblk = GROUP_W
    for cbi in range(CONV_DIM // cblk):
        cols = slice(cbi * cblk, (cbi + 1) * cblk)
        ext = jnp.concatenate([halo_ref[:, cols] * halo_on, xbc_ref[:, cols]], axis=0)
        wtap = cw_ref[:, cols].astype(BF16)
        prods = jnp.concatenate([ext * wtap[kk:kk + 1, :] for kk in range(CONV_WIDTH)], axis=0)
        y = _silu(cb_ref[:, cols] + _dot(shift, prods))
        if cbi < SSM_GROUPS:
            xs_ref[cbi] = y
        else:
            per = cblk // SSM_STATE
            for u in range(per):
                gi = (cbi - SSM_GROUPS) * per + u
                piece = y[:, u * SSM_STATE:(u + 1) * SSM_STATE]
                if gi < SSM_GROUPS:
                    bm_ref[gi] = piece
                else:
                    cm_ref[gi - SSM_GROUPS] = piece

    dt_in = sm_ref[:, S_DT:S_DT + SSM_HEADS] + dtb_ref[...]
    dt_act = jnp.maximum(dt_in, 0.0) + jnp.log1p(jnp.exp(-jnp.abs(dt_in)))
    a = dt_act * (-jnp.exp(alog_ref[...]))
    ri = lax.broadcasted_iota(I32, (L, L), 0)
    ci = lax.broadcasted_iota(I32, (L, L), 1)
    tril = ri >= ci
    tri_b = jnp.where(tril, 1.0, 0.0).astype(BF16)
    a_cum = sum(_dot(tri_b, p) for p in _split3(a))
    eye_b = jnp.where(lax.broadcasted_iota(I32, (SSM_HEADS, SSM_HEADS), 0)
                      == lax.broadcasted_iota(I32, (SSM_HEADS, SSM_HEADS), 1), 1.0, 0.0).astype(BF16)
    a_cum_t = sum(_dot_nt(eye_b, p) for p in _split3(a_cum))
    a_last = a_cum[L - 1:L, :]
    for gi in range(SSM_GROUPS):
        acg_ref[gi] = a_cum[:, gi * HEADS_PER_GROUP:(gi + 1) * HEADS_PER_GROUP]
        actg_ref[gi] = a_cum_t[gi * HEADS_PER_GROUP:(gi + 1) * HEADS_PER_GROUP, :]
    cd3 = _split3(jnp.exp(a_last))
    ds3 = _split3(dsk_ref[...])
    extras = jnp.concatenate([p.astype(F32) for p in cd3 + ds3]
                             + [jnp.zeros((E_ROWS - 3 * L - 6, SSM_HEADS), F32)], axis=0)
    e_mat = jnp.concatenate([dt_act, jnp.exp(a_cum), jnp.exp(a_last - a_cum), extras], axis=0).astype(BF16)

    lane = lax.broadcasted_iota(I32, (L, LANES), 1)
    lo_mask = lane < SSM_HEAD_DIM

    def group_body(gi, carry):
        xs = xs_ref[gi]
        bg = bm_ref[gi]
        cg_b = cm_ref[gi].astype(BF16)
        hsel = (lax.broadcasted_iota(I32, (SSM_HEADS, GROUP_W), 0)
                == gi * HEADS_PER_GROUP + lax.broadcasted_iota(I32, (SSM_HEADS, GROUP_W), 1) // SSM_HEAD_DIM)
        ex = _dot(e_mat, jnp.where(hsel, 1.0, 0.0).astype(BF16))
        dt_rep = ex[0:L]
        expa_rep = ex[L:2 * L]
        dte_rep = ex[2 * L:3 * L]
        cd_rep = ex[3 * L:3 * L + 1] + ex[3 * L + 1:3 * L + 2] + ex[3 * L + 2:3 * L + 3]
        dsk_rep = ex[3 * L + 3:3 * L + 4] + ex[3 * L + 4:3 * L + 5] + ex[3 * L + 5:3 * L + 6]

        xd = xs * dt_rep
        xd_b = xd.astype(BF16)
        cb = _dot_nt(cg_b, bg.astype(BF16))
        acg = acg_ref[gi]
        actg = actg_ref[gi]
        pairs = []
        for pj in range(HEADS_PER_GROUP // 2):
            gmat = []
            for e in (2 * pj, 2 * pj + 1):
                seg = acg[:, e:e + 1] - actg[e:e + 1, :]
                dec = jnp.exp(jnp.where(tril, seg, NEG_INF))
                gmat.append((cb * dec).astype(BF16))
            xp = xd_b[:, pj * LANES:(pj + 1) * LANES]
            zero = jnp.zeros_like(xp)
            pairs.append(_dot(gmat[0], jnp.where(lo_mask, xp, zero))
                         + _dot(gmat[1], jnp.where(lo_mask, zero, xp)))
        y_diag = jnp.concatenate(pairs, axis=1)

        st = state_ref[gi]
        y_off = _dot(cg_b, st.astype(BF16)) * expa_rep
        xdd = (xd * dte_rep).astype(BF16)
        state_ref[gi] = st * cd_rep + _dot(bg.T.astype(BF16), xdd)
        ypre_ref[gi] = y_diag + y_off + dsk_rep * xs
        return carry

    lax.fori_loop(0, SSM_GROUPS, group_body, 0, unroll=8)

    for gi in range(SSM_GROUPS):
        cols = slice(gi * GROUP_W, (gi + 1) * GROUP_W)
        zz = z_ref[:, cols].astype(F32)
        y = ypre_ref[gi] * _silu(zz)
        ms = jnp.mean(y * y, axis=-1, keepdims=True)
        o_ref[:, cols] = (y * lax.rsqrt(ms + EPS) * ng_ref[:, cols]).astype(o_ref.dtype)


def _ssd(main, small, conv_w, conv_b, dt_bias, a_log, d_skip, norm_g, batch, seq):
    nc = seq // CHUNK
    hb = CHUNK // CONV_HALO
    row = lambda b, c: b * nc + c
    full = lambda shape: pl.BlockSpec(shape, lambda b, c: (0,) * len(shape))
    shift = jnp.asarray(_conv_shift_matrix(), BF16)
    return pl.pallas_call(
        _ssd_kernel,
        grid=(batch, nc),
        in_specs=[pl.BlockSpec((CHUNK, CONV_DIM), lambda b, c: (row(b, c), M_XBC // CONV_DIM)),
                  pl.BlockSpec((CONV_HALO, CONV_DIM),
                               lambda b, c: (jnp.maximum(row(b, c) * hb - 1, 0), M_XBC // CONV_DIM)),
                  pl.BlockSpec((CHUNK, SSM_INNER), lambda b, c: (row(b, c), M_Z // SSM_INNER)),
                  pl.BlockSpec((CHUNK, S_END), lambda b, c: (row(b, c), 0)),
                  full((CHUNK, CONV_WIDTH * (CONV_HALO + CHUNK))),
                  full((CONV_WIDTH, CONV_DIM)), full((1, CONV_DIM)), full((1, SSM_HEADS)),
                  full((1, SSM_HEADS)), full((1, SSM_HEADS)), full((1, SSM_INNER))],
        out_specs=pl.BlockSpec((CHUNK, SSM_INNER), lambda b, c: (row(b, c), 0)),
        out_shape=jax.ShapeDtypeStruct((batch * seq, SSM_INNER), BF16),
        scratch_shapes=[pltpu.VMEM((SSM_GROUPS, CHUNK, GROUP_W), F32),
                        pltpu.VMEM((SSM_GROUPS, CHUNK, SSM_STATE), F32),
                        pltpu.VMEM((SSM_GROUPS, CHUNK, SSM_STATE), F32),
                        pltpu.VMEM((SSM_GROUPS, SSM_STATE, GROUP_W), F32),
                        pltpu.VMEM((SSM_GROUPS, CHUNK, GROUP_W), F32),
                        pltpu.VMEM((SSM_GROUPS, CHUNK, HEADS_PER_GROUP), F32),
                        pltpu.VMEM((SSM_GROUPS, HEADS_PER_GROUP, CHUNK), F32)],
        compiler_params=_params(("arbitrary", "arbitrary")),
        name="ssd_scan",
    )(main, main, main, small, shift, conv_w, conv_b, dt_bias, a_log, d_skip, norm_g)


def _merge_kernel(att_ref, y_ref, ga_ref, gs_ref, wa_ref, ws_ref, o_ref):
    pa = _dot(att_ref[...], wa_ref[...])
    ps = _dot(y_ref[...], ws_ref[...])
    o_ref[...] = (_sigmoid(ga_ref[...].astype(F32)) * pa + _sigmoid(gs_ref[...].astype(F32)) * ps).astype(o_ref.dtype)


def _merge(att, y, main, wa, ws, tm, tn):
    t = att.shape[0]
    return pl.pallas_call(
        _merge_kernel,
        grid=(t // tm, D_MODEL // tn),
        in_specs=[pl.BlockSpec((tm, ATT_Q), lambda i, j: (i, 0)),
                  pl.BlockSpec((tm, SSM_INNER), lambda i, j: (i, 0)),
                  pl.BlockSpec((tm, tn), lambda i, j: (i, M_GA // tn + j)),
                  pl.BlockSpec((tm, tn), lambda i, j: (i, M_GS // tn + j)),
                  pl.BlockSpec((ATT_Q, tn), lambda i, j: (0, j)),
                  pl.BlockSpec((SSM_INNER, tn), lambda i, j: (0, j))],
        out_specs=pl.BlockSpec((tm, tn), lambda i, j: (i, j)),
        out_shape=jax.ShapeDtypeStruct((t, D_MODEL), BF16),
        compiler_params=_params(("parallel", "arbitrary")),
        name="gated_merge",
    )(att, y, main, main, wa, ws)


def _out_proj_kernel(m_ref, w_ref, x_ref, g_ref, x1_ref, h2_ref):
    x1 = x_ref[...] + _dot(m_ref[...], w_ref[...])
    x1_ref[...] = x1
    ms = jnp.mean(x1 * x1, axis=-1, keepdims=True)
    h2_ref[...] = (x1 * lax.rsqrt(ms + EPS) * g_ref[...]).astype(BF16)


def _out_proj(merged, w, x, g, tm):
    t = x.shape[0]
    return pl.pallas_call(
        _out_proj_kernel,
        grid=(t // tm,),
        in_specs=[pl.BlockSpec((tm, D_MODEL), lambda i: (i, 0)),
                  pl.BlockSpec((D_MODEL, D_MODEL), lambda i: (0, 0)),
                  pl.BlockSpec((tm, D_MODEL), lambda i: (i, 0)),
                  pl.BlockSpec((1, D_MODEL), lambda i: (0, 0))],
        out_specs=[pl.BlockSpec((tm, D_MODEL), lambda i: (i, 0)),
                   pl.BlockSpec((tm, D_MODEL), lambda i: (i, 0))],
        out_shape=[jax.ShapeDtypeStruct((t, D_MODEL), F32),
                   jax.ShapeDtypeStruct((t, D_MODEL), BF16)],
        compiler_params=_params(("parallel",)),
        name="out_proj_norm",
    )(merged, w, x, g)


def _mlp_kernel(h_ref, x1_ref, wu_ref, wd_ref, o_ref):
    @pl.when(pl.program_id(1) == 0)
    def _():
        o_ref[...] = x1_ref[...]

    u = _dot(h_ref[...], wu_ref[...])
    u = jnp.square(jnp.maximum(u, 0.0)).astype(BF16)
    o_ref[...] += _dot(u, wd_ref[...])


def _mlp(h2, x1, wu, wd, tm, th):
    t = h2.shape[0]
    return pl.pallas_call(
        _mlp_kernel,
        grid=(t // tm, MLP_HIDDEN // th),
        in_specs=[pl.BlockSpec((tm, D_MODEL), lambda i, j: (i, 0)),
                  pl.BlockSpec((tm, D_MODEL), lambda i, j: (i, 0)),
                  pl.BlockSpec((D_MODEL, th), lambda i, j: (0, j)),
                  pl.BlockSpec((th, D_MODEL), lambda i, j: (j, 0))],
        out_specs=pl.BlockSpec((tm, D_MODEL), lambda i, j: (i, 0)),
        out_shape=jax.ShapeDtypeStruct((t, D_MODEL), F32),
        compiler_params=_params(("parallel", "arbitrary")),
        name="relu2_mlp",
    )(h2, x1, wu, wd)


def _pack_w_small(w_t):
    assert (O_GA, O_KI, O_Z, O_DT) == (M_GA, M_Z, O_WI + IDX_HEADS, O_Z + M_END - M_Z)
    return jnp.concatenate([w_t[O_KI:O_Z], w_t[O_DT:O_DT + SSM_HEADS],
                            jnp.zeros((S_END - S_DT - SSM_HEADS, w_t.shape[1]), w_t.dtype)], axis=0)


def _block(x2, batch, seq, norm1_g, w_in, conv_w, conv_b, dt_bias, a_log, d_skip, ssm_norm_g, q_norm_g,
           k_norm_g, rel_bias, w_att_branch, w_ssm_branch, w_out, norm2_g, w_up, w_down):
    row = lambda v: v.reshape(1, -1)
    w_t = w_in.T
    xn, small = _norm_small(x2, row(norm1_g), _pack_w_small(w_t), tm=min(1024, x2.shape[0]))
    main, (w_att_b, w_ssm_b, w_out_b, w_up_b, w_down_b) = _in_proj(
        xn, w_t, (w_att_branch, w_ssm_branch, w_out, w_up, w_down), tm=min(1024, x2.shape[0]))
    qt, kn, vt, qit, wt, ki = _qk_prep(main, small, row(q_norm_g), row(k_norm_g), batch, seq, tq=DSA_TQ)
    att = _dsa(qt, kn, vt, qit, wt, ki, rel_bias, batch, seq)
    y = _ssd(main, small, conv_w, row(conv_b), row(dt_bias), row(a_log), row(d_skip), row(ssm_norm_g),
             batch, seq)
    merged = _merge(att, y, main, w_att_b, w_ssm_b, tm=512, tn=1024)
    x1, h2 = _out_proj(merged, w_out_b, x2, row(norm2_g), tm=512)
    return _mlp(h2, x1, w_up_b, w_down_b, tm=512, th=1024)


def kernel(x, norm1_g, w_in, conv_w, conv_b, dt_bias, a_log, d_skip, ssm_norm_g, q_norm_g, k_norm_g, rel_bias,
           w_att_branch, w_ssm_branch, w_out, norm2_g, w_up, w_down):
    batch, seq, d = x.shape
    x2 = x.reshape(batch * seq, d)
    for l in range(norm1_g.shape[0]):
        x2 = _block(x2, batch, seq, norm1_g[l], w_in[l], conv_w[l], conv_b[l], dt_bias[l], a_log[l], d_skip[l],
                    ssm_norm_g[l], q_norm_g[l], k_norm_g[l], rel_bias, w_att_branch[l], w_ssm_branch[l],
                    w_out[l], norm2_g[l], w_up[l], w_down[l])
    return x2.reshape(batch, seq, d)
```

```python
import functools
import math

import numpy as np
import jax
import jax.numpy as jnp
from jax import lax
from jax.experimental import pallas as pl
from jax.experimental.pallas import tpu as pltpu

F32 = jnp.float32
BF16 = jnp.bfloat16
I32 = jnp.int32

D_MODEL = 2048
ATT_HEADS = 16
ATT_KV_HEADS = 4
HEAD_DIM = 128
REP = ATT_HEADS // ATT_KV_HEADS
IDX_HEADS = 16
IDX_DIM = 64
TOPK_MAX = 256
N_BUCKETS = 32
MAX_DISTANCE = 128
SSM_INNER = 2 * D_MODEL
SSM_HEAD_DIM = 64
SSM_HEADS = SSM_INNER // SSM_HEAD_DIM
SSM_GROUPS = 8
SSM_STATE = 128
CONV_WIDTH = 4
CHUNK = 128
MLP_HIDDEN = 4 * D_MODEL
EPS = 1e-6

ATT_Q = ATT_HEADS * HEAD_DIM
ATT_KV = ATT_KV_HEADS * HEAD_DIM
IDX_Q = IDX_HEADS * IDX_DIM
SSM_BC = SSM_GROUPS * SSM_STATE
CONV_DIM = SSM_INNER + 2 * SSM_BC
SPLITS = (D_MODEL, D_MODEL, ATT_Q, ATT_KV, ATT_KV, IDX_Q, IDX_DIM, IDX_HEADS, SSM_INNER, CONV_DIM, SSM_HEADS)
_OFFS = tuple(int(v) for v in np.cumsum((0,) + SPLITS))
(O_GA, O_GS, O_Q, O_K, O_V, O_QI, O_KI, O_WI, O_Z, O_XBC, O_DT, _O_END) = _OFFS

M_GA = 0
M_GS = M_GA + D_MODEL
M_Q = M_GS + D_MODEL
M_K = M_Q + ATT_Q
M_V = M_K + ATT_KV
M_QI = M_V + ATT_KV
M_Z = M_QI + IDX_Q
M_XBC = M_Z + SSM_INNER
M_END = M_XBC + CONV_DIM
S_KI = 0
S_WI = S_KI + IDX_DIM
S_DT = S_WI + IDX_HEADS
S_END = 256

HEADS_PER_GROUP = SSM_HEADS // SSM_GROUPS
GROUP_W = HEADS_PER_GROUP * SSM_HEAD_DIM

LANES = 128
SUBLANES = 8
BF16_ROWS = 2 * SUBLANES
VMEM_LIMIT = 56 * 1024 * 1024

NEG_INF = float("-inf")
INT_MIN = -(2 ** 31)
KEY_NEG_INF = int(np.int32(np.uint32(0xFF800000) ^ np.uint32(0x7FFFFFFF)))


def _dot(a, b):
    return jnp.dot(a, b, preferred_element_type=F32)


def _dot_nt(a, b):
    return lax.dot_general(a, b, (((1,), (1,)), ((), ())), preferred_element_type=F32)


def _split3(x):
    hi = x.astype(BF16)
    r = x - hi.astype(F32)
    mid = r.astype(BF16)
    lo = (r - mid.astype(F32)).astype(BF16)
    return hi, mid, lo


def _silu(x):
    h = 0.5 * x
    return h + h * jnp.tanh(h)


def _sigmoid(x):
    return 1.0 / (1.0 + jnp.exp(-x))


def _params(sem):
    return pltpu.CompilerParams(dimension_semantics=sem, vmem_limit_bytes=VMEM_LIMIT)


def _norm_small_kernel(x_ref, g_ref, w_ref, xn_ref, sm_ref):
    x = x_ref[...]
    ms = jnp.mean(x * x, axis=-1, keepdims=True)
    xn = (x * lax.rsqrt(ms + EPS) * g_ref[...]).astype(BF16)
    xn_ref[...] = xn
    sm_ref[...] = _dot_nt(xn, w_ref[...].astype(BF16))


def _norm_small(x, g, w_small_t, tm):
    m, d = x.shape
    n = w_small_t.shape[0]
    return pl.pallas_call(
        _norm_small_kernel,
        grid=(m // tm,),
        in_specs=[pl.BlockSpec((tm, d), lambda i: (i, 0)),
                  pl.BlockSpec((1, d), lambda i: (0, 0)),
                  pl.BlockSpec((n, d), lambda i: (0, 0))],
        out_specs=[pl.BlockSpec((tm, d), lambda i: (i, 0)),
                   pl.BlockSpec((tm, n), lambda i: (i, 0))],
        out_shape=[jax.ShapeDtypeStruct((m, d), BF16), jax.ShapeDtypeStruct((m, n), F32)],
        compiler_params=_params(("parallel",)),
        name="norm_small_proj",
    )(x, g, w_small_t)


IN_TN = 1024
IN_ALIGNED_TILES = O_KI // IN_TN
IN_SHIFT = O_Z - M_Z
IN_NEXT = 128
IN_ROWS = 256


def _in_proj_kernel(xn_ref, wa_ref, wn_ref, *rest, n_cast, n_side):
    side_in, o_ref, side_out, w_scr = rest[:n_side], rest[n_side], rest[n_side + 1:2 * n_side + 1], rest[-1]
    j = pl.program_id(0)
    i = pl.program_id(1)

    @pl.when(j * pl.num_programs(1) + i < n_cast)
    def _():
        for src, dst in zip(side_in, side_out):
            dst[...] = src[...].astype(BF16)

    @pl.when((i == 0) & (j < IN_ALIGNED_TILES))
    def _():
        for r0 in range(0, IN_TN, IN_ROWS):
            w_scr[r0:r0 + IN_ROWS, :] = wa_ref[r0:r0 + IN_ROWS, :].astype(BF16)

    @pl.when((i == 0) & (j >= IN_ALIGNED_TILES))
    def _():
        for r0 in range(0, IN_TN - IN_SHIFT, IN_ROWS):
            r1 = min(r0 + IN_ROWS, IN_TN - IN_SHIFT)
            w_scr[r0:r1, :] = wa_ref[r0 + IN_SHIFT:r1 + IN_SHIFT, :].astype(BF16)
        w_scr[IN_TN - IN_SHIFT:IN_TN, :] = wn_ref[0:IN_SHIFT, :].astype(BF16)

    o_ref[...] = _dot_nt(xn_ref[...], w_scr[...]).astype(o_ref.dtype)


def _in_proj(xn, w_t, side_weights, tm):
    m, d = xn.shape
    assert O_KI % IN_TN == 0 and M_END % IN_TN == 0 and IN_TN % IN_NEXT == 0
    assert 0 < IN_SHIFT <= IN_NEXT and IN_SHIFT % BF16_ROWS == 0
    nj, ni = M_END // IN_TN, m // tm
    n_cast = 1 << ((nj * ni).bit_length() - 2)
    side_specs = []
    for w in side_weights:
        rows = w.shape[0] // n_cast
        assert w.shape[0] % n_cast == 0 and rows % BF16_ROWS == 0
        side_specs.append(pl.BlockSpec((rows, w.shape[1]), lambda j, i: (jnp.minimum(j * ni + i, n_cast - 1), 0)))
    kern = functools.partial(_in_proj_kernel, n_cast=n_cast, n_side=len(side_weights))
    outs = pl.pallas_call(
        kern,
        grid=(nj, ni),
        in_specs=[pl.BlockSpec((tm, d), lambda j, i: (i, 0)),
                  pl.BlockSpec((IN_TN, d), lambda j, i: (j, 0)),
                  pl.BlockSpec((IN_NEXT, d), lambda j, i: ((j + 1) * (IN_TN // IN_NEXT), 0))] + side_specs,
        out_specs=[pl.BlockSpec((tm, IN_TN), lambda j, i: (i, j))] + side_specs,
        out_shape=[jax.ShapeDtypeStruct((m, M_END), BF16)]
                  + [jax.ShapeDtypeStruct(w.shape, BF16) for w in side_weights],
        scratch_shapes=[pltpu.VMEM((IN_TN, d), BF16)],
        compiler_params=_params(("arbitrary", "arbitrary")),
        name="in_proj",
    )(xn, w_t, w_t, *side_weights)
    return outs[0], outs[1:]


LOG2E = math.log2(math.e)
VT_ROWS = HEAD_DIM + BF16_ROWS


def _qk_prep_kernel(q_ref, k_ref, v_ref, qi_ref, sm_ref, qg_ref, kg_ref,
                    qt_ref, kn_ref, vt_ref, qit_ref, wt_ref, ki_ref):
    qg = qg_ref[...]
    for h in range(ATT_HEADS):
        x = q_ref[:, h * HEAD_DIM:(h + 1) * HEAD_DIM].astype(F32)
        ms = jnp.mean(x * x, axis=-1, keepdims=True)
        y = x * lax.rsqrt(ms + EPS) * qg * (HEAD_DIM ** -0.5 * LOG2E)
        qt_ref[0, h] = y.T.astype(BF16)
    kg = kg_ref[...]
    for h in range(ATT_KV_HEADS):
        x = k_ref[:, h * HEAD_DIM:(h + 1) * HEAD_DIM].astype(F32)
        ms = jnp.mean(x * x, axis=-1, keepdims=True)
        kn_ref[:, h * HEAD_DIM:(h + 1) * HEAD_DIM] = (x * lax.rsqrt(ms + EPS) * kg).astype(BF16)
        vt_ref[0, h, 0, 0:HEAD_DIM, :] = v_ref[:, h * HEAD_DIM:(h + 1) * HEAD_DIM].astype(F32).T.astype(BF16)
        vt_ref[0, h, 0, HEAD_DIM:VT_ROWS, :] = jnp.ones((VT_ROWS - HEAD_DIM, v_ref.shape[0]), BF16)
    for p in range(IDX_Q // LANES):
        qit_ref[0, p * LANES:(p + 1) * LANES, :] = qi_ref[:, p * LANES:(p + 1) * LANES].astype(F32).T.astype(BF16)
    sm_t = sm_ref[:, 0:LANES].T
    wt_ref[0] = sm_t[S_WI:S_WI + IDX_HEADS, :] * (IDX_HEADS ** -0.5 * IDX_DIM ** -0.5)
    ki_ref[...] = sm_ref[:, S_KI:S_KI + IDX_DIM].astype(BF16)


def _qk_prep(main, small, qg, kg, batch, seq, tq):
    t = main.shape[0]
    nq = seq // tq
    return pl.pallas_call(
        _qk_prep_kernel,
        grid=(batch, nq),
        in_specs=[pl.BlockSpec((tq, ATT_Q), lambda b, i: (b * nq + i, M_Q // ATT_Q)),
                  pl.BlockSpec((tq, ATT_KV), lambda b, i: (b * nq + i, M_K // ATT_KV)),
                  pl.BlockSpec((tq, ATT_KV), lambda b, i: (b * nq + i, M_V // ATT_KV)),
                  pl.BlockSpec((tq, IDX_Q), lambda b, i: (b * nq + i, M_QI // IDX_Q)),
                  pl.BlockSpec((tq, S_END), lambda b, i: (b * nq + i, 0)),
                  pl.BlockSpec((1, HEAD_DIM), lambda b, i: (0, 0)),
                  pl.BlockSpec((1, HEAD_DIM), lambda b, i: (0, 0))],
        out_specs=[pl.BlockSpec((1, ATT_HEADS, HEAD_DIM, tq), lambda b, i: (b * nq + i, 0, 0, 0)),
                   pl.BlockSpec((tq, ATT_KV), lambda b, i: (b * nq + i, 0)),
                   pl.BlockSpec((1, ATT_KV_HEADS, 1, VT_ROWS, tq), lambda b, i: (b, 0, i, 0, 0)),
                   pl.BlockSpec((1, IDX_Q, tq), lambda b, i: (b * nq + i, 0, 0)),
                   pl.BlockSpec((1, IDX_HEADS, tq), lambda b, i: (b * nq + i, 0, 0)),
                   pl.BlockSpec((tq, IDX_DIM), lambda b, i: (b * nq + i, 0))],
        out_shape=[jax.ShapeDtypeStruct((t // tq, ATT_HEADS, HEAD_DIM, tq), BF16),
                   jax.ShapeDtypeStruct((t, ATT_KV), BF16),
                   jax.ShapeDtypeStruct((batch, ATT_KV_HEADS, nq, VT_ROWS, tq), BF16),
                   jax.ShapeDtypeStruct((t // tq, IDX_Q, tq), BF16),
                   jax.ShapeDtypeStruct((t // tq, IDX_HEADS, tq), F32),
                   jax.ShapeDtypeStruct((t, IDX_DIM), BF16)],
        compiler_params=_params(("parallel", "parallel")),
        name="qk_prep",
    )(main, main, main, main, small, qg, kg)


DSA_TQ = 256
DSA_TK = 256
CNT_WAYS = 4


def _t5_bucket_np(dist):
    n = np.maximum(dist, 0)
    max_exact = N_BUCKETS // 2
    nf = np.maximum(n, 1).astype(np.float32)
    ratio = (np.log(nf / np.float32(max_exact)) / np.float32(math.log(MAX_DISTANCE / max_exact))
             * np.float32(N_BUCKETS - max_exact))
    large = max_exact + ratio.astype(np.int32)
    large = np.minimum(large, N_BUCKETS - 1)
    return np.where(n < max_exact, n, large).astype(np.int32)


def _bias_bucket_tiles(tq, tk):
    r = np.arange(tq)[None, :]
    c = np.arange(tk)[:, None]
    d0 = _t5_bucket_np(r - c)
    d1 = _t5_bucket_np(tk + r - c)
    assert np.all(_t5_bucket_np(np.arange(tk + 1, 8 * tk)) == N_BUCKETS - 1)
    return np.stack([d0, d1]).astype(np.int32)


def _dsa_kernel(relb_ref, bidx_ref, qt_ref, qit_ref, wt_ref, k_ref, vt_ref, ki_ref, o_ref,
                key_ref, plane_ref, madd_ref, bias_ref, s_buf, p_buf, st_ref, acc_ref, *, tq, tk, nkc, topk):
    b = pl.program_id(0)
    i = pl.program_id(1)
    neg_slot = nkc

    @pl.when((b == 0) & (i == 0))
    def _init():
        madd_ref[neg_slot] = jnp.full((tk, tq), NEG_INF, F32)
        for t in range(2):
            bt = bidx_ref[t]

            def head_body(h, carry):
                far = relb_ref[N_BUCKETS - 1, h]

                def bucket_body(bk, acc):
                    return jnp.where(bt == bk, (relb_ref[bk, h] - far) * LOG2E, acc)

                bias_ref[t, h] = lax.fori_loop(0, N_BUCKETS, bucket_body, jnp.zeros((tk, tq), F32))
                return carry

            lax.fori_loop(0, ATT_HEADS, head_body, 0)

    def _select():
        nj = i + 1
        qpos = i * tq + lax.broadcasted_iota(I32, (tk, tq), 1)

        def score_chunk(j):
            kc = ki_ref[pl.ds(pl.multiple_of(j * tk, tk), tk), :]
            acc = jnp.zeros((tk, tq), F32)
            for h in range(IDX_HEADS):
                z = _dot(kc, qit_ref[0, h * IDX_DIM:(h + 1) * IDX_DIM, :])
                acc = acc + jnp.maximum(z, 0.0) * wt_ref[0, h:h + 1, :]
            kpos = j * tk + lax.broadcasted_iota(I32, (tk, tq), 0)
            acc = jnp.where(kpos <= qpos, acc, NEG_INF)
            bits = pltpu.bitcast(acc, I32)
            bits = jnp.where(bits == INT_MIN, 0, bits)
            key = bits ^ ((bits >> 31) & 0x7FFFFFFF)
            key_ref[j] = key
            plane_ref[0, j] = pltpu.bitcast(bits & jnp.int32(-65536), F32).astype(BF16)
            plane_ref[1, j] = ((key >> 8) & 0xFF).astype(F32).astype(BF16)
            plane_ref[2, j] = (key & 0xFF).astype(F32).astype(BF16)

        def score_quad(u, carry):
            for v in range(4):
                score_chunk(4 * u + v)
            return carry

        lax.fori_loop(0, nj // 4, score_quad, 0)
        done = (nj // 4) * 4

        @pl.when(nj % 4 >= 2)
        def _():
            score_chunk(done)
            score_chunk(done + 1)

        @pl.when(nj % 2 == 1)
        def _():
            score_chunk(nj - 1)

        kf = float(topk)
        one_b = jnp.ones((tk, tq), BF16)
        zero_b = jnp.zeros((tk, tq), BF16)
        grp = BF16_ROWS * CNT_WAYS

        def count(plane, cand_b, strict):
            def body(j, cnt):
                pv = plane_ref[plane, j]
                hit = jnp.where(pv > cand_b if strict else pv >= cand_b, one_b, zero_b)
                for r0 in range(0, tk, grp):
                    cnt = cnt + hit[r0:r0 + grp]
                return cnt

            cnt = lax.fori_loop(0, nj, body, jnp.zeros((grp, tq), BF16))
            return jnp.sum(cnt.astype(F32), axis=0, keepdims=True)

        def keep_ties(src, dst, tie_b):
            def body(j, carry):
                plane_ref[dst, j] = jnp.where(plane_ref[src, j] == tie_b, plane_ref[dst, j], -one_b)
                return carry

            lax.fori_loop(0, nj, body, 0)

        def top_digit_float(p16):
            fb = (p16 & 0xFFFF) ^ jnp.where(p16 < 0, 0x7FFF, 0)
            return pltpu.bitcast(lax.shift_left(fb, 16), F32).astype(BF16)

        def top_iter(it, prefix):
            cand = prefix + lax.shift_left(jnp.int32(1), 15 - it)
            return jnp.where(count(0, top_digit_float(cand), False) >= kf, cand, prefix)

        d_top = lax.fori_loop(0, 16, top_iter, jnp.full((1, tq), -32768, I32))
        t_top = top_digit_float(d_top)
        above = count(0, t_top, True)

        def byte_digit(plane, above_n):
            def it_body(it, prefix):
                cand = prefix + lax.shift_left(jnp.int32(1), 7 - it)
                tot = above_n + count(plane, cand.astype(F32).astype(BF16), False)
                return jnp.where(tot >= kf, cand, prefix)

            return lax.fori_loop(0, 8, it_body, jnp.zeros((1, tq), I32))

        keep_ties(0, 1, t_top)
        d_mid = byte_digit(1, above)
        t_mid = d_mid.astype(F32).astype(BF16)
        above = above + count(1, t_mid, True)
        keep_ties(1, 2, t_mid)
        d_low = byte_digit(2, above)
        t_low = d_low.astype(F32).astype(BF16)
        thr = lax.shift_left(d_top, 16) | lax.shift_left(d_mid, 8) | d_low
        n_ge = above + count(2, t_low, False)
        tied = jnp.max(n_ge) > kf

        @pl.when(jnp.logical_not(tied))
        def _():
            thr_c = jnp.maximum(thr, KEY_NEG_INF + 1)

            def madd_chunk(j, carry):
                madd_ref[j] = jnp.where(key_ref[j] >= thr_c, 0.0, NEG_INF)
                return carry

            lax.fori_loop(0, nj, madd_chunk, 0)

        @pl.when(tied)
        def _():
            need = kf - (above + count(2, t_low, True))
            row = lax.broadcasted_iota(I32, (tk, tq), 0)

            def pos_iter(it, q_pos):
                cand = q_pos + lax.shift_left(jnp.int32(1), (nkc * tk).bit_length() - 2 - it)

                def body(j, cnt):
                    hit = jnp.where(key_ref[j] == thr, jnp.where(row + j * tk < cand, 1.0, 0.0), 0.0)
                    return cnt + jnp.sum(hit.reshape(tk // SUBLANES, SUBLANES, tq), axis=0)

                cnt = lax.fori_loop(0, nj, body, jnp.zeros((SUBLANES, tq), F32))
                return jnp.where(jnp.sum(cnt, axis=0, keepdims=True) < need, cand, q_pos)

            last = lax.fori_loop(0, (nkc * tk).bit_length() - 1, pos_iter, jnp.zeros((1, tq), I32))
            thr_c = jnp.maximum(thr, KEY_NEG_INF)

            def madd_chunk(j, carry):
                kk = key_ref[j]
                take_tie = jnp.where(row + j * tk <= last, jnp.where(kk > KEY_NEG_INF, 0.0, NEG_INF), NEG_INF)
                madd_ref[j] = jnp.where(kk > thr_c, 0.0, jnp.where(kk == thr, take_tie, NEG_INF))
                return carry

            lax.fori_loop(0, nj, madd_chunk, 0)

    _select()

    ng = ATT_KV_HEADS
    c0 = jnp.maximum(i - 1, 0)
    c1 = jnp.minimum(c0 + 1, nkc - 1)
    n_far = (c0 + 1) // 2

    def scores_to(slot, g, ca, madd_c, bias_c):
        rows = pl.ds(pl.multiple_of(ca * tk, tk), 2 * tk)
        qt = jnp.concatenate([qt_ref[0, g * REP + r] for r in range(REP)], axis=1)
        add = jnp.concatenate([madd_c if bias_c is None else madd_c + bias_c[r] for r in range(REP)], axis=1)
        s = _dot(k_ref[rows, g * HEAD_DIM:(g + 1) * HEAD_DIM], qt) + add
        s_buf[slot] = s
        st_ref[ng + 2 + slot:ng + 3 + slot, :] = jnp.max(s, axis=0, keepdims=True)

    def softmax_to(slot, g):
        m = st_ref[g:g + 1, :]
        m_new = jnp.maximum(m, st_ref[ng + 2 + slot:ng + 3 + slot, :])
        m_safe = jnp.where(m_new == NEG_INF, 0.0, m_new)
        alpha = jnp.exp2(m - m_safe)
        p_buf[slot] = jnp.exp2((s_buf[slot] - m_safe).astype(BF16))
        st_ref[g:g + 1, :] = m_new
        st_ref[ng + slot:ng + 1 + slot, :] = alpha

    def values_from(slot, g, ca, cb):
        vt_c = jnp.concatenate([vt_ref[0, g, ca], vt_ref[0, g, cb]], axis=1)
        acc_ref[g] = st_ref[ng + slot:ng + 1 + slot, :] * acc_ref[g] + _dot(vt_c, p_buf[slot])

    def near_scores_to(slot, g):
        first = i == 0
        t_a = jnp.where(first, 0, 1)
        idx_b = jnp.where(first, neg_slot, c0 + 1)
        madd_near = jnp.concatenate([madd_ref[c0], madd_ref[idx_b]], axis=0)
        bias_near = [jnp.concatenate([bias_ref[t_a, g * REP + r], bias_ref[0, g * REP + r]], axis=0)
                     for r in range(REP)]
        scores_to(slot, g, c0, madd_near, bias_near)

    def far_scores_to(slot, g, k):
        ca = jnp.minimum(2 * (k - 1), nkc - 2)
        cb = jnp.where(ca + 1 >= c0, neg_slot, ca + 1)
        scores_to(slot, g, ca, jnp.concatenate([madd_ref[ca], madd_ref[cb]], axis=0), None)

    st_ref[0:ng, :] = jnp.full((ng, REP * tq), NEG_INF, F32)
    acc_ref[...] = jnp.zeros_like(acc_ref)

    near_scores_to(0, 0)
    for g in range(ng):
        if g + 1 < ng:
            near_scores_to((g + 1) % 2, g + 1)
        else:
            far_scores_to(0, 0, 1)
        softmax_to(g % 2, g)
        values_from(g % 2, g, c0, c1)

    def far_step(k, carry):
        for g in range(ng):
            if g + 1 < ng:
                far_scores_to((g + 1) % 2, g + 1, k)
            else:
                far_scores_to(0, 0, k + 1)
            softmax_to(g % 2, g)
            values_from(g % 2, g, 2 * (k - 1), 2 * (k - 1) + 1)
        return carry

    lax.fori_loop(1, n_far + 1, far_step, 0)

    for g in range(ng):
        out = acc_ref[g, 0:HEAD_DIM, :] / acc_ref[g, HEAD_DIM:HEAD_DIM + 1, :]
        for r in range(REP):
            h = g * REP + r
            o_ref[:, h * HEAD_DIM:(h + 1) * HEAD_DIM] = out[:, r * tq:(r + 1) * tq].T.astype(o_ref.dtype)


def _dsa(qt, kn, vt, qit, wt, ki, rel_bias, batch, seq):
    tq, tk = DSA_TQ, DSA_TK
    nq = seq // tq
    nkc = seq // tk
    topk = min(TOPK_MAX, seq // 4)
    bidx = jnp.asarray(_bias_bucket_tiles(tq, tk))
    kern = functools.partial(_dsa_kernel, tq=tq, tk=tk, nkc=nkc, topk=topk)
    once = pl.Buffered(1)
    return pl.pallas_call(
        kern,
        grid=(batch, nq),
        in_specs=[pl.BlockSpec(memory_space=pltpu.SMEM),
                  pl.BlockSpec((2, tk, tq), lambda b, i: (0, 0, 0), pipeline_mode=once),
                  pl.BlockSpec((1, ATT_HEADS, HEAD_DIM, tq), lambda b, i: (b * nq + i, 0, 0, 0)),
                  pl.BlockSpec((1, IDX_Q, tq), lambda b, i: (b * nq + i, 0, 0)),
                  pl.BlockSpec((1, IDX_HEADS, tq), lambda b, i: (b * nq + i, 0, 0)),
                  pl.BlockSpec((seq, ATT_KV), lambda b, i: (b, 0), pipeline_mode=once),
                  pl.BlockSpec((1, ATT_KV_HEADS, nkc, VT_ROWS, tk), lambda b, i: (b, 0, 0, 0, 0),
                               pipeline_mode=once),
                  pl.BlockSpec((seq, IDX_DIM), lambda b, i: (b, 0), pipeline_mode=once)],
        out_specs=pl.BlockSpec((tq, ATT_Q), lambda b, i: (b * nq + i, 0)),
        out_shape=jax.ShapeDtypeStruct((batch * seq, ATT_Q), BF16),
        scratch_shapes=[pltpu.VMEM((nkc, tk, tq), I32),
                        pltpu.VMEM((3, nkc, tk, tq), BF16),
                        pltpu.VMEM((nkc + 1, tk, tq), F32),
                        pltpu.VMEM((2, ATT_HEADS, tk, tq), F32),
                        pltpu.VMEM((2, 2 * tk, REP * tq), F32),
                        pltpu.VMEM((2, 2 * tk, REP * tq), BF16),
                        pltpu.VMEM((ATT_KV_HEADS + 4, REP * tq), F32),
                        pltpu.VMEM((ATT_KV_HEADS, VT_ROWS, REP * tq), F32)],
        compiler_params=_params(("arbitrary", "arbitrary")),
        name="dsa_attention",
    )(rel_bias, bidx, qt, qit, wt, kn, vt, ki)


E_ROWS = 3 * CHUNK + BF16_ROWS
CONV_HALO = BF16_ROWS


def _conv_shift_matrix():
    ext = CONV_HALO + CHUNK
    s = np.zeros((CHUNK, CONV_WIDTH * ext), np.float32)
    for k in range(CONV_WIDTH):
        t = np.arange(CHUNK)
        s[t, k * ext + CONV_HALO + t - (CONV_WIDTH - 1) + k] = 1.0
    return s


def _ssd_kernel(xbc_ref, halo_ref, z_ref, sm_ref, shift_ref, cw_ref, cb_ref, dtb_ref, alog_ref, dsk_ref, ng_ref,
                o_ref, xs_ref, bm_ref, cm_ref, state_ref, ypre_ref, acg_ref, actg_ref):
    c = pl.program_id(1)
    L = CHUNK

    @pl.when(c == 0)
    def _():
        state_ref[...] = jnp.zeros_like(state_ref)

    halo_on = (c > 0).astype(BF16)
    shift = shift_ref[...]
    cblk = GROUP_W
    for cbi in range(CONV_DIM // cblk):
        cols = slice(cbi * cblk, (cbi + 1) * cblk)
        ext = jnp.concatenate([halo_ref[:, cols] * halo_on, xbc_ref[:, cols]], axis=0)
        wtap = cw_ref[:, cols].astype(BF16)
        prods = jnp.concatenate([ext * wtap[kk:kk + 1, :] for kk in range(CONV_WIDTH)], axis=0)
        y = _silu(cb_ref[:, cols] + _dot(shift, prods))
        if cbi < SSM_GROUPS:
            xs_ref[cbi] = y
        else:
            per = cblk // SSM_STATE
            for u in range(per):
                gi = (cbi - SSM_GROUPS) * per + u
                piece = y[:, u * SSM_STATE:(u + 1) * SSM_STATE]
                if gi < SSM_GROUPS:
                    bm_ref[gi] = piece
                else:
                    cm_ref[gi - SSM_GROUPS] = piece

    dt_in = sm_ref[:, S_DT:S_DT + SSM_HEADS] + dtb_ref[...]
    dt_act = jnp.maximum(dt_in, 0.0) + jnp.log1p(jnp.exp(-jnp.abs(dt_in)))
    a = dt_act * (-jnp.exp(alog_ref[...]))
    ri = lax.broadcasted_iota(I32, (L, L), 0)
    ci = lax.broadcasted_iota(I32, (L, L), 1)
    tril = ri >= ci
    tri_b = jnp.where(tril, 1.0, 0.0).astype(BF16)
    a_cum = sum(_dot(tri_b, p) for p in _split3(a))
    eye_b = jnp.where(lax.broadcasted_iota(I32, (SSM_HEADS, SSM_HEADS), 0)
                      == lax.broadcasted_iota(I32, (SSM_HEADS, SSM_HEADS), 1), 1.0, 0.0).astype(BF16)
    a_cum_t = sum(_dot_nt(eye_b, p) for p in _split3(a_cum))
    a_last = a_cum[L - 1:L, :]
    for gi in range(SSM_GROUPS):
        acg_ref[gi] = a_cum[:, gi * HEADS_PER_GROUP:(gi + 1) * HEADS_PER_GROUP]
        actg_ref[gi] = a_cum_t[gi * HEADS_PER_GROUP:(gi + 1) * HEADS_PER_GROUP, :]
    cd3 = _split3(jnp.exp(a_last))
    ds3 = _split3(dsk_ref[...])
    extras = jnp.concatenate([p.astype(F32) for p in cd3 + ds3]
                             + [jnp.zeros((E_ROWS - 3 * L - 6, SSM_HEADS), F32)], axis=0)
    e_mat = jnp.concatenate([dt_act, jnp.exp(a_cum), jnp.exp(a_last - a_cum), extras], axis=0).astype(BF16)

    lane = lax.broadcasted_iota(I32, (L, LANES), 1)
    lo_mask = lane < SSM_HEAD_DIM

    def group_body(gi, carry):
        xs = xs_ref[gi]
        bg = bm_ref[gi]
        cg_b = cm_ref[gi].astype(BF16)
        hsel = (lax.broadcasted_iota(I32, (SSM_HEADS, GROUP_W), 0)
                == gi * HEADS_PER_GROUP + lax.broadcasted_iota(I32, (SSM_HEADS, GROUP_W), 1) // SSM_HEAD_DIM)
        ex = _dot(e_mat, jnp.where(hsel, 1.0, 0.0).astype(BF16))
        dt_rep = ex[0:L]
        expa_rep = ex[L:2 * L]
        dte_rep = ex[2 * L:3 * L]
        cd_rep = ex[3 * L:3 * L + 1] + ex[3 * L + 1:3 * L + 2] + ex[3 * L + 2:3 * L + 3]
        dsk_rep = ex[3 * L + 3:3 * L + 4] + ex[3 * L + 4:3 * L + 5] + ex[3 * L + 5:3 * L + 6]

        xd = xs * dt_rep
        xd_b = xd.astype(BF16)
        cb = _dot_nt(cg_b, bg.astype(BF16))
        acg = acg_ref[gi]
        actg = actg_ref[gi]
        pairs = []
        for pj in range(HEADS_PER_GROUP // 2):
            gmat = []
            for e in (2 * pj, 2 * pj + 1):
                seg = acg[:, e:e + 1] - actg[e:e + 1, :]
                dec = jnp.exp(jnp.where(tril, seg, NEG_INF))
                gmat.append((cb * dec).astype(BF16))
            xp = xd_b[:, pj * LANES:(pj + 1) * LANES]
            zero = jnp.zeros_like(xp)
            pairs.append(_dot(gmat[0], jnp.where(lo_mask, xp, zero))
                         + _dot(gmat[1], jnp.where(lo_mask, zero, xp)))
        y_diag = jnp.concatenate(pairs, axis=1)

        st = state_ref[gi]
        y_off = _dot(cg_b, st.astype(BF16)) * expa_rep
        xdd = (xd * dte_rep).astype(BF16)
        state_ref[gi] = st * cd_rep + _dot(bg.T.astype(BF16), xdd)
        ypre_ref[gi] = y_diag + y_off + dsk_rep * xs
        return carry

    lax.fori_loop(0, SSM_GROUPS, group_body, 0, unroll=8)

    for gi in range(SSM_GROUPS):
        cols = slice(gi * GROUP_W, (gi + 1) * GROUP_W)
        zz = z_ref[:, cols].astype(F32)
        y = ypre_ref[gi] * _silu(zz)
        ms = jnp.mean(y * y, axis=-1, keepdims=True)
        o_ref[:, cols] = (y * lax.rsqrt(ms + EPS) * ng_ref[:, cols]).astype(o_ref.dtype)


def _ssd(main, small, conv_w, conv_b, dt_bias, a_log, d_skip, norm_g, batch, seq):
    nc = seq // CHUNK
    hb = CHUNK // CONV_HALO
    row = lambda b, c: b * nc + c
    full = lambda shape: pl.BlockSpec(shape, lambda b, c: (0,) * len(shape))
    shift = jnp.asarray(_conv_shift_matrix(), BF16)
    return pl.pallas_call(
        _ssd_kernel,
        grid=(batch, nc),
        in_specs=[pl.BlockSpec((CHUNK, CONV_DIM), lambda b, c: (row(b, c), M_XBC // CONV_DIM)),
                  pl.BlockSpec((CONV_HALO, CONV_DIM),
                               lambda b, c: (jnp.maximum(row(b, c) * hb - 1, 0), M_XBC // CONV_DIM)),
                  pl.BlockSpec((CHUNK, SSM_INNER), lambda b, c: (row(b, c), M_Z // SSM_INNER)),
                  pl.BlockSpec((CHUNK, S_END), lambda b, c: (row(b, c), 0)),
                  full((CHUNK, CONV_WIDTH * (CONV_HALO + CHUNK))),
                  full((CONV_WIDTH, CONV_DIM)), full((1, CONV_DIM)), full((1, SSM_HEADS)),
                  full((1, SSM_HEADS)), full((1, SSM_HEADS)), full((1, SSM_INNER))],
        out_specs=pl.BlockSpec((CHUNK, SSM_INNER), lambda b, c: (row(b, c), 0)),
        out_shape=jax.ShapeDtypeStruct((batch * seq, SSM_INNER), BF16),
        scratch_shapes=[pltpu.VMEM((SSM_GROUPS, CHUNK, GROUP_W), F32),
                        pltpu.VMEM((SSM_GROUPS, CHUNK, SSM_STATE), F32),
                        pltpu.VMEM((SSM_GROUPS, CHUNK, SSM_STATE), F32),
                        pltpu.VMEM((SSM_GROUPS, SSM_STATE, GROUP_W), F32),
                        pltpu.VMEM((SSM_GROUPS, CHUNK, GROUP_W), F32),
                        pltpu.VMEM((SSM_GROUPS, CHUNK, HEADS_PER_GROUP), F32),
                        pltpu.VMEM((SSM_GROUPS, HEADS_PER_GROUP, CHUNK), F32)],
        compiler_params=_params(("arbitrary", "arbitrary")),
        name="ssd_scan",
    )(main, main, main, small, shift, conv_w, conv_b, dt_bias, a_log, d_skip, norm_g)


def _merge_kernel(att_ref, y_ref, ga_ref, gs_ref, wa_ref, ws_ref, o_ref):
    pa = _dot(att_ref[...], wa_ref[...])
    ps = _dot(y_ref[...], ws_ref[...])
    o_ref[...] = (_sigmoid(ga_ref[...].astype(F32)) * pa + _sigmoid(gs_ref[...].astype(F32)) * ps).astype(o_ref.dtype)


def _merge(att, y, main, wa, ws, tm, tn):
    t = att.shape[0]
    return pl.pallas_call(
        _merge_kernel,
        grid=(t // tm, D_MODEL // tn),
        in_specs=[pl.BlockSpec((tm, ATT_Q), lambda i, j: (i, 0)),
                  pl.BlockSpec((tm, SSM_INNER), lambda i, j: (i, 0)),
                  pl.BlockSpec((tm, tn), lambda i, j: (i, M_GA // tn + j)),
                  pl.BlockSpec((tm, tn), lambda i, j: (i, M_GS // tn + j)),
                  pl.BlockSpec((ATT_Q, tn), lambda i, j: (0, j)),
                  pl.BlockSpec((SSM_INNER, tn), lambda i, j: (0, j))],
        out_specs=pl.BlockSpec((tm, tn), lambda i, j: (i, j)),
        out_shape=jax.ShapeDtypeStruct((t, D_MODEL), BF16),
        compiler_params=_params(("parallel", "arbitrary")),
        name="gated_merge",
    )(att, y, main, main, wa, ws)


def _out_proj_kernel(m_ref, w_ref, x_ref, g_ref, x1_ref, h2_ref):
    x1 = x_ref[...] + _dot(m_ref[...], w_ref[...])
    x1_ref[...] = x1
    ms = jnp.mean(x1 * x1, axis=-1, keepdims=True)
    h2_ref[...] = (x1 * lax.rsqrt(ms + EPS) * g_ref[...]).astype(BF16)


def _out_proj(merged, w, x, g, tm):
    t = x.shape[0]
    return pl.pallas_call(
        _out_proj_kernel,
        grid=(t // tm,),
        in_specs=[pl.BlockSpec((tm, D_MODEL), lambda i: (i, 0)),
                  pl.BlockSpec((D_MODEL, D_MODEL), lambda i: (0, 0)),
                  pl.BlockSpec((tm, D_MODEL), lambda i: (i, 0)),
                  pl.BlockSpec((1, D_MODEL), lambda i: (0, 0))],
        out_specs=[pl.BlockSpec((tm, D_MODEL), lambda i: (i, 0)),
                   pl.BlockSpec((tm, D_MODEL), lambda i: (i, 0))],
        out_shape=[jax.ShapeDtypeStruct((t, D_MODEL), F32),
                   jax.ShapeDtypeStruct((t, D_MODEL), BF16)],
        compiler_params=_params(("parallel",)),
        name="out_proj_norm",
    )(merged, w, x, g)


def _mlp_kernel(h_ref, x1_ref, wu_ref, wd_ref, o_ref):
    @pl.when(pl.program_id(1) == 0)
    def _():
        o_ref[...] = x1_ref[...]

    u = _dot(h_ref[...], wu_ref[...])
    u = jnp.square(jnp.maximum(u, 0.0)).astype(BF16)
    o_ref[...] += _dot(u, wd_ref[...])


def _mlp(h2, x1, wu, wd, tm, th):
    t = h2.shape[0]
    return pl.pallas_call(
        _mlp_kernel,
        grid=(t // tm, MLP_HIDDEN // th),
        in_specs=[pl.BlockSpec((tm, D_MODEL), lambda i, j: (i, 0)),
                  pl.BlockSpec((tm, D_MODEL), lambda i, j: (i, 0)),
                  pl.BlockSpec((D_MODEL, th), lambda i, j: (0, j)),
                  pl.BlockSpec((th, D_MODEL), lambda i, j: (j, 0))],
        out_specs=pl.BlockSpec((tm, D_MODEL), lambda i, j: (i, 0)),
        out_shape=jax.ShapeDtypeStruct((t, D_MODEL), F32),
        compiler_params=_params(("parallel", "arbitrary")),
        name="relu2_mlp",
    )(h2, x1, wu, wd)


def _pack_w_small(w_t):
    assert (O_GA, O_KI, O_Z, O_DT) == (M_GA, M_Z, O_WI + IDX_HEADS, O_Z + M_END - M_Z)
    return jnp.concatenate([w_t[O_KI:O_Z], w_t[O_DT:O_DT + SSM_HEADS],
                            jnp.zeros((S_END - S_DT - SSM_HEADS, w_t.shape[1]), w_t.dtype)], axis=0)


def _block(x2, batch, seq, norm1_g, w_in, conv_w, conv_b, dt_bias, a_log, d_skip, ssm_norm_g, q_norm_g,
           k_norm_g, rel_bias, w_att_branch, w_ssm_branch, w_out, norm2_g, w_up, w_down):
    row = lambda v: v.reshape(1, -1)
    w_t = w_in.T
    xn, small = _norm_small(x2, row(norm1_g), _pack_w_small(w_t), tm=min(1024, x2.shape[0]))
    main, (w_att_b, w_ssm_b, w_out_b, w_up_b, w_down_b) = _in_proj(
        xn, w_t, (w_att_branch, w_ssm_branch, w_out, w_up, w_down), tm=min(1024, x2.shape[0]))
    qt, kn, vt, qit, wt, ki = _qk_prep(main, small, row(q_norm_g), row(k_norm_g), batch, seq, tq=DSA_TQ)
    att = _dsa(qt, kn, vt, qit, wt, ki, rel_bias, batch, seq)
    y = _ssd(main, small, conv_w, row(conv_b), row(dt_bias), row(a_log), row(d_skip), row(ssm_norm_g),
             batch, seq)
    merged = _merge(att, y, main, w_att_b, w_ssm_b, tm=512, tn=1024)
    x1, h2 = _out_proj(merged, w_out_b, x2, row(norm2_g), tm=512)
    return _mlp(h2, x1, w_up_b, w_down_b, tm=512, th=1024)


def kernel(x, norm1_g, w_in, conv_w, conv_b, dt_bias, a_log, d_skip, ssm_norm_g, q_norm_g, k_norm_g, rel_bias,
           w_att_branch, w_ssm_branch, w_out, norm2_g, w_up, w_down):
    batch, seq, d = x.shape
    x2 = x.reshape(batch * seq, d)
    for l in range(norm1_g.shape[0]):
        x2 = _block(x2, batch, seq, norm1_g[l], w_in[l], conv_w[l], conv_b[l], dt_bias[l], a_log[l], d_skip[l],
                    ssm_norm_g[l], q_norm_g[l], k_norm_g[l], rel_bias, w_att_branch[l], w_ssm_branch[l],
                    w_out[l], norm2_g[l], w_up[l], w_down[l])
    return x2.reshape(batch, seq, d)
```

```python
import functools
import math

import numpy as np
import jax
import jax.numpy as jnp
from jax import lax
from jax.experimental import pallas as pl
from jax.experimental.pallas import tpu as pltpu

F32 = jnp.float32
BF16 = jnp.bfloat16
I32 = jnp.int32

D_MODEL = 2048
ATT_HEADS = 16
ATT_KV_HEADS = 4
HEAD_DIM = 128
REP = ATT_HEADS // ATT_KV_HEADS
IDX_HEADS = 16
IDX_DIM = 64
TOPK_MAX = 256
N_BUCKETS = 32
MAX_DISTANCE = 128
SSM_INNER = 2 * D_MODEL
SSM_HEAD_DIM = 64
SSM_HEADS = SSM_INNER // SSM_HEAD_DIM
SSM_GROUPS = 8
SSM_STATE = 128
CONV_WIDTH = 4
CHUNK = 128
MLP_HIDDEN = 4 * D_MODEL
EPS = 1e-6

ATT_Q = ATT_HEADS * HEAD_DIM
ATT_KV = ATT_KV_HEADS * HEAD_DIM
IDX_Q = IDX_HEADS * IDX_DIM
SSM_BC = SSM_GROUPS * SSM_STATE
CONV_DIM = SSM_INNER + 2 * SSM_BC
SPLITS = (D_MODEL, D_MODEL, ATT_Q, ATT_KV, ATT_KV, IDX_Q, IDX_DIM, IDX_HEADS, SSM_INNER, CONV_DIM, SSM_HEADS)
_OFFS = tuple(int(v) for v in np.cumsum((0,) + SPLITS))
(O_GA, O_GS, O_Q, O_K, O_V, O_QI, O_KI, O_WI, O_Z, O_XBC, O_DT, _O_END) = _OFFS

M_GA = 0
M_GS = M_GA + D_MODEL
M_Q = M_GS + D_MODEL
M_K = M_Q + ATT_Q
M_V = M_K + ATT_KV
M_QI = M_V + ATT_KV
M_Z = M_QI + IDX_Q
M_XBC = M_Z + SSM_INNER
M_END = M_XBC + CONV_DIM
S_KI = 0
S_WI = S_KI + IDX_DIM
S_DT = S_WI + IDX_HEADS
S_END = 256

HEADS_PER_GROUP = SSM_HEADS // SSM_GROUPS
GROUP_W = HEADS_PER_GROUP * SSM_HEAD_DIM

LANES = 128
SUBLANES = 8
BF16_ROWS = 2 * SUBLANES
VMEM_LIMIT = 56 * 1024 * 1024

NEG_INF = float("-inf")
INT_MIN = -(2 ** 31)
KEY_NEG_INF = int(np.int32(np.uint32(0xFF800000) ^ np.uint32(0x7FFFFFFF)))


def _dot(a, b):
    return jnp.dot(a, b, preferred_element_type=F32)


def _dot_nt(a, b):
    return lax.dot_general(a, b, (((1,), (1,)), ((), ())), preferred_element_type=F32)


def _split3(x):
    hi = x.astype(BF16)
    r = x - hi.astype(F32)
    mid = r.astype(BF16)
    lo = (r - mid.astype(F32)).astype(BF16)
    return hi, mid, lo


def _silu(x):
    h = 0.5 * x
    return h + h * jnp.tanh(h)


def _sigmoid(x):
    return 1.0 / (1.0 + jnp.exp(-x))


def _params(sem):
    return pltpu.CompilerParams(dimension_semantics=sem, vmem_limit_bytes=VMEM_LIMIT)


def _norm_small_kernel(x_ref, g_ref, w_ref, xn_ref, sm_ref):
    x = x_ref[...]
    ms = jnp.mean(x * x, axis=-1, keepdims=True)
    xn = (x * lax.rsqrt(ms + EPS) * g_ref[...]).astype(BF16)
    xn_ref[...] = xn
    sm_ref[...] = _dot_nt(xn, w_ref[...].astype(BF16))


def _norm_small(x, g, w_small_t, tm):
    m, d = x.shape
    n = w_small_t.shape[0]
    return pl.pallas_call(
        _norm_small_kernel,
        grid=(m // tm,),
        in_specs=[pl.BlockSpec((tm, d), lambda i: (i, 0)),
                  pl.BlockSpec((1, d), lambda i: (0, 0)),
                  pl.BlockSpec((n, d), lambda i: (0, 0))],
        out_specs=[pl.BlockSpec((tm, d), lambda i: (i, 0)),
                   pl.BlockSpec((tm, n), lambda i: (i, 0))],
        out_shape=[jax.ShapeDtypeStruct((m, d), BF16), jax.ShapeDtypeStruct((m, n), F32)],
        compiler_params=_params(("parallel",)),
        name="norm_small_proj",
    )(x, g, w_small_t)


IN_TN = 1024
IN_ALIGNED_TILES = O_KI // IN_TN
IN_SHIFT = O_Z - M_Z
IN_NEXT = 128
IN_ROWS = 256


def _in_proj_kernel(xn_ref, wa_ref, wn_ref, o_ref, w_scr):
    j = pl.program_id(0)
    i = pl.program_id(1)

    @pl.when((i == 0) & (j < IN_ALIGNED_TILES))
    def _():
        for r0 in range(0, IN_TN, IN_ROWS):
            w_scr[r0:r0 + IN_ROWS, :] = wa_ref[r0:r0 + IN_ROWS, :].astype(BF16)

    @pl.when((i == 0) & (j >= IN_ALIGNED_TILES))
    def _():
        for r0 in range(0, IN_TN - IN_SHIFT, IN_ROWS):
            r1 = min(r0 + IN_ROWS, IN_TN - IN_SHIFT)
            w_scr[r0:r1, :] = wa_ref[r0 + IN_SHIFT:r1 + IN_SHIFT, :].astype(BF16)
        w_scr[IN_TN - IN_SHIFT:IN_TN, :] = wn_ref[0:IN_SHIFT, :].astype(BF16)

    o_ref[...] = _dot_nt(xn_ref[...], w_scr[...]).astype(o_ref.dtype)


def _in_proj(xn, w_t, tm):
    m, d = xn.shape
    assert O_KI % IN_TN == 0 and M_END % IN_TN == 0 and IN_TN % IN_NEXT == 0
    assert 0 < IN_SHIFT <= IN_NEXT and IN_SHIFT % BF16_ROWS == 0
    return pl.pallas_call(
        _in_proj_kernel,
        grid=(M_END // IN_TN, m // tm),
        in_specs=[pl.BlockSpec((tm, d), lambda j, i: (i, 0)),
                  pl.BlockSpec((IN_TN, d), lambda j, i: (j, 0)),
                  pl.BlockSpec((IN_NEXT, d), lambda j, i: ((j + 1) * (IN_TN // IN_NEXT), 0))],
        out_specs=pl.BlockSpec((tm, IN_TN), lambda j, i: (i, j)),
        out_shape=jax.ShapeDtypeStruct((m, M_END), BF16),
        scratch_shapes=[pltpu.VMEM((IN_TN, d), BF16)],
        compiler_params=_params(("arbitrary", "arbitrary")),
        name="in_proj",
    )(xn, w_t, w_t)


def _side_cast_specs(weights, n_steps, step_of):
    n_cast = 1 << (n_steps.bit_length() - 1)
    specs = []
    for w in weights:
        rows = w.shape[0] // n_cast
        assert w.shape[0] % n_cast == 0 and rows % BF16_ROWS == 0
        specs.append(pl.BlockSpec((rows, w.shape[1]), lambda *ids: (jnp.minimum(step_of(*ids), n_cast - 1), 0)))
    return n_cast, specs


def _side_cast(step, n_cast, srcs, dsts):
    @pl.when(step < n_cast)
    def _():
        for src, dst in zip(srcs, dsts):
            dst[...] = src[...].astype(BF16)


LOG2E = math.log2(math.e)
VT_ROWS = HEAD_DIM + BF16_ROWS


def _qk_prep_kernel(q_ref, k_ref, v_ref, qi_ref, sm_ref, qg_ref, kg_ref,
                    qt_ref, kn_ref, vt_ref, qit_ref, wt_ref, ki_ref):
    qg = qg_ref[...]
    for h in range(ATT_HEADS):
        x = q_ref[:, h * HEAD_DIM:(h + 1) * HEAD_DIM].astype(F32)
        ms = jnp.mean(x * x, axis=-1, keepdims=True)
        y = x * lax.rsqrt(ms + EPS) * qg * (HEAD_DIM ** -0.5 * LOG2E)
        qt_ref[0, h] = y.T.astype(BF16)
    kg = kg_ref[...]
    for h in range(ATT_KV_HEADS):
        x = k_ref[:, h * HEAD_DIM:(h + 1) * HEAD_DIM].astype(F32)
        ms = jnp.mean(x * x, axis=-1, keepdims=True)
        kn_ref[:, h * HEAD_DIM:(h + 1) * HEAD_DIM] = (x * lax.rsqrt(ms + EPS) * kg).astype(BF16)
        vt_ref[0, h, 0, 0:HEAD_DIM, :] = v_ref[:, h * HEAD_DIM:(h + 1) * HEAD_DIM].astype(F32).T.astype(BF16)
        vt_ref[0, h, 0, HEAD_DIM:VT_ROWS, :] = jnp.ones((VT_ROWS - HEAD_DIM, v_ref.shape[0]), BF16)
    for p in range(IDX_Q // LANES):
        qit_ref[0, p * LANES:(p + 1) * LANES, :] = qi_ref[:, p * LANES:(p + 1) * LANES].astype(F32).T.astype(BF16)
    sm_t = sm_ref[:, 0:LANES].T
    wt_ref[0] = sm_t[S_WI:S_WI + IDX_HEADS, :] * (IDX_HEADS ** -0.5 * IDX_DIM ** -0.5)
    ki_ref[...] = sm_ref[:, S_KI:S_KI + IDX_DIM].astype(BF16)


def _qk_prep(main, small, qg, kg, batch, seq, tq):
    t = main.shape[0]
    nq = seq // tq
    return pl.pallas_call(
        _qk_prep_kernel,
        grid=(batch, nq),
        in_specs=[pl.BlockSpec((tq, ATT_Q), lambda b, i: (b * nq + i, M_Q // ATT_Q)),
                  pl.BlockSpec((tq, ATT_KV), lambda b, i: (b * nq + i, M_K // ATT_KV)),
                  pl.BlockSpec((tq, ATT_KV), lambda b, i: (b * nq + i, M_V // ATT_KV)),
                  pl.BlockSpec((tq, IDX_Q), lambda b, i: (b * nq + i, M_QI // IDX_Q)),
                  pl.BlockSpec((tq, S_END), lambda b, i: (b * nq + i, 0)),
                  pl.BlockSpec((1, HEAD_DIM), lambda b, i: (0, 0)),
                  pl.BlockSpec((1, HEAD_DIM), lambda b, i: (0, 0))],
        out_specs=[pl.BlockSpec((1, ATT_HEADS, HEAD_DIM, tq), lambda b, i: (b * nq + i, 0, 0, 0)),
                   pl.BlockSpec((tq, ATT_KV), lambda b, i: (b * nq + i, 0)),
                   pl.BlockSpec((1, ATT_KV_HEADS, 1, VT_ROWS, tq), lambda b, i: (b, 0, i, 0, 0)),
                   pl.BlockSpec((1, IDX_Q, tq), lambda b, i: (b * nq + i, 0, 0)),
                   pl.BlockSpec((1, IDX_HEADS, tq), lambda b, i: (b * nq + i, 0, 0)),
                   pl.BlockSpec((tq, IDX_DIM), lambda b, i: (b * nq + i, 0))],
        out_shape=[jax.ShapeDtypeStruct((t // tq, ATT_HEADS, HEAD_DIM, tq), BF16),
                   jax.ShapeDtypeStruct((t, ATT_KV), BF16),
                   jax.ShapeDtypeStruct((batch, ATT_KV_HEADS, nq, VT_ROWS, tq), BF16),
                   jax.ShapeDtypeStruct((t // tq, IDX_Q, tq), BF16),
                   jax.ShapeDtypeStruct((t // tq, IDX_HEADS, tq), F32),
                   jax.ShapeDtypeStruct((t, IDX_DIM), BF16)],
        compiler_params=_params(("parallel", "parallel")),
        name="qk_prep",
    )(main, main, main, main, small, qg, kg)


DSA_TQ = 256
DSA_TK = 256
CNT_WAYS = 4


def _t5_bucket_np(dist):
    n = np.maximum(dist, 0)
    max_exact = N_BUCKETS // 2
    nf = np.maximum(n, 1).astype(np.float32)
    ratio = (np.log(nf / np.float32(max_exact)) / np.float32(math.log(MAX_DISTANCE / max_exact))
             * np.float32(N_BUCKETS - max_exact))
    large = max_exact + ratio.astype(np.int32)
    large = np.minimum(large, N_BUCKETS - 1)
    return np.where(n < max_exact, n, large).astype(np.int32)


def _bias_bucket_tiles(tq, tk):
    r = np.arange(tq)[None, :]
    c = np.arange(tk)[:, None]
    d0 = _t5_bucket_np(r - c)
    d1 = _t5_bucket_np(tk + r - c)
    assert np.all(_t5_bucket_np(np.arange(tk + 1, 8 * tk)) == N_BUCKETS - 1)
    return np.stack([d0, d1]).astype(np.int32)


def _dsa_kernel(relb_ref, bidx_ref, qt_ref, qit_ref, wt_ref, k_ref, vt_ref, ki_ref, o_ref,
                key_ref, plane_ref, madd_ref, bias_ref, s_buf, p_buf, st_ref, acc_ref, *, tq, tk, nkc, topk):
    b = pl.program_id(0)
    i = pl.program_id(1)
    neg_slot = nkc

    @pl.when((b == 0) & (i == 0))
    def _init():
        madd_ref[neg_slot] = jnp.full((tk, tq), NEG_INF, F32)
        for t in range(2):
            bt = bidx_ref[t]

            def head_body(h, carry):
                far = relb_ref[N_BUCKETS - 1, h]

                def bucket_body(bk, acc):
                    return jnp.where(bt == bk, (relb_ref[bk, h] - far) * LOG2E, acc)

                bias_ref[t, h] = lax.fori_loop(0, N_BUCKETS, bucket_body, jnp.zeros((tk, tq), F32))
                return carry

            lax.fori_loop(0, ATT_HEADS, head_body, 0)

    def _select():
        nj = i + 1
        qpos = i * tq + lax.broadcasted_iota(I32, (tk, tq), 1)

        def score_chunk(j):
            kc = ki_ref[pl.ds(pl.multiple_of(j * tk, tk), tk), :]
            acc = jnp.zeros((tk, tq), F32)
            for h in range(IDX_HEADS):
                z = _dot(kc, qit_ref[0, h * IDX_DIM:(h + 1) * IDX_DIM, :])
                acc = acc + jnp.maximum(z, 0.0) * wt_ref[0, h:h + 1, :]
            kpos = j * tk + lax.broadcasted_iota(I32, (tk, tq), 0)
            acc = jnp.where(kpos <= qpos, acc, NEG_INF)
            bits = pltpu.bitcast(acc, I32)
            bits = jnp.where(bits == INT_MIN, 0, bits)
            key = bits ^ ((bits >> 31) & 0x7FFFFFFF)
            key_ref[j] = key
            plane_ref[0, j] = pltpu.bitcast(bits & jnp.int32(-65536), F32).astype(BF16)
            plane_ref[1, j] = ((key >> 8) & 0xFF).astype(F32).astype(BF16)
            plane_ref[2, j] = (key & 0xFF).astype(F32).astype(BF16)

        def score_quad(u, carry):
            for v in range(4):
                score_chunk(4 * u + v)
            return carry

        lax.fori_loop(0, nj // 4, score_quad, 0)
        done = (nj // 4) * 4

        @pl.when(nj % 4 >= 2)
        def _():
            score_chunk(done)
            score_chunk(done + 1)

        @pl.when(nj % 2 == 1)
        def _():
            score_chunk(nj - 1)

        kf = float(topk)
        one_b = jnp.ones((tk, tq), BF16)
        zero_b = jnp.zeros((tk, tq), BF16)
        grp = BF16_ROWS * CNT_WAYS

        def count(plane, cand_b, strict):
            def body(j, cnt):
                pv = plane_ref[plane, j]
                hit = jnp.where(pv > cand_b if strict else pv >= cand_b, one_b, zero_b)
                for r0 in range(0, tk, grp):
                    cnt = cnt + hit[r0:r0 + grp]
                return cnt

            cnt = lax.fori_loop(0, nj, body, jnp.zeros((grp, tq), BF16))
            return jnp.sum(cnt.astype(F32), axis=0, keepdims=True)

        def keep_ties(src, dst, tie_b):
            def body(j, carry):
                plane_ref[dst, j] = jnp.where(plane_ref[src, j] == tie_b, plane_ref[dst, j], -one_b)
                return carry

            lax.fori_loop(0, nj, body, 0)

        def top_digit_float(p16):
            fb = (p16 & 0xFFFF) ^ jnp.where(p16 < 0, 0x7FFF, 0)
            return pltpu.bitcast(lax.shift_left(fb, 16), F32).astype(BF16)

        def top_iter(it, prefix):
            cand = prefix + lax.shift_left(jnp.int32(1), 15 - it)
            return jnp.where(count(0, top_digit_float(cand), False) >= kf, cand, prefix)

        d_top = lax.fori_loop(0, 16, top_iter, jnp.full((1, tq), -32768, I32))
        t_top = top_digit_float(d_top)
        above = count(0, t_top, True)

        def byte_digit(plane, above_n):
            def it_body(it, prefix):
                cand = prefix + lax.shift_left(jnp.int32(1), 7 - it)
                tot = above_n + count(plane, cand.astype(F32).astype(BF16), False)
                return jnp.where(tot >= kf, cand, prefix)

            return lax.fori_loop(0, 8, it_body, jnp.zeros((1, tq), I32))

        keep_ties(0, 1, t_top)
        d_mid = byte_digit(1, above)
        t_mid = d_mid.astype(F32).astype(BF16)
        above = above + count(1, t_mid, True)
        keep_ties(1, 2, t_mid)
        d_low = byte_digit(2, above)
        t_low = d_low.astype(F32).astype(BF16)
        thr = lax.shift_left(d_top, 16) | lax.shift_left(d_mid, 8) | d_low
        n_ge = above + count(2, t_low, False)
        tied = jnp.max(n_ge) > kf

        @pl.when(jnp.logical_not(tied))
        def _():
            thr_c = jnp.maximum(thr, KEY_NEG_INF + 1)

            def madd_chunk(j, carry):
                madd_ref[j] = jnp.where(key_ref[j] >= thr_c, 0.0, NEG_INF)
                return carry

            lax.fori_loop(0, nj, madd_chunk, 0)

        @pl.when(tied)
        def _():
            need = kf - (above + count(2, t_low, True))
            row = lax.broadcasted_iota(I32, (tk, tq), 0)

            def pos_iter(it, q_pos):
                cand = q_pos + lax.shift_left(jnp.int32(1), (nkc * tk).bit_length() - 2 - it)

                def body(j, cnt):
                    hit = jnp.where(key_ref[j] == thr, jnp.where(row + j * tk < cand, 1.0, 0.0), 0.0)
                    return cnt + jnp.sum(hit.reshape(tk // SUBLANES, SUBLANES, tq), axis=0)

                cnt = lax.fori_loop(0, nj, body, jnp.zeros((SUBLANES, tq), F32))
                return jnp.where(jnp.sum(cnt, axis=0, keepdims=True) < need, cand, q_pos)

            last = lax.fori_loop(0, (nkc * tk).bit_length() - 1, pos_iter, jnp.zeros((1, tq), I32))
            thr_c = jnp.maximum(thr, KEY_NEG_INF)

            def madd_chunk(j, carry):
                kk = key_ref[j]
                take_tie = jnp.where(row + j * tk <= last, jnp.where(kk > KEY_NEG_INF, 0.0, NEG_INF), NEG_INF)
                madd_ref[j] = jnp.where(kk > thr_c, 0.0, jnp.where(kk == thr, take_tie, NEG_INF))
                return carry

            lax.fori_loop(0, nj, madd_chunk, 0)

    _select()

    ng = ATT_KV_HEADS
    c0 = jnp.maximum(i - 1, 0)
    c1 = jnp.minimum(c0 + 1, nkc - 1)
    n_far = (c0 + 1) // 2

    def scores_to(slot, g, ca, madd_c, bias_c):
        rows = pl.ds(pl.multiple_of(ca * tk, tk), 2 * tk)
        qt = jnp.concatenate([qt_ref[0, g * REP + r] for r in range(REP)], axis=1)
        add = jnp.concatenate([madd_c if bias_c is None else madd_c + bias_c[r] for r in range(REP)], axis=1)
        s = _dot(k_ref[rows, g * HEAD_DIM:(g + 1) * HEAD_DIM], qt) + add
        s_buf[slot] = s
        st_ref[ng + 2 + slot:ng + 3 + slot, :] = jnp.max(s, axis=0, keepdims=True)

    def softmax_to(slot, g):
        m = st_ref[g:g + 1, :]
        m_new = jnp.maximum(m, st_ref[ng + 2 + slot:ng + 3 + slot, :])
        m_safe = jnp.where(m_new == NEG_INF, 0.0, m_new)
        alpha = jnp.exp2(m - m_safe)
        p_buf[slot] = jnp.exp2((s_buf[slot] - m_safe).astype(BF16))
        st_ref[g:g + 1, :] = m_new
        st_ref[ng + slot:ng + 1 + slot, :] = alpha

    def values_from(slot, g, ca, cb):
        vt_c = jnp.concatenate([vt_ref[0, g, ca], vt_ref[0, g, cb]], axis=1)
        acc_ref[g] = st_ref[ng + slot:ng + 1 + slot, :] * acc_ref[g] + _dot(vt_c, p_buf[slot])

    def near_scores_to(slot, g):
        first = i == 0
        t_a = jnp.where(first, 0, 1)
        idx_b = jnp.where(first, neg_slot, c0 + 1)
        madd_near = jnp.concatenate([madd_ref[c0], madd_ref[idx_b]], axis=0)
        bias_near = [jnp.concatenate([bias_ref[t_a, g * REP + r], bias_ref[0, g * REP + r]], axis=0)
                     for r in range(REP)]
        scores_to(slot, g, c0, madd_near, bias_near)

    def far_scores_to(slot, g, k):
        ca = jnp.minimum(2 * (k - 1), nkc - 2)
        cb = jnp.where(ca + 1 >= c0, neg_slot, ca + 1)
        scores_to(slot, g, ca, jnp.concatenate([madd_ref[ca], madd_ref[cb]], axis=0), None)

    st_ref[0:ng, :] = jnp.full((ng, REP * tq), NEG_INF, F32)
    acc_ref[...] = jnp.zeros_like(acc_ref)

    near_scores_to(0, 0)
    for g in range(ng):
        if g + 1 < ng:
            near_scores_to((g + 1) % 2, g + 1)
        else:
            far_scores_to(0, 0, 1)
        softmax_to(g % 2, g)
        values_from(g % 2, g, c0, c1)

    def far_step(k, carry):
        for g in range(ng):
            if g + 1 < ng:
                far_scores_to((g + 1) % 2, g + 1, k)
            else:
                far_scores_to(0, 0, k + 1)
            softmax_to(g % 2, g)
            values_from(g % 2, g, 2 * (k - 1), 2 * (k - 1) + 1)
        return carry

    lax.fori_loop(1, n_far + 1, far_step, 0)

    for g in range(ng):
        out = acc_ref[g, 0:HEAD_DIM, :] / acc_ref[g, HEAD_DIM:HEAD_DIM + 1, :]
        for r in range(REP):
            h = g * REP + r
            o_ref[:, h * HEAD_DIM:(h + 1) * HEAD_DIM] = out[:, r * tq:(r + 1) * tq].T.astype(o_ref.dtype)


def _dsa(qt, kn, vt, qit, wt, ki, rel_bias, batch, seq):
    tq, tk = DSA_TQ, DSA_TK
    nq = seq // tq
    nkc = seq // tk
    topk = min(TOPK_MAX, seq // 4)
    bidx = jnp.asarray(_bias_bucket_tiles(tq, tk))
    kern = functools.partial(_dsa_kernel, tq=tq, tk=tk, nkc=nkc, topk=topk)
    once = pl.Buffered(1)
    return pl.pallas_call(
        kern,
        grid=(batch, nq),
        in_specs=[pl.BlockSpec(memory_space=pltpu.SMEM),
                  pl.BlockSpec((2, tk, tq), lambda b, i: (0, 0, 0), pipeline_mode=once),
                  pl.BlockSpec((1, ATT_HEADS, HEAD_DIM, tq), lambda b, i: (b * nq + i, 0, 0, 0)),
                  pl.BlockSpec((1, IDX_Q, tq), lambda b, i: (b * nq + i, 0, 0)),
                  pl.BlockSpec((1, IDX_HEADS, tq), lambda b, i: (b * nq + i, 0, 0)),
                  pl.BlockSpec((seq, ATT_KV), lambda b, i: (b, 0), pipeline_mode=once),
                  pl.BlockSpec((1, ATT_KV_HEADS, nkc, VT_ROWS, tk), lambda b, i: (b, 0, 0, 0, 0),
                               pipeline_mode=once),
                  pl.BlockSpec((seq, IDX_DIM), lambda b, i: (b, 0), pipeline_mode=once)],
        out_specs=pl.BlockSpec((tq, ATT_Q), lambda b, i: (b * nq + i, 0)),
        out_shape=jax.ShapeDtypeStruct((batch * seq, ATT_Q), BF16),
        scratch_shapes=[pltpu.VMEM((nkc, tk, tq), I32),
                        pltpu.VMEM((3, nkc, tk, tq), BF16),
                        pltpu.VMEM((nkc + 1, tk, tq), F32),
                        pltpu.VMEM((2, ATT_HEADS, tk, tq), F32),
                        pltpu.VMEM((2, 2 * tk, REP * tq), F32),
                        pltpu.VMEM((2, 2 * tk, REP * tq), BF16),
                        pltpu.VMEM((ATT_KV_HEADS + 4, REP * tq), F32),
                        pltpu.VMEM((ATT_KV_HEADS, VT_ROWS, REP * tq), F32)],
        compiler_params=_params(("arbitrary", "arbitrary")),
        name="dsa_attention",
    )(rel_bias, bidx, qt, qit, wt, kn, vt, ki)


E_ROWS = 3 * CHUNK + BF16_ROWS
CONV_HALO = BF16_ROWS


def _conv_shift_matrix():
    ext = CONV_HALO + CHUNK
    s = np.zeros((CHUNK, CONV_WIDTH * ext), np.float32)
    for k in range(CONV_WIDTH):
        t = np.arange(CHUNK)
        s[t, k * ext + CONV_HALO + t - (CONV_WIDTH - 1) + k] = 1.0
    return s


SSD_INPUTS = 11


def _ssd_kernel(*refs, n_cast, n_side):
    (xbc_ref, halo_ref, z_ref, sm_ref, shift_ref, cw_ref, cb_ref, dtb_ref, alog_ref, dsk_ref,
     ng_ref) = refs[:SSD_INPUTS]
    side_in = refs[SSD_INPUTS:SSD_INPUTS + n_side]
    o_ref = refs[SSD_INPUTS + n_side]
    side_out = refs[SSD_INPUTS + n_side + 1:SSD_INPUTS + 2 * n_side + 1]
    xs_ref, bm_ref, cm_ref, state_ref, ypre_ref, acg_ref, actg_ref = refs[SSD_INPUTS + 2 * n_side + 1:]
    c = pl.program_id(1)
    L = CHUNK
    _side_cast(pl.program_id(0) * pl.num_programs(1) + c, n_cast, side_in, side_out)

    @pl.when(c == 0)
    def _():
        state_ref[...] = jnp.zeros_like(state_ref)

    halo_on = (c > 0).astype(BF16)
    shift = shift_ref[...]
    cblk = GROUP_W
    for cbi in range(CONV_DIM // cblk):
        cols = slice(cbi * cblk, (cbi + 1) * cblk)
        ext = jnp.concatenate([halo_ref[:, cols] * halo_on, xbc_ref[:, cols]], axis=0)
        wtap = cw_ref[:, cols].astype(BF16)
        prods = jnp.concatenate([ext * wtap[kk:kk + 1, :] for kk in range(CONV_WIDTH)], axis=0)
        y = _silu(cb_ref[:, cols] + _dot(shift, prods))
        if cbi < SSM_GROUPS:
            xs_ref[cbi] = y
        else:
            per = cblk // SSM_STATE
            for u in range(per):
                gi = (cbi - SSM_GROUPS) * per + u
                piece = y[:, u * SSM_STATE:(u + 1) * SSM_STATE]
                if gi < SSM_GROUPS:
                    bm_ref[gi] = piece
                else:
                    cm_ref[gi - SSM_GROUPS] = piece

    dt_in = sm_ref[:, S_DT:S_DT + SSM_HEADS] + dtb_ref[...]
    dt_act = jnp.maximum(dt_in, 0.0) + jnp.log1p(jnp.exp(-jnp.abs(dt_in)))
    a = dt_act * (-jnp.exp(alog_ref[...]))
    ri = lax.broadcasted_iota(I32, (L, L), 0)
    ci = lax.broadcasted_iota(I32, (L, L), 1)
    tril = ri >= ci
    tri_b = jnp.where(tril, 1.0, 0.0).astype(BF16)
    a_cum = sum(_dot(tri_b, p) for p in _split3(a))
    eye_b = jnp.where(lax.broadcasted_iota(I32, (SSM_HEADS, SSM_HEADS), 0)
                      == lax.broadcasted_iota(I32, (SSM_HEADS, SSM_HEADS), 1), 1.0, 0.0).astype(BF16)
    a_cum_t = sum(_dot_nt(eye_b, p) for p in _split3(a_cum))
    a_last = a_cum[L - 1:L, :]
    for gi in range(SSM_GROUPS):
        acg_ref[gi] = a_cum[:, gi * HEADS_PER_GROUP:(gi + 1) * HEADS_PER_GROUP]
        actg_ref[gi] = a_cum_t[gi * HEADS_PER_GROUP:(gi + 1) * HEADS_PER_GROUP, :]
    cd3 = _split3(jnp.exp(a_last))
    ds3 = _split3(dsk_ref[...])
    extras = jnp.concatenate([p.astype(F32) for p in cd3 + ds3]
                             + [jnp.zeros((E_ROWS - 3 * L - 6, SSM_HEADS), F32)], axis=0)
    e_mat = jnp.concatenate([dt_act, jnp.exp(a_cum), jnp.exp(a_last - a_cum), extras], axis=0).astype(BF16)

    lane = lax.broadcasted_iota(I32, (L, LANES), 1)
    lo_mask = lane < SSM_HEAD_DIM

    def group_body(gi, carry):
        xs = xs_ref[gi]
        bg = bm_ref[gi]
        cg_b = cm_ref[gi].astype(BF16)
        hsel = (lax.broadcasted_iota(I32, (SSM_HEADS, GROUP_W), 0)
                == gi * HEADS_PER_GROUP + lax.broadcasted_iota(I32, (SSM_HEADS, GROUP_W), 1) // SSM_HEAD_DIM)
        ex = _dot(e_mat, jnp.where(hsel, 1.0, 0.0).astype(BF16))
        dt_rep = ex[0:L]
        expa_rep = ex[L:2 * L]
        dte_rep = ex[2 * L:3 * L]
        cd_rep = ex[3 * L:3 * L + 1] + ex[3 * L + 1:3 * L + 2] + ex[3 * L + 2:3 * L + 3]
        dsk_rep = ex[3 * L + 3:3 * L + 4] + ex[3 * L + 4:3 * L + 5] + ex[3 * L + 5:3 * L + 6]

        xd = xs * dt_rep
        xd_b = xd.astype(BF16)
        cb = _dot_nt(cg_b, bg.astype(BF16))
        acg = acg_ref[gi]
        actg = actg_ref[gi]
        pairs = []
        for pj in range(HEADS_PER_GROUP // 2):
            gmat = []
            for e in (2 * pj, 2 * pj + 1):
                seg = acg[:, e:e + 1] - actg[e:e + 1, :]
                dec = jnp.exp(jnp.where(tril, seg, NEG_INF))
                gmat.append((cb * dec).astype(BF16))
            xp = xd_b[:, pj * LANES:(pj + 1) * LANES]
            zero = jnp.zeros_like(xp)
            pairs.append(_dot(gmat[0], jnp.where(lo_mask, xp, zero))
                         + _dot(gmat[1], jnp.where(lo_mask, zero, xp)))
        y_diag = jnp.concatenate(pairs, axis=1)

        st = state_ref[gi]
        y_off = _dot(cg_b, st.astype(BF16)) * expa_rep
        xdd = (xd * dte_rep).astype(BF16)
        state_ref[gi] = st * cd_rep + _dot(bg.T.astype(BF16), xdd)
        ypre_ref[gi] = y_diag + y_off + dsk_rep * xs
        return carry

    lax.fori_loop(0, SSM_GROUPS, group_body, 0, unroll=8)

    for gi in range(SSM_GROUPS):
        cols = slice(gi * GROUP_W, (gi + 1) * GROUP_W)
        zz = z_ref[:, cols].astype(F32)
        y = ypre_ref[gi] * _silu(zz)
        ms = jnp.mean(y * y, axis=-1, keepdims=True)
        o_ref[:, cols] = (y * lax.rsqrt(ms + EPS) * ng_ref[:, cols]).astype(o_ref.dtype)


def _ssd(main, small, conv_w, conv_b, dt_bias, a_log, d_skip, norm_g, side_weights, batch, seq):
    nc = seq // CHUNK
    n_cast, side_specs = _side_cast_specs(side_weights, batch * nc, lambda b, c: b * nc + c)
    hb = CHUNK // CONV_HALO
    row = lambda b, c: b * nc + c
    full = lambda shape: pl.BlockSpec(shape, lambda b, c: (0,) * len(shape))
    shift = jnp.asarray(_conv_shift_matrix(), BF16)
    outs = pl.pallas_call(
        functools.partial(_ssd_kernel, n_cast=n_cast, n_side=len(side_weights)),
        grid=(batch, nc),
        in_specs=[pl.BlockSpec((CHUNK, CONV_DIM), lambda b, c: (row(b, c), M_XBC // CONV_DIM)),
                  pl.BlockSpec((CONV_HALO, CONV_DIM),
                               lambda b, c: (jnp.maximum(row(b, c) * hb - 1, 0), M_XBC // CONV_DIM)),
                  pl.BlockSpec((CHUNK, SSM_INNER), lambda b, c: (row(b, c), M_Z // SSM_INNER)),
                  pl.BlockSpec((CHUNK, S_END), lambda b, c: (row(b, c), 0)),
                  full((CHUNK, CONV_WIDTH * (CONV_HALO + CHUNK))),
                  full((CONV_WIDTH, CONV_DIM)), full((1, CONV_DIM)), full((1, SSM_HEADS)),
                  full((1, SSM_HEADS)), full((1, SSM_HEADS)), full((1, SSM_INNER))] + side_specs,
        out_specs=[pl.BlockSpec((CHUNK, SSM_INNER), lambda b, c: (row(b, c), 0))] + side_specs,
        out_shape=[jax.ShapeDtypeStruct((batch * seq, SSM_INNER), BF16)]
                  + [jax.ShapeDtypeStruct(w.shape, BF16) for w in side_weights],
        scratch_shapes=[pltpu.VMEM((SSM_GROUPS, CHUNK, GROUP_W), F32),
                        pltpu.VMEM((SSM_GROUPS, CHUNK, SSM_STATE), F32),
                        pltpu.VMEM((SSM_GROUPS, CHUNK, SSM_STATE), F32),
                        pltpu.VMEM((SSM_GROUPS, SSM_STATE, GROUP_W), F32),
                        pltpu.VMEM((SSM_GROUPS, CHUNK, GROUP_W), F32),
                        pltpu.VMEM((SSM_GROUPS, CHUNK, HEADS_PER_GROUP), F32),
                        pltpu.VMEM((SSM_GROUPS, HEADS_PER_GROUP, CHUNK), F32)],
        compiler_params=_params(("arbitrary", "arbitrary")),
        name="ssd_scan",
    )(main, main, main, small, shift, conv_w, conv_b, dt_bias, a_log, d_skip, norm_g, *side_weights)
    return outs[0], outs[1:]


def _merge_kernel(att_ref, y_ref, ga_ref, gs_ref, wa_ref, ws_ref, o_ref):
    pa = _dot(att_ref[...], wa_ref[...])
    ps = _dot(y_ref[...], ws_ref[...])
    o_ref[...] = (_sigmoid(ga_ref[...].astype(F32)) * pa + _sigmoid(gs_ref[...].astype(F32)) * ps).astype(o_ref.dtype)


def _merge(att, y, main, wa, ws, tm, tn):
    t = att.shape[0]
    return pl.pallas_call(
        _merge_kernel,
        grid=(t // tm, D_MODEL // tn),
        in_specs=[pl.BlockSpec((tm, ATT_Q), lambda i, j: (i, 0)),
                  pl.BlockSpec((tm, SSM_INNER), lambda i, j: (i, 0)),
                  pl.BlockSpec((tm, tn), lambda i, j: (i, M_GA // tn + j)),
                  pl.BlockSpec((tm, tn), lambda i, j: (i, M_GS // tn + j)),
                  pl.BlockSpec((ATT_Q, tn), lambda i, j: (0, j)),
                  pl.BlockSpec((SSM_INNER, tn), lambda i, j: (0, j))],
        out_specs=pl.BlockSpec((tm, tn), lambda i, j: (i, j)),
        out_shape=jax.ShapeDtypeStruct((t, D_MODEL), BF16),
        compiler_params=_params(("parallel", "arbitrary")),
        name="gated_merge",
    )(att, y, main, main, wa, ws)


def _out_proj_kernel(m_ref, w_ref, x_ref, g_ref, x1_ref, h2_ref):
    x1 = x_ref[...] + _dot(m_ref[...], w_ref[...])
    x1_ref[...] = x1
    ms = jnp.mean(x1 * x1, axis=-1, keepdims=True)
    h2_ref[...] = (x1 * lax.rsqrt(ms + EPS) * g_ref[...]).astype(BF16)


def _out_proj(merged, w, x, g, tm):
    t = x.shape[0]
    return pl.pallas_call(
        _out_proj_kernel,
        grid=(t // tm,),
        in_specs=[pl.BlockSpec((tm, D_MODEL), lambda i: (i, 0)),
                  pl.BlockSpec((D_MODEL, D_MODEL), lambda i: (0, 0)),
                  pl.BlockSpec((tm, D_MODEL), lambda i: (i, 0)),
                  pl.BlockSpec((1, D_MODEL), lambda i: (0, 0))],
        out_specs=[pl.BlockSpec((tm, D_MODEL), lambda i: (i, 0)),
                   pl.BlockSpec((tm, D_MODEL), lambda i: (i, 0))],
        out_shape=[jax.ShapeDtypeStruct((t, D_MODEL), F32),
                   jax.ShapeDtypeStruct((t, D_MODEL), BF16)],
        compiler_params=_params(("parallel",)),
        name="out_proj_norm",
    )(merged, w, x, g)


def _mlp_kernel(h_ref, x1_ref, wu_ref, wd_ref, o_ref):
    @pl.when(pl.program_id(1) == 0)
    def _():
        o_ref[...] = x1_ref[...]

    u = _dot(h_ref[...], wu_ref[...])
    u = jnp.square(jnp.maximum(u, 0.0)).astype(BF16)
    o_ref[...] += _dot(u, wd_ref[...])


def _mlp(h2, x1, wu, wd, tm, th):
    t = h2.shape[0]
    return pl.pallas_call(
        _mlp_kernel,
        grid=(t // tm, MLP_HIDDEN // th),
        in_specs=[pl.BlockSpec((tm, D_MODEL), lambda i, j: (i, 0)),
                  pl.BlockSpec((tm, D_MODEL), lambda i, j: (i, 0)),
                  pl.BlockSpec((D_MODEL, th), lambda i, j: (0, j)),
                  pl.BlockSpec((th, D_MODEL), lambda i, j: (j, 0))],
        out_specs=pl.BlockSpec((tm, D_MODEL), lambda i, j: (i, 0)),
        out_shape=jax.ShapeDtypeStruct((t, D_MODEL), F32),
        compiler_params=_params(("parallel", "arbitrary")),
        name="relu2_mlp",
    )(h2, x1, wu, wd)


def _pack_w_small(w_t):
    assert (O_GA, O_KI, O_Z, O_DT) == (M_GA, M_Z, O_WI + IDX_HEADS, O_Z + M_END - M_Z)
    return jnp.concatenate([w_t[O_KI:O_Z], w_t[O_DT:O_DT + SSM_HEADS],
                            jnp.zeros((S_END - S_DT - SSM_HEADS, w_t.shape[1]), w_t.dtype)], axis=0)


def _block(x2, batch, seq, norm1_g, w_in, conv_w, conv_b, dt_bias, a_log, d_skip, ssm_norm_g, q_norm_g,
           k_norm_g, rel_bias, w_att_branch, w_ssm_branch, w_out, norm2_g, w_up, w_down):
    row = lambda v: v.reshape(1, -1)
    w_t = w_in.T
    xn, small = _norm_small(x2, row(norm1_g), _pack_w_small(w_t), tm=min(1024, x2.shape[0]))
    main = _in_proj(xn, w_t, tm=min(1024, x2.shape[0]))
    qt, kn, vt, qit, wt, ki = _qk_prep(main, small, row(q_norm_g), row(k_norm_g), batch, seq, tq=DSA_TQ)
    att = _dsa(qt, kn, vt, qit, wt, ki, rel_bias, batch, seq)
    y, (w_att_b, w_ssm_b, w_out_b, w_up_b, w_down_b) = _ssd(
        main, small, conv_w, row(conv_b), row(dt_bias), row(a_log), row(d_skip), row(ssm_norm_g),
        (w_att_branch, w_ssm_branch, w_out, w_up, w_down), batch, seq)
    merged = _merge(att, y, main, w_att_b, w_ssm_b, tm=512, tn=1024)
    x1, h2 = _out_proj(merged, w_out_b, x2, row(norm2_g), tm=512)
    return _mlp(h2, x1, w_up_b, w_down_b, tm=512, th=1024)


def kernel(x, norm1_g, w_in, conv_w, conv_b, dt_bias, a_log, d_skip, ssm_norm_g, q_norm_g, k_norm_g, rel_bias,
           w_att_branch, w_ssm_branch, w_out, norm2_g, w_up, w_down):
    batch, seq, d = x.shape
    x2 = x.reshape(batch * seq, d)
    for l in range(norm1_g.shape[0]):
        x2 = _block(x2, batch, seq, norm1_g[l], w_in[l], conv_w[l], conv_b[l], dt_bias[l], a_log[l], d_skip[l],
                    ssm_norm_g[l], q_norm_g[l], k_norm_g[l], rel_bias, w_att_branch[l], w_ssm_branch[l],
                    w_out[l], norm2_g[l], w_up[l], w_down[l])
    return x2.reshape(batch, seq, d)
```

```python
import functools
import math

import numpy as np
import jax
import jax.numpy as jnp
from jax import lax
from jax.experimental import pallas as pl
from jax.experimental.pallas import tpu as pltpu

F32 = jnp.float32
BF16 = jnp.bfloat16
I32 = jnp.int32

D_MODEL = 2048
ATT_HEADS = 16
ATT_KV_HEADS = 4
HEAD_DIM = 128
REP = ATT_HEADS // ATT_KV_HEADS
IDX_HEADS = 16
IDX_DIM = 64
TOPK_MAX = 256
N_BUCKETS = 32
MAX_DISTANCE = 128
SSM_INNER = 2 * D_MODEL
SSM_HEAD_DIM = 64
SSM_HEADS = SSM_INNER // SSM_HEAD_DIM
SSM_GROUPS = 8
SSM_STATE = 128
CONV_WIDTH = 4
CHUNK = 128
MLP_HIDDEN = 4 * D_MODEL
EPS = 1e-6

ATT_Q = ATT_HEADS * HEAD_DIM
ATT_KV = ATT_KV_HEADS * HEAD_DIM
IDX_Q = IDX_HEADS * IDX_DIM
SSM_BC = SSM_GROUPS * SSM_STATE
CONV_DIM = SSM_INNER + 2 * SSM_BC
SPLITS = (D_MODEL, D_MODEL, ATT_Q, ATT_KV, ATT_KV, IDX_Q, IDX_DIM, IDX_HEADS, SSM_INNER, CONV_DIM, SSM_HEADS)
_OFFS = tuple(int(v) for v in np.cumsum((0,) + SPLITS))
(O_GA, O_GS, O_Q, O_K, O_V, O_QI, O_KI, O_WI, O_Z, O_XBC, O_DT, _O_END) = _OFFS

M_GA = 0
M_GS = M_GA + D_MODEL
M_Q = M_GS + D_MODEL
M_K = M_Q + ATT_Q
M_V = M_K + ATT_KV
M_QI = M_V + ATT_KV
M_Z = M_QI + IDX_Q
M_XBC = M_Z + SSM_INNER
M_END = M_XBC + CONV_DIM
S_KI = 0
S_WI = S_KI + IDX_DIM
S_DT = S_WI + IDX_HEADS
S_END = 256

HEADS_PER_GROUP = SSM_HEADS // SSM_GROUPS
GROUP_W = HEADS_PER_GROUP * SSM_HEAD_DIM

LANES = 128
SUBLANES = 8
BF16_ROWS = 2 * SUBLANES
VMEM_LIMIT = 56 * 1024 * 1024

NEG_INF = float("-inf")
INT_MIN = -(2 ** 31)
KEY_NEG_INF = int(np.int32(np.uint32(0xFF800000) ^ np.uint32(0x7FFFFFFF)))


def _dot(a, b):
    return jnp.dot(a, b, preferred_element_type=F32)


def _dot_nt(a, b):
    return lax.dot_general(a, b, (((1,), (1,)), ((), ())), preferred_element_type=F32)


def _split3(x):
    hi = x.astype(BF16)
    r = x - hi.astype(F32)
    mid = r.astype(BF16)
    lo = (r - mid.astype(F32)).astype(BF16)
    return hi, mid, lo


def _silu(x):
    h = 0.5 * x
    return h + h * jnp.tanh(h)


def _sigmoid(x):
    return 1.0 / (1.0 + jnp.exp(-x))


def _params(sem):
    return pltpu.CompilerParams(dimension_semantics=sem, vmem_limit_bytes=VMEM_LIMIT)


def _norm_small_kernel(x_ref, g_ref, w_ref, xn_ref, sm_ref):
    x = x_ref[...]
    ms = jnp.mean(x * x, axis=-1, keepdims=True)
    xn = (x * lax.rsqrt(ms + EPS) * g_ref[...]).astype(BF16)
    xn_ref[...] = xn
    sm_ref[...] = _dot_nt(xn, w_ref[...].astype(BF16))


def _norm_small(x, g, w_small_t, tm):
    m, d = x.shape
    n = w_small_t.shape[0]
    return pl.pallas_call(
        _norm_small_kernel,
        grid=(m // tm,),
        in_specs=[pl.BlockSpec((tm, d), lambda i: (i, 0)),
                  pl.BlockSpec((1, d), lambda i: (0, 0)),
                  pl.BlockSpec((n, d), lambda i: (0, 0))],
        out_specs=[pl.BlockSpec((tm, d), lambda i: (i, 0)),
                   pl.BlockSpec((tm, n), lambda i: (i, 0))],
        out_shape=[jax.ShapeDtypeStruct((m, d), BF16), jax.ShapeDtypeStruct((m, n), F32)],
        compiler_params=_params(("parallel",)),
        name="norm_small_proj",
    )(x, g, w_small_t)


IN_TN = 1024
IN_ALIGNED_TILES = O_KI // IN_TN
IN_SHIFT = O_Z - M_Z
IN_NEXT = 128
IN_ROWS = 256


def _in_proj_kernel(xn_ref, wa_ref, wn_ref, o_ref, w_scr):
    j = pl.program_id(0)
    i = pl.program_id(1)

    @pl.when((i == 0) & (j < IN_ALIGNED_TILES))
    def _():
        for r0 in range(0, IN_TN, IN_ROWS):
            w_scr[r0:r0 + IN_ROWS, :] = wa_ref[r0:r0 + IN_ROWS, :].astype(BF16)

    @pl.when((i == 0) & (j >= IN_ALIGNED_TILES))
    def _():
        for r0 in range(0, IN_TN - IN_SHIFT, IN_ROWS):
            r1 = min(r0 + IN_ROWS, IN_TN - IN_SHIFT)
            w_scr[r0:r1, :] = wa_ref[r0 + IN_SHIFT:r1 + IN_SHIFT, :].astype(BF16)
        w_scr[IN_TN - IN_SHIFT:IN_TN, :] = wn_ref[0:IN_SHIFT, :].astype(BF16)

    o_ref[...] = _dot_nt(xn_ref[...], w_scr[...]).astype(o_ref.dtype)


def _in_proj(xn, w_t, tm):
    m, d = xn.shape
    assert O_KI % IN_TN == 0 and M_END % IN_TN == 0 and IN_TN % IN_NEXT == 0
    assert 0 < IN_SHIFT <= IN_NEXT and IN_SHIFT % BF16_ROWS == 0
    return pl.pallas_call(
        _in_proj_kernel,
        grid=(M_END // IN_TN, m // tm),
        in_specs=[pl.BlockSpec((tm, d), lambda j, i: (i, 0)),
                  pl.BlockSpec((IN_TN, d), lambda j, i: (j, 0)),
                  pl.BlockSpec((IN_NEXT, d), lambda j, i: ((j + 1) * (IN_TN // IN_NEXT), 0))],
        out_specs=pl.BlockSpec((tm, IN_TN), lambda j, i: (i, j)),
        out_shape=jax.ShapeDtypeStruct((m, M_END), BF16),
        scratch_shapes=[pltpu.VMEM((IN_TN, d), BF16)],
        compiler_params=_params(("arbitrary", "arbitrary")),
        name="in_proj",
    )(xn, w_t, w_t)


def _side_cast_specs(weights, n_steps, step_of):
    n_cast = 1 << (n_steps.bit_length() - 1)
    specs = []
    for w in weights:
        rows = w.shape[0] // n_cast
        assert w.shape[0] % n_cast == 0 and rows % BF16_ROWS == 0
        specs.append(pl.BlockSpec((rows, w.shape[1]), lambda *ids: (jnp.minimum(step_of(*ids), n_cast - 1), 0)))
    return n_cast, specs


def _side_cast(step, n_cast, srcs, dsts):
    @pl.when(step < n_cast)
    def _():
        for src, dst in zip(srcs, dsts):
            dst[...] = src[...].astype(BF16)


LOG2E = math.log2(math.e)
VT_ROWS = HEAD_DIM + BF16_ROWS


def _qk_prep_kernel(q_ref, k_ref, v_ref, qi_ref, sm_ref, qg_ref, kg_ref,
                    qt_ref, kn_ref, vt_ref, qit_ref, wt_ref, ki_ref):
    qg = qg_ref[...]
    for h in range(ATT_HEADS):
        x = q_ref[:, h * HEAD_DIM:(h + 1) * HEAD_DIM].astype(F32)
        ms = jnp.mean(x * x, axis=-1, keepdims=True)
        y = x * lax.rsqrt(ms + EPS) * qg * (HEAD_DIM ** -0.5 * LOG2E)
        qt_ref[0, h] = y.T.astype(BF16)
    kg = kg_ref[...]
    for h in range(ATT_KV_HEADS):
        x = k_ref[:, h * HEAD_DIM:(h + 1) * HEAD_DIM].astype(F32)
        ms = jnp.mean(x * x, axis=-1, keepdims=True)
        kn_ref[:, h * HEAD_DIM:(h + 1) * HEAD_DIM] = (x * lax.rsqrt(ms + EPS) * kg).astype(BF16)
        vt_ref[0, h, 0, 0:HEAD_DIM, :] = v_ref[:, h * HEAD_DIM:(h + 1) * HEAD_DIM].astype(F32).T.astype(BF16)
        vt_ref[0, h, 0, HEAD_DIM:VT_ROWS, :] = jnp.ones((VT_ROWS - HEAD_DIM, v_ref.shape[0]), BF16)
    for p in range(IDX_Q // LANES):
        qit_ref[0, p * LANES:(p + 1) * LANES, :] = qi_ref[:, p * LANES:(p + 1) * LANES].astype(F32).T.astype(BF16)
    sm_t = sm_ref[:, 0:LANES].T
    wt_ref[0] = sm_t[S_WI:S_WI + IDX_HEADS, :] * (IDX_HEADS ** -0.5 * IDX_DIM ** -0.5)
    ki_ref[...] = sm_ref[:, S_KI:S_KI + IDX_DIM].astype(BF16)


def _qk_prep(main, small, qg, kg, batch, seq, tq):
    t = main.shape[0]
    nq = seq // tq
    return pl.pallas_call(
        _qk_prep_kernel,
        grid=(batch, nq),
        in_specs=[pl.BlockSpec((tq, ATT_Q), lambda b, i: (b * nq + i, M_Q // ATT_Q)),
                  pl.BlockSpec((tq, ATT_KV), lambda b, i: (b * nq + i, M_K // ATT_KV)),
                  pl.BlockSpec((tq, ATT_KV), lambda b, i: (b * nq + i, M_V // ATT_KV)),
                  pl.BlockSpec((tq, IDX_Q), lambda b, i: (b * nq + i, M_QI // IDX_Q)),
                  pl.BlockSpec((tq, S_END), lambda b, i: (b * nq + i, 0)),
                  pl.BlockSpec((1, HEAD_DIM), lambda b, i: (0, 0)),
                  pl.BlockSpec((1, HEAD_DIM), lambda b, i: (0, 0))],
        out_specs=[pl.BlockSpec((1, ATT_HEADS, HEAD_DIM, tq), lambda b, i: (b * nq + i, 0, 0, 0)),
                   pl.BlockSpec((tq, ATT_KV), lambda b, i: (b * nq + i, 0)),
                   pl.BlockSpec((1, ATT_KV_HEADS, 1, VT_ROWS, tq), lambda b, i: (b, 0, i, 0, 0)),
                   pl.BlockSpec((1, IDX_Q, tq), lambda b, i: (b * nq + i, 0, 0)),
                   pl.BlockSpec((1, IDX_HEADS, tq), lambda b, i: (b * nq + i, 0, 0)),
                   pl.BlockSpec((tq, IDX_DIM), lambda b, i: (b * nq + i, 0))],
        out_shape=[jax.ShapeDtypeStruct((t // tq, ATT_HEADS, HEAD_DIM, tq), BF16),
                   jax.ShapeDtypeStruct((t, ATT_KV), BF16),
                   jax.ShapeDtypeStruct((batch, ATT_KV_HEADS, nq, VT_ROWS, tq), BF16),
                   jax.ShapeDtypeStruct((t // tq, IDX_Q, tq), BF16),
                   jax.ShapeDtypeStruct((t // tq, IDX_HEADS, tq), F32),
                   jax.ShapeDtypeStruct((t, IDX_DIM), BF16)],
        compiler_params=_params(("parallel", "parallel")),
        name="qk_prep",
    )(main, main, main, main, small, qg, kg)


DSA_TQ = 256
DSA_TK = 256
CNT_WAYS = 4


def _t5_bucket_np(dist):
    n = np.maximum(dist, 0)
    max_exact = N_BUCKETS // 2
    nf = np.maximum(n, 1).astype(np.float32)
    ratio = (np.log(nf / np.float32(max_exact)) / np.float32(math.log(MAX_DISTANCE / max_exact))
             * np.float32(N_BUCKETS - max_exact))
    large = max_exact + ratio.astype(np.int32)
    large = np.minimum(large, N_BUCKETS - 1)
    return np.where(n < max_exact, n, large).astype(np.int32)


def _bias_bucket_tiles(tq, tk):
    r = np.arange(tq)[None, :]
    c = np.arange(tk)[:, None]
    d0 = _t5_bucket_np(r - c)
    d1 = _t5_bucket_np(tk + r - c)
    assert np.all(_t5_bucket_np(np.arange(tk + 1, 8 * tk)) == N_BUCKETS - 1)
    return np.stack([d0, d1]).astype(np.int32)


def _dsa_kernel(relb_ref, bidx_ref, qt_ref, qit_ref, wt_ref, k_ref, vt_ref, ki_ref, o_ref,
                key_ref, plane_ref, madd_ref, bias_ref, s_buf, p_buf, st_ref, acc_ref, *, tq, tk, nkc, topk):
    b = pl.program_id(0)
    i = pl.program_id(1)
    neg_slot = nkc

    @pl.when((b == 0) & (i == 0))
    def _init():
        madd_ref[neg_slot] = jnp.full((tk, tq), NEG_INF, F32)
        for t in range(2):
            bt = bidx_ref[t]

            def head_body(h, carry):
                far = relb_ref[N_BUCKETS - 1, h]

                def bucket_body(bk, acc):
                    return jnp.where(bt == bk, (relb_ref[bk, h] - far) * LOG2E, acc)

                bias_ref[t, h] = lax.fori_loop(0, N_BUCKETS, bucket_body, jnp.zeros((tk, tq), F32))
                return carry

            lax.fori_loop(0, ATT_HEADS, head_body, 0)

    def _select():
        nj = i + 1
        qpos = i * tq + lax.broadcasted_iota(I32, (tk, tq), 1)

        def score_chunk(j):
            kc = ki_ref[pl.ds(pl.multiple_of(j * tk, tk), tk), :]
            acc = jnp.zeros((tk, tq), F32)
            for h in range(IDX_HEADS):
                z = _dot(kc, qit_ref[0, h * IDX_DIM:(h + 1) * IDX_DIM, :])
                acc = acc + jnp.maximum(z, 0.0) * wt_ref[0, h:h + 1, :]
            kpos = j * tk + lax.broadcasted_iota(I32, (tk, tq), 0)
            acc = jnp.where(kpos <= qpos, acc, NEG_INF)
            bits = pltpu.bitcast(acc, I32)
            bits = jnp.where(bits == INT_MIN, 0, bits)
            key = bits ^ ((bits >> 31) & 0x7FFFFFFF)
            key_ref[j] = key
            plane_ref[0, j] = pltpu.bitcast(bits & jnp.int32(-65536), F32).astype(BF16)
            plane_ref[1, j] = ((key >> 8) & 0xFF).astype(F32).astype(BF16)
            plane_ref[2, j] = (key & 0xFF).astype(F32).astype(BF16)

        def score_quad(u, carry):
            for v in range(4):
                score_chunk(4 * u + v)
            return carry

        lax.fori_loop(0, nj // 4, score_quad, 0)
        done = (nj // 4) * 4

        @pl.when(nj % 4 >= 2)
        def _():
            score_chunk(done)
            score_chunk(done + 1)

        @pl.when(nj % 2 == 1)
        def _():
            score_chunk(nj - 1)

        kf = float(topk)
        one_b = jnp.ones((tk, tq), BF16)
        zero_b = jnp.zeros((tk, tq), BF16)
        grp = BF16_ROWS * CNT_WAYS

        def count(plane, cand_b, strict):
            def body(j, cnt):
                pv = plane_ref[plane, j]
                hit = jnp.where(pv > cand_b if strict else pv >= cand_b, one_b, zero_b)
                for r0 in range(0, tk, grp):
                    cnt = cnt + hit[r0:r0 + grp]
                return cnt

            cnt = lax.fori_loop(0, nj, body, jnp.zeros((grp, tq), BF16))
            return jnp.sum(cnt.astype(F32), axis=0, keepdims=True)

        def keep_ties(src, dst, tie_b):
            def body(j, carry):
                plane_ref[dst, j] = jnp.where(plane_ref[src, j] == tie_b, plane_ref[dst, j], -one_b)
                return carry

            lax.fori_loop(0, nj, body, 0)

        def top_digit_float(p16):
            fb = (p16 & 0xFFFF) ^ jnp.where(p16 < 0, 0x7FFF, 0)
            return pltpu.bitcast(lax.shift_left(fb, 16), F32).astype(BF16)

        def top_iter(it, prefix):
            cand = prefix + lax.shift_left(jnp.int32(1), 15 - it)
            return jnp.where(count(0, top_digit_float(cand), False) >= kf, cand, prefix)

        d_top = lax.fori_loop(0, 16, top_iter, jnp.full((1, tq), -32768, I32))
        t_top = top_digit_float(d_top)
        above = count(0, t_top, True)

        def byte_digit(plane, above_n):
            def it_body(it, prefix):
                cand = prefix + lax.shift_left(jnp.int32(1), 7 - it)
                tot = above_n + count(plane, cand.astype(F32).astype(BF16), False)
                return jnp.where(tot >= kf, cand, prefix)

            return lax.fori_loop(0, 8, it_body, jnp.zeros((1, tq), I32))

        keep_ties(0, 1, t_top)
        d_mid = byte_digit(1, above)
        t_mid = d_mid.astype(F32).astype(BF16)
        above = above + count(1, t_mid, True)
        keep_ties(1, 2, t_mid)
        d_low = byte_digit(2, above)
        t_low = d_low.astype(F32).astype(BF16)
        thr = lax.shift_left(d_top, 16) | lax.shift_left(d_mid, 8) | d_low
        n_ge = above + count(2, t_low, False)
        tied = jnp.max(n_ge) > kf

        @pl.when(jnp.logical_not(tied))
        def _():
            thr_c = jnp.maximum(thr, KEY_NEG_INF + 1)

            def madd_chunk(j, carry):
                madd_ref[j] = jnp.where(key_ref[j] >= thr_c, 0.0, NEG_INF)
                return carry

            lax.fori_loop(0, nj, madd_chunk, 0)

        @pl.when(tied)
        def _():
            need = kf - (above + count(2, t_low, True))
            row = lax.broadcasted_iota(I32, (tk, tq), 0)

            def pos_iter(it, q_pos):
                cand = q_pos + lax.shift_left(jnp.int32(1), (nkc * tk).bit_length() - 2 - it)

                def body(j, cnt):
                    hit = jnp.where(key_ref[j] == thr, jnp.where(row + j * tk < cand, 1.0, 0.0), 0.0)
                    return cnt + jnp.sum(hit.reshape(tk // SUBLANES, SUBLANES, tq), axis=0)

                cnt = lax.fori_loop(0, nj, body, jnp.zeros((SUBLANES, tq), F32))
                return jnp.where(jnp.sum(cnt, axis=0, keepdims=True) < need, cand, q_pos)

            last = lax.fori_loop(0, (nkc * tk).bit_length() - 1, pos_iter, jnp.zeros((1, tq), I32))
            thr_c = jnp.maximum(thr, KEY_NEG_INF)

            def madd_chunk(j, carry):
                kk = key_ref[j]
                take_tie = jnp.where(row + j * tk <= last, jnp.where(kk > KEY_NEG_INF, 0.0, NEG_INF), NEG_INF)
                madd_ref[j] = jnp.where(kk > thr_c, 0.0, jnp.where(kk == thr, take_tie, NEG_INF))
                return carry

            lax.fori_loop(0, nj, madd_chunk, 0)

    _select()

    ng = ATT_KV_HEADS
    c0 = jnp.maximum(i - 1, 0)
    c1 = jnp.minimum(c0 + 1, nkc - 1)
    n_far = (c0 + 1) // 2

    def scores_to(slot, g, ca, madd_c, bias_c):
        rows = pl.ds(pl.multiple_of(ca * tk, tk), 2 * tk)
        qt = jnp.concatenate([qt_ref[0, g * REP + r] for r in range(REP)], axis=1)
        add = jnp.concatenate([madd_c if bias_c is None else madd_c + bias_c[r] for r in range(REP)], axis=1)
        s = _dot(k_ref[rows, g * HEAD_DIM:(g + 1) * HEAD_DIM], qt) + add
        s_buf[slot] = s
        st_ref[ng + 2 + slot:ng + 3 + slot, :] = jnp.max(s, axis=0, keepdims=True)

    def softmax_to(slot, g):
        m = st_ref[g:g + 1, :]
        m_new = jnp.maximum(m, st_ref[ng + 2 + slot:ng + 3 + slot, :])
        m_safe = jnp.where(m_new == NEG_INF, 0.0, m_new)
        alpha = jnp.exp2(m - m_safe)
        p_buf[slot] = jnp.exp2((s_buf[slot] - m_safe).astype(BF16))
        st_ref[g:g + 1, :] = m_new
        st_ref[ng + slot:ng + 1 + slot, :] = alpha

    def values_from(slot, g, ca, cb):
        vt_c = jnp.concatenate([vt_ref[0, g, ca], vt_ref[0, g, cb]], axis=1)
        acc_ref[g] = st_ref[ng + slot:ng + 1 + slot, :] * acc_ref[g] + _dot(vt_c, p_buf[slot])

    def near_scores_to(slot, g):
        first = i == 0
        t_a = jnp.where(first, 0, 1)
        idx_b = jnp.where(first, neg_slot, c0 + 1)
        madd_near = jnp.concatenate([madd_ref[c0], madd_ref[idx_b]], axis=0)
        bias_near = [jnp.concatenate([bias_ref[t_a, g * REP + r], bias_ref[0, g * REP + r]], axis=0)
                     for r in range(REP)]
        scores_to(slot, g, c0, madd_near, bias_near)

    def far_scores_to(slot, g, k):
        ca = jnp.minimum(2 * (k - 1), nkc - 2)
        cb = jnp.where(ca + 1 >= c0, neg_slot, ca + 1)
        scores_to(slot, g, ca, jnp.concatenate([madd_ref[ca], madd_ref[cb]], axis=0), None)

    st_ref[0:ng, :] = jnp.full((ng, REP * tq), NEG_INF, F32)
    acc_ref[...] = jnp.zeros_like(acc_ref)

    near_scores_to(0, 0)
    for g in range(ng):
        if g + 1 < ng:
            near_scores_to((g + 1) % 2, g + 1)
        else:
            far_scores_to(0, 0, 1)
        softmax_to(g % 2, g)
        values_from(g % 2, g, c0, c1)

    def far_step(k, carry):
        for g in range(ng):
            if g + 1 < ng:
                far_scores_to((g + 1) % 2, g + 1, k)
            else:
                far_scores_to(0, 0, k + 1)
            softmax_to(g % 2, g)
            values_from(g % 2, g, 2 * (k - 1), 2 * (k - 1) + 1)
        return carry

    lax.fori_loop(1, n_far + 1, far_step, 0)

    for g in range(ng):
        out = acc_ref[g, 0:HEAD_DIM, :] / acc_ref[g, HEAD_DIM:HEAD_DIM + 1, :]
        for r in range(REP):
            h = g * REP + r
            o_ref[:, h * HEAD_DIM:(h + 1) * HEAD_DIM] = out[:, r * tq:(r + 1) * tq].T.astype(o_ref.dtype)


def _dsa(qt, kn, vt, qit, wt, ki, rel_bias, batch, seq):
    tq, tk = DSA_TQ, DSA_TK
    nq = seq // tq
    nkc = seq // tk
    topk = min(TOPK_MAX, seq // 4)
    bidx = jnp.asarray(_bias_bucket_tiles(tq, tk))
    kern = functools.partial(_dsa_kernel, tq=tq, tk=tk, nkc=nkc, topk=topk)
    once = pl.Buffered(1)
    return pl.pallas_call(
        kern,
        grid=(batch, nq),
        in_specs=[pl.BlockSpec(memory_space=pltpu.SMEM),
                  pl.BlockSpec((2, tk, tq), lambda b, i: (0, 0, 0), pipeline_mode=once),
                  pl.BlockSpec((1, ATT_HEADS, HEAD_DIM, tq), lambda b, i: (b * nq + i, 0, 0, 0)),
                  pl.BlockSpec((1, IDX_Q, tq), lambda b, i: (b * nq + i, 0, 0)),
                  pl.BlockSpec((1, IDX_HEADS, tq), lambda b, i: (b * nq + i, 0, 0)),
                  pl.BlockSpec((seq, ATT_KV), lambda b, i: (b, 0), pipeline_mode=once),
                  pl.BlockSpec((1, ATT_KV_HEADS, nkc, VT_ROWS, tk), lambda b, i: (b, 0, 0, 0, 0),
                               pipeline_mode=once),
                  pl.BlockSpec((seq, IDX_DIM), lambda b, i: (b, 0), pipeline_mode=once)],
        out_specs=pl.BlockSpec((tq, ATT_Q), lambda b, i: (b * nq + i, 0)),
        out_shape=jax.ShapeDtypeStruct((batch * seq, ATT_Q), BF16),
        scratch_shapes=[pltpu.VMEM((nkc, tk, tq), I32),
                        pltpu.VMEM((3, nkc, tk, tq), BF16),
                        pltpu.VMEM((nkc + 1, tk, tq), F32),
                        pltpu.VMEM((2, ATT_HEADS, tk, tq), F32),
                        pltpu.VMEM((2, 2 * tk, REP * tq), F32),
                        pltpu.VMEM((2, 2 * tk, REP * tq), BF16),
                        pltpu.VMEM((ATT_KV_HEADS + 4, REP * tq), F32),
                        pltpu.VMEM((ATT_KV_HEADS, VT_ROWS, REP * tq), F32)],
        compiler_params=_params(("arbitrary", "arbitrary")),
        name="dsa_attention",
    )(rel_bias, bidx, qt, qit, wt, kn, vt, ki)


E_ROWS = 3 * CHUNK + BF16_ROWS
CONV_HALO = BF16_ROWS


def _conv_shift_matrix():
    ext = CONV_HALO + CHUNK
    s = np.zeros((CHUNK, CONV_WIDTH * ext), np.float32)
    for k in range(CONV_WIDTH):
        t = np.arange(CHUNK)
        s[t, k * ext + CONV_HALO + t - (CONV_WIDTH - 1) + k] = 1.0
    return s


SSD_INPUTS = 12


def _ssd_kernel(*refs, n_cast, n_side):
    (xbc0_ref, xbcn_ref, halon_ref, z_ref, sm_ref, shift_ref, cw_ref, cb_ref, dtb_ref, alog_ref, dsk_ref,
     ng_ref) = refs[:SSD_INPUTS]
    side_in = refs[SSD_INPUTS:SSD_INPUTS + n_side]
    o_ref = refs[SSD_INPUTS + n_side]
    side_out = refs[SSD_INPUTS + n_side + 1:SSD_INPUTS + 2 * n_side + 1]
    xs_ref, bm_ref, cm_ref, state_ref, ypre_ref, acg_ref, actg_ref = refs[SSD_INPUTS + 2 * n_side + 1:]
    c = pl.program_id(1)
    L = CHUNK
    _side_cast(pl.program_id(0) * pl.num_programs(1) + c, n_cast, side_in, side_out)

    def conv_to(slot, x_ref, halo):
        shift = shift_ref[...]
        cblk = GROUP_W
        for cbi in range(CONV_DIM // cblk):
            cols = slice(cbi * cblk, (cbi + 1) * cblk)
            ext = jnp.concatenate([halo[:, cols], x_ref[:, cols]], axis=0)
            wtap = cw_ref[:, cols].astype(BF16)
            prods = jnp.concatenate([ext * wtap[kk:kk + 1, :] for kk in range(CONV_WIDTH)], axis=0)
            y = _silu(cb_ref[:, cols] + _dot(shift, prods))
            if cbi < SSM_GROUPS:
                xs_ref[slot, cbi] = y
            else:
                per = cblk // SSM_STATE
                for u in range(per):
                    gi = (cbi - SSM_GROUPS) * per + u
                    piece = y[:, u * SSM_STATE:(u + 1) * SSM_STATE]
                    if gi < SSM_GROUPS:
                        bm_ref[slot, gi] = piece
                    else:
                        cm_ref[slot, gi - SSM_GROUPS] = piece

    def scan_chunk(slot):
        _ssd_scan_chunk(slot, z_ref, sm_ref, dtb_ref, alog_ref, dsk_ref, ng_ref, o_ref,
                        xs_ref, bm_ref, cm_ref, state_ref, ypre_ref, acg_ref, actg_ref)

    @pl.when(c == 0)
    def _():
        state_ref[...] = jnp.zeros_like(state_ref)
        conv_to(0, xbc0_ref, jnp.zeros((CONV_HALO, CONV_DIM), BF16))

    @pl.when(c % 2 == 0)
    def _():
        conv_to(1, xbcn_ref, halon_ref[...])
        scan_chunk(0)

    @pl.when(c % 2 == 1)
    def _():
        conv_to(0, xbcn_ref, halon_ref[...])
        scan_chunk(1)


def _ssd_scan_chunk(slot, z_ref, sm_ref, dtb_ref, alog_ref, dsk_ref, ng_ref, o_ref,
                    xs_ref, bm_ref, cm_ref, state_ref, ypre_ref, acg_ref, actg_ref):
    L = CHUNK
    dt_in = sm_ref[:, S_DT:S_DT + SSM_HEADS] + dtb_ref[...]
    dt_act = jnp.maximum(dt_in, 0.0) + jnp.log1p(jnp.exp(-jnp.abs(dt_in)))
    a = dt_act * (-jnp.exp(alog_ref[...]))
    ri = lax.broadcasted_iota(I32, (L, L), 0)
    ci = lax.broadcasted_iota(I32, (L, L), 1)
    tril = ri >= ci
    tri_b = jnp.where(tril, 1.0, 0.0).astype(BF16)
    a_cum = sum(_dot(tri_b, p) for p in _split3(a))
    eye_b = jnp.where(lax.broadcasted_iota(I32, (SSM_HEADS, SSM_HEADS), 0)
                      == lax.broadcasted_iota(I32, (SSM_HEADS, SSM_HEADS), 1), 1.0, 0.0).astype(BF16)
    a_cum_t = sum(_dot_nt(eye_b, p) for p in _split3(a_cum))
    a_last = a_cum[L - 1:L, :]
    for gi in range(SSM_GROUPS):
        acg_ref[gi] = a_cum[:, gi * HEADS_PER_GROUP:(gi + 1) * HEADS_PER_GROUP]
        actg_ref[gi] = a_cum_t[gi * HEADS_PER_GROUP:(gi + 1) * HEADS_PER_GROUP, :]
    cd3 = _split3(jnp.exp(a_last))
    ds3 = _split3(dsk_ref[...])
    extras = jnp.concatenate([p.astype(F32) for p in cd3 + ds3]
                             + [jnp.zeros((E_ROWS - 3 * L - 6, SSM_HEADS), F32)], axis=0)
    e_mat = jnp.concatenate([dt_act, jnp.exp(a_cum), jnp.exp(a_last - a_cum), extras], axis=0).astype(BF16)

    lane = lax.broadcasted_iota(I32, (L, LANES), 1)
    lo_mask = lane < SSM_HEAD_DIM

    def group_body(gi, carry):
        xs = xs_ref[slot, gi]
        bg = bm_ref[slot, gi]
        cg_b = cm_ref[slot, gi].astype(BF16)
        hsel = (lax.broadcasted_iota(I32, (SSM_HEADS, GROUP_W), 0)
                == gi * HEADS_PER_GROUP + lax.broadcasted_iota(I32, (SSM_HEADS, GROUP_W), 1) // SSM_HEAD_DIM)
        ex = _dot(e_mat, jnp.where(hsel, 1.0, 0.0).astype(BF16))
        dt_rep = ex[0:L]
        expa_rep = ex[L:2 * L]
        dte_rep = ex[2 * L:3 * L]
        cd_rep = ex[3 * L:3 * L + 1] + ex[3 * L + 1:3 * L + 2] + ex[3 * L + 2:3 * L + 3]
        dsk_rep = ex[3 * L + 3:3 * L + 4] + ex[3 * L + 4:3 * L + 5] + ex[3 * L + 5:3 * L + 6]

        xd = xs * dt_rep
        xd_b = xd.astype(BF16)
        cb = _dot_nt(cg_b, bg.astype(BF16))
        acg = acg_ref[gi]
        actg = actg_ref[gi]
        pairs = []
        for pj in range(HEADS_PER_GROUP // 2):
            gmat = []
            for e in (2 * pj, 2 * pj + 1):
                seg = acg[:, e:e + 1] - actg[e:e + 1, :]
                dec = jnp.exp(jnp.where(tril, seg, NEG_INF))
                gmat.append((cb * dec).astype(BF16))
            xp = xd_b[:, pj * LANES:(pj + 1) * LANES]
            zero = jnp.zeros_like(xp)
            pairs.append(_dot(gmat[0], jnp.where(lo_mask, xp, zero))
                         + _dot(gmat[1], jnp.where(lo_mask, zero, xp)))
        y_diag = jnp.concatenate(pairs, axis=1)

        st = state_ref[gi]
        y_off = _dot(cg_b, st.astype(BF16)) * expa_rep
        xdd = (xd * dte_rep).astype(BF16)
        state_ref[gi] = st * cd_rep + _dot(bg.T.astype(BF16), xdd)
        ypre_ref[gi] = y_diag + y_off + dsk_rep * xs
        return carry

    lax.fori_loop(0, SSM_GROUPS, group_body, 0, unroll=8)

    for gi in range(SSM_GROUPS):
        cols = slice(gi * GROUP_W, (gi + 1) * GROUP_W)
        zz = z_ref[:, cols].astype(F32)
        y = ypre_ref[gi] * _silu(zz)
        ms = jnp.mean(y * y, axis=-1, keepdims=True)
        o_ref[:, cols] = (y * lax.rsqrt(ms + EPS) * ng_ref[:, cols]).astype(o_ref.dtype)


def _ssd(main, small, conv_w, conv_b, dt_bias, a_log, d_skip, norm_g, side_weights, batch, seq):
    nc = seq // CHUNK
    n_cast, side_specs = _side_cast_specs(side_weights, batch * nc, lambda b, c: b * nc + c)
    hb = CHUNK // CONV_HALO
    row = lambda b, c: b * nc + c
    full = lambda shape: pl.BlockSpec(shape, lambda b, c: (0,) * len(shape))
    shift = jnp.asarray(_conv_shift_matrix(), BF16)
    outs = pl.pallas_call(
        functools.partial(_ssd_kernel, n_cast=n_cast, n_side=len(side_weights)),
        grid=(batch, nc),
        in_specs=[pl.BlockSpec((CHUNK, CONV_DIM), lambda b, c: (row(b, 0), M_XBC // CONV_DIM)),
                  pl.BlockSpec((CHUNK, CONV_DIM),
                               lambda b, c: (row(b, jnp.minimum(c + 1, nc - 1)), M_XBC // CONV_DIM)),
                  pl.BlockSpec((CONV_HALO, CONV_DIM),
                               lambda b, c: (row(b, c) * hb + hb - 1, M_XBC // CONV_DIM)),
                  pl.BlockSpec((CHUNK, SSM_INNER), lambda b, c: (row(b, c), M_Z // SSM_INNER)),
                  pl.BlockSpec((CHUNK, S_END), lambda b, c: (row(b, c), 0)),
                  full((CHUNK, CONV_WIDTH * (CONV_HALO + CHUNK))),
                  full((CONV_WIDTH, CONV_DIM)), full((1, CONV_DIM)), full((1, SSM_HEADS)),
                  full((1, SSM_HEADS)), full((1, SSM_HEADS)), full((1, SSM_INNER))] + side_specs,
        out_specs=[pl.BlockSpec((CHUNK, SSM_INNER), lambda b, c: (row(b, c), 0))] + side_specs,
        out_shape=[jax.ShapeDtypeStruct((batch * seq, SSM_INNER), BF16)]
                  + [jax.ShapeDtypeStruct(w.shape, BF16) for w in side_weights],
        scratch_shapes=[pltpu.VMEM((2, SSM_GROUPS, CHUNK, GROUP_W), F32),
                        pltpu.VMEM((2, SSM_GROUPS, CHUNK, SSM_STATE), F32),
                        pltpu.VMEM((2, SSM_GROUPS, CHUNK, SSM_STATE), F32),
                        pltpu.VMEM((SSM_GROUPS, SSM_STATE, GROUP_W), F32),
                        pltpu.VMEM((SSM_GROUPS, CHUNK, GROUP_W), F32),
                        pltpu.VMEM((SSM_GROUPS, CHUNK, HEADS_PER_GROUP), F32),
                        pltpu.VMEM((SSM_GROUPS, HEADS_PER_GROUP, CHUNK), F32)],
        compiler_params=_params(("arbitrary", "arbitrary")),
        name="ssd_scan",
    )(main, main, main, main, small, shift, conv_w, conv_b, dt_bias, a_log, d_skip, norm_g, *side_weights)
    return outs[0], outs[1:]


def _merge_kernel(att_ref, y_ref, ga_ref, gs_ref, wa_ref, ws_ref, o_ref):
    pa = _dot(att_ref[...], wa_ref[...])
    ps = _dot(y_ref[...], ws_ref[...])
    o_ref[...] = (_sigmoid(ga_ref[...].astype(F32)) * pa + _sigmoid(gs_ref[...].astype(F32)) * ps).astype(o_ref.dtype)


def _merge(att, y, main, wa, ws, tm, tn):
    t = att.shape[0]
    return pl.pallas_call(
        _merge_kernel,
        grid=(t // tm, D_MODEL // tn),
        in_specs=[pl.BlockSpec((tm, ATT_Q), lambda i, j: (i, 0)),
                  pl.BlockSpec((tm, SSM_INNER), lambda i, j: (i, 0)),
                  pl.BlockSpec((tm, tn), lambda i, j: (i, M_GA // tn + j)),
                  pl.BlockSpec((tm, tn), lambda i, j: (i, M_GS // tn + j)),
                  pl.BlockSpec((ATT_Q, tn), lambda i, j: (0, j)),
                  pl.BlockSpec((SSM_INNER, tn), lambda i, j: (0, j))],
        out_specs=pl.BlockSpec((tm, tn), lambda i, j: (i, j)),
        out_shape=jax.ShapeDtypeStruct((t, D_MODEL), BF16),
        compiler_params=_params(("parallel", "arbitrary")),
        name="gated_merge",
    )(att, y, main, main, wa, ws)


def _out_proj_kernel(m_ref, w_ref, x_ref, g_ref, x1_ref, h2_ref):
    x1 = x_ref[...] + _dot(m_ref[...], w_ref[...])
    x1_ref[...] = x1
    ms = jnp.mean(x1 * x1, axis=-1, keepdims=True)
    h2_ref[...] = (x1 * lax.rsqrt(ms + EPS) * g_ref[...]).astype(BF16)


def _out_proj(merged, w, x, g, tm):
    t = x.shape[0]
    return pl.pallas_call(
        _out_proj_kernel,
        grid=(t // tm,),
        in_specs=[pl.BlockSpec((tm, D_MODEL), lambda i: (i, 0)),
                  pl.BlockSpec((D_MODEL, D_MODEL), lambda i: (0, 0)),
                  pl.BlockSpec((tm, D_MODEL), lambda i: (i, 0)),
                  pl.BlockSpec((1, D_MODEL), lambda i: (0, 0))],
        out_specs=[pl.BlockSpec((tm, D_MODEL), lambda i: (i, 0)),
                   pl.BlockSpec((tm, D_MODEL), lambda i: (i, 0))],
        out_shape=[jax.ShapeDtypeStruct((t, D_MODEL), F32),
                   jax.ShapeDtypeStruct((t, D_MODEL), BF16)],
        compiler_params=_params(("parallel",)),
        name="out_proj_norm",
    )(merged, w, x, g)


def _mlp_kernel(h_ref, x1_ref, wu_ref, wd_ref, o_ref):
    @pl.when(pl.program_id(1) == 0)
    def _():
        o_ref[...] = x1_ref[...]

    u = _dot(h_ref[...], wu_ref[...])
    u = jnp.square(jnp.maximum(u, 0.0)).astype(BF16)
    o_ref[...] += _dot(u, wd_ref[...])


def _mlp(h2, x1, wu, wd, tm, th):
    t = h2.shape[0]
    return pl.pallas_call(
        _mlp_kernel,
        grid=(t // tm, MLP_HIDDEN // th),
        in_specs=[pl.BlockSpec((tm, D_MODEL), lambda i, j: (i, 0)),
                  pl.BlockSpec((tm, D_MODEL), lambda i, j: (i, 0)),
                  pl.BlockSpec((D_MODEL, th), lambda i, j: (0, j)),
                  pl.BlockSpec((th, D_MODEL), lambda i, j: (j, 0))],
        out_specs=pl.BlockSpec((tm, D_MODEL), lambda i, j: (i, 0)),
        out_shape=jax.ShapeDtypeStruct((t, D_MODEL), F32),
        compiler_params=_params(("parallel", "arbitrary")),
        name="relu2_mlp",
    )(h2, x1, wu, wd)


def _pack_w_small(w_t):
    assert (O_GA, O_KI, O_Z, O_DT) == (M_GA, M_Z, O_WI + IDX_HEADS, O_Z + M_END - M_Z)
    return jnp.concatenate([w_t[O_KI:O_Z], w_t[O_DT:O_DT + SSM_HEADS],
                            jnp.zeros((S_END - S_DT - SSM_HEADS, w_t.shape[1]), w_t.dtype)], axis=0)


def _block(x2, batch, seq, norm1_g, w_in, conv_w, conv_b, dt_bias, a_log, d_skip, ssm_norm_g, q_norm_g,
           k_norm_g, rel_bias, w_att_branch, w_ssm_branch, w_out, norm2_g, w_up, w_down):
    row = lambda v: v.reshape(1, -1)
    w_t = w_in.T
    xn, small = _norm_small(x2, row(norm1_g), _pack_w_small(w_t), tm=min(1024, x2.shape[0]))
    main = _in_proj(xn, w_t, tm=min(1024, x2.shape[0]))
    qt, kn, vt, qit, wt, ki = _qk_prep(main, small, row(q_norm_g), row(k_norm_g), batch, seq, tq=DSA_TQ)
    att = _dsa(qt, kn, vt, qit, wt, ki, rel_bias, batch, seq)
    y, (w_att_b, w_ssm_b, w_out_b, w_up_b, w_down_b) = _ssd(
        main, small, conv_w, row(conv_b), row(dt_bias), row(a_log), row(d_skip), row(ssm_norm_g),
        (w_att_branch, w_ssm_branch, w_out, w_up, w_down), batch, seq)
    merged = _merge(att, y, main, w_att_b, w_ssm_b, tm=512, tn=1024)
    x1, h2 = _out_proj(merged, w_out_b, x2, row(norm2_g), tm=512)
    return _mlp(h2, x1, w_up_b, w_down_b, tm=512, th=1024)


def kernel(x, norm1_g, w_in, conv_w, conv_b, dt_bias, a_log, d_skip, ssm_norm_g, q_norm_g, k_norm_g, rel_bias,
           w_att_branch, w_ssm_branch, w_out, norm2_g, w_up, w_down):
    batch, seq, d = x.shape
    x2 = x.reshape(batch * seq, d)
    for l in range(norm1_g.shape[0]):
        x2 = _block(x2, batch, seq, norm1_g[l], w_in[l], conv_w[l], conv_b[l], dt_bias[l], a_log[l], d_skip[l],
                    ssm_norm_g[l], q_norm_g[l], k_norm_g[l], rel_bias, w_att_branch[l], w_ssm_branch[l],
                    w_out[l], norm2_g[l], w_up[l], w_down[l])
    return x2.reshape(batch, seq, d)
```

```python
import functools
import math

import numpy as np
import jax
import jax.numpy as jnp
from jax import lax
from jax.experimental import pallas as pl
from jax.experimental.pallas import tpu as pltpu

F32 = jnp.float32
BF16 = jnp.bfloat16
I32 = jnp.int32

D_MODEL = 2048
ATT_HEADS = 16
ATT_KV_HEADS = 4
HEAD_DIM = 128
REP = ATT_HEADS // ATT_KV_HEADS
IDX_HEADS = 16
IDX_DIM = 64
TOPK_MAX = 256
N_BUCKETS = 32
MAX_DISTANCE = 128
SSM_INNER = 2 * D_MODEL
SSM_HEAD_DIM = 64
SSM_HEADS = SSM_INNER // SSM_HEAD_DIM
SSM_GROUPS = 8
SSM_STATE = 128
CONV_WIDTH = 4
CHUNK = 128
MLP_HIDDEN = 4 * D_MODEL
EPS = 1e-6

ATT_Q = ATT_HEADS * HEAD_DIM
ATT_KV = ATT_KV_HEADS * HEAD_DIM
IDX_Q = IDX_HEADS * IDX_DIM
SSM_BC = SSM_GROUPS * SSM_STATE
CONV_DIM = SSM_INNER + 2 * SSM_BC
SPLITS = (D_MODEL, D_MODEL, ATT_Q, ATT_KV, ATT_KV, IDX_Q, IDX_DIM, IDX_HEADS, SSM_INNER, CONV_DIM, SSM_HEADS)
_OFFS = tuple(int(v) for v in np.cumsum((0,) + SPLITS))
(O_GA, O_GS, O_Q, O_K, O_V, O_QI, O_KI, O_WI, O_Z, O_XBC, O_DT, _O_END) = _OFFS

M_GA = 0
M_GS = M_GA + D_MODEL
M_Q = M_GS + D_MODEL
M_K = M_Q + ATT_Q
M_V = M_K + ATT_KV
M_QI = M_V + ATT_KV
M_Z = M_QI + IDX_Q
M_XBC = M_Z + SSM_INNER
M_END = M_XBC + CONV_DIM
S_KI = 0
S_WI = S_KI + IDX_DIM
S_DT = S_WI + IDX_HEADS
S_END = 256

HEADS_PER_GROUP = SSM_HEADS // SSM_GROUPS
GROUP_W = HEADS_PER_GROUP * SSM_HEAD_DIM

LANES = 128
SUBLANES = 8
BF16_ROWS = 2 * SUBLANES
VMEM_LIMIT = 56 * 1024 * 1024

NEG_INF = float("-inf")
INT_MIN = -(2 ** 31)
KEY_NEG_INF = int(np.int32(np.uint32(0xFF800000) ^ np.uint32(0x7FFFFFFF)))


def _dot(a, b):
    return jnp.dot(a, b, preferred_element_type=F32)


def _dot_nt(a, b):
    return lax.dot_general(a, b, (((1,), (1,)), ((), ())), preferred_element_type=F32)


def _split3(x):
    hi = x.astype(BF16)
    r = x - hi.astype(F32)
    mid = r.astype(BF16)
    lo = (r - mid.astype(F32)).astype(BF16)
    return hi, mid, lo


def _silu(x):
    h = 0.5 * x
    return h + h * jnp.tanh(h)


def _sigmoid(x):
    return 0.5 + 0.5 * jnp.tanh(0.5 * x)


def _params(sem):
    return pltpu.CompilerParams(dimension_semantics=sem, vmem_limit_bytes=VMEM_LIMIT)


def _norm_small_kernel(x_ref, g_ref, w_ref, xn_ref, sm_ref):
    x = x_ref[...]
    ms = jnp.mean(x * x, axis=-1, keepdims=True)
    xn = (x * lax.rsqrt(ms + EPS) * g_ref[...]).astype(BF16)
    xn_ref[...] = xn
    sm_ref[...] = _dot_nt(xn, w_ref[...].astype(BF16))


def _norm_small(x, g, w_small_t, tm):
    m, d = x.shape
    n = w_small_t.shape[0]
    return pl.pallas_call(
        _norm_small_kernel,
        grid=(m // tm,),
        in_specs=[pl.BlockSpec((tm, d), lambda i: (i, 0)),
                  pl.BlockSpec((1, d), lambda i: (0, 0)),
                  pl.BlockSpec((n, d), lambda i: (0, 0))],
        out_specs=[pl.BlockSpec((tm, d), lambda i: (i, 0)),
                   pl.BlockSpec((tm, n), lambda i: (i, 0))],
        out_shape=[jax.ShapeDtypeStruct((m, d), BF16), jax.ShapeDtypeStruct((m, n), F32)],
        compiler_params=_params(("parallel",)),
        name="norm_small_proj",
    )(x, g, w_small_t)


IN_TN = 1024
IN_ALIGNED_TILES = O_KI // IN_TN
IN_SHIFT = O_Z - M_Z
IN_NEXT = 128
IN_ROWS = 256


def _in_proj_kernel(xn_ref, wa_ref, wn_ref, o_ref, w_scr):
    j = pl.program_id(0)
    i = pl.program_id(1)

    @pl.when((i == 0) & (j < IN_ALIGNED_TILES))
    def _():
        for r0 in range(0, IN_TN, IN_ROWS):
            w_scr[r0:r0 + IN_ROWS, :] = wa_ref[r0:r0 + IN_ROWS, :].astype(BF16)

    @pl.when((i == 0) & (j >= IN_ALIGNED_TILES))
    def _():
        for r0 in range(0, IN_TN - IN_SHIFT, IN_ROWS):
            r1 = min(r0 + IN_ROWS, IN_TN - IN_SHIFT)
            w_scr[r0:r1, :] = wa_ref[r0 + IN_SHIFT:r1 + IN_SHIFT, :].astype(BF16)
        w_scr[IN_TN - IN_SHIFT:IN_TN, :] = wn_ref[0:IN_SHIFT, :].astype(BF16)

    o_ref[...] = _dot_nt(xn_ref[...], w_scr[...]).astype(o_ref.dtype)


def _in_proj(xn, w_t, tm):
    m, d = xn.shape
    assert O_KI % IN_TN == 0 and M_END % IN_TN == 0 and IN_TN % IN_NEXT == 0
    assert 0 < IN_SHIFT <= IN_NEXT and IN_SHIFT % BF16_ROWS == 0
    return pl.pallas_call(
        _in_proj_kernel,
        grid=(M_END // IN_TN, m // tm),
        in_specs=[pl.BlockSpec((tm, d), lambda j, i: (i, 0)),
                  pl.BlockSpec((IN_TN, d), lambda j, i: (j, 0)),
                  pl.BlockSpec((IN_NEXT, d), lambda j, i: ((j + 1) * (IN_TN // IN_NEXT), 0))],
        out_specs=pl.BlockSpec((tm, IN_TN), lambda j, i: (i, j)),
        out_shape=jax.ShapeDtypeStruct((m, M_END), BF16),
        scratch_shapes=[pltpu.VMEM((IN_TN, d), BF16)],
        compiler_params=_params(("arbitrary", "arbitrary")),
        name="in_proj",
    )(xn, w_t, w_t)


def _side_cast_specs(weights, n_steps, step_of):
    n_cast = 1 << (n_steps.bit_length() - 1)
    specs = []
    for w in weights:
        rows = w.shape[0] // n_cast
        assert w.shape[0] % n_cast == 0 and rows % BF16_ROWS == 0
        specs.append(pl.BlockSpec((rows, w.shape[1]), lambda *ids: (jnp.minimum(step_of(*ids), n_cast - 1), 0)))
    return n_cast, specs


def _side_cast(step, n_cast, srcs, dsts):
    @pl.when(step < n_cast)
    def _():
        for src, dst in zip(srcs, dsts):
            dst[...] = src[...].astype(BF16)


LOG2E = math.log2(math.e)
VT_ROWS = HEAD_DIM + BF16_ROWS


def _qk_prep_kernel(q_ref, k_ref, v_ref, qi_ref, sm_ref, qg_ref, kg_ref,
                    qt_ref, kn_ref, vt_ref, qit_ref, wt_ref, ki_ref):
    qg = qg_ref[...]
    for h in range(ATT_HEADS):
        x = q_ref[:, h * HEAD_DIM:(h + 1) * HEAD_DIM].astype(F32)
        ms = jnp.mean(x * x, axis=-1, keepdims=True)
        y = x * lax.rsqrt(ms + EPS) * qg * (HEAD_DIM ** -0.5 * LOG2E)
        qt_ref[0, h] = y.T.astype(BF16)
    kg = kg_ref[...]
    for h in range(ATT_KV_HEADS):
        x = k_ref[:, h * HEAD_DIM:(h + 1) * HEAD_DIM].astype(F32)
        ms = jnp.mean(x * x, axis=-1, keepdims=True)
        kn_ref[:, h * HEAD_DIM:(h + 1) * HEAD_DIM] = (x * lax.rsqrt(ms + EPS) * kg).astype(BF16)
        vt_ref[0, h, 0, 0:HEAD_DIM, :] = v_ref[:, h * HEAD_DIM:(h + 1) * HEAD_DIM].astype(F32).T.astype(BF16)
        vt_ref[0, h, 0, HEAD_DIM:VT_ROWS, :] = jnp.ones((VT_ROWS - HEAD_DIM, v_ref.shape[0]), BF16)
    for p in range(IDX_Q // LANES):
        qit_ref[0, p * LANES:(p + 1) * LANES, :] = qi_ref[:, p * LANES:(p + 1) * LANES].astype(F32).T.astype(BF16)
    sm_t = sm_ref[:, 0:LANES].T
    wt_ref[0] = sm_t[S_WI:S_WI + IDX_HEADS, :] * (IDX_HEADS ** -0.5 * IDX_DIM ** -0.5)
    ki_ref[...] = sm_ref[:, S_KI:S_KI + IDX_DIM].astype(BF16)


def _qk_prep(main, small, qg, kg, batch, seq, tq):
    t = main.shape[0]
    nq = seq // tq
    return pl.pallas_call(
        _qk_prep_kernel,
        grid=(batch, nq),
        in_specs=[pl.BlockSpec((tq, ATT_Q), lambda b, i: (b * nq + i, M_Q // ATT_Q)),
                  pl.BlockSpec((tq, ATT_KV), lambda b, i: (b * nq + i, M_K // ATT_KV)),
                  pl.BlockSpec((tq, ATT_KV), lambda b, i: (b * nq + i, M_V // ATT_KV)),
                  pl.BlockSpec((tq, IDX_Q), lambda b, i: (b * nq + i, M_QI // IDX_Q)),
                  pl.BlockSpec((tq, S_END), lambda b, i: (b * nq + i, 0)),
                  pl.BlockSpec((1, HEAD_DIM), lambda b, i: (0, 0)),
                  pl.BlockSpec((1, HEAD_DIM), lambda b, i: (0, 0))],
        out_specs=[pl.BlockSpec((1, ATT_HEADS, HEAD_DIM, tq), lambda b, i: (b * nq + i, 0, 0, 0)),
                   pl.BlockSpec((tq, ATT_KV), lambda b, i: (b * nq + i, 0)),
                   pl.BlockSpec((1, ATT_KV_HEADS, 1, VT_ROWS, tq), lambda b, i: (b, 0, i, 0, 0)),
                   pl.BlockSpec((1, IDX_Q, tq), lambda b, i: (b * nq + i, 0, 0)),
                   pl.BlockSpec((1, IDX_HEADS, tq), lambda b, i: (b * nq + i, 0, 0)),
                   pl.BlockSpec((tq, IDX_DIM), lambda b, i: (b * nq + i, 0))],
        out_shape=[jax.ShapeDtypeStruct((t // tq, ATT_HEADS, HEAD_DIM, tq), BF16),
                   jax.ShapeDtypeStruct((t, ATT_KV), BF16),
                   jax.ShapeDtypeStruct((batch, ATT_KV_HEADS, nq, VT_ROWS, tq), BF16),
                   jax.ShapeDtypeStruct((t // tq, IDX_Q, tq), BF16),
                   jax.ShapeDtypeStruct((t // tq, IDX_HEADS, tq), F32),
                   jax.ShapeDtypeStruct((t, IDX_DIM), BF16)],
        compiler_params=_params(("parallel", "parallel")),
        name="qk_prep",
    )(main, main, main, main, small, qg, kg)


DSA_TQ = 256
DSA_TK = 256
CNT_WAYS = 4


def _t5_bucket_np(dist):
    n = np.maximum(dist, 0)
    max_exact = N_BUCKETS // 2
    nf = np.maximum(n, 1).astype(np.float32)
    ratio = (np.log(nf / np.float32(max_exact)) / np.float32(math.log(MAX_DISTANCE / max_exact))
             * np.float32(N_BUCKETS - max_exact))
    large = max_exact + ratio.astype(np.int32)
    large = np.minimum(large, N_BUCKETS - 1)
    return np.where(n < max_exact, n, large).astype(np.int32)


def _bias_bucket_tiles(tq, tk):
    r = np.arange(tq)[None, :]
    c = np.arange(tk)[:, None]
    d0 = _t5_bucket_np(r - c)
    d1 = _t5_bucket_np(tk + r - c)
    assert np.all(_t5_bucket_np(np.arange(tk + 1, 8 * tk)) == N_BUCKETS - 1)
    return np.stack([d0, d1]).astype(np.int32)


def _dsa_kernel(relb_ref, bidx_ref, qt_ref, qit_ref, wt_ref, k_ref, vt_ref, ki_ref, o_ref,
                key_ref, plane_ref, madd_ref, bias_ref, s_buf, p_buf, st_ref, acc_ref, *, tq, tk, nkc, topk):
    b = pl.program_id(0)
    i = pl.program_id(1)
    neg_slot = nkc

    @pl.when((b == 0) & (i == 0))
    def _init():
        madd_ref[neg_slot] = jnp.full((tk, tq), NEG_INF, F32)
        for t in range(2):
            bt = bidx_ref[t]

            def head_body(h, carry):
                far = relb_ref[N_BUCKETS - 1, h]

                def bucket_body(bk, acc):
                    return jnp.where(bt == bk, (relb_ref[bk, h] - far) * LOG2E, acc)

                bias_ref[t, h] = lax.fori_loop(0, N_BUCKETS, bucket_body, jnp.zeros((tk, tq), F32))
                return carry

            lax.fori_loop(0, ATT_HEADS, head_body, 0)

    def _select():
        nj = i + 1
        qpos = i * tq + lax.broadcasted_iota(I32, (tk, tq), 1)

        def score_chunk(j):
            kc = ki_ref[pl.ds(pl.multiple_of(j * tk, tk), tk), :]
            acc = jnp.zeros((tk, tq), F32)
            for h in range(IDX_HEADS):
                z = _dot(kc, qit_ref[0, h * IDX_DIM:(h + 1) * IDX_DIM, :])
                acc = acc + jnp.maximum(z, 0.0) * wt_ref[0, h:h + 1, :]
            kpos = j * tk + lax.broadcasted_iota(I32, (tk, tq), 0)
            acc = jnp.where(kpos <= qpos, acc, NEG_INF)
            bits = pltpu.bitcast(acc, I32)
            bits = jnp.where(bits == INT_MIN, 0, bits)
            key = bits ^ ((bits >> 31) & 0x7FFFFFFF)
            key_ref[j] = key
            plane_ref[0, j] = pltpu.bitcast(bits & jnp.int32(-65536), F32).astype(BF16)
            plane_ref[1, j] = ((key >> 8) & 0xFF).astype(F32).astype(BF16)
            plane_ref[2, j] = (key & 0xFF).astype(F32).astype(BF16)

        def score_quad(u, carry):
            for v in range(4):
                score_chunk(4 * u + v)
            return carry

        lax.fori_loop(0, nj // 4, score_quad, 0)
        done = (nj // 4) * 4

        @pl.when(nj % 4 >= 2)
        def _():
            score_chunk(done)
            score_chunk(done + 1)

        @pl.when(nj % 2 == 1)
        def _():
            score_chunk(nj - 1)

        kf = float(topk)
        one_b = jnp.ones((tk, tq), BF16)
        zero_b = jnp.zeros((tk, tq), BF16)
        grp = BF16_ROWS * CNT_WAYS

        def count(plane, cand_b, strict):
            def body(j, cnt):
                pv = plane_ref[plane, j]
                hit = jnp.where(pv > cand_b if strict else pv >= cand_b, one_b, zero_b)
                for r0 in range(0, tk, grp):
                    cnt = cnt + hit[r0:r0 + grp]
                return cnt

            cnt = lax.fori_loop(0, nj, body, jnp.zeros((grp, tq), BF16))
            return jnp.sum(cnt.astype(F32), axis=0, keepdims=True)

        def keep_ties(src, dst, tie_b):
            def body(j, carry):
                plane_ref[dst, j] = jnp.where(plane_ref[src, j] == tie_b, plane_ref[dst, j], -one_b)
                return carry

            lax.fori_loop(0, nj, body, 0)

        def top_digit_float(p16):
            fb = (p16 & 0xFFFF) ^ jnp.where(p16 < 0, 0x7FFF, 0)
            return pltpu.bitcast(lax.shift_left(fb, 16), F32).astype(BF16)

        def top_iter(it, prefix):
            cand = prefix + lax.shift_left(jnp.int32(1), 15 - it)
            return jnp.where(count(0, top_digit_float(cand), False) >= kf, cand, prefix)

        d_top = lax.fori_loop(0, 16, top_iter, jnp.full((1, tq), -32768, I32))
        t_top = top_digit_float(d_top)
        above = count(0, t_top, True)

        def byte_digit(plane, above_n):
            def it_body(it, prefix):
                cand = prefix + lax.shift_left(jnp.int32(1), 7 - it)
                tot = above_n + count(plane, cand.astype(F32).astype(BF16), False)
                return jnp.where(tot >= kf, cand, prefix)

            return lax.fori_loop(0, 8, it_body, jnp.zeros((1, tq), I32))

        keep_ties(0, 1, t_top)
        d_mid = byte_digit(1, above)
        t_mid = d_mid.astype(F32).astype(BF16)
        above = above + count(1, t_mid, True)
        keep_ties(1, 2, t_mid)
        d_low = byte_digit(2, above)
        t_low = d_low.astype(F32).astype(BF16)
        thr = lax.shift_left(d_top, 16) | lax.shift_left(d_mid, 8) | d_low
        n_ge = above + count(2, t_low, False)
        tied = jnp.max(n_ge) > kf

        @pl.when(jnp.logical_not(tied))
        def _():
            thr_c = jnp.maximum(thr, KEY_NEG_INF + 1)

            def madd_chunk(j, carry):
                madd_ref[j] = jnp.where(key_ref[j] >= thr_c, 0.0, NEG_INF)
                return carry

            lax.fori_loop(0, nj, madd_chunk, 0)

        @pl.when(tied)
        def _():
            need = kf - (above + count(2, t_low, True))
            row = lax.broadcasted_iota(I32, (tk, tq), 0)

            def pos_iter(it, q_pos):
                cand = q_pos + lax.shift_left(jnp.int32(1), (nkc * tk).bit_length() - 2 - it)

                def body(j, cnt):
                    hit = jnp.where(key_ref[j] == thr, jnp.where(row + j * tk < cand, 1.0, 0.0), 0.0)
                    return cnt + jnp.sum(hit.reshape(tk // SUBLANES, SUBLANES, tq), axis=0)

                cnt = lax.fori_loop(0, nj, body, jnp.zeros((SUBLANES, tq), F32))
                return jnp.where(jnp.sum(cnt, axis=0, keepdims=True) < need, cand, q_pos)

            last = lax.fori_loop(0, (nkc * tk).bit_length() - 1, pos_iter, jnp.zeros((1, tq), I32))
            thr_c = jnp.maximum(thr, KEY_NEG_INF)

            def madd_chunk(j, carry):
                kk = key_ref[j]
                take_tie = jnp.where(row + j * tk <= last, jnp.where(kk > KEY_NEG_INF, 0.0, NEG_INF), NEG_INF)
                madd_ref[j] = jnp.where(kk > thr_c, 0.0, jnp.where(kk == thr, take_tie, NEG_INF))
                return carry

            lax.fori_loop(0, nj, madd_chunk, 0)

    _select()

    ng = ATT_KV_HEADS
    c0 = jnp.maximum(i - 1, 0)
    c1 = jnp.minimum(c0 + 1, nkc - 1)
    n_far = (c0 + 1) // 2

    def scores_to(slot, g, ca, madd_c, bias_c):
        rows = pl.ds(pl.multiple_of(ca * tk, tk), 2 * tk)
        qt = jnp.concatenate([qt_ref[0, g * REP + r] for r in range(REP)], axis=1)
        add = jnp.concatenate([madd_c if bias_c is None else madd_c + bias_c[r] for r in range(REP)], axis=1)
        s = _dot(k_ref[rows, g * HEAD_DIM:(g + 1) * HEAD_DIM], qt) + add
        s_buf[slot] = s
        st_ref[ng + 2 + slot:ng + 3 + slot, :] = jnp.max(s, axis=0, keepdims=True)

    def softmax_to(slot, g):
        m = st_ref[g:g + 1, :]
        m_new = jnp.maximum(m, st_ref[ng + 2 + slot:ng + 3 + slot, :])
        m_safe = jnp.where(m_new == NEG_INF, 0.0, m_new)
        alpha = jnp.exp2(m - m_safe)
        p_buf[slot] = jnp.exp2((s_buf[slot] - m_safe).astype(BF16))
        st_ref[g:g + 1, :] = m_new
        st_ref[ng + slot:ng + 1 + slot, :] = alpha

    def values_from(slot, g, ca, cb):
        vt_c = jnp.concatenate([vt_ref[0, g, ca], vt_ref[0, g, cb]], axis=1)
        acc_ref[g] = st_ref[ng + slot:ng + 1 + slot, :] * acc_ref[g] + _dot(vt_c, p_buf[slot])

    def near_scores_to(slot, g):
        first = i == 0
        t_a = jnp.where(first, 0, 1)
        idx_b = jnp.where(first, neg_slot, c0 + 1)
        madd_near = jnp.concatenate([madd_ref[c0], madd_ref[idx_b]], axis=0)
        bias_near = [jnp.concatenate([bias_ref[t_a, g * REP + r], bias_ref[0, g * REP + r]], axis=0)
                     for r in range(REP)]
        scores_to(slot, g, c0, madd_near, bias_near)

    def far_scores_to(slot, g, k):
        ca = jnp.minimum(2 * (k - 1), nkc - 2)
        cb = jnp.where(ca + 1 >= c0, neg_slot, ca + 1)
        scores_to(slot, g, ca, jnp.concatenate([madd_ref[ca], madd_ref[cb]], axis=0), None)

    st_ref[0:ng, :] = jnp.full((ng, REP * tq), NEG_INF, F32)
    acc_ref[...] = jnp.zeros_like(acc_ref)

    near_scores_to(0, 0)
    for g in range(ng):
        if g + 1 < ng:
            near_scores_to((g + 1) % 2, g + 1)
        else:
            far_scores_to(0, 0, 1)
        softmax_to(g % 2, g)
        values_from(g % 2, g, c0, c1)

    def far_step(k, carry):
        for g in range(ng):
            if g + 1 < ng:
                far_scores_to((g + 1) % 2, g + 1, k)
            else:
                far_scores_to(0, 0, k + 1)
            softmax_to(g % 2, g)
            values_from(g % 2, g, 2 * (k - 1), 2 * (k - 1) + 1)
        return carry

    lax.fori_loop(1, n_far + 1, far_step, 0)

    for g in range(ng):
        out = acc_ref[g, 0:HEAD_DIM, :] / acc_ref[g, HEAD_DIM:HEAD_DIM + 1, :]
        for r in range(REP):
            h = g * REP + r
            o_ref[:, h * HEAD_DIM:(h + 1) * HEAD_DIM] = out[:, r * tq:(r + 1) * tq].T.astype(o_ref.dtype)


def _dsa(qt, kn, vt, qit, wt, ki, rel_bias, batch, seq):
    tq, tk = DSA_TQ, DSA_TK
    nq = seq // tq
    nkc = seq // tk
    topk = min(TOPK_MAX, seq // 4)
    bidx = jnp.asarray(_bias_bucket_tiles(tq, tk))
    kern = functools.partial(_dsa_kernel, tq=tq, tk=tk, nkc=nkc, topk=topk)
    once = pl.Buffered(1)
    return pl.pallas_call(
        kern,
        grid=(batch, nq),
        in_specs=[pl.BlockSpec(memory_space=pltpu.SMEM),
                  pl.BlockSpec((2, tk, tq), lambda b, i: (0, 0, 0), pipeline_mode=once),
                  pl.BlockSpec((1, ATT_HEADS, HEAD_DIM, tq), lambda b, i: (b * nq + i, 0, 0, 0)),
                  pl.BlockSpec((1, IDX_Q, tq), lambda b, i: (b * nq + i, 0, 0)),
                  pl.BlockSpec((1, IDX_HEADS, tq), lambda b, i: (b * nq + i, 0, 0)),
                  pl.BlockSpec((seq, ATT_KV), lambda b, i: (b, 0), pipeline_mode=once),
                  pl.BlockSpec((1, ATT_KV_HEADS, nkc, VT_ROWS, tk), lambda b, i: (b, 0, 0, 0, 0),
                               pipeline_mode=once),
                  pl.BlockSpec((seq, IDX_DIM), lambda b, i: (b, 0), pipeline_mode=once)],
        out_specs=pl.BlockSpec((tq, ATT_Q), lambda b, i: (b * nq + i, 0)),
        out_shape=jax.ShapeDtypeStruct((batch * seq, ATT_Q), BF16),
        scratch_shapes=[pltpu.VMEM((nkc, tk, tq), I32),
                        pltpu.VMEM((3, nkc, tk, tq), BF16),
                        pltpu.VMEM((nkc + 1, tk, tq), F32),
                        pltpu.VMEM((2, ATT_HEADS, tk, tq), F32),
                        pltpu.VMEM((2, 2 * tk, REP * tq), F32),
                        pltpu.VMEM((2, 2 * tk, REP * tq), BF16),
                        pltpu.VMEM((ATT_KV_HEADS + 4, REP * tq), F32),
                        pltpu.VMEM((ATT_KV_HEADS, VT_ROWS, REP * tq), F32)],
        compiler_params=_params(("arbitrary", "arbitrary")),
        name="dsa_attention",
    )(rel_bias, bidx, qt, qit, wt, kn, vt, ki)


E_ROWS = 3 * CHUNK + BF16_ROWS
CONV_HALO = BF16_ROWS


def _conv_shift_matrix():
    ext = CONV_HALO + CHUNK
    s = np.zeros((CHUNK, CONV_WIDTH * ext), np.float32)
    for k in range(CONV_WIDTH):
        t = np.arange(CHUNK)
        s[t, k * ext + CONV_HALO + t - (CONV_WIDTH - 1) + k] = 1.0
    return s


SSD_INPUTS = 12


def _ssd_kernel(*refs, n_cast, n_side):
    (xbc0_ref, xbcn_ref, halon_ref, z_ref, sm_ref, shift_ref, cw_ref, cb_ref, dtb_ref, alog_ref, dsk_ref,
     ng_ref) = refs[:SSD_INPUTS]
    side_in = refs[SSD_INPUTS:SSD_INPUTS + n_side]
    o_ref = refs[SSD_INPUTS + n_side]
    side_out = refs[SSD_INPUTS + n_side + 1:SSD_INPUTS + 2 * n_side + 1]
    xs_ref, bm_ref, cm_ref, state_ref, ypre_ref, acg_ref, actg_ref = refs[SSD_INPUTS + 2 * n_side + 1:]
    c = pl.program_id(1)
    L = CHUNK
    _side_cast(pl.program_id(0) * pl.num_programs(1) + c, n_cast, side_in, side_out)

    def conv_to(slot, x_ref, halo):
        shift = shift_ref[...]
        cblk = GROUP_W
        for cbi in range(CONV_DIM // cblk):
            cols = slice(cbi * cblk, (cbi + 1) * cblk)
            ext = jnp.concatenate([halo[:, cols], x_ref[:, cols]], axis=0)
            wtap = cw_ref[:, cols].astype(BF16)
            prods = jnp.concatenate([ext * wtap[kk:kk + 1, :] for kk in range(CONV_WIDTH)], axis=0)
            y = _silu(cb_ref[:, cols] + _dot(shift, prods))
            if cbi < SSM_GROUPS:
                xs_ref[slot, cbi] = y
            else:
                per = cblk // SSM_STATE
                for u in range(per):
                    gi = (cbi - SSM_GROUPS) * per + u
                    piece = y[:, u * SSM_STATE:(u + 1) * SSM_STATE]
                    if gi < SSM_GROUPS:
                        bm_ref[slot, gi] = piece
                    else:
                        cm_ref[slot, gi - SSM_GROUPS] = piece

    def scan_chunk(slot):
        _ssd_scan_chunk(slot, z_ref, sm_ref, dtb_ref, alog_ref, dsk_ref, ng_ref, o_ref,
                        xs_ref, bm_ref, cm_ref, state_ref, ypre_ref, acg_ref, actg_ref)

    @pl.when(c == 0)
    def _():
        state_ref[...] = jnp.zeros_like(state_ref)
        conv_to(0, xbc0_ref, jnp.zeros((CONV_HALO, CONV_DIM), BF16))

    @pl.when(c % 2 == 0)
    def _():
        conv_to(1, xbcn_ref, halon_ref[...])
        scan_chunk(0)

    @pl.when(c % 2 == 1)
    def _():
        conv_to(0, xbcn_ref, halon_ref[...])
        scan_chunk(1)


def _ssd_scan_chunk(slot, z_ref, sm_ref, dtb_ref, alog_ref, dsk_ref, ng_ref, o_ref,
                    xs_ref, bm_ref, cm_ref, state_ref, ypre_ref, acg_ref, actg_ref):
    L = CHUNK
    dt_in = sm_ref[:, S_DT:S_DT + SSM_HEADS] + dtb_ref[...]
    dt_act = jnp.maximum(dt_in, 0.0) + jnp.log1p(jnp.exp(-jnp.abs(dt_in)))
    a = dt_act * (-jnp.exp(alog_ref[...]))
    ri = lax.broadcasted_iota(I32, (L, L), 0)
    ci = lax.broadcasted_iota(I32, (L, L), 1)
    tril = ri >= ci
    tri_b = jnp.where(tril, 1.0, 0.0).astype(BF16)
    a_cum = sum(_dot(tri_b, p) for p in _split3(a))
    eye_b = jnp.where(lax.broadcasted_iota(I32, (SSM_HEADS, SSM_HEADS), 0)
                      == lax.broadcasted_iota(I32, (SSM_HEADS, SSM_HEADS), 1), 1.0, 0.0).astype(BF16)
    a_cum_t = sum(_dot_nt(eye_b, p) for p in _split3(a_cum))
    a_last = a_cum[L - 1:L, :]
    for gi in range(SSM_GROUPS):
        acg_ref[gi] = a_cum[:, gi * HEADS_PER_GROUP:(gi + 1) * HEADS_PER_GROUP]
        actg_ref[gi] = a_cum_t[gi * HEADS_PER_GROUP:(gi + 1) * HEADS_PER_GROUP, :]
    cd3 = _split3(jnp.exp(a_last))
    ds3 = _split3(dsk_ref[...])
    extras = jnp.concatenate([p.astype(F32) for p in cd3 + ds3]
                             + [jnp.zeros((E_ROWS - 3 * L - 6, SSM_HEADS), F32)], axis=0)
    e_mat = jnp.concatenate([dt_act, jnp.exp(a_cum), jnp.exp(a_last - a_cum), extras], axis=0).astype(BF16)

    lane = lax.broadcasted_iota(I32, (L, LANES), 1)
    lo_mask = lane < SSM_HEAD_DIM

    def group_body(gi, carry):
        xs = xs_ref[slot, gi]
        bg = bm_ref[slot, gi]
        cg_b = cm_ref[slot, gi].astype(BF16)
        hsel = (lax.broadcasted_iota(I32, (SSM_HEADS, GROUP_W), 0)
                == gi * HEADS_PER_GROUP + lax.broadcasted_iota(I32, (SSM_HEADS, GROUP_W), 1) // SSM_HEAD_DIM)
        ex = _dot(e_mat, jnp.where(hsel, 1.0, 0.0).astype(BF16))
        dt_rep = ex[0:L]
        expa_rep = ex[L:2 * L]
        dte_rep = ex[2 * L:3 * L]
        cd_rep = ex[3 * L:3 * L + 1] + ex[3 * L + 1:3 * L + 2] + ex[3 * L + 2:3 * L + 3]
        dsk_rep = ex[3 * L + 3:3 * L + 4] + ex[3 * L + 4:3 * L + 5] + ex[3 * L + 5:3 * L + 6]

        xd = xs * dt_rep
        xd_b = xd.astype(BF16)
        cb = _dot_nt(cg_b, bg.astype(BF16))
        acg = acg_ref[gi]
        actg = actg_ref[gi]
        pairs = []
        for pj in range(HEADS_PER_GROUP // 2):
            gmat = []
            for e in (2 * pj, 2 * pj + 1):
                seg = acg[:, e:e + 1] - actg[e:e + 1, :]
                dec = jnp.exp(jnp.where(tril, seg, NEG_INF))
                gmat.append((cb * dec).astype(BF16))
            xp = xd_b[:, pj * LANES:(pj + 1) * LANES]
            zero = jnp.zeros_like(xp)
            pairs.append(_dot(gmat[0], jnp.where(lo_mask, xp, zero))
                         + _dot(gmat[1], jnp.where(lo_mask, zero, xp)))
        y_diag = jnp.concatenate(pairs, axis=1)

        st = state_ref[gi]
        y_off = _dot(cg_b, st.astype(BF16)) * expa_rep
        xdd = (xd * dte_rep).astype(BF16)
        state_ref[gi] = st * cd_rep + _dot(bg.T.astype(BF16), xdd)
        ypre_ref[gi] = y_diag + y_off + dsk_rep * xs
        return carry

    lax.fori_loop(0, SSM_GROUPS, group_body, 0, unroll=8)

    for gi in range(SSM_GROUPS):
        cols = slice(gi * GROUP_W, (gi + 1) * GROUP_W)
        zz = z_ref[:, cols].astype(F32)
        y = ypre_ref[gi] * _silu(zz)
        ms = jnp.mean(y * y, axis=-1, keepdims=True)
        o_ref[:, cols] = (y * lax.rsqrt(ms + EPS) * ng_ref[:, cols]).astype(o_ref.dtype)


def _ssd(main, small, conv_w, conv_b, dt_bias, a_log, d_skip, norm_g, side_weights, batch, seq):
    nc = seq // CHUNK
    n_cast, side_specs = _side_cast_specs(side_weights, batch * nc, lambda b, c: b * nc + c)
    hb = CHUNK // CONV_HALO
    row = lambda b, c: b * nc + c
    full = lambda shape: pl.BlockSpec(shape, lambda b, c: (0,) * len(shape))
    shift = jnp.asarray(_conv_shift_matrix(), BF16)
    outs = pl.pallas_call(
        functools.partial(_ssd_kernel, n_cast=n_cast, n_side=len(side_weights)),
        grid=(batch, nc),
        in_specs=[pl.BlockSpec((CHUNK, CONV_DIM), lambda b, c: (row(b, 0), M_XBC // CONV_DIM)),
                  pl.BlockSpec((CHUNK, CONV_DIM),
                               lambda b, c: (row(b, jnp.minimum(c + 1, nc - 1)), M_XBC // CONV_DIM)),
                  pl.BlockSpec((CONV_HALO, CONV_DIM),
                               lambda b, c: (row(b, c) * hb + hb - 1, M_XBC // CONV_DIM)),
                  pl.BlockSpec((CHUNK, SSM_INNER), lambda b, c: (row(b, c), M_Z // SSM_INNER)),
                  pl.BlockSpec((CHUNK, S_END), lambda b, c: (row(b, c), 0)),
                  full((CHUNK, CONV_WIDTH * (CONV_HALO + CHUNK))),
                  full((CONV_WIDTH, CONV_DIM)), full((1, CONV_DIM)), full((1, SSM_HEADS)),
                  full((1, SSM_HEADS)), full((1, SSM_HEADS)), full((1, SSM_INNER))] + side_specs,
        out_specs=[pl.BlockSpec((CHUNK, SSM_INNER), lambda b, c: (row(b, c), 0))] + side_specs,
        out_shape=[jax.ShapeDtypeStruct((batch * seq, SSM_INNER), BF16)]
                  + [jax.ShapeDtypeStruct(w.shape, BF16) for w in side_weights],
        scratch_shapes=[pltpu.VMEM((2, SSM_GROUPS, CHUNK, GROUP_W), F32),
                        pltpu.VMEM((2, SSM_GROUPS, CHUNK, SSM_STATE), F32),
                        pltpu.VMEM((2, SSM_GROUPS, CHUNK, SSM_STATE), F32),
                        pltpu.VMEM((SSM_GROUPS, SSM_STATE, GROUP_W), F32),
                        pltpu.VMEM((SSM_GROUPS, CHUNK, GROUP_W), F32),
                        pltpu.VMEM((SSM_GROUPS, CHUNK, HEADS_PER_GROUP), F32),
                        pltpu.VMEM((SSM_GROUPS, HEADS_PER_GROUP, CHUNK), F32)],
        compiler_params=_params(("arbitrary", "arbitrary")),
        name="ssd_scan",
    )(main, main, main, main, small, shift, conv_w, conv_b, dt_bias, a_log, d_skip, norm_g, *side_weights)
    return outs[0], outs[1:]


def _merge_kernel(att_ref, y_ref, ga_ref, gs_ref, wa_ref, ws_ref, o_ref):
    pa = _dot(att_ref[...], wa_ref[...])
    ps = _dot(y_ref[...], ws_ref[...])
    o_ref[...] = (_sigmoid(ga_ref[...].astype(F32)) * pa + _sigmoid(gs_ref[...].astype(F32)) * ps).astype(o_ref.dtype)


def _merge(att, y, main, wa, ws, tm, tn):
    t = att.shape[0]
    return pl.pallas_call(
        _merge_kernel,
        grid=(t // tm, D_MODEL // tn),
        in_specs=[pl.BlockSpec((tm, ATT_Q), lambda i, j: (i, 0)),
                  pl.BlockSpec((tm, SSM_INNER), lambda i, j: (i, 0)),
                  pl.BlockSpec((tm, tn), lambda i, j: (i, M_GA // tn + j)),
                  pl.BlockSpec((tm, tn), lambda i, j: (i, M_GS // tn + j)),
                  pl.BlockSpec((ATT_Q, tn), lambda i, j: (0, j)),
                  pl.BlockSpec((SSM_INNER, tn), lambda i, j: (0, j))],
        out_specs=pl.BlockSpec((tm, tn), lambda i, j: (i, j)),
        out_shape=jax.ShapeDtypeStruct((t, D_MODEL), BF16),
        compiler_params=_params(("parallel", "arbitrary")),
        name="gated_merge",
    )(att, y, main, main, wa, ws)


def _out_proj_kernel(m_ref, w_ref, x_ref, g_ref, x1_ref, h2_ref):
    half = m_ref.shape[0] // 2
    for r0 in (0, half):
        rows = slice(r0, r0 + half)
        x1 = x_ref[rows, :] + _dot(m_ref[rows, :], w_ref[...])
        x1_ref[rows, :] = x1
        ms = jnp.mean(x1 * x1, axis=-1, keepdims=True)
        h2_ref[rows, :] = (x1 * lax.rsqrt(ms + EPS) * g_ref[...]).astype(BF16)


def _out_proj(merged, w, x, g, tm):
    t = x.shape[0]
    return pl.pallas_call(
        _out_proj_kernel,
        grid=(t // tm,),
        in_specs=[pl.BlockSpec((tm, D_MODEL), lambda i: (i, 0)),
                  pl.BlockSpec((D_MODEL, D_MODEL), lambda i: (0, 0)),
                  pl.BlockSpec((tm, D_MODEL), lambda i: (i, 0)),
                  pl.BlockSpec((1, D_MODEL), lambda i: (0, 0))],
        out_specs=[pl.BlockSpec((tm, D_MODEL), lambda i: (i, 0)),
                   pl.BlockSpec((tm, D_MODEL), lambda i: (i, 0))],
        out_shape=[jax.ShapeDtypeStruct((t, D_MODEL), F32),
                   jax.ShapeDtypeStruct((t, D_MODEL), BF16)],
        compiler_params=_params(("parallel",)),
        name="out_proj_norm",
    )(merged, w, x, g)


def _mlp_kernel(h_ref, x1_ref, wu_ref, wd_ref, o_ref):
    @pl.when(pl.program_id(1) == 0)
    def _():
        o_ref[...] = x1_ref[...]

    u = _dot(h_ref[...], wu_ref[...])
    u = jnp.square(jnp.maximum(u, 0.0)).astype(BF16)
    o_ref[...] += _dot(u, wd_ref[...])


def _mlp(h2, x1, wu, wd, tm, th):
    t = h2.shape[0]
    return pl.pallas_call(
        _mlp_kernel,
        grid=(t // tm, MLP_HIDDEN // th),
        in_specs=[pl.BlockSpec((tm, D_MODEL), lambda i, j: (i, 0)),
                  pl.BlockSpec((tm, D_MODEL), lambda i, j: (i, 0)),
                  pl.BlockSpec((D_MODEL, th), lambda i, j: (0, j)),
                  pl.BlockSpec((th, D_MODEL), lambda i, j: (j, 0))],
        out_specs=pl.BlockSpec((tm, D_MODEL), lambda i, j: (i, 0)),
        out_shape=jax.ShapeDtypeStruct((t, D_MODEL), F32),
        compiler_params=_params(("parallel", "arbitrary")),
        name="relu2_mlp",
    )(h2, x1, wu, wd)


def _pack_w_small(w_t):
    assert (O_GA, O_KI, O_Z, O_DT) == (M_GA, M_Z, O_WI + IDX_HEADS, O_Z + M_END - M_Z)
    return jnp.concatenate([w_t[O_KI:O_Z], w_t[O_DT:O_DT + SSM_HEADS],
                            jnp.zeros((S_END - S_DT - SSM_HEADS, w_t.shape[1]), w_t.dtype)], axis=0)


def _block(x2, batch, seq, norm1_g, w_in, conv_w, conv_b, dt_bias, a_log, d_skip, ssm_norm_g, q_norm_g,
           k_norm_g, rel_bias, w_att_branch, w_ssm_branch, w_out, norm2_g, w_up, w_down):
    row = lambda v: v.reshape(1, -1)
    w_t = w_in.T
    xn, small = _norm_small(x2, row(norm1_g), _pack_w_small(w_t), tm=min(1024, x2.shape[0]))
    main = _in_proj(xn, w_t, tm=min(1024, x2.shape[0]))
    qt, kn, vt, qit, wt, ki = _qk_prep(main, small, row(q_norm_g), row(k_norm_g), batch, seq, tq=DSA_TQ)
    att = _dsa(qt, kn, vt, qit, wt, ki, rel_bias, batch, seq)
    y, (w_att_b, w_ssm_b, w_out_b, w_up_b, w_down_b) = _ssd(
        main, small, conv_w, row(conv_b), row(dt_bias), row(a_log), row(d_skip), row(ssm_norm_g),
        (w_att_branch, w_ssm_branch, w_out, w_up, w_down), batch, seq)
    merged = _merge(att, y, main, w_att_b, w_ssm_b, tm=512, tn=1024)
    x1, h2 = _out_proj(merged, w_out_b, x2, row(norm2_g), tm=512)
    return _mlp(h2, x1, w_up_b, w_down_b, tm=512, th=1024)


def kernel(x, norm1_g, w_in, conv_w, conv_b, dt_bias, a_log, d_skip, ssm_norm_g, q_norm_g, k_norm_g, rel_bias,
           w_att_branch, w_ssm_branch, w_out, norm2_g, w_up, w_down):
    batch, seq, d = x.shape
    x2 = x.reshape(batch * seq, d)
    for l in range(norm1_g.shape[0]):
        x2 = _block(x2, batch, seq, norm1_g[l], w_in[l], conv_w[l], conv_b[l], dt_bias[l], a_log[l], d_skip[l],
                    ssm_norm_g[l], q_norm_g[l], k_norm_g[l], rel_bias, w_att_branch[l], w_ssm_branch[l],
                    w_out[l], norm2_g[l], w_up[l], w_down[l])
    return x2.reshape(batch, seq, d)
```

```python
import functools
import math

import numpy as np
import jax
import jax.numpy as jnp
from jax import lax
from jax.experimental import pallas as pl
from jax.experimental.pallas import tpu as pltpu

F32 = jnp.float32
BF16 = jnp.bfloat16
I32 = jnp.int32

D_MODEL = 2048
ATT_HEADS = 16
ATT_KV_HEADS = 4
HEAD_DIM = 128
REP = ATT_HEADS // ATT_KV_HEADS
IDX_HEADS = 16
IDX_DIM = 64
TOPK_MAX = 256
N_BUCKETS = 32
MAX_DISTANCE = 128
SSM_INNER = 2 * D_MODEL
SSM_HEAD_DIM = 64
SSM_HEADS = SSM_INNER // SSM_HEAD_DIM
SSM_GROUPS = 8
SSM_STATE = 128
CONV_WIDTH = 4
CHUNK = 128
MLP_HIDDEN = 4 * D_MODEL
EPS = 1e-6

ATT_Q = ATT_HEADS * HEAD_DIM
ATT_KV = ATT_KV_HEADS * HEAD_DIM
IDX_Q = IDX_HEADS * IDX_DIM
SSM_BC = SSM_GROUPS * SSM_STATE
CONV_DIM = SSM_INNER + 2 * SSM_BC
SPLITS = (D_MODEL, D_MODEL, ATT_Q, ATT_KV, ATT_KV, IDX_Q, IDX_DIM, IDX_HEADS, SSM_INNER, CONV_DIM, SSM_HEADS)
_OFFS = tuple(int(v) for v in np.cumsum((0,) + SPLITS))
(O_GA, O_GS, O_Q, O_K, O_V, O_QI, O_KI, O_WI, O_Z, O_XBC, O_DT, _O_END) = _OFFS

M_GA = 0
M_GS = M_GA + D_MODEL
M_Q = M_GS + D_MODEL
M_K = M_Q + ATT_Q
M_V = M_K + ATT_KV
M_QI = M_V + ATT_KV
M_Z = M_QI + IDX_Q
M_XBC = M_Z + SSM_INNER
M_END = M_XBC + CONV_DIM
S_KI = 0
S_WI = S_KI + IDX_DIM
S_DT = S_WI + IDX_HEADS
S_END = 256

HEADS_PER_GROUP = SSM_HEADS // SSM_GROUPS
GROUP_W = HEADS_PER_GROUP * SSM_HEAD_DIM

LANES = 128
SUBLANES = 8
BF16_ROWS = 2 * SUBLANES
VMEM_LIMIT = 56 * 1024 * 1024

NEG_INF = float("-inf")
INT_MIN = -(2 ** 31)
KEY_NEG_INF = int(np.int32(np.uint32(0xFF800000) ^ np.uint32(0x7FFFFFFF)))


def _dot(a, b):
    return jnp.dot(a, b, preferred_element_type=F32)


def _dot_nt(a, b):
    return lax.dot_general(a, b, (((1,), (1,)), ((), ())), preferred_element_type=F32)


def _split3(x):
    hi = x.astype(BF16)
    r = x - hi.astype(F32)
    mid = r.astype(BF16)
    lo = (r - mid.astype(F32)).astype(BF16)
    return hi, mid, lo


def _silu(x):
    h = 0.5 * x
    return h + h * jnp.tanh(h)


def _sigmoid(x):
    return 0.5 + 0.5 * jnp.tanh(0.5 * x)


def _params(sem):
    return pltpu.CompilerParams(dimension_semantics=sem, vmem_limit_bytes=VMEM_LIMIT)


def _norm_small_kernel(x_ref, g_ref, w_ref, xn_ref, sm_ref):
    x = x_ref[...]
    ms = jnp.mean(x * x, axis=-1, keepdims=True)
    xn = (x * lax.rsqrt(ms + EPS) * g_ref[...]).astype(BF16)
    xn_ref[...] = xn
    sm_ref[...] = _dot_nt(xn, w_ref[...].astype(BF16))


def _norm_small(x, g, w_small_t, tm):
    m, d = x.shape
    n = w_small_t.shape[0]
    return pl.pallas_call(
        _norm_small_kernel,
        grid=(m // tm,),
        in_specs=[pl.BlockSpec((tm, d), lambda i: (i, 0)),
                  pl.BlockSpec((1, d), lambda i: (0, 0)),
                  pl.BlockSpec((n, d), lambda i: (0, 0))],
        out_specs=[pl.BlockSpec((tm, d), lambda i: (i, 0)),
                   pl.BlockSpec((tm, n), lambda i: (i, 0))],
        out_shape=[jax.ShapeDtypeStruct((m, d), BF16), jax.ShapeDtypeStruct((m, n), F32)],
        compiler_params=_params(("parallel",)),
        name="norm_small_proj",
    )(x, g, w_small_t)


IN_TN = 1024
IN_ALIGNED_TILES = O_KI // IN_TN
IN_SHIFT = O_Z - M_Z
IN_NEXT = 128
IN_ROWS = 256


def _in_proj_kernel(xn_ref, wa_ref, wn_ref, o_ref, w_scr):
    j = pl.program_id(0)
    i = pl.program_id(1)

    @pl.when((i == 0) & (j < IN_ALIGNED_TILES))
    def _():
        for r0 in range(0, IN_TN, IN_ROWS):
            w_scr[r0:r0 + IN_ROWS, :] = wa_ref[r0:r0 + IN_ROWS, :].astype(BF16)

    @pl.when((i == 0) & (j >= IN_ALIGNED_TILES))
    def _():
        for r0 in range(0, IN_TN - IN_SHIFT, IN_ROWS):
            r1 = min(r0 + IN_ROWS, IN_TN - IN_SHIFT)
            w_scr[r0:r1, :] = wa_ref[r0 + IN_SHIFT:r1 + IN_SHIFT, :].astype(BF16)
        w_scr[IN_TN - IN_SHIFT:IN_TN, :] = wn_ref[0:IN_SHIFT, :].astype(BF16)

    o_ref[...] = _dot_nt(xn_ref[...], w_scr[...]).astype(o_ref.dtype)


def _in_proj(xn, w_t, tm):
    m, d = xn.shape
    assert O_KI % IN_TN == 0 and M_END % IN_TN == 0 and IN_TN % IN_NEXT == 0
    assert 0 < IN_SHIFT <= IN_NEXT and IN_SHIFT % BF16_ROWS == 0
    return pl.pallas_call(
        _in_proj_kernel,
        grid=(M_END // IN_TN, m // tm),
        in_specs=[pl.BlockSpec((tm, d), lambda j, i: (i, 0)),
                  pl.BlockSpec((IN_TN, d), lambda j, i: (j, 0)),
                  pl.BlockSpec((IN_NEXT, d), lambda j, i: ((j + 1) * (IN_TN // IN_NEXT), 0))],
        out_specs=pl.BlockSpec((tm, IN_TN), lambda j, i: (i, j)),
        out_shape=jax.ShapeDtypeStruct((m, M_END), BF16),
        scratch_shapes=[pltpu.VMEM((IN_TN, d), BF16)],
        compiler_params=_params(("arbitrary", "arbitrary")),
        name="in_proj",
    )(xn, w_t, w_t)


def _side_cast_specs(weights, n_steps, step_of):
    n_cast = 1 << (n_steps.bit_length() - 1)
    specs = []
    for w in weights:
        rows = w.shape[0] // n_cast
        assert w.shape[0] % n_cast == 0 and rows % BF16_ROWS == 0
        specs.append(pl.BlockSpec((rows, w.shape[1]), lambda *ids: (jnp.minimum(step_of(*ids), n_cast - 1), 0)))
    return n_cast, specs


def _side_cast(step, n_cast, srcs, dsts):
    @pl.when(step < n_cast)
    def _():
        for src, dst in zip(srcs, dsts):
            dst[...] = src[...].astype(BF16)


LOG2E = math.log2(math.e)
VT_ROWS = HEAD_DIM + BF16_ROWS


def _qk_prep_kernel(q_ref, k_ref, v_ref, qi_ref, sm_ref, qg_ref, kg_ref,
                    qt_ref, kn_ref, vt_ref, qit_ref, wt_ref, ki_ref):
    qg = qg_ref[...]
    for h in range(ATT_HEADS):
        x = q_ref[:, h * HEAD_DIM:(h + 1) * HEAD_DIM].astype(F32)
        ms = jnp.mean(x * x, axis=-1, keepdims=True)
        y = x * lax.rsqrt(ms + EPS) * qg * (HEAD_DIM ** -0.5 * LOG2E)
        qt_ref[0, h] = y.T.astype(BF16)
    kg = kg_ref[...]
    for h in range(ATT_KV_HEADS):
        x = k_ref[:, h * HEAD_DIM:(h + 1) * HEAD_DIM].astype(F32)
        ms = jnp.mean(x * x, axis=-1, keepdims=True)
        kn_ref[:, h * HEAD_DIM:(h + 1) * HEAD_DIM] = (x * lax.rsqrt(ms + EPS) * kg).astype(BF16)
        vt_ref[0, h, 0, 0:HEAD_DIM, :] = v_ref[:, h * HEAD_DIM:(h + 1) * HEAD_DIM].astype(F32).T.astype(BF16)
        vt_ref[0, h, 0, HEAD_DIM:VT_ROWS, :] = jnp.ones((VT_ROWS - HEAD_DIM, v_ref.shape[0]), BF16)
    for p in range(IDX_Q // LANES):
        qit_ref[0, p * LANES:(p + 1) * LANES, :] = qi_ref[:, p * LANES:(p + 1) * LANES].astype(F32).T.astype(BF16)
    sm_t = sm_ref[:, 0:LANES].T
    wt_ref[0] = sm_t[S_WI:S_WI + IDX_HEADS, :] * (IDX_HEADS ** -0.5 * IDX_DIM ** -0.5)
    ki_ref[...] = sm_ref[:, S_KI:S_KI + IDX_DIM].astype(BF16)


def _qk_prep(main, small, qg, kg, batch, seq, tq):
    t = main.shape[0]
    nq = seq // tq
    return pl.pallas_call(
        _qk_prep_kernel,
        grid=(batch, nq),
        in_specs=[pl.BlockSpec((tq, ATT_Q), lambda b, i: (b * nq + i, M_Q // ATT_Q)),
                  pl.BlockSpec((tq, ATT_KV), lambda b, i: (b * nq + i, M_K // ATT_KV)),
                  pl.BlockSpec((tq, ATT_KV), lambda b, i: (b * nq + i, M_V // ATT_KV)),
                  pl.BlockSpec((tq, IDX_Q), lambda b, i: (b * nq + i, M_QI // IDX_Q)),
                  pl.BlockSpec((tq, S_END), lambda b, i: (b * nq + i, 0)),
                  pl.BlockSpec((1, HEAD_DIM), lambda b, i: (0, 0)),
                  pl.BlockSpec((1, HEAD_DIM), lambda b, i: (0, 0))],
        out_specs=[pl.BlockSpec((1, ATT_HEADS, HEAD_DIM, tq), lambda b, i: (b * nq + i, 0, 0, 0)),
                   pl.BlockSpec((tq, ATT_KV), lambda b, i: (b * nq + i, 0)),
                   pl.BlockSpec((1, ATT_KV_HEADS, 1, VT_ROWS, tq), lambda b, i: (b, 0, i, 0, 0)),
                   pl.BlockSpec((1, IDX_Q, tq), lambda b, i: (b * nq + i, 0, 0)),
                   pl.BlockSpec((1, IDX_HEADS, tq), lambda b, i: (b * nq + i, 0, 0)),
                   pl.BlockSpec((tq, IDX_DIM), lambda b, i: (b * nq + i, 0))],
        out_shape=[jax.ShapeDtypeStruct((t // tq, ATT_HEADS, HEAD_DIM, tq), BF16),
                   jax.ShapeDtypeStruct((t, ATT_KV), BF16),
                   jax.ShapeDtypeStruct((batch, ATT_KV_HEADS, nq, VT_ROWS, tq), BF16),
                   jax.ShapeDtypeStruct((t // tq, IDX_Q, tq), BF16),
                   jax.ShapeDtypeStruct((t // tq, IDX_HEADS, tq), F32),
                   jax.ShapeDtypeStruct((t, IDX_DIM), BF16)],
        compiler_params=_params(("parallel", "parallel")),
        name="qk_prep",
    )(main, main, main, main, small, qg, kg)


DSA_TQ = 256
DSA_TK = 256
CNT_WAYS = 4


def _t5_bucket_np(dist):
    n = np.maximum(dist, 0)
    max_exact = N_BUCKETS // 2
    nf = np.maximum(n, 1).astype(np.float32)
    ratio = (np.log(nf / np.float32(max_exact)) / np.float32(math.log(MAX_DISTANCE / max_exact))
             * np.float32(N_BUCKETS - max_exact))
    large = max_exact + ratio.astype(np.int32)
    large = np.minimum(large, N_BUCKETS - 1)
    return np.where(n < max_exact, n, large).astype(np.int32)


def _bias_bucket_tiles(tq, tk):
    r = np.arange(tq)[None, :]
    c = np.arange(tk)[:, None]
    d0 = _t5_bucket_np(r - c)
    d1 = _t5_bucket_np(tk + r - c)
    assert np.all(_t5_bucket_np(np.arange(tk + 1, 8 * tk)) == N_BUCKETS - 1)
    return np.stack([d0, d1]).astype(np.int32)


def _dsa_kernel(relb_ref, bidx_ref, qt_ref, qit_ref, wt_ref, k_ref, vt_ref, ki_ref, o_ref,
                key_ref, plane_ref, madd_ref, bias_ref, s_buf, p_buf, st_ref, acc_ref, *, tq, tk, nkc, topk):
    b = pl.program_id(0)
    i = pl.program_id(1)
    neg_slot = nkc

    @pl.when((b == 0) & (i == 0))
    def _init():
        madd_ref[neg_slot] = jnp.full((tk, tq), NEG_INF, F32)
        for t in range(2):
            bt = bidx_ref[t]

            def head_body(h, carry):
                far = relb_ref[N_BUCKETS - 1, h]

                def bucket_body(bk, acc):
                    return jnp.where(bt == bk, (relb_ref[bk, h] - far) * LOG2E, acc)

                bias_ref[t, h] = lax.fori_loop(0, N_BUCKETS, bucket_body, jnp.zeros((tk, tq), F32))
                return carry

            lax.fori_loop(0, ATT_HEADS, head_body, 0)

    def _select():
        nj = i + 1
        qpos = i * tq + lax.broadcasted_iota(I32, (tk, tq), 1)

        def score_chunk(j):
            kc = ki_ref[pl.ds(pl.multiple_of(j * tk, tk), tk), :]
            acc = jnp.zeros((tk, tq), F32)
            for h in range(IDX_HEADS):
                z = _dot(kc, qit_ref[0, h * IDX_DIM:(h + 1) * IDX_DIM, :])
                acc = acc + jnp.maximum(z, 0.0) * wt_ref[0, h:h + 1, :]
            kpos = j * tk + lax.broadcasted_iota(I32, (tk, tq), 0)
            acc = jnp.where(kpos <= qpos, acc, NEG_INF)
            bits = pltpu.bitcast(acc, I32)
            bits = jnp.where(bits == INT_MIN, 0, bits)
            key = bits ^ ((bits >> 31) & 0x7FFFFFFF)
            key_ref[j] = key
            plane_ref[0, j] = pltpu.bitcast(bits & jnp.int32(-65536), F32).astype(BF16)
            plane_ref[1, j] = ((key >> 8) & 0xFF).astype(F32).astype(BF16)
            plane_ref[2, j] = (key & 0xFF).astype(F32).astype(BF16)

        def score_quad(u, carry):
            for v in range(4):
                score_chunk(4 * u + v)
            return carry

        lax.fori_loop(0, nj // 4, score_quad, 0)
        done = (nj // 4) * 4

        @pl.when(nj % 4 >= 2)
        def _():
            score_chunk(done)
            score_chunk(done + 1)

        @pl.when(nj % 2 == 1)
        def _():
            score_chunk(nj - 1)

        kf = float(topk)
        one_b = jnp.ones((tk, tq), BF16)
        zero_b = jnp.zeros((tk, tq), BF16)
        grp = BF16_ROWS * CNT_WAYS

        def count(plane, cand_b, strict):
            def body(j, cnt):
                pv = plane_ref[plane, j]
                hit = jnp.where(pv > cand_b if strict else pv >= cand_b, one_b, zero_b)
                for r0 in range(0, tk, grp):
                    cnt = cnt + hit[r0:r0 + grp]
                return cnt

            cnt = lax.fori_loop(0, nj, body, jnp.zeros((grp, tq), BF16))
            return jnp.sum(cnt.astype(F32), axis=0, keepdims=True)

        def keep_ties(src, dst, tie_b):
            def body(j, carry):
                plane_ref[dst, j] = jnp.where(plane_ref[src, j] == tie_b, plane_ref[dst, j], -one_b)
                return carry

            lax.fori_loop(0, nj, body, 0)

        def top_digit_float(p16):
            fb = (p16 & 0xFFFF) ^ jnp.where(p16 < 0, 0x7FFF, 0)
            return pltpu.bitcast(lax.shift_left(fb, 16), F32).astype(BF16)

        def top_iter(it, prefix):
            cand = prefix + lax.shift_left(jnp.int32(1), 15 - it)
            return jnp.where(count(0, top_digit_float(cand), False) >= kf, cand, prefix)

        d_top = lax.fori_loop(0, 16, top_iter, jnp.full((1, tq), -32768, I32))
        t_top = top_digit_float(d_top)
        above = count(0, t_top, True)

        def byte_digit(plane, above_n):
            def it_body(it, prefix):
                cand = prefix + lax.shift_left(jnp.int32(1), 7 - it)
                tot = above_n + count(plane, cand.astype(F32).astype(BF16), False)
                return jnp.where(tot >= kf, cand, prefix)

            return lax.fori_loop(0, 8, it_body, jnp.zeros((1, tq), I32))

        keep_ties(0, 1, t_top)
        d_mid = byte_digit(1, above)
        t_mid = d_mid.astype(F32).astype(BF16)
        above = above + count(1, t_mid, True)
        keep_ties(1, 2, t_mid)
        d_low = byte_digit(2, above)
        t_low = d_low.astype(F32).astype(BF16)
        thr = lax.shift_left(d_top, 16) | lax.shift_left(d_mid, 8) | d_low
        n_ge = above + count(2, t_low, False)
        tied = jnp.max(n_ge) > kf

        @pl.when(jnp.logical_not(tied))
        def _():
            thr_c = jnp.maximum(thr, KEY_NEG_INF + 1)

            def madd_chunk(j, carry):
                madd_ref[j] = jnp.where(key_ref[j] >= thr_c, 0.0, NEG_INF)
                return carry

            lax.fori_loop(0, nj, madd_chunk, 0)

        @pl.when(tied)
        def _():
            need = kf - (above + count(2, t_low, True))
            row = lax.broadcasted_iota(I32, (tk, tq), 0)

            def pos_iter(it, q_pos):
                cand = q_pos + lax.shift_left(jnp.int32(1), (nkc * tk).bit_length() - 2 - it)

                def body(j, cnt):
                    hit = jnp.where(key_ref[j] == thr, jnp.where(row + j * tk < cand, 1.0, 0.0), 0.0)
                    return cnt + jnp.sum(hit.reshape(tk // SUBLANES, SUBLANES, tq), axis=0)

                cnt = lax.fori_loop(0, nj, body, jnp.zeros((SUBLANES, tq), F32))
                return jnp.where(jnp.sum(cnt, axis=0, keepdims=True) < need, cand, q_pos)

            last = lax.fori_loop(0, (nkc * tk).bit_length() - 1, pos_iter, jnp.zeros((1, tq), I32))
            thr_c = jnp.maximum(thr, KEY_NEG_INF)

            def madd_chunk(j, carry):
                kk = key_ref[j]
                take_tie = jnp.where(row + j * tk <= last, jnp.where(kk > KEY_NEG_INF, 0.0, NEG_INF), NEG_INF)
                madd_ref[j] = jnp.where(kk > thr_c, 0.0, jnp.where(kk == thr, take_tie, NEG_INF))
                return carry

            lax.fori_loop(0, nj, madd_chunk, 0)

    _select()

    ng = ATT_KV_HEADS
    c0 = jnp.maximum(i - 1, 0)
    c1 = jnp.minimum(c0 + 1, nkc - 1)
    n_far = (c0 + 1) // 2

    def scores_to(slot, g, ca, madd_c, bias_c):
        rows = pl.ds(pl.multiple_of(ca * tk, tk), 2 * tk)
        qt = jnp.concatenate([qt_ref[0, g * REP + r] for r in range(REP)], axis=1)
        add = jnp.concatenate([madd_c if bias_c is None else madd_c + bias_c[r] for r in range(REP)], axis=1)
        s = _dot(k_ref[rows, g * HEAD_DIM:(g + 1) * HEAD_DIM], qt) + add
        s_buf[slot] = s
        st_ref[ng + 2 + slot:ng + 3 + slot, :] = jnp.max(s, axis=0, keepdims=True)

    def softmax_to(slot, g):
        m = st_ref[g:g + 1, :]
        m_new = jnp.maximum(m, st_ref[ng + 2 + slot:ng + 3 + slot, :])
        m_safe = jnp.where(m_new == NEG_INF, 0.0, m_new)
        alpha = jnp.exp2(m - m_safe)
        p_buf[slot] = jnp.exp2((s_buf[slot] - m_safe).astype(BF16))
        st_ref[g:g + 1, :] = m_new
        st_ref[ng + slot:ng + 1 + slot, :] = alpha

    def values_from(slot, g, ca, cb):
        vt_c = jnp.concatenate([vt_ref[0, g, ca], vt_ref[0, g, cb]], axis=1)
        acc_ref[g] = st_ref[ng + slot:ng + 1 + slot, :] * acc_ref[g] + _dot(vt_c, p_buf[slot])

    def near_scores_to(slot, g):
        first = i == 0
        t_a = jnp.where(first, 0, 1)
        idx_b = jnp.where(first, neg_slot, c0 + 1)
        madd_near = jnp.concatenate([madd_ref[c0], madd_ref[idx_b]], axis=0)
        bias_near = [jnp.concatenate([bias_ref[t_a, g * REP + r], bias_ref[0, g * REP + r]], axis=0)
                     for r in range(REP)]
        scores_to(slot, g, c0, madd_near, bias_near)

    def far_scores_to(slot, g, k):
        ca = jnp.minimum(2 * (k - 1), nkc - 2)
        cb = jnp.where(ca + 1 >= c0, neg_slot, ca + 1)
        scores_to(slot, g, ca, jnp.concatenate([madd_ref[ca], madd_ref[cb]], axis=0), None)

    st_ref[0:ng, :] = jnp.full((ng, REP * tq), NEG_INF, F32)
    acc_ref[...] = jnp.zeros_like(acc_ref)

    near_scores_to(0, 0)
    for g in range(ng):
        if g + 1 < ng:
            near_scores_to((g + 1) % 2, g + 1)
        else:
            far_scores_to(0, 0, 1)
        softmax_to(g % 2, g)
        values_from(g % 2, g, c0, c1)

    def far_step(k, carry):
        for g in range(ng):
            if g + 1 < ng:
                far_scores_to((g + 1) % 2, g + 1, k)
            else:
                far_scores_to(0, 0, k + 1)
            softmax_to(g % 2, g)
            values_from(g % 2, g, 2 * (k - 1), 2 * (k - 1) + 1)
        return carry

    lax.fori_loop(1, n_far + 1, far_step, 0)

    for g in range(ng):
        out = acc_ref[g, 0:HEAD_DIM, :] / acc_ref[g, HEAD_DIM:HEAD_DIM + 1, :]
        for r in range(REP):
            h = g * REP + r
            o_ref[:, h * HEAD_DIM:(h + 1) * HEAD_DIM] = out[:, r * tq:(r + 1) * tq].T.astype(o_ref.dtype)


def _dsa(qt, kn, vt, qit, wt, ki, rel_bias, batch, seq):
    tq, tk = DSA_TQ, DSA_TK
    nq = seq // tq
    nkc = seq // tk
    topk = min(TOPK_MAX, seq // 4)
    bidx = jnp.asarray(_bias_bucket_tiles(tq, tk))
    kern = functools.partial(_dsa_kernel, tq=tq, tk=tk, nkc=nkc, topk=topk)
    once = pl.Buffered(1)
    return pl.pallas_call(
        kern,
        grid=(batch, nq),
        in_specs=[pl.BlockSpec(memory_space=pltpu.SMEM),
                  pl.BlockSpec((2, tk, tq), lambda b, i: (0, 0, 0), pipeline_mode=once),
                  pl.BlockSpec((1, ATT_HEADS, HEAD_DIM, tq), lambda b, i: (b * nq + i, 0, 0, 0)),
                  pl.BlockSpec((1, IDX_Q, tq), lambda b, i: (b * nq + i, 0, 0)),
                  pl.BlockSpec((1, IDX_HEADS, tq), lambda b, i: (b * nq + i, 0, 0)),
                  pl.BlockSpec((seq, ATT_KV), lambda b, i: (b, 0), pipeline_mode=once),
                  pl.BlockSpec((1, ATT_KV_HEADS, nkc, VT_ROWS, tk), lambda b, i: (b, 0, 0, 0, 0),
                               pipeline_mode=once),
                  pl.BlockSpec((seq, IDX_DIM), lambda b, i: (b, 0), pipeline_mode=once)],
        out_specs=pl.BlockSpec((tq, ATT_Q), lambda b, i: (b * nq + i, 0)),
        out_shape=jax.ShapeDtypeStruct((batch * seq, ATT_Q), BF16),
        scratch_shapes=[pltpu.VMEM((nkc, tk, tq), I32),
                        pltpu.VMEM((3, nkc, tk, tq), BF16),
                        pltpu.VMEM((nkc + 1, tk, tq), F32),
                        pltpu.VMEM((2, ATT_HEADS, tk, tq), F32),
                        pltpu.VMEM((2, 2 * tk, REP * tq), F32),
                        pltpu.VMEM((2, 2 * tk, REP * tq), BF16),
                        pltpu.VMEM((ATT_KV_HEADS + 4, REP * tq), F32),
                        pltpu.VMEM((ATT_KV_HEADS, VT_ROWS, REP * tq), F32)],
        compiler_params=_params(("arbitrary", "arbitrary")),
        name="dsa_attention",
    )(rel_bias, bidx, qt, qit, wt, kn, vt, ki)


E_ROWS = 3 * CHUNK + BF16_ROWS
CONV_HALO = BF16_ROWS


def _conv_shift_matrix():
    ext = CONV_HALO + CHUNK
    s = np.zeros((CHUNK, CONV_WIDTH * ext), np.float32)
    for k in range(CONV_WIDTH):
        t = np.arange(CHUNK)
        s[t, k * ext + CONV_HALO + t - (CONV_WIDTH - 1) + k] = 1.0
    return s


SSD_INPUTS = 12


def _ssd_kernel(*refs, n_cast, n_side):
    (xbc0_ref, xbcn_ref, halon_ref, z_ref, sm_ref, shift_ref, cw_ref, cb_ref, dtb_ref, alog_ref, dsk_ref,
     ng_ref) = refs[:SSD_INPUTS]
    side_in = refs[SSD_INPUTS:SSD_INPUTS + n_side]
    o_ref = refs[SSD_INPUTS + n_side]
    side_out = refs[SSD_INPUTS + n_side + 1:SSD_INPUTS + 2 * n_side + 1]
    xs_ref, bm_ref, cm_ref, state_ref, ypre_ref, acg_ref, actg_ref = refs[SSD_INPUTS + 2 * n_side + 1:]
    c = pl.program_id(1)
    L = CHUNK
    _side_cast(pl.program_id(0) * pl.num_programs(1) + c, n_cast, side_in, side_out)

    def conv_to(slot, x_ref, halo):
        shift = shift_ref[...]
        cblk = GROUP_W
        for cbi in range(CONV_DIM // cblk):
            cols = slice(cbi * cblk, (cbi + 1) * cblk)
            ext = jnp.concatenate([halo[:, cols], x_ref[:, cols]], axis=0)
            wtap = cw_ref[:, cols].astype(BF16)
            prods = jnp.concatenate([ext * wtap[kk:kk + 1, :] for kk in range(CONV_WIDTH)], axis=0)
            y = _silu(cb_ref[:, cols] + _dot(shift, prods))
            if cbi < SSM_GROUPS:
                xs_ref[slot, cbi] = y
            else:
                per = cblk // SSM_STATE
                for u in range(per):
                    gi = (cbi - SSM_GROUPS) * per + u
                    piece = y[:, u * SSM_STATE:(u + 1) * SSM_STATE]
                    if gi < SSM_GROUPS:
                        bm_ref[slot, gi] = piece
                    else:
                        cm_ref[slot, gi - SSM_GROUPS] = piece

    def scan_chunk(slot):
        _ssd_scan_chunk(slot, z_ref, sm_ref, dtb_ref, alog_ref, dsk_ref, ng_ref, o_ref,
                        xs_ref, bm_ref, cm_ref, state_ref, ypre_ref, acg_ref, actg_ref)

    @pl.when(c == 0)
    def _():
        state_ref[...] = jnp.zeros_like(state_ref)
        conv_to(0, xbc0_ref, jnp.zeros((CONV_HALO, CONV_DIM), BF16))

    @pl.when(c % 2 == 0)
    def _():
        conv_to(1, xbcn_ref, halon_ref[...])
        scan_chunk(0)

    @pl.when(c % 2 == 1)
    def _():
        conv_to(0, xbcn_ref, halon_ref[...])
        scan_chunk(1)


def _ssd_scan_chunk(slot, z_ref, sm_ref, dtb_ref, alog_ref, dsk_ref, ng_ref, o_ref,
                    xs_ref, bm_ref, cm_ref, state_ref, ypre_ref, acg_ref, actg_ref):
    L = CHUNK
    dt_in = sm_ref[:, S_DT:S_DT + SSM_HEADS] + dtb_ref[...]
    dt_act = jnp.maximum(dt_in, 0.0) + jnp.log1p(jnp.exp(-jnp.abs(dt_in)))
    a = dt_act * (-jnp.exp(alog_ref[...]))
    ri = lax.broadcasted_iota(I32, (L, L), 0)
    ci = lax.broadcasted_iota(I32, (L, L), 1)
    tril = ri >= ci
    tri_b = jnp.where(tril, 1.0, 0.0).astype(BF16)
    a_cum = sum(_dot(tri_b, p) for p in _split3(a))
    eye_b = jnp.where(lax.broadcasted_iota(I32, (SSM_HEADS, SSM_HEADS), 0)
                      == lax.broadcasted_iota(I32, (SSM_HEADS, SSM_HEADS), 1), 1.0, 0.0).astype(BF16)
    a_cum_t = sum(_dot_nt(eye_b, p) for p in _split3(a_cum))
    a_last = a_cum[L - 1:L, :]
    for gi in range(SSM_GROUPS):
        acg_ref[gi] = a_cum[:, gi * HEADS_PER_GROUP:(gi + 1) * HEADS_PER_GROUP]
        actg_ref[gi] = a_cum_t[gi * HEADS_PER_GROUP:(gi + 1) * HEADS_PER_GROUP, :]
    cd3 = _split3(jnp.exp(a_last))
    ds3 = _split3(dsk_ref[...])
    extras = jnp.concatenate([p.astype(F32) for p in cd3 + ds3]
                             + [jnp.zeros((E_ROWS - 3 * L - 6, SSM_HEADS), F32)], axis=0)
    e_mat = jnp.concatenate([dt_act, jnp.exp(a_cum), jnp.exp(a_last - a_cum), extras], axis=0).astype(BF16)

    lane = lax.broadcasted_iota(I32, (L, LANES), 1)
    lo_mask = lane < SSM_HEAD_DIM

    def group_body(gi, carry):
        xs = xs_ref[slot, gi]
        bg = bm_ref[slot, gi]
        cg_b = cm_ref[slot, gi].astype(BF16)
        hsel = (lax.broadcasted_iota(I32, (SSM_HEADS, GROUP_W), 0)
                == gi * HEADS_PER_GROUP + lax.broadcasted_iota(I32, (SSM_HEADS, GROUP_W), 1) // SSM_HEAD_DIM)
        ex = _dot(e_mat, jnp.where(hsel, 1.0, 0.0).astype(BF16))
        dt_rep = ex[0:L]
        expa_rep = ex[L:2 * L]
        dte_rep = ex[2 * L:3 * L]
        cd_rep = ex[3 * L:3 * L + 1] + ex[3 * L + 1:3 * L + 2] + ex[3 * L + 2:3 * L + 3]
        dsk_rep = ex[3 * L + 3:3 * L + 4] + ex[3 * L + 4:3 * L + 5] + ex[3 * L + 5:3 * L + 6]

        xd = xs * dt_rep
        xd_b = xd.astype(BF16)
        cb = _dot_nt(cg_b, bg.astype(BF16))
        acg = acg_ref[gi]
        actg = actg_ref[gi]
        pairs = []
        for pj in range(HEADS_PER_GROUP // 2):
            gmat = []
            for e in (2 * pj, 2 * pj + 1):
                seg = acg[:, e:e + 1] - actg[e:e + 1, :]
                dec = jnp.exp(jnp.where(tril, seg, NEG_INF))
                gmat.append((cb * dec).astype(BF16))
            xp = xd_b[:, pj * LANES:(pj + 1) * LANES]
            zero = jnp.zeros_like(xp)
            pairs.append(_dot(gmat[0], jnp.where(lo_mask, xp, zero))
                         + _dot(gmat[1], jnp.where(lo_mask, zero, xp)))
        y_diag = jnp.concatenate(pairs, axis=1)

        st = state_ref[gi]
        y_off = _dot(cg_b, st.astype(BF16)) * expa_rep
        xdd = (xd * dte_rep).astype(BF16)
        state_ref[gi] = st * cd_rep + _dot(bg.T.astype(BF16), xdd)
        ypre_ref[gi] = y_diag + y_off + dsk_rep * xs
        return carry

    lax.fori_loop(0, SSM_GROUPS, group_body, 0, unroll=8)

    for gi in range(SSM_GROUPS):
        cols = slice(gi * GROUP_W, (gi + 1) * GROUP_W)
        zz = z_ref[:, cols].astype(F32)
        y = ypre_ref[gi] * _silu(zz)
        ms = jnp.mean(y * y, axis=-1, keepdims=True)
        o_ref[:, cols] = (y * lax.rsqrt(ms + EPS) * ng_ref[:, cols]).astype(o_ref.dtype)


def _ssd(main, small, conv_w, conv_b, dt_bias, a_log, d_skip, norm_g, side_weights, batch, seq):
    nc = seq // CHUNK
    n_cast, side_specs = _side_cast_specs(side_weights, batch * nc, lambda b, c: b * nc + c)
    hb = CHUNK // CONV_HALO
    row = lambda b, c: b * nc + c
    full = lambda shape: pl.BlockSpec(shape, lambda b, c: (0,) * len(shape))
    shift = jnp.asarray(_conv_shift_matrix(), BF16)
    outs = pl.pallas_call(
        functools.partial(_ssd_kernel, n_cast=n_cast, n_side=len(side_weights)),
        grid=(batch, nc),
        in_specs=[pl.BlockSpec((CHUNK, CONV_DIM), lambda b, c: (row(b, 0), M_XBC // CONV_DIM)),
                  pl.BlockSpec((CHUNK, CONV_DIM),
                               lambda b, c: (row(b, jnp.minimum(c + 1, nc - 1)), M_XBC // CONV_DIM)),
                  pl.BlockSpec((CONV_HALO, CONV_DIM),
                               lambda b, c: (row(b, c) * hb + hb - 1, M_XBC // CONV_DIM)),
                  pl.BlockSpec((CHUNK, SSM_INNER), lambda b, c: (row(b, c), M_Z // SSM_INNER)),
                  pl.BlockSpec((CHUNK, S_END), lambda b, c: (row(b, c), 0)),
                  full((CHUNK, CONV_WIDTH * (CONV_HALO + CHUNK))),
                  full((CONV_WIDTH, CONV_DIM)), full((1, CONV_DIM)), full((1, SSM_HEADS)),
                  full((1, SSM_HEADS)), full((1, SSM_HEADS)), full((1, SSM_INNER))] + side_specs,
        out_specs=[pl.BlockSpec((CHUNK, SSM_INNER), lambda b, c: (row(b, c), 0))] + side_specs,
        out_shape=[jax.ShapeDtypeStruct((batch * seq, SSM_INNER), BF16)]
                  + [jax.ShapeDtypeStruct(w.shape, BF16) for w in side_weights],
        scratch_shapes=[pltpu.VMEM((2, SSM_GROUPS, CHUNK, GROUP_W), F32),
                        pltpu.VMEM((2, SSM_GROUPS, CHUNK, SSM_STATE), F32),
                        pltpu.VMEM((2, SSM_GROUPS, CHUNK, SSM_STATE), F32),
                        pltpu.VMEM((SSM_GROUPS, SSM_STATE, GROUP_W), F32),
                        pltpu.VMEM((SSM_GROUPS, CHUNK, GROUP_W), F32),
                        pltpu.VMEM((SSM_GROUPS, CHUNK, HEADS_PER_GROUP), F32),
                        pltpu.VMEM((SSM_GROUPS, HEADS_PER_GROUP, CHUNK), F32)],
        compiler_params=_params(("arbitrary", "arbitrary")),
        name="ssd_scan",
    )(main, main, main, main, small, shift, conv_w, conv_b, dt_bias, a_log, d_skip, norm_g, *side_weights)
    return outs[0], outs[1:]


def _merge_kernel(att_ref, y_ref, ga_ref, gs_ref, wa_ref, ws_ref, o_ref):
    pa = _dot(att_ref[...], wa_ref[...])
    ps = _dot(y_ref[...], ws_ref[...])
    o_ref[...] = (_sigmoid(ga_ref[...].astype(F32)) * pa + _sigmoid(gs_ref[...].astype(F32)) * ps).astype(o_ref.dtype)


def _merge(att, y, main, wa, ws, tm, tn):
    t = att.shape[0]
    return pl.pallas_call(
        _merge_kernel,
        grid=(t // tm, D_MODEL // tn),
        in_specs=[pl.BlockSpec((tm, ATT_Q), lambda i, j: (i, 0)),
                  pl.BlockSpec((tm, SSM_INNER), lambda i, j: (i, 0)),
                  pl.BlockSpec((tm, tn), lambda i, j: (i, M_GA // tn + j)),
                  pl.BlockSpec((tm, tn), lambda i, j: (i, M_GS // tn + j)),
                  pl.BlockSpec((ATT_Q, tn), lambda i, j: (0, j)),
                  pl.BlockSpec((SSM_INNER, tn), lambda i, j: (0, j))],
        out_specs=pl.BlockSpec((tm, tn), lambda i, j: (i, j)),
        out_shape=jax.ShapeDtypeStruct((t, D_MODEL), BF16),
        compiler_params=_params(("parallel", "arbitrary")),
        name="gated_merge",
    )(att, y, main, main, wa, ws)


def _out_proj_kernel(m_ref, w_ref, x_ref, g_ref, x1_ref, h2_ref):
    x1 = x_ref[...] + _dot(m_ref[...], w_ref[...])
    x1_ref[...] = x1
    ms = jnp.mean(x1 * x1, axis=-1, keepdims=True)
    h2_ref[...] = (x1 * lax.rsqrt(ms + EPS) * g_ref[...]).astype(BF16)


def _out_proj(merged, w, x, g, tm):
    t = x.shape[0]
    return pl.pallas_call(
        _out_proj_kernel,
        grid=(t // tm,),
        in_specs=[pl.BlockSpec((tm, D_MODEL), lambda i: (i, 0)),
                  pl.BlockSpec((D_MODEL, D_MODEL), lambda i: (0, 0)),
                  pl.BlockSpec((tm, D_MODEL), lambda i: (i, 0)),
                  pl.BlockSpec((1, D_MODEL), lambda i: (0, 0))],
        out_specs=[pl.BlockSpec((tm, D_MODEL), lambda i: (i, 0)),
                   pl.BlockSpec((tm, D_MODEL), lambda i: (i, 0))],
        out_shape=[jax.ShapeDtypeStruct((t, D_MODEL), F32),
                   jax.ShapeDtypeStruct((t, D_MODEL), BF16)],
        compiler_params=_params(("parallel",)),
        name="out_proj_norm",
    )(merged, w, x, g)


def _mlp_kernel(h_ref, x1_ref, wu_ref, wd_ref, o_ref):
    @pl.when(pl.program_id(1) == 0)
    def _():
        o_ref[...] = x1_ref[...]

    u = _dot(h_ref[...], wu_ref[...])
    u = jnp.square(jnp.maximum(u, 0.0)).astype(BF16)
    o_ref[...] += _dot(u, wd_ref[...])


def _mlp(h2, x1, wu, wd, tm, th):
    t = h2.shape[0]
    return pl.pallas_call(
        _mlp_kernel,
        grid=(t // tm, MLP_HIDDEN // th),
        in_specs=[pl.BlockSpec((tm, D_MODEL), lambda i, j: (i, 0)),
                  pl.BlockSpec((tm, D_MODEL), lambda i, j: (i, 0)),
                  pl.BlockSpec((D_MODEL, th), lambda i, j: (0, j)),
                  pl.BlockSpec((th, D_MODEL), lambda i, j: (j, 0))],
        out_specs=pl.BlockSpec((tm, D_MODEL), lambda i, j: (i, 0)),
        out_shape=jax.ShapeDtypeStruct((t, D_MODEL), F32),
        compiler_params=_params(("parallel", "arbitrary")),
        name="relu2_mlp",
    )(h2, x1, wu, wd)


def _pack_w_small(w_t):
    assert (O_GA, O_KI, O_Z, O_DT) == (M_GA, M_Z, O_WI + IDX_HEADS, O_Z + M_END - M_Z)
    return jnp.concatenate([w_t[O_KI:O_Z], w_t[O_DT:O_DT + SSM_HEADS],
                            jnp.zeros((S_END - S_DT - SSM_HEADS, w_t.shape[1]), w_t.dtype)], axis=0)


def _block(x2, batch, seq, norm1_g, w_in, conv_w, conv_b, dt_bias, a_log, d_skip, ssm_norm_g, q_norm_g,
           k_norm_g, rel_bias, w_att_branch, w_ssm_branch, w_out, norm2_g, w_up, w_down):
    row = lambda v: v.reshape(1, -1)
    w_t = w_in.T
    xn, small = _norm_small(x2, row(norm1_g), _pack_w_small(w_t), tm=min(1024, x2.shape[0]))
    main = _in_proj(xn, w_t, tm=min(1024, x2.shape[0]))
    qt, kn, vt, qit, wt, ki = _qk_prep(main, small, row(q_norm_g), row(k_norm_g), batch, seq, tq=DSA_TQ)
    att = _dsa(qt, kn, vt, qit, wt, ki, rel_bias, batch, seq)
    y, (w_att_b, w_ssm_b, w_out_b, w_up_b, w_down_b) = _ssd(
        main, small, conv_w, row(conv_b), row(dt_bias), row(a_log), row(d_skip), row(ssm_norm_g),
        (w_att_branch, w_ssm_branch, w_out, w_up, w_down), batch, seq)
    merged = _merge(att, y, main, w_att_b, w_ssm_b, tm=512, tn=1024)
    x1, h2 = _out_proj(merged, w_out_b, x2, row(norm2_g), tm=512)
    return _mlp(h2, x1, w_up_b, w_down_b, tm=512, th=1024)


def kernel(x, norm1_g, w_in, conv_w, conv_b, dt_bias, a_log, d_skip, ssm_norm_g, q_norm_g, k_norm_g, rel_bias,
           w_att_branch, w_ssm_branch, w_out, norm2_g, w_up, w_down):
    batch, seq, d = x.shape
    x2 = x.reshape(batch * seq, d)
    for l in range(norm1_g.shape[0]):
        x2 = _block(x2, batch, seq, norm1_g[l], w_in[l], conv_w[l], conv_b[l], dt_bias[l], a_log[l], d_skip[l],
                    ssm_norm_g[l], q_norm_g[l], k_norm_g[l], rel_bias, w_att_branch[l], w_ssm_branch[l],
                    w_out[l], norm2_g[l], w_up[l], w_down[l])
    return x2.reshape(batch, seq, d)
```

```python
import functools
import math

import numpy as np
import jax
import jax.numpy as jnp
from jax import lax
from jax.experimental import pallas as pl
from jax.experimental.pallas import tpu as pltpu

F32 = jnp.float32
BF16 = jnp.bfloat16
I32 = jnp.int32

D_MODEL = 2048
ATT_HEADS = 16
ATT_KV_HEADS = 4
HEAD_DIM = 128
REP = ATT_HEADS // ATT_KV_HEADS
IDX_HEADS = 16
IDX_DIM = 64
TOPK_MAX = 256
N_BUCKETS = 32
MAX_DISTANCE = 128
SSM_INNER = 2 * D_MODEL
SSM_HEAD_DIM = 64
SSM_HEADS = SSM_INNER // SSM_HEAD_DIM
SSM_GROUPS = 8
SSM_STATE = 128
CONV_WIDTH = 4
CHUNK = 128
MLP_HIDDEN = 4 * D_MODEL
EPS = 1e-6

ATT_Q = ATT_HEADS * HEAD_DIM
ATT_KV = ATT_KV_HEADS * HEAD_DIM
IDX_Q = IDX_HEADS * IDX_DIM
SSM_BC = SSM_GROUPS * SSM_STATE
CONV_DIM = SSM_INNER + 2 * SSM_BC
SPLITS = (D_MODEL, D_MODEL, ATT_Q, ATT_KV, ATT_KV, IDX_Q, IDX_DIM, IDX_HEADS, SSM_INNER, CONV_DIM, SSM_HEADS)
_OFFS = tuple(int(v) for v in np.cumsum((0,) + SPLITS))
(O_GA, O_GS, O_Q, O_K, O_V, O_QI, O_KI, O_WI, O_Z, O_XBC, O_DT, _O_END) = _OFFS

M_GA = 0
M_GS = M_GA + D_MODEL
M_Q = M_GS + D_MODEL
M_K = M_Q + ATT_Q
M_V = M_K + ATT_KV
M_QI = M_V + ATT_KV
M_Z = M_QI + IDX_Q
M_XBC = M_Z + SSM_INNER
M_END = M_XBC + CONV_DIM
S_KI = 0
S_WI = S_KI + IDX_DIM
S_DT = S_WI + IDX_HEADS
S_END = 256

HEADS_PER_GROUP = SSM_HEADS // SSM_GROUPS
GROUP_W = HEADS_PER_GROUP * SSM_HEAD_DIM

LANES = 128
SUBLANES = 8
BF16_ROWS = 2 * SUBLANES
VMEM_LIMIT = 56 * 1024 * 1024

NEG_INF = float("-inf")
INT_MIN = -(2 ** 31)
KEY_NEG_INF = int(np.int32(np.uint32(0xFF800000) ^ np.uint32(0x7FFFFFFF)))
BF16_MIN_NORMAL_BITS = 0x0080


def _dot(a, b):
    return jnp.dot(a, b, preferred_element_type=F32)


def _dot_nt(a, b):
    return lax.dot_general(a, b, (((1,), (1,)), ((), ())), preferred_element_type=F32)


def _split3(x):
    hi = x.astype(BF16)
    r = x - hi.astype(F32)
    mid = r.astype(BF16)
    lo = (r - mid.astype(F32)).astype(BF16)
    return hi, mid, lo


def _silu(x):
    h = 0.5 * x
    return h + h * jnp.tanh(h)


def _sigmoid(x):
    return 0.5 + 0.5 * jnp.tanh(0.5 * x)


def _params(sem):
    return pltpu.CompilerParams(dimension_semantics=sem, vmem_limit_bytes=VMEM_LIMIT)


def _norm_small_kernel(x_ref, g_ref, w_ref, xn_ref, sm_ref):
    x = x_ref[...]
    ms = jnp.mean(x * x, axis=-1, keepdims=True)
    xn = (x * lax.rsqrt(ms + EPS) * g_ref[...]).astype(BF16)
    xn_ref[...] = xn
    sm_ref[...] = _dot_nt(xn, w_ref[...].astype(BF16))


def _norm_small(x, g, w_small_t, tm):
    m, d = x.shape
    n = w_small_t.shape[0]
    return pl.pallas_call(
        _norm_small_kernel,
        grid=(m // tm,),
        in_specs=[pl.BlockSpec((tm, d), lambda i: (i, 0)),
                  pl.BlockSpec((1, d), lambda i: (0, 0)),
                  pl.BlockSpec((n, d), lambda i: (0, 0))],
        out_specs=[pl.BlockSpec((tm, d), lambda i: (i, 0)),
                   pl.BlockSpec((tm, n), lambda i: (i, 0))],
        out_shape=[jax.ShapeDtypeStruct((m, d), BF16), jax.ShapeDtypeStruct((m, n), F32)],
        compiler_params=_params(("parallel",)),
        name="norm_small_proj",
    )(x, g, w_small_t)


IN_TN = 1024
IN_ALIGNED_TILES = O_KI // IN_TN
IN_SHIFT = O_Z - M_Z
IN_NEXT = 128
IN_ROWS = 256


def _in_proj_kernel(*refs, n_cast, n_side):
    xn_ref, wa_ref, wn_ref = refs[:3]
    side_in = refs[3:3 + n_side]
    o_ref = refs[3 + n_side]
    side_out = refs[4 + n_side:4 + 2 * n_side]
    w_scr = refs[4 + 2 * n_side]
    j = pl.program_id(0)
    i = pl.program_id(1)
    _side_cast(j * pl.num_programs(1) + i, n_cast, side_in, side_out)

    @pl.when((i == 0) & (j < IN_ALIGNED_TILES))
    def _():
        for r0 in range(0, IN_TN, IN_ROWS):
            w_scr[r0:r0 + IN_ROWS, :] = wa_ref[r0:r0 + IN_ROWS, :].astype(BF16)

    @pl.when((i == 0) & (j >= IN_ALIGNED_TILES))
    def _():
        for r0 in range(0, IN_TN - IN_SHIFT, IN_ROWS):
            r1 = min(r0 + IN_ROWS, IN_TN - IN_SHIFT)
            w_scr[r0:r1, :] = wa_ref[r0 + IN_SHIFT:r1 + IN_SHIFT, :].astype(BF16)
        w_scr[IN_TN - IN_SHIFT:IN_TN, :] = wn_ref[0:IN_SHIFT, :].astype(BF16)

    o_ref[...] = _dot_nt(xn_ref[...], w_scr[...]).astype(o_ref.dtype)


def _in_proj(xn, w_t, side_weights, tm):
    m, d = xn.shape
    assert O_KI % IN_TN == 0 and M_END % IN_TN == 0 and IN_TN % IN_NEXT == 0
    assert 0 < IN_SHIFT <= IN_NEXT and IN_SHIFT % BF16_ROWS == 0
    nj, ni = M_END // IN_TN, m // tm
    n_cast, side_specs = _side_cast_specs(side_weights, nj * ni, lambda j, i: j * ni + i)
    outs = pl.pallas_call(
        functools.partial(_in_proj_kernel, n_cast=n_cast, n_side=len(side_weights)),
        grid=(nj, ni),
        in_specs=[pl.BlockSpec((tm, d), lambda j, i: (i, 0)),
                  pl.BlockSpec((IN_TN, d), lambda j, i: (j, 0)),
                  pl.BlockSpec((IN_NEXT, d), lambda j, i: ((j + 1) * (IN_TN // IN_NEXT), 0))] + side_specs,
        out_specs=[pl.BlockSpec((tm, IN_TN), lambda j, i: (i, j))] + side_specs,
        out_shape=[jax.ShapeDtypeStruct((m, M_END), BF16)]
                  + [jax.ShapeDtypeStruct(w.shape, BF16) for w in side_weights],
        scratch_shapes=[pltpu.VMEM((IN_TN, d), BF16)],
        compiler_params=_params(("arbitrary", "arbitrary")),
        name="in_proj",
    )(xn, w_t, w_t, *side_weights)
    return outs[0], outs[1:]


def _side_cast_specs(weights, n_steps, step_of):
    n_cast = 1 << (n_steps.bit_length() - 1)
    specs = []
    for w in weights:
        rows = w.shape[0] // n_cast
        assert w.shape[0] % n_cast == 0 and rows % BF16_ROWS == 0
        specs.append(pl.BlockSpec((rows, w.shape[1]), lambda *ids: (jnp.minimum(step_of(*ids), n_cast - 1), 0)))
    return n_cast, specs


def _side_cast(step, n_cast, srcs, dsts):
    @pl.when(step < n_cast)
    def _():
        for src, dst in zip(srcs, dsts):
            dst[...] = src[...].astype(BF16)


LOG2E = math.log2(math.e)
VT_ROWS = HEAD_DIM + BF16_ROWS


def _qk_prep_kernel(q_ref, k_ref, v_ref, qi_ref, sm_ref, qg_ref, kg_ref,
                    qt_ref, kn_ref, vt_ref, qit_ref, wt_ref, ki_ref):
    qg = qg_ref[...]
    for h in range(ATT_HEADS):
        x = q_ref[:, h * HEAD_DIM:(h + 1) * HEAD_DIM].astype(F32)
        ms = jnp.mean(x * x, axis=-1, keepdims=True)
        y = x * lax.rsqrt(ms + EPS) * qg * (HEAD_DIM ** -0.5 * LOG2E)
        qt_ref[0, h] = y.T.astype(BF16)
    kg = kg_ref[...]
    for h in range(ATT_KV_HEADS):
        x = k_ref[:, h * HEAD_DIM:(h + 1) * HEAD_DIM].astype(F32)
        ms = jnp.mean(x * x, axis=-1, keepdims=True)
        kn_ref[:, h * HEAD_DIM:(h + 1) * HEAD_DIM] = (x * lax.rsqrt(ms + EPS) * kg).astype(BF16)
        vt_ref[0, h, 0, 0:HEAD_DIM, :] = v_ref[:, h * HEAD_DIM:(h + 1) * HEAD_DIM].astype(F32).T.astype(BF16)
        vt_ref[0, h, 0, HEAD_DIM:VT_ROWS, :] = jnp.ones((VT_ROWS - HEAD_DIM, v_ref.shape[0]), BF16)
    for p in range(IDX_Q // LANES):
        qit_ref[0, p * LANES:(p + 1) * LANES, :] = qi_ref[:, p * LANES:(p + 1) * LANES].astype(F32).T.astype(BF16)
    sm_t = sm_ref[:, 0:LANES].T
    wt_ref[0] = sm_t[S_WI:S_WI + IDX_HEADS, :] * (IDX_HEADS ** -0.5 * IDX_DIM ** -0.5)
    ki_ref[...] = sm_ref[:, S_KI:S_KI + IDX_DIM].astype(BF16)


def _qk_prep(main, small, qg, kg, batch, seq, tq):
    t = main.shape[0]
    nq = seq // tq
    return pl.pallas_call(
        _qk_prep_kernel,
        grid=(batch, nq),
        in_specs=[pl.BlockSpec((tq, ATT_Q), lambda b, i: (b * nq + i, M_Q // ATT_Q)),
                  pl.BlockSpec((tq, ATT_KV), lambda b, i: (b * nq + i, M_K // ATT_KV)),
                  pl.BlockSpec((tq, ATT_KV), lambda b, i: (b * nq + i, M_V // ATT_KV)),
                  pl.BlockSpec((tq, IDX_Q), lambda b, i: (b * nq + i, M_QI // IDX_Q)),
                  pl.BlockSpec((tq, S_END), lambda b, i: (b * nq + i, 0)),
                  pl.BlockSpec((1, HEAD_DIM), lambda b, i: (0, 0)),
                  pl.BlockSpec((1, HEAD_DIM), lambda b, i: (0, 0))],
        out_specs=[pl.BlockSpec((1, ATT_HEADS, HEAD_DIM, tq), lambda b, i: (b * nq + i, 0, 0, 0)),
                   pl.BlockSpec((tq, ATT_KV), lambda b, i: (b * nq + i, 0)),
                   pl.BlockSpec((1, ATT_KV_HEADS, 1, VT_ROWS, tq), lambda b, i: (b, 0, i, 0, 0)),
                   pl.BlockSpec((1, IDX_Q, tq), lambda b, i: (b * nq + i, 0, 0)),
                   pl.BlockSpec((1, IDX_HEADS, tq), lambda b, i: (b * nq + i, 0, 0)),
                   pl.BlockSpec((tq, IDX_DIM), lambda b, i: (b * nq + i, 0))],
        out_shape=[jax.ShapeDtypeStruct((t // tq, ATT_HEADS, HEAD_DIM, tq), BF16),
                   jax.ShapeDtypeStruct((t, ATT_KV), BF16),
                   jax.ShapeDtypeStruct((batch, ATT_KV_HEADS, nq, VT_ROWS, tq), BF16),
                   jax.ShapeDtypeStruct((t // tq, IDX_Q, tq), BF16),
                   jax.ShapeDtypeStruct((t // tq, IDX_HEADS, tq), F32),
                   jax.ShapeDtypeStruct((t, IDX_DIM), BF16)],
        compiler_params=_params(("parallel", "parallel")),
        name="qk_prep",
    )(main, main, main, main, small, qg, kg)


DSA_TQ = 256
DSA_TK = 256
CNT_WAYS = 4


def _t5_bucket_np(dist):
    n = np.maximum(dist, 0)
    max_exact = N_BUCKETS // 2
    nf = np.maximum(n, 1).astype(np.float32)
    ratio = (np.log(nf / np.float32(max_exact)) / np.float32(math.log(MAX_DISTANCE / max_exact))
             * np.float32(N_BUCKETS - max_exact))
    large = max_exact + ratio.astype(np.int32)
    large = np.minimum(large, N_BUCKETS - 1)
    return np.where(n < max_exact, n, large).astype(np.int32)


def _bias_bucket_tiles(tq, tk):
    r = np.arange(tq)[None, :]
    c = np.arange(tk)[:, None]
    d0 = _t5_bucket_np(r - c)
    d1 = _t5_bucket_np(tk + r - c)
    assert np.all(_t5_bucket_np(np.arange(tk + 1, 8 * tk)) == N_BUCKETS - 1)
    return np.stack([d0, d1]).astype(np.int32)


def _dsa_kernel(relb_ref, bidx_ref, qt_ref, qit_ref, wt_ref, k_ref, vt_ref, ki_ref, o_ref,
                key_ref, plane_ref, madd_ref, bias_ref, s_buf, p_buf, st_ref, acc_ref, *, tq, tk, nkc, topk):
    b = pl.program_id(0)
    i = pl.program_id(1)
    neg_slot = nkc

    @pl.when((b == 0) & (i == 0))
    def _init():
        madd_ref[neg_slot] = jnp.full((tk, tq), NEG_INF, F32)
        for t in range(2):
            bt = bidx_ref[t]

            def head_body(h, carry):
                far = relb_ref[N_BUCKETS - 1, h]

                def bucket_body(bk, acc):
                    return jnp.where(bt == bk, (relb_ref[bk, h] - far) * LOG2E, acc)

                bias_ref[t, h] = lax.fori_loop(0, N_BUCKETS, bucket_body, jnp.zeros((tk, tq), F32))
                return carry

            lax.fori_loop(0, ATT_HEADS, head_body, 0)

    def _select():
        nj = i + 1
        qpos = i * tq + lax.broadcasted_iota(I32, (tk, tq), 1)

        def score_chunk(j):
            kc = ki_ref[pl.ds(pl.multiple_of(j * tk, tk), tk), :]
            acc = jnp.zeros((tk, tq), F32)
            for h in range(IDX_HEADS):
                z = _dot(kc, qit_ref[0, h * IDX_DIM:(h + 1) * IDX_DIM, :])
                acc = acc + jnp.maximum(z, 0.0) * wt_ref[0, h:h + 1, :]
            kpos = j * tk + lax.broadcasted_iota(I32, (tk, tq), 0)
            acc = jnp.where(kpos <= qpos, acc, NEG_INF)
            bits = pltpu.bitcast(acc, I32)
            bits = jnp.where(bits == INT_MIN, 0, bits)
            key = bits ^ ((bits >> 31) & 0x7FFFFFFF)
            key_ref[j] = key
            plane_ref[0, j] = pltpu.bitcast(bits & jnp.int32(-65536), F32).astype(BF16)
            plane_ref[1, j] = ((key >> 8) & 0xFF).astype(F32).astype(BF16)
            plane_ref[2, j] = (key & 0xFF).astype(F32).astype(BF16)

        def score_quad(u, carry):
            for v in range(4):
                score_chunk(4 * u + v)
            return carry

        lax.fori_loop(0, nj // 4, score_quad, 0)
        done = (nj // 4) * 4

        @pl.when(nj % 4 >= 2)
        def _():
            score_chunk(done)
            score_chunk(done + 1)

        @pl.when(nj % 2 == 1)
        def _():
            score_chunk(nj - 1)

        kf = float(topk)
        one_b = jnp.ones((tk, tq), BF16)
        zero_b = jnp.zeros((tk, tq), BF16)
        grp = BF16_ROWS * CNT_WAYS

        def count(plane, cand_b, strict):
            def body(j, cnt):
                pv = plane_ref[plane, j]
                hit = jnp.where(pv > cand_b if strict else pv >= cand_b, one_b, zero_b)
                for r0 in range(0, tk, grp):
                    cnt = cnt + hit[r0:r0 + grp]
                return cnt

            cnt = lax.fori_loop(0, nj, body, jnp.zeros((grp, tq), BF16))
            return jnp.sum(cnt.astype(F32), axis=0, keepdims=True)

        def keep_ties(src, dst, tie_b):
            def body(j, carry):
                plane_ref[dst, j] = jnp.where(plane_ref[src, j] == tie_b, plane_ref[dst, j], -one_b)
                return carry

            lax.fori_loop(0, nj, body, 0)

        def top_digit_float(p16):
            p16 = jnp.where((p16 > 0) & (p16 < BF16_MIN_NORMAL_BITS), BF16_MIN_NORMAL_BITS, p16)
            fb = (p16 & 0xFFFF) ^ jnp.where(p16 < 0, 0x7FFF, 0)
            return pltpu.bitcast(lax.shift_left(fb, 16), F32).astype(BF16)

        def top_iter(it, prefix):
            cand = prefix + lax.shift_left(jnp.int32(1), 15 - it)
            return jnp.where(count(0, top_digit_float(cand), False) >= kf, cand, prefix)

        d_top = lax.fori_loop(0, 16, top_iter, jnp.full((1, tq), -32768, I32))
        t_top = top_digit_float(d_top)
        above = count(0, t_top, True)

        def byte_digit(plane, above_n):
            def it_body(it, prefix):
                cand = prefix + lax.shift_left(jnp.int32(1), 7 - it)
                tot = above_n + count(plane, cand.astype(F32).astype(BF16), False)
                return jnp.where(tot >= kf, cand, prefix)

            return lax.fori_loop(0, 8, it_body, jnp.zeros((1, tq), I32))

        keep_ties(0, 1, t_top)
        d_mid = byte_digit(1, above)
        t_mid = d_mid.astype(F32).astype(BF16)
        above = above + count(1, t_mid, True)
        keep_ties(1, 2, t_mid)
        d_low = byte_digit(2, above)
        t_low = d_low.astype(F32).astype(BF16)
        thr = lax.shift_left(d_top, 16) | lax.shift_left(d_mid, 8) | d_low
        n_ge = above + count(2, t_low, False)
        tied = jnp.max(n_ge) > kf

        @pl.when(jnp.logical_not(tied))
        def _():
            thr_c = jnp.maximum(thr, KEY_NEG_INF + 1)

            def madd_chunk(j, carry):
                madd_ref[j] = jnp.where(key_ref[j] >= thr_c, 0.0, NEG_INF)
                return carry

            lax.fori_loop(0, nj, madd_chunk, 0)

        @pl.when(tied)
        def _():
            need = kf - (above + count(2, t_low, True))
            row = lax.broadcasted_iota(I32, (tk, tq), 0)

            def pos_iter(it, q_pos):
                cand = q_pos + lax.shift_left(jnp.int32(1), (nkc * tk).bit_length() - 2 - it)

                def body(j, cnt):
                    hit = jnp.where(key_ref[j] == thr, jnp.where(row + j * tk < cand, 1.0, 0.0), 0.0)
                    return cnt + jnp.sum(hit.reshape(tk // SUBLANES, SUBLANES, tq), axis=0)

                cnt = lax.fori_loop(0, nj, body, jnp.zeros((SUBLANES, tq), F32))
                return jnp.where(jnp.sum(cnt, axis=0, keepdims=True) < need, cand, q_pos)

            last = lax.fori_loop(0, (nkc * tk).bit_length() - 1, pos_iter, jnp.zeros((1, tq), I32))
            thr_c = jnp.maximum(thr, KEY_NEG_INF)

            def madd_chunk(j, carry):
                kk = key_ref[j]
                take_tie = jnp.where(row + j * tk <= last, jnp.where(kk > KEY_NEG_INF, 0.0, NEG_INF), NEG_INF)
                madd_ref[j] = jnp.where(kk > thr_c, 0.0, jnp.where(kk == thr, take_tie, NEG_INF))
                return carry

            lax.fori_loop(0, nj, madd_chunk, 0)

    _select()

    ng = ATT_KV_HEADS
    c0 = jnp.maximum(i - 1, 0)
    c1 = jnp.minimum(c0 + 1, nkc - 1)
    n_far = (c0 + 1) // 2

    def scores_to(slot, g, ca, madd_c, bias_c):
        rows = pl.ds(pl.multiple_of(ca * tk, tk), 2 * tk)
        qt = jnp.concatenate([qt_ref[0, g * REP + r] for r in range(REP)], axis=1)
        add = jnp.concatenate([madd_c if bias_c is None else madd_c + bias_c[r] for r in range(REP)], axis=1)
        s = _dot(k_ref[rows, g * HEAD_DIM:(g + 1) * HEAD_DIM], qt) + add
        s_buf[slot] = s
        st_ref[ng + 2 + slot:ng + 3 + slot, :] = jnp.max(s, axis=0, keepdims=True)

    def softmax_to(slot, g):
        m = st_ref[g:g + 1, :]
        m_new = jnp.maximum(m, st_ref[ng + 2 + slot:ng + 3 + slot, :])
        m_safe = jnp.where(m_new == NEG_INF, 0.0, m_new)
        alpha = jnp.exp2(m - m_safe)
        p_buf[slot] = jnp.exp2((s_buf[slot] - m_safe).astype(BF16))
        st_ref[g:g + 1, :] = m_new
        st_ref[ng + slot:ng + 1 + slot, :] = alpha

    def values_from(slot, g, ca, cb):
        vt_c = jnp.concatenate([vt_ref[0, g, ca], vt_ref[0, g, cb]], axis=1)
        acc_ref[g] = st_ref[ng + slot:ng + 1 + slot, :] * acc_ref[g] + _dot(vt_c, p_buf[slot])

    def near_scores_to(slot, g):
        first = i == 0
        t_a = jnp.where(first, 0, 1)
        idx_b = jnp.where(first, neg_slot, c0 + 1)
        madd_near = jnp.concatenate([madd_ref[c0], madd_ref[idx_b]], axis=0)
        bias_near = [jnp.concatenate([bias_ref[t_a, g * REP + r], bias_ref[0, g * REP + r]], axis=0)
                     for r in range(REP)]
        scores_to(slot, g, c0, madd_near, bias_near)

    def far_scores_to(slot, g, k):
        ca = jnp.minimum(2 * (k - 1), nkc - 2)
        cb = jnp.where(ca + 1 >= c0, neg_slot, ca + 1)
        scores_to(slot, g, ca, jnp.concatenate([madd_ref[ca], madd_ref[cb]], axis=0), None)

    st_ref[0:ng, :] = jnp.full((ng, REP * tq), NEG_INF, F32)
    acc_ref[...] = jnp.zeros_like(acc_ref)

    near_scores_to(0, 0)
    for g in range(ng):
        if g + 1 < ng:
            near_scores_to((g + 1) % 2, g + 1)
        else:
            far_scores_to(0, 0, 1)
        softmax_to(g % 2, g)
        values_from(g % 2, g, c0, c1)

    def far_step(k, carry):
        for g in range(ng):
            if g + 1 < ng:
                far_scores_to((g + 1) % 2, g + 1, k)
            else:
                far_scores_to(0, 0, k + 1)
            softmax_to(g % 2, g)
            values_from(g % 2, g, 2 * (k - 1), 2 * (k - 1) + 1)
        return carry

    lax.fori_loop(1, n_far + 1, far_step, 0)

    for g in range(ng):
        out = acc_ref[g, 0:HEAD_DIM, :] / acc_ref[g, HEAD_DIM:HEAD_DIM + 1, :]
        for r in range(REP):
            h = g * REP + r
            o_ref[:, h * HEAD_DIM:(h + 1) * HEAD_DIM] = out[:, r * tq:(r + 1) * tq].T.astype(o_ref.dtype)


def _dsa(qt, kn, vt, qit, wt, ki, rel_bias, batch, seq):
    tq, tk = DSA_TQ, DSA_TK
    nq = seq // tq
    nkc = seq // tk
    topk = min(TOPK_MAX, seq // 4)
    bidx = jnp.asarray(_bias_bucket_tiles(tq, tk))
    kern = functools.partial(_dsa_kernel, tq=tq, tk=tk, nkc=nkc, topk=topk)
    once = pl.Buffered(1)
    return pl.pallas_call(
        kern,
        grid=(batch, nq),
        in_specs=[pl.BlockSpec(memory_space=pltpu.SMEM),
                  pl.BlockSpec((2, tk, tq), lambda b, i: (0, 0, 0), pipeline_mode=once),
                  pl.BlockSpec((1, ATT_HEADS, HEAD_DIM, tq), lambda b, i: (b * nq + i, 0, 0, 0)),
                  pl.BlockSpec((1, IDX_Q, tq), lambda b, i: (b * nq + i, 0, 0)),
                  pl.BlockSpec((1, IDX_HEADS, tq), lambda b, i: (b * nq + i, 0, 0)),
                  pl.BlockSpec((seq, ATT_KV), lambda b, i: (b, 0), pipeline_mode=once),
                  pl.BlockSpec((1, ATT_KV_HEADS, nkc, VT_ROWS, tk), lambda b, i: (b, 0, 0, 0, 0),
                               pipeline_mode=once),
                  pl.BlockSpec((seq, IDX_DIM), lambda b, i: (b, 0), pipeline_mode=once)],
        out_specs=pl.BlockSpec((tq, ATT_Q), lambda b, i: (b * nq + i, 0)),
        out_shape=jax.ShapeDtypeStruct((batch * seq, ATT_Q), BF16),
        scratch_shapes=[pltpu.VMEM((nkc, tk, tq), I32),
                        pltpu.VMEM((3, nkc, tk, tq), BF16),
                        pltpu.VMEM((nkc + 1, tk, tq), F32),
                        pltpu.VMEM((2, ATT_HEADS, tk, tq), F32),
                        pltpu.VMEM((2, 2 * tk, REP * tq), F32),
                        pltpu.VMEM((2, 2 * tk, REP * tq), BF16),
                        pltpu.VMEM((ATT_KV_HEADS + 4, REP * tq), F32),
                        pltpu.VMEM((ATT_KV_HEADS, VT_ROWS, REP * tq), F32)],
        compiler_params=_params(("arbitrary", "arbitrary")),
        name="dsa_attention",
    )(rel_bias, bidx, qt, qit, wt, kn, vt, ki)


E_ROWS = 3 * CHUNK + BF16_ROWS
CONV_HALO = BF16_ROWS


def _conv_shift_matrix():
    ext = CONV_HALO + CHUNK
    s = np.zeros((CHUNK, CONV_WIDTH * ext), np.float32)
    for k in range(CONV_WIDTH):
        t = np.arange(CHUNK)
        s[t, k * ext + CONV_HALO + t - (CONV_WIDTH - 1) + k] = 1.0
    return s


def _ssd_kernel(xbc0_ref, xbcn_ref, halon_ref, z_ref, sm_ref, shift_ref, cw_ref, cb_ref, dtb_ref, alog_ref,
                dsk_ref, ng_ref, o_ref, xs_ref, bm_ref, cm_ref, state_ref, ypre_ref, acg_ref, actg_ref, *, batch):
    c = pl.program_id(0)

    def conv_to(slot, bb, x_ref, halo):
        shift = shift_ref[...]
        cblk = GROUP_W
        for cbi in range(CONV_DIM // cblk):
            cols = slice(cbi * cblk, (cbi + 1) * cblk)
            ext = jnp.concatenate([halo[:, cols], x_ref[bb, :, cols]], axis=0)
            wtap = cw_ref[:, cols].astype(BF16)
            prods = jnp.concatenate([ext * wtap[kk:kk + 1, :] for kk in range(CONV_WIDTH)], axis=0)
            y = _silu(cb_ref[:, cols] + _dot(shift, prods))
            if cbi < SSM_GROUPS:
                xs_ref[bb, slot, cbi] = y
            else:
                per = cblk // SSM_STATE
                for u in range(per):
                    gi = (cbi - SSM_GROUPS) * per + u
                    piece = y[:, u * SSM_STATE:(u + 1) * SSM_STATE]
                    if gi < SSM_GROUPS:
                        bm_ref[bb, slot, gi] = piece
                    else:
                        cm_ref[bb, slot, gi - SSM_GROUPS] = piece

    def scan_chunk(slot):
        _ssd_scan_chunk(slot, batch, z_ref, sm_ref, dtb_ref, alog_ref, dsk_ref, ng_ref, o_ref,
                        xs_ref, bm_ref, cm_ref, state_ref, ypre_ref, acg_ref, actg_ref)

    @pl.when(c == 0)
    def _():
        state_ref[...] = jnp.zeros_like(state_ref)
        for bb in range(batch):
            conv_to(0, bb, xbc0_ref, jnp.zeros((CONV_HALO, CONV_DIM), BF16))

    @pl.when(c % 2 == 0)
    def _():
        for bb in range(batch):
            conv_to(1, bb, xbcn_ref, halon_ref[bb])
        scan_chunk(0)

    @pl.when(c % 2 == 1)
    def _():
        for bb in range(batch):
            conv_to(0, bb, xbcn_ref, halon_ref[bb])
        scan_chunk(1)


def _ssd_scan_prep(sm_ref, dtb_ref, alog_ref, dsk_ref, acg_ref, actg_ref, tril):
    L = CHUNK
    dt_in = sm_ref[:, S_DT:S_DT + SSM_HEADS] + dtb_ref[...]
    dt_act = jnp.maximum(dt_in, 0.0) + jnp.log1p(jnp.exp(-jnp.abs(dt_in)))
    a = dt_act * (-jnp.exp(alog_ref[...]))
    tri_b = jnp.where(tril, 1.0, 0.0).astype(BF16)
    a_cum = sum(_dot(tri_b, p) for p in _split3(a))
    eye_b = jnp.where(lax.broadcasted_iota(I32, (SSM_HEADS, SSM_HEADS), 0)
                      == lax.broadcasted_iota(I32, (SSM_HEADS, SSM_HEADS), 1), 1.0, 0.0).astype(BF16)
    a_cum_t = sum(_dot_nt(eye_b, p) for p in _split3(a_cum))
    a_last = a_cum[L - 1:L, :]
    for gi in range(SSM_GROUPS):
        acg_ref[gi] = a_cum[:, gi * HEADS_PER_GROUP:(gi + 1) * HEADS_PER_GROUP]
        actg_ref[gi] = a_cum_t[gi * HEADS_PER_GROUP:(gi + 1) * HEADS_PER_GROUP, :]
    cd3 = _split3(jnp.exp(a_last))
    ds3 = _split3(dsk_ref[...])
    extras = jnp.concatenate([p.astype(F32) for p in cd3 + ds3]
                             + [jnp.zeros((E_ROWS - 3 * L - 6, SSM_HEADS), F32)], axis=0)
    return jnp.concatenate([dt_act, jnp.exp(a_cum), jnp.exp(a_last - a_cum), extras], axis=0).astype(BF16)


def _ssd_scan_chunk(slot, batch, z_ref, sm_ref, dtb_ref, alog_ref, dsk_ref, ng_ref, o_ref,
                    xs_ref, bm_ref, cm_ref, state_ref, ypre_ref, acg_ref, actg_ref):
    L = CHUNK
    ri = lax.broadcasted_iota(I32, (L, L), 0)
    ci = lax.broadcasted_iota(I32, (L, L), 1)
    tril = ri >= ci
    e_mats = [_ssd_scan_prep(sm_ref.at[bb], dtb_ref, alog_ref, dsk_ref, acg_ref.at[bb], actg_ref.at[bb], tril)
              for bb in range(batch)]

    lane = lax.broadcasted_iota(I32, (L, LANES), 1)
    lo_mask = lane < SSM_HEAD_DIM

    def group_body(gi, carry):
        for bb in range(batch):
            group_of(bb, gi)
        return carry

    def group_of(bb, gi):
        e_mat = e_mats[bb]
        xs = xs_ref[bb, slot, gi]
        bg = bm_ref[bb, slot, gi]
        cg_b = cm_ref[bb, slot, gi].astype(BF16)
        hsel = (lax.broadcasted_iota(I32, (SSM_HEADS, GROUP_W), 0)
                == gi * HEADS_PER_GROUP + lax.broadcasted_iota(I32, (SSM_HEADS, GROUP_W), 1) // SSM_HEAD_DIM)
        ex = _dot(e_mat, jnp.where(hsel, 1.0, 0.0).astype(BF16))
        dt_rep = ex[0:L]
        expa_rep = ex[L:2 * L]
        dte_rep = ex[2 * L:3 * L]
        cd_rep = ex[3 * L:3 * L + 1] + ex[3 * L + 1:3 * L + 2] + ex[3 * L + 2:3 * L + 3]
        dsk_rep = ex[3 * L + 3:3 * L + 4] + ex[3 * L + 4:3 * L + 5] + ex[3 * L + 5:3 * L + 6]

        xd = xs * dt_rep
        xd_b = xd.astype(BF16)
        cb = _dot_nt(cg_b, bg.astype(BF16))
        acg = acg_ref[bb, gi]
        actg = actg_ref[bb, gi]
        pairs = []
        for pj in range(HEADS_PER_GROUP // 2):
            gmat = []
            for e in (2 * pj, 2 * pj + 1):
                seg = acg[:, e:e + 1] - actg[e:e + 1, :]
                dec = jnp.exp(jnp.where(tril, seg, NEG_INF))
                gmat.append((cb * dec).astype(BF16))
            xp = xd_b[:, pj * LANES:(pj + 1) * LANES]
            zero = jnp.zeros_like(xp)
            pairs.append(_dot(jnp.concatenate(gmat, axis=1),
                              jnp.concatenate([jnp.where(lo_mask, xp, zero), jnp.where(lo_mask, zero, xp)], axis=0)))
        y_diag = jnp.concatenate(pairs, axis=1)

        st = state_ref[bb, gi]
        y_off = _dot(cg_b, st.astype(BF16)) * expa_rep
        xdd = (xd * dte_rep).astype(BF16)
        state_ref[bb, gi] = st * cd_rep + _dot(bg.T.astype(BF16), xdd)
        ypre_ref[bb, gi] = y_diag + y_off + dsk_rep * xs

    lax.fori_loop(0, SSM_GROUPS, group_body, 0, unroll=8)

    for gi in range(SSM_GROUPS):
        cols = slice(gi * GROUP_W, (gi + 1) * GROUP_W)
        for bb in range(batch):
            zz = z_ref[bb, :, cols].astype(F32)
            y = ypre_ref[bb, gi] * _silu(zz)
            ms = jnp.mean(y * y, axis=-1, keepdims=True)
            o_ref[bb, :, cols] = (y * lax.rsqrt(ms + EPS) * ng_ref[:, cols]).astype(o_ref.dtype)


def _ssd(main, small, conv_w, conv_b, dt_bias, a_log, d_skip, norm_g, batch, seq):
    nc = seq // CHUNK
    hb = CHUNK // CONV_HALO
    full = lambda shape: pl.BlockSpec(shape, lambda c: (0,) * len(shape))
    shift = jnp.asarray(_conv_shift_matrix(), BF16)
    main3 = main.reshape(batch, seq, M_END)
    small3 = small.reshape(batch, seq, S_END)
    y = pl.pallas_call(
        functools.partial(_ssd_kernel, batch=batch),
        grid=(nc,),
        in_specs=[pl.BlockSpec((batch, CHUNK, CONV_DIM), lambda c: (0, 0, M_XBC // CONV_DIM),
                               pipeline_mode=pl.Buffered(1)),
                  pl.BlockSpec((batch, CHUNK, CONV_DIM),
                               lambda c: (0, jnp.minimum(c + 1, nc - 1), M_XBC // CONV_DIM)),
                  pl.BlockSpec((batch, CONV_HALO, CONV_DIM),
                               lambda c: (0, c * hb + hb - 1, M_XBC // CONV_DIM)),
                  pl.BlockSpec((batch, CHUNK, SSM_INNER), lambda c: (0, c, M_Z // SSM_INNER)),
                  pl.BlockSpec((batch, CHUNK, S_END), lambda c: (0, c, 0)),
                  full((CHUNK, CONV_WIDTH * (CONV_HALO + CHUNK))),
                  full((CONV_WIDTH, CONV_DIM)), full((1, CONV_DIM)), full((1, SSM_HEADS)),
                  full((1, SSM_HEADS)), full((1, SSM_HEADS)), full((1, SSM_INNER))],
        out_specs=pl.BlockSpec((batch, CHUNK, SSM_INNER), lambda c: (0, c, 0)),
        out_shape=jax.ShapeDtypeStruct((batch, seq, SSM_INNER), BF16),
        scratch_shapes=[pltpu.VMEM((batch, 2, SSM_GROUPS, CHUNK, GROUP_W), F32),
                        pltpu.VMEM((batch, 2, SSM_GROUPS, CHUNK, SSM_STATE), F32),
                        pltpu.VMEM((batch, 2, SSM_GROUPS, CHUNK, SSM_STATE), F32),
                        pltpu.VMEM((batch, SSM_GROUPS, SSM_STATE, GROUP_W), F32),
                        pltpu.VMEM((batch, SSM_GROUPS, CHUNK, GROUP_W), F32),
                        pltpu.VMEM((batch, SSM_GROUPS, CHUNK, HEADS_PER_GROUP), F32),
                        pltpu.VMEM((batch, SSM_GROUPS, HEADS_PER_GROUP, CHUNK), F32)],
        compiler_params=_params(("arbitrary",)),
        name="ssd_scan",
    )(main3, main3, main3, main3, small3, shift, conv_w, conv_b, dt_bias, a_log, d_skip, norm_g)
    return y.reshape(batch * seq, SSM_INNER)


def _merge_kernel(att_ref, y_ref, ga_ref, gs_ref, wa_ref, ws_ref, o_ref):
    pa = _dot(att_ref[...], wa_ref[...])
    ps = _dot(y_ref[...], ws_ref[...])
    o_ref[...] = (_sigmoid(ga_ref[...].astype(F32)) * pa + _sigmoid(gs_ref[...].astype(F32)) * ps).astype(o_ref.dtype)


def _merge(att, y, main, wa, ws, tm, tn):
    t = att.shape[0]
    return pl.pallas_call(
        _merge_kernel,
        grid=(t // tm, D_MODEL // tn),
        in_specs=[pl.BlockSpec((tm, ATT_Q), lambda i, j: (i, 0)),
                  pl.BlockSpec((tm, SSM_INNER), lambda i, j: (i, 0)),
                  pl.BlockSpec((tm, tn), lambda i, j: (i, M_GA // tn + j)),
                  pl.BlockSpec((tm, tn), lambda i, j: (i, M_GS // tn + j)),
                  pl.BlockSpec((ATT_Q, tn), lambda i, j: (0, j)),
                  pl.BlockSpec((SSM_INNER, tn), lambda i, j: (0, j))],
        out_specs=pl.BlockSpec((tm, tn), lambda i, j: (i, j)),
        out_shape=jax.ShapeDtypeStruct((t, D_MODEL), BF16),
        compiler_params=_params(("parallel", "arbitrary")),
        name="gated_merge",
    )(att, y, main, main, wa, ws)


def _out_proj_kernel(m_ref, w_ref, x_ref, g_ref, x1_ref, h2_ref):
    x1 = x_ref[...] + _dot(m_ref[...], w_ref[...])
    x1_ref[...] = x1
    ms = jnp.mean(x1 * x1, axis=-1, keepdims=True)
    h2_ref[...] = (x1 * lax.rsqrt(ms + EPS) * g_ref[...]).astype(BF16)


def _out_proj(merged, w, x, g, tm):
    t = x.shape[0]
    return pl.pallas_call(
        _out_proj_kernel,
        grid=(t // tm,),
        in_specs=[pl.BlockSpec((tm, D_MODEL), lambda i: (i, 0)),
                  pl.BlockSpec((D_MODEL, D_MODEL), lambda i: (0, 0)),
                  pl.BlockSpec((tm, D_MODEL), lambda i: (i, 0)),
                  pl.BlockSpec((1, D_MODEL), lambda i: (0, 0))],
        out_specs=[pl.BlockSpec((tm, D_MODEL), lambda i: (i, 0)),
                   pl.BlockSpec((tm, D_MODEL), lambda i: (i, 0))],
        out_shape=[jax.ShapeDtypeStruct((t, D_MODEL), F32),
                   jax.ShapeDtypeStruct((t, D_MODEL), BF16)],
        compiler_params=_params(("parallel",)),
        name="out_proj_norm",
    )(merged, w, x, g)


def _mlp_kernel(h_ref, x1_ref, wu_ref, wd_ref, o_ref):
    @pl.when(pl.program_id(1) == 0)
    def _():
        o_ref[...] = x1_ref[...]

    u = _dot(h_ref[...], wu_ref[...])
    u = jnp.square(jnp.maximum(u, 0.0)).astype(BF16)
    o_ref[...] += _dot(u, wd_ref[...])


def _mlp(h2, x1, wu, wd, tm, th):
    t = h2.shape[0]
    return pl.pallas_call(
        _mlp_kernel,
        grid=(t // tm, MLP_HIDDEN // th),
        in_specs=[pl.BlockSpec((tm, D_MODEL), lambda i, j: (i, 0)),
                  pl.BlockSpec((tm, D_MODEL), lambda i, j: (i, 0)),
                  pl.BlockSpec((D_MODEL, th), lambda i, j: (0, j)),
                  pl.BlockSpec((th, D_MODEL), lambda i, j: (j, 0))],
        out_specs=pl.BlockSpec((tm, D_MODEL), lambda i, j: (i, 0)),
        out_shape=jax.ShapeDtypeStruct((t, D_MODEL), F32),
        compiler_params=_params(("parallel", "arbitrary")),
        name="relu2_mlp",
    )(h2, x1, wu, wd)


def _pack_w_small(w_t):
    assert (O_GA, O_KI, O_Z, O_DT) == (M_GA, M_Z, O_WI + IDX_HEADS, O_Z + M_END - M_Z)
    return jnp.concatenate([w_t[O_KI:O_Z], w_t[O_DT:O_DT + SSM_HEADS],
                            jnp.zeros((S_END - S_DT - SSM_HEADS, w_t.shape[1]), w_t.dtype)], axis=0)


def _block(x2, batch, seq, norm1_g, w_in, conv_w, conv_b, dt_bias, a_log, d_skip, ssm_norm_g, q_norm_g,
           k_norm_g, rel_bias, w_att_branch, w_ssm_branch, w_out, norm2_g, w_up, w_down):
    row = lambda v: v.reshape(1, -1)
    w_t = w_in.T
    xn, small = _norm_small(x2, row(norm1_g), _pack_w_small(w_t), tm=min(1024, x2.shape[0]))
    main, (w_att_b, w_ssm_b, w_out_b, w_up_b, w_down_b) = _in_proj(
        xn, w_t, (w_att_branch, w_ssm_branch, w_out, w_up, w_down), tm=min(1024, x2.shape[0]))
    qt, kn, vt, qit, wt, ki = _qk_prep(main, small, row(q_norm_g), row(k_norm_g), batch, seq, tq=DSA_TQ)
    att = _dsa(qt, kn, vt, qit, wt, ki, rel_bias, batch, seq)
    y = _ssd(main, small, conv_w, row(conv_b), row(dt_bias), row(a_log), row(d_skip), row(ssm_norm_g),
             batch, seq)
    merged = _merge(att, y, main, w_att_b, w_ssm_b, tm=512, tn=1024)
    x1, h2 = _out_proj(merged, w_out_b, x2, row(norm2_g), tm=512)
    return _mlp(h2, x1, w_up_b, w_down_b, tm=512, th=1024)


def kernel(x, norm1_g, w_in, conv_w, conv_b, dt_bias, a_log, d_skip, ssm_norm_g, q_norm_g, k_norm_g, rel_bias,
           w_att_branch, w_ssm_branch, w_out, norm2_g, w_up, w_down):
    batch, seq, d = x.shape
    x2 = x.reshape(batch * seq, d)
    for l in range(norm1_g.shape[0]):
        x2 = _block(x2, batch, seq, norm1_g[l], w_in[l], conv_w[l], conv_b[l], dt_bias[l], a_log[l], d_skip[l],
                    ssm_norm_g[l], q_norm_g[l], k_norm_g[l], rel_bias, w_att_branch[l], w_ssm_branch[l],
                    w_out[l], norm2_g[l], w_up[l], w_down[l])
    return x2.reshape(batch, seq, d)
```

```python
import functools
import math

import numpy as np
import jax
import jax.numpy as jnp
from jax import lax
from jax.experimental import pallas as pl
from jax.experimental.pallas import tpu as pltpu

F32 = jnp.float32
BF16 = jnp.bfloat16
I32 = jnp.int32

D_MODEL = 2048
ATT_HEADS = 16
ATT_KV_HEADS = 4
HEAD_DIM = 128
REP = ATT_HEADS // ATT_KV_HEADS
IDX_HEADS = 16
IDX_DIM = 64
TOPK_MAX = 256
N_BUCKETS = 32
MAX_DISTANCE = 128
SSM_INNER = 2 * D_MODEL
SSM_HEAD_DIM = 64
SSM_HEADS = SSM_INNER // SSM_HEAD_DIM
SSM_GROUPS = 8
SSM_STATE = 128
CONV_WIDTH = 4
CHUNK = 128
MLP_HIDDEN = 4 * D_MODEL
EPS = 1e-6

ATT_Q = ATT_HEADS * HEAD_DIM
ATT_KV = ATT_KV_HEADS * HEAD_DIM
IDX_Q = IDX_HEADS * IDX_DIM
SSM_BC = SSM_GROUPS * SSM_STATE
CONV_DIM = SSM_INNER + 2 * SSM_BC
SPLITS = (D_MODEL, D_MODEL, ATT_Q, ATT_KV, ATT_KV, IDX_Q, IDX_DIM, IDX_HEADS, SSM_INNER, CONV_DIM, SSM_HEADS)
_OFFS = tuple(int(v) for v in np.cumsum((0,) + SPLITS))
(O_GA, O_GS, O_Q, O_K, O_V, O_QI, O_KI, O_WI, O_Z, O_XBC, O_DT, _O_END) = _OFFS

M_GA = 0
M_GS = M_GA + D_MODEL
M_Q = M_GS + D_MODEL
M_K = M_Q + ATT_Q
M_V = M_K + ATT_KV
M_QI = M_V + ATT_KV
M_Z = M_QI + IDX_Q
M_XBC = M_Z + SSM_INNER
M_END = M_XBC + CONV_DIM
S_KI = 0
S_WI = S_KI + IDX_DIM
S_DT = S_WI + IDX_HEADS
S_END = 256

HEADS_PER_GROUP = SSM_HEADS // SSM_GROUPS
GROUP_W = HEADS_PER_GROUP * SSM_HEAD_DIM

LANES = 128
SUBLANES = 8
BF16_ROWS = 2 * SUBLANES
VMEM_LIMIT = 56 * 1024 * 1024

NEG_INF = float("-inf")
INT_MIN = -(2 ** 31)
KEY_NEG_INF = int(np.int32(np.uint32(0xFF800000) ^ np.uint32(0x7FFFFFFF)))
BF16_MIN_NORMAL_BITS = 0x0080


def _dot(a, b):
    return jnp.dot(a, b, preferred_element_type=F32)


def _dot_nt(a, b):
    return lax.dot_general(a, b, (((1,), (1,)), ((), ())), preferred_element_type=F32)


def _split3(x):
    hi = x.astype(BF16)
    r = x - hi.astype(F32)
    mid = r.astype(BF16)
    lo = (r - mid.astype(F32)).astype(BF16)
    return hi, mid, lo


def _silu(x):
    h = 0.5 * x
    return h + h * jnp.tanh(h)


def _sigmoid(x):
    return 0.5 + 0.5 * jnp.tanh(0.5 * x)


def _params(sem):
    return pltpu.CompilerParams(dimension_semantics=sem, vmem_limit_bytes=VMEM_LIMIT)


def _norm_small_kernel(x_ref, g_ref, w_ref, xn_ref, sm_ref):
    x = x_ref[...]
    ms = jnp.mean(x * x, axis=-1, keepdims=True)
    xn = (x * lax.rsqrt(ms + EPS) * g_ref[...]).astype(BF16)
    xn_ref[...] = xn
    sm_ref[...] = _dot_nt(xn, w_ref[...].astype(BF16))


def _norm_small(x, g, w_small_t, tm):
    m, d = x.shape
    n = w_small_t.shape[0]
    return pl.pallas_call(
        _norm_small_kernel,
        grid=(m // tm,),
        in_specs=[pl.BlockSpec((tm, d), lambda i: (i, 0)),
                  pl.BlockSpec((1, d), lambda i: (0, 0)),
                  pl.BlockSpec((n, d), lambda i: (0, 0))],
        out_specs=[pl.BlockSpec((tm, d), lambda i: (i, 0)),
                   pl.BlockSpec((tm, n), lambda i: (i, 0))],
        out_shape=[jax.ShapeDtypeStruct((m, d), BF16), jax.ShapeDtypeStruct((m, n), F32)],
        compiler_params=_params(("parallel",)),
        name="norm_small_proj",
    )(x, g, w_small_t)


IN_TN = 1024
IN_ALIGNED_TILES = O_KI // IN_TN
IN_SHIFT = O_Z - M_Z
IN_NEXT = 128
IN_ROWS = 256


def _in_proj_kernel(*refs, n_cast, n_side):
    xn_ref, wa_ref, wn_ref = refs[:3]
    side_in = refs[3:3 + n_side]
    o_ref = refs[3 + n_side]
    side_out = refs[4 + n_side:4 + 2 * n_side]
    w_scr = refs[4 + 2 * n_side]
    j = pl.program_id(0)
    i = pl.program_id(1)
    _side_cast(j * pl.num_programs(1) + i, n_cast, side_in, side_out)

    @pl.when((i == 0) & (j < IN_ALIGNED_TILES))
    def _():
        for r0 in range(0, IN_TN, IN_ROWS):
            w_scr[r0:r0 + IN_ROWS, :] = wa_ref[r0:r0 + IN_ROWS, :].astype(BF16)

    @pl.when((i == 0) & (j >= IN_ALIGNED_TILES))
    def _():
        for r0 in range(0, IN_TN - IN_SHIFT, IN_ROWS):
            r1 = min(r0 + IN_ROWS, IN_TN - IN_SHIFT)
            w_scr[r0:r1, :] = wa_ref[r0 + IN_SHIFT:r1 + IN_SHIFT, :].astype(BF16)
        w_scr[IN_TN - IN_SHIFT:IN_TN, :] = wn_ref[0:IN_SHIFT, :].astype(BF16)

    o_ref[...] = _dot_nt(xn_ref[...], w_scr[...]).astype(o_ref.dtype)


def _in_proj(xn, w_t, side_weights, tm):
    m, d = xn.shape
    assert O_KI % IN_TN == 0 and M_END % IN_TN == 0 and IN_TN % IN_NEXT == 0
    assert 0 < IN_SHIFT <= IN_NEXT and IN_SHIFT % BF16_ROWS == 0
    nj, ni = M_END // IN_TN, m // tm
    n_cast, side_specs = _side_cast_specs(side_weights, nj * ni, lambda j, i: j * ni + i)
    outs = pl.pallas_call(
        functools.partial(_in_proj_kernel, n_cast=n_cast, n_side=len(side_weights)),
        grid=(nj, ni),
        in_specs=[pl.BlockSpec((tm, d), lambda j, i: (i, 0)),
                  pl.BlockSpec((IN_TN, d), lambda j, i: (j, 0)),
                  pl.BlockSpec((IN_NEXT, d), lambda j, i: ((j + 1) * (IN_TN // IN_NEXT), 0))] + side_specs,
        out_specs=[pl.BlockSpec((tm, IN_TN), lambda j, i: (i, j))] + side_specs,
        out_shape=[jax.ShapeDtypeStruct((m, M_END), BF16)]
                  + [jax.ShapeDtypeStruct(w.shape, BF16) for w in side_weights],
        scratch_shapes=[pltpu.VMEM((IN_TN, d), BF16)],
        compiler_params=_params(("arbitrary", "arbitrary")),
        name="in_proj",
    )(xn, w_t, w_t, *side_weights)
    return outs[0], outs[1:]


def _side_cast_specs(weights, n_steps, step_of):
    n_cast = 1 << (n_steps.bit_length() - 1)
    specs = []
    for w in weights:
        rows = w.shape[0] // n_cast
        assert w.shape[0] % n_cast == 0 and rows % BF16_ROWS == 0
        specs.append(pl.BlockSpec((rows, w.shape[1]), lambda *ids: (jnp.minimum(step_of(*ids), n_cast - 1), 0)))
    return n_cast, specs


def _side_cast(step, n_cast, srcs, dsts):
    @pl.when(step < n_cast)
    def _():
        for src, dst in zip(srcs, dsts):
            dst[...] = src[...].astype(BF16)


LOG2E = math.log2(math.e)
VT_ROWS = HEAD_DIM + BF16_ROWS


def _qk_prep_kernel(q_ref, k_ref, v_ref, qi_ref, sm_ref, qg_ref, kg_ref,
                    qt_ref, kn_ref, vt_ref, qit_ref, wt_ref, ki_ref):
    qg = qg_ref[...]
    for h in range(ATT_HEADS):
        x = q_ref[:, h * HEAD_DIM:(h + 1) * HEAD_DIM].astype(F32)
        ms = jnp.mean(x * x, axis=-1, keepdims=True)
        y = x * lax.rsqrt(ms + EPS) * qg * (HEAD_DIM ** -0.5 * LOG2E)
        qt_ref[0, h] = y.T.astype(BF16)
    kg = kg_ref[...]
    for h in range(ATT_KV_HEADS):
        x = k_ref[:, h * HEAD_DIM:(h + 1) * HEAD_DIM].astype(F32)
        ms = jnp.mean(x * x, axis=-1, keepdims=True)
        kn_ref[:, h * HEAD_DIM:(h + 1) * HEAD_DIM] = (x * lax.rsqrt(ms + EPS) * kg).astype(BF16)
        vt_ref[0, h, 0, 0:HEAD_DIM, :] = v_ref[:, h * HEAD_DIM:(h + 1) * HEAD_DIM].astype(F32).T.astype(BF16)
        vt_ref[0, h, 0, HEAD_DIM:VT_ROWS, :] = jnp.ones((VT_ROWS - HEAD_DIM, v_ref.shape[0]), BF16)
    for p in range(IDX_Q // LANES):
        qit_ref[0, p * LANES:(p + 1) * LANES, :] = qi_ref[:, p * LANES:(p + 1) * LANES].astype(F32).T.astype(BF16)
    sm_t = sm_ref[:, 0:LANES].T
    wt_ref[0] = sm_t[S_WI:S_WI + IDX_HEADS, :] * (IDX_HEADS ** -0.5 * IDX_DIM ** -0.5)
    ki_ref[...] = sm_ref[:, S_KI:S_KI + IDX_DIM].astype(BF16)


def _qk_prep(main, small, qg, kg, batch, seq, tq):
    t = main.shape[0]
    nq = seq // tq
    return pl.pallas_call(
        _qk_prep_kernel,
        grid=(batch, nq),
        in_specs=[pl.BlockSpec((tq, ATT_Q), lambda b, i: (b * nq + i, M_Q // ATT_Q)),
                  pl.BlockSpec((tq, ATT_KV), lambda b, i: (b * nq + i, M_K // ATT_KV)),
                  pl.BlockSpec((tq, ATT_KV), lambda b, i: (b * nq + i, M_V // ATT_KV)),
                  pl.BlockSpec((tq, IDX_Q), lambda b, i: (b * nq + i, M_QI // IDX_Q)),
                  pl.BlockSpec((tq, S_END), lambda b, i: (b * nq + i, 0)),
                  pl.BlockSpec((1, HEAD_DIM), lambda b, i: (0, 0)),
                  pl.BlockSpec((1, HEAD_DIM), lambda b, i: (0, 0))],
        out_specs=[pl.BlockSpec((1, ATT_HEADS, HEAD_DIM, tq), lambda b, i: (b * nq + i, 0, 0, 0)),
                   pl.BlockSpec((tq, ATT_KV), lambda b, i: (b * nq + i, 0)),
                   pl.BlockSpec((1, ATT_KV_HEADS, 1, VT_ROWS, tq), lambda b, i: (b, 0, i, 0, 0)),
                   pl.BlockSpec((1, IDX_Q, tq), lambda b, i: (b * nq + i, 0, 0)),
                   pl.BlockSpec((1, IDX_HEADS, tq), lambda b, i: (b * nq + i, 0, 0)),
                   pl.BlockSpec((tq, IDX_DIM), lambda b, i: (b * nq + i, 0))],
        out_shape=[jax.ShapeDtypeStruct((t // tq, ATT_HEADS, HEAD_DIM, tq), BF16),
                   jax.ShapeDtypeStruct((t, ATT_KV), BF16),
                   jax.ShapeDtypeStruct((batch, ATT_KV_HEADS, nq, VT_ROWS, tq), BF16),
                   jax.ShapeDtypeStruct((t // tq, IDX_Q, tq), BF16),
                   jax.ShapeDtypeStruct((t // tq, IDX_HEADS, tq), F32),
                   jax.ShapeDtypeStruct((t, IDX_DIM), BF16)],
        compiler_params=_params(("parallel", "parallel")),
        name="qk_prep",
    )(main, main, main, main, small, qg, kg)


DSA_TQ = 256
DSA_TK = 256
CNT_WAYS = 4


def _t5_bucket_np(dist):
    n = np.maximum(dist, 0)
    max_exact = N_BUCKETS // 2
    nf = np.maximum(n, 1).astype(np.float32)
    ratio = (np.log(nf / np.float32(max_exact)) / np.float32(math.log(MAX_DISTANCE / max_exact))
             * np.float32(N_BUCKETS - max_exact))
    large = max_exact + ratio.astype(np.int32)
    large = np.minimum(large, N_BUCKETS - 1)
    return np.where(n < max_exact, n, large).astype(np.int32)


def _bias_bucket_tiles(tq, tk):
    r = np.arange(tq)[None, :]
    c = np.arange(tk)[:, None]
    d0 = _t5_bucket_np(r - c)
    d1 = _t5_bucket_np(tk + r - c)
    assert np.all(_t5_bucket_np(np.arange(tk + 1, 8 * tk)) == N_BUCKETS - 1)
    return np.stack([d0, d1]).astype(np.int32)


def _dsa_kernel(relb_ref, bidx_ref, qt_ref, qit_ref, wt_ref, k_ref, vt_ref, ki_ref, o_ref,
                key_ref, plane_ref, madd_ref, bias_ref, s_buf, p_buf, st_ref, acc_ref, *, tq, tk, nkc, topk):
    b = pl.program_id(0)
    i = pl.program_id(1)
    neg_slot = nkc

    @pl.when((b == 0) & (i == 0))
    def _init():
        madd_ref[neg_slot] = jnp.full((tk, tq), NEG_INF, F32)
        for t in range(2):
            bt = bidx_ref[t]

            def head_body(h, carry):
                far = relb_ref[N_BUCKETS - 1, h]

                def bucket_body(bk, acc):
                    return jnp.where(bt == bk, (relb_ref[bk, h] - far) * LOG2E, acc)

                bias_ref[t, h] = lax.fori_loop(0, N_BUCKETS, bucket_body, jnp.zeros((tk, tq), F32))
                return carry

            lax.fori_loop(0, ATT_HEADS, head_body, 0)

    def _select():
        nj = i + 1
        qpos = i * tq + lax.broadcasted_iota(I32, (tk, tq), 1)

        def score_chunk(j):
            kc = ki_ref[pl.ds(pl.multiple_of(j * tk, tk), tk), :]
            acc = jnp.zeros((tk, tq), F32)
            for h in range(IDX_HEADS):
                z = _dot(kc, qit_ref[0, h * IDX_DIM:(h + 1) * IDX_DIM, :])
                acc = acc + jnp.maximum(z, 0.0) * wt_ref[0, h:h + 1, :]
            kpos = j * tk + lax.broadcasted_iota(I32, (tk, tq), 0)
            acc = jnp.where(kpos <= qpos, acc, NEG_INF)
            bits = pltpu.bitcast(acc, I32)
            bits = jnp.where(bits == INT_MIN, 0, bits)
            key = bits ^ ((bits >> 31) & 0x7FFFFFFF)
            key_ref[j] = key
            plane_ref[0, j] = pltpu.bitcast(bits & jnp.int32(-65536), F32).astype(BF16)
            plane_ref[1, j] = ((key >> 8) & 0xFF).astype(F32).astype(BF16)
            plane_ref[2, j] = (key & 0xFF).astype(F32).astype(BF16)

        def score_quad(u, carry):
            for v in range(4):
                score_chunk(4 * u + v)
            return carry

        lax.fori_loop(0, nj // 4, score_quad, 0)
        done = (nj // 4) * 4

        @pl.when(nj % 4 >= 2)
        def _():
            score_chunk(done)
            score_chunk(done + 1)

        @pl.when(nj % 2 == 1)
        def _():
            score_chunk(nj - 1)

        kf = float(topk)
        one_b = jnp.ones((tk, tq), BF16)
        zero_b = jnp.zeros((tk, tq), BF16)
        grp = BF16_ROWS * CNT_WAYS

        def count(plane, cand_b, strict):
            def body(j, cnt):
                pv = plane_ref[plane, j]
                hit = jnp.where(pv > cand_b if strict else pv >= cand_b, one_b, zero_b)
                for r0 in range(0, tk, grp):
                    cnt = cnt + hit[r0:r0 + grp]
                return cnt

            cnt = lax.fori_loop(0, nj, body, jnp.zeros((grp, tq), BF16))
            return jnp.sum(cnt.astype(F32), axis=0, keepdims=True)

        def keep_ties(src, dst, tie_b):
            def body(j, carry):
                plane_ref[dst, j] = jnp.where(plane_ref[src, j] == tie_b, plane_ref[dst, j], -one_b)
                return carry

            lax.fori_loop(0, nj, body, 0)

        def top_digit_float(p16):
            p16 = jnp.where((p16 > 0) & (p16 < BF16_MIN_NORMAL_BITS), BF16_MIN_NORMAL_BITS, p16)
            fb = (p16 & 0xFFFF) ^ jnp.where(p16 < 0, 0x7FFF, 0)
            return pltpu.bitcast(lax.shift_left(fb, 16), F32).astype(BF16)

        def top_iter(it, prefix):
            cand = prefix + lax.shift_left(jnp.int32(1), 15 - it)
            return jnp.where(count(0, top_digit_float(cand), False) >= kf, cand, prefix)

        d_top = lax.fori_loop(0, 16, top_iter, jnp.full((1, tq), -32768, I32))
        t_top = top_digit_float(d_top)
        above = count(0, t_top, True)

        def byte_digit(plane, above_n):
            def it_body(it, prefix):
                cand = prefix + lax.shift_left(jnp.int32(1), 7 - it)
                tot = above_n + count(plane, cand.astype(F32).astype(BF16), False)
                return jnp.where(tot >= kf, cand, prefix)

            return lax.fori_loop(0, 8, it_body, jnp.zeros((1, tq), I32))

        keep_ties(0, 1, t_top)
        d_mid = byte_digit(1, above)
        t_mid = d_mid.astype(F32).astype(BF16)
        above = above + count(1, t_mid, True)
        keep_ties(1, 2, t_mid)
        d_low = byte_digit(2, above)
        t_low = d_low.astype(F32).astype(BF16)
        thr = lax.shift_left(d_top, 16) | lax.shift_left(d_mid, 8) | d_low
        n_ge = above + count(2, t_low, False)
        tied = jnp.max(n_ge) > kf

        @pl.when(jnp.logical_not(tied))
        def _():
            thr_c = jnp.maximum(thr, KEY_NEG_INF + 1)

            def madd_chunk(j, carry):
                madd_ref[j] = jnp.where(key_ref[j] >= thr_c, 0.0, NEG_INF)
                return carry

            lax.fori_loop(0, nj, madd_chunk, 0)

        @pl.when(tied)
        def _():
            need = kf - (above + count(2, t_low, True))
            row = lax.broadcasted_iota(I32, (tk, tq), 0)

            def pos_iter(it, q_pos):
                cand = q_pos + lax.shift_left(jnp.int32(1), (nkc * tk).bit_length() - 2 - it)

                def body(j, cnt):
                    hit = jnp.where(key_ref[j] == thr, jnp.where(row + j * tk < cand, 1.0, 0.0), 0.0)
                    return cnt + jnp.sum(hit.reshape(tk // SUBLANES, SUBLANES, tq), axis=0)

                cnt = lax.fori_loop(0, nj, body, jnp.zeros((SUBLANES, tq), F32))
                return jnp.where(jnp.sum(cnt, axis=0, keepdims=True) < need, cand, q_pos)

            last = lax.fori_loop(0, (nkc * tk).bit_length() - 1, pos_iter, jnp.zeros((1, tq), I32))
            thr_c = jnp.maximum(thr, KEY_NEG_INF)

            def madd_chunk(j, carry):
                kk = key_ref[j]
                take_tie = jnp.where(row + j * tk <= last, jnp.where(kk > KEY_NEG_INF, 0.0, NEG_INF), NEG_INF)
                madd_ref[j] = jnp.where(kk > thr_c, 0.0, jnp.where(kk == thr, take_tie, NEG_INF))
                return carry

            lax.fori_loop(0, nj, madd_chunk, 0)

    _select()

    ng = ATT_KV_HEADS
    c0 = jnp.maximum(i - 1, 0)
    c1 = jnp.minimum(c0 + 1, nkc - 1)
    n_far = (c0 + 1) // 2

    def scores_to(slot, g, ca, madd_c, bias_c):
        rows = pl.ds(pl.multiple_of(ca * tk, tk), 2 * tk)
        qt = jnp.concatenate([qt_ref[0, g * REP + r] for r in range(REP)], axis=1)
        add = jnp.concatenate([madd_c if bias_c is None else madd_c + bias_c[r] for r in range(REP)], axis=1)
        s = _dot(k_ref[rows, g * HEAD_DIM:(g + 1) * HEAD_DIM], qt) + add
        s_buf[slot] = s
        st_ref[ng + 2 + slot:ng + 3 + slot, :] = jnp.max(s, axis=0, keepdims=True)

    def softmax_to(slot, g):
        m = st_ref[g:g + 1, :]
        m_new = jnp.maximum(m, st_ref[ng + 2 + slot:ng + 3 + slot, :])
        m_safe = jnp.where(m_new == NEG_INF, 0.0, m_new)
        alpha = jnp.exp2(m - m_safe)
        p_buf[slot] = jnp.exp2((s_buf[slot] - m_safe).astype(BF16))
        st_ref[g:g + 1, :] = m_new
        st_ref[ng + slot:ng + 1 + slot, :] = alpha

    def values_from(slot, g, ca, cb):
        vt_c = jnp.concatenate([vt_ref[0, g, ca], vt_ref[0, g, cb]], axis=1)
        acc_ref[g] = st_ref[ng + slot:ng + 1 + slot, :] * acc_ref[g] + _dot(vt_c, p_buf[slot])

    def near_scores_to(slot, g):
        first = i == 0
        t_a = jnp.where(first, 0, 1)
        idx_b = jnp.where(first, neg_slot, c0 + 1)
        madd_near = jnp.concatenate([madd_ref[c0], madd_ref[idx_b]], axis=0)
        bias_near = [jnp.concatenate([bias_ref[t_a, g * REP + r], bias_ref[0, g * REP + r]], axis=0)
                     for r in range(REP)]
        scores_to(slot, g, c0, madd_near, bias_near)

    def far_scores_to(slot, g, k):
        ca = jnp.minimum(2 * (k - 1), nkc - 2)
        cb = jnp.where(ca + 1 >= c0, neg_slot, ca + 1)
        scores_to(slot, g, ca, jnp.concatenate([madd_ref[ca], madd_ref[cb]], axis=0), None)

    st_ref[0:ng, :] = jnp.full((ng, REP * tq), NEG_INF, F32)
    acc_ref[...] = jnp.zeros_like(acc_ref)

    near_scores_to(0, 0)
    for g in range(ng):
        if g + 1 < ng:
            near_scores_to((g + 1) % 2, g + 1)
        else:
            far_scores_to(0, 0, 1)
        softmax_to(g % 2, g)
        values_from(g % 2, g, c0, c1)

    def far_step(k, carry):
        for g in range(ng):
            if g + 1 < ng:
                far_scores_to((g + 1) % 2, g + 1, k)
            else:
                far_scores_to(0, 0, k + 1)
            softmax_to(g % 2, g)
            values_from(g % 2, g, 2 * (k - 1), 2 * (k - 1) + 1)
        return carry

    lax.fori_loop(1, n_far + 1, far_step, 0)

    for g in range(ng):
        out = acc_ref[g, 0:HEAD_DIM, :] / acc_ref[g, HEAD_DIM:HEAD_DIM + 1, :]
        for r in range(REP):
            h = g * REP + r
            o_ref[:, h * HEAD_DIM:(h + 1) * HEAD_DIM] = out[:, r * tq:(r + 1) * tq].T.astype(o_ref.dtype)


def _dsa(qt, kn, vt, qit, wt, ki, rel_bias, batch, seq):
    tq, tk = DSA_TQ, DSA_TK
    nq = seq // tq
    nkc = seq // tk
    topk = min(TOPK_MAX, seq // 4)
    bidx = jnp.asarray(_bias_bucket_tiles(tq, tk))
    kern = functools.partial(_dsa_kernel, tq=tq, tk=tk, nkc=nkc, topk=topk)
    once = pl.Buffered(1)
    return pl.pallas_call(
        kern,
        grid=(batch, nq),
        in_specs=[pl.BlockSpec(memory_space=pltpu.SMEM),
                  pl.BlockSpec((2, tk, tq), lambda b, i: (0, 0, 0), pipeline_mode=once),
                  pl.BlockSpec((1, ATT_HEADS, HEAD_DIM, tq), lambda b, i: (b * nq + i, 0, 0, 0)),
                  pl.BlockSpec((1, IDX_Q, tq), lambda b, i: (b * nq + i, 0, 0)),
                  pl.BlockSpec((1, IDX_HEADS, tq), lambda b, i: (b * nq + i, 0, 0)),
                  pl.BlockSpec((seq, ATT_KV), lambda b, i: (b, 0), pipeline_mode=once),
                  pl.BlockSpec((1, ATT_KV_HEADS, nkc, VT_ROWS, tk), lambda b, i: (b, 0, 0, 0, 0),
                               pipeline_mode=once),
                  pl.BlockSpec((seq, IDX_DIM), lambda b, i: (b, 0), pipeline_mode=once)],
        out_specs=pl.BlockSpec((tq, ATT_Q), lambda b, i: (b * nq + i, 0)),
        out_shape=jax.ShapeDtypeStruct((batch * seq, ATT_Q), BF16),
        scratch_shapes=[pltpu.VMEM((nkc, tk, tq), I32),
                        pltpu.VMEM((3, nkc, tk, tq), BF16),
                        pltpu.VMEM((nkc + 1, tk, tq), F32),
                        pltpu.VMEM((2, ATT_HEADS, tk, tq), F32),
                        pltpu.VMEM((2, 2 * tk, REP * tq), F32),
                        pltpu.VMEM((2, 2 * tk, REP * tq), BF16),
                        pltpu.VMEM((ATT_KV_HEADS + 4, REP * tq), F32),
                        pltpu.VMEM((ATT_KV_HEADS, VT_ROWS, REP * tq), F32)],
        compiler_params=_params(("arbitrary", "arbitrary")),
        name="dsa_attention",
    )(rel_bias, bidx, qt, qit, wt, kn, vt, ki)


E_ROWS = 3 * CHUNK + BF16_ROWS
CONV_HALO = BF16_ROWS


def _conv_shift_matrix():
    ext = CONV_HALO + CHUNK
    s = np.zeros((CHUNK, (CONV_WIDTH - 1) * ext), np.float32)
    for k in range(CONV_WIDTH - 1):
        t = np.arange(CHUNK)
        s[t, k * ext + CONV_HALO + t - (CONV_WIDTH - 1) + k] = 1.0
    return s


def _ssd_kernel(xbc0_ref, xbcn_ref, halon_ref, z_ref, sm_ref, shift_ref, cw_ref, cb_ref, dtb_ref, alog_ref,
                dsk_ref, ng_ref, o_ref, xs_ref, bm_ref, cm_ref, state_ref, ypre_ref, acg_ref, actg_ref, *, batch):
    c = pl.program_id(0)

    cblk = GROUP_W

    def conv_block(slot, bb, cbi, x_ref, halo_ref):
        cols = slice(cbi * cblk, (cbi + 1) * cblk)
        halo = jnp.zeros((CONV_HALO, cblk), BF16) if halo_ref is None else halo_ref[bb, :, cols]
        ext = jnp.concatenate([halo, x_ref[bb, :, cols]], axis=0)
        wtap = cw_ref[:, cols].astype(BF16)
        prods = jnp.concatenate([ext * wtap[kk:kk + 1, :] for kk in range(CONV_WIDTH - 1)], axis=0)
        last = (ext[CONV_HALO:, :] * wtap[CONV_WIDTH - 1:CONV_WIDTH, :]).astype(F32)
        y = _silu(cb_ref[:, cols] + last + _dot(shift_ref[...], prods))
        if cbi < SSM_GROUPS:
            xs_ref[bb, slot, cbi] = y
        else:
            per = cblk // SSM_STATE
            for u in range(per):
                gi = (cbi - SSM_GROUPS) * per + u
                piece = y[:, u * SSM_STATE:(u + 1) * SSM_STATE]
                if gi < SSM_GROUPS:
                    bm_ref[bb, slot, gi] = piece
                else:
                    cm_ref[bb, slot, gi - SSM_GROUPS] = piece

    def conv_jobs(slot, x_ref, halo_ref):
        return [functools.partial(conv_block, slot, bb, cbi, x_ref, halo_ref)
                for cbi in range(CONV_DIM // cblk) for bb in range(batch)]

    def scan_chunk(slot, jobs):
        _ssd_scan_chunk(slot, batch, jobs, z_ref, sm_ref, dtb_ref, alog_ref, dsk_ref, ng_ref, o_ref,
                        xs_ref, bm_ref, cm_ref, state_ref, ypre_ref, acg_ref, actg_ref)

    @pl.when(c == 0)
    def _():
        state_ref[...] = jnp.zeros_like(state_ref)
        for job in conv_jobs(0, xbc0_ref, None):
            job()

    @pl.when(c % 2 == 0)
    def _():
        scan_chunk(0, conv_jobs(1, xbcn_ref, halon_ref))

    @pl.when(c % 2 == 1)
    def _():
        scan_chunk(1, conv_jobs(0, xbcn_ref, halon_ref))


def _ssd_scan_prep(sm_ref, dtb_ref, alog_ref, dsk_ref, acg_ref, actg_ref, tril):
    L = CHUNK
    dt_in = sm_ref[:, S_DT:S_DT + SSM_HEADS] + dtb_ref[...]
    dt_act = jnp.maximum(dt_in, 0.0) + jnp.log1p(jnp.exp(-jnp.abs(dt_in)))
    a = dt_act * (-jnp.exp(alog_ref[...]))
    tri_b = jnp.where(tril, 1.0, 0.0).astype(BF16)
    a_cum = sum(_dot(tri_b, p) for p in _split3(a))
    eye_b = jnp.where(lax.broadcasted_iota(I32, (SSM_HEADS, SSM_HEADS), 0)
                      == lax.broadcasted_iota(I32, (SSM_HEADS, SSM_HEADS), 1), 1.0, 0.0).astype(BF16)
    a_cum_t = sum(_dot_nt(eye_b, p) for p in _split3(a_cum))
    a_last = a_cum[L - 1:L, :]
    for gi in range(SSM_GROUPS):
        acg_ref[gi] = a_cum[:, gi * HEADS_PER_GROUP:(gi + 1) * HEADS_PER_GROUP]
        actg_ref[gi] = a_cum_t[gi * HEADS_PER_GROUP:(gi + 1) * HEADS_PER_GROUP, :]
    cd3 = _split3(jnp.exp(a_last))
    ds3 = _split3(dsk_ref[...])
    extras = jnp.concatenate([p.astype(F32) for p in cd3 + ds3]
                             + [jnp.zeros((E_ROWS - 3 * L - 6, SSM_HEADS), F32)], axis=0)
    return jnp.concatenate([dt_act, jnp.exp(a_cum), jnp.exp(a_last - a_cum), extras], axis=0).astype(BF16)


def _ssd_scan_chunk(slot, batch, jobs, z_ref, sm_ref, dtb_ref, alog_ref, dsk_ref, ng_ref, o_ref,
                    xs_ref, bm_ref, cm_ref, state_ref, ypre_ref, acg_ref, actg_ref):
    L = CHUNK
    ri = lax.broadcasted_iota(I32, (L, L), 0)
    ci = lax.broadcasted_iota(I32, (L, L), 1)
    tril = ri >= ci
    e_mats = [_ssd_scan_prep(sm_ref.at[bb], dtb_ref, alog_ref, dsk_ref, acg_ref.at[bb], actg_ref.at[bb], tril)
              for bb in range(batch)]

    lane = lax.broadcasted_iota(I32, (L, LANES), 1)
    lo_mask = lane < SSM_HEAD_DIM

    def front(bb, gi):
        e_mat = e_mats[bb]
        xs = xs_ref[bb, slot, gi]
        bg = bm_ref[bb, slot, gi]
        cg_b = cm_ref[bb, slot, gi].astype(BF16)
        hsel = (lax.broadcasted_iota(I32, (SSM_HEADS, GROUP_W), 0)
                == gi * HEADS_PER_GROUP + lax.broadcasted_iota(I32, (SSM_HEADS, GROUP_W), 1) // SSM_HEAD_DIM)
        ex = _dot(e_mat, jnp.where(hsel, 1.0, 0.0).astype(BF16))
        dt_rep = ex[0:L]
        expa_rep = ex[L:2 * L]
        dte_rep = ex[2 * L:3 * L]
        cd_rep = ex[3 * L:3 * L + 1] + ex[3 * L + 1:3 * L + 2] + ex[3 * L + 2:3 * L + 3]
        dsk_rep = ex[3 * L + 3:3 * L + 4] + ex[3 * L + 4:3 * L + 5] + ex[3 * L + 5:3 * L + 6]

        xd = xs * dt_rep
        xd_b = xd.astype(BF16)
        cb = _dot_nt(cg_b, bg.astype(BF16))
        st = state_ref[bb, gi]
        y_off = _dot(cg_b, st.astype(BF16)) * expa_rep
        xdd = (xd * dte_rep).astype(BF16)
        state_ref[bb, gi] = st * cd_rep + _dot(bg.T.astype(BF16), xdd)
        ypre_ref[bb, gi] = y_off + dsk_rep * xs
        return cb, xd_b

    def back(bb, gi, cb, xd_b):
        acg = acg_ref[bb, gi]
        actg = actg_ref[bb, gi]
        for pj in range(HEADS_PER_GROUP // 2):
            gmat = []
            for e in (2 * pj, 2 * pj + 1):
                seg = acg[:, e:e + 1] - actg[e:e + 1, :]
                dec = jnp.exp(jnp.where(tril, seg, NEG_INF))
                gmat.append((cb * dec).astype(BF16))
            xp = xd_b[:, pj * LANES:(pj + 1) * LANES]
            zero = jnp.zeros_like(xp)
            ypre_ref[bb, gi, :, pj * LANES:(pj + 1) * LANES] += _dot(
                jnp.concatenate(gmat, axis=1),
                jnp.concatenate([jnp.where(lo_mask, xp, zero), jnp.where(lo_mask, zero, xp)], axis=0))

    stages = [(bb, gi) for gi in range(SSM_GROUPS) for bb in range(batch)]
    ahead = front(*stages[0])
    for idx, stage in enumerate(stages):
        cur = ahead
        if idx + 1 < len(stages):
            ahead = front(*stages[idx + 1])
        back(*stage, *cur)
        for job in jobs[idx * len(jobs) // len(stages):(idx + 1) * len(jobs) // len(stages)]:
            job()

    for gi in range(SSM_GROUPS):
        cols = slice(gi * GROUP_W, (gi + 1) * GROUP_W)
        for bb in range(batch):
            zz = z_ref[bb, :, cols].astype(F32)
            y = ypre_ref[bb, gi] * _silu(zz)
            ms = jnp.mean(y * y, axis=-1, keepdims=True)
            o_ref[bb, :, cols] = (y * lax.rsqrt(ms + EPS) * ng_ref[:, cols]).astype(o_ref.dtype)


def _ssd(main, small, conv_w, conv_b, dt_bias, a_log, d_skip, norm_g, batch, seq):
    nc = seq // CHUNK
    hb = CHUNK // CONV_HALO
    full = lambda shape: pl.BlockSpec(shape, lambda c: (0,) * len(shape))
    shift = jnp.asarray(_conv_shift_matrix(), BF16)
    main3 = main.reshape(batch, seq, M_END)
    small3 = small.reshape(batch, seq, S_END)
    y = pl.pallas_call(
        functools.partial(_ssd_kernel, batch=batch),
        grid=(nc,),
        in_specs=[pl.BlockSpec((batch, CHUNK, CONV_DIM), lambda c: (0, 0, M_XBC // CONV_DIM),
                               pipeline_mode=pl.Buffered(1)),
                  pl.BlockSpec((batch, CHUNK, CONV_DIM),
                               lambda c: (0, jnp.minimum(c + 1, nc - 1), M_XBC // CONV_DIM)),
                  pl.BlockSpec((batch, CONV_HALO, CONV_DIM),
                               lambda c: (0, c * hb + hb - 1, M_XBC // CONV_DIM)),
                  pl.BlockSpec((batch, CHUNK, SSM_INNER), lambda c: (0, c, M_Z // SSM_INNER)),
                  pl.BlockSpec((batch, CHUNK, S_END), lambda c: (0, c, 0)),
                  full((CHUNK, (CONV_WIDTH - 1) * (CONV_HALO + CHUNK))),
                  full((CONV_WIDTH, CONV_DIM)), full((1, CONV_DIM)), full((1, SSM_HEADS)),
                  full((1, SSM_HEADS)), full((1, SSM_HEADS)), full((1, SSM_INNER))],
        out_specs=pl.BlockSpec((batch, CHUNK, SSM_INNER), lambda c: (0, c, 0)),
        out_shape=jax.ShapeDtypeStruct((batch, seq, SSM_INNER), BF16),
        scratch_shapes=[pltpu.VMEM((batch, 2, SSM_GROUPS, CHUNK, GROUP_W), F32),
                        pltpu.VMEM((batch, 2, SSM_GROUPS, CHUNK, SSM_STATE), F32),
                        pltpu.VMEM((batch, 2, SSM_GROUPS, CHUNK, SSM_STATE), F32),
                        pltpu.VMEM((batch, SSM_GROUPS, SSM_STATE, GROUP_W), F32),
                        pltpu.VMEM((batch, SSM_GROUPS, CHUNK, GROUP_W), F32),
                        pltpu.VMEM((batch, SSM_GROUPS, CHUNK, HEADS_PER_GROUP), F32),
                        pltpu.VMEM((batch, SSM_GROUPS, HEADS_PER_GROUP, CHUNK), F32)],
        compiler_params=_params(("arbitrary",)),
        name="ssd_scan",
    )(main3, main3, main3, main3, small3, shift, conv_w, conv_b, dt_bias, a_log, d_skip, norm_g)
    return y.reshape(batch * seq, SSM_INNER)


def _merge_kernel(att_ref, y_ref, ga_ref, gs_ref, wa_ref, ws_ref, o_ref):
    pa = _dot(att_ref[...], wa_ref[...])
    ps = _dot(y_ref[...], ws_ref[...])
    o_ref[...] = (_sigmoid(ga_ref[...].astype(F32)) * pa + _sigmoid(gs_ref[...].astype(F32)) * ps).astype(o_ref.dtype)


def _merge(att, y, main, wa, ws, tm, tn):
    t = att.shape[0]
    return pl.pallas_call(
        _merge_kernel,
        grid=(t // tm, D_MODEL // tn),
        in_specs=[pl.BlockSpec((tm, ATT_Q), lambda i, j: (i, 0)),
                  pl.BlockSpec((tm, SSM_INNER), lambda i, j: (i, 0)),
                  pl.BlockSpec((tm, tn), lambda i, j: (i, M_GA // tn + j)),
                  pl.BlockSpec((tm, tn), lambda i, j: (i, M_GS // tn + j)),
                  pl.BlockSpec((ATT_Q, tn), lambda i, j: (0, j)),
                  pl.BlockSpec((SSM_INNER, tn), lambda i, j: (0, j))],
        out_specs=pl.BlockSpec((tm, tn), lambda i, j: (i, j)),
        out_shape=jax.ShapeDtypeStruct((t, D_MODEL), BF16),
        compiler_params=_params(("parallel", "arbitrary")),
        name="gated_merge",
    )(att, y, main, main, wa, ws)


def _out_proj_kernel(m_ref, w_ref, x_ref, g_ref, x1_ref, h2_ref):
    x1 = x_ref[...] + _dot(m_ref[...], w_ref[...])
    x1_ref[...] = x1
    ms = jnp.mean(x1 * x1, axis=-1, keepdims=True)
    h2_ref[...] = (x1 * lax.rsqrt(ms + EPS) * g_ref[...]).astype(BF16)


def _out_proj(merged, w, x, g, tm):
    t = x.shape[0]
    return pl.pallas_call(
        _out_proj_kernel,
        grid=(t // tm,),
        in_specs=[pl.BlockSpec((tm, D_MODEL), lambda i: (i, 0)),
                  pl.BlockSpec((D_MODEL, D_MODEL), lambda i: (0, 0)),
                  pl.BlockSpec((tm, D_MODEL), lambda i: (i, 0)),
                  pl.BlockSpec((1, D_MODEL), lambda i: (0, 0))],
        out_specs=[pl.BlockSpec((tm, D_MODEL), lambda i: (i, 0)),
                   pl.BlockSpec((tm, D_MODEL), lambda i: (i, 0))],
        out_shape=[jax.ShapeDtypeStruct((t, D_MODEL), F32),
                   jax.ShapeDtypeStruct((t, D_MODEL), BF16)],
        compiler_params=_params(("parallel",)),
        name="out_proj_norm",
    )(merged, w, x, g)


def _mlp_kernel(h_ref, x1_ref, wu_ref, wd_ref, o_ref):
    @pl.when(pl.program_id(1) == 0)
    def _():
        o_ref[...] = x1_ref[...]

    u = _dot(h_ref[...], wu_ref[...])
    u = jnp.square(jnp.maximum(u, 0.0)).astype(BF16)
    o_ref[...] += _dot(u, wd_ref[...])


def _mlp(h2, x1, wu, wd, tm, th):
    t = h2.shape[0]
    return pl.pallas_call(
        _mlp_kernel,
        grid=(t // tm, MLP_HIDDEN // th),
        in_specs=[pl.BlockSpec((tm, D_MODEL), lambda i, j: (i, 0)),
                  pl.BlockSpec((tm, D_MODEL), lambda i, j: (i, 0)),
                  pl.BlockSpec((D_MODEL, th), lambda i, j: (0, j)),
                  pl.BlockSpec((th, D_MODEL), lambda i, j: (j, 0))],
        out_specs=pl.BlockSpec((tm, D_MODEL), lambda i, j: (i, 0)),
        out_shape=jax.ShapeDtypeStruct((t, D_MODEL), F32),
        compiler_params=_params(("parallel", "arbitrary")),
        name="relu2_mlp",
    )(h2, x1, wu, wd)


def _pack_w_small(w_t):
    assert (O_GA, O_KI, O_Z, O_DT) == (M_GA, M_Z, O_WI + IDX_HEADS, O_Z + M_END - M_Z)
    return jnp.concatenate([w_t[O_KI:O_Z], w_t[O_DT:O_DT + SSM_HEADS],
                            jnp.zeros((S_END - S_DT - SSM_HEADS, w_t.shape[1]), w_t.dtype)], axis=0)


def _block(x2, batch, seq, norm1_g, w_in, conv_w, conv_b, dt_bias, a_log, d_skip, ssm_norm_g, q_norm_g,
           k_norm_g, rel_bias, w_att_branch, w_ssm_branch, w_out, norm2_g, w_up, w_down):
    row = lambda v: v.reshape(1, -1)
    w_t = w_in.T
    xn, small = _norm_small(x2, row(norm1_g), _pack_w_small(w_t), tm=min(1024, x2.shape[0]))
    main, (w_att_b, w_ssm_b, w_out_b, w_up_b, w_down_b) = _in_proj(
        xn, w_t, (w_att_branch, w_ssm_branch, w_out, w_up, w_down), tm=min(1024, x2.shape[0]))
    qt, kn, vt, qit, wt, ki = _qk_prep(main, small, row(q_norm_g), row(k_norm_g), batch, seq, tq=DSA_TQ)
    att = _dsa(qt, kn, vt, qit, wt, ki, rel_bias, batch, seq)
    y = _ssd(main, small, conv_w, row(conv_b), row(dt_bias), row(a_log), row(d_skip), row(ssm_norm_g),
             batch, seq)
    merged = _merge(att, y, main, w_att_b, w_ssm_b, tm=512, tn=1024)
    x1, h2 = _out_proj(merged, w_out_b, x2, row(norm2_g), tm=512)
    return _mlp(h2, x1, w_up_b, w_down_b, tm=512, th=1024)


def kernel(x, norm1_g, w_in, conv_w, conv_b, dt_bias, a_log, d_skip, ssm_norm_g, q_norm_g, k_norm_g, rel_bias,
           w_att_branch, w_ssm_branch, w_out, norm2_g, w_up, w_down):
    batch, seq, d = x.shape
    x2 = x.reshape(batch * seq, d)
    for l in range(norm1_g.shape[0]):
        x2 = _block(x2, batch, seq, norm1_g[l], w_in[l], conv_w[l], conv_b[l], dt_bias[l], a_log[l], d_skip[l],
                    ssm_norm_g[l], q_norm_g[l], k_norm_g[l], rel_bias, w_att_branch[l], w_ssm_branch[l],
                    w_out[l], norm2_g[l], w_up[l], w_down[l])
    return x2.reshape(batch, seq, d)
```

```python
import functools
import math

import numpy as np
import jax
import jax.numpy as jnp
from jax import lax
from jax.experimental import pallas as pl
from jax.experimental.pallas import tpu as pltpu

F32 = jnp.float32
BF16 = jnp.bfloat16
I32 = jnp.int32

D_MODEL = 2048
ATT_HEADS = 16
ATT_KV_HEADS = 4
HEAD_DIM = 128
REP = ATT_HEADS // ATT_KV_HEADS
IDX_HEADS = 16
IDX_DIM = 64
TOPK_MAX = 256
N_BUCKETS = 32
MAX_DISTANCE = 128
SSM_INNER = 2 * D_MODEL
SSM_HEAD_DIM = 64
SSM_HEADS = SSM_INNER // SSM_HEAD_DIM
SSM_GROUPS = 8
SSM_STATE = 128
CONV_WIDTH = 4
CHUNK = 128
MLP_HIDDEN = 4 * D_MODEL
EPS = 1e-6

ATT_Q = ATT_HEADS * HEAD_DIM
ATT_KV = ATT_KV_HEADS * HEAD_DIM
IDX_Q = IDX_HEADS * IDX_DIM
SSM_BC = SSM_GROUPS * SSM_STATE
CONV_DIM = SSM_INNER + 2 * SSM_BC
SPLITS = (D_MODEL, D_MODEL, ATT_Q, ATT_KV, ATT_KV, IDX_Q, IDX_DIM, IDX_HEADS, SSM_INNER, CONV_DIM, SSM_HEADS)
_OFFS = tuple(int(v) for v in np.cumsum((0,) + SPLITS))
(O_GA, O_GS, O_Q, O_K, O_V, O_QI, O_KI, O_WI, O_Z, O_XBC, O_DT, _O_END) = _OFFS

M_GA = 0
M_GS = M_GA + D_MODEL
M_Q = M_GS + D_MODEL
M_K = M_Q + ATT_Q
M_V = M_K + ATT_KV
M_QI = M_V + ATT_KV
M_Z = M_QI + IDX_Q
M_XBC = M_Z + SSM_INNER
M_END = M_XBC + CONV_DIM
S_KI = 0
S_WI = S_KI + IDX_DIM
S_DT = S_WI + IDX_HEADS
S_END = 256

HEADS_PER_GROUP = SSM_HEADS // SSM_GROUPS
GROUP_W = HEADS_PER_GROUP * SSM_HEAD_DIM

LANES = 128
SUBLANES = 8
BF16_ROWS = 2 * SUBLANES
VMEM_LIMIT = 56 * 1024 * 1024

NEG_INF = float("-inf")
INT_MIN = -(2 ** 31)
KEY_NEG_INF = int(np.int32(np.uint32(0xFF800000) ^ np.uint32(0x7FFFFFFF)))
BF16_MIN_NORMAL_BITS = 0x0080


def _dot(a, b):
    return jnp.dot(a, b, preferred_element_type=F32)


def _dot_nt(a, b):
    return lax.dot_general(a, b, (((1,), (1,)), ((), ())), preferred_element_type=F32)


def _split3(x):
    hi = x.astype(BF16)
    r = x - hi.astype(F32)
    mid = r.astype(BF16)
    lo = (r - mid.astype(F32)).astype(BF16)
    return hi, mid, lo


def _silu(x):
    h = 0.5 * x
    return h + h * jnp.tanh(h)


def _sigmoid(x):
    return 0.5 + 0.5 * jnp.tanh(0.5 * x)


def _params(sem):
    return pltpu.CompilerParams(dimension_semantics=sem, vmem_limit_bytes=VMEM_LIMIT)


def _norm_small_kernel(x_ref, g_ref, w_ref, xn_ref, sm_ref):
    x = x_ref[...]
    ms = jnp.mean(x * x, axis=-1, keepdims=True)
    xn = (x * lax.rsqrt(ms + EPS) * g_ref[...]).astype(BF16)
    xn_ref[...] = xn
    sm_ref[...] = _dot_nt(xn, w_ref[...].astype(BF16))


def _norm_small(x, g, w_small_t, tm):
    m, d = x.shape
    n = w_small_t.shape[0]
    return pl.pallas_call(
        _norm_small_kernel,
        grid=(m // tm,),
        in_specs=[pl.BlockSpec((tm, d), lambda i: (i, 0)),
                  pl.BlockSpec((1, d), lambda i: (0, 0)),
                  pl.BlockSpec((n, d), lambda i: (0, 0))],
        out_specs=[pl.BlockSpec((tm, d), lambda i: (i, 0)),
                   pl.BlockSpec((tm, n), lambda i: (i, 0))],
        out_shape=[jax.ShapeDtypeStruct((m, d), BF16), jax.ShapeDtypeStruct((m, n), F32)],
        compiler_params=_params(("parallel",)),
        name="norm_small_proj",
    )(x, g, w_small_t)


IN_TN = 1024
IN_ALIGNED_TILES = O_KI // IN_TN
IN_SHIFT = O_Z - M_Z
IN_NEXT = 128
IN_ROWS = 256


def _in_proj_kernel(*refs, n_cast, n_side):
    xn_ref, wa_ref, wn_ref = refs[:3]
    side_in = refs[3:3 + n_side]
    o_ref = refs[3 + n_side]
    side_out = refs[4 + n_side:4 + 2 * n_side]
    w_scr = refs[4 + 2 * n_side]
    j = pl.program_id(0)
    i = pl.program_id(1)
    _side_cast(j * pl.num_programs(1) + i, n_cast, side_in, side_out)

    @pl.when((i == 0) & (j < IN_ALIGNED_TILES))
    def _():
        for r0 in range(0, IN_TN, IN_ROWS):
            w_scr[r0:r0 + IN_ROWS, :] = wa_ref[r0:r0 + IN_ROWS, :].astype(BF16)

    @pl.when((i == 0) & (j >= IN_ALIGNED_TILES))
    def _():
        for r0 in range(0, IN_TN - IN_SHIFT, IN_ROWS):
            r1 = min(r0 + IN_ROWS, IN_TN - IN_SHIFT)
            w_scr[r0:r1, :] = wa_ref[r0 + IN_SHIFT:r1 + IN_SHIFT, :].astype(BF16)
        w_scr[IN_TN - IN_SHIFT:IN_TN, :] = wn_ref[0:IN_SHIFT, :].astype(BF16)

    o_ref[...] = _dot_nt(xn_ref[...], w_scr[...]).astype(o_ref.dtype)


def _in_proj(xn, w_t, side_weights, tm):
    m, d = xn.shape
    assert O_KI % IN_TN == 0 and M_END % IN_TN == 0 and IN_TN % IN_NEXT == 0
    assert 0 < IN_SHIFT <= IN_NEXT and IN_SHIFT % BF16_ROWS == 0
    nj, ni = M_END // IN_TN, m // tm
    n_cast, side_specs = _side_cast_specs(side_weights, nj * ni, lambda j, i: j * ni + i)
    outs = pl.pallas_call(
        functools.partial(_in_proj_kernel, n_cast=n_cast, n_side=len(side_weights)),
        grid=(nj, ni),
        in_specs=[pl.BlockSpec((tm, d), lambda j, i: (i, 0)),
                  pl.BlockSpec((IN_TN, d), lambda j, i: (j, 0)),
                  pl.BlockSpec((IN_NEXT, d), lambda j, i: ((j + 1) * (IN_TN // IN_NEXT), 0))] + side_specs,
        out_specs=[pl.BlockSpec((tm, IN_TN), lambda j, i: (i, j))] + side_specs,
        out_shape=[jax.ShapeDtypeStruct((m, M_END), BF16)]
                  + [jax.ShapeDtypeStruct(w.shape, BF16) for w in side_weights],
        scratch_shapes=[pltpu.VMEM((IN_TN, d), BF16)],
        compiler_params=_params(("arbitrary", "arbitrary")),
        name="in_proj",
    )(xn, w_t, w_t, *side_weights)
    return outs[0], outs[1:]


def _side_cast_specs(weights, n_steps, step_of):
    n_cast = 1 << (n_steps.bit_length() - 1)
    specs = []
    for w in weights:
        rows = w.shape[0] // n_cast
        assert w.shape[0] % n_cast == 0 and rows % BF16_ROWS == 0
        specs.append(pl.BlockSpec((rows, w.shape[1]), lambda *ids: (jnp.minimum(step_of(*ids), n_cast - 1), 0)))
    return n_cast, specs


def _side_cast(step, n_cast, srcs, dsts):
    @pl.when(step < n_cast)
    def _():
        for src, dst in zip(srcs, dsts):
            dst[...] = src[...].astype(BF16)


LOG2E = math.log2(math.e)
VT_ROWS = HEAD_DIM + BF16_ROWS


def _qk_prep_kernel(q_ref, k_ref, v_ref, qi_ref, sm_ref, qg_ref, kg_ref,
                    qt_ref, kn_ref, vt_ref, qit_ref, wt_ref, ki_ref):
    qg = qg_ref[...]
    for h in range(ATT_HEADS):
        x = q_ref[:, h * HEAD_DIM:(h + 1) * HEAD_DIM].astype(F32)
        ms = jnp.mean(x * x, axis=-1, keepdims=True)
        y = x * lax.rsqrt(ms + EPS) * qg * (HEAD_DIM ** -0.5 * LOG2E)
        qt_ref[0, h] = y.T.astype(BF16)
    kg = kg_ref[...]
    for h in range(ATT_KV_HEADS):
        x = k_ref[:, h * HEAD_DIM:(h + 1) * HEAD_DIM].astype(F32)
        ms = jnp.mean(x * x, axis=-1, keepdims=True)
        kn_ref[:, h * HEAD_DIM:(h + 1) * HEAD_DIM] = (x * lax.rsqrt(ms + EPS) * kg).astype(BF16)
        vt_ref[0, h, 0, 0:HEAD_DIM, :] = v_ref[:, h * HEAD_DIM:(h + 1) * HEAD_DIM].astype(F32).T.astype(BF16)
        vt_ref[0, h, 0, HEAD_DIM:VT_ROWS, :] = jnp.ones((VT_ROWS - HEAD_DIM, v_ref.shape[0]), BF16)
    for p in range(IDX_Q // LANES):
        qit_ref[0, p * LANES:(p + 1) * LANES, :] = qi_ref[:, p * LANES:(p + 1) * LANES].astype(F32).T.astype(BF16)
    sm_t = sm_ref[:, 0:LANES].T
    wt_ref[0] = sm_t[S_WI:S_WI + IDX_HEADS, :] * (IDX_HEADS ** -0.5 * IDX_DIM ** -0.5)
    ki_ref[...] = sm_ref[:, S_KI:S_KI + IDX_DIM].astype(BF16)


def _qk_prep(main, small, qg, kg, batch, seq, tq):
    t = main.shape[0]
    nq = seq // tq
    return pl.pallas_call(
        _qk_prep_kernel,
        grid=(batch, nq),
        in_specs=[pl.BlockSpec((tq, ATT_Q), lambda b, i: (b * nq + i, M_Q // ATT_Q)),
                  pl.BlockSpec((tq, ATT_KV), lambda b, i: (b * nq + i, M_K // ATT_KV)),
                  pl.BlockSpec((tq, ATT_KV), lambda b, i: (b * nq + i, M_V // ATT_KV)),
                  pl.BlockSpec((tq, IDX_Q), lambda b, i: (b * nq + i, M_QI // IDX_Q)),
                  pl.BlockSpec((tq, S_END), lambda b, i: (b * nq + i, 0)),
                  pl.BlockSpec((1, HEAD_DIM), lambda b, i: (0, 0)),
                  pl.BlockSpec((1, HEAD_DIM), lambda b, i: (0, 0))],
        out_specs=[pl.BlockSpec((1, ATT_HEADS, HEAD_DIM, tq), lambda b, i: (b * nq + i, 0, 0, 0)),
                   pl.BlockSpec((tq, ATT_KV), lambda b, i: (b * nq + i, 0)),
                   pl.BlockSpec((1, ATT_KV_HEADS, 1, VT_ROWS, tq), lambda b, i: (b, 0, i, 0, 0)),
                   pl.BlockSpec((1, IDX_Q, tq), lambda b, i: (b * nq + i, 0, 0)),
                   pl.BlockSpec((1, IDX_HEADS, tq), lambda b, i: (b * nq + i, 0, 0)),
                   pl.BlockSpec((tq, IDX_DIM), lambda b, i: (b * nq + i, 0))],
        out_shape=[jax.ShapeDtypeStruct((t // tq, ATT_HEADS, HEAD_DIM, tq), BF16),
                   jax.ShapeDtypeStruct((t, ATT_KV), BF16),
                   jax.ShapeDtypeStruct((batch, ATT_KV_HEADS, nq, VT_ROWS, tq), BF16),
                   jax.ShapeDtypeStruct((t // tq, IDX_Q, tq), BF16),
                   jax.ShapeDtypeStruct((t // tq, IDX_HEADS, tq), F32),
                   jax.ShapeDtypeStruct((t, IDX_DIM), BF16)],
        compiler_params=_params(("parallel", "parallel")),
        name="qk_prep",
    )(main, main, main, main, small, qg, kg)


DSA_TQ = 256
DSA_TK = 256
CNT_WAYS = 4


def _t5_bucket_np(dist):
    n = np.maximum(dist, 0)
    max_exact = N_BUCKETS // 2
    nf = np.maximum(n, 1).astype(np.float32)
    ratio = (np.log(nf / np.float32(max_exact)) / np.float32(math.log(MAX_DISTANCE / max_exact))
             * np.float32(N_BUCKETS - max_exact))
    large = max_exact + ratio.astype(np.int32)
    large = np.minimum(large, N_BUCKETS - 1)
    return np.where(n < max_exact, n, large).astype(np.int32)


def _bias_bucket_tiles(tq, tk):
    r = np.arange(tq)[None, :]
    c = np.arange(tk)[:, None]
    d0 = _t5_bucket_np(r - c)
    d1 = _t5_bucket_np(tk + r - c)
    assert np.all(_t5_bucket_np(np.arange(tk + 1, 8 * tk)) == N_BUCKETS - 1)
    return np.stack([d0, d1]).astype(np.int32)


def _dsa_kernel(relb_ref, bidx_ref, qt_ref, qit_ref, wt_ref, k_ref, vt_ref, ki_ref, o_ref,
                key_ref, plane_ref, madd_ref, bias_ref, s_buf, p_buf, st_ref, acc_ref, *, tq, tk, nkc, topk):
    b = pl.program_id(0)
    i = pl.program_id(1)
    neg_slot = nkc

    @pl.when((b == 0) & (i == 0))
    def _init():
        madd_ref[neg_slot] = jnp.full((tk, tq), NEG_INF, F32)
        for t in range(2):
            bt = bidx_ref[t]

            def head_body(h, carry):
                far = relb_ref[N_BUCKETS - 1, h]

                def bucket_body(bk, acc):
                    return jnp.where(bt == bk, (relb_ref[bk, h] - far) * LOG2E, acc)

                bias_ref[t, h] = lax.fori_loop(0, N_BUCKETS, bucket_body, jnp.zeros((tk, tq), F32))
                return carry

            lax.fori_loop(0, ATT_HEADS, head_body, 0)

    def _select():
        nj = i + 1
        qpos = i * tq + lax.broadcasted_iota(I32, (tk, tq), 1)

        def score_chunk(j):
            kc = ki_ref[pl.ds(pl.multiple_of(j * tk, tk), tk), :]
            acc = jnp.zeros((tk, tq), F32)
            for h in range(IDX_HEADS):
                z = _dot(kc, qit_ref[0, h * IDX_DIM:(h + 1) * IDX_DIM, :])
                acc = acc + jnp.maximum(z, 0.0) * wt_ref[0, h:h + 1, :]
            kpos = j * tk + lax.broadcasted_iota(I32, (tk, tq), 0)
            acc = jnp.where(kpos <= qpos, acc, NEG_INF)
            bits = pltpu.bitcast(acc, I32)
            bits = jnp.where(bits == INT_MIN, 0, bits)
            key = bits ^ ((bits >> 31) & 0x7FFFFFFF)
            key_ref[j] = key
            plane_ref[0, j] = pltpu.bitcast(bits & jnp.int32(-65536), F32).astype(BF16)
            plane_ref[1, j] = ((key >> 8) & 0xFF).astype(F32).astype(BF16)
            plane_ref[2, j] = (key & 0xFF).astype(F32).astype(BF16)

        def score_quad(u, carry):
            for v in range(4):
                score_chunk(4 * u + v)
            return carry

        lax.fori_loop(0, nj // 4, score_quad, 0)
        done = (nj // 4) * 4

        @pl.when(nj % 4 >= 2)
        def _():
            score_chunk(done)
            score_chunk(done + 1)

        @pl.when(nj % 2 == 1)
        def _():
            score_chunk(nj - 1)

        kf = float(topk)
        one_b = jnp.ones((tk, tq), BF16)
        zero_b = jnp.zeros((tk, tq), BF16)
        grp = BF16_ROWS * CNT_WAYS

        def count(plane, cand_b, strict):
            def body(j, cnt):
                pv = plane_ref[plane, j]
                hit = jnp.where(pv > cand_b if strict else pv >= cand_b, one_b, zero_b)
                for r0 in range(0, tk, grp):
                    cnt = cnt + hit[r0:r0 + grp]
                return cnt

            cnt = lax.fori_loop(0, nj, body, jnp.zeros((grp, tq), BF16))
            return jnp.sum(cnt.astype(F32), axis=0, keepdims=True)

        def keep_ties(src, dst, tie_b):
            def body(j, carry):
                plane_ref[dst, j] = jnp.where(plane_ref[src, j] == tie_b, plane_ref[dst, j], -one_b)
                return carry

            lax.fori_loop(0, nj, body, 0)

        def top_digit_float(p16):
            p16 = jnp.where((p16 > 0) & (p16 < BF16_MIN_NORMAL_BITS), BF16_MIN_NORMAL_BITS, p16)
            fb = (p16 & 0xFFFF) ^ jnp.where(p16 < 0, 0x7FFF, 0)
            return pltpu.bitcast(lax.shift_left(fb, 16), F32).astype(BF16)

        def top_iter(it, prefix):
            cand = prefix + lax.shift_left(jnp.int32(1), 15 - it)
            return jnp.where(count(0, top_digit_float(cand), False) >= kf, cand, prefix)

        d_top = lax.fori_loop(0, 16, top_iter, jnp.full((1, tq), -32768, I32))
        t_top = top_digit_float(d_top)
        above = count(0, t_top, True)

        def byte_digit(plane, above_n):
            def it_body(it, prefix):
                cand = prefix + lax.shift_left(jnp.int32(1), 7 - it)
                tot = above_n + count(plane, cand.astype(F32).astype(BF16), False)
                return jnp.where(tot >= kf, cand, prefix)

            return lax.fori_loop(0, 8, it_body, jnp.zeros((1, tq), I32))

        keep_ties(0, 1, t_top)
        d_mid = byte_digit(1, above)
        t_mid = d_mid.astype(F32).astype(BF16)
        above = above + count(1, t_mid, True)
        keep_ties(1, 2, t_mid)
        d_low = byte_digit(2, above)
        t_low = d_low.astype(F32).astype(BF16)
        thr = lax.shift_left(d_top, 16) | lax.shift_left(d_mid, 8) | d_low
        n_ge = above + count(2, t_low, False)
        tied = jnp.max(n_ge) > kf

        @pl.when(jnp.logical_not(tied))
        def _():
            thr_c = jnp.maximum(thr, KEY_NEG_INF + 1)

            def madd_chunk(j, carry):
                madd_ref[j] = jnp.where(key_ref[j] >= thr_c, 0.0, NEG_INF)
                return carry

            lax.fori_loop(0, nj, madd_chunk, 0)

        @pl.when(tied)
        def _():
            need = kf - (above + count(2, t_low, True))
            row = lax.broadcasted_iota(I32, (tk, tq), 0)

            def pos_iter(it, q_pos):
                cand = q_pos + lax.shift_left(jnp.int32(1), (nkc * tk).bit_length() - 2 - it)

                def body(j, cnt):
                    hit = jnp.where(key_ref[j] == thr, jnp.where(row + j * tk < cand, 1.0, 0.0), 0.0)
                    return cnt + jnp.sum(hit.reshape(tk // SUBLANES, SUBLANES, tq), axis=0)

                cnt = lax.fori_loop(0, nj, body, jnp.zeros((SUBLANES, tq), F32))
                return jnp.where(jnp.sum(cnt, axis=0, keepdims=True) < need, cand, q_pos)

            last = lax.fori_loop(0, (nkc * tk).bit_length() - 1, pos_iter, jnp.zeros((1, tq), I32))
            thr_c = jnp.maximum(thr, KEY_NEG_INF)

            def madd_chunk(j, carry):
                kk = key_ref[j]
                take_tie = jnp.where(row + j * tk <= last, jnp.where(kk > KEY_NEG_INF, 0.0, NEG_INF), NEG_INF)
                madd_ref[j] = jnp.where(kk > thr_c, 0.0, jnp.where(kk == thr, take_tie, NEG_INF))
                return carry

            lax.fori_loop(0, nj, madd_chunk, 0)

    all_selected = (i + 1) * tk <= topk

    @pl.when(all_selected)
    def _():
        def madd_chunk(j, carry):
            madd_ref[j] = jnp.zeros((tk, tq), F32)
            return carry

        lax.fori_loop(0, i, madd_chunk, 0)
        madd_ref[i] = jnp.where(lax.broadcasted_iota(I32, (tk, tq), 0) <= lax.broadcasted_iota(I32, (tk, tq), 1),
                                0.0, NEG_INF)

    pl.when(jnp.logical_not(all_selected))(_select)

    ng = ATT_KV_HEADS
    c0 = jnp.maximum(i - 1, 0)
    c1 = jnp.minimum(c0 + 1, nkc - 1)
    n_far = (c0 + 1) // 2

    def scores_to(slot, g, ca, madd_c, bias_c):
        rows = pl.ds(pl.multiple_of(ca * tk, tk), 2 * tk)
        qt = jnp.concatenate([qt_ref[0, g * REP + r] for r in range(REP)], axis=1)
        add = jnp.concatenate([madd_c if bias_c is None else madd_c + bias_c[r] for r in range(REP)], axis=1)
        s = _dot(k_ref[rows, g * HEAD_DIM:(g + 1) * HEAD_DIM], qt) + add
        s_buf[slot] = s
        st_ref[ng + 2 + slot:ng + 3 + slot, :] = jnp.max(s, axis=0, keepdims=True)

    def softmax_to(slot, g):
        m = st_ref[g:g + 1, :]
        m_new = jnp.maximum(m, st_ref[ng + 2 + slot:ng + 3 + slot, :])
        m_safe = jnp.where(m_new == NEG_INF, 0.0, m_new)
        alpha = jnp.exp2(m - m_safe)
        p_buf[slot] = jnp.exp2((s_buf[slot] - m_safe).astype(BF16))
        st_ref[g:g + 1, :] = m_new
        st_ref[ng + slot:ng + 1 + slot, :] = alpha

    def values_from(slot, g, ca, cb):
        vt_c = jnp.concatenate([vt_ref[0, g, ca], vt_ref[0, g, cb]], axis=1)
        acc_ref[g] = st_ref[ng + slot:ng + 1 + slot, :] * acc_ref[g] + _dot(vt_c, p_buf[slot])

    def near_scores_to(slot, g):
        first = i == 0
        t_a = jnp.where(first, 0, 1)
        idx_b = jnp.where(first, neg_slot, c0 + 1)
        madd_near = jnp.concatenate([madd_ref[c0], madd_ref[idx_b]], axis=0)
        bias_near = [jnp.concatenate([bias_ref[t_a, g * REP + r], bias_ref[0, g * REP + r]], axis=0)
                     for r in range(REP)]
        scores_to(slot, g, c0, madd_near, bias_near)

    def far_scores_to(slot, g, k):
        ca = jnp.minimum(2 * (k - 1), nkc - 2)
        cb = jnp.where(ca + 1 >= c0, neg_slot, ca + 1)
        scores_to(slot, g, ca, jnp.concatenate([madd_ref[ca], madd_ref[cb]], axis=0), None)

    st_ref[0:ng, :] = jnp.full((ng, REP * tq), NEG_INF, F32)
    acc_ref[...] = jnp.zeros_like(acc_ref)

    near_scores_to(0, 0)
    for g in range(ng):
        if g + 1 < ng:
            near_scores_to((g + 1) % 2, g + 1)
        else:
            far_scores_to(0, 0, 1)
        softmax_to(g % 2, g)
        values_from(g % 2, g, c0, c1)

    def far_step(k, carry):
        for g in range(ng):
            if g + 1 < ng:
                far_scores_to((g + 1) % 2, g + 1, k)
            else:
                far_scores_to(0, 0, k + 1)
            softmax_to(g % 2, g)
            values_from(g % 2, g, 2 * (k - 1), 2 * (k - 1) + 1)
        return carry

    lax.fori_loop(1, n_far + 1, far_step, 0)

    for g in range(ng):
        out = acc_ref[g, 0:HEAD_DIM, :] / acc_ref[g, HEAD_DIM:HEAD_DIM + 1, :]
        for r in range(REP):
            h = g * REP + r
            o_ref[:, h * HEAD_DIM:(h + 1) * HEAD_DIM] = out[:, r * tq:(r + 1) * tq].T.astype(o_ref.dtype)


def _dsa(qt, kn, vt, qit, wt, ki, rel_bias, batch, seq):
    tq, tk = DSA_TQ, DSA_TK
    nq = seq // tq
    nkc = seq // tk
    topk = min(TOPK_MAX, seq // 4)
    bidx = jnp.asarray(_bias_bucket_tiles(tq, tk))
    kern = functools.partial(_dsa_kernel, tq=tq, tk=tk, nkc=nkc, topk=topk)
    once = pl.Buffered(1)
    return pl.pallas_call(
        kern,
        grid=(batch, nq),
        in_specs=[pl.BlockSpec(memory_space=pltpu.SMEM),
                  pl.BlockSpec((2, tk, tq), lambda b, i: (0, 0, 0), pipeline_mode=once),
                  pl.BlockSpec((1, ATT_HEADS, HEAD_DIM, tq), lambda b, i: (b * nq + i, 0, 0, 0)),
                  pl.BlockSpec((1, IDX_Q, tq), lambda b, i: (b * nq + i, 0, 0)),
                  pl.BlockSpec((1, IDX_HEADS, tq), lambda b, i: (b * nq + i, 0, 0)),
                  pl.BlockSpec((seq, ATT_KV), lambda b, i: (b, 0), pipeline_mode=once),
                  pl.BlockSpec((1, ATT_KV_HEADS, nkc, VT_ROWS, tk), lambda b, i: (b, 0, 0, 0, 0),
                               pipeline_mode=once),
                  pl.BlockSpec((seq, IDX_DIM), lambda b, i: (b, 0), pipeline_mode=once)],
        out_specs=pl.BlockSpec((tq, ATT_Q), lambda b, i: (b * nq + i, 0)),
        out_shape=jax.ShapeDtypeStruct((batch * seq, ATT_Q), BF16),
        scratch_shapes=[pltpu.VMEM((nkc, tk, tq), I32),
                        pltpu.VMEM((3, nkc, tk, tq), BF16),
                        pltpu.VMEM((nkc + 1, tk, tq), F32),
                        pltpu.VMEM((2, ATT_HEADS, tk, tq), F32),
                        pltpu.VMEM((2, 2 * tk, REP * tq), F32),
                        pltpu.VMEM((2, 2 * tk, REP * tq), BF16),
                        pltpu.VMEM((ATT_KV_HEADS + 4, REP * tq), F32),
                        pltpu.VMEM((ATT_KV_HEADS, VT_ROWS, REP * tq), F32)],
        compiler_params=_params(("arbitrary", "arbitrary")),
        name="dsa_attention",
    )(rel_bias, bidx, qt, qit, wt, kn, vt, ki)


E_ROWS = 3 * CHUNK + BF16_ROWS
CONV_HALO = BF16_ROWS


def _conv_shift_matrix():
    ext = CONV_HALO + CHUNK
    s = np.zeros((CHUNK, (CONV_WIDTH - 1) * ext), np.float32)
    for k in range(CONV_WIDTH - 1):
        t = np.arange(CHUNK)
        s[t, k * ext + CONV_HALO + t - (CONV_WIDTH - 1) + k] = 1.0
    return s


def _ssd_kernel(xbc0_ref, xbcn_ref, halon_ref, z_ref, sm_ref, shift_ref, cw_ref, cb_ref, dtb_ref, alog_ref,
                dsk_ref, ng_ref, o_ref, xs_ref, bm_ref, cm_ref, state_ref, ypre_ref, acg_ref, actg_ref, *, batch):
    c = pl.program_id(0)

    cblk = GROUP_W

    def conv_block(slot, bb, cbi, x_ref, halo_ref):
        cols = slice(cbi * cblk, (cbi + 1) * cblk)
        halo = jnp.zeros((CONV_HALO, cblk), BF16) if halo_ref is None else halo_ref[bb, :, cols]
        ext = jnp.concatenate([halo, x_ref[bb, :, cols]], axis=0)
        wtap = cw_ref[:, cols].astype(BF16)
        prods = jnp.concatenate([ext * wtap[kk:kk + 1, :] for kk in range(CONV_WIDTH - 1)], axis=0)
        last = (ext[CONV_HALO:, :] * wtap[CONV_WIDTH - 1:CONV_WIDTH, :]).astype(F32)
        y = _silu(cb_ref[:, cols] + last + _dot(shift_ref[...], prods))
        if cbi < SSM_GROUPS:
            xs_ref[bb, slot, cbi] = y
        else:
            per = cblk // SSM_STATE
            for u in range(per):
                gi = (cbi - SSM_GROUPS) * per + u
                piece = y[:, u * SSM_STATE:(u + 1) * SSM_STATE]
                if gi < SSM_GROUPS:
                    bm_ref[bb, slot, gi] = piece
                else:
                    cm_ref[bb, slot, gi - SSM_GROUPS] = piece

    def conv_jobs(slot, x_ref, halo_ref):
        return [functools.partial(conv_block, slot, bb, cbi, x_ref, halo_ref)
                for cbi in range(CONV_DIM // cblk) for bb in range(batch)]

    def scan_chunk(slot, jobs):
        _ssd_scan_chunk(slot, batch, jobs, z_ref, sm_ref, dtb_ref, alog_ref, dsk_ref, ng_ref, o_ref,
                        xs_ref, bm_ref, cm_ref, state_ref, ypre_ref, acg_ref, actg_ref)

    @pl.when(c == 0)
    def _():
        state_ref[...] = jnp.zeros_like(state_ref)
        for job in conv_jobs(0, xbc0_ref, None):
            job()

    @pl.when(c % 2 == 0)
    def _():
        scan_chunk(0, conv_jobs(1, xbcn_ref, halon_ref))

    @pl.when(c % 2 == 1)
    def _():
        scan_chunk(1, conv_jobs(0, xbcn_ref, halon_ref))


def _ssd_scan_prep(sm_ref, dtb_ref, alog_ref, dsk_ref, acg_ref, actg_ref, tril):
    L = CHUNK
    dt_in = sm_ref[:, S_DT:S_DT + SSM_HEADS] + dtb_ref[...]
    dt_act = jnp.maximum(dt_in, 0.0) + jnp.log1p(jnp.exp(-jnp.abs(dt_in)))
    a = dt_act * (-jnp.exp(alog_ref[...]))
    tri_b = jnp.where(tril, 1.0, 0.0).astype(BF16)
    a_cum = sum(_dot(tri_b, p) for p in _split3(a))
    eye_b = jnp.where(lax.broadcasted_iota(I32, (SSM_HEADS, SSM_HEADS), 0)
                      == lax.broadcasted_iota(I32, (SSM_HEADS, SSM_HEADS), 1), 1.0, 0.0).astype(BF16)
    a_cum_t = sum(_dot_nt(eye_b, p) for p in _split3(a_cum))
    a_last = a_cum[L - 1:L, :]
    for gi in range(SSM_GROUPS):
        acg_ref[gi] = a_cum[:, gi * HEADS_PER_GROUP:(gi + 1) * HEADS_PER_GROUP]
        actg_ref[gi] = a_cum_t[gi * HEADS_PER_GROUP:(gi + 1) * HEADS_PER_GROUP, :]
    cd3 = _split3(jnp.exp(a_last))
    ds3 = _split3(dsk_ref[...])
    extras = jnp.concatenate([p.astype(F32) for p in cd3 + ds3]
                             + [jnp.zeros((E_ROWS - 3 * L - 6, SSM_HEADS), F32)], axis=0)
    return jnp.concatenate([dt_act, jnp.exp(a_cum), jnp.exp(a_last - a_cum), extras], axis=0).astype(BF16)


def _ssd_scan_chunk(slot, batch, jobs, z_ref, sm_ref, dtb_ref, alog_ref, dsk_ref, ng_ref, o_ref,
                    xs_ref, bm_ref, cm_ref, state_ref, ypre_ref, acg_ref, actg_ref):
    L = CHUNK
    ri = lax.broadcasted_iota(I32, (L, L), 0)
    ci = lax.broadcasted_iota(I32, (L, L), 1)
    tril = ri >= ci
    e_mats = [_ssd_scan_prep(sm_ref.at[bb], dtb_ref, alog_ref, dsk_ref, acg_ref.at[bb], actg_ref.at[bb], tril)
              for bb in range(batch)]

    lane = lax.broadcasted_iota(I32, (L, LANES), 1)
    lo_mask = lane < SSM_HEAD_DIM

    def front(bb, gi):
        e_mat = e_mats[bb]
        xs = xs_ref[bb, slot, gi]
        bg = bm_ref[bb, slot, gi]
        cg_b = cm_ref[bb, slot, gi].astype(BF16)
        hsel = (lax.broadcasted_iota(I32, (SSM_HEADS, GROUP_W), 0)
                == gi * HEADS_PER_GROUP + lax.broadcasted_iota(I32, (SSM_HEADS, GROUP_W), 1) // SSM_HEAD_DIM)
        ex = _dot(e_mat, jnp.where(hsel, 1.0, 0.0).astype(BF16))
        dt_rep = ex[0:L]
        expa_rep = ex[L:2 * L]
        dte_rep = ex[2 * L:3 * L]
        cd_rep = ex[3 * L:3 * L + 1] + ex[3 * L + 1:3 * L + 2] + ex[3 * L + 2:3 * L + 3]
        dsk_rep = ex[3 * L + 3:3 * L + 4] + ex[3 * L + 4:3 * L + 5] + ex[3 * L + 5:3 * L + 6]

        xd = xs * dt_rep
        xd_b = xd.astype(BF16)
        cb = _dot_nt(cg_b, bg.astype(BF16))
        st = state_ref[bb, gi]
        y_off = _dot(cg_b, st.astype(BF16)) * expa_rep
        xdd = (xd * dte_rep).astype(BF16)
        state_ref[bb, gi] = st * cd_rep + _dot(bg.T.astype(BF16), xdd)
        ypre_ref[bb, gi] = y_off + dsk_rep * xs
        return cb, xd_b

    def back(bb, gi, cb, xd_b):
        acg = acg_ref[bb, gi]
        actg = actg_ref[bb, gi]
        for pj in range(HEADS_PER_GROUP // 2):
            gmat = []
            for e in (2 * pj, 2 * pj + 1):
                seg = acg[:, e:e + 1] - actg[e:e + 1, :]
                dec = jnp.exp(jnp.where(tril, seg, NEG_INF))
                gmat.append((cb * dec).astype(BF16))
            xp = xd_b[:, pj * LANES:(pj + 1) * LANES]
            zero = jnp.zeros_like(xp)
            ypre_ref[bb, gi, :, pj * LANES:(pj + 1) * LANES] += _dot(
                jnp.concatenate(gmat, axis=1),
                jnp.concatenate([jnp.where(lo_mask, xp, zero), jnp.where(lo_mask, zero, xp)], axis=0))

    stages = [(bb, gi) for gi in range(SSM_GROUPS) for bb in range(batch)]
    ahead = front(*stages[0])
    for idx, stage in enumerate(stages):
        cur = ahead
        if idx + 1 < len(stages):
            ahead = front(*stages[idx + 1])
        back(*stage, *cur)
        for job in jobs[idx * len(jobs) // len(stages):(idx + 1) * len(jobs) // len(stages)]:
            job()

    for gi in range(SSM_GROUPS):
        cols = slice(gi * GROUP_W, (gi + 1) * GROUP_W)
        for bb in range(batch):
            zz = z_ref[bb, :, cols].astype(F32)
            y = ypre_ref[bb, gi] * _silu(zz)
            ms = jnp.mean(y * y, axis=-1, keepdims=True)
            o_ref[bb, :, cols] = (y * lax.rsqrt(ms + EPS) * ng_ref[:, cols]).astype(o_ref.dtype)


def _ssd(main, small, conv_w, conv_b, dt_bias, a_log, d_skip, norm_g, batch, seq):
    nc = seq // CHUNK
    hb = CHUNK // CONV_HALO
    full = lambda shape: pl.BlockSpec(shape, lambda c: (0,) * len(shape))
    shift = jnp.asarray(_conv_shift_matrix(), BF16)
    main3 = main.reshape(batch, seq, M_END)
    small3 = small.reshape(batch, seq, S_END)
    y = pl.pallas_call(
        functools.partial(_ssd_kernel, batch=batch),
        grid=(nc,),
        in_specs=[pl.BlockSpec((batch, CHUNK, CONV_DIM), lambda c: (0, 0, M_XBC // CONV_DIM),
                               pipeline_mode=pl.Buffered(1)),
                  pl.BlockSpec((batch, CHUNK, CONV_DIM),
                               lambda c: (0, jnp.minimum(c + 1, nc - 1), M_XBC // CONV_DIM)),
                  pl.BlockSpec((batch, CONV_HALO, CONV_DIM),
                               lambda c: (0, c * hb + hb - 1, M_XBC // CONV_DIM)),
                  pl.BlockSpec((batch, CHUNK, SSM_INNER), lambda c: (0, c, M_Z // SSM_INNER)),
                  pl.BlockSpec((batch, CHUNK, S_END), lambda c: (0, c, 0)),
                  full((CHUNK, (CONV_WIDTH - 1) * (CONV_HALO + CHUNK))),
                  full((CONV_WIDTH, CONV_DIM)), full((1, CONV_DIM)), full((1, SSM_HEADS)),
                  full((1, SSM_HEADS)), full((1, SSM_HEADS)), full((1, SSM_INNER))],
        out_specs=pl.BlockSpec((batch, CHUNK, SSM_INNER), lambda c: (0, c, 0)),
        out_shape=jax.ShapeDtypeStruct((batch, seq, SSM_INNER), BF16),
        scratch_shapes=[pltpu.VMEM((batch, 2, SSM_GROUPS, CHUNK, GROUP_W), F32),
                        pltpu.VMEM((batch, 2, SSM_GROUPS, CHUNK, SSM_STATE), F32),
                        pltpu.VMEM((batch, 2, SSM_GROUPS, CHUNK, SSM_STATE), F32),
                        pltpu.VMEM((batch, SSM_GROUPS, SSM_STATE, GROUP_W), F32),
                        pltpu.VMEM((batch, SSM_GROUPS, CHUNK, GROUP_W), F32),
                        pltpu.VMEM((batch, SSM_GROUPS, CHUNK, HEADS_PER_GROUP), F32),
                        pltpu.VMEM((batch, SSM_GROUPS, HEADS_PER_GROUP, CHUNK), F32)],
        compiler_params=_params(("arbitrary",)),
        name="ssd_scan",
    )(main3, main3, main3, main3, small3, shift, conv_w, conv_b, dt_bias, a_log, d_skip, norm_g)
    return y.reshape(batch * seq, SSM_INNER)


def _merge_kernel(att_ref, y_ref, ga_ref, gs_ref, wa_ref, ws_ref, o_ref):
    pa = _dot(att_ref[...], wa_ref[...])
    ps = _dot(y_ref[...], ws_ref[...])
    o_ref[...] = (_sigmoid(ga_ref[...].astype(F32)) * pa + _sigmoid(gs_ref[...].astype(F32)) * ps).astype(o_ref.dtype)


def _merge(att, y, main, wa, ws, tm, tn):
    t = att.shape[0]
    return pl.pallas_call(
        _merge_kernel,
        grid=(t // tm, D_MODEL // tn),
        in_specs=[pl.BlockSpec((tm, ATT_Q), lambda i, j: (i, 0)),
                  pl.BlockSpec((tm, SSM_INNER), lambda i, j: (i, 0)),
                  pl.BlockSpec((tm, tn), lambda i, j: (i, M_GA // tn + j)),
                  pl.BlockSpec((tm, tn), lambda i, j: (i, M_GS // tn + j)),
                  pl.BlockSpec((ATT_Q, tn), lambda i, j: (0, j)),
                  pl.BlockSpec((SSM_INNER, tn), lambda i, j: (0, j))],
        out_specs=pl.BlockSpec((tm, tn), lambda i, j: (i, j)),
        out_shape=jax.ShapeDtypeStruct((t, D_MODEL), BF16),
        compiler_params=_params(("parallel", "arbitrary")),
        name="gated_merge",
    )(att, y, main, main, wa, ws)


def _out_proj_kernel(m_ref, w_ref, x_ref, g_ref, x1_ref, h2_ref):
    x1 = x_ref[...] + _dot(m_ref[...], w_ref[...])
    x1_ref[...] = x1
    ms = jnp.mean(x1 * x1, axis=-1, keepdims=True)
    h2_ref[...] = (x1 * lax.rsqrt(ms + EPS) * g_ref[...]).astype(BF16)


def _out_proj(merged, w, x, g, tm):
    t = x.shape[0]
    return pl.pallas_call(
        _out_proj_kernel,
        grid=(t // tm,),
        in_specs=[pl.BlockSpec((tm, D_MODEL), lambda i: (i, 0)),
                  pl.BlockSpec((D_MODEL, D_MODEL), lambda i: (0, 0)),
                  pl.BlockSpec((tm, D_MODEL), lambda i: (i, 0)),
                  pl.BlockSpec((1, D_MODEL), lambda i: (0, 0))],
        out_specs=[pl.BlockSpec((tm, D_MODEL), lambda i: (i, 0)),
                   pl.BlockSpec((tm, D_MODEL), lambda i: (i, 0))],
        out_shape=[jax.ShapeDtypeStruct((t, D_MODEL), F32),
                   jax.ShapeDtypeStruct((t, D_MODEL), BF16)],
        compiler_params=_params(("parallel",)),
        name="out_proj_norm",
    )(merged, w, x, g)


def _mlp_kernel(h_ref, x1_ref, wu_ref, wd_ref, o_ref):
    @pl.when(pl.program_id(1) == 0)
    def _():
        o_ref[...] = x1_ref[...]

    u = _dot(h_ref[...], wu_ref[...])
    u = jnp.square(jnp.maximum(u, 0.0)).astype(BF16)
    o_ref[...] += _dot(u, wd_ref[...])


def _mlp(h2, x1, wu, wd, tm, th):
    t = h2.shape[0]
    return pl.pallas_call(
        _mlp_kernel,
        grid=(t // tm, MLP_HIDDEN // th),
        in_specs=[pl.BlockSpec((tm, D_MODEL), lambda i, j: (i, 0)),
                  pl.BlockSpec((tm, D_MODEL), lambda i, j: (i, 0)),
                  pl.BlockSpec((D_MODEL, th), lambda i, j: (0, j)),
                  pl.BlockSpec((th, D_MODEL), lambda i, j: (j, 0))],
        out_specs=pl.BlockSpec((tm, D_MODEL), lambda i, j: (i, 0)),
        out_shape=jax.ShapeDtypeStruct((t, D_MODEL), F32),
        compiler_params=_params(("parallel", "arbitrary")),
        name="relu2_mlp",
    )(h2, x1, wu, wd)


def _pack_w_small(w_t):
    assert (O_GA, O_KI, O_Z, O_DT) == (M_GA, M_Z, O_WI + IDX_HEADS, O_Z + M_END - M_Z)
    return jnp.concatenate([w_t[O_KI:O_Z], w_t[O_DT:O_DT + SSM_HEADS],
                            jnp.zeros((S_END - S_DT - SSM_HEADS, w_t.shape[1]), w_t.dtype)], axis=0)


def _block(x2, batch, seq, norm1_g, w_in, conv_w, conv_b, dt_bias, a_log, d_skip, ssm_norm_g, q_norm_g,
           k_norm_g, rel_bias, w_att_branch, w_ssm_branch, w_out, norm2_g, w_up, w_down):
    row = lambda v: v.reshape(1, -1)
    w_t = w_in.T
    xn, small = _norm_small(x2, row(norm1_g), _pack_w_small(w_t), tm=min(1024, x2.shape[0]))
    main, (w_att_b, w_ssm_b, w_out_b, w_up_b, w_down_b) = _in_proj(
        xn, w_t, (w_att_branch, w_ssm_branch, w_out, w_up, w_down), tm=min(1024, x2.shape[0]))
    qt, kn, vt, qit, wt, ki = _qk_prep(main, small, row(q_norm_g), row(k_norm_g), batch, seq, tq=DSA_TQ)
    att = _dsa(qt, kn, vt, qit, wt, ki, rel_bias, batch, seq)
    y = _ssd(main, small, conv_w, row(conv_b), row(dt_bias), row(a_log), row(d_skip), row(ssm_norm_g),
             batch, seq)
    merged = _merge(att, y, main, w_att_b, w_ssm_b, tm=512, tn=1024)
    x1, h2 = _out_proj(merged, w_out_b, x2, row(norm2_g), tm=512)
    return _mlp(h2, x1, w_up_b, w_down_b, tm=512, th=1024)


def kernel(x, norm1_g, w_in, conv_w, conv_b, dt_bias, a_log, d_skip, ssm_norm_g, q_norm_g, k_norm_g, rel_bias,
           w_att_branch, w_ssm_branch, w_out, norm2_g, w_up, w_down):
    batch, seq, d = x.shape
    x2 = x.reshape(batch * seq, d)
    for l in range(norm1_g.shape[0]):
        x2 = _block(x2, batch, seq, norm1_g[l], w_in[l], conv_w[l], conv_b[l], dt_bias[l], a_log[l], d_skip[l],
                    ssm_norm_g[l], q_norm_g[l], k_norm_g[l], rel_bias, w_att_branch[l], w_ssm_branch[l],
                    w_out[l], norm2_g[l], w_up[l], w_down[l])
    return x2.reshape(batch, seq, d)
```

```python
import functools
import math

import numpy as np
import jax
import jax.numpy as jnp
from jax import lax
from jax.experimental import pallas as pl
from jax.experimental.pallas import tpu as pltpu

F32 = jnp.float32
BF16 = jnp.bfloat16
I32 = jnp.int32

D_MODEL = 2048
ATT_HEADS = 16
ATT_KV_HEADS = 4
HEAD_DIM = 128
REP = ATT_HEADS // ATT_KV_HEADS
IDX_HEADS = 16
IDX_DIM = 64
TOPK_MAX = 256
N_BUCKETS = 32
MAX_DISTANCE = 128
SSM_INNER = 2 * D_MODEL
SSM_HEAD_DIM = 64
SSM_HEADS = SSM_INNER // SSM_HEAD_DIM
SSM_GROUPS = 8
SSM_STATE = 128
CONV_WIDTH = 4
CHUNK = 128
MLP_HIDDEN = 4 * D_MODEL
EPS = 1e-6

ATT_Q = ATT_HEADS * HEAD_DIM
ATT_KV = ATT_KV_HEADS * HEAD_DIM
IDX_Q = IDX_HEADS * IDX_DIM
SSM_BC = SSM_GROUPS * SSM_STATE
CONV_DIM = SSM_INNER + 2 * SSM_BC
SPLITS = (D_MODEL, D_MODEL, ATT_Q, ATT_KV, ATT_KV, IDX_Q, IDX_DIM, IDX_HEADS, SSM_INNER, CONV_DIM, SSM_HEADS)
_OFFS = tuple(int(v) for v in np.cumsum((0,) + SPLITS))
(O_GA, O_GS, O_Q, O_K, O_V, O_QI, O_KI, O_WI, O_Z, O_XBC, O_DT, _O_END) = _OFFS

M_GA = 0
M_GS = M_GA + D_MODEL
M_Q = M_GS + D_MODEL
M_K = M_Q + ATT_Q
M_V = M_K + ATT_KV
M_QI = M_V + ATT_KV
M_Z = M_QI + IDX_Q
M_XBC = M_Z + SSM_INNER
M_END = M_XBC + CONV_DIM
S_KI = 0
S_WI = S_KI + IDX_DIM
S_DT = S_WI + IDX_HEADS
S_END = 256

HEADS_PER_GROUP = SSM_HEADS // SSM_GROUPS
GROUP_W = HEADS_PER_GROUP * SSM_HEAD_DIM

LANES = 128
SUBLANES = 8
BF16_ROWS = 2 * SUBLANES
VMEM_LIMIT = 56 * 1024 * 1024

NEG_INF = float("-inf")
INT_MIN = -(2 ** 31)
KEY_NEG_INF = int(np.int32(np.uint32(0xFF800000) ^ np.uint32(0x7FFFFFFF)))
BF16_MIN_NORMAL_BITS = 0x0080


def _dot(a, b):
    return jnp.dot(a, b, preferred_element_type=F32)


def _dot_nt(a, b):
    return lax.dot_general(a, b, (((1,), (1,)), ((), ())), preferred_element_type=F32)


def _split3(x):
    hi = x.astype(BF16)
    r = x - hi.astype(F32)
    mid = r.astype(BF16)
    lo = (r - mid.astype(F32)).astype(BF16)
    return hi, mid, lo


def _silu(x):
    h = 0.5 * x
    return h + h * jnp.tanh(h)


def _sigmoid(x):
    return 0.5 + 0.5 * jnp.tanh(0.5 * x)


def _params(sem):
    return pltpu.CompilerParams(dimension_semantics=sem, vmem_limit_bytes=VMEM_LIMIT)


def _norm_small_kernel(x_ref, g_ref, w_ref, xn_ref, sm_ref):
    x = x_ref[...]
    ms = jnp.mean(x * x, axis=-1, keepdims=True)
    xn = (x * lax.rsqrt(ms + EPS) * g_ref[...]).astype(BF16)
    xn_ref[...] = xn
    sm_ref[...] = _dot_nt(xn, w_ref[...].astype(BF16))


def _norm_small(x, g, w_small_t, tm):
    m, d = x.shape
    n = w_small_t.shape[0]
    return pl.pallas_call(
        _norm_small_kernel,
        grid=(m // tm,),
        in_specs=[pl.BlockSpec((tm, d), lambda i: (i, 0)),
                  pl.BlockSpec((1, d), lambda i: (0, 0)),
                  pl.BlockSpec((n, d), lambda i: (0, 0))],
        out_specs=[pl.BlockSpec((tm, d), lambda i: (i, 0)),
                   pl.BlockSpec((tm, n), lambda i: (i, 0))],
        out_shape=[jax.ShapeDtypeStruct((m, d), BF16), jax.ShapeDtypeStruct((m, n), F32)],
        compiler_params=_params(("parallel",)),
        name="norm_small_proj",
    )(x, g, w_small_t)


IN_TN = 1024
IN_ALIGNED_TILES = O_KI // IN_TN
IN_SHIFT = O_Z - M_Z
IN_NEXT = 128
IN_ROWS = 256


def _in_proj_kernel(*refs, n_cast, n_side):
    xn_ref, wa_ref, wn_ref = refs[:3]
    side_in = refs[3:3 + n_side]
    o_ref = refs[3 + n_side]
    side_out = refs[4 + n_side:4 + 2 * n_side]
    w_scr = refs[4 + 2 * n_side]
    j = pl.program_id(0)
    i = pl.program_id(1)
    _side_cast(j * pl.num_programs(1) + i, n_cast, side_in, side_out)

    @pl.when((i == 0) & (j < IN_ALIGNED_TILES))
    def _():
        for r0 in range(0, IN_TN, IN_ROWS):
            w_scr[r0:r0 + IN_ROWS, :] = wa_ref[r0:r0 + IN_ROWS, :].astype(BF16)

    @pl.when((i == 0) & (j >= IN_ALIGNED_TILES))
    def _():
        for r0 in range(0, IN_TN - IN_SHIFT, IN_ROWS):
            r1 = min(r0 + IN_ROWS, IN_TN - IN_SHIFT)
            w_scr[r0:r1, :] = wa_ref[r0 + IN_SHIFT:r1 + IN_SHIFT, :].astype(BF16)
        w_scr[IN_TN - IN_SHIFT:IN_TN, :] = wn_ref[0:IN_SHIFT, :].astype(BF16)

    o_ref[...] = _dot_nt(xn_ref[...], w_scr[...]).astype(o_ref.dtype)


def _in_proj(xn, w_t, side_weights, tm):
    m, d = xn.shape
    assert O_KI % IN_TN == 0 and M_END % IN_TN == 0 and IN_TN % IN_NEXT == 0
    assert 0 < IN_SHIFT <= IN_NEXT and IN_SHIFT % BF16_ROWS == 0
    nj, ni = M_END // IN_TN, m // tm
    n_cast, side_specs = _side_cast_specs(side_weights, nj * ni, lambda j, i: j * ni + i)
    outs = pl.pallas_call(
        functools.partial(_in_proj_kernel, n_cast=n_cast, n_side=len(side_weights)),
        grid=(nj, ni),
        in_specs=[pl.BlockSpec((tm, d), lambda j, i: (i, 0)),
                  pl.BlockSpec((IN_TN, d), lambda j, i: (j, 0)),
                  pl.BlockSpec((IN_NEXT, d), lambda j, i: ((j + 1) * (IN_TN // IN_NEXT), 0))] + side_specs,
        out_specs=[pl.BlockSpec((tm, IN_TN), lambda j, i: (i, j))] + side_specs,
        out_shape=[jax.ShapeDtypeStruct((m, M_END), BF16)]
                  + [jax.ShapeDtypeStruct(w.shape, BF16) for w in side_weights],
        scratch_shapes=[pltpu.VMEM((IN_TN, d), BF16)],
        compiler_params=_params(("arbitrary", "arbitrary")),
        name="in_proj",
    )(xn, w_t, w_t, *side_weights)
    return outs[0], outs[1:]


def _side_cast_specs(weights, n_steps, step_of):
    n_cast = 1 << (n_steps.bit_length() - 1)
    specs = []
    for w in weights:
        rows = w.shape[0] // n_cast
        assert w.shape[0] % n_cast == 0 and rows % BF16_ROWS == 0
        specs.append(pl.BlockSpec((rows, w.shape[1]), lambda *ids: (jnp.minimum(step_of(*ids), n_cast - 1), 0)))
    return n_cast, specs


def _side_cast(step, n_cast, srcs, dsts):
    @pl.when(step < n_cast)
    def _():
        for src, dst in zip(srcs, dsts):
            dst[...] = src[...].astype(BF16)


LOG2E = math.log2(math.e)
VT_ROWS = HEAD_DIM + BF16_ROWS


def _qk_prep_kernel(q_ref, k_ref, v_ref, qi_ref, sm_ref, qg_ref, kg_ref,
                    qt_ref, kn_ref, vt_ref, qit_ref, wt_ref, ki_ref):
    qg = qg_ref[...]
    for h in range(ATT_HEADS):
        x = q_ref[:, h * HEAD_DIM:(h + 1) * HEAD_DIM].astype(F32)
        ms = jnp.mean(x * x, axis=-1, keepdims=True)
        y = x * lax.rsqrt(ms + EPS) * qg * (HEAD_DIM ** -0.5 * LOG2E)
        qt_ref[0, h] = y.T.astype(BF16)
    kg = kg_ref[...]
    for h in range(ATT_KV_HEADS):
        x = k_ref[:, h * HEAD_DIM:(h + 1) * HEAD_DIM].astype(F32)
        ms = jnp.mean(x * x, axis=-1, keepdims=True)
        kn_ref[:, h * HEAD_DIM:(h + 1) * HEAD_DIM] = (x * lax.rsqrt(ms + EPS) * kg).astype(BF16)
        vt_ref[0, h, 0, 0:HEAD_DIM, :] = v_ref[:, h * HEAD_DIM:(h + 1) * HEAD_DIM].astype(F32).T.astype(BF16)
        vt_ref[0, h, 0, HEAD_DIM:VT_ROWS, :] = jnp.ones((VT_ROWS - HEAD_DIM, v_ref.shape[0]), BF16)
    for p in range(IDX_Q // LANES):
        qit_ref[0, p * LANES:(p + 1) * LANES, :] = qi_ref[:, p * LANES:(p + 1) * LANES].astype(F32).T.astype(BF16)
    sm_t = sm_ref[:, 0:LANES].T
    wt_ref[0] = sm_t[S_WI:S_WI + IDX_HEADS, :] * (IDX_HEADS ** -0.5 * IDX_DIM ** -0.5)
    ki_ref[...] = sm_ref[:, S_KI:S_KI + IDX_DIM].astype(BF16)


def _qk_prep(main, small, qg, kg, batch, seq, tq):
    t = main.shape[0]
    nq = seq // tq
    return pl.pallas_call(
        _qk_prep_kernel,
        grid=(batch, nq),
        in_specs=[pl.BlockSpec((tq, ATT_Q), lambda b, i: (b * nq + i, M_Q // ATT_Q)),
                  pl.BlockSpec((tq, ATT_KV), lambda b, i: (b * nq + i, M_K // ATT_KV)),
                  pl.BlockSpec((tq, ATT_KV), lambda b, i: (b * nq + i, M_V // ATT_KV)),
                  pl.BlockSpec((tq, IDX_Q), lambda b, i: (b * nq + i, M_QI // IDX_Q)),
                  pl.BlockSpec((tq, S_END), lambda b, i: (b * nq + i, 0)),
                  pl.BlockSpec((1, HEAD_DIM), lambda b, i: (0, 0)),
                  pl.BlockSpec((1, HEAD_DIM), lambda b, i: (0, 0))],
        out_specs=[pl.BlockSpec((1, ATT_HEADS, HEAD_DIM, tq), lambda b, i: (b * nq + i, 0, 0, 0)),
                   pl.BlockSpec((tq, ATT_KV), lambda b, i: (b * nq + i, 0)),
                   pl.BlockSpec((1, ATT_KV_HEADS, 1, VT_ROWS, tq), lambda b, i: (b, 0, i, 0, 0)),
                   pl.BlockSpec((1, IDX_Q, tq), lambda b, i: (b * nq + i, 0, 0)),
                   pl.BlockSpec((1, IDX_HEADS, tq), lambda b, i: (b * nq + i, 0, 0)),
                   pl.BlockSpec((tq, IDX_DIM), lambda b, i: (b * nq + i, 0))],
        out_shape=[jax.ShapeDtypeStruct((t // tq, ATT_HEADS, HEAD_DIM, tq), BF16),
                   jax.ShapeDtypeStruct((t, ATT_KV), BF16),
                   jax.ShapeDtypeStruct((batch, ATT_KV_HEADS, nq, VT_ROWS, tq), BF16),
                   jax.ShapeDtypeStruct((t // tq, IDX_Q, tq), BF16),
                   jax.ShapeDtypeStruct((t // tq, IDX_HEADS, tq), F32),
                   jax.ShapeDtypeStruct((t, IDX_DIM), BF16)],
        compiler_params=_params(("parallel", "parallel")),
        name="qk_prep",
    )(main, main, main, main, small, qg, kg)


DSA_TQ = 256
DSA_TK = 256
CNT_WAYS = 4


def _t5_bucket_np(dist):
    n = np.maximum(dist, 0)
    max_exact = N_BUCKETS // 2
    nf = np.maximum(n, 1).astype(np.float32)
    ratio = (np.log(nf / np.float32(max_exact)) / np.float32(math.log(MAX_DISTANCE / max_exact))
             * np.float32(N_BUCKETS - max_exact))
    large = max_exact + ratio.astype(np.int32)
    large = np.minimum(large, N_BUCKETS - 1)
    return np.where(n < max_exact, n, large).astype(np.int32)


def _bias_bucket_tiles(tq, tk):
    r = np.arange(tq)[None, :]
    c = np.arange(tk)[:, None]
    d0 = _t5_bucket_np(r - c)
    d1 = _t5_bucket_np(tk + r - c)
    assert np.all(_t5_bucket_np(np.arange(tk + 1, 8 * tk)) == N_BUCKETS - 1)
    return np.stack([d0, d1]).astype(np.int32)


def _bias_bucket_rows(tq, tk):
    assert tq == tk
    j = np.arange(2 * tq)
    signed = np.where(j < tq, j, j - 2 * tq)
    rows = np.stack([_t5_bucket_np(signed + t * tk) for t in range(2)]).astype(np.int32)
    tiles = _bias_bucket_tiles(tq, tk)
    r = np.arange(tq)[None, :]
    c = np.arange(tk)[:, None]
    assert all(np.array_equal(tiles[t], rows[t][(r - c) % (2 * tq)]) for t in range(2))
    return np.broadcast_to(rows[:, None, :], (2, SUBLANES, 2 * tq)).copy()


def _dsa_kernel(relb_ref, bidx_ref, qt_ref, qit_ref, wt_ref, k_ref, vt_ref, ki_ref, o_ref,
                key_ref, plane_ref, madd_ref, bias_ref, s_buf, p_buf, st_ref, acc_ref, *, tq, tk, nkc, topk):
    b = pl.program_id(0)
    i = pl.program_id(1)
    neg_slot = nkc

    @pl.when((b == 0) & (i == 0))
    def _init():
        madd_ref[neg_slot] = jnp.full((tk, tq), NEG_INF, F32)
        for t in range(2):
            bt = bidx_ref[t]

            def head_body(h, carry):
                far = relb_ref[N_BUCKETS - 1, h]

                def bucket_body(bk, acc):
                    return jnp.where(bt == bk, (relb_ref[bk, h] - far) * LOG2E, acc)

                gen = lax.fori_loop(0, N_BUCKETS, bucket_body, jnp.zeros((SUBLANES, 2 * tq), F32))
                full = jnp.broadcast_to(gen[0:1, :], (tk, 2 * tq))
                bias_ref[t, h] = pltpu.roll(full, 0, 1, stride=1, stride_axis=0)[:, 0:tq]
                return carry

            lax.fori_loop(0, ATT_HEADS, head_body, 0)

    def _select():
        nj = i + 1
        qpos = i * tq + lax.broadcasted_iota(I32, (tk, tq), 1)

        def score_chunk(j):
            kc = ki_ref[pl.ds(pl.multiple_of(j * tk, tk), tk), :]
            acc = jnp.zeros((tk, tq), F32)
            for h in range(IDX_HEADS):
                z = _dot(kc, qit_ref[0, h * IDX_DIM:(h + 1) * IDX_DIM, :])
                acc = acc + jnp.maximum(z, 0.0) * wt_ref[0, h:h + 1, :]
            kpos = j * tk + lax.broadcasted_iota(I32, (tk, tq), 0)
            acc = jnp.where(kpos <= qpos, acc, NEG_INF)
            bits = pltpu.bitcast(acc, I32)
            bits = jnp.where(bits == INT_MIN, 0, bits)
            key = bits ^ ((bits >> 31) & 0x7FFFFFFF)
            key_ref[j] = key
            plane_ref[0, j] = pltpu.bitcast(bits & jnp.int32(-65536), F32).astype(BF16)
            plane_ref[1, j] = ((key >> 8) & 0xFF).astype(F32).astype(BF16)
            plane_ref[2, j] = (key & 0xFF).astype(F32).astype(BF16)

        def score_quad(u, carry):
            for v in range(4):
                score_chunk(4 * u + v)
            return carry

        lax.fori_loop(0, nj // 4, score_quad, 0)
        done = (nj // 4) * 4

        @pl.when(nj % 4 >= 2)
        def _():
            score_chunk(done)
            score_chunk(done + 1)

        @pl.when(nj % 2 == 1)
        def _():
            score_chunk(nj - 1)

        kf = float(topk)
        one_b = jnp.ones((tk, tq), BF16)
        zero_b = jnp.zeros((tk, tq), BF16)
        grp = BF16_ROWS * CNT_WAYS

        def count(plane, cand_b, strict):
            def body(j, cnt):
                pv = plane_ref[plane, j]
                hit = jnp.where(pv > cand_b if strict else pv >= cand_b, one_b, zero_b)
                for r0 in range(0, tk, grp):
                    cnt = cnt + hit[r0:r0 + grp]
                return cnt

            cnt = lax.fori_loop(0, nj, body, jnp.zeros((grp, tq), BF16))
            return jnp.sum(cnt.astype(F32), axis=0, keepdims=True)

        def keep_ties(src, dst, tie_b):
            def body(j, carry):
                plane_ref[dst, j] = jnp.where(plane_ref[src, j] == tie_b, plane_ref[dst, j], -one_b)
                return carry

            lax.fori_loop(0, nj, body, 0)

        def top_digit_float(p16):
            p16 = jnp.where((p16 > 0) & (p16 < BF16_MIN_NORMAL_BITS), BF16_MIN_NORMAL_BITS, p16)
            fb = (p16 & 0xFFFF) ^ jnp.where(p16 < 0, 0x7FFF, 0)
            return pltpu.bitcast(lax.shift_left(fb, 16), F32).astype(BF16)

        def top_iter(it, prefix):
            cand = prefix + lax.shift_left(jnp.int32(1), 15 - it)
            return jnp.where(count(0, top_digit_float(cand), False) >= kf, cand, prefix)

        d_top = lax.fori_loop(0, 16, top_iter, jnp.full((1, tq), -32768, I32))
        t_top = top_digit_float(d_top)
        above = count(0, t_top, True)

        def byte_digit(plane, above_n):
            def it_body(it, prefix):
                cand = prefix + lax.shift_left(jnp.int32(1), 7 - it)
                tot = above_n + count(plane, cand.astype(F32).astype(BF16), False)
                return jnp.where(tot >= kf, cand, prefix)

            return lax.fori_loop(0, 8, it_body, jnp.zeros((1, tq), I32))

        keep_ties(0, 1, t_top)
        d_mid = byte_digit(1, above)
        t_mid = d_mid.astype(F32).astype(BF16)
        above = above + count(1, t_mid, True)
        keep_ties(1, 2, t_mid)
        d_low = byte_digit(2, above)
        t_low = d_low.astype(F32).astype(BF16)
        thr = lax.shift_left(d_top, 16) | lax.shift_left(d_mid, 8) | d_low
        n_ge = above + count(2, t_low, False)
        tied = jnp.max(n_ge) > kf

        @pl.when(jnp.logical_not(tied))
        def _():
            thr_c = jnp.maximum(thr, KEY_NEG_INF + 1)

            def madd_chunk(j, carry):
                madd_ref[j] = jnp.where(key_ref[j] >= thr_c, 0.0, NEG_INF)
                return carry

            lax.fori_loop(0, nj, madd_chunk, 0)

        @pl.when(tied)
        def _():
            need = kf - (above + count(2, t_low, True))
            row = lax.broadcasted_iota(I32, (tk, tq), 0)

            def pos_iter(it, q_pos):
                cand = q_pos + lax.shift_left(jnp.int32(1), (nkc * tk).bit_length() - 2 - it)

                def body(j, cnt):
                    hit = jnp.where(key_ref[j] == thr, jnp.where(row + j * tk < cand, 1.0, 0.0), 0.0)
                    return cnt + jnp.sum(hit.reshape(tk // SUBLANES, SUBLANES, tq), axis=0)

                cnt = lax.fori_loop(0, nj, body, jnp.zeros((SUBLANES, tq), F32))
                return jnp.where(jnp.sum(cnt, axis=0, keepdims=True) < need, cand, q_pos)

            last = lax.fori_loop(0, (nkc * tk).bit_length() - 1, pos_iter, jnp.zeros((1, tq), I32))
            thr_c = jnp.maximum(thr, KEY_NEG_INF)

            def madd_chunk(j, carry):
                kk = key_ref[j]
                take_tie = jnp.where(row + j * tk <= last, jnp.where(kk > KEY_NEG_INF, 0.0, NEG_INF), NEG_INF)
                madd_ref[j] = jnp.where(kk > thr_c, 0.0, jnp.where(kk == thr, take_tie, NEG_INF))
                return carry

            lax.fori_loop(0, nj, madd_chunk, 0)

    all_selected = (i + 1) * tk <= topk

    @pl.when(all_selected)
    def _():
        def madd_chunk(j, carry):
            madd_ref[j] = jnp.zeros((tk, tq), F32)
            return carry

        lax.fori_loop(0, i, madd_chunk, 0)
        madd_ref[i] = jnp.where(lax.broadcasted_iota(I32, (tk, tq), 0) <= lax.broadcasted_iota(I32, (tk, tq), 1),
                                0.0, NEG_INF)

    pl.when(jnp.logical_not(all_selected))(_select)

    ng = ATT_KV_HEADS
    c0 = jnp.maximum(i - 1, 0)
    c1 = jnp.minimum(c0 + 1, nkc - 1)
    n_far = (c0 + 1) // 2

    def scores_to(slot, g, ca, madd_c, bias_c):
        rows = pl.ds(pl.multiple_of(ca * tk, tk), 2 * tk)
        qt = jnp.concatenate([qt_ref[0, g * REP + r] for r in range(REP)], axis=1)
        add = jnp.concatenate([madd_c if bias_c is None else madd_c + bias_c[r] for r in range(REP)], axis=1)
        s = _dot(k_ref[rows, g * HEAD_DIM:(g + 1) * HEAD_DIM], qt) + add
        s_buf[slot] = s
        st_ref[ng + 2 + slot:ng + 3 + slot, :] = jnp.max(s, axis=0, keepdims=True)

    def softmax_to(slot, g):
        m = st_ref[g:g + 1, :]
        m_new = jnp.maximum(m, st_ref[ng + 2 + slot:ng + 3 + slot, :])
        m_safe = jnp.where(m_new == NEG_INF, 0.0, m_new)
        alpha = jnp.exp2(m - m_safe)
        p_buf[slot] = jnp.exp2((s_buf[slot] - m_safe).astype(BF16))
        st_ref[g:g + 1, :] = m_new
        st_ref[ng + slot:ng + 1 + slot, :] = alpha

    def values_from(slot, g, ca, cb):
        vt_c = jnp.concatenate([vt_ref[0, g, ca], vt_ref[0, g, cb]], axis=1)
        acc_ref[g] = st_ref[ng + slot:ng + 1 + slot, :] * acc_ref[g] + _dot(vt_c, p_buf[slot])

    def near_scores_to(slot, g):
        first = i == 0
        t_a = jnp.where(first, 0, 1)
        idx_b = jnp.where(first, neg_slot, c0 + 1)
        madd_near = jnp.concatenate([madd_ref[c0], madd_ref[idx_b]], axis=0)
        bias_near = [jnp.concatenate([bias_ref[t_a, g * REP + r], bias_ref[0, g * REP + r]], axis=0)
                     for r in range(REP)]
        scores_to(slot, g, c0, madd_near, bias_near)

    def far_scores_to(slot, g, k):
        ca = jnp.minimum(2 * (k - 1), nkc - 2)
        cb = jnp.where(ca + 1 >= c0, neg_slot, ca + 1)
        scores_to(slot, g, ca, jnp.concatenate([madd_ref[ca], madd_ref[cb]], axis=0), None)

    st_ref[0:ng, :] = jnp.full((ng, REP * tq), NEG_INF, F32)
    acc_ref[...] = jnp.zeros_like(acc_ref)

    near_scores_to(0, 0)
    for g in range(ng):
        if g + 1 < ng:
            near_scores_to((g + 1) % 2, g + 1)
        else:
            far_scores_to(0, 0, 1)
        softmax_to(g % 2, g)
        values_from(g % 2, g, c0, c1)

    def far_step(k, carry):
        for g in range(ng):
            if g + 1 < ng:
                far_scores_to((g + 1) % 2, g + 1, k)
            else:
                far_scores_to(0, 0, k + 1)
            softmax_to(g % 2, g)
            values_from(g % 2, g, 2 * (k - 1), 2 * (k - 1) + 1)
        return carry

    lax.fori_loop(1, n_far + 1, far_step, 0)

    for g in range(ng):
        out = acc_ref[g, 0:HEAD_DIM, :] / acc_ref[g, HEAD_DIM:HEAD_DIM + 1, :]
        for r in range(REP):
            h = g * REP + r
            o_ref[:, h * HEAD_DIM:(h + 1) * HEAD_DIM] = out[:, r * tq:(r + 1) * tq].T.astype(o_ref.dtype)


def _dsa(qt, kn, vt, qit, wt, ki, rel_bias, batch, seq):
    tq, tk = DSA_TQ, DSA_TK
    nq = seq // tq
    nkc = seq // tk
    topk = min(TOPK_MAX, seq // 4)
    bidx = jnp.asarray(_bias_bucket_rows(tq, tk))
    kern = functools.partial(_dsa_kernel, tq=tq, tk=tk, nkc=nkc, topk=topk)
    once = pl.Buffered(1)
    return pl.pallas_call(
        kern,
        grid=(batch, nq),
        in_specs=[pl.BlockSpec(memory_space=pltpu.SMEM),
                  pl.BlockSpec((2, SUBLANES, 2 * tq), lambda b, i: (0, 0, 0), pipeline_mode=once),
                  pl.BlockSpec((1, ATT_HEADS, HEAD_DIM, tq), lambda b, i: (b * nq + i, 0, 0, 0)),
                  pl.BlockSpec((1, IDX_Q, tq), lambda b, i: (b * nq + i, 0, 0)),
                  pl.BlockSpec((1, IDX_HEADS, tq), lambda b, i: (b * nq + i, 0, 0)),
                  pl.BlockSpec((seq, ATT_KV), lambda b, i: (b, 0), pipeline_mode=once),
                  pl.BlockSpec((1, ATT_KV_HEADS, nkc, VT_ROWS, tk), lambda b, i: (b, 0, 0, 0, 0),
                               pipeline_mode=once),
                  pl.BlockSpec((seq, IDX_DIM), lambda b, i: (b, 0), pipeline_mode=once)],
        out_specs=pl.BlockSpec((tq, ATT_Q), lambda b, i: (b * nq + i, 0)),
        out_shape=jax.ShapeDtypeStruct((batch * seq, ATT_Q), BF16),
        scratch_shapes=[pltpu.VMEM((nkc, tk, tq), I32),
                        pltpu.VMEM((3, nkc, tk, tq), BF16),
                        pltpu.VMEM((nkc + 1, tk, tq), F32),
                        pltpu.VMEM((2, ATT_HEADS, tk, tq), F32),
                        pltpu.VMEM((2, 2 * tk, REP * tq), F32),
                        pltpu.VMEM((2, 2 * tk, REP * tq), BF16),
                        pltpu.VMEM((ATT_KV_HEADS + 4, REP * tq), F32),
                        pltpu.VMEM((ATT_KV_HEADS, VT_ROWS, REP * tq), F32)],
        compiler_params=_params(("arbitrary", "arbitrary")),
        name="dsa_attention",
    )(rel_bias, bidx, qt, qit, wt, kn, vt, ki)


E_ROWS = 3 * CHUNK + BF16_ROWS
CONV_HALO = BF16_ROWS


def _conv_shift_matrix():
    ext = CONV_HALO + CHUNK
    s = np.zeros((CHUNK, (CONV_WIDTH - 1) * ext), np.float32)
    for k in range(CONV_WIDTH - 1):
        t = np.arange(CHUNK)
        s[t, k * ext + CONV_HALO + t - (CONV_WIDTH - 1) + k] = 1.0
    return s


def _ssd_kernel(xbc0_ref, xbcn_ref, halon_ref, z_ref, sm_ref, shift_ref, cw_ref, cb_ref, dtb_ref, alog_ref,
                dsk_ref, ng_ref, o_ref, xs_ref, bm_ref, cm_ref, state_ref, ypre_ref, acg_ref, actg_ref, *, batch):
    c = pl.program_id(0)

    cblk = GROUP_W

    def conv_block(slot, bb, cbi, x_ref, halo_ref):
        cols = slice(cbi * cblk, (cbi + 1) * cblk)
        halo = jnp.zeros((CONV_HALO, cblk), BF16) if halo_ref is None else halo_ref[bb, :, cols]
        ext = jnp.concatenate([halo, x_ref[bb, :, cols]], axis=0)
        wtap = cw_ref[:, cols].astype(BF16)
        prods = jnp.concatenate([ext * wtap[kk:kk + 1, :] for kk in range(CONV_WIDTH - 1)], axis=0)
        last = (ext[CONV_HALO:, :] * wtap[CONV_WIDTH - 1:CONV_WIDTH, :]).astype(F32)
        y = _silu(cb_ref[:, cols] + last + _dot(shift_ref[...], prods))
        if cbi < SSM_GROUPS:
            xs_ref[bb, slot, cbi] = y
        else:
            per = cblk // SSM_STATE
            for u in range(per):
                gi = (cbi - SSM_GROUPS) * per + u
                piece = y[:, u * SSM_STATE:(u + 1) * SSM_STATE]
                if gi < SSM_GROUPS:
                    bm_ref[bb, slot, gi] = piece
                else:
                    cm_ref[bb, slot, gi - SSM_GROUPS] = piece

    def conv_jobs(slot, x_ref, halo_ref):
        return [functools.partial(conv_block, slot, bb, cbi, x_ref, halo_ref)
                for cbi in range(CONV_DIM // cblk) for bb in range(batch)]

    def scan_chunk(slot, jobs):
        _ssd_scan_chunk(slot, batch, jobs, z_ref, sm_ref, dtb_ref, alog_ref, dsk_ref, ng_ref, o_ref,
                        xs_ref, bm_ref, cm_ref, state_ref, ypre_ref, acg_ref, actg_ref)

    @pl.when(c == 0)
    def _():
        state_ref[...] = jnp.zeros_like(state_ref)
        for job in conv_jobs(0, xbc0_ref, None):
            job()

    @pl.when(c % 2 == 0)
    def _():
        scan_chunk(0, conv_jobs(1, xbcn_ref, halon_ref))

    @pl.when(c % 2 == 1)
    def _():
        scan_chunk(1, conv_jobs(0, xbcn_ref, halon_ref))


def _ssd_scan_prep(sm_ref, dtb_ref, alog_ref, dsk_ref, acg_ref, actg_ref, tril):
    L = CHUNK
    dt_in = sm_ref[:, S_DT:S_DT + SSM_HEADS] + dtb_ref[...]
    dt_act = jnp.maximum(dt_in, 0.0) + jnp.log1p(jnp.exp(-jnp.abs(dt_in)))
    a = dt_act * (-jnp.exp(alog_ref[...]))
    tri_b = jnp.where(tril, 1.0, 0.0).astype(BF16)
    a_cum = sum(_dot(tri_b, p) for p in _split3(a))
    eye_b = jnp.where(lax.broadcasted_iota(I32, (SSM_HEADS, SSM_HEADS), 0)
                      == lax.broadcasted_iota(I32, (SSM_HEADS, SSM_HEADS), 1), 1.0, 0.0).astype(BF16)
    a_cum_t = sum(_dot_nt(eye_b, p) for p in _split3(a_cum))
    a_last = a_cum[L - 1:L, :]
    for gi in range(SSM_GROUPS):
        acg_ref[gi] = a_cum[:, gi * HEADS_PER_GROUP:(gi + 1) * HEADS_PER_GROUP]
        actg_ref[gi] = a_cum_t[gi * HEADS_PER_GROUP:(gi + 1) * HEADS_PER_GROUP, :]
    cd3 = _split3(jnp.exp(a_last))
    ds3 = _split3(dsk_ref[...])
    extras = jnp.concatenate([p.astype(F32) for p in cd3 + ds3]
                             + [jnp.zeros((E_ROWS - 3 * L - 6, SSM_HEADS), F32)], axis=0)
    return jnp.concatenate([dt_act, jnp.exp(a_cum), jnp.exp(a_last - a_cum), extras], axis=0).astype(BF16)


def _ssd_scan_chunk(slot, batch, jobs, z_ref, sm_ref, dtb_ref, alog_ref, dsk_ref, ng_ref, o_ref,
                    xs_ref, bm_ref, cm_ref, state_ref, ypre_ref, acg_ref, actg_ref):
    L = CHUNK
    ri = lax.broadcasted_iota(I32, (L, L), 0)
    ci = lax.broadcasted_iota(I32, (L, L), 1)
    tril = ri >= ci
    e_mats = [_ssd_scan_prep(sm_ref.at[bb], dtb_ref, alog_ref, dsk_ref, acg_ref.at[bb], actg_ref.at[bb], tril)
              for bb in range(batch)]

    lane = lax.broadcasted_iota(I32, (L, LANES), 1)
    lo_mask = lane < SSM_HEAD_DIM

    def front(bb, gi):
        e_mat = e_mats[bb]
        xs = xs_ref[bb, slot, gi]
        bg = bm_ref[bb, slot, gi]
        cg_b = cm_ref[bb, slot, gi].astype(BF16)
        hsel = (lax.broadcasted_iota(I32, (SSM_HEADS, GROUP_W), 0)
                == gi * HEADS_PER_GROUP + lax.broadcasted_iota(I32, (SSM_HEADS, GROUP_W), 1) // SSM_HEAD_DIM)
        ex = _dot(e_mat, jnp.where(hsel, 1.0, 0.0).astype(BF16))
        dt_rep = ex[0:L]
        expa_rep = ex[L:2 * L]
        dte_rep = ex[2 * L:3 * L]
        cd_rep = ex[3 * L:3 * L + 1] + ex[3 * L + 1:3 * L + 2] + ex[3 * L + 2:3 * L + 3]
        dsk_rep = ex[3 * L + 3:3 * L + 4] + ex[3 * L + 4:3 * L + 5] + ex[3 * L + 5:3 * L + 6]

        xd = xs * dt_rep
        xd_b = xd.astype(BF16)
        cb = _dot_nt(cg_b, bg.astype(BF16))
        st = state_ref[bb, gi]
        y_off = _dot(cg_b, st.astype(BF16)) * expa_rep
        xdd = (xd * dte_rep).astype(BF16)
        state_ref[bb, gi] = st * cd_rep + _dot(bg.T.astype(BF16), xdd)
        ypre_ref[bb, gi] = y_off + dsk_rep * xs
        return cb, xd_b

    def back(bb, gi, cb, xd_b):
        acg = acg_ref[bb, gi]
        actg = actg_ref[bb, gi]
        for pj in range(HEADS_PER_GROUP // 2):
            gmat = []
            for e in (2 * pj, 2 * pj + 1):
                seg = acg[:, e:e + 1] - actg[e:e + 1, :]
                dec = jnp.exp(jnp.where(tril, seg, NEG_INF))
                gmat.append((cb * dec).astype(BF16))
            xp = xd_b[:, pj * LANES:(pj + 1) * LANES]
            zero = jnp.zeros_like(xp)
            ypre_ref[bb, gi, :, pj * LANES:(pj + 1) * LANES] += _dot(
                jnp.concatenate(gmat, axis=1),
                jnp.concatenate([jnp.where(lo_mask, xp, zero), jnp.where(lo_mask, zero, xp)], axis=0))

    stages = [(bb, gi) for gi in range(SSM_GROUPS) for bb in range(batch)]
    ahead = front(*stages[0])
    for idx, stage in enumerate(stages):
        cur = ahead
        if idx + 1 < len(stages):
            ahead = front(*stages[idx + 1])
        back(*stage, *cur)
        for job in jobs[idx * len(jobs) // len(stages):(idx + 1) * len(jobs) // len(stages)]:
            job()

    for gi in range(SSM_GROUPS):
        cols = slice(gi * GROUP_W, (gi + 1) * GROUP_W)
        for bb in range(batch):
            zz = z_ref[bb, :, cols].astype(F32)
            y = ypre_ref[bb, gi] * _silu(zz)
            ms = jnp.mean(y * y, axis=-1, keepdims=True)
            o_ref[bb, :, cols] = (y * lax.rsqrt(ms + EPS) * ng_ref[:, cols]).astype(o_ref.dtype)


def _ssd(main, small, conv_w, conv_b, dt_bias, a_log, d_skip, norm_g, batch, seq):
    nc = seq // CHUNK
    hb = CHUNK // CONV_HALO
    full = lambda shape: pl.BlockSpec(shape, lambda c: (0,) * len(shape))
    shift = jnp.asarray(_conv_shift_matrix(), BF16)
    main3 = main.reshape(batch, seq, M_END)
    small3 = small.reshape(batch, seq, S_END)
    y = pl.pallas_call(
        functools.partial(_ssd_kernel, batch=batch),
        grid=(nc,),
        in_specs=[pl.BlockSpec((batch, CHUNK, CONV_DIM), lambda c: (0, 0, M_XBC // CONV_DIM),
                               pipeline_mode=pl.Buffered(1)),
                  pl.BlockSpec((batch, CHUNK, CONV_DIM),
                               lambda c: (0, jnp.minimum(c + 1, nc - 1), M_XBC // CONV_DIM)),
                  pl.BlockSpec((batch, CONV_HALO, CONV_DIM),
                               lambda c: (0, c * hb + hb - 1, M_XBC // CONV_DIM)),
                  pl.BlockSpec((batch, CHUNK, SSM_INNER), lambda c: (0, c, M_Z // SSM_INNER)),
                  pl.BlockSpec((batch, CHUNK, S_END), lambda c: (0, c, 0)),
                  full((CHUNK, (CONV_WIDTH - 1) * (CONV_HALO + CHUNK))),
                  full((CONV_WIDTH, CONV_DIM)), full((1, CONV_DIM)), full((1, SSM_HEADS)),
                  full((1, SSM_HEADS)), full((1, SSM_HEADS)), full((1, SSM_INNER))],
        out_specs=pl.BlockSpec((batch, CHUNK, SSM_INNER), lambda c: (0, c, 0)),
        out_shape=jax.ShapeDtypeStruct((batch, seq, SSM_INNER), BF16),
        scratch_shapes=[pltpu.VMEM((batch, 2, SSM_GROUPS, CHUNK, GROUP_W), F32),
                        pltpu.VMEM((batch, 2, SSM_GROUPS, CHUNK, SSM_STATE), F32),
                        pltpu.VMEM((batch, 2, SSM_GROUPS, CHUNK, SSM_STATE), F32),
                        pltpu.VMEM((batch, SSM_GROUPS, SSM_STATE, GROUP_W), F32),
                        pltpu.VMEM((batch, SSM_GROUPS, CHUNK, GROUP_W), F32),
                        pltpu.VMEM((batch, SSM_GROUPS, CHUNK, HEADS_PER_GROUP), F32),
                        pltpu.VMEM((batch, SSM_GROUPS, HEADS_PER_GROUP, CHUNK), F32)],
        compiler_params=_params(("arbitrary",)),
        name="ssd_scan",
    )(main3, main3, main3, main3, small3, shift, conv_w, conv_b, dt_bias, a_log, d_skip, norm_g)
    return y.reshape(batch * seq, SSM_INNER)


def _merge_kernel(att_ref, y_ref, ga_ref, gs_ref, wa_ref, ws_ref, o_ref):
    pa = _dot(att_ref[...], wa_ref[...])
    ps = _dot(y_ref[...], ws_ref[...])
    o_ref[...] = (_sigmoid(ga_ref[...].astype(F32)) * pa + _sigmoid(gs_ref[...].astype(F32)) * ps).astype(o_ref.dtype)


def _merge(att, y, main, wa, ws, tm, tn):
    t = att.shape[0]
    return pl.pallas_call(
        _merge_kernel,
        grid=(t // tm, D_MODEL // tn),
        in_specs=[pl.BlockSpec((tm, ATT_Q), lambda i, j: (i, 0)),
                  pl.BlockSpec((tm, SSM_INNER), lambda i, j: (i, 0)),
                  pl.BlockSpec((tm, tn), lambda i, j: (i, M_GA // tn + j)),
                  pl.BlockSpec((tm, tn), lambda i, j: (i, M_GS // tn + j)),
                  pl.BlockSpec((ATT_Q, tn), lambda i, j: (0, j)),
                  pl.BlockSpec((SSM_INNER, tn), lambda i, j: (0, j))],
        out_specs=pl.BlockSpec((tm, tn), lambda i, j: (i, j)),
        out_shape=jax.ShapeDtypeStruct((t, D_MODEL), BF16),
        compiler_params=_params(("parallel", "arbitrary")),
        name="gated_merge",
    )(att, y, main, main, wa, ws)


def _out_proj_kernel(m_ref, w_ref, x_ref, g_ref, x1_ref, h2_ref):
    x1 = x_ref[...] + _dot(m_ref[...], w_ref[...])
    x1_ref[...] = x1
    ms = jnp.mean(x1 * x1, axis=-1, keepdims=True)
    h2_ref[...] = (x1 * lax.rsqrt(ms + EPS) * g_ref[...]).astype(BF16)


def _out_proj(merged, w, x, g, tm):
    t = x.shape[0]
    return pl.pallas_call(
        _out_proj_kernel,
        grid=(t // tm,),
        in_specs=[pl.BlockSpec((tm, D_MODEL), lambda i: (i, 0)),
                  pl.BlockSpec((D_MODEL, D_MODEL), lambda i: (0, 0)),
                  pl.BlockSpec((tm, D_MODEL), lambda i: (i, 0)),
                  pl.BlockSpec((1, D_MODEL), lambda i: (0, 0))],
        out_specs=[pl.BlockSpec((tm, D_MODEL), lambda i: (i, 0)),
                   pl.BlockSpec((tm, D_MODEL), lambda i: (i, 0))],
        out_shape=[jax.ShapeDtypeStruct((t, D_MODEL), F32),
                   jax.ShapeDtypeStruct((t, D_MODEL), BF16)],
        compiler_params=_params(("parallel",)),
        name="out_proj_norm",
    )(merged, w, x, g)


def _mlp_kernel(h_ref, x1_ref, wu_ref, wd_ref, o_ref):
    @pl.when(pl.program_id(1) == 0)
    def _():
        o_ref[...] = x1_ref[...]

    u = _dot(h_ref[...], wu_ref[...])
    u = jnp.square(jnp.maximum(u, 0.0)).astype(BF16)
    o_ref[...] += _dot(u, wd_ref[...])


def _mlp(h2, x1, wu, wd, tm, th):
    t = h2.shape[0]
    return pl.pallas_call(
        _mlp_kernel,
        grid=(t // tm, MLP_HIDDEN // th),
        in_specs=[pl.BlockSpec((tm, D_MODEL), lambda i, j: (i, 0)),
                  pl.BlockSpec((tm, D_MODEL), lambda i, j: (i, 0)),
                  pl.BlockSpec((D_MODEL, th), lambda i, j: (0, j)),
                  pl.BlockSpec((th, D_MODEL), lambda i, j: (j, 0))],
        out_specs=pl.BlockSpec((tm, D_MODEL), lambda i, j: (i, 0)),
        out_shape=jax.ShapeDtypeStruct((t, D_MODEL), F32),
        compiler_params=_params(("parallel", "arbitrary")),
        name="relu2_mlp",
    )(h2, x1, wu, wd)


def _pack_w_small(w_t):
    assert (O_GA, O_KI, O_Z, O_DT) == (M_GA, M_Z, O_WI + IDX_HEADS, O_Z + M_END - M_Z)
    return jnp.concatenate([w_t[O_KI:O_Z], w_t[O_DT:O_DT + SSM_HEADS],
                            jnp.zeros((S_END - S_DT - SSM_HEADS, w_t.shape[1]), w_t.dtype)], axis=0)


def _block(x2, batch, seq, norm1_g, w_in, conv_w, conv_b, dt_bias, a_log, d_skip, ssm_norm_g, q_norm_g,
           k_norm_g, rel_bias, w_att_branch, w_ssm_branch, w_out, norm2_g, w_up, w_down):
    row = lambda v: v.reshape(1, -1)
    w_t = w_in.T
    xn, small = _norm_small(x2, row(norm1_g), _pack_w_small(w_t), tm=min(1024, x2.shape[0]))
    main, (w_att_b, w_ssm_b, w_out_b, w_up_b, w_down_b) = _in_proj(
        xn, w_t, (w_att_branch, w_ssm_branch, w_out, w_up, w_down), tm=min(1024, x2.shape[0]))
    qt, kn, vt, qit, wt, ki = _qk_prep(main, small, row(q_norm_g), row(k_norm_g), batch, seq, tq=DSA_TQ)
    att = _dsa(qt, kn, vt, qit, wt, ki, rel_bias, batch, seq)
    y = _ssd(main, small, conv_w, row(conv_b), row(dt_bias), row(a_log), row(d_skip), row(ssm_norm_g),
             batch, seq)
    merged = _merge(att, y, main, w_att_b, w_ssm_b, tm=512, tn=1024)
    x1, h2 = _out_proj(merged, w_out_b, x2, row(norm2_g), tm=512)
    return _mlp(h2, x1, w_up_b, w_down_b, tm=512, th=1024)


def kernel(x, norm1_g, w_in, conv_w, conv_b, dt_bias, a_log, d_skip, ssm_norm_g, q_norm_g, k_norm_g, rel_bias,
           w_att_branch, w_ssm_branch, w_out, norm2_g, w_up, w_down):
    batch, seq, d = x.shape
    x2 = x.reshape(batch * seq, d)
    for l in range(norm1_g.shape[0]):
        x2 = _block(x2, batch, seq, norm1_g[l], w_in[l], conv_w[l], conv_b[l], dt_bias[l], a_log[l], d_skip[l],
                    ssm_norm_g[l], q_norm_g[l], k_norm_g[l], rel_bias, w_att_branch[l], w_ssm_branch[l],
                    w_out[l], norm2_g[l], w_up[l], w_down[l])
    return x2.reshape(batch, seq, d)
```

```python
import functools
import math

import numpy as np
import jax
import jax.numpy as jnp
from jax import lax
from jax.experimental import pallas as pl
from jax.experimental.pallas import tpu as pltpu

F32 = jnp.float32
BF16 = jnp.bfloat16
I32 = jnp.int32

D_MODEL = 2048
ATT_HEADS = 16
ATT_KV_HEADS = 4
HEAD_DIM = 128
REP = ATT_HEADS // ATT_KV_HEADS
IDX_HEADS = 16
IDX_DIM = 64
TOPK_MAX = 256
N_BUCKETS = 32
MAX_DISTANCE = 128
SSM_INNER = 2 * D_MODEL
SSM_HEAD_DIM = 64
SSM_HEADS = SSM_INNER // SSM_HEAD_DIM
SSM_GROUPS = 8
SSM_STATE = 128
CONV_WIDTH = 4
CHUNK = 128
MLP_HIDDEN = 4 * D_MODEL
EPS = 1e-6

ATT_Q = ATT_HEADS * HEAD_DIM
ATT_KV = ATT_KV_HEADS * HEAD_DIM
IDX_Q = IDX_HEADS * IDX_DIM
SSM_BC = SSM_GROUPS * SSM_STATE
CONV_DIM = SSM_INNER + 2 * SSM_BC
SPLITS = (D_MODEL, D_MODEL, ATT_Q, ATT_KV, ATT_KV, IDX_Q, IDX_DIM, IDX_HEADS, SSM_INNER, CONV_DIM, SSM_HEADS)
_OFFS = tuple(int(v) for v in np.cumsum((0,) + SPLITS))
(O_GA, O_GS, O_Q, O_K, O_V, O_QI, O_KI, O_WI, O_Z, O_XBC, O_DT, _O_END) = _OFFS

M_GA = 0
M_GS = M_GA + D_MODEL
M_Q = M_GS + D_MODEL
M_K = M_Q + ATT_Q
M_V = M_K + ATT_KV
M_QI = M_V + ATT_KV
M_Z = M_QI + IDX_Q
M_XBC = M_Z + SSM_INNER
M_END = M_XBC + CONV_DIM
S_KI = 0
S_WI = S_KI + IDX_DIM
S_DT = S_WI + IDX_HEADS
S_END = 256

HEADS_PER_GROUP = SSM_HEADS // SSM_GROUPS
GROUP_W = HEADS_PER_GROUP * SSM_HEAD_DIM

LANES = 128
SUBLANES = 8
BF16_ROWS = 2 * SUBLANES
VMEM_LIMIT = 56 * 1024 * 1024

NEG_INF = float("-inf")
INT_MIN = -(2 ** 31)
KEY_NEG_INF = int(np.int32(np.uint32(0xFF800000) ^ np.uint32(0x7FFFFFFF)))
BF16_MIN_NORMAL_BITS = 0x0080


def _dot(a, b):
    return jnp.dot(a, b, preferred_element_type=F32)


def _dot_nt(a, b):
    return lax.dot_general(a, b, (((1,), (1,)), ((), ())), preferred_element_type=F32)


def _split3(x):
    hi = x.astype(BF16)
    r = x - hi.astype(F32)
    mid = r.astype(BF16)
    lo = (r - mid.astype(F32)).astype(BF16)
    return hi, mid, lo


def _silu(x):
    h = 0.5 * x
    return h + h * jnp.tanh(h)


def _sigmoid(x):
    return 0.5 + 0.5 * jnp.tanh(0.5 * x)


def _params(sem):
    return pltpu.CompilerParams(dimension_semantics=sem, vmem_limit_bytes=VMEM_LIMIT)


def _norm_small_kernel(x_ref, g_ref, w_ref, xn_ref, sm_ref):
    x = x_ref[...]
    ms = jnp.mean(x * x, axis=-1, keepdims=True)
    xn = (x * lax.rsqrt(ms + EPS) * g_ref[...]).astype(BF16)
    xn_ref[...] = xn
    sm_ref[...] = _dot_nt(xn, w_ref[...].astype(BF16))


def _norm_small(x, g, w_small_t, tm):
    m, d = x.shape
    n = w_small_t.shape[0]
    return pl.pallas_call(
        _norm_small_kernel,
        grid=(m // tm,),
        in_specs=[pl.BlockSpec((tm, d), lambda i: (i, 0)),
                  pl.BlockSpec((1, d), lambda i: (0, 0)),
                  pl.BlockSpec((n, d), lambda i: (0, 0))],
        out_specs=[pl.BlockSpec((tm, d), lambda i: (i, 0)),
                   pl.BlockSpec((tm, n), lambda i: (i, 0))],
        out_shape=[jax.ShapeDtypeStruct((m, d), BF16), jax.ShapeDtypeStruct((m, n), F32)],
        compiler_params=_params(("parallel",)),
        name="norm_small_proj",
    )(x, g, w_small_t)


IN_TN = 1024
IN_ALIGNED_TILES = O_KI // IN_TN
IN_SHIFT = O_Z - M_Z
IN_NEXT = 128
IN_ROWS = 256


def _in_proj_kernel(*refs, n_cast, n_side):
    xn_ref, wa_ref, wn_ref = refs[:3]
    side_in = refs[3:3 + n_side]
    o_ref = refs[3 + n_side]
    side_out = refs[4 + n_side:4 + 2 * n_side]
    w_scr = refs[4 + 2 * n_side]
    j = pl.program_id(0)
    i = pl.program_id(1)
    _side_cast(j * pl.num_programs(1) + i, n_cast, side_in, side_out)

    @pl.when((i == 0) & (j < IN_ALIGNED_TILES))
    def _():
        for r0 in range(0, IN_TN, IN_ROWS):
            w_scr[r0:r0 + IN_ROWS, :] = wa_ref[r0:r0 + IN_ROWS, :].astype(BF16)

    @pl.when((i == 0) & (j >= IN_ALIGNED_TILES))
    def _():
        for r0 in range(0, IN_TN - IN_SHIFT, IN_ROWS):
            r1 = min(r0 + IN_ROWS, IN_TN - IN_SHIFT)
            w_scr[r0:r1, :] = wa_ref[r0 + IN_SHIFT:r1 + IN_SHIFT, :].astype(BF16)
        w_scr[IN_TN - IN_SHIFT:IN_TN, :] = wn_ref[0:IN_SHIFT, :].astype(BF16)

    o_ref[...] = _dot_nt(xn_ref[...], w_scr[...]).astype(o_ref.dtype)


def _in_proj(xn, w_t, side_weights, tm):
    m, d = xn.shape
    assert O_KI % IN_TN == 0 and M_END % IN_TN == 0 and IN_TN % IN_NEXT == 0
    assert 0 < IN_SHIFT <= IN_NEXT and IN_SHIFT % BF16_ROWS == 0
    nj, ni = M_END // IN_TN, m // tm
    n_cast, side_specs = _side_cast_specs(side_weights, nj * ni, lambda j, i: j * ni + i)
    outs = pl.pallas_call(
        functools.partial(_in_proj_kernel, n_cast=n_cast, n_side=len(side_weights)),
        grid=(nj, ni),
        in_specs=[pl.BlockSpec((tm, d), lambda j, i: (i, 0)),
                  pl.BlockSpec((IN_TN, d), lambda j, i: (j, 0)),
                  pl.BlockSpec((IN_NEXT, d), lambda j, i: ((j + 1) * (IN_TN // IN_NEXT), 0))] + side_specs,
        out_specs=[pl.BlockSpec((tm, IN_TN), lambda j, i: (i, j))] + side_specs,
        out_shape=[jax.ShapeDtypeStruct((m, M_END), BF16)]
                  + [jax.ShapeDtypeStruct(w.shape, BF16) for w in side_weights],
        scratch_shapes=[pltpu.VMEM((IN_TN, d), BF16)],
        compiler_params=_params(("arbitrary", "arbitrary")),
        name="in_proj",
    )(xn, w_t, w_t, *side_weights)
    return outs[0], outs[1:]


def _side_cast_specs(weights, n_steps, step_of):
    n_cast = 1 << (n_steps.bit_length() - 1)
    specs = []
    for w in weights:
        rows = w.shape[0] // n_cast
        assert w.shape[0] % n_cast == 0 and rows % BF16_ROWS == 0
        specs.append(pl.BlockSpec((rows, w.shape[1]), lambda *ids: (jnp.minimum(step_of(*ids), n_cast - 1), 0)))
    return n_cast, specs


def _side_cast(step, n_cast, srcs, dsts):
    @pl.when(step < n_cast)
    def _():
        for src, dst in zip(srcs, dsts):
            dst[...] = src[...].astype(BF16)


LOG2E = math.log2(math.e)
VT_ROWS = HEAD_DIM + BF16_ROWS


def _qk_prep_kernel(q_ref, k_ref, v_ref, qi_ref, sm_ref, qg_ref, kg_ref,
                    qt_ref, kn_ref, vt_ref, qit_ref, wt_ref, ki_ref):
    qg = qg_ref[...] * (HEAD_DIM ** -0.5 * LOG2E)
    for h in range(ATT_HEADS):
        x = q_ref[:, h * HEAD_DIM:(h + 1) * HEAD_DIM].astype(F32)
        ms = jnp.mean(x * x, axis=-1, keepdims=True)
        y = x * lax.rsqrt(ms + EPS) * qg
        qt_ref[0, h] = y.T.astype(BF16)
    kg = kg_ref[...]
    for h in range(ATT_KV_HEADS):
        x = k_ref[:, h * HEAD_DIM:(h + 1) * HEAD_DIM].astype(F32)
        ms = jnp.mean(x * x, axis=-1, keepdims=True)
        kn_ref[:, h * HEAD_DIM:(h + 1) * HEAD_DIM] = (x * lax.rsqrt(ms + EPS) * kg).astype(BF16)
        vt_ref[0, h, 0, 0:HEAD_DIM, :] = v_ref[:, h * HEAD_DIM:(h + 1) * HEAD_DIM].astype(F32).T.astype(BF16)
        vt_ref[0, h, 0, HEAD_DIM:VT_ROWS, :] = jnp.ones((VT_ROWS - HEAD_DIM, v_ref.shape[0]), BF16)
    for p in range(IDX_Q // LANES):
        qit_ref[0, p * LANES:(p + 1) * LANES, :] = qi_ref[:, p * LANES:(p + 1) * LANES].astype(F32).T.astype(BF16)
    sm_t = sm_ref[:, 0:LANES].T
    wt_ref[0] = sm_t[S_WI:S_WI + IDX_HEADS, :] * (IDX_HEADS ** -0.5 * IDX_DIM ** -0.5)
    ki_ref[...] = sm_ref[:, S_KI:S_KI + IDX_DIM].astype(BF16)


def _qk_prep(main, small, qg, kg, batch, seq, tq):
    t = main.shape[0]
    nq = seq // tq
    return pl.pallas_call(
        _qk_prep_kernel,
        grid=(batch, nq),
        in_specs=[pl.BlockSpec((tq, ATT_Q), lambda b, i: (b * nq + i, M_Q // ATT_Q)),
                  pl.BlockSpec((tq, ATT_KV), lambda b, i: (b * nq + i, M_K // ATT_KV)),
                  pl.BlockSpec((tq, ATT_KV), lambda b, i: (b * nq + i, M_V // ATT_KV)),
                  pl.BlockSpec((tq, IDX_Q), lambda b, i: (b * nq + i, M_QI // IDX_Q)),
                  pl.BlockSpec((tq, S_END), lambda b, i: (b * nq + i, 0)),
                  pl.BlockSpec((1, HEAD_DIM), lambda b, i: (0, 0)),
                  pl.BlockSpec((1, HEAD_DIM), lambda b, i: (0, 0))],
        out_specs=[pl.BlockSpec((1, ATT_HEADS, HEAD_DIM, tq), lambda b, i: (b * nq + i, 0, 0, 0)),
                   pl.BlockSpec((tq, ATT_KV), lambda b, i: (b * nq + i, 0)),
                   pl.BlockSpec((1, ATT_KV_HEADS, 1, VT_ROWS, tq), lambda b, i: (b, 0, i, 0, 0)),
                   pl.BlockSpec((1, IDX_Q, tq), lambda b, i: (b * nq + i, 0, 0)),
                   pl.BlockSpec((1, IDX_HEADS, tq), lambda b, i: (b * nq + i, 0, 0)),
                   pl.BlockSpec((tq, IDX_DIM), lambda b, i: (b * nq + i, 0))],
        out_shape=[jax.ShapeDtypeStruct((t // tq, ATT_HEADS, HEAD_DIM, tq), BF16),
                   jax.ShapeDtypeStruct((t, ATT_KV), BF16),
                   jax.ShapeDtypeStruct((batch, ATT_KV_HEADS, nq, VT_ROWS, tq), BF16),
                   jax.ShapeDtypeStruct((t // tq, IDX_Q, tq), BF16),
                   jax.ShapeDtypeStruct((t // tq, IDX_HEADS, tq), F32),
                   jax.ShapeDtypeStruct((t, IDX_DIM), BF16)],
        compiler_params=_params(("parallel", "parallel")),
        name="qk_prep",
    )(main, main, main, main, small, qg, kg)


DSA_TQ = 256
DSA_TK = 256
CNT_WAYS = 4


def _t5_bucket_np(dist):
    n = np.maximum(dist, 0)
    max_exact = N_BUCKETS // 2
    nf = np.maximum(n, 1).astype(np.float32)
    ratio = (np.log(nf / np.float32(max_exact)) / np.float32(math.log(MAX_DISTANCE / max_exact))
             * np.float32(N_BUCKETS - max_exact))
    large = max_exact + ratio.astype(np.int32)
    large = np.minimum(large, N_BUCKETS - 1)
    return np.where(n < max_exact, n, large).astype(np.int32)


def _bias_bucket_tiles(tq, tk):
    r = np.arange(tq)[None, :]
    c = np.arange(tk)[:, None]
    d0 = _t5_bucket_np(r - c)
    d1 = _t5_bucket_np(tk + r - c)
    assert np.all(_t5_bucket_np(np.arange(tk + 1, 8 * tk)) == N_BUCKETS - 1)
    return np.stack([d0, d1]).astype(np.int32)


def _bias_bucket_rows(tq, tk):
    assert tq == tk
    j = np.arange(2 * tq)
    signed = np.where(j < tq, j, j - 2 * tq)
    rows = np.stack([_t5_bucket_np(signed + t * tk) for t in range(2)]).astype(np.int32)
    tiles = _bias_bucket_tiles(tq, tk)
    r = np.arange(tq)[None, :]
    c = np.arange(tk)[:, None]
    assert all(np.array_equal(tiles[t], rows[t][(r - c) % (2 * tq)]) for t in range(2))
    return np.broadcast_to(rows[:, None, :], (2, SUBLANES, 2 * tq)).copy()


def _dsa_kernel(relb_ref, bidx_ref, qt_ref, qit_ref, wt_ref, k_ref, vt_ref, ki_ref, o_ref,
                key_ref, plane_ref, madd_ref, bias_ref, s_buf, p_buf, st_ref, acc_ref, *, tq, tk, nkc, topk):
    b = pl.program_id(0)
    i = pl.program_id(1)
    neg_slot = nkc

    @pl.when((b == 0) & (i == 0))
    def _init():
        madd_ref[neg_slot] = jnp.full((tk, tq), NEG_INF, F32)
        relb_t = relb_ref[...]
        adj = (relb_t - relb_t[:, N_BUCKETS - 1:N_BUCKETS]) * LOG2E
        for t in range(2):
            onehot = jnp.where(lax.broadcasted_iota(I32, (N_BUCKETS, 2 * tq), 0) == bidx_ref[t][0:1, :],
                               1.0, 0.0).astype(BF16)
            gen = sum(_dot(p, onehot) for p in _split3(adj))
            for h in range(ATT_HEADS):
                full = jnp.broadcast_to(gen[h:h + 1, :], (tk, 2 * tq))
                bias_ref[t, h] = pltpu.roll(full, 0, 1, stride=1, stride_axis=0)[:, 0:tq]

    def _select():
        nj = i + 1
        qpos = i * tq + lax.broadcasted_iota(I32, (tk, tq), 1)

        def score_chunk(j):
            kc = ki_ref[pl.ds(pl.multiple_of(j * tk, tk), tk), :]
            acc = jnp.zeros((tk, tq), F32)
            for h in range(IDX_HEADS):
                z = _dot(kc, qit_ref[0, h * IDX_DIM:(h + 1) * IDX_DIM, :])
                acc = acc + jnp.maximum(z, 0.0) * wt_ref[0, h:h + 1, :]
            kpos = j * tk + lax.broadcasted_iota(I32, (tk, tq), 0)
            acc = jnp.where(kpos <= qpos, acc, NEG_INF)
            bits = pltpu.bitcast(acc, I32)
            bits = jnp.where(bits == INT_MIN, 0, bits)
            key = bits ^ ((bits >> 31) & 0x7FFFFFFF)
            key_ref[j] = key
            plane_ref[0, j] = pltpu.bitcast(bits & jnp.int32(-65536), F32).astype(BF16)
            plane_ref[1, j] = ((key >> 8) & 0xFF).astype(F32).astype(BF16)
            plane_ref[2, j] = (key & 0xFF).astype(F32).astype(BF16)

        def score_quad(u, carry):
            for v in range(4):
                score_chunk(4 * u + v)
            return carry

        lax.fori_loop(0, nj // 4, score_quad, 0)
        done = (nj // 4) * 4

        @pl.when(nj % 4 >= 2)
        def _():
            score_chunk(done)
            score_chunk(done + 1)

        @pl.when(nj % 2 == 1)
        def _():
            score_chunk(nj - 1)

        kf = float(topk)
        one_b = jnp.ones((tk, tq), BF16)
        zero_b = jnp.zeros((tk, tq), BF16)
        grp = BF16_ROWS * CNT_WAYS

        def count(plane, cand_b, strict):
            def body(j, cnt):
                pv = plane_ref[plane, j]
                hit = jnp.where(pv > cand_b if strict else pv >= cand_b, one_b, zero_b)
                for r0 in range(0, tk, grp):
                    cnt = cnt + hit[r0:r0 + grp]
                return cnt

            cnt = lax.fori_loop(0, nj, body, jnp.zeros((grp, tq), BF16))
            return jnp.sum(cnt.astype(F32), axis=0, keepdims=True)

        def keep_ties(src, dst, tie_b):
            def body(j, carry):
                plane_ref[dst, j] = jnp.where(plane_ref[src, j] == tie_b, plane_ref[dst, j], -one_b)
                return carry

            lax.fori_loop(0, nj, body, 0)

        def top_digit_float(p16):
            p16 = jnp.where((p16 > 0) & (p16 < BF16_MIN_NORMAL_BITS), BF16_MIN_NORMAL_BITS, p16)
            fb = (p16 & 0xFFFF) ^ jnp.where(p16 < 0, 0x7FFF, 0)
            return pltpu.bitcast(lax.shift_left(fb, 16), F32).astype(BF16)

        def top_iter(it, prefix):
            cand = prefix + lax.shift_left(jnp.int32(1), 15 - it)
            return jnp.where(count(0, top_digit_float(cand), False) >= kf, cand, prefix)

        d_top = lax.fori_loop(0, 16, top_iter, jnp.full((1, tq), -32768, I32))
        t_top = top_digit_float(d_top)
        above = count(0, t_top, True)

        def byte_digit(plane, above_n):
            def it_body(it, prefix):
                cand = prefix + lax.shift_left(jnp.int32(1), 7 - it)
                tot = above_n + count(plane, cand.astype(F32).astype(BF16), False)
                return jnp.where(tot >= kf, cand, prefix)

            return lax.fori_loop(0, 8, it_body, jnp.zeros((1, tq), I32))

        keep_ties(0, 1, t_top)
        d_mid = byte_digit(1, above)
        t_mid = d_mid.astype(F32).astype(BF16)
        above = above + count(1, t_mid, True)
        keep_ties(1, 2, t_mid)
        d_low = byte_digit(2, above)
        t_low = d_low.astype(F32).astype(BF16)
        thr = lax.shift_left(d_top, 16) | lax.shift_left(d_mid, 8) | d_low
        n_ge = above + count(2, t_low, False)
        tied = jnp.max(n_ge) > kf

        @pl.when(jnp.logical_not(tied))
        def _():
            thr_c = jnp.maximum(thr, KEY_NEG_INF + 1)

            def madd_chunk(j, carry):
                madd_ref[j] = jnp.where(key_ref[j] >= thr_c, 0.0, NEG_INF)
                return carry

            lax.fori_loop(0, nj, madd_chunk, 0)

        @pl.when(tied)
        def _():
            need = kf - (above + count(2, t_low, True))
            row = lax.broadcasted_iota(I32, (tk, tq), 0)

            def pos_iter(it, q_pos):
                cand = q_pos + lax.shift_left(jnp.int32(1), (nkc * tk).bit_length() - 2 - it)

                def body(j, cnt):
                    hit = jnp.where(key_ref[j] == thr, jnp.where(row + j * tk < cand, 1.0, 0.0), 0.0)
                    return cnt + jnp.sum(hit.reshape(tk // SUBLANES, SUBLANES, tq), axis=0)

                cnt = lax.fori_loop(0, nj, body, jnp.zeros((SUBLANES, tq), F32))
                return jnp.where(jnp.sum(cnt, axis=0, keepdims=True) < need, cand, q_pos)

            last = lax.fori_loop(0, (nkc * tk).bit_length() - 1, pos_iter, jnp.zeros((1, tq), I32))
            thr_c = jnp.maximum(thr, KEY_NEG_INF)

            def madd_chunk(j, carry):
                kk = key_ref[j]
                take_tie = jnp.where(row + j * tk <= last, jnp.where(kk > KEY_NEG_INF, 0.0, NEG_INF), NEG_INF)
                madd_ref[j] = jnp.where(kk > thr_c, 0.0, jnp.where(kk == thr, take_tie, NEG_INF))
                return carry

            lax.fori_loop(0, nj, madd_chunk, 0)

    all_selected = (i + 1) * tk <= topk

    @pl.when(all_selected)
    def _():
        def madd_chunk(j, carry):
            madd_ref[j] = jnp.zeros((tk, tq), F32)
            return carry

        lax.fori_loop(0, i, madd_chunk, 0)
        madd_ref[i] = jnp.where(lax.broadcasted_iota(I32, (tk, tq), 0) <= lax.broadcasted_iota(I32, (tk, tq), 1),
                                0.0, NEG_INF)

    pl.when(jnp.logical_not(all_selected))(_select)

    ng = ATT_KV_HEADS
    c0 = jnp.maximum(i - 1, 0)
    c1 = jnp.minimum(c0 + 1, nkc - 1)
    n_far = (c0 + 1) // 2

    def scores_to(slot, g, ca, madd_c, bias_c):
        rows = pl.ds(pl.multiple_of(ca * tk, tk), 2 * tk)
        qt = jnp.concatenate([qt_ref[0, g * REP + r] for r in range(REP)], axis=1)
        add = jnp.concatenate([madd_c if bias_c is None else madd_c + bias_c[r] for r in range(REP)], axis=1)
        s = _dot(k_ref[rows, g * HEAD_DIM:(g + 1) * HEAD_DIM], qt) + add
        s_buf[slot] = s
        st_ref[ng + 2 + slot:ng + 3 + slot, :] = jnp.max(s, axis=0, keepdims=True)

    def softmax_to(slot, g):
        m = st_ref[g:g + 1, :]
        m_new = jnp.maximum(m, st_ref[ng + 2 + slot:ng + 3 + slot, :])
        m_safe = jnp.where(m_new == NEG_INF, 0.0, m_new)
        alpha = jnp.exp2(m - m_safe)
        p_buf[slot] = jnp.exp2((s_buf[slot] - m_safe).astype(BF16))
        st_ref[g:g + 1, :] = m_new
        st_ref[ng + slot:ng + 1 + slot, :] = alpha

    def values_from(slot, g, ca, cb):
        vt_c = jnp.concatenate([vt_ref[0, g, ca], vt_ref[0, g, cb]], axis=1)
        acc_ref[g] = st_ref[ng + slot:ng + 1 + slot, :] * acc_ref[g] + _dot(vt_c, p_buf[slot])

    def near_scores_to(slot, g):
        first = i == 0
        t_a = jnp.where(first, 0, 1)
        idx_b = jnp.where(first, neg_slot, c0 + 1)
        madd_near = jnp.concatenate([madd_ref[c0], madd_ref[idx_b]], axis=0)
        bias_near = [jnp.concatenate([bias_ref[t_a, g * REP + r], bias_ref[0, g * REP + r]], axis=0)
                     for r in range(REP)]
        scores_to(slot, g, c0, madd_near, bias_near)

    def far_scores_to(slot, g, k):
        ca = jnp.minimum(2 * (k - 1), nkc - 2)
        cb = jnp.where(ca + 1 >= c0, neg_slot, ca + 1)
        scores_to(slot, g, ca, jnp.concatenate([madd_ref[ca], madd_ref[cb]], axis=0), None)

    st_ref[0:ng, :] = jnp.full((ng, REP * tq), NEG_INF, F32)
    acc_ref[...] = jnp.zeros_like(acc_ref)

    near_scores_to(0, 0)
    for g in range(ng):
        if g + 1 < ng:
            near_scores_to((g + 1) % 2, g + 1)
        else:
            far_scores_to(0, 0, 1)
        softmax_to(g % 2, g)
        values_from(g % 2, g, c0, c1)

    def far_step(k, carry):
        for g in range(ng):
            if g + 1 < ng:
                far_scores_to((g + 1) % 2, g + 1, k)
            else:
                far_scores_to(0, 0, k + 1)
            softmax_to(g % 2, g)
            values_from(g % 2, g, 2 * (k - 1), 2 * (k - 1) + 1)
        return carry

    lax.fori_loop(1, n_far + 1, far_step, 0)

    for g in range(ng):
        out = acc_ref[g, 0:HEAD_DIM, :] / acc_ref[g, HEAD_DIM:HEAD_DIM + 1, :]
        for r in range(REP):
            h = g * REP + r
            o_ref[:, h * HEAD_DIM:(h + 1) * HEAD_DIM] = out[:, r * tq:(r + 1) * tq].T.astype(o_ref.dtype)


def _dsa(qt, kn, vt, qit, wt, ki, rel_bias, batch, seq):
    tq, tk = DSA_TQ, DSA_TK
    nq = seq // tq
    nkc = seq // tk
    topk = min(TOPK_MAX, seq // 4)
    bidx = jnp.asarray(_bias_bucket_rows(tq, tk))
    kern = functools.partial(_dsa_kernel, tq=tq, tk=tk, nkc=nkc, topk=topk)
    once = pl.Buffered(1)
    return pl.pallas_call(
        kern,
        grid=(batch, nq),
        in_specs=[pl.BlockSpec((ATT_HEADS, N_BUCKETS), lambda b, i: (0, 0), pipeline_mode=once),
                  pl.BlockSpec((2, SUBLANES, 2 * tq), lambda b, i: (0, 0, 0), pipeline_mode=once),
                  pl.BlockSpec((1, ATT_HEADS, HEAD_DIM, tq), lambda b, i: (b * nq + i, 0, 0, 0)),
                  pl.BlockSpec((1, IDX_Q, tq), lambda b, i: (b * nq + i, 0, 0)),
                  pl.BlockSpec((1, IDX_HEADS, tq), lambda b, i: (b * nq + i, 0, 0)),
                  pl.BlockSpec((seq, ATT_KV), lambda b, i: (b, 0), pipeline_mode=once),
                  pl.BlockSpec((1, ATT_KV_HEADS, nkc, VT_ROWS, tk), lambda b, i: (b, 0, 0, 0, 0),
                               pipeline_mode=once),
                  pl.BlockSpec((seq, IDX_DIM), lambda b, i: (b, 0), pipeline_mode=once)],
        out_specs=pl.BlockSpec((tq, ATT_Q), lambda b, i: (b * nq + i, 0)),
        out_shape=jax.ShapeDtypeStruct((batch * seq, ATT_Q), BF16),
        scratch_shapes=[pltpu.VMEM((nkc, tk, tq), I32),
                        pltpu.VMEM((3, nkc, tk, tq), BF16),
                        pltpu.VMEM((nkc + 1, tk, tq), F32),
                        pltpu.VMEM((2, ATT_HEADS, tk, tq), F32),
                        pltpu.VMEM((2, 2 * tk, REP * tq), F32),
                        pltpu.VMEM((2, 2 * tk, REP * tq), BF16),
                        pltpu.VMEM((ATT_KV_HEADS + 4, REP * tq), F32),
                        pltpu.VMEM((ATT_KV_HEADS, VT_ROWS, REP * tq), F32)],
        compiler_params=_params(("arbitrary", "arbitrary")),
        name="dsa_attention",
    )(rel_bias.T, bidx, qt, qit, wt, kn, vt, ki)


E_ROWS = 3 * CHUNK + BF16_ROWS
CONV_HALO = BF16_ROWS


def _conv_shift_matrix():
    ext = CONV_HALO + CHUNK
    s = np.zeros((CHUNK, (CONV_WIDTH - 1) * ext), np.float32)
    for k in range(CONV_WIDTH - 1):
        t = np.arange(CHUNK)
        s[t, k * ext + CONV_HALO + t - (CONV_WIDTH - 1) + k] = 1.0
    return s


def _ssd_kernel(xbc0_ref, xbcn_ref, halon_ref, z_ref, sm_ref, shift_ref, cw_ref, cb_ref, dtb_ref, alog_ref,
                dsk_ref, ng_ref, o_ref, xs_ref, bm_ref, cm_ref, state_ref, ypre_ref, acg_ref, actg_ref, *, batch):
    c = pl.program_id(0)

    cblk = GROUP_W

    def conv_block(slot, bb, cbi, x_ref, halo_ref):
        cols = slice(cbi * cblk, (cbi + 1) * cblk)
        halo = jnp.zeros((CONV_HALO, cblk), BF16) if halo_ref is None else halo_ref[bb, :, cols]
        ext = jnp.concatenate([halo, x_ref[bb, :, cols]], axis=0)
        wtap = cw_ref[:, cols].astype(BF16)
        prods = jnp.concatenate([ext * wtap[kk:kk + 1, :] for kk in range(CONV_WIDTH - 1)], axis=0)
        last = (ext[CONV_HALO:, :] * wtap[CONV_WIDTH - 1:CONV_WIDTH, :]).astype(F32)
        y = _silu(cb_ref[:, cols] + last + _dot(shift_ref[...], prods))
        if cbi < SSM_GROUPS:
            xs_ref[bb, slot, cbi] = y
        else:
            per = cblk // SSM_STATE
            for u in range(per):
                gi = (cbi - SSM_GROUPS) * per + u
                piece = y[:, u * SSM_STATE:(u + 1) * SSM_STATE]
                if gi < SSM_GROUPS:
                    bm_ref[bb, slot, gi] = piece
                else:
                    cm_ref[bb, slot, gi - SSM_GROUPS] = piece

    def conv_jobs(slot, x_ref, halo_ref):
        return [functools.partial(conv_block, slot, bb, cbi, x_ref, halo_ref)
                for cbi in range(CONV_DIM // cblk) for bb in range(batch)]

    def scan_chunk(slot, jobs):
        _ssd_scan_chunk(slot, batch, jobs, z_ref, sm_ref, dtb_ref, alog_ref, dsk_ref, ng_ref, o_ref,
                        xs_ref, bm_ref, cm_ref, state_ref, ypre_ref, acg_ref, actg_ref)

    @pl.when(c == 0)
    def _():
        state_ref[...] = jnp.zeros_like(state_ref)
        for job in conv_jobs(0, xbc0_ref, None):
            job()

    @pl.when(c % 2 == 0)
    def _():
        scan_chunk(0, conv_jobs(1, xbcn_ref, halon_ref))

    @pl.when(c % 2 == 1)
    def _():
        scan_chunk(1, conv_jobs(0, xbcn_ref, halon_ref))


def _ssd_scan_prep(sm_ref, dtb_ref, alog_ref, dsk_ref, acg_ref, actg_ref, tril):
    L = CHUNK
    dt_in = sm_ref[:, S_DT:S_DT + SSM_HEADS] + dtb_ref[...]
    dt_act = jnp.maximum(dt_in, 0.0) + jnp.log1p(jnp.exp(-jnp.abs(dt_in)))
    a = dt_act * (-jnp.exp(alog_ref[...]))
    tri_b = jnp.where(tril, 1.0, 0.0).astype(BF16)
    a_cum = sum(_dot(tri_b, p) for p in _split3(a))
    eye_b = jnp.where(lax.broadcasted_iota(I32, (SSM_HEADS, SSM_HEADS), 0)
                      == lax.broadcasted_iota(I32, (SSM_HEADS, SSM_HEADS), 1), 1.0, 0.0).astype(BF16)
    a_cum_t = sum(_dot_nt(eye_b, p) for p in _split3(a_cum))
    a_last = a_cum[L - 1:L, :]
    for gi in range(SSM_GROUPS):
        acg_ref[gi] = a_cum[:, gi * HEADS_PER_GROUP:(gi + 1) * HEADS_PER_GROUP]
        actg_ref[gi] = a_cum_t[gi * HEADS_PER_GROUP:(gi + 1) * HEADS_PER_GROUP, :]
    cd3 = _split3(jnp.exp(a_last))
    ds3 = _split3(dsk_ref[...])
    extras = jnp.concatenate([p.astype(F32) for p in cd3 + ds3]
                             + [jnp.zeros((E_ROWS - 3 * L - 6, SSM_HEADS), F32)], axis=0)
    return jnp.concatenate([dt_act, jnp.exp(a_cum), jnp.exp(a_last - a_cum), extras], axis=0).astype(BF16)


def _ssd_scan_chunk(slot, batch, jobs, z_ref, sm_ref, dtb_ref, alog_ref, dsk_ref, ng_ref, o_ref,
                    xs_ref, bm_ref, cm_ref, state_ref, ypre_ref, acg_ref, actg_ref):
    L = CHUNK
    ri = lax.broadcasted_iota(I32, (L, L), 0)
    ci = lax.broadcasted_iota(I32, (L, L), 1)
    tril = ri >= ci
    e_mats = [_ssd_scan_prep(sm_ref.at[bb], dtb_ref, alog_ref, dsk_ref, acg_ref.at[bb], actg_ref.at[bb], tril)
              for bb in range(batch)]

    lane = lax.broadcasted_iota(I32, (L, LANES), 1)
    lo_mask = lane < SSM_HEAD_DIM

    def front(bb, gi):
        e_mat = e_mats[bb]
        xs = xs_ref[bb, slot, gi]
        bg = bm_ref[bb, slot, gi]
        cg_b = cm_ref[bb, slot, gi].astype(BF16)
        hsel = (lax.broadcasted_iota(I32, (SSM_HEADS, GROUP_W), 0)
                == gi * HEADS_PER_GROUP + lax.broadcasted_iota(I32, (SSM_HEADS, GROUP_W), 1) // SSM_HEAD_DIM)
        ex = _dot(e_mat, jnp.where(hsel, 1.0, 0.0).astype(BF16))
        dt_rep = ex[0:L]
        expa_rep = ex[L:2 * L]
        dte_rep = ex[2 * L:3 * L]
        cd_rep = ex[3 * L:3 * L + 1] + ex[3 * L + 1:3 * L + 2] + ex[3 * L + 2:3 * L + 3]
        dsk_rep = ex[3 * L + 3:3 * L + 4] + ex[3 * L + 4:3 * L + 5] + ex[3 * L + 5:3 * L + 6]

        xd = xs * dt_rep
        xd_b = xd.astype(BF16)
        cb = _dot_nt(cg_b, bg.astype(BF16))
        st = state_ref[bb, gi]
        y_off = _dot(cg_b, st.astype(BF16)) * expa_rep
        xdd = (xd * dte_rep).astype(BF16)
        state_ref[bb, gi] = st * cd_rep + _dot(bg.T.astype(BF16), xdd)
        ypre_ref[bb, gi] = y_off + dsk_rep * xs
        return cb, xd_b

    def back(bb, gi, cb, xd_b):
        acg = acg_ref[bb, gi]
        actg = actg_ref[bb, gi]
        for pj in range(HEADS_PER_GROUP // 2):
            gmat = []
            for e in (2 * pj, 2 * pj + 1):
                seg = acg[:, e:e + 1] - actg[e:e + 1, :]
                dec = jnp.exp(jnp.where(tril, seg, NEG_INF))
                gmat.append((cb * dec).astype(BF16))
            xp = xd_b[:, pj * LANES:(pj + 1) * LANES]
            zero = jnp.zeros_like(xp)
            ypre_ref[bb, gi, :, pj * LANES:(pj + 1) * LANES] += _dot(
                jnp.concatenate(gmat, axis=1),
                jnp.concatenate([jnp.where(lo_mask, xp, zero), jnp.where(lo_mask, zero, xp)], axis=0))

    stages = [(bb, gi) for gi in range(SSM_GROUPS) for bb in range(batch)]
    ahead = front(*stages[0])
    for idx, stage in enumerate(stages):
        cur = ahead
        if idx + 1 < len(stages):
            ahead = front(*stages[idx + 1])
        back(*stage, *cur)
        for job in jobs[idx * len(jobs) // len(stages):(idx + 1) * len(jobs) // len(stages)]:
            job()

    for gi in range(SSM_GROUPS):
        cols = slice(gi * GROUP_W, (gi + 1) * GROUP_W)
        for bb in range(batch):
            zz = z_ref[bb, :, cols].astype(F32)
            y = ypre_ref[bb, gi] * _silu(zz)
            ms = jnp.mean(y * y, axis=-1, keepdims=True)
            o_ref[bb, :, cols] = (y * lax.rsqrt(ms + EPS) * ng_ref[:, cols]).astype(o_ref.dtype)


def _ssd(main, small, conv_w, conv_b, dt_bias, a_log, d_skip, norm_g, batch, seq):
    nc = seq // CHUNK
    hb = CHUNK // CONV_HALO
    full = lambda shape: pl.BlockSpec(shape, lambda c: (0,) * len(shape))
    shift = jnp.asarray(_conv_shift_matrix(), BF16)
    main3 = main.reshape(batch, seq, M_END)
    small3 = small.reshape(batch, seq, S_END)
    y = pl.pallas_call(
        functools.partial(_ssd_kernel, batch=batch),
        grid=(nc,),
        in_specs=[pl.BlockSpec((batch, CHUNK, CONV_DIM), lambda c: (0, 0, M_XBC // CONV_DIM),
                               pipeline_mode=pl.Buffered(1)),
                  pl.BlockSpec((batch, CHUNK, CONV_DIM),
                               lambda c: (0, jnp.minimum(c + 1, nc - 1), M_XBC // CONV_DIM)),
                  pl.BlockSpec((batch, CONV_HALO, CONV_DIM),
                               lambda c: (0, c * hb + hb - 1, M_XBC // CONV_DIM)),
                  pl.BlockSpec((batch, CHUNK, SSM_INNER), lambda c: (0, c, M_Z // SSM_INNER)),
                  pl.BlockSpec((batch, CHUNK, S_END), lambda c: (0, c, 0)),
                  full((CHUNK, (CONV_WIDTH - 1) * (CONV_HALO + CHUNK))),
                  full((CONV_WIDTH, CONV_DIM)), full((1, CONV_DIM)), full((1, SSM_HEADS)),
                  full((1, SSM_HEADS)), full((1, SSM_HEADS)), full((1, SSM_INNER))],
        out_specs=pl.BlockSpec((batch, CHUNK, SSM_INNER), lambda c: (0, c, 0)),
        out_shape=jax.ShapeDtypeStruct((batch, seq, SSM_INNER), BF16),
        scratch_shapes=[pltpu.VMEM((batch, 2, SSM_GROUPS, CHUNK, GROUP_W), F32),
                        pltpu.VMEM((batch, 2, SSM_GROUPS, CHUNK, SSM_STATE), F32),
                        pltpu.VMEM((batch, 2, SSM_GROUPS, CHUNK, SSM_STATE), F32),
                        pltpu.VMEM((batch, SSM_GROUPS, SSM_STATE, GROUP_W), F32),
                        pltpu.VMEM((batch, SSM_GROUPS, CHUNK, GROUP_W), F32),
                        pltpu.VMEM((batch, SSM_GROUPS, CHUNK, HEADS_PER_GROUP), F32),
                        pltpu.VMEM((batch, SSM_GROUPS, HEADS_PER_GROUP, CHUNK), F32)],
        compiler_params=_params(("arbitrary",)),
        name="ssd_scan",
    )(main3, main3, main3, main3, small3, shift, conv_w, conv_b, dt_bias, a_log, d_skip, norm_g)
    return y.reshape(batch * seq, SSM_INNER)


def _merge_kernel(att_ref, y_ref, ga_ref, gs_ref, wa_ref, ws_ref, o_ref):
    pa = _dot(att_ref[...], wa_ref[...])
    ps = _dot(y_ref[...], ws_ref[...])
    o_ref[...] = (_sigmoid(ga_ref[...].astype(F32)) * pa + _sigmoid(gs_ref[...].astype(F32)) * ps).astype(o_ref.dtype)


def _merge(att, y, main, wa, ws, tm, tn):
    t = att.shape[0]
    return pl.pallas_call(
        _merge_kernel,
        grid=(t // tm, D_MODEL // tn),
        in_specs=[pl.BlockSpec((tm, ATT_Q), lambda i, j: (i, 0)),
                  pl.BlockSpec((tm, SSM_INNER), lambda i, j: (i, 0)),
                  pl.BlockSpec((tm, tn), lambda i, j: (i, M_GA // tn + j)),
                  pl.BlockSpec((tm, tn), lambda i, j: (i, M_GS // tn + j)),
                  pl.BlockSpec((ATT_Q, tn), lambda i, j: (0, j)),
                  pl.BlockSpec((SSM_INNER, tn), lambda i, j: (0, j))],
        out_specs=pl.BlockSpec((tm, tn), lambda i, j: (i, j)),
        out_shape=jax.ShapeDtypeStruct((t, D_MODEL), BF16),
        compiler_params=_params(("parallel", "arbitrary")),
        name="gated_merge",
    )(att, y, main, main, wa, ws)


def _out_proj_kernel(m_ref, w_ref, x_ref, g_ref, x1_ref, h2_ref):
    x1 = x_ref[...] + _dot(m_ref[...], w_ref[...])
    x1_ref[...] = x1
    ms = jnp.mean(x1 * x1, axis=-1, keepdims=True)
    h2_ref[...] = (x1 * lax.rsqrt(ms + EPS) * g_ref[...]).astype(BF16)


def _out_proj(merged, w, x, g, tm):
    t = x.shape[0]
    return pl.pallas_call(
        _out_proj_kernel,
        grid=(t // tm,),
        in_specs=[pl.BlockSpec((tm, D_MODEL), lambda i: (i, 0)),
                  pl.BlockSpec((D_MODEL, D_MODEL), lambda i: (0, 0)),
                  pl.BlockSpec((tm, D_MODEL), lambda i: (i, 0)),
                  pl.BlockSpec((1, D_MODEL), lambda i: (0, 0))],
        out_specs=[pl.BlockSpec((tm, D_MODEL), lambda i: (i, 0)),
                   pl.BlockSpec((tm, D_MODEL), lambda i: (i, 0))],
        out_shape=[jax.ShapeDtypeStruct((t, D_MODEL), F32),
                   jax.ShapeDtypeStruct((t, D_MODEL), BF16)],
        compiler_params=_params(("parallel",)),
        name="out_proj_norm",
    )(merged, w, x, g)


def _mlp_kernel(h_ref, x1_ref, wu_ref, wd_ref, o_ref):
    @pl.when(pl.program_id(1) == 0)
    def _():
        o_ref[...] = x1_ref[...]

    u = _dot(h_ref[...], wu_ref[...])
    u = jnp.square(jnp.maximum(u, 0.0)).astype(BF16)
    o_ref[...] += _dot(u, wd_ref[...])


def _mlp(h2, x1, wu, wd, tm, th):
    t = h2.shape[0]
    return pl.pallas_call(
        _mlp_kernel,
        grid=(t // tm, MLP_HIDDEN // th),
        in_specs=[pl.BlockSpec((tm, D_MODEL), lambda i, j: (i, 0)),
                  pl.BlockSpec((tm, D_MODEL), lambda i, j: (i, 0)),
                  pl.BlockSpec((D_MODEL, th), lambda i, j: (0, j)),
                  pl.BlockSpec((th, D_MODEL), lambda i, j: (j, 0))],
        out_specs=pl.BlockSpec((tm, D_MODEL), lambda i, j: (i, 0)),
        out_shape=jax.ShapeDtypeStruct((t, D_MODEL), F32),
        compiler_params=_params(("parallel", "arbitrary")),
        name="relu2_mlp",
    )(h2, x1, wu, wd)


def _pack_w_small(w_t):
    assert (O_GA, O_KI, O_Z, O_DT) == (M_GA, M_Z, O_WI + IDX_HEADS, O_Z + M_END - M_Z)
    return jnp.concatenate([w_t[O_KI:O_Z], w_t[O_DT:O_DT + SSM_HEADS],
                            jnp.zeros((S_END - S_DT - SSM_HEADS, w_t.shape[1]), w_t.dtype)], axis=0)


def _block(x2, batch, seq, norm1_g, w_in, conv_w, conv_b, dt_bias, a_log, d_skip, ssm_norm_g, q_norm_g,
           k_norm_g, rel_bias, w_att_branch, w_ssm_branch, w_out, norm2_g, w_up, w_down):
    row = lambda v: v.reshape(1, -1)
    w_t = w_in.T
    xn, small = _norm_small(x2, row(norm1_g), _pack_w_small(w_t), tm=min(1024, x2.shape[0]))
    main, (w_att_b, w_ssm_b, w_out_b, w_up_b, w_down_b) = _in_proj(
        xn, w_t, (w_att_branch, w_ssm_branch, w_out, w_up, w_down), tm=min(1024, x2.shape[0]))
    qt, kn, vt, qit, wt, ki = _qk_prep(main, small, row(q_norm_g), row(k_norm_g), batch, seq, tq=DSA_TQ)
    att = _dsa(qt, kn, vt, qit, wt, ki, rel_bias, batch, seq)
    y = _ssd(main, small, conv_w, row(conv_b), row(dt_bias), row(a_log), row(d_skip), row(ssm_norm_g),
             batch, seq)
    merged = _merge(att, y, main, w_att_b, w_ssm_b, tm=512, tn=1024)
    x1, h2 = _out_proj(merged, w_out_b, x2, row(norm2_g), tm=512)
    return _mlp(h2, x1, w_up_b, w_down_b, tm=512, th=1024)


def kernel(x, norm1_g, w_in, conv_w, conv_b, dt_bias, a_log, d_skip, ssm_norm_g, q_norm_g, k_norm_g, rel_bias,
           w_att_branch, w_ssm_branch, w_out, norm2_g, w_up, w_down):
    batch, seq, d = x.shape
    x2 = x.reshape(batch * seq, d)
    for l in range(norm1_g.shape[0]):
        x2 = _block(x2, batch, seq, norm1_g[l], w_in[l], conv_w[l], conv_b[l], dt_bias[l], a_log[l], d_skip[l],
                    ssm_norm_g[l], q_norm_g[l], k_norm_g[l], rel_bias, w_att_branch[l], w_ssm_branch[l],
                    w_out[l], norm2_g[l], w_up[l], w_down[l])
    return x2.reshape(batch, seq, d)
```

```python
import functools
import math

import numpy as np
import jax
import jax.numpy as jnp
from jax import lax
from jax.experimental import pallas as pl
from jax.experimental.pallas import tpu as pltpu

F32 = jnp.float32
BF16 = jnp.bfloat16
I32 = jnp.int32

D_MODEL = 2048
ATT_HEADS = 16
ATT_KV_HEADS = 4
HEAD_DIM = 128
REP = ATT_HEADS // ATT_KV_HEADS
IDX_HEADS = 16
IDX_DIM = 64
TOPK_MAX = 256
N_BUCKETS = 32
MAX_DISTANCE = 128
SSM_INNER = 2 * D_MODEL
SSM_HEAD_DIM = 64
SSM_HEADS = SSM_INNER // SSM_HEAD_DIM
SSM_GROUPS = 8
SSM_STATE = 128
CONV_WIDTH = 4
CHUNK = 128
MLP_HIDDEN = 4 * D_MODEL
EPS = 1e-6

ATT_Q = ATT_HEADS * HEAD_DIM
ATT_KV = ATT_KV_HEADS * HEAD_DIM
IDX_Q = IDX_HEADS * IDX_DIM
SSM_BC = SSM_GROUPS * SSM_STATE
CONV_DIM = SSM_INNER + 2 * SSM_BC
SPLITS = (D_MODEL, D_MODEL, ATT_Q, ATT_KV, ATT_KV, IDX_Q, IDX_DIM, IDX_HEADS, SSM_INNER, CONV_DIM, SSM_HEADS)
_OFFS = tuple(int(v) for v in np.cumsum((0,) + SPLITS))
(O_GA, O_GS, O_Q, O_K, O_V, O_QI, O_KI, O_WI, O_Z, O_XBC, O_DT, _O_END) = _OFFS

M_GA = 0
M_GS = M_GA + D_MODEL
M_Q = M_GS + D_MODEL
M_K = M_Q + ATT_Q
M_V = M_K + ATT_KV
M_QI = M_V + ATT_KV
M_Z = M_QI + IDX_Q
M_XBC = M_Z + SSM_INNER
M_END = M_XBC + CONV_DIM
S_KI = 0
S_WI = S_KI + IDX_DIM
S_DT = S_WI + IDX_HEADS
S_END = 256

HEADS_PER_GROUP = SSM_HEADS // SSM_GROUPS
GROUP_W = HEADS_PER_GROUP * SSM_HEAD_DIM

LANES = 128
SUBLANES = 8
BF16_ROWS = 2 * SUBLANES
VMEM_LIMIT = 56 * 1024 * 1024

NEG_INF = float("-inf")
INT_MIN = -(2 ** 31)
KEY_NEG_INF = int(np.int32(np.uint32(0xFF800000) ^ np.uint32(0x7FFFFFFF)))
BF16_MIN_NORMAL_BITS = 0x0080


def _dot(a, b):
    return jnp.dot(a, b, preferred_element_type=F32)


def _dot_nt(a, b):
    return lax.dot_general(a, b, (((1,), (1,)), ((), ())), preferred_element_type=F32)


def _split3(x):
    hi = x.astype(BF16)
    r = x - hi.astype(F32)
    mid = r.astype(BF16)
    lo = (r - mid.astype(F32)).astype(BF16)
    return hi, mid, lo


def _silu(x):
    return _silu_of_half(0.5 * x)


def _silu_of_half(h):
    return h + h * jnp.tanh(h)


def _sigmoid(x):
    return 0.5 + 0.5 * jnp.tanh(0.5 * x)


def _params(sem):
    return pltpu.CompilerParams(dimension_semantics=sem, vmem_limit_bytes=VMEM_LIMIT)


def _norm_small_kernel(x_ref, g_ref, w_ref, xn_ref, sm_ref):
    x = x_ref[...]
    ms = jnp.mean(x * x, axis=-1, keepdims=True)
    xn = (x * lax.rsqrt(ms + EPS) * g_ref[...]).astype(BF16)
    xn_ref[...] = xn
    sm_ref[...] = _dot_nt(xn, w_ref[...].astype(BF16))


def _norm_small(x, g, w_small_t, tm):
    m, d = x.shape
    n = w_small_t.shape[0]
    return pl.pallas_call(
        _norm_small_kernel,
        grid=(m // tm,),
        in_specs=[pl.BlockSpec((tm, d), lambda i: (i, 0)),
                  pl.BlockSpec((1, d), lambda i: (0, 0)),
                  pl.BlockSpec((n, d), lambda i: (0, 0))],
        out_specs=[pl.BlockSpec((tm, d), lambda i: (i, 0)),
                   pl.BlockSpec((tm, n), lambda i: (i, 0))],
        out_shape=[jax.ShapeDtypeStruct((m, d), BF16), jax.ShapeDtypeStruct((m, n), F32)],
        compiler_params=_params(("parallel",)),
        name="norm_small_proj",
    )(x, g, w_small_t)


IN_TN = 1024
IN_ALIGNED_TILES = O_KI // IN_TN
IN_SHIFT = O_Z - M_Z
IN_NEXT = 128
IN_ROWS = 256


def _in_proj_kernel(*refs, n_cast, n_side):
    xn_ref, wa_ref, wn_ref = refs[:3]
    side_in = refs[3:3 + n_side]
    o_ref = refs[3 + n_side]
    side_out = refs[4 + n_side:4 + 2 * n_side]
    w_scr = refs[4 + 2 * n_side]
    j = pl.program_id(0)
    i = pl.program_id(1)
    _side_cast(j * pl.num_programs(1) + i, n_cast, side_in, side_out)

    @pl.when((i == 0) & (j < IN_ALIGNED_TILES))
    def _():
        for r0 in range(0, IN_TN, IN_ROWS):
            w_scr[r0:r0 + IN_ROWS, :] = wa_ref[r0:r0 + IN_ROWS, :].astype(BF16)

    @pl.when((i == 0) & (j >= IN_ALIGNED_TILES))
    def _():
        for r0 in range(0, IN_TN - IN_SHIFT, IN_ROWS):
            r1 = min(r0 + IN_ROWS, IN_TN - IN_SHIFT)
            w_scr[r0:r1, :] = wa_ref[r0 + IN_SHIFT:r1 + IN_SHIFT, :].astype(BF16)
        w_scr[IN_TN - IN_SHIFT:IN_TN, :] = wn_ref[0:IN_SHIFT, :].astype(BF16)

    o_ref[...] = _dot_nt(xn_ref[...], w_scr[...]).astype(o_ref.dtype)


def _in_proj(xn, w_t, side_weights, tm):
    m, d = xn.shape
    assert O_KI % IN_TN == 0 and M_END % IN_TN == 0 and IN_TN % IN_NEXT == 0
    assert 0 < IN_SHIFT <= IN_NEXT and IN_SHIFT % BF16_ROWS == 0
    nj, ni = M_END // IN_TN, m // tm
    n_cast, side_specs = _side_cast_specs(side_weights, nj * ni, lambda j, i: j * ni + i)
    outs = pl.pallas_call(
        functools.partial(_in_proj_kernel, n_cast=n_cast, n_side=len(side_weights)),
        grid=(nj, ni),
        in_specs=[pl.BlockSpec((tm, d), lambda j, i: (i, 0)),
                  pl.BlockSpec((IN_TN, d), lambda j, i: (j, 0)),
                  pl.BlockSpec((IN_NEXT, d), lambda j, i: ((j + 1) * (IN_TN // IN_NEXT), 0))] + side_specs,
        out_specs=[pl.BlockSpec((tm, IN_TN), lambda j, i: (i, j))] + side_specs,
        out_shape=[jax.ShapeDtypeStruct((m, M_END), BF16)]
                  + [jax.ShapeDtypeStruct(w.shape, BF16) for w in side_weights],
        scratch_shapes=[pltpu.VMEM((IN_TN, d), BF16)],
        compiler_params=_params(("arbitrary", "arbitrary")),
        name="in_proj",
    )(xn, w_t, w_t, *side_weights)
    return outs[0], outs[1:]


def _side_cast_specs(weights, n_steps, step_of):
    n_cast = 1 << (n_steps.bit_length() - 1)
    specs = []
    for w in weights:
        rows = w.shape[0] // n_cast
        assert w.shape[0] % n_cast == 0 and rows % BF16_ROWS == 0
        specs.append(pl.BlockSpec((rows, w.shape[1]), lambda *ids: (jnp.minimum(step_of(*ids), n_cast - 1), 0)))
    return n_cast, specs


def _side_cast(step, n_cast, srcs, dsts):
    @pl.when(step < n_cast)
    def _():
        for src, dst in zip(srcs, dsts):
            dst[...] = src[...].astype(BF16)


LOG2E = math.log2(math.e)
VT_ROWS = HEAD_DIM + BF16_ROWS


def _qk_prep_kernel(q_ref, k_ref, v_ref, qi_ref, sm_ref, qg_ref, kg_ref,
                    qt_ref, kn_ref, vt_ref, qit_ref, wt_ref, ki_ref):
    qg = qg_ref[...]
    for h in range(ATT_HEADS):
        x = q_ref[:, h * HEAD_DIM:(h + 1) * HEAD_DIM].astype(F32)
        ms = jnp.mean(x * x, axis=-1, keepdims=True)
        y = x * lax.rsqrt(ms + EPS) * qg * (HEAD_DIM ** -0.5 * LOG2E)
        qt_ref[0, h] = y.T.astype(BF16)
    kg = kg_ref[...]
    for h in range(ATT_KV_HEADS):
        x = k_ref[:, h * HEAD_DIM:(h + 1) * HEAD_DIM].astype(F32)
        ms = jnp.mean(x * x, axis=-1, keepdims=True)
        kn_ref[:, h * HEAD_DIM:(h + 1) * HEAD_DIM] = (x * lax.rsqrt(ms + EPS) * kg).astype(BF16)
        vt_ref[0, h, 0, 0:HEAD_DIM, :] = v_ref[:, h * HEAD_DIM:(h + 1) * HEAD_DIM].astype(F32).T.astype(BF16)
        vt_ref[0, h, 0, HEAD_DIM:VT_ROWS, :] = jnp.ones((VT_ROWS - HEAD_DIM, v_ref.shape[0]), BF16)
    for p in range(IDX_Q // LANES):
        qit_ref[0, p * LANES:(p + 1) * LANES, :] = qi_ref[:, p * LANES:(p + 1) * LANES].astype(F32).T.astype(BF16)
    sm_t = sm_ref[:, 0:LANES].T
    wt_ref[0] = sm_t[S_WI:S_WI + IDX_HEADS, :] * (IDX_HEADS ** -0.5 * IDX_DIM ** -0.5)
    ki_ref[...] = sm_ref[:, S_KI:S_KI + IDX_DIM].astype(BF16)


def _qk_prep(main, small, qg, kg, batch, seq, tq):
    t = main.shape[0]
    nq = seq // tq
    return pl.pallas_call(
        _qk_prep_kernel,
        grid=(batch, nq),
        in_specs=[pl.BlockSpec((tq, ATT_Q), lambda b, i: (b * nq + i, M_Q // ATT_Q)),
                  pl.BlockSpec((tq, ATT_KV), lambda b, i: (b * nq + i, M_K // ATT_KV)),
                  pl.BlockSpec((tq, ATT_KV), lambda b, i: (b * nq + i, M_V // ATT_KV)),
                  pl.BlockSpec((tq, IDX_Q), lambda b, i: (b * nq + i, M_QI // IDX_Q)),
                  pl.BlockSpec((tq, S_END), lambda b, i: (b * nq + i, 0)),
                  pl.BlockSpec((1, HEAD_DIM), lambda b, i: (0, 0)),
                  pl.BlockSpec((1, HEAD_DIM), lambda b, i: (0, 0))],
        out_specs=[pl.BlockSpec((1, ATT_HEADS, HEAD_DIM, tq), lambda b, i: (b * nq + i, 0, 0, 0)),
                   pl.BlockSpec((tq, ATT_KV), lambda b, i: (b * nq + i, 0)),
                   pl.BlockSpec((1, ATT_KV_HEADS, 1, VT_ROWS, tq), lambda b, i: (b, 0, i, 0, 0)),
                   pl.BlockSpec((1, IDX_Q, tq), lambda b, i: (b * nq + i, 0, 0)),
                   pl.BlockSpec((1, IDX_HEADS, tq), lambda b, i: (b * nq + i, 0, 0)),
                   pl.BlockSpec((tq, IDX_DIM), lambda b, i: (b * nq + i, 0))],
        out_shape=[jax.ShapeDtypeStruct((t // tq, ATT_HEADS, HEAD_DIM, tq), BF16),
                   jax.ShapeDtypeStruct((t, ATT_KV), BF16),
                   jax.ShapeDtypeStruct((batch, ATT_KV_HEADS, nq, VT_ROWS, tq), BF16),
                   jax.ShapeDtypeStruct((t // tq, IDX_Q, tq), BF16),
                   jax.ShapeDtypeStruct((t // tq, IDX_HEADS, tq), F32),
                   jax.ShapeDtypeStruct((t, IDX_DIM), BF16)],
        compiler_params=_params(("parallel", "parallel")),
        name="qk_prep",
    )(main, main, main, main, small, qg, kg)


DSA_TQ = 256
DSA_TK = 256
CNT_WAYS = 4


def _t5_bucket_np(dist):
    n = np.maximum(dist, 0)
    max_exact = N_BUCKETS // 2
    nf = np.maximum(n, 1).astype(np.float32)
    ratio = (np.log(nf / np.float32(max_exact)) / np.float32(math.log(MAX_DISTANCE / max_exact))
             * np.float32(N_BUCKETS - max_exact))
    large = max_exact + ratio.astype(np.int32)
    large = np.minimum(large, N_BUCKETS - 1)
    return np.where(n < max_exact, n, large).astype(np.int32)


def _bias_bucket_tiles(tq, tk):
    r = np.arange(tq)[None, :]
    c = np.arange(tk)[:, None]
    d0 = _t5_bucket_np(r - c)
    d1 = _t5_bucket_np(tk + r - c)
    assert np.all(_t5_bucket_np(np.arange(tk + 1, 8 * tk)) == N_BUCKETS - 1)
    return np.stack([d0, d1]).astype(np.int32)


def _bias_bucket_rows(tq, tk):
    assert tq == tk
    j = np.arange(2 * tq)
    signed = np.where(j < tq, j, j - 2 * tq)
    rows = np.stack([_t5_bucket_np(signed + t * tk) for t in range(2)]).astype(np.int32)
    tiles = _bias_bucket_tiles(tq, tk)
    r = np.arange(tq)[None, :]
    c = np.arange(tk)[:, None]
    assert all(np.array_equal(tiles[t], rows[t][(r - c) % (2 * tq)]) for t in range(2))
    return np.broadcast_to(rows[:, None, :], (2, SUBLANES, 2 * tq)).copy()


def _dsa_kernel(relb_ref, bidx_ref, qt_ref, qit_ref, wt_ref, k_ref, vt_ref, ki_ref, o_ref,
                key_ref, plane_ref, madd_ref, bias_ref, s_buf, p_buf, st_ref, acc_ref, *, tq, tk, nkc, topk):
    b = pl.program_id(0)
    i = pl.program_id(1)
    neg_slot = nkc

    @pl.when((b == 0) & (i == 0))
    def _init():
        madd_ref[neg_slot] = jnp.full((tk, tq), NEG_INF, F32)
        for t in range(2):
            bt = bidx_ref[t]

            def head_body(h, carry):
                far = relb_ref[N_BUCKETS - 1, h]

                def bucket_body(bk, acc):
                    return jnp.where(bt == bk, (relb_ref[bk, h] - far) * LOG2E, acc)

                gen = lax.fori_loop(0, N_BUCKETS, bucket_body, jnp.zeros((SUBLANES, 2 * tq), F32))
                full = jnp.broadcast_to(gen[0:1, :], (tk, 2 * tq))
                bias_ref[t, h] = pltpu.roll(full, 0, 1, stride=1, stride_axis=0)[:, 0:tq]
                return carry

            lax.fori_loop(0, ATT_HEADS, head_body, 0)

    def _select():
        nj = i + 1
        qpos = i * tq + lax.broadcasted_iota(I32, (tk, tq), 1)

        def score_chunk(j):
            kc = ki_ref[pl.ds(pl.multiple_of(j * tk, tk), tk), :]
            acc = jnp.zeros((tk, tq), F32)
            for h in range(IDX_HEADS):
                z = _dot(kc, qit_ref[0, h * IDX_DIM:(h + 1) * IDX_DIM, :])
                acc = acc + jnp.maximum(z, 0.0) * wt_ref[0, h:h + 1, :]
            kpos = j * tk + lax.broadcasted_iota(I32, (tk, tq), 0)
            acc = jnp.where(kpos <= qpos, acc, NEG_INF)
            bits = pltpu.bitcast(acc, I32)
            bits = jnp.where(bits == INT_MIN, 0, bits)
            key = bits ^ ((bits >> 31) & 0x7FFFFFFF)
            key_ref[j] = key
            plane_ref[0, j] = pltpu.bitcast(bits & jnp.int32(-65536), F32).astype(BF16)
            plane_ref[1, j] = ((key >> 8) & 0xFF).astype(F32).astype(BF16)
            plane_ref[2, j] = (key & 0xFF).astype(F32).astype(BF16)

        def score_quad(u, carry):
            for v in range(4):
                score_chunk(4 * u + v)
            return carry

        lax.fori_loop(0, nj // 4, score_quad, 0)
        done = (nj // 4) * 4

        @pl.when(nj % 4 >= 2)
        def _():
            score_chunk(done)
            score_chunk(done + 1)

        @pl.when(nj % 2 == 1)
        def _():
            score_chunk(nj - 1)

        kf = float(topk)
        one_b = jnp.ones((tk, tq), BF16)
        zero_b = jnp.zeros((tk, tq), BF16)
        grp = BF16_ROWS * CNT_WAYS

        def count(plane, cand_b, strict):
            def body(j, cnt):
                pv = plane_ref[plane, j]
                hit = jnp.where(pv > cand_b if strict else pv >= cand_b, one_b, zero_b)
                for r0 in range(0, tk, grp):
                    cnt = cnt + hit[r0:r0 + grp]
                return cnt

            cnt = lax.fori_loop(0, nj, body, jnp.zeros((grp, tq), BF16))
            return jnp.sum(cnt.astype(F32), axis=0, keepdims=True)

        def keep_ties(src, dst, tie_b):
            def body(j, carry):
                plane_ref[dst, j] = jnp.where(plane_ref[src, j] == tie_b, plane_ref[dst, j], -one_b)
                return carry

            lax.fori_loop(0, nj, body, 0)

        def top_digit_float(p16):
            p16 = jnp.where((p16 > 0) & (p16 < BF16_MIN_NORMAL_BITS), BF16_MIN_NORMAL_BITS, p16)
            fb = (p16 & 0xFFFF) ^ jnp.where(p16 < 0, 0x7FFF, 0)
            return pltpu.bitcast(lax.shift_left(fb, 16), F32).astype(BF16)

        def top_iter(it, prefix):
            cand = prefix + lax.shift_left(jnp.int32(1), 15 - it)
            return jnp.where(count(0, top_digit_float(cand), False) >= kf, cand, prefix)

        d_top = lax.fori_loop(0, 16, top_iter, jnp.full((1, tq), -32768, I32))
        t_top = top_digit_float(d_top)
        above = count(0, t_top, True)

        def byte_digit(plane, above_n):
            def it_body(it, prefix):
                cand = prefix + lax.shift_left(jnp.int32(1), 7 - it)
                tot = above_n + count(plane, cand.astype(F32).astype(BF16), False)
                return jnp.where(tot >= kf, cand, prefix)

            return lax.fori_loop(0, 8, it_body, jnp.zeros((1, tq), I32))

        keep_ties(0, 1, t_top)
        d_mid = byte_digit(1, above)
        t_mid = d_mid.astype(F32).astype(BF16)
        above = above + count(1, t_mid, True)
        keep_ties(1, 2, t_mid)
        d_low = byte_digit(2, above)
        t_low = d_low.astype(F32).astype(BF16)
        thr = lax.shift_left(d_top, 16) | lax.shift_left(d_mid, 8) | d_low
        n_ge = above + count(2, t_low, False)
        tied = jnp.max(n_ge) > kf

        @pl.when(jnp.logical_not(tied))
        def _():
            thr_c = jnp.maximum(thr, KEY_NEG_INF + 1)

            def madd_chunk(j, carry):
                madd_ref[j] = jnp.where(key_ref[j] >= thr_c, 0.0, NEG_INF)
                return carry

            lax.fori_loop(0, nj, madd_chunk, 0)

        @pl.when(tied)
        def _():
            need = kf - (above + count(2, t_low, True))
            row = lax.broadcasted_iota(I32, (tk, tq), 0)

            def pos_iter(it, q_pos):
                cand = q_pos + lax.shift_left(jnp.int32(1), (nkc * tk).bit_length() - 2 - it)

                def body(j, cnt):
                    hit = jnp.where(key_ref[j] == thr, jnp.where(row + j * tk < cand, 1.0, 0.0), 0.0)
                    return cnt + jnp.sum(hit.reshape(tk // SUBLANES, SUBLANES, tq), axis=0)

                cnt = lax.fori_loop(0, nj, body, jnp.zeros((SUBLANES, tq), F32))
                return jnp.where(jnp.sum(cnt, axis=0, keepdims=True) < need, cand, q_pos)

            last = lax.fori_loop(0, (nkc * tk).bit_length() - 1, pos_iter, jnp.zeros((1, tq), I32))
            thr_c = jnp.maximum(thr, KEY_NEG_INF)

            def madd_chunk(j, carry):
                kk = key_ref[j]
                take_tie = jnp.where(row + j * tk <= last, jnp.where(kk > KEY_NEG_INF, 0.0, NEG_INF), NEG_INF)
                madd_ref[j] = jnp.where(kk > thr_c, 0.0, jnp.where(kk == thr, take_tie, NEG_INF))
                return carry

            lax.fori_loop(0, nj, madd_chunk, 0)

    all_selected = (i + 1) * tk <= topk

    @pl.when(all_selected)
    def _():
        def madd_chunk(j, carry):
            madd_ref[j] = jnp.zeros((tk, tq), F32)
            return carry

        lax.fori_loop(0, i, madd_chunk, 0)
        madd_ref[i] = jnp.where(lax.broadcasted_iota(I32, (tk, tq), 0) <= lax.broadcasted_iota(I32, (tk, tq), 1),
                                0.0, NEG_INF)

    pl.when(jnp.logical_not(all_selected))(_select)

    ng = ATT_KV_HEADS
    c0 = jnp.maximum(i - 1, 0)
    c1 = jnp.minimum(c0 + 1, nkc - 1)
    n_far = (c0 + 1) // 2

    def scores_to(slot, g, ca, madd_c, bias_c):
        rows = pl.ds(pl.multiple_of(ca * tk, tk), 2 * tk)
        qt = jnp.concatenate([qt_ref[0, g * REP + r] for r in range(REP)], axis=1)
        add = jnp.concatenate([madd_c if bias_c is None else madd_c + bias_c[r] for r in range(REP)], axis=1)
        s = _dot(k_ref[rows, g * HEAD_DIM:(g + 1) * HEAD_DIM], qt) + add
        s_buf[slot] = s
        st_ref[ng + 2 + slot:ng + 3 + slot, :] = jnp.max(s, axis=0, keepdims=True)

    def softmax_to(slot, g):
        m = st_ref[g:g + 1, :]
        m_new = jnp.maximum(m, st_ref[ng + 2 + slot:ng + 3 + slot, :])
        m_safe = jnp.where(m_new == NEG_INF, 0.0, m_new)
        alpha = jnp.exp2(m - m_safe)
        p_buf[slot] = jnp.exp2((s_buf[slot] - m_safe).astype(BF16))
        st_ref[g:g + 1, :] = m_new
        st_ref[ng + slot:ng + 1 + slot, :] = alpha

    def values_from(slot, g, ca, cb):
        vt_c = jnp.concatenate([vt_ref[0, g, ca], vt_ref[0, g, cb]], axis=1)
        acc_ref[g] = st_ref[ng + slot:ng + 1 + slot, :] * acc_ref[g] + _dot(vt_c, p_buf[slot])

    def near_scores_to(slot, g):
        first = i == 0
        t_a = jnp.where(first, 0, 1)
        idx_b = jnp.where(first, neg_slot, c0 + 1)
        madd_near = jnp.concatenate([madd_ref[c0], madd_ref[idx_b]], axis=0)
        bias_near = [jnp.concatenate([bias_ref[t_a, g * REP + r], bias_ref[0, g * REP + r]], axis=0)
                     for r in range(REP)]
        scores_to(slot, g, c0, madd_near, bias_near)

    def far_scores_to(slot, g, k):
        ca = jnp.minimum(2 * (k - 1), nkc - 2)
        cb = jnp.where(ca + 1 >= c0, neg_slot, ca + 1)
        scores_to(slot, g, ca, jnp.concatenate([madd_ref[ca], madd_ref[cb]], axis=0), None)

    st_ref[0:ng, :] = jnp.full((ng, REP * tq), NEG_INF, F32)
    acc_ref[...] = jnp.zeros_like(acc_ref)

    near_scores_to(0, 0)
    for g in range(ng):
        if g + 1 < ng:
            near_scores_to((g + 1) % 2, g + 1)
        else:
            far_scores_to(0, 0, 1)
        softmax_to(g % 2, g)
        values_from(g % 2, g, c0, c1)

    def far_step(k, carry):
        for g in range(ng):
            if g + 1 < ng:
                far_scores_to((g + 1) % 2, g + 1, k)
            else:
                far_scores_to(0, 0, k + 1)
            softmax_to(g % 2, g)
            values_from(g % 2, g, 2 * (k - 1), 2 * (k - 1) + 1)
        return carry

    lax.fori_loop(1, n_far + 1, far_step, 0)

    for g in range(ng):
        out = acc_ref[g, 0:HEAD_DIM, :] / acc_ref[g, HEAD_DIM:HEAD_DIM + 1, :]
        for r in range(REP):
            h = g * REP + r
            o_ref[:, h * HEAD_DIM:(h + 1) * HEAD_DIM] = out[:, r * tq:(r + 1) * tq].T.astype(o_ref.dtype)


def _dsa(qt, kn, vt, qit, wt, ki, rel_bias, batch, seq):
    tq, tk = DSA_TQ, DSA_TK
    nq = seq // tq
    nkc = seq // tk
    topk = min(TOPK_MAX, seq // 4)
    bidx = jnp.asarray(_bias_bucket_rows(tq, tk))
    kern = functools.partial(_dsa_kernel, tq=tq, tk=tk, nkc=nkc, topk=topk)
    once = pl.Buffered(1)
    return pl.pallas_call(
        kern,
        grid=(batch, nq),
        in_specs=[pl.BlockSpec(memory_space=pltpu.SMEM),
                  pl.BlockSpec((2, SUBLANES, 2 * tq), lambda b, i: (0, 0, 0), pipeline_mode=once),
                  pl.BlockSpec((1, ATT_HEADS, HEAD_DIM, tq), lambda b, i: (b * nq + i, 0, 0, 0)),
                  pl.BlockSpec((1, IDX_Q, tq), lambda b, i: (b * nq + i, 0, 0)),
                  pl.BlockSpec((1, IDX_HEADS, tq), lambda b, i: (b * nq + i, 0, 0)),
                  pl.BlockSpec((seq, ATT_KV), lambda b, i: (b, 0), pipeline_mode=once),
                  pl.BlockSpec((1, ATT_KV_HEADS, nkc, VT_ROWS, tk), lambda b, i: (b, 0, 0, 0, 0),
                               pipeline_mode=once),
                  pl.BlockSpec((seq, IDX_DIM), lambda b, i: (b, 0), pipeline_mode=once)],
        out_specs=pl.BlockSpec((tq, ATT_Q), lambda b, i: (b * nq + i, 0)),
        out_shape=jax.ShapeDtypeStruct((batch * seq, ATT_Q), BF16),
        scratch_shapes=[pltpu.VMEM((nkc, tk, tq), I32),
                        pltpu.VMEM((3, nkc, tk, tq), BF16),
                        pltpu.VMEM((nkc + 1, tk, tq), F32),
                        pltpu.VMEM((2, ATT_HEADS, tk, tq), F32),
                        pltpu.VMEM((2, 2 * tk, REP * tq), F32),
                        pltpu.VMEM((2, 2 * tk, REP * tq), BF16),
                        pltpu.VMEM((ATT_KV_HEADS + 4, REP * tq), F32),
                        pltpu.VMEM((ATT_KV_HEADS, VT_ROWS, REP * tq), F32)],
        compiler_params=_params(("arbitrary", "arbitrary")),
        name="dsa_attention",
    )(rel_bias, bidx, qt, qit, wt, kn, vt, ki)


E_ROWS = 3 * CHUNK + BF16_ROWS
CONV_HALO = BF16_ROWS


def _conv_shift_matrix():
    ext = CONV_HALO + CHUNK
    s = np.zeros((CHUNK, (CONV_WIDTH - 1) * ext), np.float32)
    for k in range(CONV_WIDTH - 1):
        t = np.arange(CHUNK)
        s[t, k * ext + CONV_HALO + t - (CONV_WIDTH - 1) + k] = 1.0
    return s


def _ssd_kernel(xbc0_ref, xbcn_ref, halon_ref, z_ref, sm_ref, shift_ref, cw_ref, cb_ref, dtb_ref, alog_ref,
                dsk_ref, ng_ref, o_ref, xs_ref, bm_ref, cm_ref, state_ref, ypre_ref, acg_ref, actg_ref, *, batch):
    c = pl.program_id(0)

    cblk = GROUP_W

    def conv_block(slot, bb, cbi, x_ref, halo_ref):
        cols = slice(cbi * cblk, (cbi + 1) * cblk)
        halo = jnp.zeros((CONV_HALO, cblk), BF16) if halo_ref is None else halo_ref[bb, :, cols]
        ext = jnp.concatenate([halo, x_ref[bb, :, cols]], axis=0)
        wtap = (0.5 * cw_ref[:, cols]).astype(BF16)
        prods = jnp.concatenate([ext * wtap[kk:kk + 1, :] for kk in range(CONV_WIDTH - 1)], axis=0)
        last = (ext[CONV_HALO:, :] * wtap[CONV_WIDTH - 1:CONV_WIDTH, :]).astype(F32)
        y = _silu_of_half(0.5 * cb_ref[:, cols] + last + _dot(shift_ref[...], prods))
        if cbi < SSM_GROUPS:
            xs_ref[bb, slot, cbi] = y
        else:
            per = cblk // SSM_STATE
            for u in range(per):
                gi = (cbi - SSM_GROUPS) * per + u
                piece = y[:, u * SSM_STATE:(u + 1) * SSM_STATE]
                if gi < SSM_GROUPS:
                    bm_ref[bb, slot, gi] = piece
                else:
                    cm_ref[bb, slot, gi - SSM_GROUPS] = piece

    def conv_jobs(slot, x_ref, halo_ref):
        return [functools.partial(conv_block, slot, bb, cbi, x_ref, halo_ref)
                for cbi in range(CONV_DIM // cblk) for bb in range(batch)]

    def scan_chunk(slot, jobs):
        _ssd_scan_chunk(slot, batch, jobs, z_ref, sm_ref, dtb_ref, alog_ref, dsk_ref, ng_ref, o_ref,
                        xs_ref, bm_ref, cm_ref, state_ref, ypre_ref, acg_ref, actg_ref)

    @pl.when(c == 0)
    def _():
        state_ref[...] = jnp.zeros_like(state_ref)
        for job in conv_jobs(0, xbc0_ref, None):
            job()

    @pl.when(c % 2 == 0)
    def _():
        scan_chunk(0, conv_jobs(1, xbcn_ref, halon_ref))

    @pl.when(c % 2 == 1)
    def _():
        scan_chunk(1, conv_jobs(0, xbcn_ref, halon_ref))


def _ssd_scan_prep(sm_ref, dtb_ref, alog_ref, dsk_ref, acg_ref, actg_ref, tril):
    L = CHUNK
    dt_in = sm_ref[:, S_DT:S_DT + SSM_HEADS] + dtb_ref[...]
    dt_act = jnp.maximum(dt_in, 0.0) + jnp.log1p(jnp.exp(-jnp.abs(dt_in)))
    a = dt_act * (-jnp.exp(alog_ref[...]))
    tri_b = jnp.where(tril, 1.0, 0.0).astype(BF16)
    a_cum = sum(_dot(tri_b, p) for p in _split3(a))
    eye_b = jnp.where(lax.broadcasted_iota(I32, (SSM_HEADS, SSM_HEADS), 0)
                      == lax.broadcasted_iota(I32, (SSM_HEADS, SSM_HEADS), 1), 1.0, 0.0).astype(BF16)
    a_cum_t = sum(_dot_nt(eye_b, p) for p in _split3(a_cum))
    a_last = a_cum[L - 1:L, :]
    for gi in range(SSM_GROUPS):
        acg_ref[gi] = a_cum[:, gi * HEADS_PER_GROUP:(gi + 1) * HEADS_PER_GROUP]
        actg_ref[gi] = a_cum_t[gi * HEADS_PER_GROUP:(gi + 1) * HEADS_PER_GROUP, :]
    cd3 = _split3(jnp.exp(a_last))
    ds3 = _split3(dsk_ref[...])
    extras = jnp.concatenate([p.astype(F32) for p in cd3 + ds3]
                             + [jnp.zeros((E_ROWS - 3 * L - 6, SSM_HEADS), F32)], axis=0)
    return jnp.concatenate([dt_act, jnp.exp(a_cum), jnp.exp(a_last - a_cum), extras], axis=0).astype(BF16)


def _ssd_scan_chunk(slot, batch, jobs, z_ref, sm_ref, dtb_ref, alog_ref, dsk_ref, ng_ref, o_ref,
                    xs_ref, bm_ref, cm_ref, state_ref, ypre_ref, acg_ref, actg_ref):
    L = CHUNK
    ri = lax.broadcasted_iota(I32, (L, L), 0)
    ci = lax.broadcasted_iota(I32, (L, L), 1)
    tril = ri >= ci
    e_mats = [_ssd_scan_prep(sm_ref.at[bb], dtb_ref, alog_ref, dsk_ref, acg_ref.at[bb], actg_ref.at[bb], tril)
              for bb in range(batch)]

    lane = lax.broadcasted_iota(I32, (L, LANES), 1)
    lo_mask = lane < SSM_HEAD_DIM

    def front(bb, gi):
        e_mat = e_mats[bb]
        xs = xs_ref[bb, slot, gi]
        bg = bm_ref[bb, slot, gi]
        cg_b = cm_ref[bb, slot, gi].astype(BF16)
        hsel = (lax.broadcasted_iota(I32, (SSM_HEADS, GROUP_W), 0)
                == gi * HEADS_PER_GROUP + lax.broadcasted_iota(I32, (SSM_HEADS, GROUP_W), 1) // SSM_HEAD_DIM)
        ex = _dot(e_mat, jnp.where(hsel, 1.0, 0.0).astype(BF16))
        dt_rep = ex[0:L]
        expa_rep = ex[L:2 * L]
        dte_rep = ex[2 * L:3 * L]
        cd_rep = ex[3 * L:3 * L + 1] + ex[3 * L + 1:3 * L + 2] + ex[3 * L + 2:3 * L + 3]
        dsk_rep = ex[3 * L + 3:3 * L + 4] + ex[3 * L + 4:3 * L + 5] + ex[3 * L + 5:3 * L + 6]

        xd = xs * dt_rep
        xd_b = xd.astype(BF16)
        cb = _dot_nt(cg_b, bg.astype(BF16))
        st = state_ref[bb, gi]
        y_off = _dot(cg_b, st.astype(BF16)) * expa_rep
        xdd = (xd * dte_rep).astype(BF16)
        state_ref[bb, gi] = st * cd_rep + _dot(bg.T.astype(BF16), xdd)
        ypre_ref[bb, gi] = y_off + dsk_rep * xs
        return cb, xd_b

    def back(bb, gi, cb, xd_b):
        acg = acg_ref[bb, gi]
        actg = actg_ref[bb, gi]
        for pj in range(HEADS_PER_GROUP // 2):
            gmat = []
            for e in (2 * pj, 2 * pj + 1):
                seg = acg[:, e:e + 1] - actg[e:e + 1, :]
                dec = jnp.exp(jnp.where(tril, seg, NEG_INF))
                gmat.append((cb * dec).astype(BF16))
            xp = xd_b[:, pj * LANES:(pj + 1) * LANES]
            zero = jnp.zeros_like(xp)
            ypre_ref[bb, gi, :, pj * LANES:(pj + 1) * LANES] += _dot(
                jnp.concatenate(gmat, axis=1),
                jnp.concatenate([jnp.where(lo_mask, xp, zero), jnp.where(lo_mask, zero, xp)], axis=0))

    stages = [(bb, gi) for gi in range(SSM_GROUPS) for bb in range(batch)]
    ahead = front(*stages[0])
    for idx, stage in enumerate(stages):
        cur = ahead
        if idx + 1 < len(stages):
            ahead = front(*stages[idx + 1])
        back(*stage, *cur)
        for job in jobs[idx * len(jobs) // len(stages):(idx + 1) * len(jobs) // len(stages)]:
            job()

    for gi in range(SSM_GROUPS):
        cols = slice(gi * GROUP_W, (gi + 1) * GROUP_W)
        for bb in range(batch):
            zz = z_ref[bb, :, cols].astype(F32)
            y = ypre_ref[bb, gi] * _silu(zz)
            ms = jnp.mean(y * y, axis=-1, keepdims=True)
            o_ref[bb, :, cols] = (y * lax.rsqrt(ms + EPS) * ng_ref[:, cols]).astype(o_ref.dtype)


def _ssd(main, small, conv_w, conv_b, dt_bias, a_log, d_skip, norm_g, batch, seq):
    nc = seq // CHUNK
    hb = CHUNK // CONV_HALO
    full = lambda shape: pl.BlockSpec(shape, lambda c: (0,) * len(shape))
    shift = jnp.asarray(_conv_shift_matrix(), BF16)
    main3 = main.reshape(batch, seq, M_END)
    small3 = small.reshape(batch, seq, S_END)
    y = pl.pallas_call(
        functools.partial(_ssd_kernel, batch=batch),
        grid=(nc,),
        in_specs=[pl.BlockSpec((batch, CHUNK, CONV_DIM), lambda c: (0, 0, M_XBC // CONV_DIM),
                               pipeline_mode=pl.Buffered(1)),
                  pl.BlockSpec((batch, CHUNK, CONV_DIM),
                               lambda c: (0, jnp.minimum(c + 1, nc - 1), M_XBC // CONV_DIM)),
                  pl.BlockSpec((batch, CONV_HALO, CONV_DIM),
                               lambda c: (0, c * hb + hb - 1, M_XBC // CONV_DIM)),
                  pl.BlockSpec((batch, CHUNK, SSM_INNER), lambda c: (0, c, M_Z // SSM_INNER)),
                  pl.BlockSpec((batch, CHUNK, S_END), lambda c: (0, c, 0)),
                  full((CHUNK, (CONV_WIDTH - 1) * (CONV_HALO + CHUNK))),
                  full((CONV_WIDTH, CONV_DIM)), full((1, CONV_DIM)), full((1, SSM_HEADS)),
                  full((1, SSM_HEADS)), full((1, SSM_HEADS)), full((1, SSM_INNER))],
        out_specs=pl.BlockSpec((batch, CHUNK, SSM_INNER), lambda c: (0, c, 0)),
        out_shape=jax.ShapeDtypeStruct((batch, seq, SSM_INNER), BF16),
        scratch_shapes=[pltpu.VMEM((batch, 2, SSM_GROUPS, CHUNK, GROUP_W), F32),
                        pltpu.VMEM((batch, 2, SSM_GROUPS, CHUNK, SSM_STATE), F32),
                        pltpu.VMEM((batch, 2, SSM_GROUPS, CHUNK, SSM_STATE), F32),
                        pltpu.VMEM((batch, SSM_GROUPS, SSM_STATE, GROUP_W), F32),
                        pltpu.VMEM((batch, SSM_GROUPS, CHUNK, GROUP_W), F32),
                        pltpu.VMEM((batch, SSM_GROUPS, CHUNK, HEADS_PER_GROUP), F32),
                        pltpu.VMEM((batch, SSM_GROUPS, HEADS_PER_GROUP, CHUNK), F32)],
        compiler_params=_params(("arbitrary",)),
        name="ssd_scan",
    )(main3, main3, main3, main3, small3, shift, conv_w, conv_b, dt_bias, a_log, d_skip, norm_g)
    return y.reshape(batch * seq, SSM_INNER)


def _merge_kernel(att_ref, y_ref, ga_ref, gs_ref, wa_ref, ws_ref, o_ref):
    pa = _dot(att_ref[...], wa_ref[...])
    ps = _dot(y_ref[...], ws_ref[...])
    o_ref[...] = (_sigmoid(ga_ref[...].astype(F32)) * pa + _sigmoid(gs_ref[...].astype(F32)) * ps).astype(o_ref.dtype)


def _merge(att, y, main, wa, ws, tm, tn):
    t = att.shape[0]
    return pl.pallas_call(
        _merge_kernel,
        grid=(t // tm, D_MODEL // tn),
        in_specs=[pl.BlockSpec((tm, ATT_Q), lambda i, j: (i, 0)),
                  pl.BlockSpec((tm, SSM_INNER), lambda i, j: (i, 0)),
                  pl.BlockSpec((tm, tn), lambda i, j: (i, M_GA // tn + j)),
                  pl.BlockSpec((tm, tn), lambda i, j: (i, M_GS // tn + j)),
                  pl.BlockSpec((ATT_Q, tn), lambda i, j: (0, j)),
                  pl.BlockSpec((SSM_INNER, tn), lambda i, j: (0, j))],
        out_specs=pl.BlockSpec((tm, tn), lambda i, j: (i, j)),
        out_shape=jax.ShapeDtypeStruct((t, D_MODEL), BF16),
        compiler_params=_params(("parallel", "arbitrary")),
        name="gated_merge",
    )(att, y, main, main, wa, ws)


def _out_proj_kernel(m_ref, w_ref, x_ref, g_ref, x1_ref, h2_ref):
    x1 = x_ref[...] + _dot(m_ref[...], w_ref[...])
    x1_ref[...] = x1
    ms = jnp.mean(x1 * x1, axis=-1, keepdims=True)
    h2_ref[...] = (x1 * lax.rsqrt(ms + EPS) * g_ref[...]).astype(BF16)


def _out_proj(merged, w, x, g, tm):
    t = x.shape[0]
    return pl.pallas_call(
        _out_proj_kernel,
        grid=(t // tm,),
        in_specs=[pl.BlockSpec((tm, D_MODEL), lambda i: (i, 0)),
                  pl.BlockSpec((D_MODEL, D_MODEL), lambda i: (0, 0)),
                  pl.BlockSpec((tm, D_MODEL), lambda i: (i, 0)),
                  pl.BlockSpec((1, D_MODEL), lambda i: (0, 0))],
        out_specs=[pl.BlockSpec((tm, D_MODEL), lambda i: (i, 0)),
                   pl.BlockSpec((tm, D_MODEL), lambda i: (i, 0))],
        out_shape=[jax.ShapeDtypeStruct((t, D_MODEL), F32),
                   jax.ShapeDtypeStruct((t, D_MODEL), BF16)],
        compiler_params=_params(("parallel",)),
        name="out_proj_norm",
    )(merged, w, x, g)


def _mlp_kernel(h_ref, x1_ref, wu_ref, wd_ref, o_ref):
    @pl.when(pl.program_id(1) == 0)
    def _():
        o_ref[...] = x1_ref[...]

    u = _dot(h_ref[...], wu_ref[...])
    u = jnp.square(jnp.maximum(u, 0.0)).astype(BF16)
    o_ref[...] += _dot(u, wd_ref[...])


def _mlp(h2, x1, wu, wd, tm, th):
    t = h2.shape[0]
    return pl.pallas_call(
        _mlp_kernel,
        grid=(t // tm, MLP_HIDDEN // th),
        in_specs=[pl.BlockSpec((tm, D_MODEL), lambda i, j: (i, 0)),
                  pl.BlockSpec((tm, D_MODEL), lambda i, j: (i, 0)),
                  pl.BlockSpec((D_MODEL, th), lambda i, j: (0, j)),
                  pl.BlockSpec((th, D_MODEL), lambda i, j: (j, 0))],
        out_specs=pl.BlockSpec((tm, D_MODEL), lambda i, j: (i, 0)),
        out_shape=jax.ShapeDtypeStruct((t, D_MODEL), F32),
        compiler_params=_params(("parallel", "arbitrary")),
        name="relu2_mlp",
    )(h2, x1, wu, wd)


def _pack_w_small(w_t):
    assert (O_GA, O_KI, O_Z, O_DT) == (M_GA, M_Z, O_WI + IDX_HEADS, O_Z + M_END - M_Z)
    return jnp.concatenate([w_t[O_KI:O_Z], w_t[O_DT:O_DT + SSM_HEADS],
                            jnp.zeros((S_END - S_DT - SSM_HEADS, w_t.shape[1]), w_t.dtype)], axis=0)


def _block(x2, batch, seq, norm1_g, w_in, conv_w, conv_b, dt_bias, a_log, d_skip, ssm_norm_g, q_norm_g,
           k_norm_g, rel_bias, w_att_branch, w_ssm_branch, w_out, norm2_g, w_up, w_down):
    row = lambda v: v.reshape(1, -1)
    w_t = w_in.T
    xn, small = _norm_small(x2, row(norm1_g), _pack_w_small(w_t), tm=min(1024, x2.shape[0]))
    main, (w_att_b, w_ssm_b, w_out_b, w_up_b, w_down_b) = _in_proj(
        xn, w_t, (w_att_branch, w_ssm_branch, w_out, w_up, w_down), tm=min(1024, x2.shape[0]))
    qt, kn, vt, qit, wt, ki = _qk_prep(main, small, row(q_norm_g), row(k_norm_g), batch, seq, tq=DSA_TQ)
    att = _dsa(qt, kn, vt, qit, wt, ki, rel_bias, batch, seq)
    y = _ssd(main, small, conv_w, row(conv_b), row(dt_bias), row(a_log), row(d_skip), row(ssm_norm_g),
             batch, seq)
    merged = _merge(att, y, main, w_att_b, w_ssm_b, tm=512, tn=1024)
    x1, h2 = _out_proj(merged, w_out_b, x2, row(norm2_g), tm=512)
    return _mlp(h2, x1, w_up_b, w_down_b, tm=512, th=1024)


def kernel(x, norm1_g, w_in, conv_w, conv_b, dt_bias, a_log, d_skip, ssm_norm_g, q_norm_g, k_norm_g, rel_bias,
           w_att_branch, w_ssm_branch, w_out, norm2_g, w_up, w_down):
    batch, seq, d = x.shape
    x2 = x.reshape(batch * seq, d)
    for l in range(norm1_g.shape[0]):
        x2 = _block(x2, batch, seq, norm1_g[l], w_in[l], conv_w[l], conv_b[l], dt_bias[l], a_log[l], d_skip[l],
                    ssm_norm_g[l], q_norm_g[l], k_norm_g[l], rel_bias, w_att_branch[l], w_ssm_branch[l],
                    w_out[l], norm2_g[l], w_up[l], w_down[l])
    return x2.reshape(batch, seq, d)
```
